```python
import jax
import jax.numpy as jnp
from jax import lax
import numpy as np

D_MODEL = 1024
BATCH = 8
SEQ = 2048
DEPTH = 2

GRID_W = 64
CTX_LEN = 256
N_EVEN = (DEPTH + 1) // 2
N_ODD = DEPTH // 2

A_WIDTH = 512
A_HEADS = 8
A_HEAD_DIM = A_WIDTH // A_HEADS
A_CONV_W = 4
LRU_C = 8.0
B_WIDTH = 512
B_CONV_W = 31
AB_IN = 2 * A_WIDTH + 2 * B_WIDTH
AB_OUT = A_WIDTH + B_WIDTH

C_WIDTH = 512
C_HEADS = 8
C_HEAD_DIM = C_WIDTH // C_HEADS
CHUNK = 128
D_WIDTH = 512
D_GROUPS = 4
D_GROUP_DIM = D_WIDTH // D_GROUPS
CD_IN = 2 * C_WIDTH + D_WIDTH
CD_OUT = C_WIDTH + D_WIDTH

D_FF = 2816
N_EXPERTS = 8
TOP_K = 2
D_FF_EXPERT = 3584

NORM_EPS = 1e-6
LN_EPS = 1e-5
POS_THETA = 10000.0

kernel_name = "hybrid_rglru_conformer_gmlp_fnet_moe_prefix_dit"


def rmsnorm(x, g):
    xf = x.astype(jnp.float32)
    xf = xf * lax.rsqrt(jnp.mean(xf * xf, axis=-1, keepdims=True) + NORM_EPS)
    return (xf * g.astype(jnp.float32)).astype(x.dtype)


def layernorm(x, g, b):
    xf = x.astype(jnp.float32)
    mu = jnp.mean(xf, axis=-1, keepdims=True)
    var = jnp.mean(jnp.square(xf - mu), axis=-1, keepdims=True)
    y = (xf - mu) * lax.rsqrt(var + LN_EPS) * g.astype(jnp.float32) + b.astype(jnp.float32)
    return y.astype(x.dtype)


def dwconv(x, w, b):
    y = lax.conv_general_dilated(x, w[:, None, :], window_strides=(1,), padding='SAME',
                                 dimension_numbers=('NWC', 'WIO', 'NWC'),
                                 feature_group_count=x.shape[-1])
    return y + b


def grid_pos_embed(rows):
    row = jnp.repeat(jnp.arange(rows, dtype=jnp.float32), GRID_W)
    col = jnp.tile(jnp.arange(GRID_W, dtype=jnp.float32), rows)
    n_freq = D_MODEL // 4
    omega = POS_THETA ** (-jnp.arange(n_freq, dtype=jnp.float32) / n_freq)
    ang_r = row[:, None] * omega
    ang_c = col[:, None] * omega
    return jnp.concatenate([jnp.sin(ang_r), jnp.cos(ang_r), jnp.sin(ang_c), jnp.cos(ang_c)], axis=-1)


def ada_terms(cvec, w_ada, b_ada):
    m = jax.nn.silu(cvec) @ w_ada + b_ada
    if m.ndim == 2:
        m = m[:, None, :]
    return jnp.split(m, 6, axis=-1)


def lru_coeffs(x, w_r, b_r, w_i, b_i, lam):
    bsz, length, _ = x.shape
    xh = x.reshape(bsz, length, A_HEADS, A_HEAD_DIM)
    r = jax.nn.sigmoid(jnp.einsum('blhd,hde->blhe', xh, w_r).reshape(bsz, length, A_WIDTH).astype(jnp.float32)
                       + b_r.astype(jnp.float32))
    i = jax.nn.sigmoid(jnp.einsum('blhd,hde->blhe', xh, w_i).reshape(bsz, length, A_WIDTH).astype(jnp.float32)
                       + b_i.astype(jnp.float32))
    log_a = -LRU_C * r * jax.nn.softplus(-lam.astype(jnp.float32))
    b = jnp.sqrt(-jnp.expm1(2.0 * log_a)) * (i * x.astype(jnp.float32))
    return jnp.exp(log_a), b


def linear_scan(a, b, h0):
    def combine(left, right):
        return left[0] * right[0], right[0] * left[1] + right[1]
    a_cum, b_cum = lax.associative_scan(combine, (a, b), axis=1)
    return a_cum * h0[:, None, :] + b_cum


def rglru_direction(x_ctx, x_lat, w_r, b_r, w_i, b_i, lam, reverse):
    a_c, b_c = lru_coeffs(x_ctx, w_r, b_r, w_i, b_i, lam)
    a_l, b_l = lru_coeffs(x_lat, w_r, b_r, w_i, b_i, lam)
    if reverse:
        a_c, b_c, a_l, b_l = (jnp.flip(t, axis=1) for t in (a_c, b_c, a_l, b_l))
    h_c = linear_scan(a_c, b_c, jnp.zeros_like(b_c[:, 0]))
    h_l = linear_scan(a_l, b_l, h_c[:, -1])
    if reverse:
        h_c, h_l = jnp.flip(h_c, axis=1), jnp.flip(h_l, axis=1)
    return h_c, h_l


def mixer_ab(h_lat, h_ctx, w_in, conv_w, conv_b, w_r, b_r, w_i, b_i, lam, dw_w, dw_b, ln_g, ln_b, w_out):
    def branches(h):
        z = h @ w_in
        xa, ga, vb, gb = jnp.split(z, [A_WIDTH, 2 * A_WIDTH, 2 * A_WIDTH + B_WIDTH], axis=-1)
        xa = dwconv(xa, conv_w, conv_b)
        u = dwconv(vb * jax.nn.sigmoid(gb), dw_w, dw_b)
        u = jax.nn.silu(layernorm(u, ln_g, ln_b))
        return xa, jax.nn.gelu(ga), u
    xa_c, ga_c, u_c = branches(h_ctx)
    xa_l, ga_l, u_l = branches(h_lat)
    fwd_c, fwd_l = rglru_direction(xa_c, xa_l, w_r[0], b_r[0], w_i[0], b_i[0], lam[0], reverse=False)
    bwd_c, bwd_l = rglru_direction(xa_c, xa_l, w_r[1], b_r[1], w_i[1], b_i[1], lam[1], reverse=True)
    rec_c = (fwd_c + bwd_c).astype(ga_c.dtype) * ga_c
    rec_l = (fwd_l + bwd_l).astype(ga_l.dtype) * ga_l
    y_c = jnp.concatenate([rec_c, u_c], axis=-1) @ w_out
    y_l = jnp.concatenate([rec_l, u_l], axis=-1) @ w_out
    return y_l, y_c


def spatial_gating(u, v, ln_g, ln_b, w_s, b_s):
    bsz, length, _ = v.shape
    v = layernorm(v, ln_g, ln_b)
    vh = v.reshape(bsz, length // CHUNK, CHUNK, C_HEADS, C_HEAD_DIM)
    mixed = jnp.einsum('hpq,bnqhd->bnphd', w_s, vh) + b_s.T[:, :, None]
    return u * mixed.reshape(bsz, length, C_WIDTH)


def fourier_mix(z):
    bsz, length, _ = z.shape
    zg = z.astype(jnp.float32).reshape(bsz, length, D_GROUPS, D_GROUP_DIM).transpose(0, 2, 1, 3)
    f = jnp.fft.fft2(zg, norm='ortho').real
    return f.transpose(0, 2, 1, 3).reshape(bsz, length, D_WIDTH).astype(z.dtype)


def mixer_cd(h, w_in, ln_g, ln_b, w_s, b_s, w_out):
    z = h @ w_in
    uv, f = z[..., :2 * C_WIDTH], z[..., 2 * C_WIDTH:]
    u, v = jnp.split(jax.nn.gelu(uv), 2, axis=-1)
    return jnp.concatenate([spatial_gating(u, v, ln_g, ln_b, w_s, b_s), fourier_mix(f)], axis=-1) @ w_out


def swiglu(x, w1, w3, w2):
    return (jax.nn.silu(x @ w1) * (x @ w3)) @ w2


def moe_swiglu(x, router, w1, w3, w2):
    logits = (x @ router).astype(jnp.float32)
    top_v, top_i = lax.top_k(logits, TOP_K)
    probs = jax.nn.softmax(top_v, axis=-1)
    gates = jnp.sum(jax.nn.one_hot(top_i, N_EXPERTS, dtype=jnp.float32) * probs[..., None], axis=-2)
    gates = gates.astype(x.dtype)
    out = jnp.zeros_like(x)
    for e in range(N_EXPERTS):
        out = out + gates[..., e:e + 1] * swiglu(x, w1[e], w3[e], w2[e])
    return out


def setup_inputs(seed: int = 0) -> dict:
    key = jax.random.key(seed)
    ks = iter(jax.random.split(key, 48))
    D = D_MODEL

    def nrm(shape, scale):
        return scale * jax.random.normal(next(ks), shape, jnp.float32)

    u = jax.random.uniform(next(ks), (N_EVEN, 2, A_WIDTH), jnp.float32, 0.9, 0.999)
    a0 = u ** (1.0 / LRU_C)
    rg_lambda = jnp.log(a0) - jnp.log1p(-a0)
    return {
        'x': nrm((BATCH, SEQ, D), 1.0),
        'c': nrm((BATCH, D), 1.0),
        'ctx': nrm((BATCH, CTX_LEN, D), 1.0),
        'c_ctx': nrm((D,), 1.0),
        'ada_w': nrm((DEPTH, D, 6 * D), 0.5 * D ** -0.5),
        'ada_b': nrm((DEPTH, 6 * D), 0.02),
        'norm_g': 1.0 + nrm((DEPTH, 4, D), 0.1),
        'ab_w_in': nrm((N_EVEN, D, AB_IN), D ** -0.5),
        'rg_conv_w': nrm((N_EVEN, A_CONV_W, A_WIDTH), A_CONV_W ** -0.5),
        'rg_conv_b': nrm((N_EVEN, A_WIDTH), 0.02),
        'rg_w_r': nrm((N_EVEN, 2, A_HEADS, A_HEAD_DIM, A_HEAD_DIM), A_HEAD_DIM ** -0.5),
        'rg_b_r': nrm((N_EVEN, 2, A_WIDTH), 0.02),
        'rg_w_i': nrm((N_EVEN, 2, A_HEADS, A_HEAD_DIM, A_HEAD_DIM), A_HEAD_DIM ** -0.5),
        'rg_b_i': nrm((N_EVEN, 2, A_WIDTH), 0.02),
        'rg_lambda': rg_lambda,
        'cv_dw_w': nrm((N_EVEN, B_CONV_W, B_WIDTH), B_CONV_W ** -0.5),
        'cv_dw_b': nrm((N_EVEN, B_WIDTH), 0.02),
        'cv_ln_g': 1.0 + nrm((N_EVEN, B_WIDTH), 0.1),
        'cv_ln_b': nrm((N_EVEN, B_WIDTH), 0.02),
        'ab_w_out': nrm((N_EVEN, AB_OUT, D), AB_OUT ** -0.5),
        'ffn_w1': nrm((N_EVEN, D, D_FF), D ** -0.5),
        'ffn_w3': nrm((N_EVEN, D, D_FF), D ** -0.5),
        'ffn_w2': nrm((N_EVEN, D_FF, D), D_FF ** -0.5),
        'cd_w_in': nrm((N_ODD, D, CD_IN), D ** -0.5),
        'sg_ln_g': 1.0 + nrm((N_ODD, C_WIDTH), 0.1),
        'sg_ln_b': nrm((N_ODD, C_WIDTH), 0.02),
        'sg_w_s': nrm((N_ODD, C_HEADS, CHUNK, CHUNK), CHUNK ** -0.5),
        'sg_b_s': 1.0 + nrm((N_ODD, C_HEADS, CHUNK), 0.1),
        'cd_w_out': nrm((N_ODD, CD_OUT, D), CD_OUT ** -0.5),
        'moe_router': nrm((N_ODD, D, N_EXPERTS), D ** -0.5),
        'moe_w1': nrm((N_ODD, N_EXPERTS, D, D_FF_EXPERT), D ** -0.5),
        'moe_w3': nrm((N_ODD, N_EXPERTS, D, D_FF_EXPERT), D ** -0.5),
        'moe_w2': nrm((N_ODD, N_EXPERTS, D_FF_EXPERT, D), D_FF_EXPERT ** -0.5),
    }


def reference(x, c, ctx, c_ctx, ada_w, ada_b, norm_g,
              ab_w_in, rg_conv_w, rg_conv_b, rg_w_r, rg_b_r, rg_w_i, rg_b_i, rg_lambda,
              cv_dw_w, cv_dw_b, cv_ln_g, cv_ln_b, ab_w_out,
              ffn_w1, ffn_w3, ffn_w2,
              cd_w_in, sg_ln_g, sg_ln_b, sg_w_s, sg_b_s, cd_w_out,
              moe_router, moe_w1, moe_w3, moe_w2):
    ROWS = x.shape[1] // GRID_W
    x = x + grid_pos_embed(ROWS).astype(x.dtype)
    xc = ctx
    for layer in range(DEPTH):
        j = layer // 2
        keep_ctx = layer < DEPTH - 1
        sh1, sc1, g1, sh2, sc2, g2 = ada_terms(c, ada_w[layer], ada_b[layer])
        sh1c, sc1c, g1c, sh2c, sc2c, g2c = ada_terms(c_ctx, ada_w[layer], ada_b[layer])
        g_pre_mix, g_post_mix, g_pre_ffn, g_post_ffn = norm_g[layer]

        h = rmsnorm(x, g_pre_mix) * (1.0 + sc1) + sh1
        hc = rmsnorm(xc, g_pre_mix) * (1.0 + sc1c) + sh1c
        if layer % 2 == 0:
            y, yc = mixer_ab(h, hc, ab_w_in[j], rg_conv_w[j], rg_conv_b[j], rg_w_r[j], rg_b_r[j],
                             rg_w_i[j], rg_b_i[j], rg_lambda[j], cv_dw_w[j], cv_dw_b[j],
                             cv_ln_g[j], cv_ln_b[j], ab_w_out[j])
        else:
            y = mixer_cd(h, cd_w_in[j], sg_ln_g[j], sg_ln_b[j], sg_w_s[j], sg_b_s[j], cd_w_out[j])
            if keep_ctx:
                yc = mixer_cd(hc, cd_w_in[j], sg_ln_g[j], sg_ln_b[j], sg_w_s[j], sg_b_s[j], cd_w_out[j])
        x = x + g1 * rmsnorm(y, g_post_mix)
        if keep_ctx:
            xc = xc + g1c * rmsnorm(yc, g_post_mix)

        if layer % 2 == 0:
            ffn = lambda t: swiglu(t, ffn_w1[j], ffn_w3[j], ffn_w2[j])
        else:
            ffn = lambda t: moe_swiglu(t, moe_router[j], moe_w1[j], moe_w3[j], moe_w2[j])
        x = x + g2 * rmsnorm(ffn(rmsnorm(x, g_pre_ffn) * (1.0 + sc2) + sh2), g_post_ffn)
        if keep_ctx:
            xc = xc + g2c * rmsnorm(ffn(rmsnorm(xc, g_pre_ffn) * (1.0 + sc2c) + sh2c), g_post_ffn)
    return x
```

```python
import functools

import numpy as np
import jax
import jax.numpy as jnp
from jax import lax
from jax.experimental import pallas as pl
from jax.experimental.pallas import tpu as pltpu

F32 = jnp.float32
BF16 = jnp.bfloat16

GRID_W = 64
POS_THETA = 10000.0
NORM_EPS = 1e-6
LN_EPS = 1e-5
LRU_C = 8.0

A_WIDTH = 512
A_HEADS = 8
A_CONV_W = 4
B_WIDTH = 512
B_CONV_W = 31
C_WIDTH = 512
C_HEADS = 8
CHUNK = 128
D_WIDTH = 512
D_GROUPS = 4
D_GROUP_DIM = D_WIDTH // D_GROUPS
N_EXPERTS = 8

LANES = 128
SCAN_SEGMENTS = 16
ROW_TILE = 512
EXPERT_ROW_TILE = 512
VMEM_LIMIT = 56 * 2 ** 20


def _params(sem):
    return pltpu.CompilerParams(dimension_semantics=sem, vmem_limit_bytes=VMEM_LIMIT)


def _rms(x, g):
    return x * lax.rsqrt(jnp.mean(x * x, axis=-1, keepdims=True) + NORM_EPS) * g


def _bdot(a, b):
    return jnp.dot(a, b, preferred_element_type=F32)


def _ada_kernel(c_ref, w_ref, b_ref, o_ref):
    s = jax.nn.silu(c_ref[...])
    o_ref[0] = _bdot(s.astype(BF16), w_ref[0].astype(BF16)) + b_ref[0]


def _ada(cpad, ada_w, ada_b):
    nl, d, n6 = ada_w.shape
    rows = cpad.shape[0]
    tn = n6 // 4
    return pl.pallas_call(
        _ada_kernel,
        grid=(nl, n6 // tn),
        in_specs=[
            pl.BlockSpec((rows, d), lambda l, j: (0, 0)),
            pl.BlockSpec((1, d, tn), lambda l, j: (l, 0, j)),
            pl.BlockSpec((1, 1, tn), lambda l, j: (l, 0, j)),
        ],
        out_specs=pl.BlockSpec((1, rows, tn), lambda l, j: (l, 0, j)),
        out_shape=jax.ShapeDtypeStruct((nl, rows, n6), F32),
        compiler_params=_params(("arbitrary", "arbitrary")),
        name="ada_terms",
    )(cpad, ada_w, ada_b.reshape(nl, 1, n6))


def _inproj_kernel(*refs, mode, add_pos):
    if add_pos:
        x_ref, pos_ref, mod_ref, g_ref, w_ref, *outs = refs
        x = x_ref[0] + pos_ref[...]
    else:
        x_ref, mod_ref, g_ref, w_ref, *outs = refs
        x = x_ref[0]
    h = _rms(x, g_ref[0:1, :]) * (1.0 + mod_ref[0, 1:2, :]) + mod_ref[0, 0:1, :]
    z = _bdot(h.astype(BF16), w_ref[...])
    if mode == "ab":
        outs[0][0] = z[:, 0:A_WIDTH]
        outs[1][0] = jax.nn.gelu(z[:, A_WIDTH:2 * A_WIDTH])
        vb = z[:, 2 * A_WIDTH:2 * A_WIDTH + B_WIDTH]
        gb = z[:, 2 * A_WIDTH + B_WIDTH:]
        outs[2][0] = vb * jax.nn.sigmoid(gb)
    elif mode == "a":
        outs[0][0] = z
    else:
        outs[0][0] = jax.nn.gelu(z[:, 0:C_WIDTH])
        outs[1][0] = jax.nn.gelu(z[:, C_WIDTH:2 * C_WIDTH])
        outs[2][0] = z[:, 2 * C_WIDTH:]


def _inproj(x, pos, mods, g4, w, mode, per_batch_mod=True):
    bsz, length, d = x.shape
    n = w.shape[1]
    tm = min(ROW_TILE, length)
    n_out = 1 if mode == "a" else 3
    width = n if mode == "a" else 512
    add_pos = pos is not None
    mod_map = (lambda b, i: (b, 0, 0)) if per_batch_mod else (lambda b, i: (0, 0, 0))
    in_specs = [pl.BlockSpec((1, tm, d), lambda b, i: (b, i, 0))]
    args = [x]
    if add_pos:
        in_specs.append(pl.BlockSpec((tm, d), lambda b, i: (i, 0)))
        args.append(pos)
    in_specs += [
        pl.BlockSpec((1, 6, d), mod_map),
        pl.BlockSpec((4, d), lambda b, i: (0, 0)),
        pl.BlockSpec((d, n), lambda b, i: (0, 0)),
    ]
    args += [mods, g4, w]
    outs = pl.pallas_call(
        functools.partial(_inproj_kernel, mode=mode, add_pos=add_pos),
        grid=(bsz, length // tm),
        in_specs=in_specs,
        out_specs=[pl.BlockSpec((1, tm, width), lambda b, i: (b, i, 0))] * n_out,
        out_shape=[jax.ShapeDtypeStruct((bsz, length, width), F32)] * n_out,
        compiler_params=_params(("parallel", "parallel")),
        name="inproj_" + mode,
    )(*args)
    return outs


def _scan_kernel(xa_ref, xc_ref, gg_ref, cw_ref, cb_ref, wg_ref, bg_ref, sp_ref, rec_ref,
                 pad_ref, a_ref, b_ref, carry_ref, *, length, ctx_len):
    nseg = SCAN_SEGMENTS
    rows_per_step = 256

    def coefficients(n):
        def body(c, _):
            base = pl.multiple_of(c * rows_per_step, rows_per_step)
            window = pad_ref[pl.ds(base, rows_per_step + 16), :]
            xconv = cb_ref[...] + cw_ref[0:1, :] * window[7:7 + rows_per_step, :]
            for k in range(1, A_CONV_W):
                xconv = xconv + cw_ref[k:k + 1, :] * window[7 + k:7 + k + rows_per_step, :]
            g = _bdot(xconv.astype(BF16), wg_ref[0]) + bg_ref[0]
            for d in range(2):
                r = jax.nn.sigmoid(g[:, 2 * LANES * d:2 * LANES * d + LANES])
                i = jax.nn.sigmoid(g[:, 2 * LANES * d + LANES:2 * LANES * (d + 1)])
                log_a = (-LRU_C * r) * sp_ref[0, d:d + 1, :]
                a = jnp.exp(log_a)
                one_minus_a2 = -jnp.tanh(log_a) * (a * a + 1.0)
                a_ref[d, pl.ds(base, rows_per_step), :] = a
                b_ref[d, pl.ds(base, rows_per_step), :] = jnp.sqrt(one_minus_a2) * (i * xconv)
            return 0
        lax.fori_loop(0, n // rows_per_step, body, 0)

    def local_scans(n, store):
        seg = n // nseg

        def body(t, carry):
            hf, pf, hb, pb = carry
            tb = seg - 1 - t
            af = a_ref[0, pl.ds(t, nseg, stride=seg), :]
            bf = b_ref[0, pl.ds(t, nseg, stride=seg), :]
            ab = a_ref[1, pl.ds(tb, nseg, stride=seg), :]
            bb = b_ref[1, pl.ds(tb, nseg, stride=seg), :]
            hf = af * hf + bf
            pf = af * pf
            hb = ab * hb + bb
            pb = ab * pb
            if store:
                b_ref[0, pl.ds(t, nseg, stride=seg), :] = hf
                a_ref[0, pl.ds(t, nseg, stride=seg), :] = pf
                b_ref[1, pl.ds(tb, nseg, stride=seg), :] = hb
                a_ref[1, pl.ds(tb, nseg, stride=seg), :] = pb
            return hf, pf, hb, pb

        zeros = jnp.zeros((nseg, LANES), F32)
        ones = jnp.ones((nseg, LANES), F32)
        return lax.fori_loop(0, seg, body, (zeros, ones, zeros, ones))

    def chain_carries(ends, init_f, init_b, store):
        hf, pf, hb, pb = ends
        cf = init_f
        for s in range(nseg):
            if store:
                carry_ref[0, s:s + 1, :] = cf
            cf = pf[s:s + 1, :] * cf + hf[s:s + 1, :]
        cb = init_b
        for s in range(nseg - 1, -1, -1):
            if store:
                carry_ref[1, s:s + 1, :] = cb
            cb = pb[s:s + 1, :] * cb + hb[s:s + 1, :]
        return cf, cb

    zrow = jnp.zeros((8, LANES), F32)

    pad_ref[0:8, :] = zrow
    pad_ref[8:8 + ctx_len, :] = xc_ref[0]
    pad_ref[8 + ctx_len:16 + ctx_len, :] = zrow
    coefficients(ctx_len)
    zero_state = jnp.zeros((1, LANES), F32)
    init_f, init_b = chain_carries(local_scans(ctx_len, False), zero_state, zero_state, False)

    pad_ref[8:8 + length, :] = xa_ref[0]
    pad_ref[8 + length:16 + length, :] = zrow
    coefficients(length)
    chain_carries(local_scans(length, True), init_f, init_b, True)

    seg = length // nseg

    def fix(s, _):
        rows = pl.ds(pl.multiple_of(s * seg, seg), seg)
        hf = b_ref[0, rows, :] + a_ref[0, rows, :] * carry_ref[0, pl.ds(s, 1), :]
        hb = b_ref[1, rows, :] + a_ref[1, rows, :] * carry_ref[1, pl.ds(s, 1), :]
        rec_ref[0, rows, :] = ((hf + hb) * gg_ref[0, rows, :]).astype(BF16)
        return 0
    lax.fori_loop(0, nseg, fix, 0)


def _scan(xa, xc, gg, conv_w, conv_b, wg, bg, sp):
    bsz, length, width = xa.shape
    ctx_len = xc.shape[1]
    ngroups = width // LANES
    return pl.pallas_call(
        functools.partial(_scan_kernel, length=length, ctx_len=ctx_len),
        grid=(bsz, ngroups),
        in_specs=[
            pl.BlockSpec((1, length, LANES), lambda b, c: (b, 0, c)),
            pl.BlockSpec((1, ctx_len, LANES), lambda b, c: (b, 0, c)),
            pl.BlockSpec((1, length, LANES), lambda b, c: (b, 0, c)),
            pl.BlockSpec((A_CONV_W, LANES), lambda b, c: (0, c)),
            pl.BlockSpec((1, LANES), lambda b, c: (0, c)),
            pl.BlockSpec((1, LANES, 4 * LANES), lambda b, c: (c, 0, 0)),
            pl.BlockSpec((1, 1, 4 * LANES), lambda b, c: (c, 0, 0)),
            pl.BlockSpec((1, 2, LANES), lambda b, c: (c, 0, 0)),
        ],
        out_specs=pl.BlockSpec((1, length, LANES), lambda b, c: (b, 0, c)),
        out_shape=jax.ShapeDtypeStruct((bsz, length, width), BF16),
        scratch_shapes=[
            pltpu.VMEM((length + 16, LANES), F32),
            pltpu.VMEM((2, length, LANES), F32),
            pltpu.VMEM((2, length, LANES), F32),
            pltpu.VMEM((2, SCAN_SEGMENTS, LANES), F32),
        ],
        compiler_params=_params(("parallel", "parallel")),
        name="rglru_scan",
    )(xa, xc, gg, conv_w, conv_b.reshape(1, width), wg, bg, sp)


def _gate_weights(w_r, b_r, w_i, b_i, lam):
    heads_per_group = LANES // (A_WIDTH // A_HEADS)
    ngroups = A_WIDTH // LANES
    hd = A_WIDTH // A_HEADS

    def blockdiag(w):
        w = w.reshape(ngroups, heads_per_group, hd, hd)
        eye = jnp.eye(heads_per_group, dtype=w.dtype)
        return jnp.einsum("ghde,hk->ghdke", w, eye).reshape(ngroups, LANES, LANES)

    wg = jnp.concatenate([blockdiag(w_r[0]), blockdiag(w_i[0]), blockdiag(w_r[1]), blockdiag(w_i[1])], axis=-1)
    bg = jnp.stack([b_r[0], b_i[0], b_r[1], b_i[1]], axis=0)
    bg = bg.reshape(4, ngroups, LANES).transpose(1, 0, 2).reshape(ngroups, 1, 4 * LANES)
    sp = jax.nn.softplus(-lam.astype(F32)).reshape(2, ngroups, LANES).transpose(1, 0, 2)
    return wg.astype(BF16), bg, sp


def _conf_out_kernel(glu_ref, rec_ref, x_ref, pos_ref, dww_ref, dwb_ref, lng_ref, lnb_ref, wout_ref, g_ref, mod_ref,
                     o_ref, pad_ref, u_ref, *, length, tm):
    j = pl.program_id(1)
    halo = 16
    sub = 32

    @pl.when(j == 0)
    def _():
        zeros = jnp.zeros((halo, B_WIDTH), F32)
        pad_ref[0:halo, :] = zeros
        pad_ref[halo:halo + length, :] = glu_ref[0]
        pad_ref[halo + length:2 * halo + length, :] = zeros

    base = j * tm

    def body(r, _):
        rb = pl.multiple_of(base + r * sub, sub)
        window = pad_ref[pl.ds(rb, sub + 2 * halo), :]
        acc = jnp.broadcast_to(dwb_ref[...], (sub, B_WIDTH))
        for phase in range(8):
            shifted = window[phase:phase + sub + 24, :]
            for q in range(4):
                k = 8 * q + phase - 1
                if 0 <= k < B_CONV_W:
                    acc = acc + dww_ref[k:k + 1, :] * shifted[8 * q:8 * q + sub, :]
        mu = jnp.mean(acc, axis=-1, keepdims=True)
        cen = acc - mu
        var = jnp.mean(cen * cen, axis=-1, keepdims=True)
        y = cen * lax.rsqrt(var + LN_EPS) * lng_ref[...] + lnb_ref[...]
        u_ref[pl.ds(pl.multiple_of(r * sub, sub), sub), :] = jax.nn.silu(y).astype(BF16)
        return 0
    lax.fori_loop(0, tm // sub, body, 0)

    y = _bdot(rec_ref[0], wout_ref[0:A_WIDTH, :]) + _bdot(u_ref[...], wout_ref[A_WIDTH:, :])
    o_ref[0] = x_ref[0] + pos_ref[...] + mod_ref[0, 2:3, :] * _rms(y, g_ref[1:2, :])


def _conf_out(glu, rec, x, pos, dw_w, dw_b, ln_g, ln_b, w_out, g4, mods):
    bsz, length, d = x.shape
    tm = min(ROW_TILE, length)
    taps = dw_w.shape[0]
    dw_w = jnp.concatenate([dw_w, jnp.zeros((32 - taps, B_WIDTH), F32)], axis=0)
    return pl.pallas_call(
        functools.partial(_conf_out_kernel, length=length, tm=tm),
        grid=(bsz, length // tm),
        in_specs=[
            pl.BlockSpec((1, length, B_WIDTH), lambda b, i: (b, 0, 0)),
            pl.BlockSpec((1, tm, A_WIDTH), lambda b, i: (b, i, 0)),
            pl.BlockSpec((1, tm, d), lambda b, i: (b, i, 0)),
            pl.BlockSpec((tm, d), lambda b, i: (i, 0)),
            pl.BlockSpec((32, B_WIDTH), lambda b, i: (0, 0)),
            pl.BlockSpec((1, B_WIDTH), lambda b, i: (0, 0)),
            pl.BlockSpec((1, B_WIDTH), lambda b, i: (0, 0)),
            pl.BlockSpec((1, B_WIDTH), lambda b, i: (0, 0)),
            pl.BlockSpec((A_WIDTH + B_WIDTH, d), lambda b, i: (0, 0)),
            pl.BlockSpec((4, d), lambda b, i: (0, 0)),
            pl.BlockSpec((1, 6, d), lambda b, i: (b, 0, 0)),
        ],
        out_specs=pl.BlockSpec((1, tm, d), lambda b, i: (b, i, 0)),
        out_shape=jax.ShapeDtypeStruct((bsz, length, d), F32),
        scratch_shapes=[
            pltpu.VMEM((length + 32, B_WIDTH), F32),
            pltpu.VMEM((tm, B_WIDTH), BF16),
        ],
        compiler_params=_params(("parallel", "arbitrary")),
        name="conformer_outproj",
    )(glu, rec, x, pos, dw_w, dw_b.reshape(1, -1), ln_g.reshape(1, -1), ln_b.reshape(1, -1), w_out, g4, mods)


def _ffn_kernel(x_ref, mod_ref, g_ref, w1_ref, w3_ref, w2_ref, o_ref):
    x = x_ref[0]
    h = (_rms(x, g_ref[2:3, :]) * (1.0 + mod_ref[0, 4:5, :]) + mod_ref[0, 3:4, :]).astype(BF16)
    a = _bdot(h, w1_ref[...])
    b = _bdot(h, w3_ref[...])
    t = (jax.nn.silu(a) * b).astype(BF16)
    y = _bdot(t, w2_ref[...])
    o_ref[0] = x + mod_ref[0, 5:6, :] * _rms(y, g_ref[3:4, :])


def _ffn(x, mods, g4, w1, w3, w2):
    bsz, length, d = x.shape
    dff = w1.shape[1]
    tm = min(ROW_TILE, length)
    const = lambda b, i: (0, 0)
    return pl.pallas_call(
        _ffn_kernel,
        grid=(bsz, length // tm),
        in_specs=[
            pl.BlockSpec((1, tm, d), lambda b, i: (b, i, 0)),
            pl.BlockSpec((1, 6, d), lambda b, i: (b, 0, 0)),
            pl.BlockSpec((4, d), const),
            pl.BlockSpec((d, dff), const, pipeline_mode=pl.Buffered(1)),
            pl.BlockSpec((d, dff), const, pipeline_mode=pl.Buffered(1)),
            pl.BlockSpec((dff, d), const, pipeline_mode=pl.Buffered(1)),
        ],
        out_specs=pl.BlockSpec((1, tm, d), lambda b, i: (b, i, 0)),
        out_shape=jax.ShapeDtypeStruct((bsz, length, d), F32),
        compiler_params=_params(("parallel", "parallel")),
        name="dense_swiglu",
    )(x, mods, g4, w1, w3, w2)


def _fourier_kernel(f_ref, dmat_ref, lmat_ref, o_ref, z_ref, *, length, scale):
    j = pl.program_id(1)
    rows_per_step = 512 if length % 512 == 0 else length

    @pl.when(j == 0)
    def _():
        def body(c, _):
            rows = pl.ds(pl.multiple_of(c * rows_per_step, rows_per_step), rows_per_step)
            fb = f_ref[0, rows, :].astype(BF16)
            for g in range(D_GROUPS):
                cs = _bdot(fb[:, g * D_GROUP_DIM:(g + 1) * D_GROUP_DIM], dmat_ref[...])
                z_ref[0, rows, g * D_GROUP_DIM:(g + 1) * D_GROUP_DIM] = cs[:, 0:D_GROUP_DIM].astype(BF16)
                z_ref[1, rows, g * D_GROUP_DIM:(g + 1) * D_GROUP_DIM] = cs[:, D_GROUP_DIM:].astype(BF16)
            return 0
        lax.fori_loop(0, length // rows_per_step, body, 0)

    out = _bdot(lmat_ref[0], z_ref[0]) + _bdot(lmat_ref[1], z_ref[1])
    o_ref[0] = (out * scale).astype(BF16)


def _dft_matrices(length):
    k = np.arange(length, dtype=np.int64)
    ang_l = 2.0 * np.pi * ((k[:, None] * k[None, :]) % length).astype(np.float64) / length
    lmat = np.stack([np.cos(ang_l), -np.sin(ang_l)], axis=0)
    d = np.arange(D_GROUP_DIM, dtype=np.int64)
    ang_d = 2.0 * np.pi * ((d[:, None] * d[None, :]) % D_GROUP_DIM).astype(np.float64) / D_GROUP_DIM
    dmat = np.concatenate([np.cos(ang_d), np.sin(ang_d)], axis=1)
    return jnp.asarray(lmat, dtype=F32).astype(BF16), jnp.asarray(dmat, dtype=F32).astype(BF16)


def _fourier(f):
    bsz, length, width = f.shape
    tm = min(ROW_TILE, length)
    lmat, dmat = _dft_matrices(length)
    scale = float(1.0 / np.sqrt(length * D_GROUP_DIM))
    return pl.pallas_call(
        functools.partial(_fourier_kernel, length=length, scale=scale),
        grid=(bsz, length // tm),
        in_specs=[
            pl.BlockSpec((1, length, width), lambda b, i: (b, 0, 0)),
            pl.BlockSpec((D_GROUP_DIM, 2 * D_GROUP_DIM), lambda b, i: (0, 0)),
            pl.BlockSpec((2, tm, length), lambda b, i: (0, i, 0)),
        ],
        out_specs=pl.BlockSpec((1, tm, width), lambda b, i: (b, i, 0)),
        out_shape=jax.ShapeDtypeStruct((bsz, length, width), BF16),
        scratch_shapes=[pltpu.VMEM((2, length, width), BF16)],
        compiler_params=_params(("parallel", "arbitrary")),
        name="fourier_mix",
    )(f, dmat, lmat)


def _gate_out_kernel(u_ref, v_ref, fo_ref, x_ref, lng_ref, lnb_ref, wcat_ref, bs_ref, wout_ref, g_ref, mod_ref,
                     o_ref, vs_ref, gt_ref, *, tm):
    head_dim = C_WIDTH // C_HEADS
    head_of_lane = lax.broadcasted_iota(jnp.int32, (CHUNK, C_WIDTH), 1) // head_dim
    for ci in range(tm // CHUNK):
        rows = slice(ci * CHUNK, (ci + 1) * CHUNK)
        v = v_ref[0, rows, :]
        mu = jnp.mean(v, axis=-1, keepdims=True)
        cen = v - mu
        var = jnp.mean(cen * cen, axis=-1, keepdims=True)
        vn = (cen * lax.rsqrt(var + LN_EPS) * lng_ref[...] + lnb_ref[...]).astype(BF16)
        for h in range(C_HEADS):
            vs_ref[h * CHUNK:(h + 1) * CHUNK, :] = jnp.where(head_of_lane == h, vn, jnp.zeros_like(vn))
        mixed = _bdot(wcat_ref[...], vs_ref[...]) + bs_ref[...]
        gt_ref[rows, :] = (u_ref[0, rows, :] * mixed).astype(BF16)
    y = _bdot(gt_ref[...], wout_ref[0:C_WIDTH, :]) + _bdot(fo_ref[0], wout_ref[C_WIDTH:, :])
    o_ref[0] = x_ref[0] + mod_ref[0, 2:3, :] * _rms(y, g_ref[1:2, :])


def _gate_out(u, v, fo, x, ln_g, ln_b, w_s, b_s, w_out, g4, mods):
    bsz, length, d = x.shape
    tm = min(ROW_TILE, length)
    head_dim = C_WIDTH // C_HEADS
    wcat = w_s.transpose(1, 0, 2).reshape(CHUNK, C_HEADS * CHUNK).astype(BF16)
    bs_full = jnp.repeat(b_s.T, head_dim, axis=1)
    const = lambda b, i: (0, 0)
    return pl.pallas_call(
        functools.partial(_gate_out_kernel, tm=tm),
        grid=(bsz, length // tm),
        in_specs=[
            pl.BlockSpec((1, tm, C_WIDTH), lambda b, i: (b, i, 0)),
            pl.BlockSpec((1, tm, C_WIDTH), lambda b, i: (b, i, 0)),
            pl.BlockSpec((1, tm, D_WIDTH), lambda b, i: (b, i, 0)),
            pl.BlockSpec((1, tm, d), lambda b, i: (b, i, 0)),
            pl.BlockSpec((1, C_WIDTH), const),
            pl.BlockSpec((1, C_WIDTH), const),
            pl.BlockSpec((CHUNK, C_HEADS * CHUNK), const),
            pl.BlockSpec((CHUNK, C_WIDTH), const),
            pl.BlockSpec((C_WIDTH + D_WIDTH, d), const),
            pl.BlockSpec((4, d), const),
            pl.BlockSpec((1, 6, d), lambda b, i: (b, 0, 0)),
        ],
        out_specs=pl.BlockSpec((1, tm, d), lambda b, i: (b, i, 0)),
        out_shape=jax.ShapeDtypeStruct((bsz, length, d), F32),
        scratch_shapes=[
            pltpu.VMEM((C_HEADS * CHUNK, C_WIDTH), BF16),
            pltpu.VMEM((tm, C_WIDTH), BF16),
        ],
        compiler_params=_params(("parallel", "parallel")),
        name="gating_outproj",
    )(u, v, fo, x, ln_g.reshape(1, -1), ln_b.reshape(1, -1), wcat, bs_full, w_out, g4, mods)


def _router_kernel(x_ref, mod_ref, g_ref, wr_ref, tri_ref, xn_ref, meta_ref, cnt_ref, carry_ref, *, tm):
    first = jnp.logical_and(pl.program_id(0) == 0, pl.program_id(1) == 0)

    @pl.when(first)
    def _():
        carry_ref[...] = jnp.zeros_like(carry_ref)

    x = x_ref[0]
    h = _rms(x, g_ref[2:3, :]) * (1.0 + mod_ref[0, 4:5, :]) + mod_ref[0, 3:4, :]
    xn_ref[0] = h.astype(BF16)
    logits = jnp.dot(h, wr_ref[...], preferred_element_type=F32, precision=lax.Precision.HIGHEST)
    lane = lax.broadcasted_iota(jnp.int32, (tm, LANES), 1)
    neg = jnp.float32(-jnp.inf)
    lm = jnp.where(lane < N_EXPERTS, logits, neg)
    m1 = jnp.max(lm, axis=-1, keepdims=True)
    i1 = jnp.min(jnp.where(lm == m1, lane, LANES), axis=-1, keepdims=True)
    lm2 = jnp.where(lane == i1, neg, lm)
    m2 = jnp.max(lm2, axis=-1, keepdims=True)
    i2 = jnp.min(jnp.where(lm2 == m2, lane, LANES), axis=-1, keepdims=True)
    e = jnp.exp(m2 - m1)
    p1 = 1.0 / (1.0 + e)
    p2 = e / (1.0 + e)
    hot1 = (lane == i1).astype(F32)
    hot2 = (lane == i2).astype(F32)
    both = hot1 + hot2
    before = _bdot(tri_ref[...], both.astype(BF16)) + carry_ref[0:1, :]
    r1 = jnp.sum(before * hot1, axis=-1, keepdims=True)
    r2 = jnp.sum(before * hot2, axis=-1, keepdims=True)
    carry_ref[0:1, :] = carry_ref[0:1, :] + jnp.sum(both, axis=0, keepdims=True)
    cnt_ref[...] = carry_ref[...]
    vals = (i1.astype(F32), i2.astype(F32), p1, p2, r1, r2)
    meta = jnp.zeros((tm, LANES), F32)
    for k, val in enumerate(vals):
        meta = jnp.where(lane == k, val, meta)
    meta_ref[0] = meta


def _router(x, mods, g4, w_router):
    bsz, length, d = x.shape
    tm = min(ROW_TILE, length)
    wr = jnp.concatenate([w_router, jnp.zeros((d, LANES - N_EXPERTS), F32)], axis=1)
    tri = jnp.asarray(np.tril(np.ones((tm, tm), np.float32), -1), dtype=BF16)
    const = lambda b, i: (0, 0)
    return pl.pallas_call(
        functools.partial(_router_kernel, tm=tm),
        grid=(bsz, length // tm),
        in_specs=[
            pl.BlockSpec((1, tm, d), lambda b, i: (b, i, 0)),
            pl.BlockSpec((1, 6, d), lambda b, i: (b, 0, 0)),
            pl.BlockSpec((4, d), const),
            pl.BlockSpec((d, LANES), const),
            pl.BlockSpec((tm, tm), const),
        ],
        out_specs=[
            pl.BlockSpec((1, tm, d), lambda b, i: (b, i, 0)),
            pl.BlockSpec((1, tm, LANES), lambda b, i: (b, i, 0)),
            pl.BlockSpec((8, LANES), const),
        ],
        out_shape=[
            jax.ShapeDtypeStruct((bsz, length, d), BF16),
            jax.ShapeDtypeStruct((bsz, length, LANES), F32),
            jax.ShapeDtypeStruct((8, LANES), F32),
        ],
        scratch_shapes=[pltpu.VMEM((8, LANES), F32)],
        compiler_params=_params(("arbitrary", "arbitrary")),
        name="moe_router",
    )(x, mods, g4, wr, tri)


def _expert_kernel(te_ref, nv_ref, x_ref, w1_ref, w3_ref, w2_ref, o_ref, acc_ref):
    i = pl.program_id(0)
    j = pl.program_id(1)

    @pl.when(i < nv_ref[0])
    def _():
        x = x_ref[...]
        a = _bdot(x, w1_ref[0])
        b = _bdot(x, w3_ref[0])
        t = (jax.nn.silu(a) * b).astype(BF16)
        y = _bdot(t, w2_ref[0])

        @pl.when(j == 0)
        def _():
            acc_ref[...] = y

        @pl.when(j > 0)
        def _():
            acc_ref[...] += y

        @pl.when(j == pl.num_programs(1) - 1)
        def _():
            o_ref[...] = acc_ref[...]


def _experts(x_sorted, tile_expert, n_valid, w1, w3, w2):
    rows, d = x_sorted.shape
    dff = w1.shape[2]
    tm = EXPERT_ROW_TILE
    n_ff = 2
    tf = dff // n_ff

    def row_map(i, j, te, nv):
        return (jnp.minimum(i, nv[0] - 1), 0)

    def w13_map(i, j, te, nv):
        return (te[jnp.minimum(i, nv[0] - 1)], 0, jnp.where(i < nv[0], j, n_ff - 1))

    def w2_map(i, j, te, nv):
        return (te[jnp.minimum(i, nv[0] - 1)], jnp.where(i < nv[0], j, n_ff - 1), 0)

    return pl.pallas_call(
        _expert_kernel,
        grid_spec=pltpu.PrefetchScalarGridSpec(
            num_scalar_prefetch=2,
            grid=(rows // tm, n_ff),
            in_specs=[
                pl.BlockSpec((tm, d), row_map),
                pl.BlockSpec((1, d, tf), w13_map),
                pl.BlockSpec((1, d, tf), w13_map),
                pl.BlockSpec((1, tf, d), w2_map),
            ],
            out_specs=pl.BlockSpec((tm, d), row_map),
            scratch_shapes=[pltpu.VMEM((tm, d), F32)],
        ),
        out_shape=jax.ShapeDtypeStruct((rows, d), F32),
        compiler_params=_params(("arbitrary", "arbitrary")),
        name="expert_swiglu",
    )(tile_expert, n_valid, x_sorted, w1, w3, w2)


def _combine_kernel(x_ref, y_ref, meta_ref, g_ref, mod_ref, o_ref):
    meta = meta_ref[0]
    y = meta[:, 2:3] * y_ref[0, 0] + meta[:, 3:4] * y_ref[1, 0]
    o_ref[0] = x_ref[0] + mod_ref[0, 5:6, :] * _rms(y, g_ref[3:4, :])


def _combine(x, y2, meta, g4, mods):
    bsz, length, d = x.shape
    tm = min(ROW_TILE, length)
    return pl.pallas_call(
        _combine_kernel,
        grid=(bsz, length // tm),
        in_specs=[
            pl.BlockSpec((1, tm, d), lambda b, i: (b, i, 0)),
            pl.BlockSpec((2, 1, tm, d), lambda b, i: (0, b, i, 0)),
            pl.BlockSpec((1, tm, LANES), lambda b, i: (b, i, 0)),
            pl.BlockSpec((4, d), lambda b, i: (0, 0)),
            pl.BlockSpec((1, 6, d), lambda b, i: (b, 0, 0)),
        ],
        out_specs=pl.BlockSpec((1, tm, d), lambda b, i: (b, i, 0)),
        out_shape=jax.ShapeDtypeStruct((bsz, length, d), F32),
        compiler_params=_params(("parallel", "parallel")),
        name="moe_combine",
    )(x, y2, meta, g4, mods)


def _moe(x, mods, g4, w_router, w1, w3, w2):
    bsz, length, d = x.shape
    tokens = bsz * length
    tm = EXPERT_ROW_TILE
    xn, meta, counts = _router(x, mods, g4, w_router)
    meta2 = meta.reshape(tokens, LANES)
    experts = meta2[:, 0:2].astype(jnp.int32)
    ranks = meta2[:, 4:6].astype(jnp.int32)
    cnt = counts[0, 0:N_EXPERTS].astype(jnp.int32)
    padded = ((cnt + tm - 1) // tm) * tm
    ends = jnp.cumsum(padded)
    starts = ends - padded
    pos = (starts[experts] + ranks).T.reshape(2 * tokens)
    rows = 2 * tokens + N_EXPERTS * tm
    n_tiles = rows // tm
    n_valid = (ends[-1] // tm).astype(jnp.int32).reshape(1)
    tile_start = jnp.arange(n_tiles, dtype=jnp.int32) * tm
    tile_expert = jnp.minimum(
        jnp.sum((tile_start[:, None] >= ends[None, :]).astype(jnp.int32), axis=1), N_EXPERTS - 1)
    token_ids = jnp.tile(jnp.arange(tokens, dtype=jnp.int32), 2)
    source = jnp.zeros((rows,), jnp.int32).at[pos].set(token_ids)
    x_sorted = jnp.take(xn.reshape(tokens, d), source, axis=0)
    y_sorted = _experts(x_sorted, tile_expert, n_valid, w1, w3, w2)
    y2 = jnp.take(y_sorted, pos, axis=0).reshape(2, bsz, length, d)
    return _combine(x, y2, meta, g4, mods)


def _grid_pos_embed(rows, d):
    row = jnp.repeat(jnp.arange(rows, dtype=F32), GRID_W)
    col = jnp.tile(jnp.arange(GRID_W, dtype=F32), rows)
    n_freq = d // 4
    omega = POS_THETA ** (-jnp.arange(n_freq, dtype=F32) / n_freq)
    ang_r = row[:, None] * omega
    ang_c = col[:, None] * omega
    return jnp.concatenate([jnp.sin(ang_r), jnp.cos(ang_r), jnp.sin(ang_c), jnp.cos(ang_c)], axis=-1)


def kernel(x, c, ctx, c_ctx, ada_w, ada_b, norm_g, ab_w_in, rg_conv_w, rg_conv_b, rg_w_r, rg_b_r, rg_w_i, rg_b_i, rg_lambda, cv_dw_w, cv_dw_b, cv_ln_g, cv_ln_b, ab_w_out, ffn_w1, ffn_w3, ffn_w2, cd_w_in, sg_ln_g, sg_ln_b, sg_w_s, sg_b_s, cd_w_out, moe_router, moe_w1, moe_w3, moe_w2):
    bsz, length, d = x.shape
    assert ada_w.shape[0] == 2, "two layers: one even (RG-LRU | Conformer), one odd (gMLP | Fourier)"
    pos = _grid_pos_embed(length // GRID_W, d)

    cpad = jnp.concatenate([c, c_ctx[None, :], jnp.zeros((16 - bsz - 1, d), F32)], axis=0)
    m = _ada(cpad, ada_w, ada_b)
    mods = m[:, :bsz].reshape(2, bsz, 6, d)
    mods_ctx = m[:, bsz:bsz + 1].reshape(2, 1, 6, d)

    w_in = ab_w_in[0].astype(BF16)
    xa, gg, glu = _inproj(x, pos, mods[0], norm_g[0], w_in, "ab")
    (xc,) = _inproj(ctx, None, mods_ctx[0], norm_g[0], w_in[:, 0:A_WIDTH], "a", per_batch_mod=False)
    wg, bg, sp = _gate_weights(rg_w_r[0], rg_b_r[0], rg_w_i[0], rg_b_i[0], rg_lambda[0])
    rec = _scan(xa, xc, gg, rg_conv_w[0], rg_conv_b[0], wg, bg, sp)
    x1 = _conf_out(glu, rec, x, pos, cv_dw_w[0], cv_dw_b[0], cv_ln_g[0], cv_ln_b[0], ab_w_out[0].astype(BF16),
                   norm_g[0], mods[0])
    x2 = _ffn(x1, mods[0], norm_g[0], ffn_w1[0].astype(BF16), ffn_w3[0].astype(BF16), ffn_w2[0].astype(BF16))

    u, v, f = _inproj(x2, None, mods[1], norm_g[1], cd_w_in[0].astype(BF16), "cd")
    fo = _fourier(f)
    x3 = _gate_out(u, v, fo, x2, sg_ln_g[0], sg_ln_b[0], sg_w_s[0], sg_b_s[0], cd_w_out[0].astype(BF16),
                   norm_g[1], mods[1])
    return _moe(x3, mods[1], norm_g[1], moe_router[0], moe_w1[0].astype(BF16), moe_w3[0].astype(BF16),
                moe_w2[0].astype(BF16))
```

```python
import functools

import numpy as np
import jax
import jax.numpy as jnp
from jax import lax
from jax.experimental import pallas as pl
from jax.experimental.pallas import tpu as pltpu
from jax.experimental.pallas import tpu_sc as plsc

F32 = jnp.float32
BF16 = jnp.bfloat16

GRID_W = 64
POS_THETA = 10000.0
NORM_EPS = 1e-6
LN_EPS = 1e-5
LRU_C = 8.0

A_WIDTH = 512
A_HEADS = 8
A_CONV_W = 4
B_WIDTH = 512
B_CONV_W = 31
C_WIDTH = 512
C_HEADS = 8
CHUNK = 128
D_WIDTH = 512
D_GROUPS = 4
D_GROUP_DIM = D_WIDTH // D_GROUPS
N_EXPERTS = 8

LANES = 128
SCAN_SEGMENTS = 16
ROW_TILE = 512
EXPERT_ROW_TILE = 512
VMEM_LIMIT = 56 * 2 ** 20


def _params(sem):
    return pltpu.CompilerParams(dimension_semantics=sem, vmem_limit_bytes=VMEM_LIMIT)


def _rms(x, g):
    return x * lax.rsqrt(jnp.mean(x * x, axis=-1, keepdims=True) + NORM_EPS) * g


def _bdot(a, b):
    return jnp.dot(a, b, preferred_element_type=F32)


def _ada_kernel(c_ref, w_ref, b_ref, o_ref):
    s = jax.nn.silu(c_ref[...])
    o_ref[0] = _bdot(s.astype(BF16), w_ref[0].astype(BF16)) + b_ref[0]


def _ada(cpad, ada_w, ada_b):
    nl, d, n6 = ada_w.shape
    rows = cpad.shape[0]
    tn = n6 // 4
    return pl.pallas_call(
        _ada_kernel,
        grid=(nl, n6 // tn),
        in_specs=[
            pl.BlockSpec((rows, d), lambda l, j: (0, 0)),
            pl.BlockSpec((1, d, tn), lambda l, j: (l, 0, j)),
            pl.BlockSpec((1, 1, tn), lambda l, j: (l, 0, j)),
        ],
        out_specs=pl.BlockSpec((1, rows, tn), lambda l, j: (l, 0, j)),
        out_shape=jax.ShapeDtypeStruct((nl, rows, n6), F32),
        compiler_params=_params(("arbitrary", "arbitrary")),
        name="ada_terms",
    )(cpad, ada_w, ada_b.reshape(nl, 1, n6))


def _inproj_kernel(*refs, mode, add_pos):
    if add_pos:
        x_ref, pos_ref, mod_ref, g_ref, w_ref, *outs = refs
        x = x_ref[0] + pos_ref[...]
    else:
        x_ref, mod_ref, g_ref, w_ref, *outs = refs
        x = x_ref[0]
    h = _rms(x, g_ref[0:1, :]) * (1.0 + mod_ref[0, 1:2, :]) + mod_ref[0, 0:1, :]
    z = _bdot(h.astype(BF16), w_ref[...])
    if mode == "ab":
        outs[0][0] = z[:, 0:A_WIDTH]
        outs[1][0] = jax.nn.gelu(z[:, A_WIDTH:2 * A_WIDTH])
        vb = z[:, 2 * A_WIDTH:2 * A_WIDTH + B_WIDTH]
        gb = z[:, 2 * A_WIDTH + B_WIDTH:]
        outs[2][0] = vb * jax.nn.sigmoid(gb)
    elif mode == "a":
        outs[0][0] = z
    else:
        outs[0][0] = jax.nn.gelu(z[:, 0:C_WIDTH])
        outs[1][0] = jax.nn.gelu(z[:, C_WIDTH:2 * C_WIDTH])
        outs[2][0] = z[:, 2 * C_WIDTH:]


def _inproj(x, pos, mods, g4, w, mode, per_batch_mod=True):
    bsz, length, d = x.shape
    n = w.shape[1]
    tm = min(ROW_TILE, length)
    n_out = 1 if mode == "a" else 3
    width = n if mode == "a" else 512
    add_pos = pos is not None
    mod_map = (lambda b, i: (b, 0, 0)) if per_batch_mod else (lambda b, i: (0, 0, 0))
    in_specs = [pl.BlockSpec((1, tm, d), lambda b, i: (b, i, 0))]
    args = [x]
    if add_pos:
        in_specs.append(pl.BlockSpec((tm, d), lambda b, i: (i, 0)))
        args.append(pos)
    in_specs += [
        pl.BlockSpec((1, 6, d), mod_map),
        pl.BlockSpec((4, d), lambda b, i: (0, 0)),
        pl.BlockSpec((d, n), lambda b, i: (0, 0)),
    ]
    args += [mods, g4, w]
    outs = pl.pallas_call(
        functools.partial(_inproj_kernel, mode=mode, add_pos=add_pos),
        grid=(bsz, length // tm),
        in_specs=in_specs,
        out_specs=[pl.BlockSpec((1, tm, width), lambda b, i: (b, i, 0))] * n_out,
        out_shape=[jax.ShapeDtypeStruct((bsz, length, width), F32)] * n_out,
        compiler_params=_params(("parallel", "parallel")),
        name="inproj_" + mode,
    )(*args)
    return outs


def _scan_kernel(xa_ref, xc_ref, gg_ref, cw_ref, cb_ref, wg_ref, bg_ref, sp_ref, rec_ref,
                 pad_ref, a_ref, b_ref, carry_ref, *, length, ctx_len):
    nseg = SCAN_SEGMENTS
    rows_per_step = 256

    def coefficients(n):
        def body(c, _):
            base = pl.multiple_of(c * rows_per_step, rows_per_step)
            window = pad_ref[pl.ds(base, rows_per_step + 16), :]
            xconv = cb_ref[...] + cw_ref[0:1, :] * window[7:7 + rows_per_step, :]
            for k in range(1, A_CONV_W):
                xconv = xconv + cw_ref[k:k + 1, :] * window[7 + k:7 + k + rows_per_step, :]
            g = _bdot(xconv.astype(BF16), wg_ref[0]) + bg_ref[0]
            for d in range(2):
                r = jax.nn.sigmoid(g[:, 2 * LANES * d:2 * LANES * d + LANES])
                i = jax.nn.sigmoid(g[:, 2 * LANES * d + LANES:2 * LANES * (d + 1)])
                log_a = (-LRU_C * r) * sp_ref[0, d:d + 1, :]
                a = jnp.exp(log_a)
                one_minus_a2 = -jnp.tanh(log_a) * (a * a + 1.0)
                a_ref[d, pl.ds(base, rows_per_step), :] = a
                b_ref[d, pl.ds(base, rows_per_step), :] = jnp.sqrt(one_minus_a2) * (i * xconv)
            return 0
        lax.fori_loop(0, n // rows_per_step, body, 0)

    def local_scans(n, store):
        seg = n // nseg

        def body(t, carry):
            hf, pf, hb, pb = carry
            tb = seg - 1 - t
            af = a_ref[0, pl.ds(t, nseg, stride=seg), :]
            bf = b_ref[0, pl.ds(t, nseg, stride=seg), :]
            ab = a_ref[1, pl.ds(tb, nseg, stride=seg), :]
            bb = b_ref[1, pl.ds(tb, nseg, stride=seg), :]
            hf = af * hf + bf
            pf = af * pf
            hb = ab * hb + bb
            pb = ab * pb
            if store:
                b_ref[0, pl.ds(t, nseg, stride=seg), :] = hf
                a_ref[0, pl.ds(t, nseg, stride=seg), :] = pf
                b_ref[1, pl.ds(tb, nseg, stride=seg), :] = hb
                a_ref[1, pl.ds(tb, nseg, stride=seg), :] = pb
            return hf, pf, hb, pb

        zeros = jnp.zeros((nseg, LANES), F32)
        ones = jnp.ones((nseg, LANES), F32)
        return lax.fori_loop(0, seg, body, (zeros, ones, zeros, ones))

    def chain_carries(ends, init_f, init_b, store):
        hf, pf, hb, pb = ends
        cf = init_f
        for s in range(nseg):
            if store:
                carry_ref[0, s:s + 1, :] = cf
            cf = pf[s:s + 1, :] * cf + hf[s:s + 1, :]
        cb = init_b
        for s in range(nseg - 1, -1, -1):
            if store:
                carry_ref[1, s:s + 1, :] = cb
            cb = pb[s:s + 1, :] * cb + hb[s:s + 1, :]
        return cf, cb

    zrow = jnp.zeros((8, LANES), F32)

    pad_ref[0:8, :] = zrow
    pad_ref[8:8 + ctx_len, :] = xc_ref[0]
    pad_ref[8 + ctx_len:16 + ctx_len, :] = zrow
    coefficients(ctx_len)
    zero_state = jnp.zeros((1, LANES), F32)
    init_f, init_b = chain_carries(local_scans(ctx_len, False), zero_state, zero_state, False)

    pad_ref[8:8 + length, :] = xa_ref[0]
    pad_ref[8 + length:16 + length, :] = zrow
    coefficients(length)
    chain_carries(local_scans(length, True), init_f, init_b, True)

    seg = length // nseg

    def fix(s, _):
        rows = pl.ds(pl.multiple_of(s * seg, seg), seg)
        hf = b_ref[0, rows, :] + a_ref[0, rows, :] * carry_ref[0, pl.ds(s, 1), :]
        hb = b_ref[1, rows, :] + a_ref[1, rows, :] * carry_ref[1, pl.ds(s, 1), :]
        rec_ref[0, rows, :] = ((hf + hb) * gg_ref[0, rows, :]).astype(BF16)
        return 0
    lax.fori_loop(0, nseg, fix, 0)


def _scan(xa, xc, gg, conv_w, conv_b, wg, bg, sp):
    bsz, length, width = xa.shape
    ctx_len = xc.shape[1]
    ngroups = width // LANES
    return pl.pallas_call(
        functools.partial(_scan_kernel, length=length, ctx_len=ctx_len),
        grid=(bsz, ngroups),
        in_specs=[
            pl.BlockSpec((1, length, LANES), lambda b, c: (b, 0, c)),
            pl.BlockSpec((1, ctx_len, LANES), lambda b, c: (b, 0, c)),
            pl.BlockSpec((1, length, LANES), lambda b, c: (b, 0, c)),
            pl.BlockSpec((A_CONV_W, LANES), lambda b, c: (0, c)),
            pl.BlockSpec((1, LANES), lambda b, c: (0, c)),
            pl.BlockSpec((1, LANES, 4 * LANES), lambda b, c: (c, 0, 0)),
            pl.BlockSpec((1, 1, 4 * LANES), lambda b, c: (c, 0, 0)),
            pl.BlockSpec((1, 2, LANES), lambda b, c: (c, 0, 0)),
        ],
        out_specs=pl.BlockSpec((1, length, LANES), lambda b, c: (b, 0, c)),
        out_shape=jax.ShapeDtypeStruct((bsz, length, width), BF16),
        scratch_shapes=[
            pltpu.VMEM((length + 16, LANES), F32),
            pltpu.VMEM((2, length, LANES), F32),
            pltpu.VMEM((2, length, LANES), F32),
            pltpu.VMEM((2, SCAN_SEGMENTS, LANES), F32),
        ],
        compiler_params=_params(("parallel", "parallel")),
        name="rglru_scan",
    )(xa, xc, gg, conv_w, conv_b.reshape(1, width), wg, bg, sp)


def _gate_weights(w_r, b_r, w_i, b_i, lam):
    heads_per_group = LANES // (A_WIDTH // A_HEADS)
    ngroups = A_WIDTH // LANES
    hd = A_WIDTH // A_HEADS

    def blockdiag(w):
        w = w.reshape(ngroups, heads_per_group, hd, hd)
        eye = jnp.eye(heads_per_group, dtype=w.dtype)
        return jnp.einsum("ghde,hk->ghdke", w, eye).reshape(ngroups, LANES, LANES)

    wg = jnp.concatenate([blockdiag(w_r[0]), blockdiag(w_i[0]), blockdiag(w_r[1]), blockdiag(w_i[1])], axis=-1)
    bg = jnp.stack([b_r[0], b_i[0], b_r[1], b_i[1]], axis=0)
    bg = bg.reshape(4, ngroups, LANES).transpose(1, 0, 2).reshape(ngroups, 1, 4 * LANES)
    sp = jax.nn.softplus(-lam.astype(F32)).reshape(2, ngroups, LANES).transpose(1, 0, 2)
    return wg.astype(BF16), bg, sp


def _conf_out_kernel(glu_ref, rec_ref, x_ref, pos_ref, dww_ref, dwb_ref, lng_ref, lnb_ref, wout_ref, g_ref, mod_ref,
                     o_ref, pad_ref, u_ref, *, length, tm):
    j = pl.program_id(1)
    halo = 16
    sub = 32

    @pl.when(j == 0)
    def _():
        zeros = jnp.zeros((halo, B_WIDTH), F32)
        pad_ref[0:halo, :] = zeros
        pad_ref[halo:halo + length, :] = glu_ref[0]
        pad_ref[halo + length:2 * halo + length, :] = zeros

    base = j * tm

    def body(r, _):
        rb = pl.multiple_of(base + r * sub, sub)
        window = pad_ref[pl.ds(rb, sub + 2 * halo), :]
        acc = jnp.broadcast_to(dwb_ref[...], (sub, B_WIDTH))
        for phase in range(8):
            shifted = window[phase:phase + sub + 24, :]
            for q in range(4):
                k = 8 * q + phase - 1
                if 0 <= k < B_CONV_W:
                    acc = acc + dww_ref[k:k + 1, :] * shifted[8 * q:8 * q + sub, :]
        mu = jnp.mean(acc, axis=-1, keepdims=True)
        cen = acc - mu
        var = jnp.mean(cen * cen, axis=-1, keepdims=True)
        y = cen * lax.rsqrt(var + LN_EPS) * lng_ref[...] + lnb_ref[...]
        u_ref[pl.ds(pl.multiple_of(r * sub, sub), sub), :] = jax.nn.silu(y).astype(BF16)
        return 0
    lax.fori_loop(0, tm // sub, body, 0)

    y = _bdot(rec_ref[0], wout_ref[0:A_WIDTH, :]) + _bdot(u_ref[...], wout_ref[A_WIDTH:, :])
    o_ref[0] = x_ref[0] + pos_ref[...] + mod_ref[0, 2:3, :] * _rms(y, g_ref[1:2, :])


def _conf_out(glu, rec, x, pos, dw_w, dw_b, ln_g, ln_b, w_out, g4, mods):
    bsz, length, d = x.shape
    tm = min(ROW_TILE, length)
    taps = dw_w.shape[0]
    dw_w = jnp.concatenate([dw_w, jnp.zeros((32 - taps, B_WIDTH), F32)], axis=0)
    return pl.pallas_call(
        functools.partial(_conf_out_kernel, length=length, tm=tm),
        grid=(bsz, length // tm),
        in_specs=[
            pl.BlockSpec((1, length, B_WIDTH), lambda b, i: (b, 0, 0)),
            pl.BlockSpec((1, tm, A_WIDTH), lambda b, i: (b, i, 0)),
            pl.BlockSpec((1, tm, d), lambda b, i: (b, i, 0)),
            pl.BlockSpec((tm, d), lambda b, i: (i, 0)),
            pl.BlockSpec((32, B_WIDTH), lambda b, i: (0, 0)),
            pl.BlockSpec((1, B_WIDTH), lambda b, i: (0, 0)),
            pl.BlockSpec((1, B_WIDTH), lambda b, i: (0, 0)),
            pl.BlockSpec((1, B_WIDTH), lambda b, i: (0, 0)),
            pl.BlockSpec((A_WIDTH + B_WIDTH, d), lambda b, i: (0, 0)),
            pl.BlockSpec((4, d), lambda b, i: (0, 0)),
            pl.BlockSpec((1, 6, d), lambda b, i: (b, 0, 0)),
        ],
        out_specs=pl.BlockSpec((1, tm, d), lambda b, i: (b, i, 0)),
        out_shape=jax.ShapeDtypeStruct((bsz, length, d), F32),
        scratch_shapes=[
            pltpu.VMEM((length + 32, B_WIDTH), F32),
            pltpu.VMEM((tm, B_WIDTH), BF16),
        ],
        compiler_params=_params(("parallel", "arbitrary")),
        name="conformer_outproj",
    )(glu, rec, x, pos, dw_w, dw_b.reshape(1, -1), ln_g.reshape(1, -1), ln_b.reshape(1, -1), w_out, g4, mods)


def _ffn_kernel(x_ref, mod_ref, g_ref, w1_ref, w3_ref, w2_ref, o_ref):
    x = x_ref[0]
    h = (_rms(x, g_ref[2:3, :]) * (1.0 + mod_ref[0, 4:5, :]) + mod_ref[0, 3:4, :]).astype(BF16)
    a = _bdot(h, w1_ref[...])
    b = _bdot(h, w3_ref[...])
    t = (jax.nn.silu(a) * b).astype(BF16)
    y = _bdot(t, w2_ref[...])
    o_ref[0] = x + mod_ref[0, 5:6, :] * _rms(y, g_ref[3:4, :])


def _ffn(x, mods, g4, w1, w3, w2):
    bsz, length, d = x.shape
    dff = w1.shape[1]
    tm = min(ROW_TILE, length)
    const = lambda b, i: (0, 0)
    return pl.pallas_call(
        _ffn_kernel,
        grid=(bsz, length // tm),
        in_specs=[
            pl.BlockSpec((1, tm, d), lambda b, i: (b, i, 0)),
            pl.BlockSpec((1, 6, d), lambda b, i: (b, 0, 0)),
            pl.BlockSpec((4, d), const),
            pl.BlockSpec((d, dff), const, pipeline_mode=pl.Buffered(1)),
            pl.BlockSpec((d, dff), const, pipeline_mode=pl.Buffered(1)),
            pl.BlockSpec((dff, d), const, pipeline_mode=pl.Buffered(1)),
        ],
        out_specs=pl.BlockSpec((1, tm, d), lambda b, i: (b, i, 0)),
        out_shape=jax.ShapeDtypeStruct((bsz, length, d), F32),
        compiler_params=_params(("parallel", "parallel")),
        name="dense_swiglu",
    )(x, mods, g4, w1, w3, w2)


def _fourier_kernel(f_ref, dmat_ref, lmat_ref, o_ref, z_ref, *, length, scale):
    j = pl.program_id(1)
    rows_per_step = 512 if length % 512 == 0 else length

    @pl.when(j == 0)
    def _():
        def body(c, _):
            rows = pl.ds(pl.multiple_of(c * rows_per_step, rows_per_step), rows_per_step)
            fb = f_ref[0, rows, :].astype(BF16)
            for g in range(D_GROUPS):
                cs = _bdot(fb[:, g * D_GROUP_DIM:(g + 1) * D_GROUP_DIM], dmat_ref[...])
                z_ref[0, rows, g * D_GROUP_DIM:(g + 1) * D_GROUP_DIM] = cs[:, 0:D_GROUP_DIM].astype(BF16)
                z_ref[1, rows, g * D_GROUP_DIM:(g + 1) * D_GROUP_DIM] = cs[:, D_GROUP_DIM:].astype(BF16)
            return 0
        lax.fori_loop(0, length // rows_per_step, body, 0)

    out = _bdot(lmat_ref[0], z_ref[0]) + _bdot(lmat_ref[1], z_ref[1])
    o_ref[0] = (out * scale).astype(BF16)


def _dft_matrices(length):
    k = np.arange(length, dtype=np.int64)
    ang_l = 2.0 * np.pi * ((k[:, None] * k[None, :]) % length).astype(np.float64) / length
    lmat = np.stack([np.cos(ang_l), -np.sin(ang_l)], axis=0)
    d = np.arange(D_GROUP_DIM, dtype=np.int64)
    ang_d = 2.0 * np.pi * ((d[:, None] * d[None, :]) % D_GROUP_DIM).astype(np.float64) / D_GROUP_DIM
    dmat = np.concatenate([np.cos(ang_d), np.sin(ang_d)], axis=1)
    return jnp.asarray(lmat, dtype=F32).astype(BF16), jnp.asarray(dmat, dtype=F32).astype(BF16)


def _fourier(f):
    bsz, length, width = f.shape
    tm = min(ROW_TILE, length)
    lmat, dmat = _dft_matrices(length)
    scale = float(1.0 / np.sqrt(length * D_GROUP_DIM))
    return pl.pallas_call(
        functools.partial(_fourier_kernel, length=length, scale=scale),
        grid=(bsz, length // tm),
        in_specs=[
            pl.BlockSpec((1, length, width), lambda b, i: (b, 0, 0)),
            pl.BlockSpec((D_GROUP_DIM, 2 * D_GROUP_DIM), lambda b, i: (0, 0)),
            pl.BlockSpec((2, tm, length), lambda b, i: (0, i, 0)),
        ],
        out_specs=pl.BlockSpec((1, tm, width), lambda b, i: (b, i, 0)),
        out_shape=jax.ShapeDtypeStruct((bsz, length, width), BF16),
        scratch_shapes=[pltpu.VMEM((2, length, width), BF16)],
        compiler_params=_params(("parallel", "arbitrary")),
        name="fourier_mix",
    )(f, dmat, lmat)


def _gate_out_kernel(u_ref, v_ref, fo_ref, x_ref, lng_ref, lnb_ref, wcat_ref, bs_ref, wout_ref, g_ref, mod_ref,
                     o_ref, vs_ref, gt_ref, *, tm):
    head_dim = C_WIDTH // C_HEADS
    head_of_lane = lax.broadcasted_iota(jnp.int32, (CHUNK, C_WIDTH), 1) // head_dim
    for ci in range(tm // CHUNK):
        rows = slice(ci * CHUNK, (ci + 1) * CHUNK)
        v = v_ref[0, rows, :]
        mu = jnp.mean(v, axis=-1, keepdims=True)
        cen = v - mu
        var = jnp.mean(cen * cen, axis=-1, keepdims=True)
        vn = (cen * lax.rsqrt(var + LN_EPS) * lng_ref[...] + lnb_ref[...]).astype(BF16)
        for h in range(C_HEADS):
            vs_ref[h * CHUNK:(h + 1) * CHUNK, :] = jnp.where(head_of_lane == h, vn, jnp.zeros_like(vn))
        mixed = _bdot(wcat_ref[...], vs_ref[...]) + bs_ref[...]
        gt_ref[rows, :] = (u_ref[0, rows, :] * mixed).astype(BF16)
    y = _bdot(gt_ref[...], wout_ref[0:C_WIDTH, :]) + _bdot(fo_ref[0], wout_ref[C_WIDTH:, :])
    o_ref[0] = x_ref[0] + mod_ref[0, 2:3, :] * _rms(y, g_ref[1:2, :])


def _gate_out(u, v, fo, x, ln_g, ln_b, w_s, b_s, w_out, g4, mods):
    bsz, length, d = x.shape
    tm = min(ROW_TILE, length)
    head_dim = C_WIDTH // C_HEADS
    wcat = w_s.transpose(1, 0, 2).reshape(CHUNK, C_HEADS * CHUNK).astype(BF16)
    bs_full = jnp.repeat(b_s.T, head_dim, axis=1)
    const = lambda b, i: (0, 0)
    return pl.pallas_call(
        functools.partial(_gate_out_kernel, tm=tm),
        grid=(bsz, length // tm),
        in_specs=[
            pl.BlockSpec((1, tm, C_WIDTH), lambda b, i: (b, i, 0)),
            pl.BlockSpec((1, tm, C_WIDTH), lambda b, i: (b, i, 0)),
            pl.BlockSpec((1, tm, D_WIDTH), lambda b, i: (b, i, 0)),
            pl.BlockSpec((1, tm, d), lambda b, i: (b, i, 0)),
            pl.BlockSpec((1, C_WIDTH), const),
            pl.BlockSpec((1, C_WIDTH), const),
            pl.BlockSpec((CHUNK, C_HEADS * CHUNK), const),
            pl.BlockSpec((CHUNK, C_WIDTH), const),
            pl.BlockSpec((C_WIDTH + D_WIDTH, d), const),
            pl.BlockSpec((4, d), const),
            pl.BlockSpec((1, 6, d), lambda b, i: (b, 0, 0)),
        ],
        out_specs=pl.BlockSpec((1, tm, d), lambda b, i: (b, i, 0)),
        out_shape=jax.ShapeDtypeStruct((bsz, length, d), F32),
        scratch_shapes=[
            pltpu.VMEM((C_HEADS * CHUNK, C_WIDTH), BF16),
            pltpu.VMEM((tm, C_WIDTH), BF16),
        ],
        compiler_params=_params(("parallel", "parallel")),
        name="gating_outproj",
    )(u, v, fo, x, ln_g.reshape(1, -1), ln_b.reshape(1, -1), wcat, bs_full, w_out, g4, mods)


def _router_kernel(x_ref, mod_ref, g_ref, wr_ref, tri_ref, xn_ref, meta_ref, cnt_ref, carry_ref, *, tm):
    first = jnp.logical_and(pl.program_id(0) == 0, pl.program_id(1) == 0)

    @pl.when(first)
    def _():
        carry_ref[...] = jnp.zeros_like(carry_ref)

    x = x_ref[0]
    h = _rms(x, g_ref[2:3, :]) * (1.0 + mod_ref[0, 4:5, :]) + mod_ref[0, 3:4, :]
    half = h.shape[1] // 2
    hi = pltpu.bitcast(h[:, :half].astype(BF16).astype(F32), jnp.int32)
    lo = pltpu.bitcast(h[:, half:].astype(BF16).astype(F32), jnp.int32)
    xn_ref[0] = hi | lax.shift_right_logical(lo, 16)
    logits = jnp.dot(h, wr_ref[...], preferred_element_type=F32, precision=lax.Precision.HIGHEST)
    lane = lax.broadcasted_iota(jnp.int32, (tm, LANES), 1)
    neg = jnp.float32(-jnp.inf)
    lm = jnp.where(lane < N_EXPERTS, logits, neg)
    m1 = jnp.max(lm, axis=-1, keepdims=True)
    i1 = jnp.min(jnp.where(lm == m1, lane, LANES), axis=-1, keepdims=True)
    lm2 = jnp.where(lane == i1, neg, lm)
    m2 = jnp.max(lm2, axis=-1, keepdims=True)
    i2 = jnp.min(jnp.where(lm2 == m2, lane, LANES), axis=-1, keepdims=True)
    e = jnp.exp(m2 - m1)
    p1 = 1.0 / (1.0 + e)
    p2 = e / (1.0 + e)
    hot1 = (lane == i1).astype(F32)
    hot2 = (lane == i2).astype(F32)
    both = hot1 + hot2
    before = _bdot(tri_ref[...], both.astype(BF16)) + carry_ref[0:1, :]
    r1 = jnp.sum(before * hot1, axis=-1, keepdims=True)
    r2 = jnp.sum(before * hot2, axis=-1, keepdims=True)
    carry_ref[0:1, :] = carry_ref[0:1, :] + jnp.sum(both, axis=0, keepdims=True)
    cnt_ref[...] = carry_ref[...]
    vals = (i1.astype(F32), i2.astype(F32), p1, p2, r1, r2)
    meta = jnp.zeros((tm, LANES), F32)
    for k, val in enumerate(vals):
        meta = jnp.where(lane == k, val, meta)
    meta_ref[0] = meta


def _router(x, mods, g4, w_router):
    bsz, length, d = x.shape
    tm = min(ROW_TILE, length)
    wr = jnp.concatenate([w_router, jnp.zeros((d, LANES - N_EXPERTS), F32)], axis=1)
    tri = jnp.asarray(np.tril(np.ones((tm, tm), np.float32), -1), dtype=BF16)
    const = lambda b, i: (0, 0)
    return pl.pallas_call(
        functools.partial(_router_kernel, tm=tm),
        grid=(bsz, length // tm),
        in_specs=[
            pl.BlockSpec((1, tm, d), lambda b, i: (b, i, 0)),
            pl.BlockSpec((1, 6, d), lambda b, i: (b, 0, 0)),
            pl.BlockSpec((4, d), const),
            pl.BlockSpec((d, LANES), const),
            pl.BlockSpec((tm, tm), const),
        ],
        out_specs=[
            pl.BlockSpec((1, tm, d // 2), lambda b, i: (b, i, 0)),
            pl.BlockSpec((1, tm, LANES), lambda b, i: (b, i, 0)),
            pl.BlockSpec((8, LANES), const),
        ],
        out_shape=[
            jax.ShapeDtypeStruct((bsz, length, d // 2), jnp.int32),
            jax.ShapeDtypeStruct((bsz, length, LANES), F32),
            jax.ShapeDtypeStruct((8, LANES), F32),
        ],
        scratch_shapes=[pltpu.VMEM((8, LANES), F32)],
        compiler_params=_params(("arbitrary", "arbitrary")),
        name="moe_router",
    )(x, mods, g4, wr, tri)


def _expert_kernel(te_ref, nv_ref, x_ref, w1_ref, w3_ref, w2_ref, o_ref, acc_ref):
    i = pl.program_id(0)
    j = pl.program_id(1)

    @pl.when(i < nv_ref[0])
    def _():
        packed = x_ref[...]
        x = jnp.concatenate(
            [pltpu.bitcast(packed & jnp.int32(-65536), F32).astype(BF16),
             pltpu.bitcast(packed << 16, F32).astype(BF16)], axis=1)
        a = _bdot(x, w1_ref[0])
        b = _bdot(x, w3_ref[0])
        t = (jax.nn.silu(a) * b).astype(BF16)
        y = _bdot(t, w2_ref[0])

        @pl.when(j == 0)
        def _():
            acc_ref[...] = y

        @pl.when(j > 0)
        def _():
            acc_ref[...] += y

        @pl.when(j == pl.num_programs(1) - 1)
        def _():
            o_ref[...] = acc_ref[...]


def _experts(x_sorted, tile_expert, n_valid, w1, w3, w2):
    rows = x_sorted.shape[0]
    d = w1.shape[1]
    dff = w1.shape[2]
    tm = EXPERT_ROW_TILE
    n_ff = 2
    tf = dff // n_ff

    def row_map(i, j, te, nv):
        return (jnp.maximum(jnp.minimum(i, nv[0] - 1), 0), 0)

    def w13_map(i, j, te, nv):
        return (te[jnp.maximum(jnp.minimum(i, nv[0] - 1), 0)], 0, jnp.where(i < nv[0], j, n_ff - 1))

    def w2_map(i, j, te, nv):
        return (te[jnp.maximum(jnp.minimum(i, nv[0] - 1), 0)], jnp.where(i < nv[0], j, n_ff - 1), 0)

    return pl.pallas_call(
        _expert_kernel,
        grid_spec=pltpu.PrefetchScalarGridSpec(
            num_scalar_prefetch=2,
            grid=(rows // tm, n_ff),
            in_specs=[
                pl.BlockSpec((tm, d // 2), row_map),
                pl.BlockSpec((1, d, tf), w13_map),
                pl.BlockSpec((1, d, tf), w13_map),
                pl.BlockSpec((1, tf, d), w2_map),
            ],
            out_specs=pl.BlockSpec((tm, d), row_map),
            scratch_shapes=[pltpu.VMEM((tm, d), F32)],
        ),
        out_shape=jax.ShapeDtypeStruct((rows, d), F32),
        compiler_params=_params(("arbitrary", "arbitrary")),
        name="expert_swiglu",
    )(tile_expert, n_valid, x_sorted, w1, w3, w2)


def _combine_kernel(x_ref, y_ref, meta_ref, g_ref, mod_ref, o_ref):
    meta = meta_ref[0]
    y = meta[:, 2:3] * y_ref[0, 0] + meta[:, 3:4] * y_ref[1, 0]
    o_ref[0] = x_ref[0] + mod_ref[0, 5:6, :] * _rms(y, g_ref[3:4, :])


def _combine(x, y2, meta, g4, mods):
    bsz, length, d = x.shape
    tm = min(ROW_TILE, length)
    return pl.pallas_call(
        _combine_kernel,
        grid=(bsz, length // tm),
        in_specs=[
            pl.BlockSpec((1, tm, d), lambda b, i: (b, i, 0)),
            pl.BlockSpec((2, 1, tm, d), lambda b, i: (0, b, i, 0)),
            pl.BlockSpec((1, tm, LANES), lambda b, i: (b, i, 0)),
            pl.BlockSpec((4, d), lambda b, i: (0, 0)),
            pl.BlockSpec((1, 6, d), lambda b, i: (b, 0, 0)),
        ],
        out_specs=pl.BlockSpec((1, tm, d), lambda b, i: (b, i, 0)),
        out_shape=jax.ShapeDtypeStruct((bsz, length, d), F32),
        compiler_params=_params(("parallel", "parallel")),
        name="moe_combine",
    )(x, y2, meta, g4, mods)


SC_CORES = 2
SC_SUBCORES = 16
SC_WORKERS = SC_CORES * SC_SUBCORES
SC_INDEX_CHUNK = 128


def _sc_mesh():
    return plsc.VectorSubcoreMesh(core_axis_name="c", subcore_axis_name="s")


def _sc_dispatch(rows_in, pos2, n_out):
    tokens, width = rows_in.shape
    per_worker = tokens // SC_WORKERS
    chunk = SC_INDEX_CHUNK
    n_chunks = per_worker // chunk
    idx = pos2.reshape(2, SC_WORKERS, n_chunks, chunk)

    @functools.partial(
        pl.kernel, mesh=_sc_mesh(),
        out_type=jax.ShapeDtypeStruct((n_out, width), rows_in.dtype),
        scratch_types=[
            pltpu.VMEM((n_chunks, chunk), jnp.int32),
            pltpu.VMEM((n_chunks, chunk), jnp.int32),
            pltpu.VMEM((chunk, width), rows_in.dtype),
        ],
        name="sc_dispatch",
    )
    def body(rows_hbm, idx_hbm, out_hbm, idx0_v, idx1_v, rows_v):
        wid = lax.axis_index("s") * SC_CORES + lax.axis_index("c")
        base = wid * per_worker
        pltpu.sync_copy(idx_hbm.at[0, wid], idx0_v)
        pltpu.sync_copy(idx_hbm.at[1, wid], idx1_v)

        @pl.loop(0, n_chunks)
        def _(c):
            pltpu.sync_copy(rows_hbm.at[pl.ds(base + c * chunk, chunk)], rows_v)
            pltpu.sync_copy(rows_v, out_hbm.at[idx0_v.at[c]])
            pltpu.sync_copy(rows_v, out_hbm.at[idx1_v.at[c]])

    return body(rows_in, idx)


def _sc_gather(table, idx, chunk):
    n = idx.shape[0]
    width = table.shape[1]
    per_worker = n // SC_WORKERS
    n_chunks = per_worker // chunk
    idx3 = idx.reshape(SC_WORKERS, n_chunks, chunk)

    @functools.partial(
        pl.kernel, mesh=_sc_mesh(),
        out_type=jax.ShapeDtypeStruct((n, width), table.dtype),
        scratch_types=[
            pltpu.VMEM((n_chunks, chunk), jnp.int32),
            pltpu.VMEM((chunk, width), table.dtype),
        ],
        name="sc_gather",
    )
    def body(table_hbm, idx_hbm, out_hbm, idx_v, rows_v):
        wid = lax.axis_index("s") * SC_CORES + lax.axis_index("c")
        base = wid * per_worker
        pltpu.sync_copy(idx_hbm.at[wid], idx_v)

        @pl.loop(0, n_chunks)
        def _(c):
            pltpu.sync_copy(table_hbm.at[idx_v.at[c]], rows_v)
            pltpu.sync_copy(rows_v, out_hbm.at[pl.ds(base + c * chunk, chunk)])

    return body(table, idx3)


def _moe(x, mods, g4, w_router, w1, w3, w2):
    bsz, length, d = x.shape
    tokens = bsz * length
    tm = EXPERT_ROW_TILE
    xn, meta, counts = _router(x, mods, g4, w_router)
    meta2 = meta.reshape(tokens, LANES)
    experts = meta2[:, 0:2].astype(jnp.int32)
    ranks = meta2[:, 4:6].astype(jnp.int32)
    cnt = counts[0, 0:N_EXPERTS].astype(jnp.int32)
    padded = ((cnt + tm - 1) // tm) * tm
    ends = jnp.cumsum(padded)
    starts = ends - padded
    pos = (starts[experts] + ranks).T.reshape(2 * tokens)
    rows = 2 * tokens + N_EXPERTS * tm
    n_tiles = rows // tm
    n_valid = (ends[-1] // tm).astype(jnp.int32).reshape(1)
    tile_start = jnp.arange(n_tiles, dtype=jnp.int32) * tm
    tile_expert = jnp.minimum(
        jnp.sum((tile_start[:, None] >= ends[None, :]).astype(jnp.int32), axis=1), N_EXPERTS - 1)
    x_sorted = _sc_dispatch(xn.reshape(tokens, d // 2), pos.reshape(2, tokens), rows)
    y_sorted = _experts(x_sorted, tile_expert, n_valid, w1, w3, w2)
    y2 = _sc_gather(y_sorted, pos, 32).reshape(2, bsz, length, d)
    return _combine(x, y2, meta, g4, mods)


def _grid_pos_embed(rows, d):
    row = jnp.repeat(jnp.arange(rows, dtype=F32), GRID_W)
    col = jnp.tile(jnp.arange(GRID_W, dtype=F32), rows)
    n_freq = d // 4
    omega = POS_THETA ** (-jnp.arange(n_freq, dtype=F32) / n_freq)
    ang_r = row[:, None] * omega
    ang_c = col[:, None] * omega
    return jnp.concatenate([jnp.sin(ang_r), jnp.cos(ang_r), jnp.sin(ang_c), jnp.cos(ang_c)], axis=-1)


def kernel(x, c, ctx, c_ctx, ada_w, ada_b, norm_g, ab_w_in, rg_conv_w, rg_conv_b, rg_w_r, rg_b_r, rg_w_i, rg_b_i, rg_lambda, cv_dw_w, cv_dw_b, cv_ln_g, cv_ln_b, ab_w_out, ffn_w1, ffn_w3, ffn_w2, cd_w_in, sg_ln_g, sg_ln_b, sg_w_s, sg_b_s, cd_w_out, moe_router, moe_w1, moe_w3, moe_w2):
    bsz, length, d = x.shape
    assert ada_w.shape[0] == 2, "two layers: one even (RG-LRU | Conformer), one odd (gMLP | Fourier)"
    pos = _grid_pos_embed(length // GRID_W, d)

    cpad = jnp.concatenate([c, c_ctx[None, :], jnp.zeros((16 - bsz - 1, d), F32)], axis=0)
    m = _ada(cpad, ada_w, ada_b)
    mods = m[:, :bsz].reshape(2, bsz, 6, d)
    mods_ctx = m[:, bsz:bsz + 1].reshape(2, 1, 6, d)

    w_in = ab_w_in[0].astype(BF16)
    xa, gg, glu = _inproj(x, pos, mods[0], norm_g[0], w_in, "ab")
    (xc,) = _inproj(ctx, None, mods_ctx[0], norm_g[0], w_in[:, 0:A_WIDTH], "a", per_batch_mod=False)
    wg, bg, sp = _gate_weights(rg_w_r[0], rg_b_r[0], rg_w_i[0], rg_b_i[0], rg_lambda[0])
    rec = _scan(xa, xc, gg, rg_conv_w[0], rg_conv_b[0], wg, bg, sp)
    x1 = _conf_out(glu, rec, x, pos, cv_dw_w[0], cv_dw_b[0], cv_ln_g[0], cv_ln_b[0], ab_w_out[0].astype(BF16),
                   norm_g[0], mods[0])
    x2 = _ffn(x1, mods[0], norm_g[0], ffn_w1[0].astype(BF16), ffn_w3[0].astype(BF16), ffn_w2[0].astype(BF16))

    u, v, f = _inproj(x2, None, mods[1], norm_g[1], cd_w_in[0].astype(BF16), "cd")
    fo = _fourier(f)
    x3 = _gate_out(u, v, fo, x2, sg_ln_g[0], sg_ln_b[0], sg_w_s[0], sg_b_s[0], cd_w_out[0].astype(BF16),
                   norm_g[1], mods[1])
    return _moe(x3, mods[1], norm_g[1], moe_router[0], moe_w1[0].astype(BF16), moe_w3[0].astype(BF16),
                moe_w2[0].astype(BF16))
```

```python
import functools

import numpy as np
import jax
import jax.numpy as jnp
from jax import lax
from jax.experimental import pallas as pl
from jax.experimental.pallas import tpu as pltpu
from jax.experimental.pallas import tpu_sc as plsc

F32 = jnp.float32
BF16 = jnp.bfloat16

GRID_W = 64
POS_THETA = 10000.0
NORM_EPS = 1e-6
LN_EPS = 1e-5
LRU_C = 8.0

A_WIDTH = 512
A_HEADS = 8
A_CONV_W = 4
B_WIDTH = 512
B_CONV_W = 31
C_WIDTH = 512
C_HEADS = 8
CHUNK = 128
D_WIDTH = 512
D_GROUPS = 4
D_GROUP_DIM = D_WIDTH // D_GROUPS
N_EXPERTS = 8

LANES = 128
ROW_TILE = 512
EXPERT_ROW_TILE = 512
VMEM_LIMIT = 56 * 2 ** 20


def _params(sem):
    return pltpu.CompilerParams(dimension_semantics=sem, vmem_limit_bytes=VMEM_LIMIT)


def _rms(x, g):
    return x * lax.rsqrt(jnp.mean(x * x, axis=-1, keepdims=True) + NORM_EPS) * g


def _bdot(a, b):
    return jnp.dot(a, b, preferred_element_type=F32)


def _ada_kernel(c_ref, w_ref, b_ref, o_ref):
    s = jax.nn.silu(c_ref[...])
    o_ref[0] = _bdot(s.astype(BF16), w_ref[0].astype(BF16)) + b_ref[0]


def _ada(cpad, ada_w, ada_b):
    nl, d, n6 = ada_w.shape
    rows = cpad.shape[0]
    tn = n6 // 4
    return pl.pallas_call(
        _ada_kernel,
        grid=(nl, n6 // tn),
        in_specs=[
            pl.BlockSpec((rows, d), lambda l, j: (0, 0)),
            pl.BlockSpec((1, d, tn), lambda l, j: (l, 0, j)),
            pl.BlockSpec((1, 1, tn), lambda l, j: (l, 0, j)),
        ],
        out_specs=pl.BlockSpec((1, rows, tn), lambda l, j: (l, 0, j)),
        out_shape=jax.ShapeDtypeStruct((nl, rows, n6), F32),
        compiler_params=_params(("arbitrary", "arbitrary")),
        name="ada_terms",
    )(cpad, ada_w, ada_b.reshape(nl, 1, n6))


TIME_TILE = 128
SCAN_TIME_TILE = 128
SCAN_LANES = 256
CONV_HALO = 16


def _inproj_t_kernel(*refs, tt, add_pos, branches):
    if add_pos:
        x_ref, xp_ref, xn_ref, pos_ref, posp_ref, posn_ref, mod_ref, g_ref, w_ref, cw_ref, cb_ref, *outs = refs
        xall = jnp.concatenate(
            [xp_ref[...] + posp_ref[...], x_ref[...] + pos_ref[...], xn_ref[...] + posn_ref[...]], axis=0)
    else:
        x_ref, xp_ref, xn_ref, mod_ref, g_ref, w_ref, cw_ref, cb_ref, *outs = refs
        xall = jnp.concatenate([xp_ref[...], x_ref[...], xn_ref[...]], axis=0)
    i = pl.program_id(0)
    last = pl.num_programs(0) - 1
    steps, bsz, d = xall.shape
    h = _rms(xall, g_ref[0:1, :]) * (1.0 + mod_ref[1]) + mod_ref[0]
    z = _bdot(h.reshape(steps * bsz, d).astype(BF16), w_ref[...])
    z3 = z.reshape(steps, bsz, z.shape[-1])
    t_idx = lax.broadcasted_iota(jnp.int32, (steps, bsz, A_WIDTH), 0)
    inside = jnp.logical_and(jnp.logical_or(t_idx >= 1, i > 0), jnp.logical_or(t_idx <= tt, i < last))
    xa = jnp.where(inside, z3[:, :, 0:A_WIDTH], 0.0)
    xconv = cb_ref[...] + cw_ref[0:1, :] * xa[0:tt]
    for k in range(1, A_CONV_W):
        xconv = xconv + cw_ref[k:k + 1, :] * xa[k:k + tt]
    outs[0][...] = xconv
    if branches:
        outs[1][...] = jax.nn.gelu(z3[1:tt + 1, :, A_WIDTH:2 * A_WIDTH])
        vb = z3[1:tt + 1, :, 2 * A_WIDTH:2 * A_WIDTH + B_WIDTH]
        gb = z3[1:tt + 1, :, 2 * A_WIDTH + B_WIDTH:]
        outs[2][...] = vb * jax.nn.sigmoid(gb)


def _inproj_t(x3, pos3, mods_t, g4, w, conv_w, conv_b, branches):
    length, bsz, d = x3.shape
    n = w.shape[1]
    tt = min(TIME_TILE, length)
    add_pos = pos3 is not None
    n_out = 3 if branches else 1
    prev_map = lambda i: (jnp.maximum(i * tt - 1, 0), 0, 0)
    next_map = lambda i: (jnp.minimum((i + 1) * (tt // 2), length // 2 - 1), 0, 0)
    in_specs = [
        pl.BlockSpec((tt, bsz, d), lambda i: (i, 0, 0)),
        pl.BlockSpec((1, bsz, d), prev_map),
        pl.BlockSpec((2, bsz, d), next_map),
    ]
    args = [x3, x3, x3]
    if add_pos:
        in_specs += [
            pl.BlockSpec((tt, 1, d), lambda i: (i, 0, 0)),
            pl.BlockSpec((1, 1, d), prev_map),
            pl.BlockSpec((2, 1, d), next_map),
        ]
        args += [pos3, pos3, pos3]
    in_specs += [
        pl.BlockSpec((6, bsz, d), lambda i: (0, 0, 0)),
        pl.BlockSpec((4, d), lambda i: (0, 0)),
        pl.BlockSpec((d, n), lambda i: (0, 0)),
        pl.BlockSpec((A_CONV_W, A_WIDTH), lambda i: (0, 0)),
        pl.BlockSpec((1, A_WIDTH), lambda i: (0, 0)),
    ]
    args += [mods_t, g4, w, conv_w, conv_b.reshape(1, A_WIDTH)]
    return pl.pallas_call(
        functools.partial(_inproj_t_kernel, tt=tt, add_pos=add_pos, branches=branches),
        grid=(length // tt,),
        in_specs=in_specs,
        out_specs=[pl.BlockSpec((tt, bsz, A_WIDTH), lambda i: (i, 0, 0))] * n_out,
        out_shape=[jax.ShapeDtypeStruct((length, bsz, A_WIDTH), F32)] * n_out,
        compiler_params=_params(("parallel",)),
        name="inproj_ab" if branches else "inproj_ctx",
    )(*args)


def _scan_t_kernel(xf_ref, xb_ref, ggf_ref, ggb_ref, xc_ref, wg_ref, bg_ref, sp_ref, hi_ref, lo_ref,
                   a_ref, b_ref, h_ref, state_ref, *, tb, nb, ctx_len):
    k = pl.program_id(1)
    half = nb // 2
    bsz, lanes = xf_ref.shape[1], xf_ref.shape[2]
    chunk = 32

    def coefficients(x_ref, n, direction):
        def body(c, _):
            t0 = pl.multiple_of(c * chunk, chunk)
            x = x_ref[pl.ds(t0, chunk)].reshape(chunk * bsz, lanes)
            g = _bdot(x.astype(BF16), wg_ref[0, direction]) + bg_ref[0, direction]
            r = jax.nn.sigmoid(g[:, 0:lanes])
            i = jax.nn.sigmoid(g[:, lanes:])
            log_a = (-LRU_C * r) * sp_ref[0, direction:direction + 1, :]
            a = jnp.exp(log_a)
            one_minus_a2 = -jnp.tanh(log_a) * (a * a + 1.0)
            a_ref[direction, pl.ds(t0, chunk)] = a.reshape(chunk, bsz, lanes)
            b_ref[direction, pl.ds(t0, chunk)] = (jnp.sqrt(one_minus_a2) * (i * x)).reshape(chunk, bsz, lanes)
            return 0
        lax.fori_loop(0, n // chunk, body, 0)

    def sweep(n, store):
        def body(t, carry):
            hf, hb = carry
            tr = n - 1 - t
            hf = a_ref[0, pl.ds(t, 1)][0] * hf + b_ref[0, pl.ds(t, 1)][0]
            hb = a_ref[1, pl.ds(tr, 1)][0] * hb + b_ref[1, pl.ds(tr, 1)][0]
            if store:
                b_ref[0, pl.ds(t, 1)] = hf[None]
                b_ref[1, pl.ds(tr, 1)] = hb[None]
            return hf, hb
        hf, hb = lax.fori_loop(0, n, body, (state_ref[0], state_ref[1]), unroll=8)
        state_ref[0] = hf
        state_ref[1] = hb

    @pl.when(k == 0)
    def _():
        state_ref[...] = jnp.zeros_like(state_ref)
        coefficients(xc_ref, ctx_len, 0)
        coefficients(xc_ref, ctx_len, 1)
        sweep(ctx_len, False)

    coefficients(xf_ref, tb, 0)
    coefficients(xb_ref, tb, 1)
    sweep(tb, True)
    m = nb - 1 - k

    @pl.when(k < half)
    def _():
        h_ref[k] = b_ref[0, 0:tb]
        h_ref[m] = b_ref[1, 0:tb]

    @pl.when(k >= half)
    def _():
        hi_ref[...] = (b_ref[0, 0:tb] + h_ref[k]) * ggf_ref[...]
        lo_ref[...] = (h_ref[m] + b_ref[1, 0:tb]) * ggb_ref[...]


def _scan_t(xa3, xc3, gg3, wg, bg, sp):
    length, bsz, width = xa3.shape
    ctx_len = xc3.shape[0]
    tb = min(SCAN_TIME_TILE, length // 2)
    nb = length // tb
    half = nb // 2
    lanes = SCAN_LANES
    blk = (tb, bsz, lanes)
    return pl.pallas_call(
        functools.partial(_scan_t_kernel, tb=tb, nb=nb, ctx_len=ctx_len),
        grid=(width // lanes, nb),
        in_specs=[
            pl.BlockSpec(blk, lambda g, k: (k, 0, g)),
            pl.BlockSpec(blk, lambda g, k: (nb - 1 - k, 0, g)),
            pl.BlockSpec(blk, lambda g, k: (jnp.maximum(k, half), 0, g)),
            pl.BlockSpec(blk, lambda g, k: (jnp.minimum(nb - 1 - k, half - 1), 0, g)),
            pl.BlockSpec((ctx_len, bsz, lanes), lambda g, k: (0, 0, g)),
            pl.BlockSpec((1, 2, lanes, 2 * lanes), lambda g, k: (g, 0, 0, 0)),
            pl.BlockSpec((1, 2, 1, 2 * lanes), lambda g, k: (g, 0, 0, 0)),
            pl.BlockSpec((1, 2, lanes), lambda g, k: (g, 0, 0)),
        ],
        out_specs=[
            pl.BlockSpec(blk, lambda g, k: (jnp.maximum(k - half, 0), 0, g)),
            pl.BlockSpec(blk, lambda g, k: (jnp.minimum(nb - 1 - k, half - 1), 0, g)),
        ],
        out_shape=[jax.ShapeDtypeStruct((length // 2, bsz, width), F32)] * 2,
        scratch_shapes=[
            pltpu.VMEM((2, max(tb, ctx_len), bsz, lanes), F32),
            pltpu.VMEM((2, max(tb, ctx_len), bsz, lanes), F32),
            pltpu.VMEM((nb, tb, bsz, lanes), F32),
            pltpu.VMEM((2, bsz, lanes), F32),
        ],
        compiler_params=_params(("parallel", "arbitrary")),
        name="rglru_scan",
    )(xa3, xa3, gg3, gg3, xc3, wg, bg, sp)


def _gate_weights(w_r, b_r, w_i, b_i, lam):
    hd = A_WIDTH // A_HEADS
    heads_per_group = SCAN_LANES // hd
    ngroups = A_WIDTH // SCAN_LANES

    def blockdiag(w):
        w = w.reshape(ngroups, heads_per_group, hd, hd)
        eye = jnp.eye(heads_per_group, dtype=w.dtype)
        return jnp.einsum("ghde,hk->ghdke", w, eye).reshape(ngroups, SCAN_LANES, SCAN_LANES)

    wg = jnp.stack([jnp.concatenate([blockdiag(w_r[d]), blockdiag(w_i[d])], axis=-1) for d in range(2)], axis=1)
    bg = jnp.stack([jnp.concatenate([b_r[d].reshape(ngroups, 1, SCAN_LANES), b_i[d].reshape(ngroups, 1, SCAN_LANES)],
                                    axis=-1) for d in range(2)], axis=1)
    sp = jax.nn.softplus(-lam.astype(F32)).reshape(2, ngroups, SCAN_LANES).transpose(1, 0, 2)
    return wg.astype(BF16), bg, sp


def _conv_t_kernel(x_ref, xp_ref, xn_ref, w_ref, b_ref, lng_ref, lnb_ref, o_ref, stage_ref, *, tb):
    i = pl.program_id(0)
    last = pl.num_programs(0) - 1
    halo = CONV_HALO
    stage_ref[0:halo] = jnp.where(i > 0, xp_ref[...], 0.0)
    stage_ref[halo:halo + tb] = x_ref[...]
    stage_ref[halo + tb:2 * halo + tb] = jnp.where(i < last, xn_ref[...], 0.0)
    sub = 8

    def body(r, _):
        t0 = pl.multiple_of(r * sub, sub)
        def tap(k, acc):
            return acc + w_ref[k] * stage_ref[pl.ds(t0 + 1 + k, sub)]
        acc = lax.fori_loop(0, B_CONV_W + 1, tap, jnp.broadcast_to(b_ref[...], (sub,) + b_ref.shape), unroll=8)
        o_ref[pl.ds(t0, sub)] = acc
        return 0
    lax.fori_loop(0, tb // sub, body, 0)

    norm_rows = 16

    def norm_body(r, _):
        t0 = pl.multiple_of(r * norm_rows, norm_rows)
        acc = o_ref[pl.ds(t0, norm_rows)]
        mu = jnp.mean(acc, axis=-1, keepdims=True)
        cen = acc - mu
        var = jnp.mean(cen * cen, axis=-1, keepdims=True)
        y = cen * lax.rsqrt(var + LN_EPS) * lng_ref[...] + lnb_ref[...]
        o_ref[pl.ds(t0, norm_rows)] = jax.nn.silu(y)
        return 0
    lax.fori_loop(0, tb // norm_rows, norm_body, 0)


def _conv_t(glu3, dw_w, dw_b, ln_g, ln_b):
    length, bsz, width = glu3.shape
    tb = min(TIME_TILE, length)
    halo = CONV_HALO
    taps = dw_w.shape[0]
    w8 = jnp.broadcast_to(jnp.concatenate([dw_w, jnp.zeros((32 - taps, width), F32)], axis=0)[:, None, :],
                          (32, bsz, width))
    b8 = jnp.broadcast_to(dw_b[None, :], (bsz, width))
    return pl.pallas_call(
        functools.partial(_conv_t_kernel, tb=tb),
        grid=(length // tb,),
        in_specs=[
            pl.BlockSpec((tb, bsz, width), lambda i: (i, 0, 0)),
            pl.BlockSpec((halo, bsz, width), lambda i: (jnp.maximum(i * (tb // halo) - 1, 0), 0, 0)),
            pl.BlockSpec((halo, bsz, width), lambda i: (jnp.minimum((i + 1) * (tb // halo), length // halo - 1), 0, 0)),
            pl.BlockSpec((32, bsz, width), lambda i: (0, 0, 0)),
            pl.BlockSpec((bsz, width), lambda i: (0, 0)),
            pl.BlockSpec((1, width), lambda i: (0, 0)),
            pl.BlockSpec((1, width), lambda i: (0, 0)),
        ],
        out_specs=pl.BlockSpec((tb, bsz, width), lambda i: (i, 0, 0)),
        out_shape=jax.ShapeDtypeStruct((length, bsz, width), F32),
        scratch_shapes=[pltpu.VMEM((tb + 2 * halo, bsz, width), F32)],
        compiler_params=_params(("parallel",)),
        name="conformer_conv",
    )(glu3, glu3, glu3, w8, b8, ln_g.reshape(1, -1), ln_b.reshape(1, -1))


def _outproj_t_kernel(lo_ref, hi_ref, u_ref, x_ref, pos_ref, wout_ref, g_ref, mod_ref, o_ref, *, half_steps):
    i = pl.program_id(0)
    tt, bsz, d = x_ref.shape
    rec = jnp.where(i < half_steps, lo_ref[...], hi_ref[...])
    rows = tt * bsz
    y = (_bdot(rec.reshape(rows, A_WIDTH).astype(BF16), wout_ref[0:A_WIDTH, :])
         + _bdot(u_ref[...].reshape(rows, B_WIDTH).astype(BF16), wout_ref[A_WIDTH:, :]))
    yn = _rms(y, g_ref[1:2, :]).reshape(tt, bsz, d)
    o_ref[...] = x_ref[...] + pos_ref[...] + mod_ref[2] * yn


def _outproj_t(rec_lo, rec_hi, u3, x3, pos3, w_out, g4, mods_t):
    length, bsz, d = x3.shape
    tt = min(TIME_TILE // 2, length // 2)
    half_steps = (length // 2) // tt
    return pl.pallas_call(
        functools.partial(_outproj_t_kernel, half_steps=half_steps),
        grid=(length // tt,),
        in_specs=[
            pl.BlockSpec((tt, bsz, A_WIDTH), lambda i: (jnp.minimum(i, half_steps - 1), 0, 0)),
            pl.BlockSpec((tt, bsz, A_WIDTH), lambda i: (jnp.maximum(i - half_steps, 0), 0, 0)),
            pl.BlockSpec((tt, bsz, B_WIDTH), lambda i: (i, 0, 0)),
            pl.BlockSpec((tt, bsz, d), lambda i: (i, 0, 0)),
            pl.BlockSpec((tt, 1, d), lambda i: (i, 0, 0)),
            pl.BlockSpec((A_WIDTH + B_WIDTH, d), lambda i: (0, 0)),
            pl.BlockSpec((4, d), lambda i: (0, 0)),
            pl.BlockSpec((6, bsz, d), lambda i: (0, 0, 0)),
        ],
        out_specs=pl.BlockSpec((tt, bsz, d), lambda i: (i, 0, 0)),
        out_shape=jax.ShapeDtypeStruct((length, bsz, d), F32),
        compiler_params=_params(("parallel",)),
        name="mixer_ab_outproj",
    )(rec_lo, rec_hi, u3, x3, pos3, w_out, g4, mods_t)


def _ffn_t_kernel(x_ref, mod_ref, g_ref, w1_ref, w3_ref, w2_ref, o_ref):
    x = x_ref[...]
    tt, bsz, d = x.shape
    h = (_rms(x, g_ref[2:3, :]) * (1.0 + mod_ref[4]) + mod_ref[3]).reshape(tt * bsz, d).astype(BF16)
    a = _bdot(h, w1_ref[...])
    b = _bdot(h, w3_ref[...])
    t = (jax.nn.silu(a) * b).astype(BF16)
    y = _bdot(t, w2_ref[...])
    o_ref[...] = x + mod_ref[5] * _rms(y, g_ref[3:4, :]).reshape(tt, bsz, d)


def _ffn_t(x3, mods_t, g4, w1, w3, w2):
    length, bsz, d = x3.shape
    dff = w1.shape[1]
    tt = min(TIME_TILE // 2, length)
    return pl.pallas_call(
        _ffn_t_kernel,
        grid=(length // tt,),
        in_specs=[
            pl.BlockSpec((tt, bsz, d), lambda i: (i, 0, 0)),
            pl.BlockSpec((6, bsz, d), lambda i: (0, 0, 0)),
            pl.BlockSpec((4, d), lambda i: (0, 0)),
            pl.BlockSpec((d, dff), lambda i: (0, 0), pipeline_mode=pl.Buffered(1)),
            pl.BlockSpec((d, dff), lambda i: (0, 0), pipeline_mode=pl.Buffered(1)),
            pl.BlockSpec((dff, d), lambda i: (0, 0), pipeline_mode=pl.Buffered(1)),
        ],
        out_specs=pl.BlockSpec((tt, bsz, d), lambda i: (i, 0, 0)),
        out_shape=jax.ShapeDtypeStruct((length, bsz, d), F32),
        compiler_params=_params(("parallel",)),
        name="dense_swiglu",
    )(x3, mods_t, g4, w1, w3, w2)


def _inproj_cd_kernel(x_ref, mod_ref, g_ref, w_ref, u_ref, v_ref, f_ref):
    x = x_ref[0]
    h = _rms(x, g_ref[0:1, :]) * (1.0 + mod_ref[0, 1:2, :]) + mod_ref[0, 0:1, :]
    z = _bdot(h.astype(BF16), w_ref[...])
    u_ref[0] = jax.nn.gelu(z[:, 0:C_WIDTH])
    v_ref[0] = jax.nn.gelu(z[:, C_WIDTH:2 * C_WIDTH])
    f_ref[0] = z[:, 2 * C_WIDTH:]


def _inproj_cd(x, mods, g4, w):
    bsz, length, d = x.shape
    n = w.shape[1]
    tm = min(ROW_TILE, length)
    return pl.pallas_call(
        _inproj_cd_kernel,
        grid=(bsz, length // tm),
        in_specs=[
            pl.BlockSpec((1, tm, d), lambda b, i: (b, i, 0)),
            pl.BlockSpec((1, 6, d), lambda b, i: (b, 0, 0)),
            pl.BlockSpec((4, d), lambda b, i: (0, 0)),
            pl.BlockSpec((d, n), lambda b, i: (0, 0)),
        ],
        out_specs=[pl.BlockSpec((1, tm, C_WIDTH), lambda b, i: (b, i, 0))] * 3,
        out_shape=[jax.ShapeDtypeStruct((bsz, length, C_WIDTH), F32)] * 3,
        compiler_params=_params(("parallel", "parallel")),
        name="inproj_cd",
    )(x, mods, g4, w)


def _fourier_kernel(f_ref, dmat_ref, lmat_ref, o_ref, z_ref, *, length, scale):
    j = pl.program_id(1)
    rows_per_step = 512 if length % 512 == 0 else length

    @pl.when(j == 0)
    def _():
        def body(c, _):
            rows = pl.ds(pl.multiple_of(c * rows_per_step, rows_per_step), rows_per_step)
            fb = f_ref[0, rows, :].astype(BF16)
            for g in range(D_GROUPS):
                cs = _bdot(fb[:, g * D_GROUP_DIM:(g + 1) * D_GROUP_DIM], dmat_ref[...])
                z_ref[0, rows, g * D_GROUP_DIM:(g + 1) * D_GROUP_DIM] = cs[:, 0:D_GROUP_DIM].astype(BF16)
                z_ref[1, rows, g * D_GROUP_DIM:(g + 1) * D_GROUP_DIM] = cs[:, D_GROUP_DIM:].astype(BF16)
            return 0
        lax.fori_loop(0, length // rows_per_step, body, 0)

    out = _bdot(lmat_ref[0], z_ref[0]) + _bdot(lmat_ref[1], z_ref[1])
    o_ref[0] = (out * scale).astype(BF16)


def _dft_matrices(length):
    k = np.arange(length, dtype=np.int64)
    ang_l = 2.0 * np.pi * ((k[:, None] * k[None, :]) % length).astype(np.float64) / length
    lmat = np.stack([np.cos(ang_l), -np.sin(ang_l)], axis=0)
    d = np.arange(D_GROUP_DIM, dtype=np.int64)
    ang_d = 2.0 * np.pi * ((d[:, None] * d[None, :]) % D_GROUP_DIM).astype(np.float64) / D_GROUP_DIM
    dmat = np.concatenate([np.cos(ang_d), np.sin(ang_d)], axis=1)
    return jnp.asarray(lmat, dtype=F32).astype(BF16), jnp.asarray(dmat, dtype=F32).astype(BF16)


def _fourier(f):
    bsz, length, width = f.shape
    tm = min(ROW_TILE, length)
    lmat, dmat = _dft_matrices(length)
    scale = float(1.0 / np.sqrt(length * D_GROUP_DIM))
    return pl.pallas_call(
        functools.partial(_fourier_kernel, length=length, scale=scale),
        grid=(bsz, length // tm),
        in_specs=[
            pl.BlockSpec((1, length, width), lambda b, i: (b, 0, 0)),
            pl.BlockSpec((D_GROUP_DIM, 2 * D_GROUP_DIM), lambda b, i: (0, 0)),
            pl.BlockSpec((2, tm, length), lambda b, i: (0, i, 0)),
        ],
        out_specs=pl.BlockSpec((1, tm, width), lambda b, i: (b, i, 0)),
        out_shape=jax.ShapeDtypeStruct((bsz, length, width), BF16),
        scratch_shapes=[pltpu.VMEM((2, length, width), BF16)],
        compiler_params=_params(("parallel", "arbitrary")),
        name="fourier_mix",
    )(f, dmat, lmat)


def _gate_out_kernel(u_ref, v_ref, fo_ref, x_ref, lng_ref, lnb_ref, wcat_ref, bs_ref, wout_ref, g_ref, mod_ref,
                     o_ref, vs_ref, gt_ref, *, tm):
    head_dim = C_WIDTH // C_HEADS
    head_of_lane = lax.broadcasted_iota(jnp.int32, (CHUNK, C_WIDTH), 1) // head_dim
    for ci in range(tm // CHUNK):
        rows = slice(ci * CHUNK, (ci + 1) * CHUNK)
        v = v_ref[0, rows, :]
        mu = jnp.mean(v, axis=-1, keepdims=True)
        cen = v - mu
        var = jnp.mean(cen * cen, axis=-1, keepdims=True)
        vn = (cen * lax.rsqrt(var + LN_EPS) * lng_ref[...] + lnb_ref[...]).astype(BF16)
        for h in range(C_HEADS):
            vs_ref[h * CHUNK:(h + 1) * CHUNK, :] = jnp.where(head_of_lane == h, vn, jnp.zeros_like(vn))
        mixed = _bdot(wcat_ref[...], vs_ref[...]) + bs_ref[...]
        gt_ref[rows, :] = (u_ref[0, rows, :] * mixed).astype(BF16)
    y = _bdot(gt_ref[...], wout_ref[0:C_WIDTH, :]) + _bdot(fo_ref[0], wout_ref[C_WIDTH:, :])
    o_ref[0] = x_ref[0] + mod_ref[0, 2:3, :] * _rms(y, g_ref[1:2, :])


def _gate_out(u, v, fo, x, ln_g, ln_b, w_s, b_s, w_out, g4, mods):
    bsz, length, d = x.shape
    tm = min(ROW_TILE, length)
    head_dim = C_WIDTH // C_HEADS
    wcat = w_s.transpose(1, 0, 2).reshape(CHUNK, C_HEADS * CHUNK).astype(BF16)
    bs_full = jnp.repeat(b_s.T, head_dim, axis=1)
    const = lambda b, i: (0, 0)
    return pl.pallas_call(
        functools.partial(_gate_out_kernel, tm=tm),
        grid=(bsz, length // tm),
        in_specs=[
            pl.BlockSpec((1, tm, C_WIDTH), lambda b, i: (b, i, 0)),
            pl.BlockSpec((1, tm, C_WIDTH), lambda b, i: (b, i, 0)),
            pl.BlockSpec((1, tm, D_WIDTH), lambda b, i: (b, i, 0)),
            pl.BlockSpec((1, tm, d), lambda b, i: (b, i, 0)),
            pl.BlockSpec((1, C_WIDTH), const),
            pl.BlockSpec((1, C_WIDTH), const),
            pl.BlockSpec((CHUNK, C_HEADS * CHUNK), const),
            pl.BlockSpec((CHUNK, C_WIDTH), const),
            pl.BlockSpec((C_WIDTH + D_WIDTH, d), const),
            pl.BlockSpec((4, d), const),
            pl.BlockSpec((1, 6, d), lambda b, i: (b, 0, 0)),
        ],
        out_specs=pl.BlockSpec((1, tm, d), lambda b, i: (b, i, 0)),
        out_shape=jax.ShapeDtypeStruct((bsz, length, d), F32),
        scratch_shapes=[
            pltpu.VMEM((C_HEADS * CHUNK, C_WIDTH), BF16),
            pltpu.VMEM((tm, C_WIDTH), BF16),
        ],
        compiler_params=_params(("parallel", "parallel")),
        name="gating_outproj",
    )(u, v, fo, x, ln_g.reshape(1, -1), ln_b.reshape(1, -1), wcat, bs_full, w_out, g4, mods)


def _router_kernel(x_ref, mod_ref, g_ref, wr_ref, tri_ref, xn_ref, meta_ref, cnt_ref, carry_ref, *, tm):
    first = jnp.logical_and(pl.program_id(0) == 0, pl.program_id(1) == 0)

    @pl.when(first)
    def _():
        carry_ref[...] = jnp.zeros_like(carry_ref)

    x = x_ref[0]
    h = _rms(x, g_ref[2:3, :]) * (1.0 + mod_ref[0, 4:5, :]) + mod_ref[0, 3:4, :]
    half = h.shape[1] // 2
    hi = pltpu.bitcast(h[:, :half].astype(BF16).astype(F32), jnp.int32)
    lo = pltpu.bitcast(h[:, half:].astype(BF16).astype(F32), jnp.int32)
    xn_ref[0] = hi | lax.shift_right_logical(lo, 16)
    logits = jnp.dot(h, wr_ref[...], preferred_element_type=F32, precision=lax.Precision.HIGHEST)
    lane = lax.broadcasted_iota(jnp.int32, (tm, LANES), 1)
    neg = jnp.float32(-jnp.inf)
    lm = jnp.where(lane < N_EXPERTS, logits, neg)
    m1 = jnp.max(lm, axis=-1, keepdims=True)
    i1 = jnp.min(jnp.where(lm == m1, lane, LANES), axis=-1, keepdims=True)
    lm2 = jnp.where(lane == i1, neg, lm)
    m2 = jnp.max(lm2, axis=-1, keepdims=True)
    i2 = jnp.min(jnp.where(lm2 == m2, lane, LANES), axis=-1, keepdims=True)
    e = jnp.exp(m2 - m1)
    p1 = 1.0 / (1.0 + e)
    p2 = e / (1.0 + e)
    hot1 = (lane == i1).astype(F32)
    hot2 = (lane == i2).astype(F32)
    both = hot1 + hot2
    before = _bdot(tri_ref[...], both.astype(BF16)) + carry_ref[0:1, :]
    r1 = jnp.sum(before * hot1, axis=-1, keepdims=True)
    r2 = jnp.sum(before * hot2, axis=-1, keepdims=True)
    carry_ref[0:1, :] = carry_ref[0:1, :] + jnp.sum(both, axis=0, keepdims=True)
    cnt_ref[...] = carry_ref[...]
    vals = (i1.astype(F32), i2.astype(F32), p1, p2, r1, r2)
    meta = jnp.zeros((tm, LANES), F32)
    for k, val in enumerate(vals):
        meta = jnp.where(lane == k, val, meta)
    meta_ref[0] = meta


def _router(x, mods, g4, w_router):
    bsz, length, d = x.shape
    tm = min(ROW_TILE, length)
    wr = jnp.concatenate([w_router, jnp.zeros((d, LANES - N_EXPERTS), F32)], axis=1)
    tri = jnp.asarray(np.tril(np.ones((tm, tm), np.float32), -1), dtype=BF16)
    const = lambda b, i: (0, 0)
    return pl.pallas_call(
        functools.partial(_router_kernel, tm=tm),
        grid=(bsz, length // tm),
        in_specs=[
            pl.BlockSpec((1, tm, d), lambda b, i: (b, i, 0)),
            pl.BlockSpec((1, 6, d), lambda b, i: (b, 0, 0)),
            pl.BlockSpec((4, d), const),
            pl.BlockSpec((d, LANES), const),
            pl.BlockSpec((tm, tm), const),
        ],
        out_specs=[
            pl.BlockSpec((1, tm, d // 2), lambda b, i: (b, i, 0)),
            pl.BlockSpec((1, tm, LANES), lambda b, i: (b, i, 0)),
            pl.BlockSpec((8, LANES), const),
        ],
        out_shape=[
            jax.ShapeDtypeStruct((bsz, length, d // 2), jnp.int32),
            jax.ShapeDtypeStruct((bsz, length, LANES), F32),
            jax.ShapeDtypeStruct((8, LANES), F32),
        ],
        scratch_shapes=[pltpu.VMEM((8, LANES), F32)],
        compiler_params=_params(("arbitrary", "arbitrary")),
        name="moe_router",
    )(x, mods, g4, wr, tri)


def _expert_kernel(te_ref, nv_ref, x_ref, w1_ref, w3_ref, w2_ref, o_ref, acc_ref):
    i = pl.program_id(0)
    j = pl.program_id(1)

    @pl.when(i < nv_ref[0])
    def _():
        packed = x_ref[...]
        x = jnp.concatenate(
            [pltpu.bitcast(packed & jnp.int32(-65536), F32).astype(BF16),
             pltpu.bitcast(packed << 16, F32).astype(BF16)], axis=1)
        a = _bdot(x, w1_ref[0])
        b = _bdot(x, w3_ref[0])
        t = (jax.nn.silu(a) * b).astype(BF16)
        y = _bdot(t, w2_ref[0])

        @pl.when(j == 0)
        def _():
            acc_ref[...] = y

        @pl.when(j > 0)
        def _():
            acc_ref[...] += y

        @pl.when(j == pl.num_programs(1) - 1)
        def _():
            o_ref[...] = acc_ref[...]


def _experts(x_sorted, tile_expert, n_valid, w1, w3, w2):
    rows = x_sorted.shape[0]
    d = w1.shape[1]
    dff = w1.shape[2]
    tm = EXPERT_ROW_TILE
    n_ff = 2
    tf = dff // n_ff

    def row_map(i, j, te, nv):
        return (jnp.maximum(jnp.minimum(i, nv[0] - 1), 0), 0)

    def w13_map(i, j, te, nv):
        return (te[jnp.maximum(jnp.minimum(i, nv[0] - 1), 0)], 0, jnp.where(i < nv[0], j, n_ff - 1))

    def w2_map(i, j, te, nv):
        return (te[jnp.maximum(jnp.minimum(i, nv[0] - 1), 0)], jnp.where(i < nv[0], j, n_ff - 1), 0)

    return pl.pallas_call(
        _expert_kernel,
        grid_spec=pltpu.PrefetchScalarGridSpec(
            num_scalar_prefetch=2,
            grid=(rows // tm, n_ff),
            in_specs=[
                pl.BlockSpec((tm, d // 2), row_map),
                pl.BlockSpec((1, d, tf), w13_map),
                pl.BlockSpec((1, d, tf), w13_map),
                pl.BlockSpec((1, tf, d), w2_map),
            ],
            out_specs=pl.BlockSpec((tm, d), row_map),
            scratch_shapes=[pltpu.VMEM((tm, d), F32)],
        ),
        out_shape=jax.ShapeDtypeStruct((rows, d), F32),
        compiler_params=_params(("arbitrary", "arbitrary")),
        name="expert_swiglu",
    )(tile_expert, n_valid, x_sorted, w1, w3, w2)


def _combine_kernel(x_ref, y_ref, meta_ref, g_ref, mod_ref, o_ref):
    meta = meta_ref[0]
    y = meta[:, 2:3] * y_ref[0, 0] + meta[:, 3:4] * y_ref[1, 0]
    o_ref[0] = x_ref[0] + mod_ref[0, 5:6, :] * _rms(y, g_ref[3:4, :])


def _combine(x, y2, meta, g4, mods):
    bsz, length, d = x.shape
    tm = min(ROW_TILE, length)
    return pl.pallas_call(
        _combine_kernel,
        grid=(bsz, length // tm),
        in_specs=[
            pl.BlockSpec((1, tm, d), lambda b, i: (b, i, 0)),
            pl.BlockSpec((2, 1, tm, d), lambda b, i: (0, b, i, 0)),
            pl.BlockSpec((1, tm, LANES), lambda b, i: (b, i, 0)),
            pl.BlockSpec((4, d), lambda b, i: (0, 0)),
            pl.BlockSpec((1, 6, d), lambda b, i: (b, 0, 0)),
        ],
        out_specs=pl.BlockSpec((1, tm, d), lambda b, i: (b, i, 0)),
        out_shape=jax.ShapeDtypeStruct((bsz, length, d), F32),
        compiler_params=_params(("parallel", "parallel")),
        name="moe_combine",
    )(x, y2, meta, g4, mods)


SC_CORES = 2
SC_SUBCORES = 16
SC_WORKERS = SC_CORES * SC_SUBCORES
SC_INDEX_CHUNK = 128


def _sc_mesh():
    return plsc.VectorSubcoreMesh(core_axis_name="c", subcore_axis_name="s")


def _sc_dispatch(rows_in, pos2, n_out):
    tokens, width = rows_in.shape
    per_worker = tokens // SC_WORKERS
    chunk = SC_INDEX_CHUNK
    n_chunks = per_worker // chunk
    idx = pos2.reshape(2, SC_WORKERS, n_chunks, chunk)

    @functools.partial(
        pl.kernel, mesh=_sc_mesh(),
        out_type=jax.ShapeDtypeStruct((n_out, width), rows_in.dtype),
        scratch_types=[
            pltpu.VMEM((n_chunks, chunk), jnp.int32),
            pltpu.VMEM((n_chunks, chunk), jnp.int32),
            pltpu.VMEM((chunk, width), rows_in.dtype),
        ],
        name="sc_dispatch",
    )
    def body(rows_hbm, idx_hbm, out_hbm, idx0_v, idx1_v, rows_v):
        wid = lax.axis_index("s") * SC_CORES + lax.axis_index("c")
        base = wid * per_worker
        pltpu.sync_copy(idx_hbm.at[0, wid], idx0_v)
        pltpu.sync_copy(idx_hbm.at[1, wid], idx1_v)

        @pl.loop(0, n_chunks)
        def _(c):
            pltpu.sync_copy(rows_hbm.at[pl.ds(base + c * chunk, chunk)], rows_v)
            pltpu.sync_copy(rows_v, out_hbm.at[idx0_v.at[c]])
            pltpu.sync_copy(rows_v, out_hbm.at[idx1_v.at[c]])

    return body(rows_in, idx)


def _sc_gather(table, idx, chunk):
    n = idx.shape[0]
    width = table.shape[1]
    per_worker = n // SC_WORKERS
    n_chunks = per_worker // chunk
    idx3 = idx.reshape(SC_WORKERS, n_chunks, chunk)

    @functools.partial(
        pl.kernel, mesh=_sc_mesh(),
        out_type=jax.ShapeDtypeStruct((n, width), table.dtype),
        scratch_types=[
            pltpu.VMEM((n_chunks, chunk), jnp.int32),
            pltpu.VMEM((chunk, width), table.dtype),
        ],
        name="sc_gather",
    )
    def body(table_hbm, idx_hbm, out_hbm, idx_v, rows_v):
        wid = lax.axis_index("s") * SC_CORES + lax.axis_index("c")
        base = wid * per_worker
        pltpu.sync_copy(idx_hbm.at[wid], idx_v)

        @pl.loop(0, n_chunks)
        def _(c):
            pltpu.sync_copy(table_hbm.at[idx_v.at[c]], rows_v)
            pltpu.sync_copy(rows_v, out_hbm.at[pl.ds(base + c * chunk, chunk)])

    return body(table, idx3)


def _moe(x, mods, g4, w_router, w1, w3, w2):
    bsz, length, d = x.shape
    tokens = bsz * length
    tm = EXPERT_ROW_TILE
    xn, meta, counts = _router(x, mods, g4, w_router)
    meta2 = meta.reshape(tokens, LANES)
    experts = meta2[:, 0:2].astype(jnp.int32)
    ranks = meta2[:, 4:6].astype(jnp.int32)
    cnt = counts[0, 0:N_EXPERTS].astype(jnp.int32)
    padded = ((cnt + tm - 1) // tm) * tm
    ends = jnp.cumsum(padded)
    starts = ends - padded
    pos = (starts[experts] + ranks).T.reshape(2 * tokens)
    rows = 2 * tokens + N_EXPERTS * tm
    n_tiles = rows // tm
    n_valid = (ends[-1] // tm).astype(jnp.int32).reshape(1)
    tile_start = jnp.arange(n_tiles, dtype=jnp.int32) * tm
    tile_expert = jnp.minimum(
        jnp.sum((tile_start[:, None] >= ends[None, :]).astype(jnp.int32), axis=1), N_EXPERTS - 1)
    x_sorted = _sc_dispatch(xn.reshape(tokens, d // 2), pos.reshape(2, tokens), rows)
    y_sorted = _experts(x_sorted, tile_expert, n_valid, w1, w3, w2)
    y2 = _sc_gather(y_sorted, pos, 32).reshape(2, bsz, length, d)
    return _combine(x, y2, meta, g4, mods)


def _grid_pos_embed(rows, d):
    row = jnp.repeat(jnp.arange(rows, dtype=F32), GRID_W)
    col = jnp.tile(jnp.arange(GRID_W, dtype=F32), rows)
    n_freq = d // 4
    omega = POS_THETA ** (-jnp.arange(n_freq, dtype=F32) / n_freq)
    ang_r = row[:, None] * omega
    ang_c = col[:, None] * omega
    return jnp.concatenate([jnp.sin(ang_r), jnp.cos(ang_r), jnp.sin(ang_c), jnp.cos(ang_c)], axis=-1)


def kernel(x, c, ctx, c_ctx, ada_w, ada_b, norm_g, ab_w_in, rg_conv_w, rg_conv_b, rg_w_r, rg_b_r, rg_w_i, rg_b_i, rg_lambda, cv_dw_w, cv_dw_b, cv_ln_g, cv_ln_b, ab_w_out, ffn_w1, ffn_w3, ffn_w2, cd_w_in, sg_ln_g, sg_ln_b, sg_w_s, sg_b_s, cd_w_out, moe_router, moe_w1, moe_w3, moe_w2):
    bsz, length, d = x.shape
    assert ada_w.shape[0] == 2, "two layers: one even (RG-LRU | Conformer), one odd (gMLP | Fourier)"
    pos = _grid_pos_embed(length // GRID_W, d)

    cpad = jnp.concatenate([c, c_ctx[None, :], jnp.zeros((16 - bsz - 1, d), F32)], axis=0)
    m = _ada(cpad, ada_w, ada_b)
    mods = m[:, :bsz].reshape(2, bsz, 6, d)
    mods_ctx = m[:, bsz:bsz + 1].reshape(2, 1, 6, d)

    mods_t = mods[0].transpose(1, 0, 2)
    mods_ctx_t = jnp.broadcast_to(mods_ctx[0].reshape(6, 1, d), (6, bsz, d))
    x_t = x.transpose(1, 0, 2)
    ctx_t = ctx.transpose(1, 0, 2)
    pos3 = pos[:, None, :]
    w_in = ab_w_in[0].astype(BF16)
    xa3, gg3, glu3 = _inproj_t(x_t, pos3, mods_t, norm_g[0], w_in, rg_conv_w[0], rg_conv_b[0], True)
    (xc3,) = _inproj_t(ctx_t, None, mods_ctx_t, norm_g[0], w_in[:, 0:A_WIDTH], rg_conv_w[0], rg_conv_b[0], False)
    wg, bg, sp = _gate_weights(rg_w_r[0], rg_b_r[0], rg_w_i[0], rg_b_i[0], rg_lambda[0])
    rec_hi, rec_lo = _scan_t(xa3, xc3, gg3, wg, bg, sp)
    u3 = _conv_t(glu3, cv_dw_w[0], cv_dw_b[0], cv_ln_g[0], cv_ln_b[0])
    x1_t = _outproj_t(rec_lo, rec_hi, u3, x_t, pos3, ab_w_out[0].astype(BF16), norm_g[0], mods_t)
    x2_t = _ffn_t(x1_t, mods_t, norm_g[0], ffn_w1[0].astype(BF16), ffn_w3[0].astype(BF16), ffn_w2[0].astype(BF16))
    x2 = x2_t.transpose(1, 0, 2)

    u, v, f = _inproj_cd(x2, mods[1], norm_g[1], cd_w_in[0].astype(BF16))
    fo = _fourier(f)
    x3 = _gate_out(u, v, fo, x2, sg_ln_g[0], sg_ln_b[0], sg_w_s[0], sg_b_s[0], cd_w_out[0].astype(BF16),
                   norm_g[1], mods[1])
    return _moe(x3, mods[1], norm_g[1], moe_router[0], moe_w1[0].astype(BF16), moe_w3[0].astype(BF16),
                moe_w2[0].astype(BF16))
```

```python
import functools

import numpy as np
import jax
import jax.numpy as jnp
from jax import lax
from jax.experimental import pallas as pl
from jax.experimental.pallas import tpu as pltpu
from jax.experimental.pallas import tpu_sc as plsc

F32 = jnp.float32
BF16 = jnp.bfloat16

GRID_W = 64
POS_THETA = 10000.0
NORM_EPS = 1e-6
LN_EPS = 1e-5
LRU_C = 8.0

A_WIDTH = 512
A_HEADS = 8
A_CONV_W = 4
B_WIDTH = 512
B_CONV_W = 31
C_WIDTH = 512
C_HEADS = 8
CHUNK = 128
D_WIDTH = 512
D_GROUPS = 4
D_GROUP_DIM = D_WIDTH // D_GROUPS
N_EXPERTS = 8

LANES = 128
ROW_TILE = 512
EXPERT_ROW_TILE = 512
VMEM_LIMIT = 56 * 2 ** 20


def _params(sem):
    return pltpu.CompilerParams(dimension_semantics=sem, vmem_limit_bytes=VMEM_LIMIT)


def _rms(x, g):
    return x * lax.rsqrt(jnp.mean(x * x, axis=-1, keepdims=True) + NORM_EPS) * g


def _bdot(a, b):
    return jnp.dot(a, b, preferred_element_type=F32)


def _ada_kernel(c_ref, w_ref, b_ref, o_ref):
    s = jax.nn.silu(c_ref[...])
    o_ref[0] = _bdot(s.astype(BF16), w_ref[0].astype(BF16)) + b_ref[0]


def _ada(cpad, ada_w, ada_b):
    nl, d, n6 = ada_w.shape
    rows = cpad.shape[0]
    tn = n6 // 4
    return pl.pallas_call(
        _ada_kernel,
        grid=(nl, n6 // tn),
        in_specs=[
            pl.BlockSpec((rows, d), lambda l, j: (0, 0)),
            pl.BlockSpec((1, d, tn), lambda l, j: (l, 0, j)),
            pl.BlockSpec((1, 1, tn), lambda l, j: (l, 0, j)),
        ],
        out_specs=pl.BlockSpec((1, rows, tn), lambda l, j: (l, 0, j)),
        out_shape=jax.ShapeDtypeStruct((nl, rows, n6), F32),
        compiler_params=_params(("arbitrary", "arbitrary")),
        name="ada_terms",
    )(cpad, ada_w, ada_b.reshape(nl, 1, n6))


TIME_TILE = 128
SCAN_TIME_TILE = 128
SCAN_LANES = 256
CONV_HALO = 16


def _inproj_t_kernel(*refs, tt, add_pos, branches):
    if add_pos:
        x_ref, xp_ref, xn_ref, pos_ref, posp_ref, posn_ref, mod_ref, g_ref, w_ref, cw_ref, cb_ref, *outs = refs
        xall = jnp.concatenate(
            [xp_ref[...] + posp_ref[...], x_ref[...] + pos_ref[...], xn_ref[...] + posn_ref[...]], axis=0)
    else:
        x_ref, xp_ref, xn_ref, mod_ref, g_ref, w_ref, cw_ref, cb_ref, *outs = refs
        xall = jnp.concatenate([xp_ref[...], x_ref[...], xn_ref[...]], axis=0)
    i = pl.program_id(0)
    last = pl.num_programs(0) - 1
    steps, bsz, d = xall.shape
    h = _rms(xall, g_ref[0:1, :]) * (1.0 + mod_ref[1]) + mod_ref[0]
    z = _bdot(h.reshape(steps * bsz, d).astype(BF16), w_ref[...])
    z3 = z.reshape(steps, bsz, z.shape[-1])
    t_idx = lax.broadcasted_iota(jnp.int32, (steps, bsz, A_WIDTH), 0)
    inside = jnp.logical_and(jnp.logical_or(t_idx >= 1, i > 0), jnp.logical_or(t_idx <= tt, i < last))
    xa = jnp.where(inside, z3[:, :, 0:A_WIDTH], 0.0)
    xconv = cb_ref[...] + cw_ref[0:1, :] * xa[0:tt]
    for k in range(1, A_CONV_W):
        xconv = xconv + cw_ref[k:k + 1, :] * xa[k:k + tt]
    outs[0][...] = xconv
    if branches:
        outs[1][...] = jax.nn.gelu(z3[1:tt + 1, :, A_WIDTH:2 * A_WIDTH])
        vb = z3[1:tt + 1, :, 2 * A_WIDTH:2 * A_WIDTH + B_WIDTH]
        gb = z3[1:tt + 1, :, 2 * A_WIDTH + B_WIDTH:]
        outs[2][...] = vb * jax.nn.sigmoid(gb)


def _inproj_t(x3, pos3, mods_t, g4, w, conv_w, conv_b, branches):
    length, bsz, d = x3.shape
    n = w.shape[1]
    tt = min(TIME_TILE, length)
    add_pos = pos3 is not None
    n_out = 3 if branches else 1
    prev_map = lambda i: (jnp.maximum(i * tt - 1, 0), 0, 0)
    next_map = lambda i: (jnp.minimum((i + 1) * (tt // 2), length // 2 - 1), 0, 0)
    in_specs = [
        pl.BlockSpec((tt, bsz, d), lambda i: (i, 0, 0)),
        pl.BlockSpec((1, bsz, d), prev_map),
        pl.BlockSpec((2, bsz, d), next_map),
    ]
    args = [x3, x3, x3]
    if add_pos:
        in_specs += [
            pl.BlockSpec((tt, 1, d), lambda i: (i, 0, 0)),
            pl.BlockSpec((1, 1, d), prev_map),
            pl.BlockSpec((2, 1, d), next_map),
        ]
        args += [pos3, pos3, pos3]
    in_specs += [
        pl.BlockSpec((6, bsz, d), lambda i: (0, 0, 0)),
        pl.BlockSpec((4, d), lambda i: (0, 0)),
        pl.BlockSpec((d, n), lambda i: (0, 0)),
        pl.BlockSpec((A_CONV_W, A_WIDTH), lambda i: (0, 0)),
        pl.BlockSpec((1, A_WIDTH), lambda i: (0, 0)),
    ]
    args += [mods_t, g4, w, conv_w, conv_b.reshape(1, A_WIDTH)]
    return pl.pallas_call(
        functools.partial(_inproj_t_kernel, tt=tt, add_pos=add_pos, branches=branches),
        grid=(length // tt,),
        in_specs=in_specs,
        out_specs=[pl.BlockSpec((tt, bsz, A_WIDTH), lambda i: (i, 0, 0))] * n_out,
        out_shape=[jax.ShapeDtypeStruct((length, bsz, A_WIDTH), F32)] * n_out,
        compiler_params=_params(("parallel",)),
        name="inproj_ab" if branches else "inproj_ctx",
    )(*args)


def _scan_t_kernel(xf_ref, xb_ref, ggf_ref, ggb_ref, xc_ref, wg_ref, bg_ref, sp_ref, hi_ref, lo_ref,
                   a_ref, b_ref, h_ref, state_ref, *, tb, nb, ctx_len):
    k = pl.program_id(1)
    half = nb // 2
    bsz, lanes = xf_ref.shape[1], xf_ref.shape[2]
    chunk = 32

    def coefficients(x_ref, n, direction):
        def body(c, _):
            t0 = pl.multiple_of(c * chunk, chunk)
            x = x_ref[pl.ds(t0, chunk)].reshape(chunk * bsz, lanes)
            g = _bdot(x.astype(BF16), wg_ref[0, direction]) + bg_ref[0, direction]
            gate = 0.5 * jnp.tanh(0.5 * g) + 0.5
            r = gate[:, 0:lanes]
            i = gate[:, lanes:]
            log_a = (-LRU_C * r) * sp_ref[0, direction:direction + 1, :]
            a = jnp.exp(log_a)
            one_minus_a2 = -jnp.tanh(log_a) * (a * a + 1.0)
            a_ref[direction, pl.ds(t0, chunk)] = a.reshape(chunk, bsz, lanes)
            b_ref[direction, pl.ds(t0, chunk)] = (jnp.sqrt(one_minus_a2) * (i * x)).reshape(chunk, bsz, lanes)
            return 0
        lax.fori_loop(0, n // chunk, body, 0)

    def sweep(n, store):
        def body(t, carry):
            hf, hb = carry
            tr = n - 1 - t
            hf = a_ref[0, pl.ds(t, 1)][0] * hf + b_ref[0, pl.ds(t, 1)][0]
            hb = a_ref[1, pl.ds(tr, 1)][0] * hb + b_ref[1, pl.ds(tr, 1)][0]
            if store:
                b_ref[0, pl.ds(t, 1)] = hf[None]
                b_ref[1, pl.ds(tr, 1)] = hb[None]
            return hf, hb
        hf, hb = lax.fori_loop(0, n, body, (state_ref[0], state_ref[1]), unroll=8)
        state_ref[0] = hf
        state_ref[1] = hb

    @pl.when(k == 0)
    def _():
        state_ref[...] = jnp.zeros_like(state_ref)
        coefficients(xc_ref, ctx_len, 0)
        coefficients(xc_ref, ctx_len, 1)
        sweep(ctx_len, False)

    coefficients(xf_ref, tb, 0)
    coefficients(xb_ref, tb, 1)
    sweep(tb, True)
    m = nb - 1 - k

    @pl.when(k < half)
    def _():
        h_ref[k] = b_ref[0, 0:tb]
        h_ref[m] = b_ref[1, 0:tb]

    @pl.when(k >= half)
    def _():
        hi_ref[...] = (b_ref[0, 0:tb] + h_ref[k]) * ggf_ref[...]
        lo_ref[...] = (h_ref[m] + b_ref[1, 0:tb]) * ggb_ref[...]


def _scan_t(xa3, xc3, gg3, wg, bg, sp):
    length, bsz, width = xa3.shape
    ctx_len = xc3.shape[0]
    tb = min(SCAN_TIME_TILE, length // 2)
    nb = length // tb
    half = nb // 2
    lanes = SCAN_LANES
    blk = (tb, bsz, lanes)
    return pl.pallas_call(
        functools.partial(_scan_t_kernel, tb=tb, nb=nb, ctx_len=ctx_len),
        grid=(width // lanes, nb),
        in_specs=[
            pl.BlockSpec(blk, lambda g, k: (k, 0, g)),
            pl.BlockSpec(blk, lambda g, k: (nb - 1 - k, 0, g)),
            pl.BlockSpec(blk, lambda g, k: (jnp.maximum(k, half), 0, g)),
            pl.BlockSpec(blk, lambda g, k: (jnp.minimum(nb - 1 - k, half - 1), 0, g)),
            pl.BlockSpec((ctx_len, bsz, lanes), lambda g, k: (0, 0, g)),
            pl.BlockSpec((1, 2, lanes, 2 * lanes), lambda g, k: (g, 0, 0, 0)),
            pl.BlockSpec((1, 2, 1, 2 * lanes), lambda g, k: (g, 0, 0, 0)),
            pl.BlockSpec((1, 2, lanes), lambda g, k: (g, 0, 0)),
        ],
        out_specs=[
            pl.BlockSpec(blk, lambda g, k: (jnp.maximum(k - half, 0), 0, g)),
            pl.BlockSpec(blk, lambda g, k: (jnp.minimum(nb - 1 - k, half - 1), 0, g)),
        ],
        out_shape=[jax.ShapeDtypeStruct((length // 2, bsz, width), F32)] * 2,
        scratch_shapes=[
            pltpu.VMEM((2, max(tb, ctx_len), bsz, lanes), F32),
            pltpu.VMEM((2, max(tb, ctx_len), bsz, lanes), F32),
            pltpu.VMEM((nb, tb, bsz, lanes), F32),
            pltpu.VMEM((2, bsz, lanes), F32),
        ],
        compiler_params=_params(("parallel", "arbitrary")),
        name="rglru_scan",
    )(xa3, xa3, gg3, gg3, xc3, wg, bg, sp)


def _gate_weights(w_r, b_r, w_i, b_i, lam):
    hd = A_WIDTH // A_HEADS
    heads_per_group = SCAN_LANES // hd
    ngroups = A_WIDTH // SCAN_LANES

    def blockdiag(w):
        w = w.reshape(ngroups, heads_per_group, hd, hd)
        eye = jnp.eye(heads_per_group, dtype=w.dtype)
        return jnp.einsum("ghde,hk->ghdke", w, eye).reshape(ngroups, SCAN_LANES, SCAN_LANES)

    wg = jnp.stack([jnp.concatenate([blockdiag(w_r[d]), blockdiag(w_i[d])], axis=-1) for d in range(2)], axis=1)
    bg = jnp.stack([jnp.concatenate([b_r[d].reshape(ngroups, 1, SCAN_LANES), b_i[d].reshape(ngroups, 1, SCAN_LANES)],
                                    axis=-1) for d in range(2)], axis=1)
    sp = jax.nn.softplus(-lam.astype(F32)).reshape(2, ngroups, SCAN_LANES).transpose(1, 0, 2)
    return wg.astype(BF16), bg, sp


def _conv_t_kernel(x_ref, xp_ref, xn_ref, w_ref, b_ref, lng_ref, lnb_ref, o_ref, stage_ref, *, tb):
    i = pl.program_id(0)
    last = pl.num_programs(0) - 1
    halo = CONV_HALO
    stage_ref[0:halo] = jnp.where(i > 0, xp_ref[...], 0.0)
    stage_ref[halo:halo + tb] = x_ref[...]
    stage_ref[halo + tb:2 * halo + tb] = jnp.where(i < last, xn_ref[...], 0.0)
    sub = 8

    def body(r, _):
        t0 = pl.multiple_of(r * sub, sub)
        def tap(k, acc):
            return acc + w_ref[k] * stage_ref[pl.ds(t0 + 1 + k, sub)]
        acc = lax.fori_loop(0, B_CONV_W + 1, tap, jnp.broadcast_to(b_ref[...], (sub,) + b_ref.shape), unroll=8)
        o_ref[pl.ds(t0, sub)] = acc
        return 0
    lax.fori_loop(0, tb // sub, body, 0)

    norm_rows = 16

    def norm_body(r, _):
        t0 = pl.multiple_of(r * norm_rows, norm_rows)
        acc = o_ref[pl.ds(t0, norm_rows)]
        mu = jnp.mean(acc, axis=-1, keepdims=True)
        cen = acc - mu
        var = jnp.mean(cen * cen, axis=-1, keepdims=True)
        y = cen * lax.rsqrt(var + LN_EPS) * lng_ref[...] + lnb_ref[...]
        o_ref[pl.ds(t0, norm_rows)] = jax.nn.silu(y)
        return 0
    lax.fori_loop(0, tb // norm_rows, norm_body, 0)


def _conv_t(glu3, dw_w, dw_b, ln_g, ln_b):
    length, bsz, width = glu3.shape
    tb = min(TIME_TILE, length)
    halo = CONV_HALO
    taps = dw_w.shape[0]
    w8 = jnp.broadcast_to(jnp.concatenate([dw_w, jnp.zeros((32 - taps, width), F32)], axis=0)[:, None, :],
                          (32, bsz, width))
    b8 = jnp.broadcast_to(dw_b[None, :], (bsz, width))
    return pl.pallas_call(
        functools.partial(_conv_t_kernel, tb=tb),
        grid=(length // tb,),
        in_specs=[
            pl.BlockSpec((tb, bsz, width), lambda i: (i, 0, 0)),
            pl.BlockSpec((halo, bsz, width), lambda i: (jnp.maximum(i * (tb // halo) - 1, 0), 0, 0)),
            pl.BlockSpec((halo, bsz, width), lambda i: (jnp.minimum((i + 1) * (tb // halo), length // halo - 1), 0, 0)),
            pl.BlockSpec((32, bsz, width), lambda i: (0, 0, 0)),
            pl.BlockSpec((bsz, width), lambda i: (0, 0)),
            pl.BlockSpec((1, width), lambda i: (0, 0)),
            pl.BlockSpec((1, width), lambda i: (0, 0)),
        ],
        out_specs=pl.BlockSpec((tb, bsz, width), lambda i: (i, 0, 0)),
        out_shape=jax.ShapeDtypeStruct((length, bsz, width), F32),
        scratch_shapes=[pltpu.VMEM((tb + 2 * halo, bsz, width), F32)],
        compiler_params=_params(("parallel",)),
        name="conformer_conv",
    )(glu3, glu3, glu3, w8, b8, ln_g.reshape(1, -1), ln_b.reshape(1, -1))


def _outproj_t_kernel(lo_ref, hi_ref, u_ref, x_ref, pos_ref, wout_ref, g_ref, mod_ref, o_ref, *, half_steps):
    i = pl.program_id(0)
    tt, bsz, d = x_ref.shape
    rec = jnp.where(i < half_steps, lo_ref[...], hi_ref[...])
    rows = tt * bsz
    y = (_bdot(rec.reshape(rows, A_WIDTH).astype(BF16), wout_ref[0:A_WIDTH, :])
         + _bdot(u_ref[...].reshape(rows, B_WIDTH).astype(BF16), wout_ref[A_WIDTH:, :]))
    yn = _rms(y, g_ref[1:2, :]).reshape(tt, bsz, d)
    o_ref[...] = x_ref[...] + pos_ref[...] + mod_ref[2] * yn


def _outproj_t(rec_lo, rec_hi, u3, x3, pos3, w_out, g4, mods_t):
    length, bsz, d = x3.shape
    tt = min(TIME_TILE // 2, length // 2)
    half_steps = (length // 2) // tt
    return pl.pallas_call(
        functools.partial(_outproj_t_kernel, half_steps=half_steps),
        grid=(length // tt,),
        in_specs=[
            pl.BlockSpec((tt, bsz, A_WIDTH), lambda i: (jnp.minimum(i, half_steps - 1), 0, 0)),
            pl.BlockSpec((tt, bsz, A_WIDTH), lambda i: (jnp.maximum(i - half_steps, 0), 0, 0)),
            pl.BlockSpec((tt, bsz, B_WIDTH), lambda i: (i, 0, 0)),
            pl.BlockSpec((tt, bsz, d), lambda i: (i, 0, 0)),
            pl.BlockSpec((tt, 1, d), lambda i: (i, 0, 0)),
            pl.BlockSpec((A_WIDTH + B_WIDTH, d), lambda i: (0, 0)),
            pl.BlockSpec((4, d), lambda i: (0, 0)),
            pl.BlockSpec((6, bsz, d), lambda i: (0, 0, 0)),
        ],
        out_specs=pl.BlockSpec((tt, bsz, d), lambda i: (i, 0, 0)),
        out_shape=jax.ShapeDtypeStruct((length, bsz, d), F32),
        compiler_params=_params(("parallel",)),
        name="mixer_ab_outproj",
    )(rec_lo, rec_hi, u3, x3, pos3, w_out, g4, mods_t)


def _ffn_t_kernel(x_ref, mod_ref, g_ref, w1_ref, w3_ref, w2_ref, o_ref):
    x = x_ref[...]
    tt, bsz, d = x.shape
    h = (_rms(x, g_ref[2:3, :]) * (1.0 + mod_ref[4]) + mod_ref[3]).reshape(tt * bsz, d).astype(BF16)
    a = _bdot(h, w1_ref[...])
    b = _bdot(h, w3_ref[...])
    t = (jax.nn.silu(a) * b).astype(BF16)
    y = _bdot(t, w2_ref[...])
    o_ref[...] = x + mod_ref[5] * _rms(y, g_ref[3:4, :]).reshape(tt, bsz, d)


def _ffn_t(x3, mods_t, g4, w1, w3, w2):
    length, bsz, d = x3.shape
    dff = w1.shape[1]
    tt = min(TIME_TILE // 2, length)
    return pl.pallas_call(
        _ffn_t_kernel,
        grid=(length // tt,),
        in_specs=[
            pl.BlockSpec((tt, bsz, d), lambda i: (i, 0, 0)),
            pl.BlockSpec((6, bsz, d), lambda i: (0, 0, 0)),
            pl.BlockSpec((4, d), lambda i: (0, 0)),
            pl.BlockSpec((d, dff), lambda i: (0, 0), pipeline_mode=pl.Buffered(1)),
            pl.BlockSpec((d, dff), lambda i: (0, 0), pipeline_mode=pl.Buffered(1)),
            pl.BlockSpec((dff, d), lambda i: (0, 0), pipeline_mode=pl.Buffered(1)),
        ],
        out_specs=pl.BlockSpec((tt, bsz, d), lambda i: (i, 0, 0)),
        out_shape=jax.ShapeDtypeStruct((length, bsz, d), F32),
        compiler_params=_params(("parallel",)),
        name="dense_swiglu",
    )(x3, mods_t, g4, w1, w3, w2)


def _inproj_cd_kernel(x_ref, mod_ref, g_ref, w_ref, u_ref, v_ref, f_ref):
    x = x_ref[0]
    h = _rms(x, g_ref[0:1, :]) * (1.0 + mod_ref[0, 1:2, :]) + mod_ref[0, 0:1, :]
    z = _bdot(h.astype(BF16), w_ref[...])
    u_ref[0] = jax.nn.gelu(z[:, 0:C_WIDTH])
    v_ref[0] = jax.nn.gelu(z[:, C_WIDTH:2 * C_WIDTH])
    f_ref[0] = z[:, 2 * C_WIDTH:]


def _inproj_cd(x, mods, g4, w):
    bsz, length, d = x.shape
    n = w.shape[1]
    tm = min(ROW_TILE, length)
    return pl.pallas_call(
        _inproj_cd_kernel,
        grid=(bsz, length // tm),
        in_specs=[
            pl.BlockSpec((1, tm, d), lambda b, i: (b, i, 0)),
            pl.BlockSpec((1, 6, d), lambda b, i: (b, 0, 0)),
            pl.BlockSpec((4, d), lambda b, i: (0, 0)),
            pl.BlockSpec((d, n), lambda b, i: (0, 0)),
        ],
        out_specs=[pl.BlockSpec((1, tm, C_WIDTH), lambda b, i: (b, i, 0))] * 3,
        out_shape=[jax.ShapeDtypeStruct((bsz, length, C_WIDTH), F32)] * 3,
        compiler_params=_params(("parallel", "parallel")),
        name="inproj_cd",
    )(x, mods, g4, w)


def _fourier_kernel(f_ref, dmat_ref, lmat_ref, o_ref, z_ref, *, length, scale):
    j = pl.program_id(1)
    rows_per_step = 512 if length % 512 == 0 else length

    @pl.when(j == 0)
    def _():
        def body(c, _):
            rows = pl.ds(pl.multiple_of(c * rows_per_step, rows_per_step), rows_per_step)
            fb = f_ref[0, rows, :].astype(BF16)
            for g in range(D_GROUPS):
                cs = _bdot(fb[:, g * D_GROUP_DIM:(g + 1) * D_GROUP_DIM], dmat_ref[...])
                z_ref[0, rows, g * D_GROUP_DIM:(g + 1) * D_GROUP_DIM] = cs[:, 0:D_GROUP_DIM].astype(BF16)
                z_ref[1, rows, g * D_GROUP_DIM:(g + 1) * D_GROUP_DIM] = cs[:, D_GROUP_DIM:].astype(BF16)
            return 0
        lax.fori_loop(0, length // rows_per_step, body, 0)

    out = _bdot(lmat_ref[0], z_ref[0]) + _bdot(lmat_ref[1], z_ref[1])
    o_ref[0] = (out * scale).astype(BF16)


def _dft_matrices(length):
    k = np.arange(length, dtype=np.int64)
    ang_l = 2.0 * np.pi * ((k[:, None] * k[None, :]) % length).astype(np.float64) / length
    lmat = np.stack([np.cos(ang_l), -np.sin(ang_l)], axis=0)
    d = np.arange(D_GROUP_DIM, dtype=np.int64)
    ang_d = 2.0 * np.pi * ((d[:, None] * d[None, :]) % D_GROUP_DIM).astype(np.float64) / D_GROUP_DIM
    dmat = np.concatenate([np.cos(ang_d), np.sin(ang_d)], axis=1)
    return jnp.asarray(lmat, dtype=F32).astype(BF16), jnp.asarray(dmat, dtype=F32).astype(BF16)


def _fourier(f):
    bsz, length, width = f.shape
    tm = min(ROW_TILE, length)
    lmat, dmat = _dft_matrices(length)
    scale = float(1.0 / np.sqrt(length * D_GROUP_DIM))
    return pl.pallas_call(
        functools.partial(_fourier_kernel, length=length, scale=scale),
        grid=(bsz, length // tm),
        in_specs=[
            pl.BlockSpec((1, length, width), lambda b, i: (b, 0, 0)),
            pl.BlockSpec((D_GROUP_DIM, 2 * D_GROUP_DIM), lambda b, i: (0, 0)),
            pl.BlockSpec((2, tm, length), lambda b, i: (0, i, 0)),
        ],
        out_specs=pl.BlockSpec((1, tm, width), lambda b, i: (b, i, 0)),
        out_shape=jax.ShapeDtypeStruct((bsz, length, width), BF16),
        scratch_shapes=[pltpu.VMEM((2, length, width), BF16)],
        compiler_params=_params(("parallel", "arbitrary")),
        name="fourier_mix",
    )(f, dmat, lmat)


def _gate_out_kernel(u_ref, v_ref, fo_ref, x_ref, lng_ref, lnb_ref, wcat_ref, bs_ref, wout_ref, g_ref, mod_ref,
                     o_ref, vs_ref, gt_ref, *, tm):
    head_dim = C_WIDTH // C_HEADS
    head_of_lane = lax.broadcasted_iota(jnp.int32, (CHUNK, C_WIDTH), 1) // head_dim
    for ci in range(tm // CHUNK):
        rows = slice(ci * CHUNK, (ci + 1) * CHUNK)
        v = v_ref[0, rows, :]
        mu = jnp.mean(v, axis=-1, keepdims=True)
        cen = v - mu
        var = jnp.mean(cen * cen, axis=-1, keepdims=True)
        vn = (cen * lax.rsqrt(var + LN_EPS) * lng_ref[...] + lnb_ref[...]).astype(BF16)
        for h in range(C_HEADS):
            vs_ref[h * CHUNK:(h + 1) * CHUNK, :] = jnp.where(head_of_lane == h, vn, jnp.zeros_like(vn))
        mixed = _bdot(wcat_ref[...], vs_ref[...]) + bs_ref[...]
        gt_ref[rows, :] = (u_ref[0, rows, :] * mixed).astype(BF16)
    y = _bdot(gt_ref[...], wout_ref[0:C_WIDTH, :]) + _bdot(fo_ref[0], wout_ref[C_WIDTH:, :])
    o_ref[0] = x_ref[0] + mod_ref[0, 2:3, :] * _rms(y, g_ref[1:2, :])


def _gate_out(u, v, fo, x, ln_g, ln_b, w_s, b_s, w_out, g4, mods):
    bsz, length, d = x.shape
    tm = min(ROW_TILE, length)
    head_dim = C_WIDTH // C_HEADS
    wcat = w_s.transpose(1, 0, 2).reshape(CHUNK, C_HEADS * CHUNK).astype(BF16)
    bs_full = jnp.repeat(b_s.T, head_dim, axis=1)
    const = lambda b, i: (0, 0)
    return pl.pallas_call(
        functools.partial(_gate_out_kernel, tm=tm),
        grid=(bsz, length // tm),
        in_specs=[
            pl.BlockSpec((1, tm, C_WIDTH), lambda b, i: (b, i, 0)),
            pl.BlockSpec((1, tm, C_WIDTH), lambda b, i: (b, i, 0)),
            pl.BlockSpec((1, tm, D_WIDTH), lambda b, i: (b, i, 0)),
            pl.BlockSpec((1, tm, d), lambda b, i: (b, i, 0)),
            pl.BlockSpec((1, C_WIDTH), const),
            pl.BlockSpec((1, C_WIDTH), const),
            pl.BlockSpec((CHUNK, C_HEADS * CHUNK), const),
            pl.BlockSpec((CHUNK, C_WIDTH), const),
            pl.BlockSpec((C_WIDTH + D_WIDTH, d), const),
            pl.BlockSpec((4, d), const),
            pl.BlockSpec((1, 6, d), lambda b, i: (b, 0, 0)),
        ],
        out_specs=pl.BlockSpec((1, tm, d), lambda b, i: (b, i, 0)),
        out_shape=jax.ShapeDtypeStruct((bsz, length, d), F32),
        scratch_shapes=[
            pltpu.VMEM((C_HEADS * CHUNK, C_WIDTH), BF16),
            pltpu.VMEM((tm, C_WIDTH), BF16),
        ],
        compiler_params=_params(("parallel", "parallel")),
        name="gating_outproj",
    )(u, v, fo, x, ln_g.reshape(1, -1), ln_b.reshape(1, -1), wcat, bs_full, w_out, g4, mods)


def _router_kernel(x_ref, mod_ref, g_ref, wr_ref, tri_ref, xn_ref, meta_ref, cnt_ref, carry_ref, *, tm):
    first = jnp.logical_and(pl.program_id(0) == 0, pl.program_id(1) == 0)

    @pl.when(first)
    def _():
        carry_ref[...] = jnp.zeros_like(carry_ref)

    x = x_ref[0]
    h = _rms(x, g_ref[2:3, :]) * (1.0 + mod_ref[0, 4:5, :]) + mod_ref[0, 3:4, :]
    xn_ref[0] = h
    h_hi = h.astype(BF16)
    h_lo = (h - h_hi.astype(F32)).astype(BF16)
    logits = _bdot(jnp.concatenate([h_hi, h_hi, h_lo], axis=1), wr_ref[...])
    lane = lax.broadcasted_iota(jnp.int32, (tm, LANES), 1)
    neg = jnp.float32(-jnp.inf)
    lm = jnp.where(lane < N_EXPERTS, logits, neg)
    m1 = jnp.max(lm, axis=-1, keepdims=True)
    i1 = jnp.min(jnp.where(lm == m1, lane, LANES), axis=-1, keepdims=True)
    lm2 = jnp.where(lane == i1, neg, lm)
    m2 = jnp.max(lm2, axis=-1, keepdims=True)
    i2 = jnp.min(jnp.where(lm2 == m2, lane, LANES), axis=-1, keepdims=True)
    e = jnp.exp(m2 - m1)
    p1 = 1.0 / (1.0 + e)
    p2 = e / (1.0 + e)
    hot1 = (lane == i1).astype(F32)
    hot2 = (lane == i2).astype(F32)
    both = hot1 + hot2
    before = _bdot(tri_ref[...], both.astype(BF16)) + carry_ref[0:1, :]
    r1 = jnp.sum(before * hot1, axis=-1, keepdims=True)
    r2 = jnp.sum(before * hot2, axis=-1, keepdims=True)
    carry_ref[0:1, :] = carry_ref[0:1, :] + jnp.sum(both, axis=0, keepdims=True)
    cnt_ref[...] = carry_ref[...]
    vals = (i1.astype(F32), i2.astype(F32), p1, p2, r1, r2)
    meta = jnp.zeros((tm, LANES), F32)
    for k, val in enumerate(vals):
        meta = jnp.where(lane == k, val, meta)
    meta_ref[0] = meta


def _router(x, mods, g4, w_router):
    bsz, length, d = x.shape
    tm = min(ROW_TILE, length)
    wr = jnp.concatenate([w_router, jnp.zeros((d, LANES - N_EXPERTS), F32)], axis=1)
    wr_hi = wr.astype(BF16)
    wr_lo = (wr - wr_hi.astype(F32)).astype(BF16)
    wr = jnp.concatenate([wr_hi, wr_lo, wr_hi], axis=0)
    tri =jnp.asarray(np.tril(np.ones((tm, tm), np.float32), -1), dtype=BF16)
    const = lambda b, i: (0, 0)
    return pl.pallas_call(
        functools.partial(_router_kernel, tm=tm),
        grid=(bsz, length // tm),
        in_specs=[
            pl.BlockSpec((1, tm, d), lambda b, i: (b, i, 0)),
            pl.BlockSpec((1, 6, d), lambda b, i: (b, 0, 0)),
            pl.BlockSpec((4, d), const),
            pl.BlockSpec((3 * d, LANES), const),
            pl.BlockSpec((tm, tm), const),
        ],
        out_specs=[
            pl.BlockSpec((1, tm, d), lambda b, i: (b, i, 0)),
            pl.BlockSpec((1, tm, LANES), lambda b, i: (b, i, 0)),
            pl.BlockSpec((8, LANES), const),
        ],
        out_shape=[
            jax.ShapeDtypeStruct((bsz, length, d), F32),
            jax.ShapeDtypeStruct((bsz, length, LANES), F32),
            jax.ShapeDtypeStruct((8, LANES), F32),
        ],
        scratch_shapes=[pltpu.VMEM((8, LANES), F32)],
        compiler_params=_params(("arbitrary", "arbitrary")),
        name="moe_router",
    )(x, mods, g4, wr, tri)


def _expert_kernel(te_ref, nv_ref, x_ref, w1_ref, w3_ref, w2_ref, o_ref, acc_ref):
    i = pl.program_id(0)
    j = pl.program_id(1)

    @pl.when(i < nv_ref[0])
    def _():
        x = x_ref[...].astype(BF16)
        a = _bdot(x, w1_ref[0])
        b = _bdot(x, w3_ref[0])
        t = (jax.nn.silu(a) * b).astype(BF16)
        y = _bdot(t, w2_ref[0])

        @pl.when(j == 0)
        def _():
            acc_ref[...] = y

        @pl.when(j > 0)
        def _():
            acc_ref[...] += y

        @pl.when(j == pl.num_programs(1) - 1)
        def _():
            o_ref[...] = acc_ref[...]


def _experts(x_sorted, tile_expert, n_valid, w1, w3, w2):
    rows = x_sorted.shape[0]
    d = w1.shape[1]
    dff = w1.shape[2]
    tm = EXPERT_ROW_TILE
    n_ff = 2
    tf = dff // n_ff

    def row_map(i, j, te, nv):
        return (jnp.maximum(jnp.minimum(i, nv[0] - 1), 0), 0)

    def w13_map(i, j, te, nv):
        return (te[jnp.maximum(jnp.minimum(i, nv[0] - 1), 0)], 0, jnp.where(i < nv[0], j, n_ff - 1))

    def w2_map(i, j, te, nv):
        return (te[jnp.maximum(jnp.minimum(i, nv[0] - 1), 0)], jnp.where(i < nv[0], j, n_ff - 1), 0)

    return pl.pallas_call(
        _expert_kernel,
        grid_spec=pltpu.PrefetchScalarGridSpec(
            num_scalar_prefetch=2,
            grid=(rows // tm, n_ff),
            in_specs=[
                pl.BlockSpec((tm, d), row_map),
                pl.BlockSpec((1, d, tf), w13_map),
                pl.BlockSpec((1, d, tf), w13_map),
                pl.BlockSpec((1, tf, d), w2_map),
            ],
            out_specs=pl.BlockSpec((tm, d), row_map),
            scratch_shapes=[pltpu.VMEM((tm, d), F32)],
        ),
        out_shape=jax.ShapeDtypeStruct((rows, d), F32),
        compiler_params=_params(("arbitrary", "arbitrary")),
        name="expert_swiglu",
    )(tile_expert, n_valid, x_sorted, w1, w3, w2)


def _combine_kernel(x_ref, y_ref, meta_ref, g_ref, mod_ref, o_ref):
    meta = meta_ref[0]
    y = meta[:, 2:3] * y_ref[0, 0] + meta[:, 3:4] * y_ref[1, 0]
    o_ref[0] = x_ref[0] + mod_ref[0, 5:6, :] * _rms(y, g_ref[3:4, :])


def _combine(x, y2, meta, g4, mods):
    bsz, length, d = x.shape
    tm = min(ROW_TILE, length)
    return pl.pallas_call(
        _combine_kernel,
        grid=(bsz, length // tm),
        in_specs=[
            pl.BlockSpec((1, tm, d), lambda b, i: (b, i, 0)),
            pl.BlockSpec((2, 1, tm, d), lambda b, i: (0, b, i, 0)),
            pl.BlockSpec((1, tm, LANES), lambda b, i: (b, i, 0)),
            pl.BlockSpec((4, d), lambda b, i: (0, 0)),
            pl.BlockSpec((1, 6, d), lambda b, i: (b, 0, 0)),
        ],
        out_specs=pl.BlockSpec((1, tm, d), lambda b, i: (b, i, 0)),
        out_shape=jax.ShapeDtypeStruct((bsz, length, d), F32),
        compiler_params=_params(("parallel", "parallel")),
        name="moe_combine",
    )(x, y2, meta, g4, mods)


SC_CORES = 2
SC_SUBCORES = 16
SC_WORKERS = SC_CORES * SC_SUBCORES
SC_ROW_CHUNK = 32


def _sc_mesh():
    return plsc.VectorSubcoreMesh(core_axis_name="c", subcore_axis_name="s")


def _sc_dispatch(rows_in, pos2, n_out):
    tokens, width = rows_in.shape
    per_worker = tokens // SC_WORKERS
    chunk = 2 * SC_ROW_CHUNK
    n_chunks = per_worker // chunk
    idx = pos2.reshape(2, SC_WORKERS, n_chunks, chunk)

    @functools.partial(
        pl.kernel, mesh=_sc_mesh(),
        out_type=jax.ShapeDtypeStruct((n_out, width), rows_in.dtype),
        scratch_types=[
            pltpu.VMEM((n_chunks, chunk), jnp.int32),
            pltpu.VMEM((n_chunks, chunk), jnp.int32),
            pltpu.VMEM((chunk, width), rows_in.dtype),
        ],
        name="sc_dispatch",
    )
    def body(rows_hbm, idx_hbm, out_hbm, idx0_v, idx1_v, rows_v):
        wid = lax.axis_index("s") * SC_CORES + lax.axis_index("c")
        base = wid * per_worker
        pltpu.sync_copy(idx_hbm.at[0, wid], idx0_v)
        pltpu.sync_copy(idx_hbm.at[1, wid], idx1_v)

        @pl.loop(0, n_chunks)
        def _(c):
            pltpu.sync_copy(rows_hbm.at[pl.ds(base + c * chunk, chunk)], rows_v)
            pltpu.sync_copy(rows_v, out_hbm.at[idx0_v.at[c]])
            pltpu.sync_copy(rows_v, out_hbm.at[idx1_v.at[c]])

    return body(rows_in, idx)


def _sc_gather(table, idx):
    n = idx.shape[0]
    width = table.shape[1]
    chunk = SC_ROW_CHUNK
    per_worker = n // SC_WORKERS
    n_pairs = per_worker // (2 * chunk)
    idx3 = idx.reshape(SC_WORKERS, 2 * n_pairs, chunk)

    @functools.partial(
        pl.kernel, mesh=_sc_mesh(),
        out_type=jax.ShapeDtypeStruct((n, width), table.dtype),
        scratch_types=[
            pltpu.VMEM((2 * n_pairs, chunk), jnp.int32),
            pltpu.VMEM((2, chunk, width), table.dtype),
            pltpu.SemaphoreType.DMA, pltpu.SemaphoreType.DMA,
            pltpu.SemaphoreType.DMA, pltpu.SemaphoreType.DMA,
        ],
        name="sc_gather",
    )
    def body(table_hbm, idx_hbm, out_hbm, idx_v, rows_v, gsem0, gsem1, wsem0, wsem1):
        wid = lax.axis_index("s") * SC_CORES + lax.axis_index("c")
        base = wid * per_worker
        gsem = (gsem0, gsem1)
        wsem = (wsem0, wsem1)
        pltpu.sync_copy(idx_hbm.at[wid], idx_v)

        @pl.loop(0, n_pairs)
        def _(p):
            c0 = 2 * p
            gathers = [pltpu.async_copy(table_hbm.at[idx_v.at[c0 + s]], rows_v.at[s], gsem[s]) for s in range(2)]
            writes = []
            for s in range(2):
                gathers[s].wait()
                rows_out = out_hbm.at[pl.ds(base + (c0 + s) * chunk, chunk)]
                writes.append(pltpu.async_copy(rows_v.at[s], rows_out, wsem[s]))
            for s in range(2):
                writes[s].wait()

    return body(table, idx3)


def _moe(x, mods, g4, w_router, w1, w3, w2):
    bsz, length, d = x.shape
    tokens = bsz * length
    tm = EXPERT_ROW_TILE
    xn, meta, counts = _router(x, mods, g4, w_router)
    meta2 = meta.reshape(tokens, LANES)
    experts = meta2[:, 0:2].astype(jnp.int32)
    ranks = meta2[:, 4:6].astype(jnp.int32)
    cnt = counts[0, 0:N_EXPERTS].astype(jnp.int32)
    padded = ((cnt + tm - 1) // tm) * tm
    ends = jnp.cumsum(padded)
    starts = ends - padded
    pos = (starts[experts] + ranks).T.reshape(2 * tokens)
    rows = 2 * tokens + N_EXPERTS * tm
    n_tiles = rows // tm
    n_valid = (ends[-1] // tm).astype(jnp.int32).reshape(1)
    tile_start = jnp.arange(n_tiles, dtype=jnp.int32) * tm
    tile_expert = jnp.minimum(
        jnp.sum((tile_start[:, None] >= ends[None, :]).astype(jnp.int32), axis=1), N_EXPERTS - 1)
    x_sorted = _sc_dispatch(xn.reshape(tokens, d), pos.reshape(2, tokens), rows)
    y_sorted = _experts(x_sorted, tile_expert, n_valid, w1, w3, w2)
    y2 = _sc_gather(y_sorted, pos).reshape(2, bsz, length, d)
    return _combine(x, y2, meta, g4, mods)


def _grid_pos_embed(rows, d):
    row = jnp.repeat(jnp.arange(rows, dtype=F32), GRID_W)
    col = jnp.tile(jnp.arange(GRID_W, dtype=F32), rows)
    n_freq = d // 4
    omega = POS_THETA ** (-jnp.arange(n_freq, dtype=F32) / n_freq)
    ang_r = row[:, None] * omega
    ang_c = col[:, None] * omega
    return jnp.concatenate([jnp.sin(ang_r), jnp.cos(ang_r), jnp.sin(ang_c), jnp.cos(ang_c)], axis=-1)


def kernel(x, c, ctx, c_ctx, ada_w, ada_b, norm_g, ab_w_in, rg_conv_w, rg_conv_b, rg_w_r, rg_b_r, rg_w_i, rg_b_i, rg_lambda, cv_dw_w, cv_dw_b, cv_ln_g, cv_ln_b, ab_w_out, ffn_w1, ffn_w3, ffn_w2, cd_w_in, sg_ln_g, sg_ln_b, sg_w_s, sg_b_s, cd_w_out, moe_router, moe_w1, moe_w3, moe_w2):
    bsz, length, d = x.shape
    assert ada_w.shape[0] == 2, "two layers: one even (RG-LRU | Conformer), one odd (gMLP | Fourier)"
    pos = _grid_pos_embed(length // GRID_W, d)

    cpad = jnp.concatenate([c, c_ctx[None, :], jnp.zeros((16 - bsz - 1, d), F32)], axis=0)
    m = _ada(cpad, ada_w, ada_b)
    mods = m[:, :bsz].reshape(2, bsz, 6, d)
    mods_ctx = m[:, bsz:bsz + 1].reshape(2, 1, 6, d)

    mods_t = mods[0].transpose(1, 0, 2)
    mods_ctx_t = jnp.broadcast_to(mods_ctx[0].reshape(6, 1, d), (6, bsz, d))
    x_t = x.transpose(1, 0, 2)
    ctx_t = ctx.transpose(1, 0, 2)
    pos3 = pos[:, None, :]
    w_in = ab_w_in[0].astype(BF16)
    xa3, gg3, glu3 = _inproj_t(x_t, pos3, mods_t, norm_g[0], w_in, rg_conv_w[0], rg_conv_b[0], True)
    (xc3,) = _inproj_t(ctx_t, None, mods_ctx_t, norm_g[0], w_in[:, 0:A_WIDTH], rg_conv_w[0], rg_conv_b[0], False)
    wg, bg, sp = _gate_weights(rg_w_r[0], rg_b_r[0], rg_w_i[0], rg_b_i[0], rg_lambda[0])
    rec_hi, rec_lo = _scan_t(xa3, xc3, gg3, wg, bg, sp)
    u3 = _conv_t(glu3, cv_dw_w[0], cv_dw_b[0], cv_ln_g[0], cv_ln_b[0])
    x1_t = _outproj_t(rec_lo, rec_hi, u3, x_t, pos3, ab_w_out[0].astype(BF16), norm_g[0], mods_t)
    x2_t = _ffn_t(x1_t, mods_t, norm_g[0], ffn_w1[0].astype(BF16), ffn_w3[0].astype(BF16), ffn_w2[0].astype(BF16))
    x2 = x2_t.transpose(1, 0, 2)

    u, v, f = _inproj_cd(x2, mods[1], norm_g[1], cd_w_in[0].astype(BF16))
    fo = _fourier(f)
    x3 = _gate_out(u, v, fo, x2, sg_ln_g[0], sg_ln_b[0], sg_w_s[0], sg_b_s[0], cd_w_out[0].astype(BF16),
                   norm_g[1], mods[1])
    return _moe(x3, mods[1], norm_g[1], moe_router[0], moe_w1[0].astype(BF16), moe_w3[0].astype(BF16),
                moe_w2[0].astype(BF16))
```

```python
import functools

import numpy as np
import jax
import jax.numpy as jnp
from jax import lax
from jax.experimental import pallas as pl
from jax.experimental.pallas import tpu as pltpu
from jax.experimental.pallas import tpu_sc as plsc

F32 = jnp.float32
BF16 = jnp.bfloat16

GRID_W = 64
POS_THETA = 10000.0
NORM_EPS = 1e-6
LN_EPS = 1e-5
LRU_C = 8.0

A_WIDTH = 512
A_HEADS = 8
A_CONV_W = 4
B_WIDTH = 512
B_CONV_W = 31
C_WIDTH = 512
C_HEADS = 8
CHUNK = 128
D_WIDTH = 512
D_GROUPS = 4
D_GROUP_DIM = D_WIDTH // D_GROUPS
N_EXPERTS = 8

LANES = 128
ROW_TILE = 512
EXPERT_ROW_TILE = 512
VMEM_LIMIT = 56 * 2 ** 20


def _params(sem):
    return pltpu.CompilerParams(dimension_semantics=sem, vmem_limit_bytes=VMEM_LIMIT)


def _rms(x, g):
    return x * lax.rsqrt(jnp.mean(x * x, axis=-1, keepdims=True) + NORM_EPS) * g


def _bdot(a, b):
    return jnp.dot(a, b, preferred_element_type=F32)


def _ada_kernel(c_ref, w_ref, b_ref, o_ref):
    s = jax.nn.silu(c_ref[...])
    o_ref[0] = _bdot(s.astype(BF16), w_ref[0].astype(BF16)) + b_ref[0]


def _ada(cpad, ada_w, ada_b):
    nl, d, n6 = ada_w.shape
    rows = cpad.shape[0]
    tn = n6 // 4
    return pl.pallas_call(
        _ada_kernel,
        grid=(nl, n6 // tn),
        in_specs=[
            pl.BlockSpec((rows, d), lambda l, j: (0, 0)),
            pl.BlockSpec((1, d, tn), lambda l, j: (l, 0, j)),
            pl.BlockSpec((1, 1, tn), lambda l, j: (l, 0, j)),
        ],
        out_specs=pl.BlockSpec((1, rows, tn), lambda l, j: (l, 0, j)),
        out_shape=jax.ShapeDtypeStruct((nl, rows, n6), F32),
        compiler_params=_params(("arbitrary", "arbitrary")),
        name="ada_terms",
    )(cpad, ada_w, ada_b.reshape(nl, 1, n6))


TIME_TILE = 128
SCAN_TIME_TILE = 128
SCAN_LANES = 256
CONV_HALO = 16


def _time_major(ref):
    return jnp.swapaxes(ref[...], 0, 1)


def _inproj_t_kernel(*refs, tt, add_pos, branches):
    if add_pos:
        x_ref, xp_ref, xn_ref, pos_ref, posp_ref, posn_ref, mod_ref, g_ref, w_ref, cw_ref, cb_ref, *outs = refs
        xall = jnp.concatenate(
            [_time_major(xp_ref)[7:8] + posp_ref[...], _time_major(x_ref) + pos_ref[...],
             _time_major(xn_ref)[0:2] + posn_ref[...]], axis=0)
    else:
        x_ref, xp_ref, xn_ref, mod_ref, g_ref, w_ref, cw_ref, cb_ref, *outs = refs
        xall = jnp.concatenate([_time_major(xp_ref)[7:8], _time_major(x_ref), _time_major(xn_ref)[0:2]], axis=0)
    i = pl.program_id(0)
    last = pl.num_programs(0) - 1
    steps, bsz, d = xall.shape
    h = _rms(xall, g_ref[0:1, :]) * (1.0 + mod_ref[1]) + mod_ref[0]
    z = _bdot(h.reshape(steps * bsz, d).astype(BF16), w_ref[...])
    z3 = z.reshape(steps, bsz, z.shape[-1])
    t_idx = lax.broadcasted_iota(jnp.int32, (steps, bsz, A_WIDTH), 0)
    inside = jnp.logical_and(jnp.logical_or(t_idx >= 1, i > 0), jnp.logical_or(t_idx <= tt, i < last))
    xa = jnp.where(inside, z3[:, :, 0:A_WIDTH], 0.0)
    xconv = cb_ref[...] + cw_ref[0:1, :] * xa[0:tt]
    for k in range(1, A_CONV_W):
        xconv = xconv + cw_ref[k:k + 1, :] * xa[k:k + tt]
    outs[0][...] = xconv
    if branches:
        outs[1][...] = jax.nn.gelu(z3[1:tt + 1, :, A_WIDTH:2 * A_WIDTH])
        vb = z3[1:tt + 1, :, 2 * A_WIDTH:2 * A_WIDTH + B_WIDTH]
        gb = z3[1:tt + 1, :, 2 * A_WIDTH + B_WIDTH:]
        outs[2][...] = vb * jax.nn.sigmoid(gb)


def _inproj_t(x, pos3, mods_t, g4, w, conv_w, conv_b, branches):
    bsz, length, d = x.shape
    n = w.shape[1]
    tt = min(TIME_TILE, length)
    add_pos = pos3 is not None
    n_out = 3 if branches else 1
    prev_map = lambda i: (jnp.maximum(i * tt - 1, 0), 0, 0)
    next_map = lambda i: (jnp.minimum((i + 1) * (tt // 2), length // 2 - 1), 0, 0)
    in_specs = [
        pl.BlockSpec((bsz, tt, d), lambda i: (0, i, 0)),
        pl.BlockSpec((bsz, 8, d), lambda i: (0, jnp.maximum(i * (tt // 8) - 1, 0), 0)),
        pl.BlockSpec((bsz, 8, d), lambda i: (0, jnp.minimum((i + 1) * (tt // 8), length // 8 - 1), 0)),
    ]
    args = [x, x, x]
    if add_pos:
        in_specs += [
            pl.BlockSpec((tt, 1, d), lambda i: (i, 0, 0)),
            pl.BlockSpec((1, 1, d), prev_map),
            pl.BlockSpec((2, 1, d), next_map),
        ]
        args += [pos3, pos3, pos3]
    in_specs += [
        pl.BlockSpec((6, bsz, d), lambda i: (0, 0, 0)),
        pl.BlockSpec((4, d), lambda i: (0, 0)),
        pl.BlockSpec((d, n), lambda i: (0, 0)),
        pl.BlockSpec((A_CONV_W, A_WIDTH), lambda i: (0, 0)),
        pl.BlockSpec((1, A_WIDTH), lambda i: (0, 0)),
    ]
    args += [mods_t, g4, w, conv_w, conv_b.reshape(1, A_WIDTH)]
    return pl.pallas_call(
        functools.partial(_inproj_t_kernel, tt=tt, add_pos=add_pos, branches=branches),
        grid=(length // tt,),
        in_specs=in_specs,
        out_specs=[pl.BlockSpec((tt, bsz, A_WIDTH), lambda i: (i, 0, 0))] * n_out,
        out_shape=[jax.ShapeDtypeStruct((length, bsz, A_WIDTH), F32)] * n_out,
        compiler_params=_params(("parallel",)),
        name="inproj_ab" if branches else "inproj_ctx",
    )(*args)


def _scan_t_kernel(xf_ref, xb_ref, ggf_ref, ggb_ref, xc_ref, wg_ref, bg_ref, sp_ref, hi_ref, lo_ref,
                   a_ref, b_ref, h_ref, state_ref, *, tb, nb, ctx_len):
    k = pl.program_id(1)
    half = nb // 2
    bsz, lanes = xf_ref.shape[1], xf_ref.shape[2]
    chunk = 32

    def coefficients(x_ref, n, direction):
        def body(c, _):
            t0 = pl.multiple_of(c * chunk, chunk)
            x = x_ref[pl.ds(t0, chunk)].reshape(chunk * bsz, lanes)
            g = _bdot(x.astype(BF16), wg_ref[0, direction]) + bg_ref[0, direction]
            gate = 0.5 * jnp.tanh(0.5 * g) + 0.5
            r = gate[:, 0:lanes]
            i = gate[:, lanes:]
            log_a = (-LRU_C * r) * sp_ref[0, direction:direction + 1, :]
            a = jnp.exp(log_a)
            one_minus_a2 = -jnp.tanh(log_a) * (a * a + 1.0)
            a_ref[direction, pl.ds(t0, chunk)] = a.reshape(chunk, bsz, lanes)
            b_ref[direction, pl.ds(t0, chunk)] = (jnp.sqrt(one_minus_a2) * (i * x)).reshape(chunk, bsz, lanes)
            return 0
        lax.fori_loop(0, n // chunk, body, 0)

    def sweep(n, store):
        def body(t, carry):
            hf, hb = carry
            tr = n - 1 - t
            hf = a_ref[0, pl.ds(t, 1)][0] * hf + b_ref[0, pl.ds(t, 1)][0]
            hb = a_ref[1, pl.ds(tr, 1)][0] * hb + b_ref[1, pl.ds(tr, 1)][0]
            if store:
                b_ref[0, pl.ds(t, 1)] = hf[None]
                b_ref[1, pl.ds(tr, 1)] = hb[None]
            return hf, hb
        hf, hb = lax.fori_loop(0, n, body, (state_ref[0], state_ref[1]), unroll=8)
        state_ref[0] = hf
        state_ref[1] = hb

    @pl.when(k == 0)
    def _():
        state_ref[...] = jnp.zeros_like(state_ref)
        coefficients(xc_ref, ctx_len, 0)
        coefficients(xc_ref, ctx_len, 1)
        sweep(ctx_len, False)

    coefficients(xf_ref, tb, 0)
    coefficients(xb_ref, tb, 1)
    sweep(tb, True)
    m = nb - 1 - k

    @pl.when(k < half)
    def _():
        h_ref[k] = b_ref[0, 0:tb]
        h_ref[m] = b_ref[1, 0:tb]

    @pl.when(k >= half)
    def _():
        hi_ref[...] = (b_ref[0, 0:tb] + h_ref[k]) * ggf_ref[...]
        lo_ref[...] = (h_ref[m] + b_ref[1, 0:tb]) * ggb_ref[...]


def _scan_t(xa3, xc3, gg3, wg, bg, sp):
    length, bsz, width = xa3.shape
    ctx_len = xc3.shape[0]
    tb = min(SCAN_TIME_TILE, length // 2)
    nb = length // tb
    half = nb // 2
    lanes = SCAN_LANES
    blk = (tb, bsz, lanes)
    return pl.pallas_call(
        functools.partial(_scan_t_kernel, tb=tb, nb=nb, ctx_len=ctx_len),
        grid=(width // lanes, nb),
        in_specs=[
            pl.BlockSpec(blk, lambda g, k: (k, 0, g)),
            pl.BlockSpec(blk, lambda g, k: (nb - 1 - k, 0, g)),
            pl.BlockSpec(blk, lambda g, k: (jnp.maximum(k, half), 0, g)),
            pl.BlockSpec(blk, lambda g, k: (jnp.minimum(nb - 1 - k, half - 1), 0, g)),
            pl.BlockSpec((ctx_len, bsz, lanes), lambda g, k: (0, 0, g)),
            pl.BlockSpec((1, 2, lanes, 2 * lanes), lambda g, k: (g, 0, 0, 0)),
            pl.BlockSpec((1, 2, 1, 2 * lanes), lambda g, k: (g, 0, 0, 0)),
            pl.BlockSpec((1, 2, lanes), lambda g, k: (g, 0, 0)),
        ],
        out_specs=[
            pl.BlockSpec(blk, lambda g, k: (jnp.maximum(k - half, 0), 0, g)),
            pl.BlockSpec(blk, lambda g, k: (jnp.minimum(nb - 1 - k, half - 1), 0, g)),
        ],
        out_shape=[jax.ShapeDtypeStruct((length // 2, bsz, width), F32)] * 2,
        scratch_shapes=[
            pltpu.VMEM((2, max(tb, ctx_len), bsz, lanes), F32),
            pltpu.VMEM((2, max(tb, ctx_len), bsz, lanes), F32),
            pltpu.VMEM((nb, tb, bsz, lanes), F32),
            pltpu.VMEM((2, bsz, lanes), F32),
        ],
        compiler_params=_params(("parallel", "arbitrary")),
        name="rglru_scan",
    )(xa3, xa3, gg3, gg3, xc3, wg, bg, sp)


def _gate_weights(w_r, b_r, w_i, b_i, lam):
    hd = A_WIDTH // A_HEADS
    heads_per_group = SCAN_LANES // hd
    ngroups = A_WIDTH // SCAN_LANES

    def blockdiag(w):
        w = w.reshape(ngroups, heads_per_group, hd, hd)
        eye = jnp.eye(heads_per_group, dtype=w.dtype)
        return jnp.einsum("ghde,hk->ghdke", w, eye).reshape(ngroups, SCAN_LANES, SCAN_LANES)

    wg = jnp.stack([jnp.concatenate([blockdiag(w_r[d]), blockdiag(w_i[d])], axis=-1) for d in range(2)], axis=1)
    bg = jnp.stack([jnp.concatenate([b_r[d].reshape(ngroups, 1, SCAN_LANES), b_i[d].reshape(ngroups, 1, SCAN_LANES)],
                                    axis=-1) for d in range(2)], axis=1)
    sp = jax.nn.softplus(-lam.astype(F32)).reshape(2, ngroups, SCAN_LANES).transpose(1, 0, 2)
    return wg.astype(BF16), bg, sp


def _conv_t_kernel(x_ref, xp_ref, xn_ref, w_ref, b_ref, lng_ref, lnb_ref, o_ref, stage_ref, *, tb):
    i = pl.program_id(0)
    last = pl.num_programs(0) - 1
    halo = CONV_HALO
    stage_ref[0:halo] = jnp.where(i > 0, xp_ref[...], 0.0)
    stage_ref[halo:halo + tb] = x_ref[...]
    stage_ref[halo + tb:2 * halo + tb] = jnp.where(i < last, xn_ref[...], 0.0)
    sub = 8

    def body(r, _):
        t0 = pl.multiple_of(r * sub, sub)
        def tap(k, acc):
            return acc + w_ref[k] * stage_ref[pl.ds(t0 + 1 + k, sub)]
        acc = lax.fori_loop(0, B_CONV_W + 1, tap, jnp.broadcast_to(b_ref[...], (sub,) + b_ref.shape), unroll=8)
        o_ref[pl.ds(t0, sub)] = acc
        return 0
    lax.fori_loop(0, tb // sub, body, 0)

    norm_rows = 16

    def norm_body(r, _):
        t0 = pl.multiple_of(r * norm_rows, norm_rows)
        acc = o_ref[pl.ds(t0, norm_rows)]
        mu = jnp.mean(acc, axis=-1, keepdims=True)
        cen = acc - mu
        var = jnp.mean(cen * cen, axis=-1, keepdims=True)
        y = cen * lax.rsqrt(var + LN_EPS) * lng_ref[...] + lnb_ref[...]
        o_ref[pl.ds(t0, norm_rows)] = jax.nn.silu(y)
        return 0
    lax.fori_loop(0, tb // norm_rows, norm_body, 0)


def _conv_t(glu3, dw_w, dw_b, ln_g, ln_b):
    length, bsz, width = glu3.shape
    tb = min(TIME_TILE, length)
    halo = CONV_HALO
    taps = dw_w.shape[0]
    w8 = jnp.broadcast_to(jnp.concatenate([dw_w, jnp.zeros((32 - taps, width), F32)], axis=0)[:, None, :],
                          (32, bsz, width))
    b8 = jnp.broadcast_to(dw_b[None, :], (bsz, width))
    return pl.pallas_call(
        functools.partial(_conv_t_kernel, tb=tb),
        grid=(length // tb,),
        in_specs=[
            pl.BlockSpec((tb, bsz, width), lambda i: (i, 0, 0)),
            pl.BlockSpec((halo, bsz, width), lambda i: (jnp.maximum(i * (tb // halo) - 1, 0), 0, 0)),
            pl.BlockSpec((halo, bsz, width), lambda i: (jnp.minimum((i + 1) * (tb // halo), length // halo - 1), 0, 0)),
            pl.BlockSpec((32, bsz, width), lambda i: (0, 0, 0)),
            pl.BlockSpec((bsz, width), lambda i: (0, 0)),
            pl.BlockSpec((1, width), lambda i: (0, 0)),
            pl.BlockSpec((1, width), lambda i: (0, 0)),
        ],
        out_specs=pl.BlockSpec((tb, bsz, width), lambda i: (i, 0, 0)),
        out_shape=jax.ShapeDtypeStruct((length, bsz, width), F32),
        scratch_shapes=[pltpu.VMEM((tb + 2 * halo, bsz, width), F32)],
        compiler_params=_params(("parallel",)),
        name="conformer_conv",
    )(glu3, glu3, glu3, w8, b8, ln_g.reshape(1, -1), ln_b.reshape(1, -1))


def _outproj_t_kernel(lo_ref, hi_ref, u_ref, x_ref, pos_ref, wout_ref, g_ref, mod_ref, o_ref, *, half_steps):
    i = pl.program_id(0)
    bsz, tt, d = x_ref.shape
    rec = jnp.where(i < half_steps, lo_ref[...], hi_ref[...])
    rows = tt * bsz
    y = (_bdot(rec.reshape(rows, A_WIDTH).astype(BF16), wout_ref[0:A_WIDTH, :])
         + _bdot(u_ref[...].reshape(rows, B_WIDTH).astype(BF16), wout_ref[A_WIDTH:, :]))
    yn = _rms(y, g_ref[1:2, :]).reshape(tt, bsz, d)
    o_ref[...] = _time_major(x_ref) + pos_ref[...] + mod_ref[2] * yn


def _outproj_t(rec_lo, rec_hi, u3, x, pos3, w_out, g4, mods_t):
    bsz, length, d = x.shape
    tt = min(TIME_TILE // 2, length // 2)
    half_steps = (length // 2) // tt
    return pl.pallas_call(
        functools.partial(_outproj_t_kernel, half_steps=half_steps),
        grid=(length // tt,),
        in_specs=[
            pl.BlockSpec((tt, bsz, A_WIDTH), lambda i: (jnp.minimum(i, half_steps - 1), 0, 0)),
            pl.BlockSpec((tt, bsz, A_WIDTH), lambda i: (jnp.maximum(i - half_steps, 0), 0, 0)),
            pl.BlockSpec((tt, bsz, B_WIDTH), lambda i: (i, 0, 0)),
            pl.BlockSpec((bsz, tt, d), lambda i: (0, i, 0)),
            pl.BlockSpec((tt, 1, d), lambda i: (i, 0, 0)),
            pl.BlockSpec((A_WIDTH + B_WIDTH, d), lambda i: (0, 0)),
            pl.BlockSpec((4, d), lambda i: (0, 0)),
            pl.BlockSpec((6, bsz, d), lambda i: (0, 0, 0)),
        ],
        out_specs=pl.BlockSpec((tt, bsz, d), lambda i: (i, 0, 0)),
        out_shape=jax.ShapeDtypeStruct((length, bsz, d), F32),
        compiler_params=_params(("parallel",)),
        name="mixer_ab_outproj",
    )(rec_lo, rec_hi, u3, x, pos3, w_out, g4, mods_t)


def _ffn_t_kernel(x_ref, mod_ref, g_ref, w1_ref, w3_ref, w2_ref, o_ref):
    x = x_ref[...]
    tt, bsz, d = x.shape
    h = (_rms(x, g_ref[2:3, :]) * (1.0 + mod_ref[4]) + mod_ref[3]).reshape(tt * bsz, d).astype(BF16)
    a = _bdot(h, w1_ref[...])
    b = _bdot(h, w3_ref[...])
    t = (jax.nn.silu(a) * b).astype(BF16)
    y = _bdot(t, w2_ref[...])
    o_ref[...] = jnp.swapaxes(x + mod_ref[5] * _rms(y, g_ref[3:4, :]).reshape(tt, bsz, d), 0, 1)


def _ffn_t(x3, mods_t, g4, w1, w3, w2):
    length, bsz, d = x3.shape
    dff = w1.shape[1]
    tt = min(TIME_TILE // 2, length)
    return pl.pallas_call(
        _ffn_t_kernel,
        grid=(length // tt,),
        in_specs=[
            pl.BlockSpec((tt, bsz, d), lambda i: (i, 0, 0)),
            pl.BlockSpec((6, bsz, d), lambda i: (0, 0, 0)),
            pl.BlockSpec((4, d), lambda i: (0, 0)),
            pl.BlockSpec((d, dff), lambda i: (0, 0), pipeline_mode=pl.Buffered(1)),
            pl.BlockSpec((d, dff), lambda i: (0, 0), pipeline_mode=pl.Buffered(1)),
            pl.BlockSpec((dff, d), lambda i: (0, 0), pipeline_mode=pl.Buffered(1)),
        ],
        out_specs=pl.BlockSpec((bsz, tt, d), lambda i: (0, i, 0)),
        out_shape=jax.ShapeDtypeStruct((bsz, length, d), F32),
        compiler_params=_params(("parallel",)),
        name="dense_swiglu",
    )(x3, mods_t, g4, w1, w3, w2)


def _inproj_cd_kernel(x_ref, mod_ref, g_ref, w_ref, u_ref, v_ref, f_ref):
    x = x_ref[0]
    h = _rms(x, g_ref[0:1, :]) * (1.0 + mod_ref[0, 1:2, :]) + mod_ref[0, 0:1, :]
    z = _bdot(h.astype(BF16), w_ref[...])
    u_ref[0] = jax.nn.gelu(z[:, 0:C_WIDTH])
    v_ref[0] = jax.nn.gelu(z[:, C_WIDTH:2 * C_WIDTH])
    f_ref[0] = z[:, 2 * C_WIDTH:]


def _inproj_cd(x, mods, g4, w):
    bsz, length, d = x.shape
    n = w.shape[1]
    tm = min(ROW_TILE, length)
    return pl.pallas_call(
        _inproj_cd_kernel,
        grid=(bsz, length // tm),
        in_specs=[
            pl.BlockSpec((1, tm, d), lambda b, i: (b, i, 0)),
            pl.BlockSpec((1, 6, d), lambda b, i: (b, 0, 0)),
            pl.BlockSpec((4, d), lambda b, i: (0, 0)),
            pl.BlockSpec((d, n), lambda b, i: (0, 0)),
        ],
        out_specs=[pl.BlockSpec((1, tm, C_WIDTH), lambda b, i: (b, i, 0))] * 3,
        out_shape=[jax.ShapeDtypeStruct((bsz, length, C_WIDTH), F32)] * 3,
        compiler_params=_params(("parallel", "parallel")),
        name="inproj_cd",
    )(x, mods, g4, w)


def _fourier_kernel(f_ref, dmat_ref, lmat_ref, o_ref, z_ref, *, length, scale):
    j = pl.program_id(1)
    rows_per_step = 512 if length % 512 == 0 else length

    @pl.when(j == 0)
    def _():
        def body(c, _):
            rows = pl.ds(pl.multiple_of(c * rows_per_step, rows_per_step), rows_per_step)
            fb = f_ref[0, rows, :].astype(BF16)
            for g in range(D_GROUPS):
                cs = _bdot(fb[:, g * D_GROUP_DIM:(g + 1) * D_GROUP_DIM], dmat_ref[...])
                z_ref[0, rows, g * D_GROUP_DIM:(g + 1) * D_GROUP_DIM] = cs[:, 0:D_GROUP_DIM].astype(BF16)
                z_ref[1, rows, g * D_GROUP_DIM:(g + 1) * D_GROUP_DIM] = cs[:, D_GROUP_DIM:].astype(BF16)
            return 0
        lax.fori_loop(0, length // rows_per_step, body, 0)

    out = _bdot(lmat_ref[0], z_ref[0]) + _bdot(lmat_ref[1], z_ref[1])
    o_ref[0] = (out * scale).astype(BF16)


def _dft_matrices(length):
    k = np.arange(length, dtype=np.int64)
    ang_l = 2.0 * np.pi * ((k[:, None] * k[None, :]) % length).astype(np.float64) / length
    lmat = np.stack([np.cos(ang_l), -np.sin(ang_l)], axis=0)
    d = np.arange(D_GROUP_DIM, dtype=np.int64)
    ang_d = 2.0 * np.pi * ((d[:, None] * d[None, :]) % D_GROUP_DIM).astype(np.float64) / D_GROUP_DIM
    dmat = np.concatenate([np.cos(ang_d), np.sin(ang_d)], axis=1)
    return jnp.asarray(lmat, dtype=F32).astype(BF16), jnp.asarray(dmat, dtype=F32).astype(BF16)


def _fourier(f):
    bsz, length, width = f.shape
    tm = min(ROW_TILE, length)
    lmat, dmat = _dft_matrices(length)
    scale = float(1.0 / np.sqrt(length * D_GROUP_DIM))
    return pl.pallas_call(
        functools.partial(_fourier_kernel, length=length, scale=scale),
        grid=(bsz, length // tm),
        in_specs=[
            pl.BlockSpec((1, length, width), lambda b, i: (b, 0, 0)),
            pl.BlockSpec((D_GROUP_DIM, 2 * D_GROUP_DIM), lambda b, i: (0, 0)),
            pl.BlockSpec((2, tm, length), lambda b, i: (0, i, 0)),
        ],
        out_specs=pl.BlockSpec((1, tm, width), lambda b, i: (b, i, 0)),
        out_shape=jax.ShapeDtypeStruct((bsz, length, width), BF16),
        scratch_shapes=[pltpu.VMEM((2, length, width), BF16)],
        compiler_params=_params(("parallel", "arbitrary")),
        name="fourier_mix",
    )(f, dmat, lmat)


def _gate_out_kernel(u_ref, v_ref, fo_ref, x_ref, lng_ref, lnb_ref, wcat_ref, bs_ref, wout_ref, g_ref, mod_ref,
                     o_ref, vs_ref, gt_ref, *, tm):
    head_dim = C_WIDTH // C_HEADS
    head_of_lane = lax.broadcasted_iota(jnp.int32, (CHUNK, C_WIDTH), 1) // head_dim
    for ci in range(tm // CHUNK):
        rows = slice(ci * CHUNK, (ci + 1) * CHUNK)
        v = v_ref[0, rows, :]
        mu = jnp.mean(v, axis=-1, keepdims=True)
        cen = v - mu
        var = jnp.mean(cen * cen, axis=-1, keepdims=True)
        vn = (cen * lax.rsqrt(var + LN_EPS) * lng_ref[...] + lnb_ref[...]).astype(BF16)
        for h in range(C_HEADS):
            vs_ref[h * CHUNK:(h + 1) * CHUNK, :] = jnp.where(head_of_lane == h, vn, jnp.zeros_like(vn))
        mixed = _bdot(wcat_ref[...], vs_ref[...]) + bs_ref[...]
        gt_ref[rows, :] = (u_ref[0, rows, :] * mixed).astype(BF16)
    y = _bdot(gt_ref[...], wout_ref[0:C_WIDTH, :]) + _bdot(fo_ref[0], wout_ref[C_WIDTH:, :])
    o_ref[0] = x_ref[0] + mod_ref[0, 2:3, :] * _rms(y, g_ref[1:2, :])


def _gate_out(u, v, fo, x, ln_g, ln_b, w_s, b_s, w_out, g4, mods):
    bsz, length, d = x.shape
    tm = min(ROW_TILE, length)
    head_dim = C_WIDTH // C_HEADS
    wcat = w_s.transpose(1, 0, 2).reshape(CHUNK, C_HEADS * CHUNK).astype(BF16)
    bs_full = jnp.repeat(b_s.T, head_dim, axis=1)
    const = lambda b, i: (0, 0)
    return pl.pallas_call(
        functools.partial(_gate_out_kernel, tm=tm),
        grid=(bsz, length // tm),
        in_specs=[
            pl.BlockSpec((1, tm, C_WIDTH), lambda b, i: (b, i, 0)),
            pl.BlockSpec((1, tm, C_WIDTH), lambda b, i: (b, i, 0)),
            pl.BlockSpec((1, tm, D_WIDTH), lambda b, i: (b, i, 0)),
            pl.BlockSpec((1, tm, d), lambda b, i: (b, i, 0)),
            pl.BlockSpec((1, C_WIDTH), const),
            pl.BlockSpec((1, C_WIDTH), const),
            pl.BlockSpec((CHUNK, C_HEADS * CHUNK), const),
            pl.BlockSpec((CHUNK, C_WIDTH), const),
            pl.BlockSpec((C_WIDTH + D_WIDTH, d), const),
            pl.BlockSpec((4, d), const),
            pl.BlockSpec((1, 6, d), lambda b, i: (b, 0, 0)),
        ],
        out_specs=pl.BlockSpec((1, tm, d), lambda b, i: (b, i, 0)),
        out_shape=jax.ShapeDtypeStruct((bsz, length, d), F32),
        scratch_shapes=[
            pltpu.VMEM((C_HEADS * CHUNK, C_WIDTH), BF16),
            pltpu.VMEM((tm, C_WIDTH), BF16),
        ],
        compiler_params=_params(("parallel", "parallel")),
        name="gating_outproj",
    )(u, v, fo, x, ln_g.reshape(1, -1), ln_b.reshape(1, -1), wcat, bs_full, w_out, g4, mods)


def _router_kernel(x_ref, mod_ref, g_ref, wr_ref, tri_ref, xn_ref, meta_ref, cnt_ref, carry_ref, *, tm):
    first = jnp.logical_and(pl.program_id(0) == 0, pl.program_id(1) == 0)

    @pl.when(first)
    def _():
        carry_ref[...] = jnp.zeros_like(carry_ref)

    x = x_ref[0]
    h = _rms(x, g_ref[2:3, :]) * (1.0 + mod_ref[0, 4:5, :]) + mod_ref[0, 3:4, :]
    xn_ref[0] = h
    h_hi = h.astype(BF16)
    h_lo = (h - h_hi.astype(F32)).astype(BF16)
    logits = _bdot(jnp.concatenate([h_hi, h_hi, h_lo], axis=1), wr_ref[...])
    lane = lax.broadcasted_iota(jnp.int32, (tm, LANES), 1)
    neg = jnp.float32(-jnp.inf)
    lm = jnp.where(lane < N_EXPERTS, logits, neg)
    m1 = jnp.max(lm, axis=-1, keepdims=True)
    i1 = jnp.min(jnp.where(lm == m1, lane, LANES), axis=-1, keepdims=True)
    lm2 = jnp.where(lane == i1, neg, lm)
    m2 = jnp.max(lm2, axis=-1, keepdims=True)
    i2 = jnp.min(jnp.where(lm2 == m2, lane, LANES), axis=-1, keepdims=True)
    e = jnp.exp(m2 - m1)
    p1 = 1.0 / (1.0 + e)
    p2 = e / (1.0 + e)
    hot1 = (lane == i1).astype(F32)
    hot2 = (lane == i2).astype(F32)
    both = hot1 + hot2
    before = _bdot(tri_ref[...], both.astype(BF16)) + carry_ref[0:1, :]
    r1 = jnp.sum(before * hot1, axis=-1, keepdims=True)
    r2 = jnp.sum(before * hot2, axis=-1, keepdims=True)
    carry_ref[0:1, :] = carry_ref[0:1, :] + jnp.sum(both, axis=0, keepdims=True)
    cnt_ref[...] = carry_ref[...]
    vals = (i1.astype(F32), i2.astype(F32), p1, p2, r1, r2)
    meta = jnp.zeros((tm, LANES), F32)
    for k, val in enumerate(vals):
        meta = jnp.where(lane == k, val, meta)
    meta_ref[0] = meta


def _router(x, mods, g4, w_router):
    bsz, length, d = x.shape
    tm = min(ROW_TILE, length)
    wr = jnp.concatenate([w_router, jnp.zeros((d, LANES - N_EXPERTS), F32)], axis=1)
    wr_hi = wr.astype(BF16)
    wr_lo = (wr - wr_hi.astype(F32)).astype(BF16)
    wr = jnp.concatenate([wr_hi, wr_lo, wr_hi], axis=0)
    tri =jnp.asarray(np.tril(np.ones((tm, tm), np.float32), -1), dtype=BF16)
    const = lambda b, i: (0, 0)
    return pl.pallas_call(
        functools.partial(_router_kernel, tm=tm),
        grid=(bsz, length // tm),
        in_specs=[
            pl.BlockSpec((1, tm, d), lambda b, i: (b, i, 0)),
            pl.BlockSpec((1, 6, d), lambda b, i: (b, 0, 0)),
            pl.BlockSpec((4, d), const),
            pl.BlockSpec((3 * d, LANES), const),
            pl.BlockSpec((tm, tm), const),
        ],
        out_specs=[
            pl.BlockSpec((1, tm, d), lambda b, i: (b, i, 0)),
            pl.BlockSpec((1, tm, LANES), lambda b, i: (b, i, 0)),
            pl.BlockSpec((8, LANES), const),
        ],
        out_shape=[
            jax.ShapeDtypeStruct((bsz, length, d), F32),
            jax.ShapeDtypeStruct((bsz, length, LANES), F32),
            jax.ShapeDtypeStruct((8, LANES), F32),
        ],
        scratch_shapes=[pltpu.VMEM((8, LANES), F32)],
        compiler_params=_params(("arbitrary", "arbitrary")),
        name="moe_router",
    )(x, mods, g4, wr, tri)


EXPERT_FF_CHUNKS = 2


def _expert_kernel(pt_ref, pe_ref, p2_ref, pp_ref, np_ref, x_ref, w1_ref, w3_ref, w2_ref, o_ref, acc_ref):
    p = pl.program_id(0)
    j = pl.program_id(1)
    sub = pl.program_id(2)
    pc = jnp.maximum(jnp.minimum(p, np_ref[0] - 1), 0)
    active = jnp.logical_and(p < np_ref[0], jnp.logical_or(sub == 0, p2_ref[pc] == 1))

    @pl.when(active)
    def _():
        x = x_ref[...].astype(BF16)
        a = _bdot(x, w1_ref[0])
        b = _bdot(x, w3_ref[0])
        t = (jax.nn.silu(a) * b).astype(BF16)
        y = _bdot(t, w2_ref[0])

        @pl.when(j == 0)
        def _():
            acc_ref[sub] = y

        @pl.when(j > 0)
        def _():
            acc_ref[sub] += y

        @pl.when(j == pl.num_programs(1) - 1)
        def _():
            o_ref[...] = acc_ref[sub]


def _experts(x_sorted, schedule, w1, w3, w2):
    rows, d = x_sorted.shape
    dff = w1.shape[2]
    tm = EXPERT_ROW_TILE
    n_ff = EXPERT_FF_CHUNKS
    tf = dff // n_ff
    max_pairs = (rows // tm + N_EXPERTS) // 2

    def clamp(p, npairs):
        return jnp.maximum(jnp.minimum(p, npairs[0] - 1), 0)

    def tile_of(p, sub, pt, p2, npairs):
        pc = clamp(p, npairs)
        second = jnp.logical_and(p2[pc] == 1, jnp.logical_or(sub == 1, p >= npairs[0]))
        return pt[pc] + second.astype(jnp.int32)

    def x_map(p, j, sub, pt, pe, p2, pp, npairs):
        return (tile_of(p, sub, pt, p2, npairs), 0)

    def w13_map(p, j, sub, pt, pe, p2, pp, npairs):
        return (pe[clamp(p, npairs)], 0, jnp.where(p < npairs[0], j, n_ff - 1))

    def w2_map(p, j, sub, pt, pe, p2, pp, npairs):
        return (pe[clamp(p, npairs)], jnp.where(p < npairs[0], j, n_ff - 1), 0)

    def o_map(p, j, sub, pt, pe, p2, pp, npairs):
        storing = jnp.logical_or(j == n_ff - 1, p >= npairs[0])
        return (jnp.where(storing, tile_of(p, sub, pt, p2, npairs), pp[clamp(p, npairs)]), 0)

    return pl.pallas_call(
        _expert_kernel,
        grid_spec=pltpu.PrefetchScalarGridSpec(
            num_scalar_prefetch=5,
            grid=(max_pairs, n_ff, 2),
            in_specs=[
                pl.BlockSpec((tm, d), x_map),
                pl.BlockSpec((1, d, tf), w13_map),
                pl.BlockSpec((1, d, tf), w13_map),
                pl.BlockSpec((1, tf, d), w2_map),
            ],
            out_specs=pl.BlockSpec((tm, d), o_map),
            scratch_shapes=[pltpu.VMEM((2, tm, d), F32)],
        ),
        out_shape=jax.ShapeDtypeStruct((rows, d), F32),
        compiler_params=_params(("arbitrary", "arbitrary", "arbitrary")),
        name="expert_swiglu",
    )(*schedule, x_sorted, w1, w3, w2)


def _expert_schedule(padded, starts, tm, max_pairs):
    tiles = padded // tm
    pairs = (tiles + 1) // 2
    pair_ends = jnp.cumsum(pairs)
    pair_starts = pair_ends - pairs
    p_idx = jnp.arange(max_pairs, dtype=jnp.int32)
    expert = jnp.minimum(jnp.sum((p_idx[:, None] >= pair_ends[None, :]).astype(jnp.int32), axis=1), N_EXPERTS - 1)
    local = p_idx - pair_starts[expert]
    first_tile = (starts // tm)[expert] + 2 * local
    two = (2 * local + 1 < tiles[expert]).astype(jnp.int32)
    last_tile = first_tile + two
    prev = jnp.concatenate([first_tile[0:1], last_tile[:-1]])
    n_pairs = pair_ends[-1].astype(jnp.int32).reshape(1)
    return (first_tile.astype(jnp.int32), expert.astype(jnp.int32), two, prev.astype(jnp.int32), n_pairs)


def _combine_kernel(x_ref, y_ref, meta_ref, g_ref, mod_ref, o_ref):
    meta = meta_ref[0]
    y = meta[:, 2:3] * y_ref[0, 0] + meta[:, 3:4] * y_ref[1, 0]
    o_ref[0] = x_ref[0] + mod_ref[0, 5:6, :] * _rms(y, g_ref[3:4, :])


def _combine(x, y2, meta, g4, mods):
    bsz, length, d = x.shape
    tm = min(ROW_TILE, length)
    return pl.pallas_call(
        _combine_kernel,
        grid=(bsz, length // tm),
        in_specs=[
            pl.BlockSpec((1, tm, d), lambda b, i: (b, i, 0)),
            pl.BlockSpec((2, 1, tm, d), lambda b, i: (0, b, i, 0)),
            pl.BlockSpec((1, tm, LANES), lambda b, i: (b, i, 0)),
            pl.BlockSpec((4, d), lambda b, i: (0, 0)),
            pl.BlockSpec((1, 6, d), lambda b, i: (b, 0, 0)),
        ],
        out_specs=pl.BlockSpec((1, tm, d), lambda b, i: (b, i, 0)),
        out_shape=jax.ShapeDtypeStruct((bsz, length, d), F32),
        compiler_params=_params(("parallel", "parallel")),
        name="moe_combine",
    )(x, y2, meta, g4, mods)


SC_CORES = 2
SC_SUBCORES = 16
SC_WORKERS = SC_CORES * SC_SUBCORES
SC_ROW_CHUNK = 32


def _sc_mesh():
    return plsc.VectorSubcoreMesh(core_axis_name="c", subcore_axis_name="s")


def _sc_dispatch(rows_in, pos2, n_out):
    tokens, width = rows_in.shape
    per_worker = tokens // SC_WORKERS
    chunk = 2 * SC_ROW_CHUNK
    n_chunks = per_worker // chunk
    idx = pos2.reshape(2, SC_WORKERS, n_chunks, chunk)

    @functools.partial(
        pl.kernel, mesh=_sc_mesh(),
        out_type=jax.ShapeDtypeStruct((n_out, width), rows_in.dtype),
        scratch_types=[
            pltpu.VMEM((n_chunks, chunk), jnp.int32),
            pltpu.VMEM((n_chunks, chunk), jnp.int32),
            pltpu.VMEM((chunk, width), rows_in.dtype),
        ],
        name="sc_dispatch",
    )
    def body(rows_hbm, idx_hbm, out_hbm, idx0_v, idx1_v, rows_v):
        wid = lax.axis_index("s") * SC_CORES + lax.axis_index("c")
        base = wid * per_worker
        pltpu.sync_copy(idx_hbm.at[0, wid], idx0_v)
        pltpu.sync_copy(idx_hbm.at[1, wid], idx1_v)

        @pl.loop(0, n_chunks)
        def _(c):
            pltpu.sync_copy(rows_hbm.at[pl.ds(base + c * chunk, chunk)], rows_v)
            pltpu.sync_copy(rows_v, out_hbm.at[idx0_v.at[c]])
            pltpu.sync_copy(rows_v, out_hbm.at[idx1_v.at[c]])

    return body(rows_in, idx)


def _sc_gather(table, idx):
    n = idx.shape[0]
    width = table.shape[1]
    chunk = SC_ROW_CHUNK
    per_worker = n // SC_WORKERS
    n_pairs = per_worker // (2 * chunk)
    idx3 = idx.reshape(SC_WORKERS, 2 * n_pairs, chunk)

    @functools.partial(
        pl.kernel, mesh=_sc_mesh(),
        out_type=jax.ShapeDtypeStruct((n, width), table.dtype),
        scratch_types=[
            pltpu.VMEM((2 * n_pairs, chunk), jnp.int32),
            pltpu.VMEM((2, chunk, width), table.dtype),
            pltpu.SemaphoreType.DMA, pltpu.SemaphoreType.DMA,
            pltpu.SemaphoreType.DMA, pltpu.SemaphoreType.DMA,
        ],
        name="sc_gather",
    )
    def body(table_hbm, idx_hbm, out_hbm, idx_v, rows_v, gsem0, gsem1, wsem0, wsem1):
        wid = lax.axis_index("s") * SC_CORES + lax.axis_index("c")
        base = wid * per_worker
        gsem = (gsem0, gsem1)
        wsem = (wsem0, wsem1)
        pltpu.sync_copy(idx_hbm.at[wid], idx_v)

        @pl.loop(0, n_pairs)
        def _(p):
            c0 = 2 * p
            gathers = [pltpu.async_copy(table_hbm.at[idx_v.at[c0 + s]], rows_v.at[s], gsem[s]) for s in range(2)]
            writes = []
            for s in range(2):
                gathers[s].wait()
                rows_out = out_hbm.at[pl.ds(base + (c0 + s) * chunk, chunk)]
                writes.append(pltpu.async_copy(rows_v.at[s], rows_out, wsem[s]))
            for s in range(2):
                writes[s].wait()

    return body(table, idx3)


def _moe(x, mods, g4, w_router, w1, w3, w2):
    bsz, length, d = x.shape
    tokens = bsz * length
    tm = EXPERT_ROW_TILE
    xn, meta, counts = _router(x, mods, g4, w_router)
    meta2 = meta.reshape(tokens, LANES)
    experts = meta2[:, 0:2].astype(jnp.int32)
    ranks = meta2[:, 4:6].astype(jnp.int32)
    cnt = counts[0, 0:N_EXPERTS].astype(jnp.int32)
    padded = ((cnt + tm - 1) // tm) * tm
    ends = jnp.cumsum(padded)
    starts = ends - padded
    pos = (starts[experts] + ranks).T.reshape(2 * tokens)
    rows = 2 * tokens + N_EXPERTS * tm
    schedule = _expert_schedule(padded, starts, tm, (rows // tm + N_EXPERTS) // 2)
    x_sorted = _sc_dispatch(xn.reshape(tokens, d), pos.reshape(2, tokens), rows)
    y_sorted = _experts(x_sorted, schedule, w1, w3, w2)
    y2 = _sc_gather(y_sorted, pos).reshape(2, bsz, length, d)
    return _combine(x, y2, meta, g4, mods)


def _grid_pos_embed(rows, d):
    row = jnp.repeat(jnp.arange(rows, dtype=F32), GRID_W)
    col = jnp.tile(jnp.arange(GRID_W, dtype=F32), rows)
    n_freq = d // 4
    omega = POS_THETA ** (-jnp.arange(n_freq, dtype=F32) / n_freq)
    ang_r = row[:, None] * omega
    ang_c = col[:, None] * omega
    return jnp.concatenate([jnp.sin(ang_r), jnp.cos(ang_r), jnp.sin(ang_c), jnp.cos(ang_c)], axis=-1)


def kernel(x, c, ctx, c_ctx, ada_w, ada_b, norm_g, ab_w_in, rg_conv_w, rg_conv_b, rg_w_r, rg_b_r, rg_w_i, rg_b_i, rg_lambda, cv_dw_w, cv_dw_b, cv_ln_g, cv_ln_b, ab_w_out, ffn_w1, ffn_w3, ffn_w2, cd_w_in, sg_ln_g, sg_ln_b, sg_w_s, sg_b_s, cd_w_out, moe_router, moe_w1, moe_w3, moe_w2):
    bsz, length, d = x.shape
    assert ada_w.shape[0] == 2, "two layers: one even (RG-LRU | Conformer), one odd (gMLP | Fourier)"
    pos = _grid_pos_embed(length // GRID_W, d)

    cpad = jnp.concatenate([c, c_ctx[None, :], jnp.zeros((16 - bsz - 1, d), F32)], axis=0)
    m = _ada(cpad, ada_w, ada_b)
    mods = m[:, :bsz].reshape(2, bsz, 6, d)
    mods_ctx = m[:, bsz:bsz + 1].reshape(2, 1, 6, d)

    mods_t = mods[0].transpose(1, 0, 2)
    mods_ctx_t = jnp.broadcast_to(mods_ctx[0].reshape(6, 1, d), (6, bsz, d))
    pos3 = pos[:, None, :]
    w_in = ab_w_in[0].astype(BF16)
    xa3, gg3, glu3 = _inproj_t(x, pos3, mods_t, norm_g[0], w_in, rg_conv_w[0], rg_conv_b[0], True)
    (xc3,) = _inproj_t(ctx, None, mods_ctx_t, norm_g[0], w_in[:, 0:A_WIDTH], rg_conv_w[0], rg_conv_b[0], False)
    wg, bg, sp = _gate_weights(rg_w_r[0], rg_b_r[0], rg_w_i[0], rg_b_i[0], rg_lambda[0])
    rec_hi, rec_lo = _scan_t(xa3, xc3, gg3, wg, bg, sp)
    u3 = _conv_t(glu3, cv_dw_w[0], cv_dw_b[0], cv_ln_g[0], cv_ln_b[0])
    x1_t = _outproj_t(rec_lo, rec_hi, u3, x, pos3, ab_w_out[0].astype(BF16), norm_g[0], mods_t)
    x2 = _ffn_t(x1_t, mods_t, norm_g[0], ffn_w1[0].astype(BF16), ffn_w3[0].astype(BF16), ffn_w2[0].astype(BF16))

    u, v, f = _inproj_cd(x2, mods[1], norm_g[1], cd_w_in[0].astype(BF16))
    fo = _fourier(f)
    x3 = _gate_out(u, v, fo, x2, sg_ln_g[0], sg_ln_b[0], sg_w_s[0], sg_b_s[0], cd_w_out[0].astype(BF16),
                   norm_g[1], mods[1])
    return _moe(x3, mods[1], norm_g[1], moe_router[0], moe_w1[0].astype(BF16), moe_w3[0].astype(BF16),
                moe_w2[0].astype(BF16))
```

```python
import functools

import numpy as np
import jax
import jax.numpy as jnp
from jax import lax
from jax.experimental import pallas as pl
from jax.experimental.pallas import tpu as pltpu
from jax.experimental.pallas import tpu_sc as plsc

F32 = jnp.float32
BF16 = jnp.bfloat16

GRID_W = 64
POS_THETA = 10000.0
NORM_EPS = 1e-6
LN_EPS = 1e-5
LRU_C = 8.0

A_WIDTH = 512
A_HEADS = 8
A_CONV_W = 4
B_WIDTH = 512
B_CONV_W = 31
C_WIDTH = 512
C_HEADS = 8
CHUNK = 128
D_WIDTH = 512
D_GROUPS = 4
D_GROUP_DIM = D_WIDTH // D_GROUPS
N_EXPERTS = 8

LANES = 128
ROW_TILE = 512
EXPERT_ROW_TILE = 512
VMEM_LIMIT = 56 * 2 ** 20


def _params(sem):
    return pltpu.CompilerParams(dimension_semantics=sem, vmem_limit_bytes=VMEM_LIMIT)


def _rms(x, g):
    return x * lax.rsqrt(jnp.mean(x * x, axis=-1, keepdims=True) + NORM_EPS) * g


def _bdot(a, b):
    return jnp.dot(a, b, preferred_element_type=F32)


def _ada_kernel(c_ref, w_ref, b_ref, o_ref):
    s = jax.nn.silu(c_ref[...])
    o_ref[0] = _bdot(s.astype(BF16), w_ref[0].astype(BF16)) + b_ref[0]


def _ada(cpad, ada_w, ada_b):
    nl, d, n6 = ada_w.shape
    rows = cpad.shape[0]
    tn = n6 // 4
    return pl.pallas_call(
        _ada_kernel,
        grid=(nl, n6 // tn),
        in_specs=[
            pl.BlockSpec((rows, d), lambda l, j: (0, 0)),
            pl.BlockSpec((1, d, tn), lambda l, j: (l, 0, j)),
            pl.BlockSpec((1, 1, tn), lambda l, j: (l, 0, j)),
        ],
        out_specs=pl.BlockSpec((1, rows, tn), lambda l, j: (l, 0, j)),
        out_shape=jax.ShapeDtypeStruct((nl, rows, n6), F32),
        compiler_params=_params(("arbitrary", "arbitrary")),
        name="ada_terms",
    )(cpad, ada_w, ada_b.reshape(nl, 1, n6))


TIME_TILE = 128
SCAN_TIME_TILE = 128
SCAN_LANES = 256
CONV_TIME_TILE = 64
CONV_HALO = 16


def _time_major(ref):
    return jnp.swapaxes(ref[...], 0, 1)


def _inproj_t_kernel(*refs, tt, add_pos, branches):
    if add_pos:
        x_ref, xp_ref, xn_ref, pos_ref, posp_ref, posn_ref, mod_ref, g_ref, w_ref, cw_ref, cb_ref, *outs = refs
        xall = jnp.concatenate(
            [_time_major(xp_ref)[7:8] + posp_ref[...], _time_major(x_ref) + pos_ref[...],
             _time_major(xn_ref)[0:2] + posn_ref[...]], axis=0)
    else:
        x_ref, xp_ref, xn_ref, mod_ref, g_ref, w_ref, cw_ref, cb_ref, *outs = refs
        xall = jnp.concatenate([_time_major(xp_ref)[7:8], _time_major(x_ref), _time_major(xn_ref)[0:2]], axis=0)
    i = pl.program_id(0)
    last = pl.num_programs(0) - 1
    steps, bsz, d = xall.shape
    h = _rms(xall, g_ref[0:1, :]) * (1.0 + mod_ref[1]) + mod_ref[0]
    z = _bdot(h.reshape(steps * bsz, d).astype(BF16), w_ref[...].astype(BF16))
    z3 = z.reshape(steps, bsz, z.shape[-1])
    t_idx = lax.broadcasted_iota(jnp.int32, (steps, bsz, A_WIDTH), 0)
    inside = jnp.logical_and(jnp.logical_or(t_idx >= 1, i > 0), jnp.logical_or(t_idx <= tt, i < last))
    xa = jnp.where(inside, z3[:, :, 0:A_WIDTH], 0.0)
    xconv = cb_ref[...] + cw_ref[0:1, :] * xa[0:tt]
    for k in range(1, A_CONV_W):
        xconv = xconv + cw_ref[k:k + 1, :] * xa[k:k + tt]
    outs[0][...] = xconv
    if branches:
        outs[1][...] = jax.nn.gelu(z3[1:tt + 1, :, A_WIDTH:2 * A_WIDTH])
        vb = z3[1:tt + 1, :, 2 * A_WIDTH:2 * A_WIDTH + B_WIDTH]
        gb = z3[1:tt + 1, :, 2 * A_WIDTH + B_WIDTH:]
        outs[2][...] = vb * jax.nn.sigmoid(gb)


def _inproj_t(x, pos3, mods_t, g4, w, conv_w, conv_b, branches):
    bsz, length, d = x.shape
    n = w.shape[1]
    tt = min(TIME_TILE, length)
    add_pos = pos3 is not None
    n_out = 3 if branches else 1
    prev_map = lambda i: (jnp.maximum(i * tt - 1, 0), 0, 0)
    next_map = lambda i: (jnp.minimum((i + 1) * (tt // 2), length // 2 - 1), 0, 0)
    in_specs = [
        pl.BlockSpec((bsz, tt, d), lambda i: (0, i, 0)),
        pl.BlockSpec((bsz, 8, d), lambda i: (0, jnp.maximum(i * (tt // 8) - 1, 0), 0)),
        pl.BlockSpec((bsz, 8, d), lambda i: (0, jnp.minimum((i + 1) * (tt // 8), length // 8 - 1), 0)),
    ]
    args = [x, x, x]
    if add_pos:
        in_specs += [
            pl.BlockSpec((tt, 1, d), lambda i: (i, 0, 0)),
            pl.BlockSpec((1, 1, d), prev_map),
            pl.BlockSpec((2, 1, d), next_map),
        ]
        args += [pos3, pos3, pos3]
    in_specs += [
        pl.BlockSpec((6, bsz, d), lambda i: (0, 0, 0)),
        pl.BlockSpec((4, d), lambda i: (0, 0)),
        pl.BlockSpec((d, n), lambda i: (0, 0)),
        pl.BlockSpec((A_CONV_W, A_WIDTH), lambda i: (0, 0)),
        pl.BlockSpec((1, A_WIDTH), lambda i: (0, 0)),
    ]
    args += [mods_t, g4, w, conv_w, conv_b.reshape(1, A_WIDTH)]
    return pl.pallas_call(
        functools.partial(_inproj_t_kernel, tt=tt, add_pos=add_pos, branches=branches),
        grid=(length // tt,),
        in_specs=in_specs,
        out_specs=[pl.BlockSpec((tt, bsz, A_WIDTH), lambda i: (i, 0, 0))] * n_out,
        out_shape=[jax.ShapeDtypeStruct((length, bsz, A_WIDTH), F32)] * n_out,
        compiler_params=_params(("parallel",)),
        name="inproj_ab" if branches else "inproj_ctx",
    )(*args)


def _cast_riders(cast_in, cast_out):
    for src, dst in zip(cast_in, cast_out):
        dst[...] = src[...].astype(BF16)


def _rider_specs(arrays, n_steps, step_of):
    in_specs, out_specs, out_shapes = [], [], []
    for arr in arrays:
        rows, cols = arr.shape
        block = (rows // n_steps, cols)
        in_specs.append(pl.BlockSpec(block, lambda *idx: (step_of(*idx), 0)))
        out_specs.append(pl.BlockSpec(block, lambda *idx: (step_of(*idx), 0)))
        out_shapes.append(jax.ShapeDtypeStruct(arr.shape, BF16))
    return in_specs, out_specs, out_shapes


def _scan_t_kernel(*refs, tb, nb, ctx_len, n_cast):
    xf_ref, xb_ref, ggf_ref, ggb_ref, xc_ref, wg_ref, bg_ref, sp_ref = refs[:8]
    cast_in = refs[8:8 + n_cast]
    hi_ref, lo_ref = refs[8 + n_cast:10 + n_cast]
    cast_out = refs[10 + n_cast:10 + 2 * n_cast]
    a_ref, b_ref, h_ref, state_ref = refs[10 + 2 * n_cast:]
    _cast_riders(cast_in, cast_out)
    k = pl.program_id(1)
    half = nb // 2
    bsz, lanes = xf_ref.shape[1], xf_ref.shape[2]
    chunk = 32

    def coefficients(x_ref, n, direction):
        def body(c, _):
            t0 = pl.multiple_of(c * chunk, chunk)
            x = x_ref[pl.ds(t0, chunk)].reshape(chunk * bsz, lanes)
            g = _bdot(x.astype(BF16), wg_ref[0, direction]) + bg_ref[0, direction]
            gate = 0.5 * jnp.tanh(0.5 * g) + 0.5
            r = gate[:, 0:lanes]
            i = gate[:, lanes:]
            log_a = (-LRU_C * r) * sp_ref[0, direction:direction + 1, :]
            a = jnp.exp(log_a)
            one_minus_a2 = -jnp.tanh(log_a) * (a * a + 1.0)
            a_ref[direction, pl.ds(t0, chunk)] = a.reshape(chunk, bsz, lanes)
            b_ref[direction, pl.ds(t0, chunk)] = (jnp.sqrt(one_minus_a2) * (i * x)).reshape(chunk, bsz, lanes)
            return 0
        lax.fori_loop(0, n // chunk, body, 0)

    def sweep(n, store):
        def body(t, carry):
            hf, hb = carry
            tr = n - 1 - t
            hf = a_ref[0, pl.ds(t, 1)][0] * hf + b_ref[0, pl.ds(t, 1)][0]
            hb = a_ref[1, pl.ds(tr, 1)][0] * hb + b_ref[1, pl.ds(tr, 1)][0]
            if store:
                b_ref[0, pl.ds(t, 1)] = hf[None]
                b_ref[1, pl.ds(tr, 1)] = hb[None]
            return hf, hb
        hf, hb = lax.fori_loop(0, n, body, (state_ref[0], state_ref[1]), unroll=8)
        state_ref[0] = hf
        state_ref[1] = hb

    @pl.when(k == 0)
    def _():
        state_ref[...] = jnp.zeros_like(state_ref)
        coefficients(xc_ref, ctx_len, 0)
        coefficients(xc_ref, ctx_len, 1)
        sweep(ctx_len, False)

    coefficients(xf_ref, tb, 0)
    coefficients(xb_ref, tb, 1)
    sweep(tb, True)
    m = nb - 1 - k

    @pl.when(k < half)
    def _():
        h_ref[k] = b_ref[0, 0:tb]
        h_ref[m] = b_ref[1, 0:tb]

    @pl.when(k >= half)
    def _():
        hi_ref[...] = (b_ref[0, 0:tb] + h_ref[k]) * ggf_ref[...]
        lo_ref[...] = (h_ref[m] + b_ref[1, 0:tb]) * ggb_ref[...]


def _scan_t(xa3, xc3, gg3, wg, bg, sp, riders=()):
    length, bsz, width = xa3.shape
    ctx_len = xc3.shape[0]
    tb = min(SCAN_TIME_TILE, length // 2)
    nb = length // tb
    half = nb // 2
    lanes = SCAN_LANES
    blk = (tb, bsz, lanes)
    r_in, r_out, r_shapes = _rider_specs(riders, (width // lanes) * nb, lambda g, k: g * nb + k)
    return pl.pallas_call(
        functools.partial(_scan_t_kernel, tb=tb, nb=nb, ctx_len=ctx_len, n_cast=len(riders)),
        grid=(width // lanes, nb),
        in_specs=[
            pl.BlockSpec(blk, lambda g, k: (k, 0, g)),
            pl.BlockSpec(blk, lambda g, k: (nb - 1 - k, 0, g)),
            pl.BlockSpec(blk, lambda g, k: (jnp.maximum(k, half), 0, g)),
            pl.BlockSpec(blk, lambda g, k: (jnp.minimum(nb - 1 - k, half - 1), 0, g)),
            pl.BlockSpec((ctx_len, bsz, lanes), lambda g, k: (0, 0, g)),
            pl.BlockSpec((1, 2, lanes, 2 * lanes), lambda g, k: (g, 0, 0, 0)),
            pl.BlockSpec((1, 2, 1, 2 * lanes), lambda g, k: (g, 0, 0, 0)),
            pl.BlockSpec((1, 2, lanes), lambda g, k: (g, 0, 0)),
        ] + r_in,
        out_specs=[
            pl.BlockSpec(blk, lambda g, k: (jnp.maximum(k - half, 0), 0, g)),
            pl.BlockSpec(blk, lambda g, k: (jnp.minimum(nb - 1 - k, half - 1), 0, g)),
        ] + r_out,
        out_shape=[jax.ShapeDtypeStruct((length // 2, bsz, width), F32)] * 2 + r_shapes,
        scratch_shapes=[
            pltpu.VMEM((2, max(tb, ctx_len), bsz, lanes), F32),
            pltpu.VMEM((2, max(tb, ctx_len), bsz, lanes), F32),
            pltpu.VMEM((nb, tb, bsz, lanes), F32),
            pltpu.VMEM((2, bsz, lanes), F32),
        ],
        compiler_params=_params(("parallel", "arbitrary")),
        name="rglru_scan",
    )(xa3, xa3, gg3, gg3, xc3, wg, bg, sp, *riders)


def _gate_weights(w_r, b_r, w_i, b_i, lam):
    hd = A_WIDTH // A_HEADS
    heads_per_group = SCAN_LANES // hd
    ngroups = A_WIDTH // SCAN_LANES

    def blockdiag(w):
        w = w.reshape(ngroups, heads_per_group, hd, hd)
        eye = jnp.eye(heads_per_group, dtype=w.dtype)
        return jnp.einsum("ghde,hk->ghdke", w, eye).reshape(ngroups, SCAN_LANES, SCAN_LANES)

    wg = jnp.stack([jnp.concatenate([blockdiag(w_r[d]), blockdiag(w_i[d])], axis=-1) for d in range(2)], axis=1)
    bg = jnp.stack([jnp.concatenate([b_r[d].reshape(ngroups, 1, SCAN_LANES), b_i[d].reshape(ngroups, 1, SCAN_LANES)],
                                    axis=-1) for d in range(2)], axis=1)
    sp = jax.nn.softplus(-lam.astype(F32)).reshape(2, ngroups, SCAN_LANES).transpose(1, 0, 2)
    return wg.astype(BF16), bg, sp


def _conv_t_kernel(*refs, tb, n_cast):
    x_ref, xp_ref, xn_ref, w_ref, b_ref, lng_ref, lnb_ref = refs[:7]
    cast_in = refs[7:7 + n_cast]
    o_ref = refs[7 + n_cast]
    cast_out = refs[8 + n_cast:8 + 2 * n_cast]
    stage_ref = refs[-1]
    _cast_riders(cast_in, cast_out)
    i = pl.program_id(0)
    last = pl.num_programs(0) - 1
    halo = CONV_HALO
    stage_ref[0:halo] = jnp.where(i > 0, xp_ref[...], 0.0)
    stage_ref[halo:halo + tb] = x_ref[...]
    stage_ref[halo + tb:2 * halo + tb] = jnp.where(i < last, xn_ref[...], 0.0)
    sub = 8

    def body(r, _):
        t0 = pl.multiple_of(r * sub, sub)
        def tap(k, acc):
            return acc + w_ref[k] * stage_ref[pl.ds(t0 + 1 + k, sub)]
        acc = lax.fori_loop(0, B_CONV_W + 1, tap, jnp.broadcast_to(b_ref[...], (sub,) + b_ref.shape), unroll=8)
        o_ref[pl.ds(t0, sub)] = acc
        return 0
    lax.fori_loop(0, tb // sub, body, 0)

    norm_rows = 16

    def norm_body(r, _):
        t0 = pl.multiple_of(r * norm_rows, norm_rows)
        acc = o_ref[pl.ds(t0, norm_rows)]
        mu = jnp.mean(acc, axis=-1, keepdims=True)
        cen = acc - mu
        var = jnp.mean(cen * cen, axis=-1, keepdims=True)
        y = cen * lax.rsqrt(var + LN_EPS) * lng_ref[...] + lnb_ref[...]
        o_ref[pl.ds(t0, norm_rows)] = jax.nn.silu(y)
        return 0
    lax.fori_loop(0, tb // norm_rows, norm_body, 0)


def _conv_t(glu3, dw_w, dw_b, ln_g, ln_b, riders=()):
    length, bsz, width = glu3.shape
    tb = min(CONV_TIME_TILE, length)
    halo = CONV_HALO
    taps = dw_w.shape[0]
    w8 = jnp.broadcast_to(jnp.concatenate([dw_w, jnp.zeros((32 - taps, width), F32)], axis=0)[:, None, :],
                          (32, bsz, width))
    b8 = jnp.broadcast_to(dw_b[None, :], (bsz, width))
    r_in, r_out, r_shapes = _rider_specs(riders, length // tb, lambda i: i)
    return pl.pallas_call(
        functools.partial(_conv_t_kernel, tb=tb, n_cast=len(riders)),
        grid=(length // tb,),
        in_specs=[
            pl.BlockSpec((tb, bsz, width), lambda i: (i, 0, 0)),
            pl.BlockSpec((halo, bsz, width), lambda i: (jnp.maximum(i * (tb // halo) - 1, 0), 0, 0)),
            pl.BlockSpec((halo, bsz, width), lambda i: (jnp.minimum((i + 1) * (tb // halo), length // halo - 1), 0, 0)),
            pl.BlockSpec((32, bsz, width), lambda i: (0, 0, 0)),
            pl.BlockSpec((bsz, width), lambda i: (0, 0)),
            pl.BlockSpec((1, width), lambda i: (0, 0)),
            pl.BlockSpec((1, width), lambda i: (0, 0)),
        ] + r_in,
        out_specs=[pl.BlockSpec((tb, bsz, width), lambda i: (i, 0, 0))] + r_out,
        out_shape=[jax.ShapeDtypeStruct((length, bsz, width), F32)] + r_shapes,
        scratch_shapes=[pltpu.VMEM((tb + 2 * halo, bsz, width), F32)],
        compiler_params=_params(("parallel",)),
        name="conformer_conv",
    )(glu3, glu3, glu3, w8, b8, ln_g.reshape(1, -1), ln_b.reshape(1, -1), *riders)


def _outproj_t_kernel(lo_ref, hi_ref, u_ref, x_ref, pos_ref, wout_ref, g_ref, mod_ref, o_ref, *, half_steps):
    i = pl.program_id(0)
    bsz, tt, d = x_ref.shape
    rec = jnp.where(i < half_steps, lo_ref[...], hi_ref[...])
    rows = tt * bsz
    y = (_bdot(rec.reshape(rows, A_WIDTH).astype(BF16), wout_ref[0:A_WIDTH, :].astype(BF16))
         + _bdot(u_ref[...].reshape(rows, B_WIDTH).astype(BF16), wout_ref[A_WIDTH:, :].astype(BF16)))
    yn = _rms(y, g_ref[1:2, :]).reshape(tt, bsz, d)
    o_ref[...] = _time_major(x_ref) + pos_ref[...] + mod_ref[2] * yn


def _outproj_t(rec_lo, rec_hi, u3, x, pos3, w_out, g4, mods_t):
    bsz, length, d = x.shape
    tt = min(TIME_TILE // 2, length // 2)
    half_steps = (length // 2) // tt
    return pl.pallas_call(
        functools.partial(_outproj_t_kernel, half_steps=half_steps),
        grid=(length // tt,),
        in_specs=[
            pl.BlockSpec((tt, bsz, A_WIDTH), lambda i: (jnp.minimum(i, half_steps - 1), 0, 0)),
            pl.BlockSpec((tt, bsz, A_WIDTH), lambda i: (jnp.maximum(i - half_steps, 0), 0, 0)),
            pl.BlockSpec((tt, bsz, B_WIDTH), lambda i: (i, 0, 0)),
            pl.BlockSpec((bsz, tt, d), lambda i: (0, i, 0)),
            pl.BlockSpec((tt, 1, d), lambda i: (i, 0, 0)),
            pl.BlockSpec((A_WIDTH + B_WIDTH, d), lambda i: (0, 0)),
            pl.BlockSpec((4, d), lambda i: (0, 0)),
            pl.BlockSpec((6, bsz, d), lambda i: (0, 0, 0)),
        ],
        out_specs=pl.BlockSpec((tt, bsz, d), lambda i: (i, 0, 0)),
        out_shape=jax.ShapeDtypeStruct((length, bsz, d), F32),
        compiler_params=_params(("parallel",)),
        name="mixer_ab_outproj",
    )(rec_lo, rec_hi, u3, x, pos3, w_out, g4, mods_t)


def _ffn_t_kernel(x_ref, mod_ref, g_ref, w1_ref, w3_ref, w2_ref, o_ref):
    x = x_ref[...]
    tt, bsz, d = x.shape
    h = (_rms(x, g_ref[2:3, :]) * (1.0 + mod_ref[4]) + mod_ref[3]).reshape(tt * bsz, d).astype(BF16)
    a = _bdot(h, w1_ref[...])
    b = _bdot(h, w3_ref[...])
    t = (jax.nn.silu(a) * b).astype(BF16)
    y = _bdot(t, w2_ref[...])
    o_ref[...] = jnp.swapaxes(x + mod_ref[5] * _rms(y, g_ref[3:4, :]).reshape(tt, bsz, d), 0, 1)


def _ffn_t(x3, mods_t, g4, w1, w3, w2):
    length, bsz, d = x3.shape
    dff = w1.shape[1]
    tt = min(TIME_TILE // 2, length)
    return pl.pallas_call(
        _ffn_t_kernel,
        grid=(length // tt,),
        in_specs=[
            pl.BlockSpec((tt, bsz, d), lambda i: (i, 0, 0)),
            pl.BlockSpec((6, bsz, d), lambda i: (0, 0, 0)),
            pl.BlockSpec((4, d), lambda i: (0, 0)),
            pl.BlockSpec((d, dff), lambda i: (0, 0), pipeline_mode=pl.Buffered(1)),
            pl.BlockSpec((d, dff), lambda i: (0, 0), pipeline_mode=pl.Buffered(1)),
            pl.BlockSpec((dff, d), lambda i: (0, 0), pipeline_mode=pl.Buffered(1)),
        ],
        out_specs=pl.BlockSpec((bsz, tt, d), lambda i: (0, i, 0)),
        out_shape=jax.ShapeDtypeStruct((bsz, length, d), F32),
        compiler_params=_params(("parallel",)),
        name="dense_swiglu",
    )(x3, mods_t, g4, w1, w3, w2)


def _inproj_cd_kernel(x_ref, mod_ref, g_ref, w_ref, u_ref, v_ref, f_ref):
    x = x_ref[0]
    h = _rms(x, g_ref[0:1, :]) * (1.0 + mod_ref[0, 1:2, :]) + mod_ref[0, 0:1, :]
    z = _bdot(h.astype(BF16), w_ref[...].astype(BF16))
    u_ref[0] = jax.nn.gelu(z[:, 0:C_WIDTH])
    v_ref[0] = jax.nn.gelu(z[:, C_WIDTH:2 * C_WIDTH])
    f_ref[0] = z[:, 2 * C_WIDTH:]


def _inproj_cd(x, mods, g4, w):
    bsz, length, d = x.shape
    n = w.shape[1]
    tm = min(ROW_TILE, length)
    return pl.pallas_call(
        _inproj_cd_kernel,
        grid=(bsz, length // tm),
        in_specs=[
            pl.BlockSpec((1, tm, d), lambda b, i: (b, i, 0)),
            pl.BlockSpec((1, 6, d), lambda b, i: (b, 0, 0)),
            pl.BlockSpec((4, d), lambda b, i: (0, 0)),
            pl.BlockSpec((d, n), lambda b, i: (0, 0)),
        ],
        out_specs=[pl.BlockSpec((1, tm, C_WIDTH), lambda b, i: (b, i, 0))] * 3,
        out_shape=[jax.ShapeDtypeStruct((bsz, length, C_WIDTH), F32)] * 3,
        compiler_params=_params(("parallel", "parallel")),
        name="inproj_cd",
    )(x, mods, g4, w)


def _fourier_kernel(f_ref, dmat_ref, lmat_ref, o_ref, z_ref, *, length, scale):
    j = pl.program_id(1)
    rows_per_step = 512 if length % 512 == 0 else length

    @pl.when(j == 0)
    def _():
        def body(c, _):
            rows = pl.ds(pl.multiple_of(c * rows_per_step, rows_per_step), rows_per_step)
            fb = f_ref[0, rows, :].astype(BF16)
            for g in range(D_GROUPS):
                cs = _bdot(fb[:, g * D_GROUP_DIM:(g + 1) * D_GROUP_DIM], dmat_ref[...])
                z_ref[0, rows, g * D_GROUP_DIM:(g + 1) * D_GROUP_DIM] = cs[:, 0:D_GROUP_DIM].astype(BF16)
                z_ref[1, rows, g * D_GROUP_DIM:(g + 1) * D_GROUP_DIM] = cs[:, D_GROUP_DIM:].astype(BF16)
            return 0
        lax.fori_loop(0, length // rows_per_step, body, 0)

    out = _bdot(lmat_ref[0], z_ref[0]) + _bdot(lmat_ref[1], z_ref[1])
    o_ref[0] = (out * scale).astype(BF16)


def _dft_matrices(length):
    k = np.arange(length, dtype=np.int64)
    ang_l = 2.0 * np.pi * ((k[:, None] * k[None, :]) % length).astype(np.float64) / length
    lmat = np.stack([np.cos(ang_l), -np.sin(ang_l)], axis=0)
    d = np.arange(D_GROUP_DIM, dtype=np.int64)
    ang_d = 2.0 * np.pi * ((d[:, None] * d[None, :]) % D_GROUP_DIM).astype(np.float64) / D_GROUP_DIM
    dmat = np.concatenate([np.cos(ang_d), np.sin(ang_d)], axis=1)
    return jnp.asarray(lmat, dtype=F32).astype(BF16), jnp.asarray(dmat, dtype=F32).astype(BF16)


def _fourier(f):
    bsz, length, width = f.shape
    tm = min(ROW_TILE, length)
    lmat, dmat = _dft_matrices(length)
    scale = float(1.0 / np.sqrt(length * D_GROUP_DIM))
    return pl.pallas_call(
        functools.partial(_fourier_kernel, length=length, scale=scale),
        grid=(bsz, length // tm),
        in_specs=[
            pl.BlockSpec((1, length, width), lambda b, i: (b, 0, 0)),
            pl.BlockSpec((D_GROUP_DIM, 2 * D_GROUP_DIM), lambda b, i: (0, 0)),
            pl.BlockSpec((2, tm, length), lambda b, i: (0, i, 0)),
        ],
        out_specs=pl.BlockSpec((1, tm, width), lambda b, i: (b, i, 0)),
        out_shape=jax.ShapeDtypeStruct((bsz, length, width), BF16),
        scratch_shapes=[pltpu.VMEM((2, length, width), BF16)],
        compiler_params=_params(("parallel", "arbitrary")),
        name="fourier_mix",
    )(f, dmat, lmat)


def _gate_out_kernel(u_ref, v_ref, fo_ref, x_ref, lng_ref, lnb_ref, wcat_ref, bs_ref, wout_ref, g_ref, mod_ref,
                     o_ref, vs_ref, gt_ref, *, tm):
    head_dim = C_WIDTH // C_HEADS
    head_of_lane = lax.broadcasted_iota(jnp.int32, (CHUNK, C_WIDTH), 1) // head_dim
    for ci in range(tm // CHUNK):
        rows = slice(ci * CHUNK, (ci + 1) * CHUNK)
        v = v_ref[0, rows, :]
        mu = jnp.mean(v, axis=-1, keepdims=True)
        cen = v - mu
        var = jnp.mean(cen * cen, axis=-1, keepdims=True)
        vn = (cen * lax.rsqrt(var + LN_EPS) * lng_ref[...] + lnb_ref[...]).astype(BF16)
        for h in range(C_HEADS):
            vs_ref[h * CHUNK:(h + 1) * CHUNK, :] = jnp.where(head_of_lane == h, vn, jnp.zeros_like(vn))
        mixed = _bdot(wcat_ref[...], vs_ref[...]) + bs_ref[...]
        gt_ref[rows, :] = (u_ref[0, rows, :] * mixed).astype(BF16)
    y = (_bdot(gt_ref[...], wout_ref[0:C_WIDTH, :].astype(BF16))
         + _bdot(fo_ref[0], wout_ref[C_WIDTH:, :].astype(BF16)))
    o_ref[0] = x_ref[0] + mod_ref[0, 2:3, :] * _rms(y, g_ref[1:2, :])


def _gate_out(u, v, fo, x, ln_g, ln_b, w_s, b_s, w_out, g4, mods):
    bsz, length, d = x.shape
    tm = min(ROW_TILE, length)
    head_dim = C_WIDTH // C_HEADS
    wcat = w_s.transpose(1, 0, 2).reshape(CHUNK, C_HEADS * CHUNK).astype(BF16)
    bs_full = jnp.repeat(b_s.T, head_dim, axis=1)
    const = lambda b, i: (0, 0)
    return pl.pallas_call(
        functools.partial(_gate_out_kernel, tm=tm),
        grid=(bsz, length // tm),
        in_specs=[
            pl.BlockSpec((1, tm, C_WIDTH), lambda b, i: (b, i, 0)),
            pl.BlockSpec((1, tm, C_WIDTH), lambda b, i: (b, i, 0)),
            pl.BlockSpec((1, tm, D_WIDTH), lambda b, i: (b, i, 0)),
            pl.BlockSpec((1, tm, d), lambda b, i: (b, i, 0)),
            pl.BlockSpec((1, C_WIDTH), const),
            pl.BlockSpec((1, C_WIDTH), const),
            pl.BlockSpec((CHUNK, C_HEADS * CHUNK), const),
            pl.BlockSpec((CHUNK, C_WIDTH), const),
            pl.BlockSpec((C_WIDTH + D_WIDTH, d), const),
            pl.BlockSpec((4, d), const),
            pl.BlockSpec((1, 6, d), lambda b, i: (b, 0, 0)),
        ],
        out_specs=pl.BlockSpec((1, tm, d), lambda b, i: (b, i, 0)),
        out_shape=jax.ShapeDtypeStruct((bsz, length, d), F32),
        scratch_shapes=[
            pltpu.VMEM((C_HEADS * CHUNK, C_WIDTH), BF16),
            pltpu.VMEM((tm, C_WIDTH), BF16),
        ],
        compiler_params=_params(("parallel", "parallel")),
        name="gating_outproj",
    )(u, v, fo, x, ln_g.reshape(1, -1), ln_b.reshape(1, -1), wcat, bs_full, w_out, g4, mods)


def _router_kernel(x_ref, mod_ref, g_ref, wr_ref, tri_ref, xn_ref, meta_ref, cnt_ref, carry_ref, *, tm):
    first = jnp.logical_and(pl.program_id(0) == 0, pl.program_id(1) == 0)

    @pl.when(first)
    def _():
        carry_ref[...] = jnp.zeros_like(carry_ref)

    x = x_ref[0]
    h = _rms(x, g_ref[2:3, :]) * (1.0 + mod_ref[0, 4:5, :]) + mod_ref[0, 3:4, :]
    xn_ref[0] = h
    h_hi = h.astype(BF16)
    h_lo = (h - h_hi.astype(F32)).astype(BF16)
    logits = _bdot(jnp.concatenate([h_hi, h_hi, h_lo], axis=1), wr_ref[...])
    lane = lax.broadcasted_iota(jnp.int32, (tm, LANES), 1)
    neg = jnp.float32(-jnp.inf)
    lm = jnp.where(lane < N_EXPERTS, logits, neg)
    m1 = jnp.max(lm, axis=-1, keepdims=True)
    i1 = jnp.min(jnp.where(lm == m1, lane, LANES), axis=-1, keepdims=True)
    lm2 = jnp.where(lane == i1, neg, lm)
    m2 = jnp.max(lm2, axis=-1, keepdims=True)
    i2 = jnp.min(jnp.where(lm2 == m2, lane, LANES), axis=-1, keepdims=True)
    e = jnp.exp(m2 - m1)
    p1 = 1.0 / (1.0 + e)
    p2 = e / (1.0 + e)
    hot1 = (lane == i1).astype(F32)
    hot2 = (lane == i2).astype(F32)
    both = hot1 + hot2
    before = _bdot(tri_ref[...], both.astype(BF16)) + carry_ref[0:1, :]
    r1 = jnp.sum(before * hot1, axis=-1, keepdims=True)
    r2 = jnp.sum(before * hot2, axis=-1, keepdims=True)
    carry_ref[0:1, :] = carry_ref[0:1, :] + jnp.sum(both, axis=0, keepdims=True)
    cnt_ref[...] = carry_ref[...]
    vals = (i1.astype(F32), i2.astype(F32), p1, p2, r1, r2)
    meta = jnp.zeros((tm, LANES), F32)
    for k, val in enumerate(vals):
        meta = jnp.where(lane == k, val, meta)
    meta_ref[0] = meta


def _router(x, mods, g4, w_router):
    bsz, length, d = x.shape
    tm = min(ROW_TILE, length)
    wr = jnp.concatenate([w_router, jnp.zeros((d, LANES - N_EXPERTS), F32)], axis=1)
    wr_hi = wr.astype(BF16)
    wr_lo = (wr - wr_hi.astype(F32)).astype(BF16)
    wr = jnp.concatenate([wr_hi, wr_lo, wr_hi], axis=0)
    tri =jnp.asarray(np.tril(np.ones((tm, tm), np.float32), -1), dtype=BF16)
    const = lambda b, i: (0, 0)
    return pl.pallas_call(
        functools.partial(_router_kernel, tm=tm),
        grid=(bsz, length // tm),
        in_specs=[
            pl.BlockSpec((1, tm, d), lambda b, i: (b, i, 0)),
            pl.BlockSpec((1, 6, d), lambda b, i: (b, 0, 0)),
            pl.BlockSpec((4, d), const),
            pl.BlockSpec((3 * d, LANES), const),
            pl.BlockSpec((tm, tm), const),
        ],
        out_specs=[
            pl.BlockSpec((1, tm, d), lambda b, i: (b, i, 0)),
            pl.BlockSpec((1, tm, LANES), lambda b, i: (b, i, 0)),
            pl.BlockSpec((8, LANES), const),
        ],
        out_shape=[
            jax.ShapeDtypeStruct((bsz, length, d), F32),
            jax.ShapeDtypeStruct((bsz, length, LANES), F32),
            jax.ShapeDtypeStruct((8, LANES), F32),
        ],
        scratch_shapes=[pltpu.VMEM((8, LANES), F32)],
        compiler_params=_params(("arbitrary", "arbitrary")),
        name="moe_router",
    )(x, mods, g4, wr, tri)


def _expert_kernel(te_ref, nv_ref, x_ref, w1_ref, w3_ref, w2_ref, o_ref, acc_ref):
    i = pl.program_id(0)
    j = pl.program_id(1)

    @pl.when(i < nv_ref[0])
    def _():
        x = x_ref[...].astype(BF16)
        a = _bdot(x, w1_ref[0])
        b = _bdot(x, w3_ref[0])
        t = (jax.nn.silu(a) * b).astype(BF16)
        y = _bdot(t, w2_ref[0])

        @pl.when(j == 0)
        def _():
            acc_ref[...] = y

        @pl.when(j > 0)
        def _():
            acc_ref[...] += y

        @pl.when(j == pl.num_programs(1) - 1)
        def _():
            o_ref[...] = acc_ref[...]


def _experts(x_sorted, tile_expert, n_valid, w1, w3, w2):
    rows = x_sorted.shape[0]
    d = w1.shape[1]
    dff = w1.shape[2]
    tm = EXPERT_ROW_TILE
    n_ff = 2
    tf = dff // n_ff

    def row_map(i, j, te, nv):
        return (jnp.maximum(jnp.minimum(i, nv[0] - 1), 0), 0)

    def w13_map(i, j, te, nv):
        return (te[jnp.maximum(jnp.minimum(i, nv[0] - 1), 0)], 0, jnp.where(i < nv[0], j, n_ff - 1))

    def w2_map(i, j, te, nv):
        return (te[jnp.maximum(jnp.minimum(i, nv[0] - 1), 0)], jnp.where(i < nv[0], j, n_ff - 1), 0)

    return pl.pallas_call(
        _expert_kernel,
        grid_spec=pltpu.PrefetchScalarGridSpec(
            num_scalar_prefetch=2,
            grid=(rows // tm, n_ff),
            in_specs=[
                pl.BlockSpec((tm, d), row_map),
                pl.BlockSpec((1, d, tf), w13_map),
                pl.BlockSpec((1, d, tf), w13_map),
                pl.BlockSpec((1, tf, d), w2_map),
            ],
            out_specs=pl.BlockSpec((tm, d), row_map),
            scratch_shapes=[pltpu.VMEM((tm, d), F32)],
        ),
        out_shape=jax.ShapeDtypeStruct((rows, d), F32),
        compiler_params=_params(("arbitrary", "arbitrary")),
        name="expert_swiglu",
    )(tile_expert, n_valid, x_sorted, w1, w3, w2)


def _combine_kernel(x_ref, y_ref, meta_ref, g_ref, mod_ref, o_ref):
    meta = meta_ref[0]
    y = meta[:, 2:3] * y_ref[0, 0] + meta[:, 3:4] * y_ref[1, 0]
    o_ref[0] = x_ref[0] + mod_ref[0, 5:6, :] * _rms(y, g_ref[3:4, :])


def _combine(x, y2, meta, g4, mods):
    bsz, length, d = x.shape
    tm = min(ROW_TILE, length)
    return pl.pallas_call(
        _combine_kernel,
        grid=(bsz, length // tm),
        in_specs=[
            pl.BlockSpec((1, tm, d), lambda b, i: (b, i, 0)),
            pl.BlockSpec((2, 1, tm, d), lambda b, i: (0, b, i, 0)),
            pl.BlockSpec((1, tm, LANES), lambda b, i: (b, i, 0)),
            pl.BlockSpec((4, d), lambda b, i: (0, 0)),
            pl.BlockSpec((1, 6, d), lambda b, i: (b, 0, 0)),
        ],
        out_specs=pl.BlockSpec((1, tm, d), lambda b, i: (b, i, 0)),
        out_shape=jax.ShapeDtypeStruct((bsz, length, d), F32),
        compiler_params=_params(("parallel", "parallel")),
        name="moe_combine",
    )(x, y2, meta, g4, mods)


SC_CORES = 2
SC_SUBCORES = 16
SC_WORKERS = SC_CORES * SC_SUBCORES
SC_ROW_CHUNK = 32


def _sc_mesh():
    return plsc.VectorSubcoreMesh(core_axis_name="c", subcore_axis_name="s")


def _sc_dispatch(rows_in, pos2, n_out):
    tokens, width = rows_in.shape
    per_worker = tokens // SC_WORKERS
    chunk = SC_ROW_CHUNK
    n_pairs = per_worker // (2 * chunk)
    idx = pos2.reshape(2, SC_WORKERS, 2 * n_pairs, chunk)

    @functools.partial(
        pl.kernel, mesh=_sc_mesh(),
        out_type=jax.ShapeDtypeStruct((n_out, width), rows_in.dtype),
        scratch_types=[
            pltpu.VMEM((2 * n_pairs, chunk), jnp.int32),
            pltpu.VMEM((2 * n_pairs, chunk), jnp.int32),
            pltpu.VMEM((2, chunk, width), rows_in.dtype),
        ] + [pltpu.SemaphoreType.DMA] * 6,
        name="sc_dispatch",
    )
    def body(rows_hbm, idx_hbm, out_hbm, idx0_v, idx1_v, rows_v, rsem0, rsem1, wsem0, wsem1, wsem2, wsem3):
        wid = lax.axis_index("s") * SC_CORES + lax.axis_index("c")
        base = wid * per_worker
        rsem = (rsem0, rsem1)
        wsem = ((wsem0, wsem1), (wsem2, wsem3))
        pltpu.sync_copy(idx_hbm.at[0, wid], idx0_v)
        pltpu.sync_copy(idx_hbm.at[1, wid], idx1_v)

        @pl.loop(0, n_pairs)
        def _(p):
            c0 = 2 * p
            reads = [pltpu.async_copy(rows_hbm.at[pl.ds(base + (c0 + s) * chunk, chunk)], rows_v.at[s], rsem[s])
                     for s in range(2)]
            writes = []
            for s in range(2):
                reads[s].wait()
                writes.append(pltpu.async_copy(rows_v.at[s], out_hbm.at[idx0_v.at[c0 + s]], wsem[s][0]))
                writes.append(pltpu.async_copy(rows_v.at[s], out_hbm.at[idx1_v.at[c0 + s]], wsem[s][1]))
            for w in writes:
                w.wait()

    return body(rows_in, idx)


def _sc_gather(table, idx):
    n = idx.shape[0]
    width = table.shape[1]
    chunk = SC_ROW_CHUNK
    per_worker = n // SC_WORKERS
    n_pairs = per_worker // (2 * chunk)
    idx3 = idx.reshape(SC_WORKERS, 2 * n_pairs, chunk)

    @functools.partial(
        pl.kernel, mesh=_sc_mesh(),
        out_type=jax.ShapeDtypeStruct((n, width), table.dtype),
        scratch_types=[
            pltpu.VMEM((2 * n_pairs, chunk), jnp.int32),
            pltpu.VMEM((2, chunk, width), table.dtype),
            pltpu.SemaphoreType.DMA, pltpu.SemaphoreType.DMA,
            pltpu.SemaphoreType.DMA, pltpu.SemaphoreType.DMA,
        ],
        name="sc_gather",
    )
    def body(table_hbm, idx_hbm, out_hbm, idx_v, rows_v, gsem0, gsem1, wsem0, wsem1):
        wid = lax.axis_index("s") * SC_CORES + lax.axis_index("c")
        base = wid * per_worker
        gsem = (gsem0, gsem1)
        wsem = (wsem0, wsem1)
        pltpu.sync_copy(idx_hbm.at[wid], idx_v)

        @pl.loop(0, n_pairs)
        def _(p):
            c0 = 2 * p
            gathers = [pltpu.async_copy(table_hbm.at[idx_v.at[c0 + s]], rows_v.at[s], gsem[s]) for s in range(2)]
            writes = []
            for s in range(2):
                gathers[s].wait()
                rows_out = out_hbm.at[pl.ds(base + (c0 + s) * chunk, chunk)]
                writes.append(pltpu.async_copy(rows_v.at[s], rows_out, wsem[s]))
            for s in range(2):
                writes[s].wait()

    return body(table, idx3)


def _moe(x, mods, g4, w_router, w1, w3, w2):
    bsz, length, d = x.shape
    tokens = bsz * length
    tm = EXPERT_ROW_TILE
    xn, meta, counts = _router(x, mods, g4, w_router)
    meta2 = meta.reshape(tokens, LANES)
    experts = meta2[:, 0:2].astype(jnp.int32)
    ranks = meta2[:, 4:6].astype(jnp.int32)
    cnt = counts[0, 0:N_EXPERTS].astype(jnp.int32)
    padded = ((cnt + tm - 1) // tm) * tm
    ends = jnp.cumsum(padded)
    starts = ends - padded
    pos = (starts[experts] + ranks).T.reshape(2 * tokens)
    rows = 2 * tokens + N_EXPERTS * tm
    n_tiles = rows // tm
    n_valid = (ends[-1] // tm).astype(jnp.int32).reshape(1)
    tile_start = jnp.arange(n_tiles, dtype=jnp.int32) * tm
    tile_expert = jnp.minimum(
        jnp.sum((tile_start[:, None] >= ends[None, :]).astype(jnp.int32), axis=1), N_EXPERTS - 1)
    x_sorted = _sc_dispatch(xn.reshape(tokens, d), pos.reshape(2, tokens), rows)
    y_sorted = _experts(x_sorted, tile_expert, n_valid, w1, w3, w2)
    y2 = _sc_gather(y_sorted, pos).reshape(2, bsz, length, d)
    return _combine(x, y2, meta, g4, mods)


def _grid_pos_embed(rows, d):
    row = jnp.repeat(jnp.arange(rows, dtype=F32), GRID_W)
    col = jnp.tile(jnp.arange(GRID_W, dtype=F32), rows)
    n_freq = d // 4
    omega = POS_THETA ** (-jnp.arange(n_freq, dtype=F32) / n_freq)
    ang_r = row[:, None] * omega
    ang_c = col[:, None] * omega
    return jnp.concatenate([jnp.sin(ang_r), jnp.cos(ang_r), jnp.sin(ang_c), jnp.cos(ang_c)], axis=-1)


def kernel(x, c, ctx, c_ctx, ada_w, ada_b, norm_g, ab_w_in, rg_conv_w, rg_conv_b, rg_w_r, rg_b_r, rg_w_i, rg_b_i, rg_lambda, cv_dw_w, cv_dw_b, cv_ln_g, cv_ln_b, ab_w_out, ffn_w1, ffn_w3, ffn_w2, cd_w_in, sg_ln_g, sg_ln_b, sg_w_s, sg_b_s, cd_w_out, moe_router, moe_w1, moe_w3, moe_w2):
    bsz, length, d = x.shape
    assert ada_w.shape[0] == 2, "two layers: one even (RG-LRU | Conformer), one odd (gMLP | Fourier)"
    pos = _grid_pos_embed(length // GRID_W, d)

    cpad = jnp.concatenate([c, c_ctx[None, :], jnp.zeros((16 - bsz - 1, d), F32)], axis=0)
    m = _ada(cpad, ada_w, ada_b)
    mods = m[:, :bsz].reshape(2, bsz, 6, d)
    mods_ctx = m[:, bsz:bsz + 1].reshape(2, 1, 6, d)

    mods_t = mods[0].transpose(1, 0, 2)
    mods_ctx_t = jnp.broadcast_to(mods_ctx[0].reshape(6, 1, d), (6, bsz, d))
    pos3 = pos[:, None, :]
    w_in = ab_w_in[0]
    xa3, gg3, glu3 = _inproj_t(x, pos3, mods_t, norm_g[0], w_in, rg_conv_w[0], rg_conv_b[0], True)
    (xc3,) = _inproj_t(ctx, None, mods_ctx_t, norm_g[0], w_in[:, 0:A_WIDTH], rg_conv_w[0], rg_conv_b[0], False)
    wg, bg, sp = _gate_weights(rg_w_r[0], rg_b_r[0], rg_w_i[0], rg_b_i[0], rg_lambda[0])
    n_exp, _, dff_e = moe_w1[0].shape
    rec_hi, rec_lo, w2b = _scan_t(xa3, xc3, gg3, wg, bg, sp, riders=(moe_w2[0].reshape(n_exp * dff_e, d),))
    u3, w1b, w3b = _conv_t(glu3, cv_dw_w[0], cv_dw_b[0], cv_ln_g[0], cv_ln_b[0],
                           riders=(moe_w1[0].reshape(n_exp * d, dff_e), moe_w3[0].reshape(n_exp * d, dff_e)))
    x1_t = _outproj_t(rec_lo, rec_hi, u3, x, pos3, ab_w_out[0], norm_g[0], mods_t)
    x2 = _ffn_t(x1_t, mods_t, norm_g[0], ffn_w1[0].astype(BF16), ffn_w3[0].astype(BF16), ffn_w2[0].astype(BF16))

    u, v, f = _inproj_cd(x2, mods[1], norm_g[1], cd_w_in[0])
    fo = _fourier(f)
    x3 = _gate_out(u, v, fo, x2, sg_ln_g[0], sg_ln_b[0], sg_w_s[0], sg_b_s[0], cd_w_out[0],
                   norm_g[1], mods[1])
    return _moe(x3, mods[1], norm_g[1], moe_router[0], w1b.reshape(n_exp, d, dff_e), w3b.reshape(n_exp, d, dff_e),
                w2b.reshape(n_exp, dff_e, d))
```

```python
import functools

import numpy as np
import jax
import jax.numpy as jnp
from jax import lax
from jax.experimental import pallas as pl
from jax.experimental.pallas import tpu as pltpu
from jax.experimental.pallas import tpu_sc as plsc

F32 = jnp.float32
BF16 = jnp.bfloat16

GRID_W = 64
POS_THETA = 10000.0
NORM_EPS = 1e-6
LN_EPS = 1e-5
LRU_C = 8.0

A_WIDTH = 512
A_HEADS = 8
A_CONV_W = 4
B_WIDTH = 512
B_CONV_W = 31
C_WIDTH = 512
C_HEADS = 8
CHUNK = 128
D_WIDTH = 512
D_GROUPS = 4
D_GROUP_DIM = D_WIDTH // D_GROUPS
N_EXPERTS = 8

LANES = 128
ROW_TILE = 512
EXPERT_ROW_TILE = 512
VMEM_LIMIT = 56 * 2 ** 20


def _params(sem):
    return pltpu.CompilerParams(dimension_semantics=sem, vmem_limit_bytes=VMEM_LIMIT)


def _rms(x, g):
    return x * lax.rsqrt(jnp.mean(x * x, axis=-1, keepdims=True) + NORM_EPS) * g


def _bdot(a, b):
    return jnp.dot(a, b, preferred_element_type=F32)


MXU_WIDTH = 256


def _swiglu_halves(x, w1_ref, w3_ref, w2_ref, lead=()):
    n = w1_ref.shape[-1]
    half = -(-(n // 2) // MXU_WIDTH) * MXU_WIDTH
    y = None
    for lo, hi in ((0, half), (half, n)):
        a = _bdot(x, w1_ref[lead + (slice(None), slice(lo, hi))])
        b = _bdot(x, w3_ref[lead + (slice(None), slice(lo, hi))])
        t = (jax.nn.silu(a) * b).astype(BF16)
        part = _bdot(t, w2_ref[lead + (slice(lo, hi), slice(None))])
        y = part if y is None else y + part
    return y


def _ada_kernel(c_ref, w_ref, b_ref, o_ref):
    s = jax.nn.silu(c_ref[...])
    o_ref[0] = _bdot(s.astype(BF16), w_ref[0].astype(BF16)) + b_ref[0]


def _ada(cpad, ada_w, ada_b):
    nl, d, n6 = ada_w.shape
    rows = cpad.shape[0]
    tn = n6 // 4
    return pl.pallas_call(
        _ada_kernel,
        grid=(nl, n6 // tn),
        in_specs=[
            pl.BlockSpec((rows, d), lambda l, j: (0, 0)),
            pl.BlockSpec((1, d, tn), lambda l, j: (l, 0, j)),
            pl.BlockSpec((1, 1, tn), lambda l, j: (l, 0, j)),
        ],
        out_specs=pl.BlockSpec((1, rows, tn), lambda l, j: (l, 0, j)),
        out_shape=jax.ShapeDtypeStruct((nl, rows, n6), F32),
        compiler_params=_params(("arbitrary", "arbitrary")),
        name="ada_terms",
    )(cpad, ada_w, ada_b.reshape(nl, 1, n6))


TIME_TILE = 128
SCAN_TIME_TILE = 128
SCAN_LANES = 256
CONV_TIME_TILE = 64
CONV_HALO = 16


def _time_major(ref):
    return jnp.swapaxes(ref[...], 0, 1)


def _inproj_t_kernel(*refs, tt, add_pos, branches, n_cast):
    if n_cast:
        n_in = len(refs) - n_cast - (3 if branches else 1) - n_cast
        _cast_riders(refs[n_in:n_in + n_cast], refs[len(refs) - n_cast:])
        refs = refs[:n_in] + refs[n_in + n_cast:len(refs) - n_cast]
    if add_pos:
        x_ref, xp_ref, xn_ref, pos_ref, posp_ref, posn_ref, mod_ref, g_ref, w_ref, cw_ref, cb_ref, *outs = refs
        xall = jnp.concatenate(
            [_time_major(xp_ref)[7:8] + posp_ref[...], _time_major(x_ref) + pos_ref[...],
             _time_major(xn_ref)[0:2] + posn_ref[...]], axis=0)
    else:
        x_ref, xp_ref, xn_ref, mod_ref, g_ref, w_ref, cw_ref, cb_ref, *outs = refs
        xall = jnp.concatenate([_time_major(xp_ref)[7:8], _time_major(x_ref), _time_major(xn_ref)[0:2]], axis=0)
    i = pl.program_id(0)
    last = pl.num_programs(0) - 1
    steps, bsz, d = xall.shape
    h = _rms(xall, g_ref[0:1, :]) * (1.0 + mod_ref[1]) + mod_ref[0]
    z = _bdot(h.reshape(steps * bsz, d).astype(BF16), w_ref[...].astype(BF16))
    z3 = z.reshape(steps, bsz, z.shape[-1])
    t_idx = lax.broadcasted_iota(jnp.int32, (steps, bsz, A_WIDTH), 0)
    inside = jnp.logical_and(jnp.logical_or(t_idx >= 1, i > 0), jnp.logical_or(t_idx <= tt, i < last))
    xa = jnp.where(inside, z3[:, :, 0:A_WIDTH], 0.0)
    xconv = cb_ref[...] + cw_ref[0:1, :] * xa[0:tt]
    for k in range(1, A_CONV_W):
        xconv = xconv + cw_ref[k:k + 1, :] * xa[k:k + tt]
    outs[0][...] = xconv
    if branches:
        outs[1][...] = jax.nn.gelu(z3[1:tt + 1, :, A_WIDTH:2 * A_WIDTH])
        vb = z3[1:tt + 1, :, 2 * A_WIDTH:2 * A_WIDTH + B_WIDTH]
        gb = z3[1:tt + 1, :, 2 * A_WIDTH + B_WIDTH:]
        outs[2][...] = vb * jax.nn.sigmoid(gb)


def _inproj_t(x, pos3, mods_t, g4, w, conv_w, conv_b, branches, riders=()):
    bsz, length, d = x.shape
    n = w.shape[1]
    tt = min(TIME_TILE, length)
    add_pos = pos3 is not None
    n_out = 3 if branches else 1
    prev_map = lambda i: (jnp.maximum(i * tt - 1, 0), 0, 0)
    next_map = lambda i: (jnp.minimum((i + 1) * (tt // 2), length // 2 - 1), 0, 0)
    in_specs = [
        pl.BlockSpec((bsz, tt, d), lambda i: (0, i, 0)),
        pl.BlockSpec((bsz, 8, d), lambda i: (0, jnp.maximum(i * (tt // 8) - 1, 0), 0)),
        pl.BlockSpec((bsz, 8, d), lambda i: (0, jnp.minimum((i + 1) * (tt // 8), length // 8 - 1), 0)),
    ]
    args = [x, x, x]
    if add_pos:
        in_specs += [
            pl.BlockSpec((tt, 1, d), lambda i: (i, 0, 0)),
            pl.BlockSpec((1, 1, d), prev_map),
            pl.BlockSpec((2, 1, d), next_map),
        ]
        args += [pos3, pos3, pos3]
    in_specs += [
        pl.BlockSpec((6, bsz, d), lambda i: (0, 0, 0)),
        pl.BlockSpec((4, d), lambda i: (0, 0)),
        pl.BlockSpec((d, n), lambda i: (0, 0)),
        pl.BlockSpec((A_CONV_W, A_WIDTH), lambda i: (0, 0)),
        pl.BlockSpec((1, A_WIDTH), lambda i: (0, 0)),
    ]
    args += [mods_t, g4, w, conv_w, conv_b.reshape(1, A_WIDTH)]
    r_in, r_out, r_shapes = _rider_specs(riders, length // tt, lambda i: i)
    return pl.pallas_call(
        functools.partial(_inproj_t_kernel, tt=tt, add_pos=add_pos, branches=branches, n_cast=len(riders)),
        grid=(length // tt,),
        in_specs=in_specs + r_in,
        out_specs=[pl.BlockSpec((tt, bsz, A_WIDTH), lambda i: (i, 0, 0))] * n_out + r_out,
        out_shape=[jax.ShapeDtypeStruct((length, bsz, A_WIDTH), F32)] * n_out + r_shapes,
        compiler_params=_params(("parallel",)),
        name="inproj_ab" if branches else "inproj_ctx",
    )(*args, *riders)


def _cast_riders(cast_in, cast_out):
    for src, dst in zip(cast_in, cast_out):
        dst[...] = src[...].astype(BF16)


def _rider_specs(arrays, n_steps, step_of):
    in_specs, out_specs, out_shapes = [], [], []
    for arr in arrays:
        rows, cols = arr.shape
        block = (rows // n_steps, cols)
        in_specs.append(pl.BlockSpec(block, lambda *idx: (step_of(*idx), 0)))
        out_specs.append(pl.BlockSpec(block, lambda *idx: (step_of(*idx), 0)))
        out_shapes.append(jax.ShapeDtypeStruct(arr.shape, BF16))
    return in_specs, out_specs, out_shapes


def _scan_t_kernel(*refs, tb, nb, ctx_len, n_cast):
    xf_ref, xb_ref, ggf_ref, ggb_ref, xc_ref, wg_ref, bg_ref, sp_ref = refs[:8]
    cast_in = refs[8:8 + n_cast]
    hi_ref, lo_ref = refs[8 + n_cast:10 + n_cast]
    cast_out = refs[10 + n_cast:10 + 2 * n_cast]
    a_ref, b_ref, h_ref, state_ref = refs[10 + 2 * n_cast:]
    _cast_riders(cast_in, cast_out)
    k = pl.program_id(1)
    half = nb // 2
    bsz, lanes = xf_ref.shape[1], xf_ref.shape[2]
    chunk = 32

    def coefficients(x_ref, n, direction):
        def body(c, _):
            t0 = pl.multiple_of(c * chunk, chunk)
            x = x_ref[pl.ds(t0, chunk)].reshape(chunk * bsz, lanes)
            g = _bdot(x.astype(BF16), wg_ref[0, direction]) + bg_ref[0, direction]
            gate = 0.5 * jnp.tanh(0.5 * g) + 0.5
            r = gate[:, 0:lanes]
            i = gate[:, lanes:]
            log_a = (-LRU_C * r) * sp_ref[0, direction:direction + 1, :]
            a = jnp.exp(log_a)
            one_minus_a2 = -jnp.tanh(log_a) * (a * a + 1.0)
            a_ref[direction, pl.ds(t0, chunk)] = a.reshape(chunk, bsz, lanes)
            b_ref[direction, pl.ds(t0, chunk)] = (jnp.sqrt(one_minus_a2) * (i * x)).reshape(chunk, bsz, lanes)
            return 0
        lax.fori_loop(0, n // chunk, body, 0)

    def sweep(n, store):
        def body(t, carry):
            hf, hb = carry
            tr = n - 1 - t
            hf = a_ref[0, pl.ds(t, 1)][0] * hf + b_ref[0, pl.ds(t, 1)][0]
            hb = a_ref[1, pl.ds(tr, 1)][0] * hb + b_ref[1, pl.ds(tr, 1)][0]
            if store:
                b_ref[0, pl.ds(t, 1)] = hf[None]
                b_ref[1, pl.ds(tr, 1)] = hb[None]
            return hf, hb
        hf, hb = lax.fori_loop(0, n, body, (state_ref[0], state_ref[1]), unroll=8)
        state_ref[0] = hf
        state_ref[1] = hb

    @pl.when(k == 0)
    def _():
        state_ref[...] = jnp.zeros_like(state_ref)
        coefficients(xc_ref, ctx_len, 0)
        coefficients(xc_ref, ctx_len, 1)
        sweep(ctx_len, False)

    coefficients(xf_ref, tb, 0)
    coefficients(xb_ref, tb, 1)
    sweep(tb, True)
    m = nb - 1 - k

    @pl.when(k < half)
    def _():
        h_ref[k] = b_ref[0, 0:tb]
        h_ref[m] = b_ref[1, 0:tb]

    @pl.when(k >= half)
    def _():
        hi_ref[...] = (b_ref[0, 0:tb] + h_ref[k]) * ggf_ref[...]
        lo_ref[...] = (h_ref[m] + b_ref[1, 0:tb]) * ggb_ref[...]


def _scan_t(xa3, xc3, gg3, wg, bg, sp, riders=()):
    length, bsz, width = xa3.shape
    ctx_len = xc3.shape[0]
    tb = min(SCAN_TIME_TILE, length // 2)
    nb = length // tb
    half = nb // 2
    lanes = SCAN_LANES
    blk = (tb, bsz, lanes)
    r_in, r_out, r_shapes = _rider_specs(riders, (width // lanes) * nb, lambda g, k: g * nb + k)
    return pl.pallas_call(
        functools.partial(_scan_t_kernel, tb=tb, nb=nb, ctx_len=ctx_len, n_cast=len(riders)),
        grid=(width // lanes, nb),
        in_specs=[
            pl.BlockSpec(blk, lambda g, k: (k, 0, g)),
            pl.BlockSpec(blk, lambda g, k: (nb - 1 - k, 0, g)),
            pl.BlockSpec(blk, lambda g, k: (jnp.maximum(k, half), 0, g)),
            pl.BlockSpec(blk, lambda g, k: (jnp.minimum(nb - 1 - k, half - 1), 0, g)),
            pl.BlockSpec((ctx_len, bsz, lanes), lambda g, k: (0, 0, g)),
            pl.BlockSpec((1, 2, lanes, 2 * lanes), lambda g, k: (g, 0, 0, 0)),
            pl.BlockSpec((1, 2, 1, 2 * lanes), lambda g, k: (g, 0, 0, 0)),
            pl.BlockSpec((1, 2, lanes), lambda g, k: (g, 0, 0)),
        ] + r_in,
        out_specs=[
            pl.BlockSpec(blk, lambda g, k: (jnp.maximum(k - half, 0), 0, g)),
            pl.BlockSpec(blk, lambda g, k: (jnp.minimum(nb - 1 - k, half - 1), 0, g)),
        ] + r_out,
        out_shape=[jax.ShapeDtypeStruct((length // 2, bsz, width), F32)] * 2 + r_shapes,
        scratch_shapes=[
            pltpu.VMEM((2, max(tb, ctx_len), bsz, lanes), F32),
            pltpu.VMEM((2, max(tb, ctx_len), bsz, lanes), F32),
            pltpu.VMEM((nb, tb, bsz, lanes), F32),
            pltpu.VMEM((2, bsz, lanes), F32),
        ],
        compiler_params=_params(("parallel", "arbitrary")),
        name="rglru_scan",
    )(xa3, xa3, gg3, gg3, xc3, wg, bg, sp, *riders)


def _gate_weights(w_r, b_r, w_i, b_i, lam):
    hd = A_WIDTH // A_HEADS
    heads_per_group = SCAN_LANES // hd
    ngroups = A_WIDTH // SCAN_LANES

    def blockdiag(w):
        w = w.reshape(ngroups, heads_per_group, hd, hd)
        eye = jnp.eye(heads_per_group, dtype=w.dtype)
        return jnp.einsum("ghde,hk->ghdke", w, eye).reshape(ngroups, SCAN_LANES, SCAN_LANES)

    wg = jnp.stack([jnp.concatenate([blockdiag(w_r[d]), blockdiag(w_i[d])], axis=-1) for d in range(2)], axis=1)
    bg = jnp.stack([jnp.concatenate([b_r[d].reshape(ngroups, 1, SCAN_LANES), b_i[d].reshape(ngroups, 1, SCAN_LANES)],
                                    axis=-1) for d in range(2)], axis=1)
    sp = jax.nn.softplus(-lam.astype(F32)).reshape(2, ngroups, SCAN_LANES).transpose(1, 0, 2)
    return wg.astype(BF16), bg, sp


def _conv_t_kernel(*refs, tb, n_cast):
    x_ref, xp_ref, xn_ref, w_ref, b_ref, lng_ref, lnb_ref = refs[:7]
    cast_in = refs[7:7 + n_cast]
    o_ref = refs[7 + n_cast]
    cast_out = refs[8 + n_cast:8 + 2 * n_cast]
    stage_ref = refs[-1]
    _cast_riders(cast_in, cast_out)
    i = pl.program_id(0)
    last = pl.num_programs(0) - 1
    halo = CONV_HALO
    stage_ref[0:halo] = jnp.where(i > 0, xp_ref[...], 0.0)
    stage_ref[halo:halo + tb] = x_ref[...]
    stage_ref[halo + tb:2 * halo + tb] = jnp.where(i < last, xn_ref[...], 0.0)
    sub = 8

    def body(r, _):
        t0 = pl.multiple_of(r * sub, sub)
        def tap(k, acc):
            return acc + w_ref[k] * stage_ref[pl.ds(t0 + 1 + k, sub)]
        acc = lax.fori_loop(0, B_CONV_W + 1, tap, jnp.broadcast_to(b_ref[...], (sub,) + b_ref.shape), unroll=8)
        o_ref[pl.ds(t0, sub)] = acc
        return 0
    lax.fori_loop(0, tb // sub, body, 0)

    norm_rows = 16

    def norm_body(r, _):
        t0 = pl.multiple_of(r * norm_rows, norm_rows)
        acc = o_ref[pl.ds(t0, norm_rows)]
        mu = jnp.mean(acc, axis=-1, keepdims=True)
        cen = acc - mu
        var = jnp.mean(cen * cen, axis=-1, keepdims=True)
        y = cen * lax.rsqrt(var + LN_EPS) * lng_ref[...] + lnb_ref[...]
        o_ref[pl.ds(t0, norm_rows)] = jax.nn.silu(y)
        return 0
    lax.fori_loop(0, tb // norm_rows, norm_body, 0)


def _conv_t(glu3, dw_w, dw_b, ln_g, ln_b, riders=()):
    length, bsz, width = glu3.shape
    tb = min(CONV_TIME_TILE, length)
    halo = CONV_HALO
    taps = dw_w.shape[0]
    w8 = jnp.broadcast_to(jnp.concatenate([dw_w, jnp.zeros((32 - taps, width), F32)], axis=0)[:, None, :],
                          (32, bsz, width))
    b8 = jnp.broadcast_to(dw_b[None, :], (bsz, width))
    r_in, r_out, r_shapes = _rider_specs(riders, length // tb, lambda i: i)
    return pl.pallas_call(
        functools.partial(_conv_t_kernel, tb=tb, n_cast=len(riders)),
        grid=(length // tb,),
        in_specs=[
            pl.BlockSpec((tb, bsz, width), lambda i: (i, 0, 0)),
            pl.BlockSpec((halo, bsz, width), lambda i: (jnp.maximum(i * (tb // halo) - 1, 0), 0, 0)),
            pl.BlockSpec((halo, bsz, width), lambda i: (jnp.minimum((i + 1) * (tb // halo), length // halo - 1), 0, 0)),
            pl.BlockSpec((32, bsz, width), lambda i: (0, 0, 0)),
            pl.BlockSpec((bsz, width), lambda i: (0, 0)),
            pl.BlockSpec((1, width), lambda i: (0, 0)),
            pl.BlockSpec((1, width), lambda i: (0, 0)),
        ] + r_in,
        out_specs=[pl.BlockSpec((tb, bsz, width), lambda i: (i, 0, 0))] + r_out,
        out_shape=[jax.ShapeDtypeStruct((length, bsz, width), F32)] + r_shapes,
        scratch_shapes=[pltpu.VMEM((tb + 2 * halo, bsz, width), F32)],
        compiler_params=_params(("parallel",)),
        name="conformer_conv",
    )(glu3, glu3, glu3, w8, b8, ln_g.reshape(1, -1), ln_b.reshape(1, -1), *riders)


def _outproj_t_kernel(lo_ref, hi_ref, u_ref, x_ref, pos_ref, wout_ref, g_ref, mod_ref, o_ref, *, half_steps):
    i = pl.program_id(0)
    bsz, tt, d = x_ref.shape
    rec = jnp.where(i < half_steps, lo_ref[...], hi_ref[...])
    rows = tt * bsz
    y = (_bdot(rec.reshape(rows, A_WIDTH).astype(BF16), wout_ref[0:A_WIDTH, :].astype(BF16))
         + _bdot(u_ref[...].reshape(rows, B_WIDTH).astype(BF16), wout_ref[A_WIDTH:, :].astype(BF16)))
    yn = _rms(y, g_ref[1:2, :]).reshape(tt, bsz, d)
    o_ref[...] = _time_major(x_ref) + pos_ref[...] + mod_ref[2] * yn


def _outproj_t(rec_lo, rec_hi, u3, x, pos3, w_out, g4, mods_t):
    bsz, length, d = x.shape
    tt = min(TIME_TILE // 2, length // 2)
    half_steps = (length // 2) // tt
    return pl.pallas_call(
        functools.partial(_outproj_t_kernel, half_steps=half_steps),
        grid=(length // tt,),
        in_specs=[
            pl.BlockSpec((tt, bsz, A_WIDTH), lambda i: (jnp.minimum(i, half_steps - 1), 0, 0)),
            pl.BlockSpec((tt, bsz, A_WIDTH), lambda i: (jnp.maximum(i - half_steps, 0), 0, 0)),
            pl.BlockSpec((tt, bsz, B_WIDTH), lambda i: (i, 0, 0)),
            pl.BlockSpec((bsz, tt, d), lambda i: (0, i, 0)),
            pl.BlockSpec((tt, 1, d), lambda i: (i, 0, 0)),
            pl.BlockSpec((A_WIDTH + B_WIDTH, d), lambda i: (0, 0)),
            pl.BlockSpec((4, d), lambda i: (0, 0)),
            pl.BlockSpec((6, bsz, d), lambda i: (0, 0, 0)),
        ],
        out_specs=pl.BlockSpec((tt, bsz, d), lambda i: (i, 0, 0)),
        out_shape=jax.ShapeDtypeStruct((length, bsz, d), F32),
        compiler_params=_params(("parallel",)),
        name="mixer_ab_outproj",
    )(rec_lo, rec_hi, u3, x, pos3, w_out, g4, mods_t)


def _ffn_t_kernel(x_ref, mod_ref, g_ref, w1_ref, w3_ref, w2_ref, o_ref):
    x = x_ref[...]
    tt, bsz, d = x.shape
    h = (_rms(x, g_ref[2:3, :]) * (1.0 + mod_ref[4]) + mod_ref[3]).reshape(tt * bsz, d).astype(BF16)
    y = _swiglu_halves(h, w1_ref, w3_ref, w2_ref)
    o_ref[...] = jnp.swapaxes(x + mod_ref[5] * _rms(y, g_ref[3:4, :]).reshape(tt, bsz, d), 0, 1)


def _ffn_t(x3, mods_t, g4, w1, w3, w2):
    length, bsz, d = x3.shape
    dff = w1.shape[1]
    tt = min(TIME_TILE, length)
    return pl.pallas_call(
        _ffn_t_kernel,
        grid=(length // tt,),
        in_specs=[
            pl.BlockSpec((tt, bsz, d), lambda i: (i, 0, 0)),
            pl.BlockSpec((6, bsz, d), lambda i: (0, 0, 0)),
            pl.BlockSpec((4, d), lambda i: (0, 0)),
            pl.BlockSpec((d, dff), lambda i: (0, 0), pipeline_mode=pl.Buffered(1)),
            pl.BlockSpec((d, dff), lambda i: (0, 0), pipeline_mode=pl.Buffered(1)),
            pl.BlockSpec((dff, d), lambda i: (0, 0), pipeline_mode=pl.Buffered(1)),
        ],
        out_specs=pl.BlockSpec((bsz, tt, d), lambda i: (0, i, 0)),
        out_shape=jax.ShapeDtypeStruct((bsz, length, d), F32),
        compiler_params=_params(("parallel",)),
        name="dense_swiglu",
    )(x3, mods_t, g4, w1, w3, w2)


def _inproj_cd_kernel(x_ref, mod_ref, g_ref, w_ref, u_ref, v_ref, f_ref):
    x = x_ref[0]
    h = _rms(x, g_ref[0:1, :]) * (1.0 + mod_ref[0, 1:2, :]) + mod_ref[0, 0:1, :]
    z = _bdot(h.astype(BF16), w_ref[...].astype(BF16))
    u_ref[0] = jax.nn.gelu(z[:, 0:C_WIDTH])
    v_ref[0] = jax.nn.gelu(z[:, C_WIDTH:2 * C_WIDTH])
    f_ref[0] = z[:, 2 * C_WIDTH:]


def _inproj_cd(x, mods, g4, w):
    bsz, length, d = x.shape
    n = w.shape[1]
    tm = min(ROW_TILE, length)
    return pl.pallas_call(
        _inproj_cd_kernel,
        grid=(bsz, length // tm),
        in_specs=[
            pl.BlockSpec((1, tm, d), lambda b, i: (b, i, 0)),
            pl.BlockSpec((1, 6, d), lambda b, i: (b, 0, 0)),
            pl.BlockSpec((4, d), lambda b, i: (0, 0)),
            pl.BlockSpec((d, n), lambda b, i: (0, 0)),
        ],
        out_specs=[pl.BlockSpec((1, tm, C_WIDTH), lambda b, i: (b, i, 0))] * 3,
        out_shape=[jax.ShapeDtypeStruct((bsz, length, C_WIDTH), F32)] * 3,
        compiler_params=_params(("parallel", "parallel")),
        name="inproj_cd",
    )(x, mods, g4, w)


def _fourier_kernel(f_ref, dmat_ref, lmat_ref, o_ref, z_ref, *, length, scale):
    j = pl.program_id(1)
    rows_per_step = 512 if length % 512 == 0 else length

    @pl.when(j == 0)
    def _():
        def body(c, _):
            rows = pl.ds(pl.multiple_of(c * rows_per_step, rows_per_step), rows_per_step)
            fb = f_ref[0, rows, :].astype(BF16)
            for g in range(D_GROUPS):
                cs = _bdot(fb[:, g * D_GROUP_DIM:(g + 1) * D_GROUP_DIM], dmat_ref[...])
                z_ref[0, rows, g * D_GROUP_DIM:(g + 1) * D_GROUP_DIM] = cs[:, 0:D_GROUP_DIM].astype(BF16)
                z_ref[1, rows, g * D_GROUP_DIM:(g + 1) * D_GROUP_DIM] = cs[:, D_GROUP_DIM:].astype(BF16)
            return 0
        lax.fori_loop(0, length // rows_per_step, body, 0)

    out = _bdot(lmat_ref[0], z_ref[0]) + _bdot(lmat_ref[1], z_ref[1])
    o_ref[0] = (out * scale).astype(BF16)


def _dft_matrices(length):
    k = np.arange(length, dtype=np.int64)
    ang_l = 2.0 * np.pi * ((k[:, None] * k[None, :]) % length).astype(np.float64) / length
    lmat = np.stack([np.cos(ang_l), -np.sin(ang_l)], axis=0)
    d = np.arange(D_GROUP_DIM, dtype=np.int64)
    ang_d = 2.0 * np.pi * ((d[:, None] * d[None, :]) % D_GROUP_DIM).astype(np.float64) / D_GROUP_DIM
    dmat = np.concatenate([np.cos(ang_d), np.sin(ang_d)], axis=1)
    return jnp.asarray(lmat, dtype=F32).astype(BF16), jnp.asarray(dmat, dtype=F32).astype(BF16)


def _fourier(f):
    bsz, length, width = f.shape
    tm = min(ROW_TILE, length)
    lmat, dmat = _dft_matrices(length)
    scale = float(1.0 / np.sqrt(length * D_GROUP_DIM))
    return pl.pallas_call(
        functools.partial(_fourier_kernel, length=length, scale=scale),
        grid=(bsz, length // tm),
        in_specs=[
            pl.BlockSpec((1, length, width), lambda b, i: (b, 0, 0)),
            pl.BlockSpec((D_GROUP_DIM, 2 * D_GROUP_DIM), lambda b, i: (0, 0)),
            pl.BlockSpec((2, tm, length), lambda b, i: (0, i, 0)),
        ],
        out_specs=pl.BlockSpec((1, tm, width), lambda b, i: (b, i, 0)),
        out_shape=jax.ShapeDtypeStruct((bsz, length, width), BF16),
        scratch_shapes=[pltpu.VMEM((2, length, width), BF16)],
        compiler_params=_params(("parallel", "arbitrary")),
        name="fourier_mix",
    )(f, dmat, lmat)


def _gate_out_kernel(u_ref, v_ref, fo_ref, x_ref, lng_ref, lnb_ref, wcat_ref, bs_ref, wout_ref, g_ref, mod_ref,
                     o_ref, vs_ref, gt_ref, *, tm):
    head_dim = C_WIDTH // C_HEADS
    head_of_lane = lax.broadcasted_iota(jnp.int32, (CHUNK, C_WIDTH), 1) // head_dim
    for ci in range(tm // CHUNK):
        rows = slice(ci * CHUNK, (ci + 1) * CHUNK)
        v = v_ref[0, rows, :]
        mu = jnp.mean(v, axis=-1, keepdims=True)
        cen = v - mu
        var = jnp.mean(cen * cen, axis=-1, keepdims=True)
        vn = (cen * lax.rsqrt(var + LN_EPS) * lng_ref[...] + lnb_ref[...]).astype(BF16)
        for h in range(C_HEADS):
            vs_ref[h * CHUNK:(h + 1) * CHUNK, :] = jnp.where(head_of_lane == h, vn, jnp.zeros_like(vn))
        mixed = _bdot(wcat_ref[...], vs_ref[...]) + bs_ref[...]
        gt_ref[rows, :] = (u_ref[0, rows, :] * mixed).astype(BF16)
    y = (_bdot(gt_ref[...], wout_ref[0:C_WIDTH, :].astype(BF16))
         + _bdot(fo_ref[0], wout_ref[C_WIDTH:, :].astype(BF16)))
    o_ref[0] = x_ref[0] + mod_ref[0, 2:3, :] * _rms(y, g_ref[1:2, :])


def _gate_out(u, v, fo, x, ln_g, ln_b, w_s, b_s, w_out, g4, mods):
    bsz, length, d = x.shape
    tm = min(ROW_TILE, length)
    head_dim = C_WIDTH // C_HEADS
    wcat = w_s.transpose(1, 0, 2).reshape(CHUNK, C_HEADS * CHUNK).astype(BF16)
    bs_full = jnp.repeat(b_s.T, head_dim, axis=1)
    const = lambda b, i: (0, 0)
    return pl.pallas_call(
        functools.partial(_gate_out_kernel, tm=tm),
        grid=(bsz, length // tm),
        in_specs=[
            pl.BlockSpec((1, tm, C_WIDTH), lambda b, i: (b, i, 0)),
            pl.BlockSpec((1, tm, C_WIDTH), lambda b, i: (b, i, 0)),
            pl.BlockSpec((1, tm, D_WIDTH), lambda b, i: (b, i, 0)),
            pl.BlockSpec((1, tm, d), lambda b, i: (b, i, 0)),
            pl.BlockSpec((1, C_WIDTH), const),
            pl.BlockSpec((1, C_WIDTH), const),
            pl.BlockSpec((CHUNK, C_HEADS * CHUNK), const),
            pl.BlockSpec((CHUNK, C_WIDTH), const),
            pl.BlockSpec((C_WIDTH + D_WIDTH, d), const),
            pl.BlockSpec((4, d), const),
            pl.BlockSpec((1, 6, d), lambda b, i: (b, 0, 0)),
        ],
        out_specs=pl.BlockSpec((1, tm, d), lambda b, i: (b, i, 0)),
        out_shape=jax.ShapeDtypeStruct((bsz, length, d), F32),
        scratch_shapes=[
            pltpu.VMEM((C_HEADS * CHUNK, C_WIDTH), BF16),
            pltpu.VMEM((tm, C_WIDTH), BF16),
        ],
        compiler_params=_params(("parallel", "parallel")),
        name="gating_outproj",
    )(u, v, fo, x, ln_g.reshape(1, -1), ln_b.reshape(1, -1), wcat, bs_full, w_out, g4, mods)


def _router_kernel(x_ref, mod_ref, g_ref, wr_ref, tri_ref, xn_ref, meta_ref, cnt_ref, carry_ref, *, tm):
    first = jnp.logical_and(pl.program_id(0) == 0, pl.program_id(1) == 0)

    @pl.when(first)
    def _():
        carry_ref[...] = jnp.zeros_like(carry_ref)

    x = x_ref[0]
    h = _rms(x, g_ref[2:3, :]) * (1.0 + mod_ref[0, 4:5, :]) + mod_ref[0, 3:4, :]
    xn_ref[0] = h
    h_hi = h.astype(BF16)
    h_lo = (h - h_hi.astype(F32)).astype(BF16)
    logits = _bdot(jnp.concatenate([h_hi, h_hi, h_lo], axis=1), wr_ref[...])
    lane = lax.broadcasted_iota(jnp.int32, (tm, LANES), 1)
    neg = jnp.float32(-jnp.inf)
    lm = jnp.where(lane < N_EXPERTS, logits, neg)
    m1 = jnp.max(lm, axis=-1, keepdims=True)
    i1 = jnp.min(jnp.where(lm == m1, lane, LANES), axis=-1, keepdims=True)
    lm2 = jnp.where(lane == i1, neg, lm)
    m2 = jnp.max(lm2, axis=-1, keepdims=True)
    i2 = jnp.min(jnp.where(lm2 == m2, lane, LANES), axis=-1, keepdims=True)
    e = jnp.exp(m2 - m1)
    p1 = 1.0 / (1.0 + e)
    p2 = e / (1.0 + e)
    hot1 = (lane == i1).astype(F32)
    hot2 = (lane == i2).astype(F32)
    both = hot1 + hot2
    before = _bdot(tri_ref[...], both.astype(BF16)) + carry_ref[0:1, :]
    r1 = jnp.sum(before * hot1, axis=-1, keepdims=True)
    r2 = jnp.sum(before * hot2, axis=-1, keepdims=True)
    carry_ref[0:1, :] = carry_ref[0:1, :] + jnp.sum(both, axis=0, keepdims=True)
    cnt_ref[...] = carry_ref[...]
    vals = (i1.astype(F32), i2.astype(F32), p1, p2, r1, r2)
    meta = jnp.zeros((tm, LANES), F32)
    for k, val in enumerate(vals):
        meta = jnp.where(lane == k, val, meta)
    meta_ref[0] = meta


def _router(x, mods, g4, w_router):
    bsz, length, d = x.shape
    tm = min(ROW_TILE, length)
    wr = jnp.concatenate([w_router, jnp.zeros((d, LANES - N_EXPERTS), F32)], axis=1)
    wr_hi = wr.astype(BF16)
    wr_lo = (wr - wr_hi.astype(F32)).astype(BF16)
    wr = jnp.concatenate([wr_hi, wr_lo, wr_hi], axis=0)
    tri =jnp.asarray(np.tril(np.ones((tm, tm), np.float32), -1), dtype=BF16)
    const = lambda b, i: (0, 0)
    return pl.pallas_call(
        functools.partial(_router_kernel, tm=tm),
        grid=(bsz, length // tm),
        in_specs=[
            pl.BlockSpec((1, tm, d), lambda b, i: (b, i, 0)),
            pl.BlockSpec((1, 6, d), lambda b, i: (b, 0, 0)),
            pl.BlockSpec((4, d), const),
            pl.BlockSpec((3 * d, LANES), const),
            pl.BlockSpec((tm, tm), const),
        ],
        out_specs=[
            pl.BlockSpec((1, tm, d), lambda b, i: (b, i, 0)),
            pl.BlockSpec((1, tm, LANES), lambda b, i: (b, i, 0)),
            pl.BlockSpec((8, LANES), const),
        ],
        out_shape=[
            jax.ShapeDtypeStruct((bsz, length, d), F32),
            jax.ShapeDtypeStruct((bsz, length, LANES), F32),
            jax.ShapeDtypeStruct((8, LANES), F32),
        ],
        scratch_shapes=[pltpu.VMEM((8, LANES), F32)],
        compiler_params=_params(("arbitrary", "arbitrary")),
        name="moe_router",
    )(x, mods, g4, wr, tri)


def _expert_kernel(te_ref, nv_ref, x_ref, w1_ref, w3_ref, w2_ref, o_ref, acc_ref):
    i = pl.program_id(0)
    j = pl.program_id(1)

    @pl.when(i < nv_ref[0])
    def _():
        y = _swiglu_halves(x_ref[...].astype(BF16), w1_ref, w3_ref, w2_ref, lead=(0,))

        @pl.when(j == 0)
        def _():
            acc_ref[...] = y

        @pl.when(j > 0)
        def _():
            acc_ref[...] += y

        @pl.when(j == pl.num_programs(1) - 1)
        def _():
            o_ref[...] = acc_ref[...]


def _experts(x_sorted, tile_expert, n_valid, w1, w3, w2):
    rows = x_sorted.shape[0]
    d = w1.shape[1]
    dff = w1.shape[2]
    tm = EXPERT_ROW_TILE
    n_ff = 2
    tf = dff // n_ff

    def row_map(i, j, te, nv):
        return (jnp.maximum(jnp.minimum(i, nv[0] - 1), 0), 0)

    def w13_map(i, j, te, nv):
        return (te[jnp.maximum(jnp.minimum(i, nv[0] - 1), 0)], 0, jnp.where(i < nv[0], j, n_ff - 1))

    def w2_map(i, j, te, nv):
        return (te[jnp.maximum(jnp.minimum(i, nv[0] - 1), 0)], jnp.where(i < nv[0], j, n_ff - 1), 0)

    return pl.pallas_call(
        _expert_kernel,
        grid_spec=pltpu.PrefetchScalarGridSpec(
            num_scalar_prefetch=2,
            grid=(rows // tm, n_ff),
            in_specs=[
                pl.BlockSpec((tm, d), row_map),
                pl.BlockSpec((1, d, tf), w13_map),
                pl.BlockSpec((1, d, tf), w13_map),
                pl.BlockSpec((1, tf, d), w2_map),
            ],
            out_specs=pl.BlockSpec((tm, d), row_map),
            scratch_shapes=[pltpu.VMEM((tm, d), F32)],
        ),
        out_shape=jax.ShapeDtypeStruct((rows, d), F32),
        compiler_params=_params(("arbitrary", "arbitrary")),
        name="expert_swiglu",
    )(tile_expert, n_valid, x_sorted, w1, w3, w2)


def _combine_kernel(x_ref, y_ref, meta_ref, g_ref, mod_ref, *rest):
    o_ref = rest[-1]
    meta = meta_ref[0]
    y = meta[:, 2:3] * y_ref[0, 0] + meta[:, 3:4] * y_ref[1, 0]
    o_ref[0] = x_ref[0] + mod_ref[0, 5:6, :] * _rms(y, g_ref[3:4, :])


def _combine(x, y2_part, meta, g4, mods, first_batch, out_so_far):
    bsz, length, d = x.shape
    part = y2_part.shape[1]
    tm = min(ROW_TILE, length)
    in_specs = [
        pl.BlockSpec((1, tm, d), lambda b, i: (first_batch + b, i, 0)),
        pl.BlockSpec((2, 1, tm, d), lambda b, i: (0, b, i, 0)),
        pl.BlockSpec((1, tm, LANES), lambda b, i: (first_batch + b, i, 0)),
        pl.BlockSpec((4, d), lambda b, i: (0, 0)),
        pl.BlockSpec((1, 6, d), lambda b, i: (first_batch + b, 0, 0)),
    ]
    args = [x, y2_part, meta, g4, mods]
    aliases = {}
    if out_so_far is not None:
        in_specs.append(pl.BlockSpec(memory_space=pl.ANY))
        args.append(out_so_far)
        aliases = {5: 0}
    return pl.pallas_call(
        _combine_kernel,
        grid=(part, length // tm),
        in_specs=in_specs,
        out_specs=pl.BlockSpec((1, tm, d), lambda b, i: (first_batch + b, i, 0)),
        out_shape=jax.ShapeDtypeStruct((bsz, length, d), F32),
        input_output_aliases=aliases,
        compiler_params=_params(("parallel", "parallel")),
        name="moe_combine",
    )(*args)


SC_CORES = 2
SC_SUBCORES = 16
SC_WORKERS = SC_CORES * SC_SUBCORES
SC_ROW_CHUNK = 32
COMBINE_GROUPS = 4


def _sc_mesh():
    return plsc.VectorSubcoreMesh(core_axis_name="c", subcore_axis_name="s")


def _sc_dispatch(rows_in, pos2, n_out):
    tokens, width = rows_in.shape
    per_worker = tokens // SC_WORKERS
    chunk = SC_ROW_CHUNK
    n_pairs = per_worker // (2 * chunk)
    idx = pos2.reshape(2, SC_WORKERS, 2 * n_pairs, chunk)

    @functools.partial(
        pl.kernel, mesh=_sc_mesh(),
        out_type=jax.ShapeDtypeStruct((n_out, width), rows_in.dtype),
        scratch_types=[
            pltpu.VMEM((2 * n_pairs, chunk), jnp.int32),
            pltpu.VMEM((2 * n_pairs, chunk), jnp.int32),
            pltpu.VMEM((2, chunk, width), rows_in.dtype),
        ] + [pltpu.SemaphoreType.DMA] * 6,
        name="sc_dispatch",
    )
    def body(rows_hbm, idx_hbm, out_hbm, idx0_v, idx1_v, rows_v, rsem0, rsem1, wsem0, wsem1, wsem2, wsem3):
        wid = lax.axis_index("s") * SC_CORES + lax.axis_index("c")
        base = wid * per_worker
        rsem = (rsem0, rsem1)
        wsem = ((wsem0, wsem1), (wsem2, wsem3))
        pltpu.sync_copy(idx_hbm.at[0, wid], idx0_v)
        pltpu.sync_copy(idx_hbm.at[1, wid], idx1_v)

        @pl.loop(0, n_pairs)
        def _(p):
            c0 = 2 * p
            reads = [pltpu.async_copy(rows_hbm.at[pl.ds(base + (c0 + s) * chunk, chunk)], rows_v.at[s], rsem[s])
                     for s in range(2)]
            writes = []
            for s in range(2):
                reads[s].wait()
                writes.append(pltpu.async_copy(rows_v.at[s], out_hbm.at[idx0_v.at[c0 + s]], wsem[s][0]))
                writes.append(pltpu.async_copy(rows_v.at[s], out_hbm.at[idx1_v.at[c0 + s]], wsem[s][1]))
            for w in writes:
                w.wait()

    return body(rows_in, idx)


def _sc_gather(table, idx):
    n = idx.shape[0]
    width = table.shape[1]
    chunk = SC_ROW_CHUNK
    per_worker = n // SC_WORKERS
    n_pairs = per_worker // (2 * chunk)
    idx3 = idx.reshape(SC_WORKERS, 2 * n_pairs, chunk)

    @functools.partial(
        pl.kernel, mesh=_sc_mesh(),
        out_type=jax.ShapeDtypeStruct((n, width), table.dtype),
        scratch_types=[
            pltpu.VMEM((2 * n_pairs, chunk), jnp.int32),
            pltpu.VMEM((2, chunk, width), table.dtype),
            pltpu.SemaphoreType.DMA, pltpu.SemaphoreType.DMA,
            pltpu.SemaphoreType.DMA, pltpu.SemaphoreType.DMA,
        ],
        name="sc_gather",
    )
    def body(table_hbm, idx_hbm, out_hbm, idx_v, rows_v, gsem0, gsem1, wsem0, wsem1):
        wid = lax.axis_index("s") * SC_CORES + lax.axis_index("c")
        base = wid * per_worker
        gsem = (gsem0, gsem1)
        wsem = (wsem0, wsem1)
        pltpu.sync_copy(idx_hbm.at[wid], idx_v)

        @pl.loop(0, n_pairs)
        def _(p):
            c0 = 2 * p
            gathers = [pltpu.async_copy(table_hbm.at[idx_v.at[c0 + s]], rows_v.at[s], gsem[s]) for s in range(2)]
            writes = []
            for s in range(2):
                gathers[s].wait()
                rows_out = out_hbm.at[pl.ds(base + (c0 + s) * chunk, chunk)]
                writes.append(pltpu.async_copy(rows_v.at[s], rows_out, wsem[s]))
            for s in range(2):
                writes[s].wait()

    return body(table, idx3)


def _moe(x, mods, g4, w_router, w1, w3, w2):
    bsz, length, d = x.shape
    tokens = bsz * length
    tm = EXPERT_ROW_TILE
    xn, meta, counts = _router(x, mods, g4, w_router)
    meta2 = meta.reshape(tokens, LANES)
    experts = meta2[:, 0:2].astype(jnp.int32)
    ranks = meta2[:, 4:6].astype(jnp.int32)
    cnt = counts[0, 0:N_EXPERTS].astype(jnp.int32)
    padded = ((cnt + tm - 1) // tm) * tm
    ends = jnp.cumsum(padded)
    starts = ends - padded
    pos = (starts[experts] + ranks).T.reshape(2 * tokens)
    rows = 2 * tokens + N_EXPERTS * tm
    n_tiles = rows // tm
    n_valid = (ends[-1] // tm).astype(jnp.int32).reshape(1)
    tile_start = jnp.arange(n_tiles, dtype=jnp.int32) * tm
    tile_expert = jnp.minimum(
        jnp.sum((tile_start[:, None] >= ends[None, :]).astype(jnp.int32), axis=1), N_EXPERTS - 1)
    x_sorted = _sc_dispatch(xn.reshape(tokens, d), pos.reshape(2, tokens), rows)
    y_sorted = _experts(x_sorted, tile_expert, n_valid, w1, w3, w2)
    groups = COMBINE_GROUPS if bsz % COMBINE_GROUPS == 0 else 1
    per = bsz // groups
    pos3 = pos.reshape(2, bsz, length)
    out = None
    for q in range(groups):
        idx = pos3[:, q * per:(q + 1) * per].reshape(2 * per * length)
        y2 = _sc_gather(y_sorted, idx).reshape(2, per, length, d)
        out = _combine(x, y2, meta, g4, mods, q * per, out)
    return out


def _grid_pos_embed(rows, d):
    row = np.repeat(np.arange(rows, dtype=np.float64), GRID_W)
    col = np.tile(np.arange(GRID_W, dtype=np.float64), rows)
    n_freq = d // 4
    omega = POS_THETA ** (-np.arange(n_freq, dtype=np.float64) / n_freq)
    ang_r = row[:, None] * omega
    ang_c = col[:, None] * omega
    return jnp.asarray(np.concatenate([np.sin(ang_r), np.cos(ang_r), np.sin(ang_c), np.cos(ang_c)], axis=-1),
                       dtype=F32)


def kernel(x, c, ctx, c_ctx, ada_w, ada_b, norm_g, ab_w_in, rg_conv_w, rg_conv_b, rg_w_r, rg_b_r, rg_w_i, rg_b_i, rg_lambda, cv_dw_w, cv_dw_b, cv_ln_g, cv_ln_b, ab_w_out, ffn_w1, ffn_w3, ffn_w2, cd_w_in, sg_ln_g, sg_ln_b, sg_w_s, sg_b_s, cd_w_out, moe_router, moe_w1, moe_w3, moe_w2):
    bsz, length, d = x.shape
    assert ada_w.shape[0] == 2, "two layers: one even (RG-LRU | Conformer), one odd (gMLP | Fourier)"
    pos = _grid_pos_embed(length // GRID_W, d)

    cpad = jnp.concatenate([c, c_ctx[None, :], jnp.zeros((16 - bsz - 1, d), F32)], axis=0)
    m = _ada(cpad, ada_w, ada_b)
    mods = m[:, :bsz].reshape(2, bsz, 6, d)
    mods_ctx = m[:, bsz:bsz + 1].reshape(2, 1, 6, d)

    mods_t = mods[0].transpose(1, 0, 2)
    mods_ctx_t = jnp.broadcast_to(mods_ctx[0].reshape(6, 1, d), (6, bsz, d))
    pos3 = pos[:, None, :]
    w_in = ab_w_in[0]
    xa3, gg3, glu3, ffn1b, ffn3b, ffn2b = _inproj_t(x, pos3, mods_t, norm_g[0], w_in, rg_conv_w[0], rg_conv_b[0], True,
                                                    riders=(ffn_w1[0], ffn_w3[0], ffn_w2[0]))
    (xc3,) = _inproj_t(ctx, None, mods_ctx_t, norm_g[0], w_in[:, 0:A_WIDTH], rg_conv_w[0], rg_conv_b[0], False)
    wg, bg, sp = _gate_weights(rg_w_r[0], rg_b_r[0], rg_w_i[0], rg_b_i[0], rg_lambda[0])
    n_exp, _, dff_e = moe_w1[0].shape
    rec_hi, rec_lo, w2b = _scan_t(xa3, xc3, gg3, wg, bg, sp, riders=(moe_w2[0].reshape(n_exp * dff_e, d),))
    u3, w1b, w3b = _conv_t(glu3, cv_dw_w[0], cv_dw_b[0], cv_ln_g[0], cv_ln_b[0],
                           riders=(moe_w1[0].reshape(n_exp * d, dff_e), moe_w3[0].reshape(n_exp * d, dff_e)))
    x1_t = _outproj_t(rec_lo, rec_hi, u3, x, pos3, ab_w_out[0], norm_g[0], mods_t)
    x2 = _ffn_t(x1_t, mods_t, norm_g[0], ffn1b, ffn3b, ffn2b)

    u, v, f = _inproj_cd(x2, mods[1], norm_g[1], cd_w_in[0])
    fo = _fourier(f)
    x3 = _gate_out(u, v, fo, x2, sg_ln_g[0], sg_ln_b[0], sg_w_s[0], sg_b_s[0], cd_w_out[0],
                   norm_g[1], mods[1])
    return _moe(x3, mods[1], norm_g[1], moe_router[0], w1b.reshape(n_exp, d, dff_e), w3b.reshape(n_exp, d, dff_e),
                w2b.reshape(n_exp, dff_e, d))
```

```python
import functools

import numpy as np
import jax
import jax.numpy as jnp
from jax import lax
from jax.experimental import pallas as pl
from jax.experimental.pallas import tpu as pltpu
from jax.experimental.pallas import tpu_sc as plsc

F32 = jnp.float32
BF16 = jnp.bfloat16

GRID_W = 64
POS_THETA = 10000.0
NORM_EPS = 1e-6
LN_EPS = 1e-5
LRU_C = 8.0

A_WIDTH = 512
A_HEADS = 8
A_CONV_W = 4
B_WIDTH = 512
B_CONV_W = 31
C_WIDTH = 512
C_HEADS = 8
CHUNK = 128
D_WIDTH = 512
D_GROUPS = 4
D_GROUP_DIM = D_WIDTH // D_GROUPS
N_EXPERTS = 8

LANES = 128
ROW_TILE = 512
EXPERT_ROW_TILE = 512
VMEM_LIMIT = 56 * 2 ** 20


def _params(sem):
    return pltpu.CompilerParams(dimension_semantics=sem, vmem_limit_bytes=VMEM_LIMIT)


def _rms(x, g):
    return x * lax.rsqrt(jnp.mean(x * x, axis=-1, keepdims=True) + NORM_EPS) * g


def _bdot(a, b):
    return jnp.dot(a, b, preferred_element_type=F32)


MXU_WIDTH = 256


def _swiglu_halves(x, w1_ref, w3_ref, w2_ref, lead=()):
    n = w1_ref.shape[-1]
    half = -(-(n // 2) // MXU_WIDTH) * MXU_WIDTH
    y = None
    for lo, hi in ((0, half), (half, n)):
        a = _bdot(x, w1_ref[lead + (slice(None), slice(lo, hi))])
        b = _bdot(x, w3_ref[lead + (slice(None), slice(lo, hi))])
        t = (jax.nn.silu(a) * b).astype(BF16)
        part = _bdot(t, w2_ref[lead + (slice(lo, hi), slice(None))])
        y = part if y is None else y + part
    return y


def _ada_kernel(c_ref, w_ref, b_ref, o_ref):
    s = jax.nn.silu(c_ref[...])
    o_ref[0] = _bdot(s.astype(BF16), w_ref[0].astype(BF16)) + b_ref[0]


def _ada(cpad, ada_w, ada_b):
    nl, d, n6 = ada_w.shape
    rows = cpad.shape[0]
    tn = n6 // 4
    return pl.pallas_call(
        _ada_kernel,
        grid=(nl, n6 // tn),
        in_specs=[
            pl.BlockSpec((rows, d), lambda l, j: (0, 0)),
            pl.BlockSpec((1, d, tn), lambda l, j: (l, 0, j)),
            pl.BlockSpec((1, 1, tn), lambda l, j: (l, 0, j)),
        ],
        out_specs=pl.BlockSpec((1, rows, tn), lambda l, j: (l, 0, j)),
        out_shape=jax.ShapeDtypeStruct((nl, rows, n6), F32),
        compiler_params=_params(("arbitrary", "arbitrary")),
        name="ada_terms",
    )(cpad, ada_w, ada_b.reshape(nl, 1, n6))


TIME_TILE = 128
SCAN_TIME_TILE = 128
SCAN_LANES = 256
CONV_TIME_TILE = 64
CONV_HALO = 16


def _time_major(ref):
    return jnp.swapaxes(ref[...], 0, 1)


def _inproj_t_kernel(*refs, tt, add_pos, branches, n_cast):
    if n_cast:
        n_in = len(refs) - n_cast - (3 if branches else 1) - n_cast
        _cast_riders(refs[n_in:n_in + n_cast], refs[len(refs) - n_cast:])
        refs = refs[:n_in] + refs[n_in + n_cast:len(refs) - n_cast]
    if add_pos:
        x_ref, xp_ref, xn_ref, pos_ref, posp_ref, posn_ref, mod_ref, g_ref, w_ref, cw_ref, cb_ref, *outs = refs
        xall = jnp.concatenate(
            [_time_major(xp_ref)[7:8] + posp_ref[...], _time_major(x_ref) + pos_ref[...],
             _time_major(xn_ref)[0:2] + posn_ref[...]], axis=0)
    else:
        x_ref, xp_ref, xn_ref, mod_ref, g_ref, w_ref, cw_ref, cb_ref, *outs = refs
        xall = jnp.concatenate([_time_major(xp_ref)[7:8], _time_major(x_ref), _time_major(xn_ref)[0:2]], axis=0)
    i = pl.program_id(0)
    last = pl.num_programs(0) - 1
    steps, bsz, d = xall.shape
    h = _rms(xall, g_ref[0:1, :]) * (1.0 + mod_ref[1]) + mod_ref[0]
    z = _bdot(h.reshape(steps * bsz, d).astype(BF16), w_ref[...].astype(BF16))
    z3 = z.reshape(steps, bsz, z.shape[-1])
    t_idx = lax.broadcasted_iota(jnp.int32, (steps, bsz, A_WIDTH), 0)
    inside = jnp.logical_and(jnp.logical_or(t_idx >= 1, i > 0), jnp.logical_or(t_idx <= tt, i < last))
    xa = jnp.where(inside, z3[:, :, 0:A_WIDTH], 0.0)
    xconv = cb_ref[...] + cw_ref[0:1, :] * xa[0:tt]
    for k in range(1, A_CONV_W):
        xconv = xconv + cw_ref[k:k + 1, :] * xa[k:k + tt]
    outs[0][...] = xconv
    if branches:
        outs[1][...] = jax.nn.gelu(z3[1:tt + 1, :, A_WIDTH:2 * A_WIDTH])
        vb = z3[1:tt + 1, :, 2 * A_WIDTH:2 * A_WIDTH + B_WIDTH]
        gb = z3[1:tt + 1, :, 2 * A_WIDTH + B_WIDTH:]
        outs[2][...] = vb * jax.nn.sigmoid(gb)


def _inproj_t(x, pos3, mods_t, g4, w, conv_w, conv_b, branches, riders=()):
    bsz, length, d = x.shape
    n = w.shape[1]
    tt = min(TIME_TILE, length)
    add_pos = pos3 is not None
    n_out = 3 if branches else 1
    prev_map = lambda i: (jnp.maximum(i * tt - 1, 0), 0, 0)
    next_map = lambda i: (jnp.minimum((i + 1) * (tt // 2), length // 2 - 1), 0, 0)
    in_specs = [
        pl.BlockSpec((bsz, tt, d), lambda i: (0, i, 0)),
        pl.BlockSpec((bsz, 8, d), lambda i: (0, jnp.maximum(i * (tt // 8) - 1, 0), 0)),
        pl.BlockSpec((bsz, 8, d), lambda i: (0, jnp.minimum((i + 1) * (tt // 8), length // 8 - 1), 0)),
    ]
    args = [x, x, x]
    if add_pos:
        in_specs += [
            pl.BlockSpec((tt, 1, d), lambda i: (i, 0, 0)),
            pl.BlockSpec((1, 1, d), prev_map),
            pl.BlockSpec((2, 1, d), next_map),
        ]
        args += [pos3, pos3, pos3]
    in_specs += [
        pl.BlockSpec((6, bsz, d), lambda i: (0, 0, 0)),
        pl.BlockSpec((4, d), lambda i: (0, 0)),
        pl.BlockSpec((d, n), lambda i: (0, 0)),
        pl.BlockSpec((A_CONV_W, A_WIDTH), lambda i: (0, 0)),
        pl.BlockSpec((1, A_WIDTH), lambda i: (0, 0)),
    ]
    args += [mods_t, g4, w, conv_w, conv_b.reshape(1, A_WIDTH)]
    r_in, r_out, r_shapes = _rider_specs(riders, length // tt, lambda i: i)
    return pl.pallas_call(
        functools.partial(_inproj_t_kernel, tt=tt, add_pos=add_pos, branches=branches, n_cast=len(riders)),
        grid=(length // tt,),
        in_specs=in_specs + r_in,
        out_specs=[pl.BlockSpec((tt, bsz, A_WIDTH), lambda i: (i, 0, 0))] * n_out + r_out,
        out_shape=[jax.ShapeDtypeStruct((length, bsz, A_WIDTH), F32)] * n_out + r_shapes,
        compiler_params=_params(("parallel",)),
        name="inproj_ab" if branches else "inproj_ctx",
    )(*args, *riders)


def _cast_riders(cast_in, cast_out):
    for src, dst in zip(cast_in, cast_out):
        dst[...] = src[...].astype(BF16)


def _rider_specs(arrays, n_steps, step_of):
    in_specs, out_specs, out_shapes = [], [], []
    for arr in arrays:
        rows, cols = arr.shape
        block = (rows // n_steps, cols)
        in_specs.append(pl.BlockSpec(block, lambda *idx: (step_of(*idx), 0)))
        out_specs.append(pl.BlockSpec(block, lambda *idx: (step_of(*idx), 0)))
        out_shapes.append(jax.ShapeDtypeStruct(arr.shape, BF16))
    return in_specs, out_specs, out_shapes


def _scan_t_kernel(*refs, tb, nb, ctx_len, n_cast):
    xf_ref, xb_ref, ggf_ref, ggb_ref, xc_ref, wg_ref, bg_ref, sp_ref = refs[:8]
    cast_in = refs[8:8 + n_cast]
    hi_ref, lo_ref = refs[8 + n_cast:10 + n_cast]
    cast_out = refs[10 + n_cast:10 + 2 * n_cast]
    a_ref, b_ref, h_ref, state_ref = refs[10 + 2 * n_cast:]
    _cast_riders(cast_in, cast_out)
    k = pl.program_id(1)
    half = nb // 2
    bsz, lanes = xf_ref.shape[1], xf_ref.shape[2]
    chunk = 128

    def coefficients(x_ref, n, direction):
        def body(c, _):
            t0 = pl.multiple_of(c * chunk, chunk)
            x = x_ref[pl.ds(t0, chunk)].reshape(chunk * bsz, lanes)
            th = jnp.tanh(_bdot(x.astype(BF16), wg_ref[0, direction]) + bg_ref[0, direction])
            i = 0.5 * th[:, lanes:] + 0.5
            c = (-0.5 * LRU_C) * sp_ref[0, direction:direction + 1, :]
            log_a = c * th[:, 0:lanes] + c
            a = jnp.exp(log_a)
            one_minus_a2 = -jnp.tanh(log_a) * (a * a + 1.0)
            a_ref[direction, pl.ds(t0, chunk)] = a.reshape(chunk, bsz, lanes)
            b_ref[direction, pl.ds(t0, chunk)] = (jnp.sqrt(one_minus_a2) * (i * x)).reshape(chunk, bsz, lanes)
            return 0
        lax.fori_loop(0, n // chunk, body, 0)

    def sweep(n, store):
        def body(t, carry):
            hf, hb = carry
            tr = n - 1 - t
            hf = a_ref[0, pl.ds(t, 1)][0] * hf + b_ref[0, pl.ds(t, 1)][0]
            hb = a_ref[1, pl.ds(tr, 1)][0] * hb + b_ref[1, pl.ds(tr, 1)][0]
            if store:
                b_ref[0, pl.ds(t, 1)] = hf[None]
                b_ref[1, pl.ds(tr, 1)] = hb[None]
            return hf, hb
        hf, hb = lax.fori_loop(0, n, body, (state_ref[0], state_ref[1]), unroll=8)
        state_ref[0] = hf
        state_ref[1] = hb

    @pl.when(k == 0)
    def _():
        state_ref[...] = jnp.zeros_like(state_ref)
        coefficients(xc_ref, ctx_len, 0)
        coefficients(xc_ref, ctx_len, 1)
        sweep(ctx_len, False)

    coefficients(xf_ref, tb, 0)
    coefficients(xb_ref, tb, 1)
    sweep(tb, True)
    m = nb - 1 - k

    @pl.when(k < half)
    def _():
        h_ref[k] = b_ref[0, 0:tb]
        h_ref[m] = b_ref[1, 0:tb]

    @pl.when(k >= half)
    def _():
        hi_ref[...] = (b_ref[0, 0:tb] + h_ref[k]) * ggf_ref[...]
        lo_ref[...] = (h_ref[m] + b_ref[1, 0:tb]) * ggb_ref[...]


def _scan_t(xa3, xc3, gg3, wg, bg, sp, riders=()):
    length, bsz, width = xa3.shape
    ctx_len = xc3.shape[0]
    tb = min(SCAN_TIME_TILE, length // 2)
    nb = length // tb
    half = nb // 2
    lanes = SCAN_LANES
    blk = (tb, bsz, lanes)
    r_in, r_out, r_shapes = _rider_specs(riders, (width // lanes) * nb, lambda g, k: g * nb + k)
    return pl.pallas_call(
        functools.partial(_scan_t_kernel, tb=tb, nb=nb, ctx_len=ctx_len, n_cast=len(riders)),
        grid=(width // lanes, nb),
        in_specs=[
            pl.BlockSpec(blk, lambda g, k: (k, 0, g)),
            pl.BlockSpec(blk, lambda g, k: (nb - 1 - k, 0, g)),
            pl.BlockSpec(blk, lambda g, k: (jnp.maximum(k, half), 0, g)),
            pl.BlockSpec(blk, lambda g, k: (jnp.minimum(nb - 1 - k, half - 1), 0, g)),
            pl.BlockSpec((ctx_len, bsz, lanes), lambda g, k: (0, 0, g)),
            pl.BlockSpec((1, 2, lanes, 2 * lanes), lambda g, k: (g, 0, 0, 0)),
            pl.BlockSpec((1, 2, 1, 2 * lanes), lambda g, k: (g, 0, 0, 0)),
            pl.BlockSpec((1, 2, lanes), lambda g, k: (g, 0, 0)),
        ] + r_in,
        out_specs=[
            pl.BlockSpec(blk, lambda g, k: (jnp.maximum(k - half, 0), 0, g)),
            pl.BlockSpec(blk, lambda g, k: (jnp.minimum(nb - 1 - k, half - 1), 0, g)),
        ] + r_out,
        out_shape=[jax.ShapeDtypeStruct((length // 2, bsz, width), F32)] * 2 + r_shapes,
        scratch_shapes=[
            pltpu.VMEM((2, max(tb, ctx_len), bsz, lanes), F32),
            pltpu.VMEM((2, max(tb, ctx_len), bsz, lanes), F32),
            pltpu.VMEM((nb, tb, bsz, lanes), F32),
            pltpu.VMEM((2, bsz, lanes), F32),
        ],
        compiler_params=_params(("parallel", "arbitrary")),
        name="rglru_scan",
    )(xa3, xa3, gg3, gg3, xc3, wg, bg, sp, *riders)


def _gate_weights(w_r, b_r, w_i, b_i, lam):
    hd = A_WIDTH // A_HEADS
    heads_per_group = SCAN_LANES // hd
    ngroups = A_WIDTH // SCAN_LANES

    def blockdiag(w):
        w = w.reshape(ngroups, heads_per_group, hd, hd)
        eye = jnp.eye(heads_per_group, dtype=w.dtype)
        return jnp.einsum("ghde,hk->ghdke", w, eye).reshape(ngroups, SCAN_LANES, SCAN_LANES)

    wg = jnp.stack([jnp.concatenate([blockdiag(w_r[d]), blockdiag(w_i[d])], axis=-1) for d in range(2)], axis=1)
    bg = jnp.stack([jnp.concatenate([b_r[d].reshape(ngroups, 1, SCAN_LANES), b_i[d].reshape(ngroups, 1, SCAN_LANES)],
                                    axis=-1) for d in range(2)], axis=1)
    sp = jax.nn.softplus(-lam.astype(F32)).reshape(2, ngroups, SCAN_LANES).transpose(1, 0, 2)
    return (0.5 * wg).astype(BF16), 0.5 * bg, sp


def _conv_t_kernel(*refs, tb, n_cast):
    x_ref, xp_ref, xn_ref, w_ref, b_ref, lng_ref, lnb_ref = refs[:7]
    cast_in = refs[7:7 + n_cast]
    o_ref = refs[7 + n_cast]
    cast_out = refs[8 + n_cast:8 + 2 * n_cast]
    stage_ref = refs[-1]
    _cast_riders(cast_in, cast_out)
    i = pl.program_id(0)
    last = pl.num_programs(0) - 1
    halo = CONV_HALO
    stage_ref[0:halo] = jnp.where(i > 0, xp_ref[...], 0.0)
    stage_ref[halo:halo + tb] = x_ref[...]
    stage_ref[halo + tb:2 * halo + tb] = jnp.where(i < last, xn_ref[...], 0.0)
    sub = 8

    def body(r, _):
        t0 = pl.multiple_of(r * sub, sub)
        def tap(k, acc):
            return acc + w_ref[k] * stage_ref[pl.ds(t0 + 1 + k, sub)]
        acc = lax.fori_loop(0, B_CONV_W + 1, tap, jnp.broadcast_to(b_ref[...], (sub,) + b_ref.shape), unroll=8)
        o_ref[pl.ds(t0, sub)] = acc
        return 0
    lax.fori_loop(0, tb // sub, body, 0)

    norm_rows = 16

    def norm_body(r, _):
        t0 = pl.multiple_of(r * norm_rows, norm_rows)
        acc = o_ref[pl.ds(t0, norm_rows)]
        mu = jnp.mean(acc, axis=-1, keepdims=True)
        cen = acc - mu
        var = jnp.mean(cen * cen, axis=-1, keepdims=True)
        y = cen * lax.rsqrt(var + LN_EPS) * lng_ref[...] + lnb_ref[...]
        o_ref[pl.ds(t0, norm_rows)] = jax.nn.silu(y)
        return 0
    lax.fori_loop(0, tb // norm_rows, norm_body, 0)


def _conv_t(glu3, dw_w, dw_b, ln_g, ln_b, riders=()):
    length, bsz, width = glu3.shape
    tb = min(CONV_TIME_TILE, length)
    halo = CONV_HALO
    taps = dw_w.shape[0]
    w8 = jnp.broadcast_to(jnp.concatenate([dw_w, jnp.zeros((32 - taps, width), F32)], axis=0)[:, None, :],
                          (32, bsz, width))
    b8 = jnp.broadcast_to(dw_b[None, :], (bsz, width))
    r_in, r_out, r_shapes = _rider_specs(riders, length // tb, lambda i: i)
    return pl.pallas_call(
        functools.partial(_conv_t_kernel, tb=tb, n_cast=len(riders)),
        grid=(length // tb,),
        in_specs=[
            pl.BlockSpec((tb, bsz, width), lambda i: (i, 0, 0)),
            pl.BlockSpec((halo, bsz, width), lambda i: (jnp.maximum(i * (tb // halo) - 1, 0), 0, 0)),
            pl.BlockSpec((halo, bsz, width), lambda i: (jnp.minimum((i + 1) * (tb // halo), length // halo - 1), 0, 0)),
            pl.BlockSpec((32, bsz, width), lambda i: (0, 0, 0)),
            pl.BlockSpec((bsz, width), lambda i: (0, 0)),
            pl.BlockSpec((1, width), lambda i: (0, 0)),
            pl.BlockSpec((1, width), lambda i: (0, 0)),
        ] + r_in,
        out_specs=[pl.BlockSpec((tb, bsz, width), lambda i: (i, 0, 0))] + r_out,
        out_shape=[jax.ShapeDtypeStruct((length, bsz, width), F32)] + r_shapes,
        scratch_shapes=[pltpu.VMEM((tb + 2 * halo, bsz, width), F32)],
        compiler_params=_params(("parallel",)),
        name="conformer_conv",
    )(glu3, glu3, glu3, w8, b8, ln_g.reshape(1, -1), ln_b.reshape(1, -1), *riders)


def _outproj_t_kernel(lo_ref, hi_ref, u_ref, x_ref, pos_ref, wout_ref, g_ref, mod_ref, o_ref, *, half_steps):
    i = pl.program_id(0)
    bsz, tt, d = x_ref.shape
    rec = jnp.where(i < half_steps, lo_ref[...], hi_ref[...])
    rows = tt * bsz
    y = (_bdot(rec.reshape(rows, A_WIDTH).astype(BF16), wout_ref[0:A_WIDTH, :].astype(BF16))
         + _bdot(u_ref[...].reshape(rows, B_WIDTH).astype(BF16), wout_ref[A_WIDTH:, :].astype(BF16)))
    yn = _rms(y, g_ref[1:2, :]).reshape(tt, bsz, d)
    o_ref[...] = _time_major(x_ref) + pos_ref[...] + mod_ref[2] * yn


def _outproj_t(rec_lo, rec_hi, u3, x, pos3, w_out, g4, mods_t):
    bsz, length, d = x.shape
    tt = min(TIME_TILE // 2, length // 2)
    half_steps = (length // 2) // tt
    return pl.pallas_call(
        functools.partial(_outproj_t_kernel, half_steps=half_steps),
        grid=(length // tt,),
        in_specs=[
            pl.BlockSpec((tt, bsz, A_WIDTH), lambda i: (jnp.minimum(i, half_steps - 1), 0, 0)),
            pl.BlockSpec((tt, bsz, A_WIDTH), lambda i: (jnp.maximum(i - half_steps, 0), 0, 0)),
            pl.BlockSpec((tt, bsz, B_WIDTH), lambda i: (i, 0, 0)),
            pl.BlockSpec((bsz, tt, d), lambda i: (0, i, 0)),
            pl.BlockSpec((tt, 1, d), lambda i: (i, 0, 0)),
            pl.BlockSpec((A_WIDTH + B_WIDTH, d), lambda i: (0, 0)),
            pl.BlockSpec((4, d), lambda i: (0, 0)),
            pl.BlockSpec((6, bsz, d), lambda i: (0, 0, 0)),
        ],
        out_specs=pl.BlockSpec((tt, bsz, d), lambda i: (i, 0, 0)),
        out_shape=jax.ShapeDtypeStruct((length, bsz, d), F32),
        compiler_params=_params(("parallel",)),
        name="mixer_ab_outproj",
    )(rec_lo, rec_hi, u3, x, pos3, w_out, g4, mods_t)


def _ffn_t_kernel(x_ref, mod_ref, g_ref, w1_ref, w3_ref, w2_ref, *rest, n_cast):
    _cast_riders(rest[:n_cast], rest[n_cast + 1:])
    o_ref = rest[n_cast]
    x = x_ref[...]
    tt, bsz, d = x.shape
    h = (_rms(x, g_ref[2:3, :]) * (1.0 + mod_ref[4]) + mod_ref[3]).reshape(tt * bsz, d).astype(BF16)
    y = _swiglu_halves(h, w1_ref, w3_ref, w2_ref)
    o_ref[...] = jnp.swapaxes(x + mod_ref[5] * _rms(y, g_ref[3:4, :]).reshape(tt, bsz, d), 0, 1)


def _ffn_t(x3, mods_t, g4, w1, w3, w2, riders=()):
    length, bsz, d = x3.shape
    dff = w1.shape[1]
    tt = min(TIME_TILE // 2, length)
    r_in, r_out, r_shapes = _rider_specs(riders, length // tt, lambda i: i)
    return pl.pallas_call(
        functools.partial(_ffn_t_kernel, n_cast=len(riders)),
        grid=(length // tt,),
        in_specs=[
            pl.BlockSpec((tt, bsz, d), lambda i: (i, 0, 0)),
            pl.BlockSpec((6, bsz, d), lambda i: (0, 0, 0)),
            pl.BlockSpec((4, d), lambda i: (0, 0)),
            pl.BlockSpec((d, dff), lambda i: (0, 0), pipeline_mode=pl.Buffered(1)),
            pl.BlockSpec((d, dff), lambda i: (0, 0), pipeline_mode=pl.Buffered(1)),
            pl.BlockSpec((dff, d), lambda i: (0, 0), pipeline_mode=pl.Buffered(1)),
        ] + r_in,
        out_specs=[pl.BlockSpec((bsz, tt, d), lambda i: (0, i, 0))] + r_out,
        out_shape=[jax.ShapeDtypeStruct((bsz, length, d), F32)] + r_shapes,
        compiler_params=_params(("parallel",)),
        name="dense_swiglu",
    )(x3, mods_t, g4, w1, w3, w2, *riders)


def _inproj_cd_kernel(x_ref, mod_ref, g_ref, w_ref, u_ref, v_ref, f_ref):
    x = x_ref[0]
    h = _rms(x, g_ref[0:1, :]) * (1.0 + mod_ref[0, 1:2, :]) + mod_ref[0, 0:1, :]
    z = _bdot(h.astype(BF16), w_ref[...].astype(BF16))
    u_ref[0] = jax.nn.gelu(z[:, 0:C_WIDTH])
    v_ref[0] = jax.nn.gelu(z[:, C_WIDTH:2 * C_WIDTH])
    f_ref[0] = z[:, 2 * C_WIDTH:]


def _inproj_cd(x, mods, g4, w):
    bsz, length, d = x.shape
    n = w.shape[1]
    tm = min(ROW_TILE, length)
    return pl.pallas_call(
        _inproj_cd_kernel,
        grid=(bsz, length // tm),
        in_specs=[
            pl.BlockSpec((1, tm, d), lambda b, i: (b, i, 0)),
            pl.BlockSpec((1, 6, d), lambda b, i: (b, 0, 0)),
            pl.BlockSpec((4, d), lambda b, i: (0, 0)),
            pl.BlockSpec((d, n), lambda b, i: (0, 0)),
        ],
        out_specs=[pl.BlockSpec((1, tm, C_WIDTH), lambda b, i: (b, i, 0))] * 3,
        out_shape=[jax.ShapeDtypeStruct((bsz, length, C_WIDTH), F32)] * 3,
        compiler_params=_params(("parallel", "parallel")),
        name="inproj_cd",
    )(x, mods, g4, w)


def _fourier_kernel(f_ref, dmat_ref, lmat_ref, o_ref, z_ref, *, length, scale):
    j = pl.program_id(1)
    rows_per_step = 512 if length % 512 == 0 else length

    @pl.when(j == 0)
    def _():
        def body(c, _):
            rows = pl.ds(pl.multiple_of(c * rows_per_step, rows_per_step), rows_per_step)
            fb = f_ref[0, rows, :].astype(BF16)
            for g in range(D_GROUPS):
                cs = _bdot(fb[:, g * D_GROUP_DIM:(g + 1) * D_GROUP_DIM], dmat_ref[...])
                z_ref[0, rows, g * D_GROUP_DIM:(g + 1) * D_GROUP_DIM] = cs[:, 0:D_GROUP_DIM].astype(BF16)
                z_ref[1, rows, g * D_GROUP_DIM:(g + 1) * D_GROUP_DIM] = cs[:, D_GROUP_DIM:].astype(BF16)
            return 0
        lax.fori_loop(0, length // rows_per_step, body, 0)

    out = _bdot(lmat_ref[0], z_ref[0]) + _bdot(lmat_ref[1], z_ref[1])
    o_ref[0] = (out * scale).astype(BF16)


def _dft_matrices(length):
    k = np.arange(length, dtype=np.int64)
    ang_l = 2.0 * np.pi * ((k[:, None] * k[None, :]) % length).astype(np.float64) / length
    lmat = np.stack([np.cos(ang_l), -np.sin(ang_l)], axis=0)
    d = np.arange(D_GROUP_DIM, dtype=np.int64)
    ang_d = 2.0 * np.pi * ((d[:, None] * d[None, :]) % D_GROUP_DIM).astype(np.float64) / D_GROUP_DIM
    dmat = np.concatenate([np.cos(ang_d), np.sin(ang_d)], axis=1)
    return jnp.asarray(lmat, dtype=F32).astype(BF16), jnp.asarray(dmat, dtype=F32).astype(BF16)


def _fourier(f):
    bsz, length, width = f.shape
    tm = min(ROW_TILE, length)
    lmat, dmat = _dft_matrices(length)
    scale = float(1.0 / np.sqrt(length * D_GROUP_DIM))
    return pl.pallas_call(
        functools.partial(_fourier_kernel, length=length, scale=scale),
        grid=(bsz, length // tm),
        in_specs=[
            pl.BlockSpec((1, length, width), lambda b, i: (b, 0, 0)),
            pl.BlockSpec((D_GROUP_DIM, 2 * D_GROUP_DIM), lambda b, i: (0, 0)),
            pl.BlockSpec((2, tm, length), lambda b, i: (0, i, 0)),
        ],
        out_specs=pl.BlockSpec((1, tm, width), lambda b, i: (b, i, 0)),
        out_shape=jax.ShapeDtypeStruct((bsz, length, width), BF16),
        scratch_shapes=[pltpu.VMEM((2, length, width), BF16)],
        compiler_params=_params(("parallel", "arbitrary")),
        name="fourier_mix",
    )(f, dmat, lmat)


def _gate_out_kernel(u_ref, v_ref, fo_ref, x_ref, lng_ref, lnb_ref, wcat_ref, bs_ref, wout_ref, g_ref, mod_ref,
                     o_ref, vs_ref, gt_ref, *, tm):
    head_dim = C_WIDTH // C_HEADS
    head_of_lane = lax.broadcasted_iota(jnp.int32, (CHUNK, C_WIDTH), 1) // head_dim
    for ci in range(tm // CHUNK):
        rows = slice(ci * CHUNK, (ci + 1) * CHUNK)
        v = v_ref[0, rows, :]
        mu = jnp.mean(v, axis=-1, keepdims=True)
        cen = v - mu
        var = jnp.mean(cen * cen, axis=-1, keepdims=True)
        vn = (cen * lax.rsqrt(var + LN_EPS) * lng_ref[...] + lnb_ref[...]).astype(BF16)
        for h in range(C_HEADS):
            vs_ref[h * CHUNK:(h + 1) * CHUNK, :] = jnp.where(head_of_lane == h, vn, jnp.zeros_like(vn))
        mixed = _bdot(wcat_ref[...], vs_ref[...]) + bs_ref[...]
        gt_ref[rows, :] = (u_ref[0, rows, :] * mixed).astype(BF16)
    y = (_bdot(gt_ref[...], wout_ref[0:C_WIDTH, :].astype(BF16))
         + _bdot(fo_ref[0], wout_ref[C_WIDTH:, :].astype(BF16)))
    o_ref[0] = x_ref[0] + mod_ref[0, 2:3, :] * _rms(y, g_ref[1:2, :])


def _gate_out(u, v, fo, x, ln_g, ln_b, w_s, b_s, w_out, g4, mods):
    bsz, length, d = x.shape
    tm = min(ROW_TILE, length)
    head_dim = C_WIDTH // C_HEADS
    wcat = w_s.transpose(1, 0, 2).reshape(CHUNK, C_HEADS * CHUNK).astype(BF16)
    bs_full = jnp.repeat(b_s.T, head_dim, axis=1)
    const = lambda b, i: (0, 0)
    return pl.pallas_call(
        functools.partial(_gate_out_kernel, tm=tm),
        grid=(bsz, length // tm),
        in_specs=[
            pl.BlockSpec((1, tm, C_WIDTH), lambda b, i: (b, i, 0)),
            pl.BlockSpec((1, tm, C_WIDTH), lambda b, i: (b, i, 0)),
            pl.BlockSpec((1, tm, D_WIDTH), lambda b, i: (b, i, 0)),
            pl.BlockSpec((1, tm, d), lambda b, i: (b, i, 0)),
            pl.BlockSpec((1, C_WIDTH), const),
            pl.BlockSpec((1, C_WIDTH), const),
            pl.BlockSpec((CHUNK, C_HEADS * CHUNK), const),
            pl.BlockSpec((CHUNK, C_WIDTH), const),
            pl.BlockSpec((C_WIDTH + D_WIDTH, d), const),
            pl.BlockSpec((4, d), const),
            pl.BlockSpec((1, 6, d), lambda b, i: (b, 0, 0)),
        ],
        out_specs=pl.BlockSpec((1, tm, d), lambda b, i: (b, i, 0)),
        out_shape=jax.ShapeDtypeStruct((bsz, length, d), F32),
        scratch_shapes=[
            pltpu.VMEM((C_HEADS * CHUNK, C_WIDTH), BF16),
            pltpu.VMEM((tm, C_WIDTH), BF16),
        ],
        compiler_params=_params(("parallel", "parallel")),
        name="gating_outproj",
    )(u, v, fo, x, ln_g.reshape(1, -1), ln_b.reshape(1, -1), wcat, bs_full, w_out, g4, mods)


def _router_kernel(x_ref, mod_ref, g_ref, wr_ref, tri_ref, xn_ref, meta_ref, cnt_ref, carry_ref, *, tm):
    first = jnp.logical_and(pl.program_id(0) == 0, pl.program_id(1) == 0)

    @pl.when(first)
    def _():
        carry_ref[...] = jnp.zeros_like(carry_ref)

    x = x_ref[0]
    h = _rms(x, g_ref[2:3, :]) * (1.0 + mod_ref[0, 4:5, :]) + mod_ref[0, 3:4, :]
    xn_ref[0] = h
    h_hi = h.astype(BF16)
    h_lo = (h - h_hi.astype(F32)).astype(BF16)
    logits = _bdot(jnp.concatenate([h_hi, h_hi, h_lo], axis=1), wr_ref[...])
    lane = lax.broadcasted_iota(jnp.int32, (tm, LANES), 1)
    neg = jnp.float32(-jnp.inf)
    lm = jnp.where(lane < N_EXPERTS, logits, neg)
    m1 = jnp.max(lm, axis=-1, keepdims=True)
    i1 = jnp.min(jnp.where(lm == m1, lane, LANES), axis=-1, keepdims=True)
    lm2 = jnp.where(lane == i1, neg, lm)
    m2 = jnp.max(lm2, axis=-1, keepdims=True)
    i2 = jnp.min(jnp.where(lm2 == m2, lane, LANES), axis=-1, keepdims=True)
    e = jnp.exp(m2 - m1)
    p1 = 1.0 / (1.0 + e)
    p2 = e / (1.0 + e)
    hot1 = (lane == i1).astype(F32)
    hot2 = (lane == i2).astype(F32)
    both = hot1 + hot2
    before = _bdot(tri_ref[...], both.astype(BF16)) + carry_ref[0:1, :]
    r1 = jnp.sum(before * hot1, axis=-1, keepdims=True)
    r2 = jnp.sum(before * hot2, axis=-1, keepdims=True)
    carry_ref[0:1, :] = carry_ref[0:1, :] + jnp.sum(both, axis=0, keepdims=True)
    cnt_ref[...] = carry_ref[...]
    vals = (i1.astype(F32), i2.astype(F32), p1, p2, r1, r2)
    meta = jnp.zeros((tm, LANES), F32)
    for k, val in enumerate(vals):
        meta = jnp.where(lane == k, val, meta)
    meta_ref[0] = meta


def _router(x, mods, g4, w_router):
    bsz, length, d = x.shape
    tm = min(ROW_TILE, length)
    wr = jnp.concatenate([w_router, jnp.zeros((d, LANES - N_EXPERTS), F32)], axis=1)
    wr_hi = wr.astype(BF16)
    wr_lo = (wr - wr_hi.astype(F32)).astype(BF16)
    wr = jnp.concatenate([wr_hi, wr_lo, wr_hi], axis=0)
    tri =jnp.asarray(np.tril(np.ones((tm, tm), np.float32), -1), dtype=BF16)
    const = lambda b, i: (0, 0)
    return pl.pallas_call(
        functools.partial(_router_kernel, tm=tm),
        grid=(bsz, length // tm),
        in_specs=[
            pl.BlockSpec((1, tm, d), lambda b, i: (b, i, 0)),
            pl.BlockSpec((1, 6, d), lambda b, i: (b, 0, 0)),
            pl.BlockSpec((4, d), const),
            pl.BlockSpec((3 * d, LANES), const),
            pl.BlockSpec((tm, tm), const),
        ],
        out_specs=[
            pl.BlockSpec((1, tm, d), lambda b, i: (b, i, 0)),
            pl.BlockSpec((1, tm, LANES), lambda b, i: (b, i, 0)),
            pl.BlockSpec((8, LANES), const),
        ],
        out_shape=[
            jax.ShapeDtypeStruct((bsz, length, d), F32),
            jax.ShapeDtypeStruct((bsz, length, LANES), F32),
            jax.ShapeDtypeStruct((8, LANES), F32),
        ],
        scratch_shapes=[pltpu.VMEM((8, LANES), F32)],
        compiler_params=_params(("arbitrary", "arbitrary")),
        name="moe_router",
    )(x, mods, g4, wr, tri)


def _expert_kernel(te_ref, nv_ref, x_ref, w1_ref, w3_ref, w2_ref, o_ref, acc_ref):
    i = pl.program_id(0)
    j = pl.program_id(1)

    @pl.when(i < nv_ref[0])
    def _():
        y = _swiglu_halves(x_ref[...].astype(BF16), w1_ref, w3_ref, w2_ref, lead=(0,))

        @pl.when(j == 0)
        def _():
            acc_ref[...] = y

        @pl.when(j > 0)
        def _():
            acc_ref[...] += y

        @pl.when(j == pl.num_programs(1) - 1)
        def _():
            o_ref[...] = acc_ref[...]


def _experts(x_sorted, tile_expert, n_valid, w1, w3, w2):
    rows = x_sorted.shape[0]
    d = w1.shape[1]
    dff = w1.shape[2]
    tm = EXPERT_ROW_TILE
    n_ff = 2
    tf = dff // n_ff

    def row_map(i, j, te, nv):
        return (jnp.maximum(jnp.minimum(i, nv[0] - 1), 0), 0)

    def w13_map(i, j, te, nv):
        return (te[jnp.maximum(jnp.minimum(i, nv[0] - 1), 0)], 0, jnp.where(i < nv[0], j, n_ff - 1))

    def w2_map(i, j, te, nv):
        return (te[jnp.maximum(jnp.minimum(i, nv[0] - 1), 0)], jnp.where(i < nv[0], j, n_ff - 1), 0)

    return pl.pallas_call(
        _expert_kernel,
        grid_spec=pltpu.PrefetchScalarGridSpec(
            num_scalar_prefetch=2,
            grid=(rows // tm, n_ff),
            in_specs=[
                pl.BlockSpec((tm, d), row_map),
                pl.BlockSpec((1, d, tf), w13_map),
                pl.BlockSpec((1, d, tf), w13_map),
                pl.BlockSpec((1, tf, d), w2_map),
            ],
            out_specs=pl.BlockSpec((tm, d), row_map),
            scratch_shapes=[pltpu.VMEM((tm, d), F32)],
        ),
        out_shape=jax.ShapeDtypeStruct((rows, d), F32),
        compiler_params=_params(("arbitrary", "arbitrary")),
        name="expert_swiglu",
    )(tile_expert, n_valid, x_sorted, w1, w3, w2)


def _combine_kernel(x_ref, y_ref, meta_ref, g_ref, mod_ref, *rest):
    o_ref = rest[-1]
    meta = meta_ref[0]
    y = meta[:, 2:3] * y_ref[0, 0] + meta[:, 3:4] * y_ref[1, 0]
    o_ref[0] = x_ref[0] + mod_ref[0, 5:6, :] * _rms(y, g_ref[3:4, :])


def _combine(x, y2_part, meta, g4, mods, first_batch, out_so_far):
    bsz, length, d = x.shape
    part = y2_part.shape[1]
    tm = min(ROW_TILE, length)
    in_specs = [
        pl.BlockSpec((1, tm, d), lambda b, i: (first_batch + b, i, 0)),
        pl.BlockSpec((2, 1, tm, d), lambda b, i: (0, b, i, 0)),
        pl.BlockSpec((1, tm, LANES), lambda b, i: (first_batch + b, i, 0)),
        pl.BlockSpec((4, d), lambda b, i: (0, 0)),
        pl.BlockSpec((1, 6, d), lambda b, i: (first_batch + b, 0, 0)),
    ]
    args = [x, y2_part, meta, g4, mods]
    aliases = {}
    if out_so_far is not None:
        in_specs.append(pl.BlockSpec(memory_space=pl.ANY))
        args.append(out_so_far)
        aliases = {5: 0}
    return pl.pallas_call(
        _combine_kernel,
        grid=(part, length // tm),
        in_specs=in_specs,
        out_specs=pl.BlockSpec((1, tm, d), lambda b, i: (first_batch + b, i, 0)),
        out_shape=jax.ShapeDtypeStruct((bsz, length, d), F32),
        input_output_aliases=aliases,
        compiler_params=_params(("parallel", "parallel")),
        name="moe_combine",
    )(*args)


SC_CORES = 2
SC_SUBCORES = 16
SC_WORKERS = SC_CORES * SC_SUBCORES
SC_ROW_CHUNK = 32
COMBINE_GROUPS = 4


def _sc_mesh():
    return plsc.VectorSubcoreMesh(core_axis_name="c", subcore_axis_name="s")


def _sc_dispatch(rows_in, pos2, n_out):
    tokens, width = rows_in.shape
    per_worker = tokens // SC_WORKERS
    chunk = SC_ROW_CHUNK
    n_pairs = per_worker // (2 * chunk)
    idx = pos2.reshape(2, SC_WORKERS, 2 * n_pairs, chunk)

    @functools.partial(
        pl.kernel, mesh=_sc_mesh(),
        out_type=jax.ShapeDtypeStruct((n_out, width), rows_in.dtype),
        scratch_types=[
            pltpu.VMEM((2 * n_pairs, chunk), jnp.int32),
            pltpu.VMEM((2 * n_pairs, chunk), jnp.int32),
            pltpu.VMEM((2, chunk, width), rows_in.dtype),
        ] + [pltpu.SemaphoreType.DMA] * 6,
        name="sc_dispatch",
    )
    def body(rows_hbm, idx_hbm, out_hbm, idx0_v, idx1_v, rows_v, rsem0, rsem1, wsem0, wsem1, wsem2, wsem3):
        wid = lax.axis_index("s") * SC_CORES + lax.axis_index("c")
        base = wid * per_worker
        rsem = (rsem0, rsem1)
        wsem = ((wsem0, wsem1), (wsem2, wsem3))
        pltpu.sync_copy(idx_hbm.at[0, wid], idx0_v)
        pltpu.sync_copy(idx_hbm.at[1, wid], idx1_v)

        @pl.loop(0, n_pairs)
        def _(p):
            c0 = 2 * p
            reads = [pltpu.async_copy(rows_hbm.at[pl.ds(base + (c0 + s) * chunk, chunk)], rows_v.at[s], rsem[s])
                     for s in range(2)]
            writes = []
            for s in range(2):
                reads[s].wait()
                writes.append(pltpu.async_copy(rows_v.at[s], out_hbm.at[idx0_v.at[c0 + s]], wsem[s][0]))
                writes.append(pltpu.async_copy(rows_v.at[s], out_hbm.at[idx1_v.at[c0 + s]], wsem[s][1]))
            for w in writes:
                w.wait()

    return body(rows_in, idx)


def _sc_gather(table, idx):
    n = idx.shape[0]
    width = table.shape[1]
    chunk = SC_ROW_CHUNK
    per_worker = n // SC_WORKERS
    n_pairs = per_worker // (2 * chunk)
    idx3 = idx.reshape(SC_WORKERS, 2 * n_pairs, chunk)

    @functools.partial(
        pl.kernel, mesh=_sc_mesh(),
        out_type=jax.ShapeDtypeStruct((n, width), table.dtype),
        scratch_types=[
            pltpu.VMEM((2 * n_pairs, chunk), jnp.int32),
            pltpu.VMEM((2, chunk, width), table.dtype),
            pltpu.SemaphoreType.DMA, pltpu.SemaphoreType.DMA,
            pltpu.SemaphoreType.DMA, pltpu.SemaphoreType.DMA,
        ],
        name="sc_gather",
    )
    def body(table_hbm, idx_hbm, out_hbm, idx_v, rows_v, gsem0, gsem1, wsem0, wsem1):
        wid = lax.axis_index("s") * SC_CORES + lax.axis_index("c")
        base = wid * per_worker
        gsem = (gsem0, gsem1)
        wsem = (wsem0, wsem1)
        pltpu.sync_copy(idx_hbm.at[wid], idx_v)

        @pl.loop(0, n_pairs)
        def _(p):
            c0 = 2 * p
            gathers = [pltpu.async_copy(table_hbm.at[idx_v.at[c0 + s]], rows_v.at[s], gsem[s]) for s in range(2)]
            writes = []
            for s in range(2):
                gathers[s].wait()
                rows_out = out_hbm.at[pl.ds(base + (c0 + s) * chunk, chunk)]
                writes.append(pltpu.async_copy(rows_v.at[s], rows_out, wsem[s]))
            for s in range(2):
                writes[s].wait()

    return body(table, idx3)


def _moe(x, mods, g4, w_router, w1, w3, w2):
    bsz, length, d = x.shape
    tokens = bsz * length
    tm = EXPERT_ROW_TILE
    xn, meta, counts = _router(x, mods, g4, w_router)
    meta2 = meta.reshape(tokens, LANES)
    experts = meta2[:, 0:2].astype(jnp.int32)
    ranks = meta2[:, 4:6].astype(jnp.int32)
    cnt = counts[0, 0:N_EXPERTS].astype(jnp.int32)
    padded = ((cnt + tm - 1) // tm) * tm
    ends = jnp.cumsum(padded)
    starts = ends - padded
    pos = (starts[experts] + ranks).T.reshape(2 * tokens)
    rows = 2 * tokens + N_EXPERTS * tm
    n_tiles = rows // tm
    n_valid = (ends[-1] // tm).astype(jnp.int32).reshape(1)
    tile_start = jnp.arange(n_tiles, dtype=jnp.int32) * tm
    tile_expert = jnp.minimum(
        jnp.sum((tile_start[:, None] >= ends[None, :]).astype(jnp.int32), axis=1), N_EXPERTS - 1)
    x_sorted = _sc_dispatch(xn.reshape(tokens, d), pos.reshape(2, tokens), rows)
    y_sorted = _experts(x_sorted, tile_expert, n_valid, w1, w3, w2)
    groups = COMBINE_GROUPS if bsz % COMBINE_GROUPS == 0 else 1
    per = bsz // groups
    pos3 = pos.reshape(2, bsz, length)
    out = None
    for q in range(groups):
        idx = pos3[:, q * per:(q + 1) * per].reshape(2 * per * length)
        y2 = _sc_gather(y_sorted, idx).reshape(2, per, length, d)
        out = _combine(x, y2, meta, g4, mods, q * per, out)
    return out


def _grid_pos_embed(rows, d):
    row = np.repeat(np.arange(rows, dtype=np.float64), GRID_W)
    col = np.tile(np.arange(GRID_W, dtype=np.float64), rows)
    n_freq = d // 4
    omega = POS_THETA ** (-np.arange(n_freq, dtype=np.float64) / n_freq)
    ang_r = row[:, None] * omega
    ang_c = col[:, None] * omega
    return jnp.asarray(np.concatenate([np.sin(ang_r), np.cos(ang_r), np.sin(ang_c), np.cos(ang_c)], axis=-1),
                       dtype=F32)


def kernel(x, c, ctx, c_ctx, ada_w, ada_b, norm_g, ab_w_in, rg_conv_w, rg_conv_b, rg_w_r, rg_b_r, rg_w_i, rg_b_i, rg_lambda, cv_dw_w, cv_dw_b, cv_ln_g, cv_ln_b, ab_w_out, ffn_w1, ffn_w3, ffn_w2, cd_w_in, sg_ln_g, sg_ln_b, sg_w_s, sg_b_s, cd_w_out, moe_router, moe_w1, moe_w3, moe_w2):
    bsz, length, d = x.shape
    assert ada_w.shape[0] == 2, "two layers: one even (RG-LRU | Conformer), one odd (gMLP | Fourier)"
    pos = _grid_pos_embed(length // GRID_W, d)

    cpad = jnp.concatenate([c, c_ctx[None, :], jnp.zeros((16 - bsz - 1, d), F32)], axis=0)
    m = _ada(cpad, ada_w, ada_b)
    mods = m[:, :bsz].reshape(2, bsz, 6, d)
    mods_ctx = m[:, bsz:bsz + 1].reshape(2, 1, 6, d)

    mods_t = mods[0].transpose(1, 0, 2)
    mods_ctx_t = jnp.broadcast_to(mods_ctx[0].reshape(6, 1, d), (6, bsz, d))
    pos3 = pos[:, None, :]
    w_in = ab_w_in[0]
    xa3, gg3, glu3, ffn1b, ffn3b, ffn2b = _inproj_t(x, pos3, mods_t, norm_g[0], w_in, rg_conv_w[0], rg_conv_b[0], True,
                                                    riders=(ffn_w1[0], ffn_w3[0], ffn_w2[0]))
    (xc3,) = _inproj_t(ctx, None, mods_ctx_t, norm_g[0], w_in[:, 0:A_WIDTH], rg_conv_w[0], rg_conv_b[0], False)
    wg, bg, sp = _gate_weights(rg_w_r[0], rg_b_r[0], rg_w_i[0], rg_b_i[0], rg_lambda[0])
    n_exp, _, dff_e = moe_w1[0].shape
    rec_hi, rec_lo, w2b = _scan_t(xa3, xc3, gg3, wg, bg, sp, riders=(moe_w2[0].reshape(n_exp * dff_e, d),))
    u3, w1b = _conv_t(glu3, cv_dw_w[0], cv_dw_b[0], cv_ln_g[0], cv_ln_b[0],
                      riders=(moe_w1[0].reshape(n_exp * d, dff_e),))
    x1_t = _outproj_t(rec_lo, rec_hi, u3, x, pos3, ab_w_out[0], norm_g[0], mods_t)
    x2, w3b = _ffn_t(x1_t, mods_t, norm_g[0], ffn1b, ffn3b, ffn2b, riders=(moe_w3[0].reshape(n_exp * d, dff_e),))

    u, v, f = _inproj_cd(x2, mods[1], norm_g[1], cd_w_in[0])
    fo = _fourier(f)
    x3 = _gate_out(u, v, fo, x2, sg_ln_g[0], sg_ln_b[0], sg_w_s[0], sg_b_s[0], cd_w_out[0],
                   norm_g[1], mods[1])
    return _moe(x3, mods[1], norm_g[1], moe_router[0], w1b.reshape(n_exp, d, dff_e), w3b.reshape(n_exp, d, dff_e),
                w2b.reshape(n_exp, dff_e, d))
```

```python
import functools

import numpy as np
import jax
import jax.numpy as jnp
from jax import lax
from jax.experimental import pallas as pl
from jax.experimental.pallas import tpu as pltpu
from jax.experimental.pallas import tpu_sc as plsc

F32 = jnp.float32
BF16 = jnp.bfloat16

GRID_W = 64
POS_THETA = 10000.0
NORM_EPS = 1e-6
LN_EPS = 1e-5
LRU_C = 8.0

A_WIDTH = 512
A_HEADS = 8
A_CONV_W = 4
B_WIDTH = 512
B_CONV_W = 31
C_WIDTH = 512
C_HEADS = 8
CHUNK = 128
D_WIDTH = 512
D_GROUPS = 4
D_GROUP_DIM = D_WIDTH // D_GROUPS
N_EXPERTS = 8

LANES = 128
ROW_TILE = 512
EXPERT_ROW_TILE = 512
VMEM_LIMIT = 56 * 2 ** 20


def _params(sem):
    return pltpu.CompilerParams(dimension_semantics=sem, vmem_limit_bytes=VMEM_LIMIT)


def _rms(x, g):
    return x * lax.rsqrt(jnp.mean(x * x, axis=-1, keepdims=True) + NORM_EPS) * g


def _bdot(a, b):
    return jnp.dot(a, b, preferred_element_type=F32)


MXU_WIDTH = 256


def _swiglu_halves(x, w1_ref, w3_ref, w2_ref, lead=()):
    n = w1_ref.shape[-1]
    half = -(-(n // 2) // MXU_WIDTH) * MXU_WIDTH
    y = None
    for lo, hi in ((0, half), (half, n)):
        a = _bdot(x, w1_ref[lead + (slice(None), slice(lo, hi))])
        b = _bdot(x, w3_ref[lead + (slice(None), slice(lo, hi))])
        t = (jax.nn.silu(a) * b).astype(BF16)
        part = _bdot(t, w2_ref[lead + (slice(lo, hi), slice(None))])
        y = part if y is None else y + part
    return y


def _ada_kernel(c_ref, w_ref, b_ref, o_ref):
    s = jax.nn.silu(c_ref[...])
    o_ref[0] = _bdot(s.astype(BF16), w_ref[0].astype(BF16)) + b_ref[0]


def _ada(cpad, ada_w, ada_b):
    nl, d, n6 = ada_w.shape
    rows = cpad.shape[0]
    tn = n6 // 4
    return pl.pallas_call(
        _ada_kernel,
        grid=(nl, n6 // tn),
        in_specs=[
            pl.BlockSpec((rows, d), lambda l, j: (0, 0)),
            pl.BlockSpec((1, d, tn), lambda l, j: (l, 0, j)),
            pl.BlockSpec((1, 1, tn), lambda l, j: (l, 0, j)),
        ],
        out_specs=pl.BlockSpec((1, rows, tn), lambda l, j: (l, 0, j)),
        out_shape=jax.ShapeDtypeStruct((nl, rows, n6), F32),
        compiler_params=_params(("arbitrary", "arbitrary")),
        name="ada_terms",
    )(cpad, ada_w, ada_b.reshape(nl, 1, n6))


TIME_TILE = 128
SCAN_TIME_TILE = 128
SCAN_LANES = 256
CONV_TIME_TILE = 64
CONV_HALO = 16


def _time_major(ref):
    return jnp.swapaxes(ref[...], 0, 1)


def _inproj_t_kernel(*refs, tt, add_pos, branches, n_cast):
    if n_cast:
        n_in = len(refs) - n_cast - (3 if branches else 1) - n_cast
        _cast_riders(refs[n_in:n_in + n_cast], refs[len(refs) - n_cast:])
        refs = refs[:n_in] + refs[n_in + n_cast:len(refs) - n_cast]
    if add_pos:
        x_ref, xp_ref, xn_ref, pos_ref, posp_ref, posn_ref, mod_ref, g_ref, w_ref, cw_ref, cb_ref, *outs = refs
        xall = jnp.concatenate(
            [_time_major(xp_ref)[7:8] + posp_ref[...], _time_major(x_ref) + pos_ref[...],
             _time_major(xn_ref)[0:2] + posn_ref[...]], axis=0)
    else:
        x_ref, xp_ref, xn_ref, mod_ref, g_ref, w_ref, cw_ref, cb_ref, *outs = refs
        xall = jnp.concatenate([_time_major(xp_ref)[7:8], _time_major(x_ref), _time_major(xn_ref)[0:2]], axis=0)
    i = pl.program_id(0)
    last = pl.num_programs(0) - 1
    steps, bsz, d = xall.shape
    h = _rms(xall, g_ref[0:1, :]) * (1.0 + mod_ref[1]) + mod_ref[0]
    z = _bdot(h.reshape(steps * bsz, d).astype(BF16), w_ref[...].astype(BF16))
    z3 = z.reshape(steps, bsz, z.shape[-1])
    t_idx = lax.broadcasted_iota(jnp.int32, (steps, bsz, A_WIDTH), 0)
    inside = jnp.logical_and(jnp.logical_or(t_idx >= 1, i > 0), jnp.logical_or(t_idx <= tt, i < last))
    xa = jnp.where(inside, z3[:, :, 0:A_WIDTH], 0.0)
    xconv = cb_ref[...] + cw_ref[0:1, :] * xa[0:tt]
    for k in range(1, A_CONV_W):
        xconv = xconv + cw_ref[k:k + 1, :] * xa[k:k + tt]
    outs[0][...] = xconv
    if branches:
        outs[1][...] = jax.nn.gelu(z3[1:tt + 1, :, A_WIDTH:2 * A_WIDTH])
        vb = z3[1:tt + 1, :, 2 * A_WIDTH:2 * A_WIDTH + B_WIDTH]
        gb = z3[1:tt + 1, :, 2 * A_WIDTH + B_WIDTH:]
        outs[2][...] = vb * jax.nn.sigmoid(gb)


def _inproj_t(x, pos3, mods_t, g4, w, conv_w, conv_b, branches, riders=()):
    bsz, length, d = x.shape
    n = w.shape[1]
    tt = min(TIME_TILE, length)
    add_pos = pos3 is not None
    n_out = 3 if branches else 1
    prev_map = lambda i: (jnp.maximum(i * tt - 1, 0), 0, 0)
    next_map = lambda i: (jnp.minimum((i + 1) * (tt // 2), length // 2 - 1), 0, 0)
    in_specs = [
        pl.BlockSpec((bsz, tt, d), lambda i: (0, i, 0)),
        pl.BlockSpec((bsz, 8, d), lambda i: (0, jnp.maximum(i * (tt // 8) - 1, 0), 0)),
        pl.BlockSpec((bsz, 8, d), lambda i: (0, jnp.minimum((i + 1) * (tt // 8), length // 8 - 1), 0)),
    ]
    args = [x, x, x]
    if add_pos:
        in_specs += [
            pl.BlockSpec((tt, 1, d), lambda i: (i, 0, 0)),
            pl.BlockSpec((1, 1, d), prev_map),
            pl.BlockSpec((2, 1, d), next_map),
        ]
        args += [pos3, pos3, pos3]
    in_specs += [
        pl.BlockSpec((6, bsz, d), lambda i: (0, 0, 0)),
        pl.BlockSpec((4, d), lambda i: (0, 0)),
        pl.BlockSpec((d, n), lambda i: (0, 0)),
        pl.BlockSpec((A_CONV_W, A_WIDTH), lambda i: (0, 0)),
        pl.BlockSpec((1, A_WIDTH), lambda i: (0, 0)),
    ]
    args += [mods_t, g4, w, conv_w, conv_b.reshape(1, A_WIDTH)]
    r_in, r_out, r_shapes = _rider_specs(riders, length // tt, lambda i: i)
    return pl.pallas_call(
        functools.partial(_inproj_t_kernel, tt=tt, add_pos=add_pos, branches=branches, n_cast=len(riders)),
        grid=(length // tt,),
        in_specs=in_specs + r_in,
        out_specs=[pl.BlockSpec((tt, bsz, A_WIDTH), lambda i: (i, 0, 0))] * n_out + r_out,
        out_shape=[jax.ShapeDtypeStruct((length, bsz, A_WIDTH), F32)] * n_out + r_shapes,
        compiler_params=_params(("parallel",)),
        name="inproj_ab" if branches else "inproj_ctx",
    )(*args, *riders)


def _cast_riders(cast_in, cast_out):
    for src, dst in zip(cast_in, cast_out):
        dst[...] = src[...].astype(BF16)


def _rider_specs(arrays, n_steps, step_of):
    in_specs, out_specs, out_shapes = [], [], []
    for arr in arrays:
        rows, cols = arr.shape
        block = (rows // n_steps, cols)
        in_specs.append(pl.BlockSpec(block, lambda *idx: (step_of(*idx), 0)))
        out_specs.append(pl.BlockSpec(block, lambda *idx: (step_of(*idx), 0)))
        out_shapes.append(jax.ShapeDtypeStruct(arr.shape, BF16))
    return in_specs, out_specs, out_shapes


def _scan_t_kernel(*refs, tb, nb, ctx_len, n_cast):
    xf_ref, xb_ref, ggf_ref, ggb_ref, xc_ref, wg_ref, bg_ref, sp_ref = refs[:8]
    cast_in = refs[8:8 + n_cast]
    hi_ref, lo_ref = refs[8 + n_cast:10 + n_cast]
    cast_out = refs[10 + n_cast:10 + 2 * n_cast]
    a_ref, b_ref, h_ref, state_ref = refs[10 + 2 * n_cast:]
    _cast_riders(cast_in, cast_out)
    k = pl.program_id(1)
    half = nb // 2
    bsz, lanes = xf_ref.shape[1], xf_ref.shape[2]
    chunk = 128

    def coefficients(x_ref, n, direction):
        def body(c, _):
            t0 = pl.multiple_of(c * chunk, chunk)
            x = x_ref[pl.ds(t0, chunk)].reshape(chunk * bsz, lanes)
            th = jnp.tanh(_bdot(x.astype(BF16), wg_ref[0, direction]) + bg_ref[0, direction])
            i = 0.5 * th[:, lanes:] + 0.5
            c = (-0.5 * LRU_C) * sp_ref[0, direction:direction + 1, :]
            log_a = c * th[:, 0:lanes] + c
            a = jnp.exp(log_a)
            one_minus_a2 = -jnp.tanh(log_a) * (a * a + 1.0)
            a_ref[direction, pl.ds(t0, chunk)] = a.reshape(chunk, bsz, lanes)
            b_ref[direction, pl.ds(t0, chunk)] = (jnp.sqrt(one_minus_a2) * (i * x)).reshape(chunk, bsz, lanes)
            return 0
        lax.fori_loop(0, n // chunk, body, 0)

    def sweep(n, store):
        def body(t, carry):
            hf, hb = carry
            tr = n - 1 - t
            hf = a_ref[0, pl.ds(t, 1)][0] * hf + b_ref[0, pl.ds(t, 1)][0]
            hb = a_ref[1, pl.ds(tr, 1)][0] * hb + b_ref[1, pl.ds(tr, 1)][0]
            if store:
                b_ref[0, pl.ds(t, 1)] = hf[None]
                b_ref[1, pl.ds(tr, 1)] = hb[None]
            return hf, hb
        hf, hb = lax.fori_loop(0, n, body, (state_ref[0], state_ref[1]), unroll=8)
        state_ref[0] = hf
        state_ref[1] = hb

    @pl.when(k == 0)
    def _():
        state_ref[...] = jnp.zeros_like(state_ref)
        coefficients(xc_ref, ctx_len, 0)
        coefficients(xc_ref, ctx_len, 1)
        sweep(ctx_len, False)

    coefficients(xf_ref, tb, 0)
    coefficients(xb_ref, tb, 1)
    sweep(tb, True)
    m = nb - 1 - k

    @pl.when(k < half)
    def _():
        h_ref[k] = b_ref[0, 0:tb]
        h_ref[m] = b_ref[1, 0:tb]

    @pl.when(k >= half)
    def _():
        hi_ref[...] = (b_ref[0, 0:tb] + h_ref[k]) * ggf_ref[...]
        lo_ref[...] = (h_ref[m] + b_ref[1, 0:tb]) * ggb_ref[...]


def _scan_t(xa3, xc3, gg3, wg, bg, sp, riders=()):
    length, bsz, width = xa3.shape
    ctx_len = xc3.shape[0]
    tb = min(SCAN_TIME_TILE, length // 2)
    nb = length // tb
    half = nb // 2
    lanes = SCAN_LANES
    blk = (tb, bsz, lanes)
    r_in, r_out, r_shapes = _rider_specs(riders, (width // lanes) * nb, lambda g, k: g * nb + k)
    return pl.pallas_call(
        functools.partial(_scan_t_kernel, tb=tb, nb=nb, ctx_len=ctx_len, n_cast=len(riders)),
        grid=(width // lanes, nb),
        in_specs=[
            pl.BlockSpec(blk, lambda g, k: (k, 0, g)),
            pl.BlockSpec(blk, lambda g, k: (nb - 1 - k, 0, g)),
            pl.BlockSpec(blk, lambda g, k: (jnp.maximum(k, half), 0, g)),
            pl.BlockSpec(blk, lambda g, k: (jnp.minimum(nb - 1 - k, half - 1), 0, g)),
            pl.BlockSpec((ctx_len, bsz, lanes), lambda g, k: (0, 0, g)),
            pl.BlockSpec((1, 2, lanes, 2 * lanes), lambda g, k: (g, 0, 0, 0)),
            pl.BlockSpec((1, 2, 1, 2 * lanes), lambda g, k: (g, 0, 0, 0)),
            pl.BlockSpec((1, 2, lanes), lambda g, k: (g, 0, 0)),
        ] + r_in,
        out_specs=[
            pl.BlockSpec(blk, lambda g, k: (jnp.maximum(k - half, 0), 0, g)),
            pl.BlockSpec(blk, lambda g, k: (jnp.minimum(nb - 1 - k, half - 1), 0, g)),
        ] + r_out,
        out_shape=[jax.ShapeDtypeStruct((length // 2, bsz, width), F32)] * 2 + r_shapes,
        scratch_shapes=[
            pltpu.VMEM((2, max(tb, ctx_len), bsz, lanes), F32),
            pltpu.VMEM((2, max(tb, ctx_len), bsz, lanes), F32),
            pltpu.VMEM((nb, tb, bsz, lanes), F32),
            pltpu.VMEM((2, bsz, lanes), F32),
        ],
        compiler_params=_params(("parallel", "arbitrary")),
        name="rglru_scan",
    )(xa3, xa3, gg3, gg3, xc3, wg, bg, sp, *riders)


def _gate_weights(w_r, b_r, w_i, b_i, lam):
    hd = A_WIDTH // A_HEADS
    heads_per_group = SCAN_LANES // hd
    ngroups = A_WIDTH // SCAN_LANES

    def blockdiag(w):
        w = w.reshape(ngroups, heads_per_group, hd, hd)
        eye = jnp.eye(heads_per_group, dtype=w.dtype)
        return jnp.einsum("ghde,hk->ghdke", w, eye).reshape(ngroups, SCAN_LANES, SCAN_LANES)

    wg = jnp.stack([jnp.concatenate([blockdiag(w_r[d]), blockdiag(w_i[d])], axis=-1) for d in range(2)], axis=1)
    bg = jnp.stack([jnp.concatenate([b_r[d].reshape(ngroups, 1, SCAN_LANES), b_i[d].reshape(ngroups, 1, SCAN_LANES)],
                                    axis=-1) for d in range(2)], axis=1)
    sp = jax.nn.softplus(-lam.astype(F32)).reshape(2, ngroups, SCAN_LANES).transpose(1, 0, 2)
    return (0.5 * wg).astype(BF16), 0.5 * bg, sp


def _conv_t_kernel(*refs, tb, n_cast):
    x_ref, xp_ref, xn_ref, w_ref, b_ref, lng_ref, lnb_ref = refs[:7]
    cast_in = refs[7:7 + n_cast]
    o_ref = refs[7 + n_cast]
    cast_out = refs[8 + n_cast:8 + 2 * n_cast]
    stage_ref = refs[-1]
    _cast_riders(cast_in, cast_out)
    i = pl.program_id(0)
    last = pl.num_programs(0) - 1
    halo = CONV_HALO
    stage_ref[0:halo] = jnp.where(i > 0, xp_ref[...], 0.0)
    stage_ref[halo:halo + tb] = x_ref[...]
    stage_ref[halo + tb:2 * halo + tb] = jnp.where(i < last, xn_ref[...], 0.0)
    sub = 8

    def body(r, _):
        t0 = pl.multiple_of(r * sub, sub)
        def tap(k, acc):
            return acc + w_ref[k] * stage_ref[pl.ds(t0 + 1 + k, sub)]
        acc = lax.fori_loop(0, B_CONV_W + 1, tap, jnp.broadcast_to(b_ref[...], (sub,) + b_ref.shape), unroll=8)
        o_ref[pl.ds(t0, sub)] = acc
        return 0
    lax.fori_loop(0, tb // sub, body, 0)

    norm_rows = 64

    def norm_body(r, _):
        t0 = pl.multiple_of(r * norm_rows, norm_rows)
        acc = o_ref[pl.ds(t0, norm_rows)]
        mu = jnp.mean(acc, axis=-1, keepdims=True)
        cen = acc - mu
        var = jnp.mean(cen * cen, axis=-1, keepdims=True)
        y = cen * lax.rsqrt(var + LN_EPS) * lng_ref[...] + lnb_ref[...]
        o_ref[pl.ds(t0, norm_rows)] = jax.nn.silu(y)
        return 0
    lax.fori_loop(0, tb // norm_rows, norm_body, 0)


def _conv_t(glu3, dw_w, dw_b, ln_g, ln_b, riders=()):
    length, bsz, width = glu3.shape
    tb = min(CONV_TIME_TILE, length)
    halo = CONV_HALO
    taps = dw_w.shape[0]
    w8 = jnp.broadcast_to(jnp.concatenate([dw_w, jnp.zeros((32 - taps, width), F32)], axis=0)[:, None, :],
                          (32, bsz, width))
    b8 = jnp.broadcast_to(dw_b[None, :], (bsz, width))
    r_in, r_out, r_shapes = _rider_specs(riders, length // tb, lambda i: i)
    return pl.pallas_call(
        functools.partial(_conv_t_kernel, tb=tb, n_cast=len(riders)),
        grid=(length // tb,),
        in_specs=[
            pl.BlockSpec((tb, bsz, width), lambda i: (i, 0, 0)),
            pl.BlockSpec((halo, bsz, width), lambda i: (jnp.maximum(i * (tb // halo) - 1, 0), 0, 0)),
            pl.BlockSpec((halo, bsz, width), lambda i: (jnp.minimum((i + 1) * (tb // halo), length // halo - 1), 0, 0)),
            pl.BlockSpec((32, bsz, width), lambda i: (0, 0, 0)),
            pl.BlockSpec((bsz, width), lambda i: (0, 0)),
            pl.BlockSpec((1, width), lambda i: (0, 0)),
            pl.BlockSpec((1, width), lambda i: (0, 0)),
        ] + r_in,
        out_specs=[pl.BlockSpec((tb, bsz, width), lambda i: (i, 0, 0))] + r_out,
        out_shape=[jax.ShapeDtypeStruct((length, bsz, width), F32)] + r_shapes,
        scratch_shapes=[pltpu.VMEM((tb + 2 * halo, bsz, width), F32)],
        compiler_params=_params(("parallel",)),
        name="conformer_conv",
    )(glu3, glu3, glu3, w8, b8, ln_g.reshape(1, -1), ln_b.reshape(1, -1), *riders)


def _outproj_t_kernel(lo_ref, hi_ref, u_ref, x_ref, pos_ref, wout_ref, g_ref, mod_ref, o_ref, *, half_steps):
    i = pl.program_id(0)
    bsz, tt, d = x_ref.shape
    rec = jnp.where(i < half_steps, lo_ref[...], hi_ref[...])
    rows = tt * bsz
    y = (_bdot(rec.reshape(rows, A_WIDTH).astype(BF16), wout_ref[0:A_WIDTH, :].astype(BF16))
         + _bdot(u_ref[...].reshape(rows, B_WIDTH).astype(BF16), wout_ref[A_WIDTH:, :].astype(BF16)))
    yn = _rms(y, g_ref[1:2, :]).reshape(tt, bsz, d)
    o_ref[...] = _time_major(x_ref) + pos_ref[...] + mod_ref[2] * yn


def _outproj_t(rec_lo, rec_hi, u3, x, pos3, w_out, g4, mods_t):
    bsz, length, d = x.shape
    tt = min(TIME_TILE // 2, length // 2)
    half_steps = (length // 2) // tt
    return pl.pallas_call(
        functools.partial(_outproj_t_kernel, half_steps=half_steps),
        grid=(length // tt,),
        in_specs=[
            pl.BlockSpec((tt, bsz, A_WIDTH), lambda i: (jnp.minimum(i, half_steps - 1), 0, 0)),
            pl.BlockSpec((tt, bsz, A_WIDTH), lambda i: (jnp.maximum(i - half_steps, 0), 0, 0)),
            pl.BlockSpec((tt, bsz, B_WIDTH), lambda i: (i, 0, 0)),
            pl.BlockSpec((bsz, tt, d), lambda i: (0, i, 0)),
            pl.BlockSpec((tt, 1, d), lambda i: (i, 0, 0)),
            pl.BlockSpec((A_WIDTH + B_WIDTH, d), lambda i: (0, 0)),
            pl.BlockSpec((4, d), lambda i: (0, 0)),
            pl.BlockSpec((6, bsz, d), lambda i: (0, 0, 0)),
        ],
        out_specs=pl.BlockSpec((tt, bsz, d), lambda i: (i, 0, 0)),
        out_shape=jax.ShapeDtypeStruct((length, bsz, d), F32),
        compiler_params=_params(("parallel",)),
        name="mixer_ab_outproj",
    )(rec_lo, rec_hi, u3, x, pos3, w_out, g4, mods_t)


def _ffn_t_kernel(x_ref, mod_ref, g_ref, w1_ref, w3_ref, w2_ref, *rest, n_cast):
    _cast_riders(rest[:n_cast], rest[n_cast + 1:])
    o_ref = rest[n_cast]
    x = x_ref[...]
    tt, bsz, d = x.shape
    h = (_rms(x, g_ref[2:3, :]) * (1.0 + mod_ref[4]) + mod_ref[3]).reshape(tt * bsz, d).astype(BF16)
    y = _swiglu_halves(h, w1_ref, w3_ref, w2_ref)
    o_ref[...] = jnp.swapaxes(x + mod_ref[5] * _rms(y, g_ref[3:4, :]).reshape(tt, bsz, d), 0, 1)


def _ffn_t(x3, mods_t, g4, w1, w3, w2, riders=()):
    length, bsz, d = x3.shape
    dff = w1.shape[1]
    tt = min(TIME_TILE // 2, length)
    r_in, r_out, r_shapes = _rider_specs(riders, length // tt, lambda i: i)
    return pl.pallas_call(
        functools.partial(_ffn_t_kernel, n_cast=len(riders)),
        grid=(length // tt,),
        in_specs=[
            pl.BlockSpec((tt, bsz, d), lambda i: (i, 0, 0)),
            pl.BlockSpec((6, bsz, d), lambda i: (0, 0, 0)),
            pl.BlockSpec((4, d), lambda i: (0, 0)),
            pl.BlockSpec((d, dff), lambda i: (0, 0), pipeline_mode=pl.Buffered(1)),
            pl.BlockSpec((d, dff), lambda i: (0, 0), pipeline_mode=pl.Buffered(1)),
            pl.BlockSpec((dff, d), lambda i: (0, 0), pipeline_mode=pl.Buffered(1)),
        ] + r_in,
        out_specs=[pl.BlockSpec((bsz, tt, d), lambda i: (0, i, 0))] + r_out,
        out_shape=[jax.ShapeDtypeStruct((bsz, length, d), F32)] + r_shapes,
        compiler_params=_params(("parallel",)),
        name="dense_swiglu",
    )(x3, mods_t, g4, w1, w3, w2, *riders)


def _inproj_cd_kernel(x_ref, mod_ref, g_ref, w_ref, u_ref, v_ref, f_ref):
    x = x_ref[0]
    h = _rms(x, g_ref[0:1, :]) * (1.0 + mod_ref[0, 1:2, :]) + mod_ref[0, 0:1, :]
    z = _bdot(h.astype(BF16), w_ref[...].astype(BF16))
    u_ref[0] = jax.nn.gelu(z[:, 0:C_WIDTH])
    v_ref[0] = jax.nn.gelu(z[:, C_WIDTH:2 * C_WIDTH])
    f_ref[0] = z[:, 2 * C_WIDTH:]


def _inproj_cd(x, mods, g4, w):
    bsz, length, d = x.shape
    n = w.shape[1]
    tm = min(ROW_TILE, length)
    return pl.pallas_call(
        _inproj_cd_kernel,
        grid=(bsz, length // tm),
        in_specs=[
            pl.BlockSpec((1, tm, d), lambda b, i: (b, i, 0)),
            pl.BlockSpec((1, 6, d), lambda b, i: (b, 0, 0)),
            pl.BlockSpec((4, d), lambda b, i: (0, 0)),
            pl.BlockSpec((d, n), lambda b, i: (0, 0)),
        ],
        out_specs=[pl.BlockSpec((1, tm, C_WIDTH), lambda b, i: (b, i, 0))] * 3,
        out_shape=[jax.ShapeDtypeStruct((bsz, length, C_WIDTH), F32)] * 3,
        compiler_params=_params(("parallel", "parallel")),
        name="inproj_cd",
    )(x, mods, g4, w)


def _fourier_kernel(f_ref, dmat_ref, lmat_ref, o_ref, z_ref, *, length, scale):
    j = pl.program_id(1)
    rows_per_step = 512 if length % 512 == 0 else length

    @pl.when(j == 0)
    def _():
        def body(c, _):
            rows = pl.ds(pl.multiple_of(c * rows_per_step, rows_per_step), rows_per_step)
            fb = f_ref[0, rows, :].astype(BF16)
            for g in range(D_GROUPS):
                cs = _bdot(fb[:, g * D_GROUP_DIM:(g + 1) * D_GROUP_DIM], dmat_ref[...])
                z_ref[0, rows, g * D_GROUP_DIM:(g + 1) * D_GROUP_DIM] = cs[:, 0:D_GROUP_DIM].astype(BF16)
                z_ref[1, rows, g * D_GROUP_DIM:(g + 1) * D_GROUP_DIM] = cs[:, D_GROUP_DIM:].astype(BF16)
            return 0
        lax.fori_loop(0, length // rows_per_step, body, 0)

    out = _bdot(lmat_ref[0], z_ref[0]) + _bdot(lmat_ref[1], z_ref[1])
    o_ref[0] = (out * scale).astype(BF16)


def _dft_matrices(length):
    k = np.arange(length, dtype=np.int64)
    ang_l = 2.0 * np.pi * ((k[:, None] * k[None, :]) % length).astype(np.float64) / length
    lmat = np.stack([np.cos(ang_l), -np.sin(ang_l)], axis=0)
    d = np.arange(D_GROUP_DIM, dtype=np.int64)
    ang_d = 2.0 * np.pi * ((d[:, None] * d[None, :]) % D_GROUP_DIM).astype(np.float64) / D_GROUP_DIM
    dmat = np.concatenate([np.cos(ang_d), np.sin(ang_d)], axis=1)
    return jnp.asarray(lmat, dtype=F32).astype(BF16), jnp.asarray(dmat, dtype=F32).astype(BF16)


def _fourier(f):
    bsz, length, width = f.shape
    tm = min(ROW_TILE, length)
    lmat, dmat = _dft_matrices(length)
    scale = float(1.0 / np.sqrt(length * D_GROUP_DIM))
    return pl.pallas_call(
        functools.partial(_fourier_kernel, length=length, scale=scale),
        grid=(bsz, length // tm),
        in_specs=[
            pl.BlockSpec((1, length, width), lambda b, i: (b, 0, 0)),
            pl.BlockSpec((D_GROUP_DIM, 2 * D_GROUP_DIM), lambda b, i: (0, 0)),
            pl.BlockSpec((2, tm, length), lambda b, i: (0, i, 0)),
        ],
        out_specs=pl.BlockSpec((1, tm, width), lambda b, i: (b, i, 0)),
        out_shape=jax.ShapeDtypeStruct((bsz, length, width), BF16),
        scratch_shapes=[pltpu.VMEM((2, length, width), BF16)],
        compiler_params=_params(("parallel", "arbitrary")),
        name="fourier_mix",
    )(f, dmat, lmat)


def _gate_out_kernel(u_ref, v_ref, fo_ref, x_ref, lng_ref, lnb_ref, wcat_ref, bs_ref, wout_ref, g_ref, mod_ref,
                     o_ref, vs_ref, gt_ref, *, tm):
    head_dim = C_WIDTH // C_HEADS
    head_of_lane = lax.broadcasted_iota(jnp.int32, (CHUNK, C_WIDTH), 1) // head_dim
    for ci in range(tm // CHUNK):
        rows = slice(ci * CHUNK, (ci + 1) * CHUNK)
        v = v_ref[0, rows, :]
        mu = jnp.mean(v, axis=-1, keepdims=True)
        cen = v - mu
        var = jnp.mean(cen * cen, axis=-1, keepdims=True)
        vn = (cen * lax.rsqrt(var + LN_EPS) * lng_ref[...] + lnb_ref[...]).astype(BF16)
        for h in range(C_HEADS):
            vs_ref[h * CHUNK:(h + 1) * CHUNK, :] = jnp.where(head_of_lane == h, vn, jnp.zeros_like(vn))
        mixed = _bdot(wcat_ref[...], vs_ref[...]) + bs_ref[...]
        gt_ref[rows, :] = (u_ref[0, rows, :] * mixed).astype(BF16)
    y = (_bdot(gt_ref[...], wout_ref[0:C_WIDTH, :].astype(BF16))
         + _bdot(fo_ref[0], wout_ref[C_WIDTH:, :].astype(BF16)))
    o_ref[0] = x_ref[0] + mod_ref[0, 2:3, :] * _rms(y, g_ref[1:2, :])


def _gate_out(u, v, fo, x, ln_g, ln_b, w_s, b_s, w_out, g4, mods):
    bsz, length, d = x.shape
    tm = min(ROW_TILE, length)
    head_dim = C_WIDTH // C_HEADS
    wcat = w_s.transpose(1, 0, 2).reshape(CHUNK, C_HEADS * CHUNK).astype(BF16)
    bs_full = jnp.repeat(b_s.T, head_dim, axis=1)
    const = lambda b, i: (0, 0)
    return pl.pallas_call(
        functools.partial(_gate_out_kernel, tm=tm),
        grid=(bsz, length // tm),
        in_specs=[
            pl.BlockSpec((1, tm, C_WIDTH), lambda b, i: (b, i, 0)),
            pl.BlockSpec((1, tm, C_WIDTH), lambda b, i: (b, i, 0)),
            pl.BlockSpec((1, tm, D_WIDTH), lambda b, i: (b, i, 0)),
            pl.BlockSpec((1, tm, d), lambda b, i: (b, i, 0)),
            pl.BlockSpec((1, C_WIDTH), const),
            pl.BlockSpec((1, C_WIDTH), const),
            pl.BlockSpec((CHUNK, C_HEADS * CHUNK), const),
            pl.BlockSpec((CHUNK, C_WIDTH), const),
            pl.BlockSpec((C_WIDTH + D_WIDTH, d), const),
            pl.BlockSpec((4, d), const),
            pl.BlockSpec((1, 6, d), lambda b, i: (b, 0, 0)),
        ],
        out_specs=pl.BlockSpec((1, tm, d), lambda b, i: (b, i, 0)),
        out_shape=jax.ShapeDtypeStruct((bsz, length, d), F32),
        scratch_shapes=[
            pltpu.VMEM((C_HEADS * CHUNK, C_WIDTH), BF16),
            pltpu.VMEM((tm, C_WIDTH), BF16),
        ],
        compiler_params=_params(("parallel", "parallel")),
        name="gating_outproj",
    )(u, v, fo, x, ln_g.reshape(1, -1), ln_b.reshape(1, -1), wcat, bs_full, w_out, g4, mods)


def _router_kernel(x_ref, mod_ref, g_ref, wr_ref, tri_ref, xn_ref, meta_ref, cnt_ref, carry_ref, *, tm):
    first = jnp.logical_and(pl.program_id(0) == 0, pl.program_id(1) == 0)

    @pl.when(first)
    def _():
        carry_ref[...] = jnp.zeros_like(carry_ref)

    x = x_ref[0]
    h = _rms(x, g_ref[2:3, :]) * (1.0 + mod_ref[0, 4:5, :]) + mod_ref[0, 3:4, :]
    xn_ref[0] = h
    h_hi = h.astype(BF16)
    h_lo = (h - h_hi.astype(F32)).astype(BF16)
    logits = _bdot(jnp.concatenate([h_hi, h_hi, h_lo], axis=1), wr_ref[...])
    lane = lax.broadcasted_iota(jnp.int32, (tm, LANES), 1).astype(F32)
    neg = jnp.float32(-jnp.inf)
    lm = jnp.where(lane < N_EXPERTS, logits, neg)
    m1 = jnp.max(lm, axis=-1, keepdims=True)
    i1 = jnp.min(jnp.where(lm == m1, lane, float(LANES)), axis=-1, keepdims=True)
    lm2 = jnp.where(lane == i1, neg, lm)
    m2 = jnp.max(lm2, axis=-1, keepdims=True)
    i2 = jnp.min(jnp.where(lm2 == m2, lane, float(LANES)), axis=-1, keepdims=True)
    e = jnp.exp(m2 - m1)
    p1 = 1.0 / (1.0 + e)
    p2 = e / (1.0 + e)
    hot1 = (lane == i1).astype(F32)
    hot2 = (lane == i2).astype(F32)
    both = hot1 + hot2
    before = _bdot(tri_ref[...], both.astype(BF16)) + carry_ref[0:1, :]
    r1 = jnp.sum(before * hot1, axis=-1, keepdims=True)
    r2 = jnp.sum(before * hot2, axis=-1, keepdims=True)
    carry_ref[0:1, :] = carry_ref[0:1, :] + jnp.sum(both, axis=0, keepdims=True)
    cnt_ref[...] = carry_ref[...]
    vals = (i1, i2, p1, p2, r1, r2)
    meta = jnp.zeros((tm, LANES), F32)
    for k, val in enumerate(vals):
        meta = jnp.where(lane == k, val, meta)
    meta_ref[0] = meta


def _router(x, mods, g4, w_router):
    bsz, length, d = x.shape
    tm = min(ROW_TILE, length)
    wr = jnp.concatenate([w_router, jnp.zeros((d, LANES - N_EXPERTS), F32)], axis=1)
    wr_hi = wr.astype(BF16)
    wr_lo = (wr - wr_hi.astype(F32)).astype(BF16)
    wr = jnp.concatenate([wr_hi, wr_lo, wr_hi], axis=0)
    tri =jnp.asarray(np.tril(np.ones((tm, tm), np.float32), -1), dtype=BF16)
    const = lambda b, i: (0, 0)
    return pl.pallas_call(
        functools.partial(_router_kernel, tm=tm),
        grid=(bsz, length // tm),
        in_specs=[
            pl.BlockSpec((1, tm, d), lambda b, i: (b, i, 0)),
            pl.BlockSpec((1, 6, d), lambda b, i: (b, 0, 0)),
            pl.BlockSpec((4, d), const),
            pl.BlockSpec((3 * d, LANES), const),
            pl.BlockSpec((tm, tm), const),
        ],
        out_specs=[
            pl.BlockSpec((1, tm, d), lambda b, i: (b, i, 0)),
            pl.BlockSpec((1, tm, LANES), lambda b, i: (b, i, 0)),
            pl.BlockSpec((8, LANES), const),
        ],
        out_shape=[
            jax.ShapeDtypeStruct((bsz, length, d), F32),
            jax.ShapeDtypeStruct((bsz, length, LANES), F32),
            jax.ShapeDtypeStruct((8, LANES), F32),
        ],
        scratch_shapes=[pltpu.VMEM((8, LANES), F32)],
        compiler_params=_params(("arbitrary", "arbitrary")),
        name="moe_router",
    )(x, mods, g4, wr, tri)


def _expert_kernel(te_ref, nv_ref, x_ref, w1_ref, w3_ref, w2_ref, o_ref, acc_ref):
    i = pl.program_id(0)
    j = pl.program_id(1)

    @pl.when(i < nv_ref[0])
    def _():
        y = _swiglu_halves(x_ref[...].astype(BF16), w1_ref, w3_ref, w2_ref, lead=(0,))

        @pl.when(j == 0)
        def _():
            acc_ref[...] = y

        @pl.when(j > 0)
        def _():
            acc_ref[...] += y

        @pl.when(j == pl.num_programs(1) - 1)
        def _():
            o_ref[...] = acc_ref[...]


def _experts(x_sorted, tile_expert, n_valid, w1, w3, w2):
    rows = x_sorted.shape[0]
    d = w1.shape[1]
    dff = w1.shape[2]
    tm = EXPERT_ROW_TILE
    n_ff = 2
    tf = dff // n_ff

    def row_map(i, j, te, nv):
        return (jnp.maximum(jnp.minimum(i, nv[0] - 1), 0), 0)

    def w13_map(i, j, te, nv):
        return (te[jnp.maximum(jnp.minimum(i, nv[0] - 1), 0)], 0, jnp.where(i < nv[0], j, n_ff - 1))

    def w2_map(i, j, te, nv):
        return (te[jnp.maximum(jnp.minimum(i, nv[0] - 1), 0)], jnp.where(i < nv[0], j, n_ff - 1), 0)

    return pl.pallas_call(
        _expert_kernel,
        grid_spec=pltpu.PrefetchScalarGridSpec(
            num_scalar_prefetch=2,
            grid=(rows // tm, n_ff),
            in_specs=[
                pl.BlockSpec((tm, d), row_map),
                pl.BlockSpec((1, d, tf), w13_map),
                pl.BlockSpec((1, d, tf), w13_map),
                pl.BlockSpec((1, tf, d), w2_map),
            ],
            out_specs=pl.BlockSpec((tm, d), row_map),
            scratch_shapes=[pltpu.VMEM((tm, d), F32)],
        ),
        out_shape=jax.ShapeDtypeStruct((rows, d), F32),
        compiler_params=_params(("arbitrary", "arbitrary")),
        name="expert_swiglu",
    )(tile_expert, n_valid, x_sorted, w1, w3, w2)


def _combine_kernel(x_ref, y_ref, meta_ref, g_ref, mod_ref, o_ref):
    meta = meta_ref[0]
    y = meta[:, 2:3] * y_ref[0, 0] + meta[:, 3:4] * y_ref[1, 0]
    o_ref[0] = x_ref[0] + mod_ref[0, 5:6, :] * _rms(y, g_ref[3:4, :])


def _combine(x, y2, meta, g4, mods):
    bsz, length, d = x.shape
    tm = min(ROW_TILE, length)
    return pl.pallas_call(
        _combine_kernel,
        grid=(bsz, length // tm),
        in_specs=[
            pl.BlockSpec((1, tm, d), lambda b, i: (b, i, 0)),
            pl.BlockSpec((2, 1, tm, d), lambda b, i: (0, b, i, 0)),
            pl.BlockSpec((1, tm, LANES), lambda b, i: (b, i, 0)),
            pl.BlockSpec((4, d), lambda b, i: (0, 0)),
            pl.BlockSpec((1, 6, d), lambda b, i: (b, 0, 0)),
        ],
        out_specs=pl.BlockSpec((1, tm, d), lambda b, i: (b, i, 0)),
        out_shape=jax.ShapeDtypeStruct((bsz, length, d), F32),
        compiler_params=_params(("parallel", "parallel")),
        name="moe_combine",
    )(x, y2, meta, g4, mods)


SC_CORES = 2
SC_SUBCORES = 16
SC_WORKERS = SC_CORES * SC_SUBCORES
SC_ROW_CHUNK = 32


def _sc_mesh():
    return plsc.VectorSubcoreMesh(core_axis_name="c", subcore_axis_name="s")


def _sc_dispatch(rows_in, pos2, n_out):
    tokens, width = rows_in.shape
    per_worker = tokens // SC_WORKERS
    chunk = SC_ROW_CHUNK
    n_pairs = per_worker // (2 * chunk)
    idx = pos2.reshape(2, SC_WORKERS, 2 * n_pairs, chunk)

    @functools.partial(
        pl.kernel, mesh=_sc_mesh(),
        out_type=jax.ShapeDtypeStruct((n_out, width), rows_in.dtype),
        scratch_types=[
            pltpu.VMEM((2 * n_pairs, chunk), jnp.int32),
            pltpu.VMEM((2 * n_pairs, chunk), jnp.int32),
            pltpu.VMEM((2, chunk, width), rows_in.dtype),
        ] + [pltpu.SemaphoreType.DMA] * 6,
        name="sc_dispatch",
    )
    def body(rows_hbm, idx_hbm, out_hbm, idx0_v, idx1_v, rows_v, rsem0, rsem1, wsem0, wsem1, wsem2, wsem3):
        wid = lax.axis_index("s") * SC_CORES + lax.axis_index("c")
        base = wid * per_worker
        rsem = (rsem0, rsem1)
        wsem = ((wsem0, wsem1), (wsem2, wsem3))
        pltpu.sync_copy(idx_hbm.at[0, wid], idx0_v)
        pltpu.sync_copy(idx_hbm.at[1, wid], idx1_v)

        @pl.loop(0, n_pairs)
        def _(p):
            c0 = 2 * p
            reads = [pltpu.async_copy(rows_hbm.at[pl.ds(base + (c0 + s) * chunk, chunk)], rows_v.at[s], rsem[s])
                     for s in range(2)]
            writes = []
            for s in range(2):
                reads[s].wait()
                writes.append(pltpu.async_copy(rows_v.at[s], out_hbm.at[idx0_v.at[c0 + s]], wsem[s][0]))
                writes.append(pltpu.async_copy(rows_v.at[s], out_hbm.at[idx1_v.at[c0 + s]], wsem[s][1]))
            for w in writes:
                w.wait()

    return body(rows_in, idx)


def _sc_gather(table, idx):
    n = idx.shape[0]
    width = table.shape[1]
    chunk = SC_ROW_CHUNK
    per_worker = n // SC_WORKERS
    n_pairs = per_worker // (2 * chunk)
    idx3 = idx.reshape(SC_WORKERS, 2 * n_pairs, chunk)

    @functools.partial(
        pl.kernel, mesh=_sc_mesh(),
        out_type=jax.ShapeDtypeStruct((n, width), table.dtype),
        scratch_types=[
            pltpu.VMEM((2 * n_pairs, chunk), jnp.int32),
            pltpu.VMEM((2, chunk, width), table.dtype),
            pltpu.SemaphoreType.DMA, pltpu.SemaphoreType.DMA,
            pltpu.SemaphoreType.DMA, pltpu.SemaphoreType.DMA,
        ],
        name="sc_gather",
    )
    def body(table_hbm, idx_hbm, out_hbm, idx_v, rows_v, gsem0, gsem1, wsem0, wsem1):
        wid = lax.axis_index("s") * SC_CORES + lax.axis_index("c")
        base = wid * per_worker
        gsem = (gsem0, gsem1)
        wsem = (wsem0, wsem1)
        pltpu.sync_copy(idx_hbm.at[wid], idx_v)

        @pl.loop(0, n_pairs)
        def _(p):
            c0 = 2 * p
            gathers = [pltpu.async_copy(table_hbm.at[idx_v.at[c0 + s]], rows_v.at[s], gsem[s]) for s in range(2)]
            writes = []
            for s in range(2):
                gathers[s].wait()
                rows_out = out_hbm.at[pl.ds(base + (c0 + s) * chunk, chunk)]
                writes.append(pltpu.async_copy(rows_v.at[s], rows_out, wsem[s]))
            for s in range(2):
                writes[s].wait()

    return body(table, idx3)


def _moe(x, mods, g4, w_router, w1, w3, w2):
    bsz, length, d = x.shape
    tokens = bsz * length
    tm = EXPERT_ROW_TILE
    xn, meta, counts = _router(x, mods, g4, w_router)
    meta2 = meta.reshape(tokens, LANES)
    experts = meta2[:, 0:2].astype(jnp.int32)
    ranks = meta2[:, 4:6].astype(jnp.int32)
    cnt = counts[0, 0:N_EXPERTS].astype(jnp.int32)
    padded = ((cnt + tm - 1) // tm) * tm
    ends = jnp.cumsum(padded)
    starts = ends - padded
    pos = (starts[experts] + ranks).T.reshape(2 * tokens)
    rows = 2 * tokens + N_EXPERTS * tm
    n_tiles = rows // tm
    n_valid = (ends[-1] // tm).astype(jnp.int32).reshape(1)
    tile_start = jnp.arange(n_tiles, dtype=jnp.int32) * tm
    tile_expert = jnp.minimum(
        jnp.sum((tile_start[:, None] >= ends[None, :]).astype(jnp.int32), axis=1), N_EXPERTS - 1)
    x_sorted = _sc_dispatch(xn.reshape(tokens, d), pos.reshape(2, tokens), rows)
    y_sorted = _experts(x_sorted, tile_expert, n_valid, w1, w3, w2)
    y2 = _sc_gather(y_sorted, pos).reshape(2, bsz, length, d)
    return _combine(x, y2, meta, g4, mods)


def _grid_pos_embed(rows, d):
    row = np.repeat(np.arange(rows, dtype=np.float64), GRID_W)
    col = np.tile(np.arange(GRID_W, dtype=np.float64), rows)
    n_freq = d // 4
    omega = POS_THETA ** (-np.arange(n_freq, dtype=np.float64) / n_freq)
    ang_r = row[:, None] * omega
    ang_c = col[:, None] * omega
    return jnp.asarray(np.concatenate([np.sin(ang_r), np.cos(ang_r), np.sin(ang_c), np.cos(ang_c)], axis=-1),
                       dtype=F32)


def kernel(x, c, ctx, c_ctx, ada_w, ada_b, norm_g, ab_w_in, rg_conv_w, rg_conv_b, rg_w_r, rg_b_r, rg_w_i, rg_b_i, rg_lambda, cv_dw_w, cv_dw_b, cv_ln_g, cv_ln_b, ab_w_out, ffn_w1, ffn_w3, ffn_w2, cd_w_in, sg_ln_g, sg_ln_b, sg_w_s, sg_b_s, cd_w_out, moe_router, moe_w1, moe_w3, moe_w2):
    bsz, length, d = x.shape
    assert ada_w.shape[0] == 2, "two layers: one even (RG-LRU | Conformer), one odd (gMLP | Fourier)"
    pos = _grid_pos_embed(length // GRID_W, d)

    cpad = jnp.concatenate([c, c_ctx[None, :], jnp.zeros((16 - bsz - 1, d), F32)], axis=0)
    m = _ada(cpad, ada_w, ada_b)
    mods = m[:, :bsz].reshape(2, bsz, 6, d)
    mods_ctx = m[:, bsz:bsz + 1].reshape(2, 1, 6, d)

    mods_t = mods[0].transpose(1, 0, 2)
    mods_ctx_t = jnp.broadcast_to(mods_ctx[0].reshape(6, 1, d), (6, bsz, d))
    pos3 = pos[:, None, :]
    w_in = ab_w_in[0]
    xa3, gg3, glu3, ffn1b, ffn3b, ffn2b = _inproj_t(x, pos3, mods_t, norm_g[0], w_in, rg_conv_w[0], rg_conv_b[0], True,
                                                    riders=(ffn_w1[0], ffn_w3[0], ffn_w2[0]))
    (xc3,) = _inproj_t(ctx, None, mods_ctx_t, norm_g[0], w_in[:, 0:A_WIDTH], rg_conv_w[0], rg_conv_b[0], False)
    wg, bg, sp = _gate_weights(rg_w_r[0], rg_b_r[0], rg_w_i[0], rg_b_i[0], rg_lambda[0])
    n_exp, _, dff_e = moe_w1[0].shape
    rec_hi, rec_lo, w2b = _scan_t(xa3, xc3, gg3, wg, bg, sp, riders=(moe_w2[0].reshape(n_exp * dff_e, d),))
    u3, w1b = _conv_t(glu3, cv_dw_w[0], cv_dw_b[0], cv_ln_g[0], cv_ln_b[0],
                      riders=(moe_w1[0].reshape(n_exp * d, dff_e),))
    x1_t = _outproj_t(rec_lo, rec_hi, u3, x, pos3, ab_w_out[0], norm_g[0], mods_t)
    x2, w3b = _ffn_t(x1_t, mods_t, norm_g[0], ffn1b, ffn3b, ffn2b, riders=(moe_w3[0].reshape(n_exp * d, dff_e),))

    u, v, f = _inproj_cd(x2, mods[1], norm_g[1], cd_w_in[0])
    fo = _fourier(f)
    x3 = _gate_out(u, v, fo, x2, sg_ln_g[0], sg_ln_b[0], sg_w_s[0], sg_b_s[0], cd_w_out[0],
                   norm_g[1], mods[1])
    return _moe(x3, mods[1], norm_g[1], moe_router[0], w1b.reshape(n_exp, d, dff_e), w3b.reshape(n_exp, d, dff_e),
                w2b.reshape(n_exp, dff_e, d))
```

```python
import functools

import numpy as np
import jax
import jax.numpy as jnp
from jax import lax
from jax.experimental import pallas as pl
from jax.experimental.pallas import tpu as pltpu
from jax.experimental.pallas import tpu_sc as plsc

F32 = jnp.float32
BF16 = jnp.bfloat16

GRID_W = 64
POS_THETA = 10000.0
NORM_EPS = 1e-6
LN_EPS = 1e-5
LRU_C = 8.0

A_WIDTH = 512
A_HEADS = 8
A_CONV_W = 4
B_WIDTH = 512
B_CONV_W = 31
C_WIDTH = 512
C_HEADS = 8
CHUNK = 128
D_WIDTH = 512
D_GROUPS = 4
D_GROUP_DIM = D_WIDTH // D_GROUPS
N_EXPERTS = 8

LANES = 128
ROW_TILE = 512
EXPERT_ROW_TILE = 512
VMEM_LIMIT = 56 * 2 ** 20


def _params(sem):
    return pltpu.CompilerParams(dimension_semantics=sem, vmem_limit_bytes=VMEM_LIMIT)


def _rms(x, g):
    return x * lax.rsqrt(jnp.mean(x * x, axis=-1, keepdims=True) + NORM_EPS) * g


def _bdot(a, b):
    return jnp.dot(a, b, preferred_element_type=F32)


MXU_WIDTH = 256


def _swiglu_halves(x, w1_ref, w3_ref, w2_ref, lead=()):
    n = w1_ref.shape[-1]
    half = -(-(n // 2) // MXU_WIDTH) * MXU_WIDTH
    y = None
    for lo, hi in ((0, half), (half, n)):
        a = _bdot(x, w1_ref[lead + (slice(None), slice(lo, hi))])
        b = _bdot(x, w3_ref[lead + (slice(None), slice(lo, hi))])
        t = (jax.nn.silu(a) * b).astype(BF16)
        part = _bdot(t, w2_ref[lead + (slice(lo, hi), slice(None))])
        y = part if y is None else y + part
    return y


def _ada_kernel(c_ref, w_ref, b_ref, o_ref):
    s = jax.nn.silu(c_ref[...])
    o_ref[0] = _bdot(s.astype(BF16), w_ref[0].astype(BF16)) + b_ref[0]


def _ada(cpad, ada_w, ada_b):
    nl, d, n6 = ada_w.shape
    rows = cpad.shape[0]
    tn = n6 // 4
    return pl.pallas_call(
        _ada_kernel,
        grid=(nl, n6 // tn),
        in_specs=[
            pl.BlockSpec((rows, d), lambda l, j: (0, 0)),
            pl.BlockSpec((1, d, tn), lambda l, j: (l, 0, j)),
            pl.BlockSpec((1, 1, tn), lambda l, j: (l, 0, j)),
        ],
        out_specs=pl.BlockSpec((1, rows, tn), lambda l, j: (l, 0, j)),
        out_shape=jax.ShapeDtypeStruct((nl, rows, n6), F32),
        compiler_params=_params(("arbitrary", "arbitrary")),
        name="ada_terms",
    )(cpad, ada_w, ada_b.reshape(nl, 1, n6))


TIME_TILE = 128
SCAN_TIME_TILE = 128
SCAN_LANES = 256
CONV_TIME_TILE = 64
CONV_HALO = 16


def _time_major(ref):
    return jnp.swapaxes(ref[...], 0, 1)


def _inproj_t_kernel(*refs, tt, add_pos, branches, n_cast):
    if n_cast:
        n_in = len(refs) - n_cast - (4 if branches else 1) - n_cast
        _cast_riders(refs[n_in:n_in + n_cast], refs[len(refs) - n_cast:])
        refs = refs[:n_in] + refs[n_in + n_cast:len(refs) - n_cast]
    if add_pos:
        x_ref, xp_ref, xn_ref, pos_ref, posp_ref, posn_ref, mod_ref, g_ref, w_ref, cw_ref, cb_ref, *outs = refs
        xall = jnp.concatenate(
            [_time_major(xp_ref)[7:8] + posp_ref[...], _time_major(x_ref) + pos_ref[...],
             _time_major(xn_ref)[0:2] + posn_ref[...]], axis=0)
    else:
        x_ref, xp_ref, xn_ref, mod_ref, g_ref, w_ref, cw_ref, cb_ref, *outs = refs
        xall = jnp.concatenate([_time_major(xp_ref)[7:8], _time_major(x_ref), _time_major(xn_ref)[0:2]], axis=0)
    i = pl.program_id(0)
    last = pl.num_programs(0) - 1
    steps, bsz, d = xall.shape
    h = _rms(xall, g_ref[0:1, :]) * (1.0 + mod_ref[1]) + mod_ref[0]
    z = _bdot(h.reshape(steps * bsz, d).astype(BF16), w_ref[...].astype(BF16))
    z3 = z.reshape(steps, bsz, z.shape[-1])
    t_idx = lax.broadcasted_iota(jnp.int32, (steps, bsz, A_WIDTH), 0)
    inside = jnp.logical_and(jnp.logical_or(t_idx >= 1, i > 0), jnp.logical_or(t_idx <= tt, i < last))
    xa = jnp.where(inside, z3[:, :, 0:A_WIDTH], 0.0)
    xconv = cb_ref[...] + cw_ref[0:1, :] * xa[0:tt]
    for k in range(1, A_CONV_W):
        xconv = xconv + cw_ref[k:k + 1, :] * xa[k:k + tt]
    outs[0][...] = xconv
    if branches:
        outs[1][...] = jax.nn.gelu(z3[1:tt + 1, :, A_WIDTH:2 * A_WIDTH])
        vb = z3[1:tt + 1, :, 2 * A_WIDTH:2 * A_WIDTH + B_WIDTH]
        gb = z3[1:tt + 1, :, 2 * A_WIDTH + B_WIDTH:]
        outs[2][...] = vb * jax.nn.sigmoid(gb)
        outs[3][...] = xall[1:tt + 1]


def _inproj_t(x, pos3, mods_t, g4, w, conv_w, conv_b, branches, riders=()):
    bsz, length, d = x.shape
    n = w.shape[1]
    tt = min(TIME_TILE, length)
    add_pos = pos3 is not None
    n_out = 3 if branches else 1
    prev_map = lambda i: (jnp.maximum(i * tt - 1, 0), 0, 0)
    next_map = lambda i: (jnp.minimum((i + 1) * (tt // 2), length // 2 - 1), 0, 0)
    in_specs = [
        pl.BlockSpec((bsz, tt, d), lambda i: (0, i, 0)),
        pl.BlockSpec((bsz, 8, d), lambda i: (0, jnp.maximum(i * (tt // 8) - 1, 0), 0)),
        pl.BlockSpec((bsz, 8, d), lambda i: (0, jnp.minimum((i + 1) * (tt // 8), length // 8 - 1), 0)),
    ]
    args = [x, x, x]
    if add_pos:
        in_specs += [
            pl.BlockSpec((tt, 1, d), lambda i: (i, 0, 0)),
            pl.BlockSpec((1, 1, d), prev_map),
            pl.BlockSpec((2, 1, d), next_map),
        ]
        args += [pos3, pos3, pos3]
    in_specs += [
        pl.BlockSpec((6, bsz, d), lambda i: (0, 0, 0)),
        pl.BlockSpec((4, d), lambda i: (0, 0)),
        pl.BlockSpec((d, n), lambda i: (0, 0)),
        pl.BlockSpec((A_CONV_W, A_WIDTH), lambda i: (0, 0)),
        pl.BlockSpec((1, A_WIDTH), lambda i: (0, 0)),
    ]
    args += [mods_t, g4, w, conv_w, conv_b.reshape(1, A_WIDTH)]
    r_in, r_out, r_shapes = _rider_specs(riders, length // tt, lambda i: i)
    out_specs = [pl.BlockSpec((tt, bsz, A_WIDTH), lambda i: (i, 0, 0))] * n_out
    out_shapes = [jax.ShapeDtypeStruct((length, bsz, A_WIDTH), F32)] * n_out
    if branches:
        out_specs.append(pl.BlockSpec((tt, bsz, d), lambda i: (i, 0, 0)))
        out_shapes.append(jax.ShapeDtypeStruct((length, bsz, d), F32))
    return pl.pallas_call(
        functools.partial(_inproj_t_kernel, tt=tt, add_pos=add_pos, branches=branches, n_cast=len(riders)),
        grid=(length // tt,),
        in_specs=in_specs + r_in,
        out_specs=out_specs + r_out,
        out_shape=out_shapes + r_shapes,
        compiler_params=_params(("parallel",)),
        name="inproj_ab" if branches else "inproj_ctx",
    )(*args, *riders)


def _cast_riders(cast_in, cast_out):
    for src, dst in zip(cast_in, cast_out):
        dst[...] = src[...].astype(BF16)


def _rider_specs(arrays, n_steps, step_of):
    in_specs, out_specs, out_shapes = [], [], []
    for arr in arrays:
        rows, cols = arr.shape
        block = (rows // n_steps, cols)
        in_specs.append(pl.BlockSpec(block, lambda *idx: (step_of(*idx), 0)))
        out_specs.append(pl.BlockSpec(block, lambda *idx: (step_of(*idx), 0)))
        out_shapes.append(jax.ShapeDtypeStruct(arr.shape, BF16))
    return in_specs, out_specs, out_shapes


def _scan_t_kernel(*refs, tb, nb, ctx_len, n_cast):
    xf_ref, xb_ref, ggf_ref, ggb_ref, xc_ref, wg_ref, bg_ref, sp_ref = refs[:8]
    cast_in = refs[8:8 + n_cast]
    hi_ref, lo_ref = refs[8 + n_cast:10 + n_cast]
    cast_out = refs[10 + n_cast:10 + 2 * n_cast]
    a_ref, b_ref, h_ref, state_ref = refs[10 + 2 * n_cast:]
    _cast_riders(cast_in, cast_out)
    k = pl.program_id(1)
    half = nb // 2
    bsz, lanes = xf_ref.shape[1], xf_ref.shape[2]
    chunk = 128

    def coefficients(x_ref, n, direction):
        def body(c, _):
            t0 = pl.multiple_of(c * chunk, chunk)
            x = x_ref[pl.ds(t0, chunk)].reshape(chunk * bsz, lanes)
            th = jnp.tanh(_bdot(x.astype(BF16), wg_ref[0, direction]) + bg_ref[0, direction])
            i = 0.5 * th[:, lanes:] + 0.5
            c = (-0.5 * LRU_C) * sp_ref[0, direction:direction + 1, :]
            log_a = c * th[:, 0:lanes] + c
            a = jnp.exp(log_a)
            one_minus_a2 = -jnp.tanh(log_a) * (a * a + 1.0)
            a_ref[direction, pl.ds(t0, chunk)] = a.reshape(chunk, bsz, lanes)
            b_ref[direction, pl.ds(t0, chunk)] = (jnp.sqrt(one_minus_a2) * (i * x)).reshape(chunk, bsz, lanes)
            return 0
        lax.fori_loop(0, n // chunk, body, 0)

    def sweep(n, store):
        def body(t, carry):
            hf, hb = carry
            tr = n - 1 - t
            hf = a_ref[0, pl.ds(t, 1)][0] * hf + b_ref[0, pl.ds(t, 1)][0]
            hb = a_ref[1, pl.ds(tr, 1)][0] * hb + b_ref[1, pl.ds(tr, 1)][0]
            if store:
                b_ref[0, pl.ds(t, 1)] = hf[None]
                b_ref[1, pl.ds(tr, 1)] = hb[None]
            return hf, hb
        hf, hb = lax.fori_loop(0, n, body, (state_ref[0], state_ref[1]), unroll=8)
        state_ref[0] = hf
        state_ref[1] = hb

    @pl.when(k == 0)
    def _():
        state_ref[...] = jnp.zeros_like(state_ref)
        coefficients(xc_ref, ctx_len, 0)
        coefficients(xc_ref, ctx_len, 1)
        sweep(ctx_len, False)

    coefficients(xf_ref, tb, 0)
    coefficients(xb_ref, tb, 1)
    sweep(tb, True)
    m = nb - 1 - k

    @pl.when(k < half)
    def _():
        h_ref[k] = b_ref[0, 0:tb]
        h_ref[m] = b_ref[1, 0:tb]

    @pl.when(k >= half)
    def _():
        hi_ref[...] = (b_ref[0, 0:tb] + h_ref[k]) * ggf_ref[...]
        lo_ref[...] = (h_ref[m] + b_ref[1, 0:tb]) * ggb_ref[...]


def _scan_t(xa3, xc3, gg3, wg, bg, sp, riders=()):
    length, bsz, width = xa3.shape
    ctx_len = xc3.shape[0]
    tb = min(SCAN_TIME_TILE, length // 2)
    nb = length // tb
    half = nb // 2
    lanes = SCAN_LANES
    blk = (tb, bsz, lanes)
    r_in, r_out, r_shapes = _rider_specs(riders, (width // lanes) * nb, lambda g, k: g * nb + k)
    return pl.pallas_call(
        functools.partial(_scan_t_kernel, tb=tb, nb=nb, ctx_len=ctx_len, n_cast=len(riders)),
        grid=(width // lanes, nb),
        in_specs=[
            pl.BlockSpec(blk, lambda g, k: (k, 0, g)),
            pl.BlockSpec(blk, lambda g, k: (nb - 1 - k, 0, g)),
            pl.BlockSpec(blk, lambda g, k: (jnp.maximum(k, half), 0, g)),
            pl.BlockSpec(blk, lambda g, k: (jnp.minimum(nb - 1 - k, half - 1), 0, g)),
            pl.BlockSpec((ctx_len, bsz, lanes), lambda g, k: (0, 0, g)),
            pl.BlockSpec((1, 2, lanes, 2 * lanes), lambda g, k: (g, 0, 0, 0)),
            pl.BlockSpec((1, 2, 1, 2 * lanes), lambda g, k: (g, 0, 0, 0)),
            pl.BlockSpec((1, 2, lanes), lambda g, k: (g, 0, 0)),
        ] + r_in,
        out_specs=[
            pl.BlockSpec(blk, lambda g, k: (jnp.maximum(k - half, 0), 0, g)),
            pl.BlockSpec(blk, lambda g, k: (jnp.minimum(nb - 1 - k, half - 1), 0, g)),
        ] + r_out,
        out_shape=[jax.ShapeDtypeStruct((length // 2, bsz, width), F32)] * 2 + r_shapes,
        scratch_shapes=[
            pltpu.VMEM((2, max(tb, ctx_len), bsz, lanes), F32),
            pltpu.VMEM((2, max(tb, ctx_len), bsz, lanes), F32),
            pltpu.VMEM((nb, tb, bsz, lanes), F32),
            pltpu.VMEM((2, bsz, lanes), F32),
        ],
        compiler_params=_params(("parallel", "arbitrary")),
        name="rglru_scan",
    )(xa3, xa3, gg3, gg3, xc3, wg, bg, sp, *riders)


def _gate_weights(w_r, b_r, w_i, b_i, lam):
    hd = A_WIDTH // A_HEADS
    heads_per_group = SCAN_LANES // hd
    ngroups = A_WIDTH // SCAN_LANES

    def blockdiag(w):
        w = w.reshape(ngroups, heads_per_group, hd, hd)
        eye = jnp.eye(heads_per_group, dtype=w.dtype)
        return jnp.einsum("ghde,hk->ghdke", w, eye).reshape(ngroups, SCAN_LANES, SCAN_LANES)

    wg = jnp.stack([jnp.concatenate([blockdiag(w_r[d]), blockdiag(w_i[d])], axis=-1) for d in range(2)], axis=1)
    bg = jnp.stack([jnp.concatenate([b_r[d].reshape(ngroups, 1, SCAN_LANES), b_i[d].reshape(ngroups, 1, SCAN_LANES)],
                                    axis=-1) for d in range(2)], axis=1)
    sp = jax.nn.softplus(-lam.astype(F32)).reshape(2, ngroups, SCAN_LANES).transpose(1, 0, 2)
    return (0.5 * wg).astype(BF16), 0.5 * bg, sp


def _conv_t_kernel(*refs, tb, n_cast):
    x_ref, xp_ref, xn_ref, w_ref, b_ref, lng_ref, lnb_ref = refs[:7]
    cast_in = refs[7:7 + n_cast]
    o_ref = refs[7 + n_cast]
    cast_out = refs[8 + n_cast:8 + 2 * n_cast]
    stage_ref = refs[-1]
    _cast_riders(cast_in, cast_out)
    i = pl.program_id(0)
    last = pl.num_programs(0) - 1
    halo = CONV_HALO
    stage_ref[0:halo] = jnp.where(i > 0, xp_ref[...], 0.0)
    stage_ref[halo:halo + tb] = x_ref[...]
    stage_ref[halo + tb:2 * halo + tb] = jnp.where(i < last, xn_ref[...], 0.0)
    sub = 8

    def body(r, _):
        t0 = pl.multiple_of(r * sub, sub)
        def tap(k, acc):
            return acc + w_ref[k] * stage_ref[pl.ds(t0 + 1 + k, sub)]
        acc = lax.fori_loop(0, B_CONV_W + 1, tap, jnp.broadcast_to(b_ref[...], (sub,) + b_ref.shape), unroll=8)
        o_ref[pl.ds(t0, sub)] = acc
        return 0
    lax.fori_loop(0, tb // sub, body, 0)

    norm_rows = 64

    def norm_body(r, _):
        t0 = pl.multiple_of(r * norm_rows, norm_rows)
        acc = o_ref[pl.ds(t0, norm_rows)]
        mu = jnp.mean(acc, axis=-1, keepdims=True)
        cen = acc - mu
        var = jnp.mean(cen * cen, axis=-1, keepdims=True)
        y = cen * lax.rsqrt(var + LN_EPS) * lng_ref[...] + lnb_ref[...]
        o_ref[pl.ds(t0, norm_rows)] = jax.nn.silu(y)
        return 0
    lax.fori_loop(0, tb // norm_rows, norm_body, 0)


def _conv_t(glu3, dw_w, dw_b, ln_g, ln_b, riders=()):
    length, bsz, width = glu3.shape
    tb = min(CONV_TIME_TILE, length)
    halo = CONV_HALO
    taps = dw_w.shape[0]
    w8 = jnp.broadcast_to(jnp.concatenate([dw_w, jnp.zeros((32 - taps, width), F32)], axis=0)[:, None, :],
                          (32, bsz, width))
    b8 = jnp.broadcast_to(dw_b[None, :], (bsz, width))
    r_in, r_out, r_shapes = _rider_specs(riders, length // tb, lambda i: i)
    return pl.pallas_call(
        functools.partial(_conv_t_kernel, tb=tb, n_cast=len(riders)),
        grid=(length // tb,),
        in_specs=[
            pl.BlockSpec((tb, bsz, width), lambda i: (i, 0, 0)),
            pl.BlockSpec((halo, bsz, width), lambda i: (jnp.maximum(i * (tb // halo) - 1, 0), 0, 0)),
            pl.BlockSpec((halo, bsz, width), lambda i: (jnp.minimum((i + 1) * (tb // halo), length // halo - 1), 0, 0)),
            pl.BlockSpec((32, bsz, width), lambda i: (0, 0, 0)),
            pl.BlockSpec((bsz, width), lambda i: (0, 0)),
            pl.BlockSpec((1, width), lambda i: (0, 0)),
            pl.BlockSpec((1, width), lambda i: (0, 0)),
        ] + r_in,
        out_specs=[pl.BlockSpec((tb, bsz, width), lambda i: (i, 0, 0))] + r_out,
        out_shape=[jax.ShapeDtypeStruct((length, bsz, width), F32)] + r_shapes,
        scratch_shapes=[pltpu.VMEM((tb + 2 * halo, bsz, width), F32)],
        compiler_params=_params(("parallel",)),
        name="conformer_conv",
    )(glu3, glu3, glu3, w8, b8, ln_g.reshape(1, -1), ln_b.reshape(1, -1), *riders)


def _outproj_t_kernel(lo_ref, hi_ref, u_ref, x_ref, wout_ref, g_ref, mod_ref, o_ref, *, half_steps):
    i = pl.program_id(0)
    tt, bsz, d = x_ref.shape
    rec = jnp.where(i < half_steps, lo_ref[...], hi_ref[...])
    rows = tt * bsz
    y = (_bdot(rec.reshape(rows, A_WIDTH).astype(BF16), wout_ref[0:A_WIDTH, :].astype(BF16))
         + _bdot(u_ref[...].reshape(rows, B_WIDTH).astype(BF16), wout_ref[A_WIDTH:, :].astype(BF16)))
    yn = _rms(y, g_ref[1:2, :]).reshape(tt, bsz, d)
    o_ref[...] = x_ref[...] + mod_ref[2] * yn


def _outproj_t(rec_lo, rec_hi, u3, x0_t, w_out, g4, mods_t):
    length, bsz, d = x0_t.shape
    tt = min(TIME_TILE // 2, length // 2)
    half_steps = (length // 2) // tt
    return pl.pallas_call(
        functools.partial(_outproj_t_kernel, half_steps=half_steps),
        grid=(length // tt,),
        in_specs=[
            pl.BlockSpec((tt, bsz, A_WIDTH), lambda i: (jnp.minimum(i, half_steps - 1), 0, 0)),
            pl.BlockSpec((tt, bsz, A_WIDTH), lambda i: (jnp.maximum(i - half_steps, 0), 0, 0)),
            pl.BlockSpec((tt, bsz, B_WIDTH), lambda i: (i, 0, 0)),
            pl.BlockSpec((tt, bsz, d), lambda i: (i, 0, 0)),
            pl.BlockSpec((A_WIDTH + B_WIDTH, d), lambda i: (0, 0)),
            pl.BlockSpec((4, d), lambda i: (0, 0)),
            pl.BlockSpec((6, bsz, d), lambda i: (0, 0, 0)),
        ],
        out_specs=pl.BlockSpec((tt, bsz, d), lambda i: (i, 0, 0)),
        out_shape=jax.ShapeDtypeStruct((length, bsz, d), F32),
        compiler_params=_params(("parallel",)),
        name="mixer_ab_outproj",
    )(rec_lo, rec_hi, u3, x0_t, w_out, g4, mods_t)


def _ffn_t_kernel(x_ref, mod_ref, g_ref, w1_ref, w3_ref, w2_ref, *rest, n_cast):
    _cast_riders(rest[:n_cast], rest[n_cast + 1:])
    o_ref = rest[n_cast]
    x = x_ref[...]
    tt, bsz, d = x.shape
    h = (_rms(x, g_ref[2:3, :]) * (1.0 + mod_ref[4]) + mod_ref[3]).reshape(tt * bsz, d).astype(BF16)
    y = _swiglu_halves(h, w1_ref, w3_ref, w2_ref)
    o_ref[...] = jnp.swapaxes(x + mod_ref[5] * _rms(y, g_ref[3:4, :]).reshape(tt, bsz, d), 0, 1)


def _ffn_t(x3, mods_t, g4, w1, w3, w2, riders=()):
    length, bsz, d = x3.shape
    dff = w1.shape[1]
    tt = min(TIME_TILE // 2, length)
    r_in, r_out, r_shapes = _rider_specs(riders, length // tt, lambda i: i)
    return pl.pallas_call(
        functools.partial(_ffn_t_kernel, n_cast=len(riders)),
        grid=(length // tt,),
        in_specs=[
            pl.BlockSpec((tt, bsz, d), lambda i: (i, 0, 0)),
            pl.BlockSpec((6, bsz, d), lambda i: (0, 0, 0)),
            pl.BlockSpec((4, d), lambda i: (0, 0)),
            pl.BlockSpec((d, dff), lambda i: (0, 0), pipeline_mode=pl.Buffered(1)),
            pl.BlockSpec((d, dff), lambda i: (0, 0), pipeline_mode=pl.Buffered(1)),
            pl.BlockSpec((dff, d), lambda i: (0, 0), pipeline_mode=pl.Buffered(1)),
        ] + r_in,
        out_specs=[pl.BlockSpec((bsz, tt, d), lambda i: (0, i, 0))] + r_out,
        out_shape=[jax.ShapeDtypeStruct((bsz, length, d), F32)] + r_shapes,
        compiler_params=_params(("parallel",)),
        name="dense_swiglu",
    )(x3, mods_t, g4, w1, w3, w2, *riders)


def _inproj_cd_kernel(x_ref, mod_ref, g_ref, w_ref, u_ref, v_ref, f_ref):
    x = x_ref[0]
    h = _rms(x, g_ref[0:1, :]) * (1.0 + mod_ref[0, 1:2, :]) + mod_ref[0, 0:1, :]
    z = _bdot(h.astype(BF16), w_ref[...].astype(BF16))
    u_ref[0] = jax.nn.gelu(z[:, 0:C_WIDTH])
    v_ref[0] = jax.nn.gelu(z[:, C_WIDTH:2 * C_WIDTH])
    f_ref[0] = z[:, 2 * C_WIDTH:]


def _inproj_cd(x, mods, g4, w):
    bsz, length, d = x.shape
    n = w.shape[1]
    tm = min(ROW_TILE, length)
    return pl.pallas_call(
        _inproj_cd_kernel,
        grid=(bsz, length // tm),
        in_specs=[
            pl.BlockSpec((1, tm, d), lambda b, i: (b, i, 0)),
            pl.BlockSpec((1, 6, d), lambda b, i: (b, 0, 0)),
            pl.BlockSpec((4, d), lambda b, i: (0, 0)),
            pl.BlockSpec((d, n), lambda b, i: (0, 0)),
        ],
        out_specs=[pl.BlockSpec((1, tm, C_WIDTH), lambda b, i: (b, i, 0))] * 3,
        out_shape=[jax.ShapeDtypeStruct((bsz, length, C_WIDTH), F32)] * 3,
        compiler_params=_params(("parallel", "parallel")),
        name="inproj_cd",
    )(x, mods, g4, w)


def _fourier_kernel(f_ref, dmat_ref, lmat_ref, o_ref, z_ref, *, length, scale):
    j = pl.program_id(1)
    rows_per_step = 512 if length % 512 == 0 else length

    @pl.when(j == 0)
    def _():
        def body(c, _):
            rows = pl.ds(pl.multiple_of(c * rows_per_step, rows_per_step), rows_per_step)
            fb = f_ref[0, rows, :].astype(BF16)
            for g in range(D_GROUPS):
                cs = _bdot(fb[:, g * D_GROUP_DIM:(g + 1) * D_GROUP_DIM], dmat_ref[...])
                z_ref[0, rows, g * D_GROUP_DIM:(g + 1) * D_GROUP_DIM] = cs[:, 0:D_GROUP_DIM].astype(BF16)
                z_ref[1, rows, g * D_GROUP_DIM:(g + 1) * D_GROUP_DIM] = cs[:, D_GROUP_DIM:].astype(BF16)
            return 0
        lax.fori_loop(0, length // rows_per_step, body, 0)

    out = _bdot(lmat_ref[0], z_ref[0]) + _bdot(lmat_ref[1], z_ref[1])
    o_ref[0] = (out * scale).astype(BF16)


def _dft_matrices(length):
    k = np.arange(length, dtype=np.int64)
    ang_l = 2.0 * np.pi * ((k[:, None] * k[None, :]) % length).astype(np.float64) / length
    lmat = np.stack([np.cos(ang_l), -np.sin(ang_l)], axis=0)
    d = np.arange(D_GROUP_DIM, dtype=np.int64)
    ang_d = 2.0 * np.pi * ((d[:, None] * d[None, :]) % D_GROUP_DIM).astype(np.float64) / D_GROUP_DIM
    dmat = np.concatenate([np.cos(ang_d), np.sin(ang_d)], axis=1)
    return jnp.asarray(lmat, dtype=F32).astype(BF16), jnp.asarray(dmat, dtype=F32).astype(BF16)


def _fourier(f):
    bsz, length, width = f.shape
    tm = min(ROW_TILE, length)
    lmat, dmat = _dft_matrices(length)
    scale = float(1.0 / np.sqrt(length * D_GROUP_DIM))
    return pl.pallas_call(
        functools.partial(_fourier_kernel, length=length, scale=scale),
        grid=(bsz, length // tm),
        in_specs=[
            pl.BlockSpec((1, length, width), lambda b, i: (b, 0, 0)),
            pl.BlockSpec((D_GROUP_DIM, 2 * D_GROUP_DIM), lambda b, i: (0, 0)),
            pl.BlockSpec((2, tm, length), lambda b, i: (0, i, 0)),
        ],
        out_specs=pl.BlockSpec((1, tm, width), lambda b, i: (b, i, 0)),
        out_shape=jax.ShapeDtypeStruct((bsz, length, width), BF16),
        scratch_shapes=[pltpu.VMEM((2, length, width), BF16)],
        compiler_params=_params(("parallel", "arbitrary")),
        name="fourier_mix",
    )(f, dmat, lmat)


def _gate_out_kernel(u_ref, v_ref, fo_ref, x_ref, lng_ref, lnb_ref, wcat_ref, bs_ref, wout_ref, g_ref, mod_ref,
                     o_ref, vs_ref, gt_ref, *, tm):
    head_dim = C_WIDTH // C_HEADS
    head_of_lane = lax.broadcasted_iota(jnp.int32, (CHUNK, C_WIDTH), 1) // head_dim
    for ci in range(tm // CHUNK):
        rows = slice(ci * CHUNK, (ci + 1) * CHUNK)
        v = v_ref[0, rows, :]
        mu = jnp.mean(v, axis=-1, keepdims=True)
        cen = v - mu
        var = jnp.mean(cen * cen, axis=-1, keepdims=True)
        vn = (cen * lax.rsqrt(var + LN_EPS) * lng_ref[...] + lnb_ref[...]).astype(BF16)
        for h in range(C_HEADS):
            vs_ref[h * CHUNK:(h + 1) * CHUNK, :] = jnp.where(head_of_lane == h, vn, jnp.zeros_like(vn))
        mixed = _bdot(wcat_ref[...], vs_ref[...]) + bs_ref[...]
        gt_ref[rows, :] = (u_ref[0, rows, :] * mixed).astype(BF16)
    y = (_bdot(gt_ref[...], wout_ref[0:C_WIDTH, :].astype(BF16))
         + _bdot(fo_ref[0], wout_ref[C_WIDTH:, :].astype(BF16)))
    o_ref[0] = x_ref[0] + mod_ref[0, 2:3, :] * _rms(y, g_ref[1:2, :])


def _gate_out(u, v, fo, x, ln_g, ln_b, w_s, b_s, w_out, g4, mods):
    bsz, length, d = x.shape
    tm = min(ROW_TILE, length)
    head_dim = C_WIDTH // C_HEADS
    wcat = w_s.transpose(1, 0, 2).reshape(CHUNK, C_HEADS * CHUNK).astype(BF16)
    bs_full = jnp.repeat(b_s.T, head_dim, axis=1)
    const = lambda b, i: (0, 0)
    return pl.pallas_call(
        functools.partial(_gate_out_kernel, tm=tm),
        grid=(bsz, length // tm),
        in_specs=[
            pl.BlockSpec((1, tm, C_WIDTH), lambda b, i: (b, i, 0)),
            pl.BlockSpec((1, tm, C_WIDTH), lambda b, i: (b, i, 0)),
            pl.BlockSpec((1, tm, D_WIDTH), lambda b, i: (b, i, 0)),
            pl.BlockSpec((1, tm, d), lambda b, i: (b, i, 0)),
            pl.BlockSpec((1, C_WIDTH), const),
            pl.BlockSpec((1, C_WIDTH), const),
            pl.BlockSpec((CHUNK, C_HEADS * CHUNK), const),
            pl.BlockSpec((CHUNK, C_WIDTH), const),
            pl.BlockSpec((C_WIDTH + D_WIDTH, d), const),
            pl.BlockSpec((4, d), const),
            pl.BlockSpec((1, 6, d), lambda b, i: (b, 0, 0)),
        ],
        out_specs=pl.BlockSpec((1, tm, d), lambda b, i: (b, i, 0)),
        out_shape=jax.ShapeDtypeStruct((bsz, length, d), F32),
        scratch_shapes=[
            pltpu.VMEM((C_HEADS * CHUNK, C_WIDTH), BF16),
            pltpu.VMEM((tm, C_WIDTH), BF16),
        ],
        compiler_params=_params(("parallel", "parallel")),
        name="gating_outproj",
    )(u, v, fo, x, ln_g.reshape(1, -1), ln_b.reshape(1, -1), wcat, bs_full, w_out, g4, mods)


def _router_kernel(x_ref, mod_ref, g_ref, wr_ref, tri_ref, xn_ref, meta_ref, metat_ref, cnt_ref, carry_ref, *, tm):
    first = jnp.logical_and(pl.program_id(0) == 0, pl.program_id(1) == 0)

    @pl.when(first)
    def _():
        carry_ref[...] = jnp.zeros_like(carry_ref)

    x = x_ref[0]
    h = _rms(x, g_ref[2:3, :]) * (1.0 + mod_ref[0, 4:5, :]) + mod_ref[0, 3:4, :]
    xn_ref[0] = h
    h_hi = h.astype(BF16)
    h_lo = (h - h_hi.astype(F32)).astype(BF16)
    logits = _bdot(jnp.concatenate([h_hi, h_hi, h_lo], axis=1), wr_ref[...])
    lane = lax.broadcasted_iota(jnp.int32, (tm, LANES), 1).astype(F32)
    neg = jnp.float32(-jnp.inf)
    lm = jnp.where(lane < N_EXPERTS, logits, neg)
    m1 = jnp.max(lm, axis=-1, keepdims=True)
    i1 = jnp.min(jnp.where(lm == m1, lane, float(LANES)), axis=-1, keepdims=True)
    lm2 = jnp.where(lane == i1, neg, lm)
    m2 = jnp.max(lm2, axis=-1, keepdims=True)
    i2 = jnp.min(jnp.where(lm2 == m2, lane, float(LANES)), axis=-1, keepdims=True)
    e = jnp.exp(m2 - m1)
    p1 = 1.0 / (1.0 + e)
    p2 = e / (1.0 + e)
    hot1 = (lane == i1).astype(F32)
    hot2 = (lane == i2).astype(F32)
    both = hot1 + hot2
    before = _bdot(tri_ref[...], both.astype(BF16)) + carry_ref[0:1, :]
    r1 = jnp.sum(before * hot1, axis=-1, keepdims=True)
    r2 = jnp.sum(before * hot2, axis=-1, keepdims=True)
    carry_ref[0:1, :] = carry_ref[0:1, :] + jnp.sum(both, axis=0, keepdims=True)
    cnt_ref[...] = carry_ref[...]
    vals = (i1, i2, p1, p2, r1, r2)
    meta = jnp.zeros((tm, LANES), F32)
    for k, val in enumerate(vals):
        meta = jnp.where(lane == k, val, meta)
    meta_ref[0] = meta
    metat_ref[...] = meta.T[0:8, :]


def _router(x, mods, g4, w_router):
    bsz, length, d = x.shape
    tm = min(ROW_TILE, length)
    wr = jnp.concatenate([w_router, jnp.zeros((d, LANES - N_EXPERTS), F32)], axis=1)
    wr_hi = wr.astype(BF16)
    wr_lo = (wr - wr_hi.astype(F32)).astype(BF16)
    wr = jnp.concatenate([wr_hi, wr_lo, wr_hi], axis=0)
    tri =jnp.asarray(np.tril(np.ones((tm, tm), np.float32), -1), dtype=BF16)
    const = lambda b, i: (0, 0)
    return pl.pallas_call(
        functools.partial(_router_kernel, tm=tm),
        grid=(bsz, length // tm),
        in_specs=[
            pl.BlockSpec((1, tm, d), lambda b, i: (b, i, 0)),
            pl.BlockSpec((1, 6, d), lambda b, i: (b, 0, 0)),
            pl.BlockSpec((4, d), const),
            pl.BlockSpec((3 * d, LANES), const),
            pl.BlockSpec((tm, tm), const),
        ],
        out_specs=[
            pl.BlockSpec((1, tm, d), lambda b, i: (b, i, 0)),
            pl.BlockSpec((1, tm, LANES), lambda b, i: (b, i, 0)),
            pl.BlockSpec((8, tm), lambda b, i: (0, b * (length // tm) + i)),
            pl.BlockSpec((8, LANES), const),
        ],
        out_shape=[
            jax.ShapeDtypeStruct((bsz, length, d), F32),
            jax.ShapeDtypeStruct((bsz, length, LANES), F32),
            jax.ShapeDtypeStruct((8, bsz * length), F32),
            jax.ShapeDtypeStruct((8, LANES), F32),
        ],
        scratch_shapes=[pltpu.VMEM((8, LANES), F32)],
        compiler_params=_params(("arbitrary", "arbitrary")),
        name="moe_router",
    )(x, mods, g4, wr, tri)


def _expert_kernel(te_ref, nv_ref, x_ref, w1_ref, w3_ref, w2_ref, o_ref, acc_ref):
    i = pl.program_id(0)
    j = pl.program_id(1)

    @pl.when(i < nv_ref[0])
    def _():
        y = _swiglu_halves(x_ref[...].astype(BF16), w1_ref, w3_ref, w2_ref, lead=(0,))

        @pl.when(j == 0)
        def _():
            acc_ref[...] = y

        @pl.when(j > 0)
        def _():
            acc_ref[...] += y

        @pl.when(j == pl.num_programs(1) - 1)
        def _():
            o_ref[...] = acc_ref[...]


def _experts(x_sorted, tile_expert, n_valid, w1, w3, w2):
    rows = x_sorted.shape[0]
    d = w1.shape[1]
    dff = w1.shape[2]
    tm = EXPERT_ROW_TILE
    n_ff = 2
    tf = dff // n_ff

    def row_map(i, j, te, nv):
        return (jnp.maximum(jnp.minimum(i, nv[0] - 1), 0), 0)

    def w13_map(i, j, te, nv):
        return (te[jnp.maximum(jnp.minimum(i, nv[0] - 1), 0)], 0, jnp.where(i < nv[0], j, n_ff - 1))

    def w2_map(i, j, te, nv):
        return (te[jnp.maximum(jnp.minimum(i, nv[0] - 1), 0)], jnp.where(i < nv[0], j, n_ff - 1), 0)

    return pl.pallas_call(
        _expert_kernel,
        grid_spec=pltpu.PrefetchScalarGridSpec(
            num_scalar_prefetch=2,
            grid=(rows // tm, n_ff),
            in_specs=[
                pl.BlockSpec((tm, d), row_map),
                pl.BlockSpec((1, d, tf), w13_map),
                pl.BlockSpec((1, d, tf), w13_map),
                pl.BlockSpec((1, tf, d), w2_map),
            ],
            out_specs=pl.BlockSpec((tm, d), row_map),
            scratch_shapes=[pltpu.VMEM((tm, d), F32)],
        ),
        out_shape=jax.ShapeDtypeStruct((rows, d), F32),
        compiler_params=_params(("arbitrary", "arbitrary")),
        name="expert_swiglu",
    )(tile_expert, n_valid, x_sorted, w1, w3, w2)


def _combine_kernel(x_ref, y_ref, meta_ref, g_ref, mod_ref, o_ref):
    meta = meta_ref[0]
    y = meta[:, 2:3] * y_ref[0, 0] + meta[:, 3:4] * y_ref[1, 0]
    o_ref[0] = x_ref[0] + mod_ref[0, 5:6, :] * _rms(y, g_ref[3:4, :])


def _combine(x, y2, meta, g4, mods):
    bsz, length, d = x.shape
    tm = min(ROW_TILE, length)
    return pl.pallas_call(
        _combine_kernel,
        grid=(bsz, length // tm),
        in_specs=[
            pl.BlockSpec((1, tm, d), lambda b, i: (b, i, 0)),
            pl.BlockSpec((2, 1, tm, d), lambda b, i: (0, b, i, 0)),
            pl.BlockSpec((1, tm, LANES), lambda b, i: (b, i, 0)),
            pl.BlockSpec((4, d), lambda b, i: (0, 0)),
            pl.BlockSpec((1, 6, d), lambda b, i: (b, 0, 0)),
        ],
        out_specs=pl.BlockSpec((1, tm, d), lambda b, i: (b, i, 0)),
        out_shape=jax.ShapeDtypeStruct((bsz, length, d), F32),
        compiler_params=_params(("parallel", "parallel")),
        name="moe_combine",
    )(x, y2, meta, g4, mods)


SC_CORES = 2
SC_SUBCORES = 16
SC_WORKERS = SC_CORES * SC_SUBCORES
SC_ROW_CHUNK = 32


def _sc_mesh():
    return plsc.VectorSubcoreMesh(core_axis_name="c", subcore_axis_name="s")


def _sc_dispatch(rows_in, pos2, n_out):
    tokens, width = rows_in.shape
    per_worker = tokens // SC_WORKERS
    chunk = SC_ROW_CHUNK
    n_pairs = per_worker // (2 * chunk)
    idx = pos2.reshape(2, SC_WORKERS, 2 * n_pairs, chunk)

    @functools.partial(
        pl.kernel, mesh=_sc_mesh(),
        out_type=jax.ShapeDtypeStruct((n_out, width), rows_in.dtype),
        scratch_types=[
            pltpu.VMEM((2 * n_pairs, chunk), jnp.int32),
            pltpu.VMEM((2 * n_pairs, chunk), jnp.int32),
            pltpu.VMEM((2, chunk, width), rows_in.dtype),
        ] + [pltpu.SemaphoreType.DMA] * 6,
        name="sc_dispatch",
    )
    def body(rows_hbm, idx_hbm, out_hbm, idx0_v, idx1_v, rows_v, rsem0, rsem1, wsem0, wsem1, wsem2, wsem3):
        wid = lax.axis_index("s") * SC_CORES + lax.axis_index("c")
        base = wid * per_worker
        rsem = (rsem0, rsem1)
        wsem = ((wsem0, wsem1), (wsem2, wsem3))
        pltpu.sync_copy(idx_hbm.at[0, wid], idx0_v)
        pltpu.sync_copy(idx_hbm.at[1, wid], idx1_v)

        @pl.loop(0, n_pairs)
        def _(p):
            c0 = 2 * p
            reads = [pltpu.async_copy(rows_hbm.at[pl.ds(base + (c0 + s) * chunk, chunk)], rows_v.at[s], rsem[s])
                     for s in range(2)]
            writes = []
            for s in range(2):
                reads[s].wait()
                writes.append(pltpu.async_copy(rows_v.at[s], out_hbm.at[idx0_v.at[c0 + s]], wsem[s][0]))
                writes.append(pltpu.async_copy(rows_v.at[s], out_hbm.at[idx1_v.at[c0 + s]], wsem[s][1]))
            for w in writes:
                w.wait()

    return body(rows_in, idx)


def _sc_gather(table, idx):
    n = idx.shape[0]
    width = table.shape[1]
    chunk = SC_ROW_CHUNK
    per_worker = n // SC_WORKERS
    n_pairs = per_worker // (2 * chunk)
    idx3 = idx.reshape(SC_WORKERS, 2 * n_pairs, chunk)

    @functools.partial(
        pl.kernel, mesh=_sc_mesh(),
        out_type=jax.ShapeDtypeStruct((n, width), table.dtype),
        scratch_types=[
            pltpu.VMEM((2 * n_pairs, chunk), jnp.int32),
            pltpu.VMEM((2, chunk, width), table.dtype),
            pltpu.SemaphoreType.DMA, pltpu.SemaphoreType.DMA,
            pltpu.SemaphoreType.DMA, pltpu.SemaphoreType.DMA,
        ],
        name="sc_gather",
    )
    def body(table_hbm, idx_hbm, out_hbm, idx_v, rows_v, gsem0, gsem1, wsem0, wsem1):
        wid = lax.axis_index("s") * SC_CORES + lax.axis_index("c")
        base = wid * per_worker
        gsem = (gsem0, gsem1)
        wsem = (wsem0, wsem1)
        pltpu.sync_copy(idx_hbm.at[wid], idx_v)

        @pl.loop(0, n_pairs)
        def _(p):
            c0 = 2 * p
            gathers = [pltpu.async_copy(table_hbm.at[idx_v.at[c0 + s]], rows_v.at[s], gsem[s]) for s in range(2)]
            writes = []
            for s in range(2):
                gathers[s].wait()
                rows_out = out_hbm.at[pl.ds(base + (c0 + s) * chunk, chunk)]
                writes.append(pltpu.async_copy(rows_v.at[s], rows_out, wsem[s]))
            for s in range(2):
                writes[s].wait()

    return body(table, idx3)


def _moe(x, mods, g4, w_router, w1, w3, w2):
    bsz, length, d = x.shape
    tokens = bsz * length
    tm = EXPERT_ROW_TILE
    xn, meta, meta_t, counts = _router(x, mods, g4, w_router)
    experts = meta_t[0:2].astype(jnp.int32)
    ranks = meta_t[4:6].astype(jnp.int32)
    cnt = counts[0, 0:N_EXPERTS].astype(jnp.int32)
    padded = ((cnt + tm - 1) // tm) * tm
    ends = jnp.cumsum(padded)
    starts = ends - padded
    pos = (starts[experts] + ranks).reshape(2 * tokens)
    rows = 2 * tokens + N_EXPERTS * tm
    n_tiles = rows // tm
    n_valid = (ends[-1] // tm).astype(jnp.int32).reshape(1)
    tile_start = jnp.arange(n_tiles, dtype=jnp.int32) * tm
    tile_expert = jnp.minimum(
        jnp.sum((tile_start[:, None] >= ends[None, :]).astype(jnp.int32), axis=1), N_EXPERTS - 1)
    x_sorted = _sc_dispatch(xn.reshape(tokens, d), pos.reshape(2, tokens), rows)
    y_sorted = _experts(x_sorted, tile_expert, n_valid, w1, w3, w2)
    y2 = _sc_gather(y_sorted, pos).reshape(2, bsz, length, d)
    return _combine(x, y2, meta, g4, mods)


def _grid_pos_embed(rows, d):
    row = np.repeat(np.arange(rows, dtype=np.float64), GRID_W)
    col = np.tile(np.arange(GRID_W, dtype=np.float64), rows)
    n_freq = d // 4
    omega = POS_THETA ** (-np.arange(n_freq, dtype=np.float64) / n_freq)
    ang_r = row[:, None] * omega
    ang_c = col[:, None] * omega
    return jnp.asarray(np.concatenate([np.sin(ang_r), np.cos(ang_r), np.sin(ang_c), np.cos(ang_c)], axis=-1),
                       dtype=F32)


def kernel(x, c, ctx, c_ctx, ada_w, ada_b, norm_g, ab_w_in, rg_conv_w, rg_conv_b, rg_w_r, rg_b_r, rg_w_i, rg_b_i, rg_lambda, cv_dw_w, cv_dw_b, cv_ln_g, cv_ln_b, ab_w_out, ffn_w1, ffn_w3, ffn_w2, cd_w_in, sg_ln_g, sg_ln_b, sg_w_s, sg_b_s, cd_w_out, moe_router, moe_w1, moe_w3, moe_w2):
    bsz, length, d = x.shape
    assert ada_w.shape[0] == 2, "two layers: one even (RG-LRU | Conformer), one odd (gMLP | Fourier)"
    pos = _grid_pos_embed(length // GRID_W, d)

    cpad = jnp.concatenate([c, c_ctx[None, :], jnp.zeros((16 - bsz - 1, d), F32)], axis=0)
    m = _ada(cpad, ada_w, ada_b)
    mods = m[:, :bsz].reshape(2, bsz, 6, d)
    mods_ctx = m[:, bsz:bsz + 1].reshape(2, 1, 6, d)

    mods_t = mods[0].transpose(1, 0, 2)
    mods_ctx_t = jnp.broadcast_to(mods_ctx[0].reshape(6, 1, d), (6, bsz, d))
    pos3 = pos[:, None, :]
    w_in = ab_w_in[0]
    xa3, gg3, glu3, x0_t, ffn1b, ffn3b, ffn2b = _inproj_t(
        x, pos3, mods_t, norm_g[0], w_in, rg_conv_w[0], rg_conv_b[0], True, riders=(ffn_w1[0], ffn_w3[0], ffn_w2[0]))
    (xc3,) = _inproj_t(ctx, None, mods_ctx_t, norm_g[0], w_in[:, 0:A_WIDTH], rg_conv_w[0], rg_conv_b[0], False)
    wg, bg, sp = _gate_weights(rg_w_r[0], rg_b_r[0], rg_w_i[0], rg_b_i[0], rg_lambda[0])
    n_exp, _, dff_e = moe_w1[0].shape
    rec_hi, rec_lo, w2b = _scan_t(xa3, xc3, gg3, wg, bg, sp, riders=(moe_w2[0].reshape(n_exp * dff_e, d),))
    u3, w1b = _conv_t(glu3, cv_dw_w[0], cv_dw_b[0], cv_ln_g[0], cv_ln_b[0],
                      riders=(moe_w1[0].reshape(n_exp * d, dff_e),))
    x1_t = _outproj_t(rec_lo, rec_hi, u3, x0_t, ab_w_out[0], norm_g[0], mods_t)
    x2, w3b = _ffn_t(x1_t, mods_t, norm_g[0], ffn1b, ffn3b, ffn2b, riders=(moe_w3[0].reshape(n_exp * d, dff_e),))

    u, v, f = _inproj_cd(x2, mods[1], norm_g[1], cd_w_in[0])
    fo = _fourier(f)
    x3 = _gate_out(u, v, fo, x2, sg_ln_g[0], sg_ln_b[0], sg_w_s[0], sg_b_s[0], cd_w_out[0],
                   norm_g[1], mods[1])
    return _moe(x3, mods[1], norm_g[1], moe_router[0], w1b.reshape(n_exp, d, dff_e), w3b.reshape(n_exp, d, dff_e),
                w2b.reshape(n_exp, dff_e, d))
```

```python
import functools

import numpy as np
import jax
import jax.numpy as jnp
from jax import lax
from jax.experimental import pallas as pl
from jax.experimental.pallas import tpu as pltpu
from jax.experimental.pallas import tpu_sc as plsc

F32 = jnp.float32
BF16 = jnp.bfloat16

GRID_W = 64
POS_THETA = 10000.0
NORM_EPS = 1e-6
LN_EPS = 1e-5
LRU_C = 8.0

A_WIDTH = 512
A_HEADS = 8
A_CONV_W = 4
B_WIDTH = 512
B_CONV_W = 31
C_WIDTH = 512
C_HEADS = 8
CHUNK = 128
D_WIDTH = 512
D_GROUPS = 4
D_GROUP_DIM = D_WIDTH // D_GROUPS
N_EXPERTS = 8

LANES = 128
ROW_TILE = 512
EXPERT_ROW_TILE = 512
VMEM_LIMIT = 56 * 2 ** 20


def _params(sem):
    return pltpu.CompilerParams(dimension_semantics=sem, vmem_limit_bytes=VMEM_LIMIT)


def _rms(x, g):
    return x * lax.rsqrt(jnp.mean(x * x, axis=-1, keepdims=True) + NORM_EPS) * g


def _bdot(a, b):
    return jnp.dot(a, b, preferred_element_type=F32)


MXU_WIDTH = 256


def _swiglu_halves(x, w1_ref, w3_ref, w2_ref, lead=()):
    n = w1_ref.shape[-1]
    half = -(-(n // 2) // MXU_WIDTH) * MXU_WIDTH
    y = None
    for lo, hi in ((0, half), (half, n)):
        a = _bdot(x, w1_ref[lead + (slice(None), slice(lo, hi))])
        b = _bdot(x, w3_ref[lead + (slice(None), slice(lo, hi))])
        t = (jax.nn.silu(a) * b).astype(BF16)
        part = _bdot(t, w2_ref[lead + (slice(lo, hi), slice(None))])
        y = part if y is None else y + part
    return y


def _ada_kernel(c_ref, w_ref, b_ref, o_ref):
    s = jax.nn.silu(c_ref[...])
    o_ref[0] = _bdot(s.astype(BF16), w_ref[0].astype(BF16)) + b_ref[0]


def _ada(cpad, ada_w, ada_b):
    nl, d, n6 = ada_w.shape
    rows = cpad.shape[0]
    tn = n6 // 4
    return pl.pallas_call(
        _ada_kernel,
        grid=(nl, n6 // tn),
        in_specs=[
            pl.BlockSpec((rows, d), lambda l, j: (0, 0)),
            pl.BlockSpec((1, d, tn), lambda l, j: (l, 0, j)),
            pl.BlockSpec((1, 1, tn), lambda l, j: (l, 0, j)),
        ],
        out_specs=pl.BlockSpec((1, rows, tn), lambda l, j: (l, 0, j)),
        out_shape=jax.ShapeDtypeStruct((nl, rows, n6), F32),
        compiler_params=_params(("arbitrary", "arbitrary")),
        name="ada_terms",
    )(cpad, ada_w, ada_b.reshape(nl, 1, n6))


TIME_TILE = 128
SCAN_TIME_TILE = 128
SCAN_LANES = 256
CONV_TIME_TILE = 64
CONV_HALO = 16


def _time_major(ref):
    return jnp.swapaxes(ref[...], 0, 1)


def _inproj_t_kernel(*refs, tt, add_pos, branches, n_cast):
    if n_cast:
        n_in = len(refs) - n_cast - (4 if branches else 1) - n_cast
        _cast_riders(refs[n_in:n_in + n_cast], refs[len(refs) - n_cast:])
        refs = refs[:n_in] + refs[n_in + n_cast:len(refs) - n_cast]
    if add_pos:
        x_ref, xp_ref, xn_ref, pos_ref, posp_ref, posn_ref, mod_ref, g_ref, w_ref, cw_ref, cb_ref, *outs = refs
        xall = jnp.concatenate(
            [_time_major(xp_ref)[7:8] + posp_ref[...], _time_major(x_ref) + pos_ref[...],
             _time_major(xn_ref)[0:2] + posn_ref[...]], axis=0)
    else:
        x_ref, xp_ref, xn_ref, mod_ref, g_ref, w_ref, cw_ref, cb_ref, *outs = refs
        xall = jnp.concatenate([_time_major(xp_ref)[7:8], _time_major(x_ref), _time_major(xn_ref)[0:2]], axis=0)
    i = pl.program_id(0)
    last = pl.num_programs(0) - 1
    steps, bsz, d = xall.shape
    h = _rms(xall, g_ref[0:1, :]) * (1.0 + mod_ref[1]) + mod_ref[0]
    z = _bdot(h.reshape(steps * bsz, d).astype(BF16), w_ref[...].astype(BF16))
    z3 = z.reshape(steps, bsz, z.shape[-1])
    t_idx = lax.broadcasted_iota(jnp.int32, (steps, bsz, A_WIDTH), 0)
    inside = jnp.logical_and(jnp.logical_or(t_idx >= 1, i > 0), jnp.logical_or(t_idx <= tt, i < last))
    xa = jnp.where(inside, z3[:, :, 0:A_WIDTH], 0.0)
    xconv = cb_ref[...] + cw_ref[0:1, :] * xa[0:tt]
    for k in range(1, A_CONV_W):
        xconv = xconv + cw_ref[k:k + 1, :] * xa[k:k + tt]
    outs[0][...] = xconv
    if branches:
        outs[1][...] = jax.nn.gelu(z3[1:tt + 1, :, A_WIDTH:2 * A_WIDTH])
        vb = z3[1:tt + 1, :, 2 * A_WIDTH:2 * A_WIDTH + B_WIDTH]
        gb = z3[1:tt + 1, :, 2 * A_WIDTH + B_WIDTH:]
        outs[2][...] = vb * jax.nn.sigmoid(gb)
        outs[3][...] = xall[1:tt + 1]


def _inproj_t(x, pos3, mods_t, g4, w, conv_w, conv_b, branches, riders=()):
    bsz, length, d = x.shape
    n = w.shape[1]
    tt = min(TIME_TILE, length)
    add_pos = pos3 is not None
    n_out = 3 if branches else 1
    prev_map = lambda i: (jnp.maximum(i * tt - 1, 0), 0, 0)
    next_map = lambda i: (jnp.minimum((i + 1) * (tt // 2), length // 2 - 1), 0, 0)
    in_specs = [
        pl.BlockSpec((bsz, tt, d), lambda i: (0, i, 0)),
        pl.BlockSpec((bsz, 8, d), lambda i: (0, jnp.maximum(i * (tt // 8) - 1, 0), 0)),
        pl.BlockSpec((bsz, 8, d), lambda i: (0, jnp.minimum((i + 1) * (tt // 8), length // 8 - 1), 0)),
    ]
    args = [x, x, x]
    if add_pos:
        in_specs += [
            pl.BlockSpec((tt, 1, d), lambda i: (i, 0, 0)),
            pl.BlockSpec((1, 1, d), prev_map),
            pl.BlockSpec((2, 1, d), next_map),
        ]
        args += [pos3, pos3, pos3]
    in_specs += [
        pl.BlockSpec((6, bsz, d), lambda i: (0, 0, 0)),
        pl.BlockSpec((4, d), lambda i: (0, 0)),
        pl.BlockSpec((d, n), lambda i: (0, 0)),
        pl.BlockSpec((A_CONV_W, A_WIDTH), lambda i: (0, 0)),
        pl.BlockSpec((1, A_WIDTH), lambda i: (0, 0)),
    ]
    args += [mods_t, g4, w, conv_w, conv_b.reshape(1, A_WIDTH)]
    r_in, r_out, r_shapes = _rider_specs(riders, length // tt, lambda i: i)
    out_specs = [pl.BlockSpec((tt, bsz, A_WIDTH), lambda i: (i, 0, 0))] * n_out
    out_shapes = [jax.ShapeDtypeStruct((length, bsz, A_WIDTH), F32)] * n_out
    if branches:
        out_specs.append(pl.BlockSpec((tt, bsz, d), lambda i: (i, 0, 0)))
        out_shapes.append(jax.ShapeDtypeStruct((length, bsz, d), F32))
    return pl.pallas_call(
        functools.partial(_inproj_t_kernel, tt=tt, add_pos=add_pos, branches=branches, n_cast=len(riders)),
        grid=(length // tt,),
        in_specs=in_specs + r_in,
        out_specs=out_specs + r_out,
        out_shape=out_shapes + r_shapes,
        compiler_params=_params(("parallel",)),
        name="inproj_ab" if branches else "inproj_ctx",
    )(*args, *riders)


def _cast_riders(cast_in, cast_out):
    for src, dst in zip(cast_in, cast_out):
        dst[...] = src[...].astype(BF16)


def _rider_specs(arrays, n_steps, step_of):
    in_specs, out_specs, out_shapes = [], [], []
    for arr in arrays:
        rows, cols = arr.shape
        block = (rows // n_steps, cols)
        in_specs.append(pl.BlockSpec(block, lambda *idx: (step_of(*idx), 0)))
        out_specs.append(pl.BlockSpec(block, lambda *idx: (step_of(*idx), 0)))
        out_shapes.append(jax.ShapeDtypeStruct(arr.shape, BF16))
    return in_specs, out_specs, out_shapes


def _scan_t_kernel(*refs, tb, nb, ctx_len, n_cast):
    xf_ref, xb_ref, ggf_ref, ggb_ref, xc_ref, wg_ref, bg_ref, sp_ref = refs[:8]
    cast_in = refs[8:8 + n_cast]
    hi_ref, lo_ref = refs[8 + n_cast:10 + n_cast]
    cast_out = refs[10 + n_cast:10 + 2 * n_cast]
    a_ref, b_ref, h_ref, state_ref = refs[10 + 2 * n_cast:]
    _cast_riders(cast_in, cast_out)
    k = pl.program_id(1)
    half = nb // 2
    bsz, lanes = xf_ref.shape[1], xf_ref.shape[2]
    chunk = 128

    def coefficients(x_ref, n, direction):
        def body(c, _):
            t0 = pl.multiple_of(c * chunk, chunk)
            x = x_ref[pl.ds(t0, chunk)].reshape(chunk * bsz, lanes)
            th = jnp.tanh(_bdot(x.astype(BF16), wg_ref[0, direction]) + bg_ref[0, direction])
            i = 0.5 * th[:, lanes:] + 0.5
            c = (-0.5 * LRU_C) * sp_ref[0, direction:direction + 1, :]
            log_a = c * th[:, 0:lanes] + c
            a = jnp.exp(log_a)
            one_minus_a2 = -jnp.tanh(log_a) * (a * a + 1.0)
            a_ref[direction, pl.ds(t0, chunk)] = a.reshape(chunk, bsz, lanes)
            b_ref[direction, pl.ds(t0, chunk)] = (jnp.sqrt(one_minus_a2) * (i * x)).reshape(chunk, bsz, lanes)
            return 0
        lax.fori_loop(0, n // chunk, body, 0)

    def sweep(n, store):
        def body(t, carry):
            hf, hb = carry
            tr = n - 1 - t
            hf = a_ref[0, pl.ds(t, 1)][0] * hf + b_ref[0, pl.ds(t, 1)][0]
            hb = a_ref[1, pl.ds(tr, 1)][0] * hb + b_ref[1, pl.ds(tr, 1)][0]
            if store:
                b_ref[0, pl.ds(t, 1)] = hf[None]
                b_ref[1, pl.ds(tr, 1)] = hb[None]
            return hf, hb
        hf, hb = lax.fori_loop(0, n, body, (state_ref[0], state_ref[1]), unroll=8)
        state_ref[0] = hf
        state_ref[1] = hb

    @pl.when(k == 0)
    def _():
        state_ref[...] = jnp.zeros_like(state_ref)
        coefficients(xc_ref, ctx_len, 0)
        coefficients(xc_ref, ctx_len, 1)
        sweep(ctx_len, False)

    coefficients(xf_ref, tb, 0)
    coefficients(xb_ref, tb, 1)
    sweep(tb, True)
    m = nb - 1 - k

    @pl.when(k < half)
    def _():
        h_ref[k] = b_ref[0, 0:tb]
        h_ref[m] = b_ref[1, 0:tb]

    @pl.when(k >= half)
    def _():
        hi_ref[...] = (b_ref[0, 0:tb] + h_ref[k]) * ggf_ref[...]
        lo_ref[...] = (h_ref[m] + b_ref[1, 0:tb]) * ggb_ref[...]


def _scan_t(xa3, xc3, gg3, wg, bg, sp, riders=()):
    length, bsz, width = xa3.shape
    ctx_len = xc3.shape[0]
    tb = min(SCAN_TIME_TILE, length // 2)
    nb = length // tb
    half = nb // 2
    lanes = SCAN_LANES
    blk = (tb, bsz, lanes)
    r_in, r_out, r_shapes = _rider_specs(riders, (width // lanes) * nb, lambda g, k: g * nb + k)
    return pl.pallas_call(
        functools.partial(_scan_t_kernel, tb=tb, nb=nb, ctx_len=ctx_len, n_cast=len(riders)),
        grid=(width // lanes, nb),
        in_specs=[
            pl.BlockSpec(blk, lambda g, k: (k, 0, g)),
            pl.BlockSpec(blk, lambda g, k: (nb - 1 - k, 0, g)),
            pl.BlockSpec(blk, lambda g, k: (jnp.maximum(k, half), 0, g)),
            pl.BlockSpec(blk, lambda g, k: (jnp.minimum(nb - 1 - k, half - 1), 0, g)),
            pl.BlockSpec((ctx_len, bsz, lanes), lambda g, k: (0, 0, g)),
            pl.BlockSpec((1, 2, lanes, 2 * lanes), lambda g, k: (g, 0, 0, 0)),
            pl.BlockSpec((1, 2, 1, 2 * lanes), lambda g, k: (g, 0, 0, 0)),
            pl.BlockSpec((1, 2, lanes), lambda g, k: (g, 0, 0)),
        ] + r_in,
        out_specs=[
            pl.BlockSpec(blk, lambda g, k: (jnp.maximum(k - half, 0), 0, g)),
            pl.BlockSpec(blk, lambda g, k: (jnp.minimum(nb - 1 - k, half - 1), 0, g)),
        ] + r_out,
        out_shape=[jax.ShapeDtypeStruct((length // 2, bsz, width), F32)] * 2 + r_shapes,
        scratch_shapes=[
            pltpu.VMEM((2, max(tb, ctx_len), bsz, lanes), F32),
            pltpu.VMEM((2, max(tb, ctx_len), bsz, lanes), F32),
            pltpu.VMEM((nb, tb, bsz, lanes), F32),
            pltpu.VMEM((2, bsz, lanes), F32),
        ],
        compiler_params=_params(("parallel", "arbitrary")),
        name="rglru_scan",
    )(xa3, xa3, gg3, gg3, xc3, wg, bg, sp, *riders)


def _gate_weights(w_r, b_r, w_i, b_i, lam):
    hd = A_WIDTH // A_HEADS
    heads_per_group = SCAN_LANES // hd
    ngroups = A_WIDTH // SCAN_LANES

    def blockdiag(w):
        w = w.reshape(ngroups, heads_per_group, hd, hd)
        eye = jnp.eye(heads_per_group, dtype=w.dtype)
        return jnp.einsum("ghde,hk->ghdke", w, eye).reshape(ngroups, SCAN_LANES, SCAN_LANES)

    wg = jnp.stack([jnp.concatenate([blockdiag(w_r[d]), blockdiag(w_i[d])], axis=-1) for d in range(2)], axis=1)
    bg = jnp.stack([jnp.concatenate([b_r[d].reshape(ngroups, 1, SCAN_LANES), b_i[d].reshape(ngroups, 1, SCAN_LANES)],
                                    axis=-1) for d in range(2)], axis=1)
    sp = jax.nn.softplus(-lam.astype(F32)).reshape(2, ngroups, SCAN_LANES).transpose(1, 0, 2)
    return (0.5 * wg).astype(BF16), 0.5 * bg, sp


def _conv_t_kernel(*refs, tb, n_cast):
    x_ref, xp_ref, xn_ref, w_ref, b_ref, lng_ref, lnb_ref = refs[:7]
    cast_in = refs[7:7 + n_cast]
    o_ref = refs[7 + n_cast]
    cast_out = refs[8 + n_cast:8 + 2 * n_cast]
    stage_ref = refs[-1]
    _cast_riders(cast_in, cast_out)
    i = pl.program_id(0)
    last = pl.num_programs(0) - 1
    halo = CONV_HALO
    stage_ref[0:halo] = jnp.where(i > 0, xp_ref[...], 0.0)
    stage_ref[halo:halo + tb] = x_ref[...]
    stage_ref[halo + tb:2 * halo + tb] = jnp.where(i < last, xn_ref[...], 0.0)
    sub = 8

    def body(r, _):
        t0 = pl.multiple_of(r * sub, sub)
        def tap(k, acc):
            return acc + w_ref[k] * stage_ref[pl.ds(t0 + 1 + k, sub)]
        acc = lax.fori_loop(0, B_CONV_W + 1, tap, jnp.broadcast_to(b_ref[...], (sub,) + b_ref.shape), unroll=8)
        o_ref[pl.ds(t0, sub)] = acc
        return 0
    lax.fori_loop(0, tb // sub, body, 0)

    norm_rows = 64

    def norm_body(r, _):
        t0 = pl.multiple_of(r * norm_rows, norm_rows)
        acc = o_ref[pl.ds(t0, norm_rows)]
        mu = jnp.mean(acc, axis=-1, keepdims=True)
        cen = acc - mu
        var = jnp.mean(cen * cen, axis=-1, keepdims=True)
        y = cen * lax.rsqrt(var + LN_EPS) * lng_ref[...] + lnb_ref[...]
        o_ref[pl.ds(t0, norm_rows)] = jax.nn.silu(y)
        return 0
    lax.fori_loop(0, tb // norm_rows, norm_body, 0)


def _conv_t(glu3, dw_w, dw_b, ln_g, ln_b, riders=()):
    length, bsz, width = glu3.shape
    tb = min(CONV_TIME_TILE, length)
    halo = CONV_HALO
    taps = dw_w.shape[0]
    w8 = jnp.broadcast_to(jnp.concatenate([dw_w, jnp.zeros((32 - taps, width), F32)], axis=0)[:, None, :],
                          (32, bsz, width))
    b8 = jnp.broadcast_to(dw_b[None, :], (bsz, width))
    r_in, r_out, r_shapes = _rider_specs(riders, length // tb, lambda i: i)
    return pl.pallas_call(
        functools.partial(_conv_t_kernel, tb=tb, n_cast=len(riders)),
        grid=(length // tb,),
        in_specs=[
            pl.BlockSpec((tb, bsz, width), lambda i: (i, 0, 0)),
            pl.BlockSpec((halo, bsz, width), lambda i: (jnp.maximum(i * (tb // halo) - 1, 0), 0, 0)),
            pl.BlockSpec((halo, bsz, width), lambda i: (jnp.minimum((i + 1) * (tb // halo), length // halo - 1), 0, 0)),
            pl.BlockSpec((32, bsz, width), lambda i: (0, 0, 0)),
            pl.BlockSpec((bsz, width), lambda i: (0, 0)),
            pl.BlockSpec((1, width), lambda i: (0, 0)),
            pl.BlockSpec((1, width), lambda i: (0, 0)),
        ] + r_in,
        out_specs=[pl.BlockSpec((tb, bsz, width), lambda i: (i, 0, 0))] + r_out,
        out_shape=[jax.ShapeDtypeStruct((length, bsz, width), F32)] + r_shapes,
        scratch_shapes=[pltpu.VMEM((tb + 2 * halo, bsz, width), F32)],
        compiler_params=_params(("parallel",)),
        name="conformer_conv",
    )(glu3, glu3, glu3, w8, b8, ln_g.reshape(1, -1), ln_b.reshape(1, -1), *riders)


def _outproj_t_kernel(lo_ref, hi_ref, u_ref, x_ref, wout_ref, g_ref, mod_ref, o_ref, *, half_steps):
    i = pl.program_id(0)
    tt, bsz, d = x_ref.shape
    rec = jnp.where(i < half_steps, lo_ref[...], hi_ref[...])
    rows = tt * bsz
    y = (_bdot(rec.reshape(rows, A_WIDTH).astype(BF16), wout_ref[0:A_WIDTH, :].astype(BF16))
         + _bdot(u_ref[...].reshape(rows, B_WIDTH).astype(BF16), wout_ref[A_WIDTH:, :].astype(BF16)))
    yn = _rms(y, g_ref[1:2, :]).reshape(tt, bsz, d)
    o_ref[...] = x_ref[...] + mod_ref[2] * yn


def _outproj_t(rec_lo, rec_hi, u3, x0_t, w_out, g4, mods_t):
    length, bsz, d = x0_t.shape
    tt = min(TIME_TILE // 2, length // 2)
    half_steps = (length // 2) // tt
    return pl.pallas_call(
        functools.partial(_outproj_t_kernel, half_steps=half_steps),
        grid=(length // tt,),
        in_specs=[
            pl.BlockSpec((tt, bsz, A_WIDTH), lambda i: (jnp.minimum(i, half_steps - 1), 0, 0)),
            pl.BlockSpec((tt, bsz, A_WIDTH), lambda i: (jnp.maximum(i - half_steps, 0), 0, 0)),
            pl.BlockSpec((tt, bsz, B_WIDTH), lambda i: (i, 0, 0)),
            pl.BlockSpec((tt, bsz, d), lambda i: (i, 0, 0)),
            pl.BlockSpec((A_WIDTH + B_WIDTH, d), lambda i: (0, 0)),
            pl.BlockSpec((4, d), lambda i: (0, 0)),
            pl.BlockSpec((6, bsz, d), lambda i: (0, 0, 0)),
        ],
        out_specs=pl.BlockSpec((tt, bsz, d), lambda i: (i, 0, 0)),
        out_shape=jax.ShapeDtypeStruct((length, bsz, d), F32),
        compiler_params=_params(("parallel",)),
        name="mixer_ab_outproj",
    )(rec_lo, rec_hi, u3, x0_t, w_out, g4, mods_t)


def _ffn_t_kernel(x_ref, mod_ref, g_ref, w1_ref, w3_ref, w2_ref, *rest, n_cast):
    _cast_riders(rest[:n_cast], rest[n_cast + 1:])
    o_ref = rest[n_cast]
    x = x_ref[...]
    tt, bsz, d = x.shape
    h = (_rms(x, g_ref[2:3, :]) * (1.0 + mod_ref[4]) + mod_ref[3]).reshape(tt * bsz, d).astype(BF16)
    y = _swiglu_halves(h, w1_ref, w3_ref, w2_ref)
    o_ref[...] = jnp.swapaxes(x + mod_ref[5] * _rms(y, g_ref[3:4, :]).reshape(tt, bsz, d), 0, 1)


def _ffn_t(x3, mods_t, g4, w1, w3, w2, riders=()):
    length, bsz, d = x3.shape
    dff = w1.shape[1]
    tt = min(TIME_TILE // 2, length)
    r_in, r_out, r_shapes = _rider_specs(riders, length // tt, lambda i: i)
    return pl.pallas_call(
        functools.partial(_ffn_t_kernel, n_cast=len(riders)),
        grid=(length // tt,),
        in_specs=[
            pl.BlockSpec((tt, bsz, d), lambda i: (i, 0, 0)),
            pl.BlockSpec((6, bsz, d), lambda i: (0, 0, 0)),
            pl.BlockSpec((4, d), lambda i: (0, 0)),
            pl.BlockSpec((d, dff), lambda i: (0, 0), pipeline_mode=pl.Buffered(1)),
            pl.BlockSpec((d, dff), lambda i: (0, 0), pipeline_mode=pl.Buffered(1)),
            pl.BlockSpec((dff, d), lambda i: (0, 0), pipeline_mode=pl.Buffered(1)),
        ] + r_in,
        out_specs=[pl.BlockSpec((bsz, tt, d), lambda i: (0, i, 0))] + r_out,
        out_shape=[jax.ShapeDtypeStruct((bsz, length, d), F32)] + r_shapes,
        compiler_params=_params(("parallel",)),
        name="dense_swiglu",
    )(x3, mods_t, g4, w1, w3, w2, *riders)


def _inproj_cd_kernel(x_ref, mod_ref, g_ref, w_ref, u_ref, v_ref, f_ref):
    x = x_ref[0]
    h = _rms(x, g_ref[0:1, :]) * (1.0 + mod_ref[0, 1:2, :]) + mod_ref[0, 0:1, :]
    z = _bdot(h.astype(BF16), w_ref[...].astype(BF16))
    u_ref[0] = jax.nn.gelu(z[:, 0:C_WIDTH])
    v_ref[0] = jax.nn.gelu(z[:, C_WIDTH:2 * C_WIDTH])
    f_ref[0] = z[:, 2 * C_WIDTH:]


def _inproj_cd(x, mods, g4, w):
    bsz, length, d = x.shape
    n = w.shape[1]
    tm = min(ROW_TILE, length)
    return pl.pallas_call(
        _inproj_cd_kernel,
        grid=(bsz, length // tm),
        in_specs=[
            pl.BlockSpec((1, tm, d), lambda b, i: (b, i, 0)),
            pl.BlockSpec((1, 6, d), lambda b, i: (b, 0, 0)),
            pl.BlockSpec((4, d), lambda b, i: (0, 0)),
            pl.BlockSpec((d, n), lambda b, i: (0, 0)),
        ],
        out_specs=[pl.BlockSpec((1, tm, C_WIDTH), lambda b, i: (b, i, 0))] * 3,
        out_shape=[jax.ShapeDtypeStruct((bsz, length, C_WIDTH), F32)] * 3,
        compiler_params=_params(("parallel", "parallel")),
        name="inproj_cd",
    )(x, mods, g4, w)


def _fourier_kernel(f_ref, dmat_ref, lmat_ref, o_ref, z_ref, *, length, scale):
    j = pl.program_id(1)
    rows_per_step = 512 if length % 512 == 0 else length

    @pl.when(j == 0)
    def _():
        def body(c, _):
            rows = pl.ds(pl.multiple_of(c * rows_per_step, rows_per_step), rows_per_step)
            fb = f_ref[0, rows, :].astype(BF16)
            for g in range(D_GROUPS):
                cs = _bdot(fb[:, g * D_GROUP_DIM:(g + 1) * D_GROUP_DIM], dmat_ref[...])
                z_ref[0, rows, g * D_GROUP_DIM:(g + 1) * D_GROUP_DIM] = cs[:, 0:D_GROUP_DIM].astype(BF16)
                z_ref[1, rows, g * D_GROUP_DIM:(g + 1) * D_GROUP_DIM] = cs[:, D_GROUP_DIM:].astype(BF16)
            return 0
        lax.fori_loop(0, length // rows_per_step, body, 0)

    out = _bdot(lmat_ref[0], z_ref[0]) + _bdot(lmat_ref[1], z_ref[1])
    o_ref[0] = (out * scale).astype(BF16)


def _dft_matrices(length):
    k = np.arange(length, dtype=np.int64)
    ang_l = 2.0 * np.pi * ((k[:, None] * k[None, :]) % length).astype(np.float64) / length
    lmat = np.stack([np.cos(ang_l), -np.sin(ang_l)], axis=0)
    d = np.arange(D_GROUP_DIM, dtype=np.int64)
    ang_d = 2.0 * np.pi * ((d[:, None] * d[None, :]) % D_GROUP_DIM).astype(np.float64) / D_GROUP_DIM
    dmat = np.concatenate([np.cos(ang_d), np.sin(ang_d)], axis=1)
    return jnp.asarray(lmat, dtype=F32).astype(BF16), jnp.asarray(dmat, dtype=F32).astype(BF16)


def _fourier(f):
    bsz, length, width = f.shape
    tm = min(ROW_TILE, length)
    lmat, dmat = _dft_matrices(length)
    scale = float(1.0 / np.sqrt(length * D_GROUP_DIM))
    return pl.pallas_call(
        functools.partial(_fourier_kernel, length=length, scale=scale),
        grid=(bsz, length // tm),
        in_specs=[
            pl.BlockSpec((1, length, width), lambda b, i: (b, 0, 0)),
            pl.BlockSpec((D_GROUP_DIM, 2 * D_GROUP_DIM), lambda b, i: (0, 0)),
            pl.BlockSpec((2, tm, length), lambda b, i: (0, i, 0)),
        ],
        out_specs=pl.BlockSpec((1, tm, width), lambda b, i: (b, i, 0)),
        out_shape=jax.ShapeDtypeStruct((bsz, length, width), BF16),
        scratch_shapes=[pltpu.VMEM((2, length, width), BF16)],
        compiler_params=_params(("parallel", "arbitrary")),
        name="fourier_mix",
    )(f, dmat, lmat)


def _gate_out_kernel(u_ref, v_ref, fo_ref, x_ref, lng_ref, lnb_ref, wcat_ref, bs_ref, wout_ref, g_ref, mod_ref,
                     o_ref, vs_ref, gt_ref, *, tm):
    head_dim = C_WIDTH // C_HEADS
    head_of_lane = lax.broadcasted_iota(jnp.int32, (CHUNK, C_WIDTH), 1) // head_dim
    for ci in range(tm // CHUNK):
        rows = slice(ci * CHUNK, (ci + 1) * CHUNK)
        v = v_ref[0, rows, :]
        mu = jnp.mean(v, axis=-1, keepdims=True)
        cen = v - mu
        var = jnp.mean(cen * cen, axis=-1, keepdims=True)
        vn = (cen * lax.rsqrt(var + LN_EPS) * lng_ref[...] + lnb_ref[...]).astype(BF16)
        for h in range(C_HEADS):
            vs_ref[h * CHUNK:(h + 1) * CHUNK, :] = jnp.where(head_of_lane == h, vn, jnp.zeros_like(vn))
        mixed = _bdot(wcat_ref[...], vs_ref[...]) + bs_ref[...]
        gt_ref[rows, :] = (u_ref[0, rows, :] * mixed).astype(BF16)
    y = (_bdot(gt_ref[...], wout_ref[0:C_WIDTH, :].astype(BF16))
         + _bdot(fo_ref[0], wout_ref[C_WIDTH:, :].astype(BF16)))
    o_ref[0] = x_ref[0] + mod_ref[0, 2:3, :] * _rms(y, g_ref[1:2, :])


def _gate_out(u, v, fo, x, ln_g, ln_b, w_s, b_s, w_out, g4, mods):
    bsz, length, d = x.shape
    tm = min(ROW_TILE, length)
    head_dim = C_WIDTH // C_HEADS
    wcat = w_s.transpose(1, 0, 2).reshape(CHUNK, C_HEADS * CHUNK).astype(BF16)
    bs_full = jnp.repeat(b_s.T, head_dim, axis=1)
    const = lambda b, i: (0, 0)
    return pl.pallas_call(
        functools.partial(_gate_out_kernel, tm=tm),
        grid=(bsz, length // tm),
        in_specs=[
            pl.BlockSpec((1, tm, C_WIDTH), lambda b, i: (b, i, 0)),
            pl.BlockSpec((1, tm, C_WIDTH), lambda b, i: (b, i, 0)),
            pl.BlockSpec((1, tm, D_WIDTH), lambda b, i: (b, i, 0)),
            pl.BlockSpec((1, tm, d), lambda b, i: (b, i, 0)),
            pl.BlockSpec((1, C_WIDTH), const),
            pl.BlockSpec((1, C_WIDTH), const),
            pl.BlockSpec((CHUNK, C_HEADS * CHUNK), const),
            pl.BlockSpec((CHUNK, C_WIDTH), const),
            pl.BlockSpec((C_WIDTH + D_WIDTH, d), const),
            pl.BlockSpec((4, d), const),
            pl.BlockSpec((1, 6, d), lambda b, i: (b, 0, 0)),
        ],
        out_specs=pl.BlockSpec((1, tm, d), lambda b, i: (b, i, 0)),
        out_shape=jax.ShapeDtypeStruct((bsz, length, d), F32),
        scratch_shapes=[
            pltpu.VMEM((C_HEADS * CHUNK, C_WIDTH), BF16),
            pltpu.VMEM((tm, C_WIDTH), BF16),
        ],
        compiler_params=_params(("parallel", "parallel")),
        name="gating_outproj",
    )(u, v, fo, x, ln_g.reshape(1, -1), ln_b.reshape(1, -1), wcat, bs_full, w_out, g4, mods)


def _router_kernel(x_ref, mod_ref, g_ref, wr_ref, tri_ref, xn_ref, meta_ref, metat_ref, cnt_ref, carry_ref, *, tm):
    first = jnp.logical_and(pl.program_id(0) == 0, pl.program_id(1) == 0)

    @pl.when(first)
    def _():
        carry_ref[...] = jnp.zeros_like(carry_ref)

    x = x_ref[0]
    h = _rms(x, g_ref[2:3, :]) * (1.0 + mod_ref[0, 4:5, :]) + mod_ref[0, 3:4, :]
    xn_ref[0] = h
    h_hi = h.astype(BF16)
    h_lo = (h - h_hi.astype(F32)).astype(BF16)
    logits = _bdot(jnp.concatenate([h_hi, h_hi, h_lo], axis=1), wr_ref[...])
    lane = lax.broadcasted_iota(jnp.int32, (tm, LANES), 1).astype(F32)
    neg = jnp.float32(-jnp.inf)
    lm = jnp.where(lane < N_EXPERTS, logits, neg)
    m1 = jnp.max(lm, axis=-1, keepdims=True)
    i1 = jnp.min(jnp.where(lm == m1, lane, float(LANES)), axis=-1, keepdims=True)
    lm2 = jnp.where(lane == i1, neg, lm)
    m2 = jnp.max(lm2, axis=-1, keepdims=True)
    i2 = jnp.min(jnp.where(lm2 == m2, lane, float(LANES)), axis=-1, keepdims=True)
    e = jnp.exp(m2 - m1)
    p1 = 1.0 / (1.0 + e)
    p2 = e / (1.0 + e)
    hot1 = (lane == i1).astype(F32)
    hot2 = (lane == i2).astype(F32)
    both = hot1 + hot2
    before = _bdot(tri_ref[...], both.astype(BF16)) + carry_ref[0:1, :]
    r1 = jnp.sum(before * hot1, axis=-1, keepdims=True)
    r2 = jnp.sum(before * hot2, axis=-1, keepdims=True)
    carry_ref[0:1, :] = carry_ref[0:1, :] + jnp.sum(both, axis=0, keepdims=True)
    cnt_ref[...] = carry_ref[...]
    vals = (i1, i2, p1, p2, r1, r2)
    meta = jnp.zeros((tm, LANES), F32)
    for k, val in enumerate(vals):
        meta = jnp.where(lane == k, val, meta)
    meta_ref[0] = meta
    metat_ref[...] = meta.T[0:8, :]


def _router(x, mods, g4, w_router):
    bsz, length, d = x.shape
    tm = min(ROW_TILE, length)
    wr = jnp.concatenate([w_router, jnp.zeros((d, LANES - N_EXPERTS), F32)], axis=1)
    wr_hi = wr.astype(BF16)
    wr_lo = (wr - wr_hi.astype(F32)).astype(BF16)
    wr = jnp.concatenate([wr_hi, wr_lo, wr_hi], axis=0)
    tri =jnp.asarray(np.tril(np.ones((tm, tm), np.float32), -1), dtype=BF16)
    const = lambda b, i: (0, 0)
    return pl.pallas_call(
        functools.partial(_router_kernel, tm=tm),
        grid=(bsz, length // tm),
        in_specs=[
            pl.BlockSpec((1, tm, d), lambda b, i: (b, i, 0)),
            pl.BlockSpec((1, 6, d), lambda b, i: (b, 0, 0)),
            pl.BlockSpec((4, d), const),
            pl.BlockSpec((3 * d, LANES), const),
            pl.BlockSpec((tm, tm), const),
        ],
        out_specs=[
            pl.BlockSpec((1, tm, d), lambda b, i: (b, i, 0)),
            pl.BlockSpec((1, tm, LANES), lambda b, i: (b, i, 0)),
            pl.BlockSpec((8, tm), lambda b, i: (0, b * (length // tm) + i)),
            pl.BlockSpec((8, LANES), const),
        ],
        out_shape=[
            jax.ShapeDtypeStruct((bsz, length, d), F32),
            jax.ShapeDtypeStruct((bsz, length, LANES), F32),
            jax.ShapeDtypeStruct((8, bsz * length), F32),
            jax.ShapeDtypeStruct((8, LANES), F32),
        ],
        scratch_shapes=[pltpu.VMEM((8, LANES), F32)],
        compiler_params=_params(("arbitrary", "arbitrary")),
        name="moe_router",
    )(x, mods, g4, wr, tri)


def _expert_kernel(te_ref, nv_ref, x_ref, w1_ref, w3_ref, w2_ref, o_ref, acc_ref):
    i = pl.program_id(0)
    j = pl.program_id(1)

    @pl.when(i < nv_ref[0])
    def _():
        y = _swiglu_halves(x_ref[...].astype(BF16), w1_ref, w3_ref, w2_ref, lead=(0,))

        @pl.when(j == 0)
        def _():
            acc_ref[...] = y

        @pl.when(j > 0)
        def _():
            acc_ref[...] += y

        @pl.when(j == pl.num_programs(1) - 1)
        def _():
            o_ref[...] = acc_ref[...]


def _experts(x_sorted, tile_expert, n_valid, w1, w3, w2):
    rows = x_sorted.shape[0]
    d = w1.shape[1]
    dff = w1.shape[2]
    tm = EXPERT_ROW_TILE
    n_ff = 2
    tf = dff // n_ff

    def row_map(i, j, te, nv):
        return (jnp.maximum(jnp.minimum(i, nv[0] - 1), 0), 0)

    def w13_map(i, j, te, nv):
        return (te[jnp.maximum(jnp.minimum(i, nv[0] - 1), 0)], 0, jnp.where(i < nv[0], j, n_ff - 1))

    def w2_map(i, j, te, nv):
        return (te[jnp.maximum(jnp.minimum(i, nv[0] - 1), 0)], jnp.where(i < nv[0], j, n_ff - 1), 0)

    return pl.pallas_call(
        _expert_kernel,
        grid_spec=pltpu.PrefetchScalarGridSpec(
            num_scalar_prefetch=2,
            grid=(rows // tm, n_ff),
            in_specs=[
                pl.BlockSpec((tm, d), row_map),
                pl.BlockSpec((1, d, tf), w13_map),
                pl.BlockSpec((1, d, tf), w13_map),
                pl.BlockSpec((1, tf, d), w2_map),
            ],
            out_specs=pl.BlockSpec((tm, d), row_map),
            scratch_shapes=[pltpu.VMEM((tm, d), F32)],
        ),
        out_shape=jax.ShapeDtypeStruct((rows, d), F32),
        compiler_params=_params(("arbitrary", "arbitrary")),
        name="expert_swiglu",
    )(tile_expert, n_valid, x_sorted, w1, w3, w2)


def _combine_kernel(x_ref, y_ref, meta_ref, g_ref, mod_ref, o_ref):
    meta = meta_ref[0]
    y = meta[:, 2:3] * y_ref[0, 0] + meta[:, 3:4] * y_ref[1, 0]
    o_ref[0] = x_ref[0] + mod_ref[0, 5:6, :] * _rms(y, g_ref[3:4, :])


def _combine(x, y2, meta, g4, mods):
    bsz, length, d = x.shape
    tm = min(ROW_TILE, length)
    return pl.pallas_call(
        _combine_kernel,
        grid=(bsz, length // tm),
        in_specs=[
            pl.BlockSpec((1, tm, d), lambda b, i: (b, i, 0)),
            pl.BlockSpec((2, 1, tm, d), lambda b, i: (0, b, i, 0)),
            pl.BlockSpec((1, tm, LANES), lambda b, i: (b, i, 0)),
            pl.BlockSpec((4, d), lambda b, i: (0, 0)),
            pl.BlockSpec((1, 6, d), lambda b, i: (b, 0, 0)),
        ],
        out_specs=pl.BlockSpec((1, tm, d), lambda b, i: (b, i, 0)),
        out_shape=jax.ShapeDtypeStruct((bsz, length, d), F32),
        compiler_params=_params(("parallel", "parallel")),
        name="moe_combine",
    )(x, y2, meta, g4, mods)


SC_CORES = 2
SC_SUBCORES = 16
SC_WORKERS = SC_CORES * SC_SUBCORES
SC_ROW_CHUNK = 32


def _sc_mesh():
    return plsc.VectorSubcoreMesh(core_axis_name="c", subcore_axis_name="s")


def _sc_dispatch(rows_in, pos2, n_out):
    tokens, width = rows_in.shape
    per_worker = tokens // SC_WORKERS
    chunk = SC_ROW_CHUNK
    n_pairs = per_worker // (2 * chunk)
    idx = pos2.reshape(2, SC_WORKERS, 2 * n_pairs, chunk)

    @functools.partial(
        pl.kernel, mesh=_sc_mesh(),
        out_type=jax.ShapeDtypeStruct((n_out, width), rows_in.dtype),
        scratch_types=[
            pltpu.VMEM((2 * n_pairs, chunk), jnp.int32),
            pltpu.VMEM((2 * n_pairs, chunk), jnp.int32),
            pltpu.VMEM((2, chunk, width), rows_in.dtype),
        ] + [pltpu.SemaphoreType.DMA] * 6,
        name="sc_dispatch",
    )
    def body(rows_hbm, idx_hbm, out_hbm, idx0_v, idx1_v, rows_v, rsem0, rsem1, wsem0, wsem1, wsem2, wsem3):
        wid = lax.axis_index("s") * SC_CORES + lax.axis_index("c")
        base = wid * per_worker
        rsem = (rsem0, rsem1)
        wsem = ((wsem0, wsem1), (wsem2, wsem3))
        pltpu.sync_copy(idx_hbm.at[0, wid], idx0_v)
        pltpu.sync_copy(idx_hbm.at[1, wid], idx1_v)

        @pl.loop(0, n_pairs)
        def _(p):
            c0 = 2 * p
            reads = [pltpu.async_copy(rows_hbm.at[pl.ds(base + (c0 + s) * chunk, chunk)], rows_v.at[s], rsem[s])
                     for s in range(2)]
            writes = []
            for s in range(2):
                reads[s].wait()
                writes.append(pltpu.async_copy(rows_v.at[s], out_hbm.at[idx0_v.at[c0 + s]], wsem[s][0]))
                writes.append(pltpu.async_copy(rows_v.at[s], out_hbm.at[idx1_v.at[c0 + s]], wsem[s][1]))
            for w in writes:
                w.wait()

    return body(rows_in, idx)


def _sc_gather(table, idx):
    n = idx.shape[0]
    width = table.shape[1]
    chunk = SC_ROW_CHUNK
    per_worker = n // SC_WORKERS
    n_pairs = per_worker // (2 * chunk)
    idx3 = idx.reshape(SC_WORKERS, 2 * n_pairs, chunk)

    @functools.partial(
        pl.kernel, mesh=_sc_mesh(),
        out_type=jax.ShapeDtypeStruct((n, width), table.dtype),
        scratch_types=[
            pltpu.VMEM((2 * n_pairs, chunk), jnp.int32),
            pltpu.VMEM((2, chunk, width), table.dtype),
            pltpu.SemaphoreType.DMA, pltpu.SemaphoreType.DMA,
            pltpu.SemaphoreType.DMA, pltpu.SemaphoreType.DMA,
        ],
        name="sc_gather",
    )
    def body(table_hbm, idx_hbm, out_hbm, idx_v, rows_v, gsem0, gsem1, wsem0, wsem1):
        wid = lax.axis_index("s") * SC_CORES + lax.axis_index("c")
        base = wid * per_worker
        gsem = (gsem0, gsem1)
        wsem = (wsem0, wsem1)
        pltpu.sync_copy(idx_hbm.at[wid], idx_v)

        @pl.loop(0, n_pairs)
        def _(p):
            c0 = 2 * p
            gathers = [pltpu.async_copy(table_hbm.at[idx_v.at[c0 + s]], rows_v.at[s], gsem[s]) for s in range(2)]
            writes = []
            for s in range(2):
                gathers[s].wait()
                rows_out = out_hbm.at[pl.ds(base + (c0 + s) * chunk, chunk)]
                writes.append(pltpu.async_copy(rows_v.at[s], rows_out, wsem[s]))
            for s in range(2):
                writes[s].wait()

    return body(table, idx3)


def _moe(x, mods, g4, w_router, w1, w3, w2):
    bsz, length, d = x.shape
    tokens = bsz * length
    tm = EXPERT_ROW_TILE
    xn, meta, meta_t, counts = _router(x, mods, g4, w_router)
    experts = meta_t[0:2].astype(jnp.int32)
    ranks = meta_t[4:6].astype(jnp.int32)
    cnt = counts[0, 0:N_EXPERTS].astype(jnp.int32)
    padded = ((cnt + tm - 1) // tm) * tm
    ends = jnp.cumsum(padded)
    starts = ends - padded
    group_start = jnp.zeros_like(experts)
    for e in range(N_EXPERTS):
        group_start = jnp.where(experts == e, starts[e], group_start)
    pos = (group_start + ranks).reshape(2 * tokens)
    rows = 2 * tokens + N_EXPERTS * tm
    n_tiles = rows // tm
    n_valid = (ends[-1] // tm).astype(jnp.int32).reshape(1)
    tile_start = jnp.arange(n_tiles, dtype=jnp.int32) * tm
    tile_expert = jnp.minimum(
        jnp.sum((tile_start[:, None] >= ends[None, :]).astype(jnp.int32), axis=1), N_EXPERTS - 1)
    x_sorted = _sc_dispatch(xn.reshape(tokens, d), pos.reshape(2, tokens), rows)
    y_sorted = _experts(x_sorted, tile_expert, n_valid, w1, w3, w2)
    y2 = _sc_gather(y_sorted, pos).reshape(2, bsz, length, d)
    return _combine(x, y2, meta, g4, mods)


def _grid_pos_embed(rows, d):
    row = np.repeat(np.arange(rows, dtype=np.float64), GRID_W)
    col = np.tile(np.arange(GRID_W, dtype=np.float64), rows)
    n_freq = d // 4
    omega = POS_THETA ** (-np.arange(n_freq, dtype=np.float64) / n_freq)
    ang_r = row[:, None] * omega
    ang_c = col[:, None] * omega
    return jnp.asarray(np.concatenate([np.sin(ang_r), np.cos(ang_r), np.sin(ang_c), np.cos(ang_c)], axis=-1),
                       dtype=F32)


def kernel(x, c, ctx, c_ctx, ada_w, ada_b, norm_g, ab_w_in, rg_conv_w, rg_conv_b, rg_w_r, rg_b_r, rg_w_i, rg_b_i, rg_lambda, cv_dw_w, cv_dw_b, cv_ln_g, cv_ln_b, ab_w_out, ffn_w1, ffn_w3, ffn_w2, cd_w_in, sg_ln_g, sg_ln_b, sg_w_s, sg_b_s, cd_w_out, moe_router, moe_w1, moe_w3, moe_w2):
    bsz, length, d = x.shape
    assert ada_w.shape[0] == 2, "two layers: one even (RG-LRU | Conformer), one odd (gMLP | Fourier)"
    pos = _grid_pos_embed(length // GRID_W, d)

    cpad = jnp.concatenate([c, c_ctx[None, :], jnp.zeros((16 - bsz - 1, d), F32)], axis=0)
    m = _ada(cpad, ada_w, ada_b)
    mods = m[:, :bsz].reshape(2, bsz, 6, d)
    mods_ctx = m[:, bsz:bsz + 1].reshape(2, 1, 6, d)

    mods_t = mods[0].transpose(1, 0, 2)
    mods_ctx_t = jnp.broadcast_to(mods_ctx[0].reshape(6, 1, d), (6, bsz, d))
    pos3 = pos[:, None, :]
    w_in = ab_w_in[0]
    xa3, gg3, glu3, x0_t, ffn1b, ffn3b, ffn2b = _inproj_t(
        x, pos3, mods_t, norm_g[0], w_in, rg_conv_w[0], rg_conv_b[0], True, riders=(ffn_w1[0], ffn_w3[0], ffn_w2[0]))
    (xc3,) = _inproj_t(ctx, None, mods_ctx_t, norm_g[0], w_in[:, 0:A_WIDTH], rg_conv_w[0], rg_conv_b[0], False)
    wg, bg, sp = _gate_weights(rg_w_r[0], rg_b_r[0], rg_w_i[0], rg_b_i[0], rg_lambda[0])
    n_exp, _, dff_e = moe_w1[0].shape
    rec_hi, rec_lo, w2b = _scan_t(xa3, xc3, gg3, wg, bg, sp, riders=(moe_w2[0].reshape(n_exp * dff_e, d),))
    u3, w1b = _conv_t(glu3, cv_dw_w[0], cv_dw_b[0], cv_ln_g[0], cv_ln_b[0],
                      riders=(moe_w1[0].reshape(n_exp * d, dff_e),))
    x1_t = _outproj_t(rec_lo, rec_hi, u3, x0_t, ab_w_out[0], norm_g[0], mods_t)
    x2, w3b = _ffn_t(x1_t, mods_t, norm_g[0], ffn1b, ffn3b, ffn2b, riders=(moe_w3[0].reshape(n_exp * d, dff_e),))

    u, v, f = _inproj_cd(x2, mods[1], norm_g[1], cd_w_in[0])
    fo = _fourier(f)
    x3 = _gate_out(u, v, fo, x2, sg_ln_g[0], sg_ln_b[0], sg_w_s[0], sg_b_s[0], cd_w_out[0],
                   norm_g[1], mods[1])
    return _moe(x3, mods[1], norm_g[1], moe_router[0], w1b.reshape(n_exp, d, dff_e), w3b.reshape(n_exp, d, dff_e),
                w2b.reshape(n_exp, dff_e, d))
```

```python
import functools

import numpy as np
import jax
import jax.numpy as jnp
from jax import lax
from jax.experimental import pallas as pl
from jax.experimental.pallas import tpu as pltpu
from jax.experimental.pallas import tpu_sc as plsc

F32 = jnp.float32
BF16 = jnp.bfloat16

GRID_W = 64
POS_THETA = 10000.0
NORM_EPS = 1e-6
LN_EPS = 1e-5
LRU_C = 8.0

A_WIDTH = 512
A_HEADS = 8
A_CONV_W = 4
B_WIDTH = 512
B_CONV_W = 31
C_WIDTH = 512
C_HEADS = 8
CHUNK = 128
D_WIDTH = 512
D_GROUPS = 4
D_GROUP_DIM = D_WIDTH // D_GROUPS
N_EXPERTS = 8

LANES = 128
ROW_TILE = 512
EXPERT_ROW_TILE = 512
VMEM_LIMIT = 56 * 2 ** 20


def _params(sem):
    return pltpu.CompilerParams(dimension_semantics=sem, vmem_limit_bytes=VMEM_LIMIT)


def _rms(x, g):
    return x * lax.rsqrt(jnp.mean(x * x, axis=-1, keepdims=True) + NORM_EPS) * g


def _bdot(a, b):
    return jnp.dot(a, b, preferred_element_type=F32)


def _cast_once(w_ref, wb_ref, first):
    @pl.when(first)
    def _():
        wb_ref[...] = w_ref[...].astype(BF16)


MXU_WIDTH = 256


def _swiglu_halves(x, w1_ref, w3_ref, w2_ref, lead=()):
    n = w1_ref.shape[-1]
    half = -(-(n // 2) // MXU_WIDTH) * MXU_WIDTH
    y = None
    for lo, hi in ((0, half), (half, n)):
        a = _bdot(x, w1_ref[lead + (slice(None), slice(lo, hi))])
        b = _bdot(x, w3_ref[lead + (slice(None), slice(lo, hi))])
        t = (jax.nn.silu(a) * b).astype(BF16)
        part = _bdot(t, w2_ref[lead + (slice(lo, hi), slice(None))])
        y = part if y is None else y + part
    return y


def _ada_kernel(c_ref, w_ref, b_ref, o_ref):
    s = jax.nn.silu(c_ref[...])
    o_ref[0] = _bdot(s.astype(BF16), w_ref[0].astype(BF16)) + b_ref[0]


def _ada(cpad, ada_w, ada_b):
    nl, d, n6 = ada_w.shape
    rows = cpad.shape[0]
    tn = n6 // 4
    return pl.pallas_call(
        _ada_kernel,
        grid=(nl, n6 // tn),
        in_specs=[
            pl.BlockSpec((rows, d), lambda l, j: (0, 0)),
            pl.BlockSpec((1, d, tn), lambda l, j: (l, 0, j)),
            pl.BlockSpec((1, 1, tn), lambda l, j: (l, 0, j)),
        ],
        out_specs=pl.BlockSpec((1, rows, tn), lambda l, j: (l, 0, j)),
        out_shape=jax.ShapeDtypeStruct((nl, rows, n6), F32),
        compiler_params=_params(("arbitrary", "arbitrary")),
        name="ada_terms",
    )(cpad, ada_w, ada_b.reshape(nl, 1, n6))


TIME_TILE = 64
SCAN_TIME_TILE = 128
SCAN_LANES = 256
CONV_TIME_TILE = 64
CONV_HALO = 16


def _time_major(ref):
    return jnp.swapaxes(ref[...], 0, 1)


def _inproj_t_kernel(*refs, tt, add_pos, branches, n_cast):
    wb_ref = refs[-1]
    refs = refs[:-1]
    if n_cast:
        n_in = len(refs) - n_cast - (4 if branches else 1) - n_cast
        _cast_riders(refs[n_in:n_in + n_cast], refs[len(refs) - n_cast:])
        refs = refs[:n_in] + refs[n_in + n_cast:len(refs) - n_cast]
    if add_pos:
        x_ref, xp_ref, xn_ref, pos_ref, posp_ref, posn_ref, mod_ref, g_ref, w_ref, cw_ref, cb_ref, *outs = refs
        xall = jnp.concatenate(
            [_time_major(xp_ref)[7:8] + posp_ref[...], _time_major(x_ref) + pos_ref[...],
             _time_major(xn_ref)[0:2] + posn_ref[...]], axis=0)
    else:
        x_ref, xp_ref, xn_ref, mod_ref, g_ref, w_ref, cw_ref, cb_ref, *outs = refs
        xall = jnp.concatenate([_time_major(xp_ref)[7:8], _time_major(x_ref), _time_major(xn_ref)[0:2]], axis=0)
    i = pl.program_id(0)
    last = pl.num_programs(0) - 1
    steps, bsz, d = xall.shape
    h = _rms(xall, g_ref[0:1, :]) * (1.0 + mod_ref[1]) + mod_ref[0]
    _cast_once(w_ref, wb_ref, i == 0)
    z = _bdot(h.reshape(steps * bsz, d).astype(BF16), wb_ref[...])
    z3 = z.reshape(steps, bsz, z.shape[-1])
    xa = z3[:, :, 0:A_WIDTH]
    xa = jnp.concatenate([jnp.where(i > 0, xa[0:1], 0.0), xa[1:tt + 1], jnp.where(i < last, xa[tt + 1:], 0.0)], axis=0)
    xconv = cb_ref[...] + cw_ref[0:1, :] * xa[0:tt]
    for k in range(1, A_CONV_W):
        xconv = xconv + cw_ref[k:k + 1, :] * xa[k:k + tt]
    outs[0][...] = xconv
    if branches:
        outs[1][...] = jax.nn.gelu(z3[1:tt + 1, :, A_WIDTH:2 * A_WIDTH])
        vb = z3[1:tt + 1, :, 2 * A_WIDTH:2 * A_WIDTH + B_WIDTH]
        gb = z3[1:tt + 1, :, 2 * A_WIDTH + B_WIDTH:]
        outs[2][...] = vb * jax.nn.sigmoid(gb)
        outs[3][...] = xall[1:tt + 1]


def _inproj_t(x, pos3, mods_t, g4, w, conv_w, conv_b, branches, riders=()):
    bsz, length, d = x.shape
    n = w.shape[1]
    tt = min(TIME_TILE, length)
    add_pos = pos3 is not None
    n_out = 3 if branches else 1
    prev_map = lambda i: (jnp.maximum(i * tt - 1, 0), 0, 0)
    next_map = lambda i: (jnp.minimum((i + 1) * (tt // 2), length // 2 - 1), 0, 0)
    in_specs = [
        pl.BlockSpec((bsz, tt, d), lambda i: (0, i, 0)),
        pl.BlockSpec((bsz, 8, d), lambda i: (0, jnp.maximum(i * (tt // 8) - 1, 0), 0)),
        pl.BlockSpec((bsz, 8, d), lambda i: (0, jnp.minimum((i + 1) * (tt // 8), length // 8 - 1), 0)),
    ]
    args = [x, x, x]
    if add_pos:
        in_specs += [
            pl.BlockSpec((tt, 1, d), lambda i: (i, 0, 0)),
            pl.BlockSpec((1, 1, d), prev_map),
            pl.BlockSpec((2, 1, d), next_map),
        ]
        args += [pos3, pos3, pos3]
    in_specs += [
        pl.BlockSpec((6, bsz, d), lambda i: (0, 0, 0)),
        pl.BlockSpec((4, d), lambda i: (0, 0)),
        pl.BlockSpec((d, n), lambda i: (0, 0), pipeline_mode=pl.Buffered(1)),
        pl.BlockSpec((A_CONV_W, A_WIDTH), lambda i: (0, 0)),
        pl.BlockSpec((1, A_WIDTH), lambda i: (0, 0)),
    ]
    args += [mods_t, g4, w, conv_w, conv_b.reshape(1, A_WIDTH)]
    r_in, r_out, r_shapes = _rider_specs(riders, length // tt, lambda i: i)
    out_specs = [pl.BlockSpec((tt, bsz, A_WIDTH), lambda i: (i, 0, 0))] * n_out
    out_shapes = [jax.ShapeDtypeStruct((length, bsz, A_WIDTH), F32)] * n_out
    if branches:
        out_specs.append(pl.BlockSpec((tt, bsz, d), lambda i: (i, 0, 0)))
        out_shapes.append(jax.ShapeDtypeStruct((length, bsz, d), F32))
    return pl.pallas_call(
        functools.partial(_inproj_t_kernel, tt=tt, add_pos=add_pos, branches=branches, n_cast=len(riders)),
        grid=(length // tt,),
        in_specs=in_specs + r_in,
        out_specs=out_specs + r_out,
        out_shape=out_shapes + r_shapes,
        scratch_shapes=[pltpu.VMEM((d, n), BF16)],
        compiler_params=_params(("arbitrary",)),
        name="inproj_ab" if branches else "inproj_ctx",
    )(*args, *riders)


def _cast_riders(cast_in, cast_out):
    for src, dst in zip(cast_in, cast_out):
        dst[...] = src[...].astype(BF16)


def _rider_specs(arrays, n_steps, step_of):
    bf16_rows = 16
    in_specs, out_specs, out_shapes = [], [], []
    for arr in arrays:
        rows, cols = arr.shape
        share = next(k for k in (1, 2, 4, 8, 16) if (rows * k) % (n_steps * bf16_rows) == 0)
        block = (rows * share // n_steps, cols)
        index_map = lambda *idx, share=share: (step_of(*idx) // share, 0)
        in_specs.append(pl.BlockSpec(block, index_map))
        out_specs.append(pl.BlockSpec(block, index_map))
        out_shapes.append(jax.ShapeDtypeStruct(arr.shape, BF16))
    return in_specs, out_specs, out_shapes


def _scan_t_kernel(*refs, tb, nb, ctx_len, n_cast):
    xf_ref, xb_ref, ggf_ref, ggb_ref, xc_ref, wg_ref, bg_ref, sp_ref = refs[:8]
    cast_in = refs[8:8 + n_cast]
    hi_ref, lo_ref = refs[8 + n_cast:10 + n_cast]
    cast_out = refs[10 + n_cast:10 + 2 * n_cast]
    a_ref, b_ref, h_ref, state_ref = refs[10 + 2 * n_cast:]
    _cast_riders(cast_in, cast_out)
    k = pl.program_id(1)
    half = nb // 2
    bsz, lanes = xf_ref.shape[1], xf_ref.shape[2]
    chunk = 128

    def coefficients(x_ref, n, direction):
        def body(c, _):
            t0 = pl.multiple_of(c * chunk, chunk)
            x = x_ref[pl.ds(t0, chunk)].reshape(chunk * bsz, lanes)
            th = jnp.tanh(_bdot(x.astype(BF16), wg_ref[0, direction]) + bg_ref[0, direction])
            i = 0.5 * th[:, lanes:] + 0.5
            c = (-0.5 * LRU_C) * sp_ref[0, direction:direction + 1, :]
            log_a = c * th[:, 0:lanes] + c
            a = jnp.exp(log_a)
            one_minus_a2 = -jnp.tanh(log_a) * (a * a + 1.0)
            a_ref[direction, pl.ds(t0, chunk)] = a.reshape(chunk, bsz, lanes)
            b_ref[direction, pl.ds(t0, chunk)] = (jnp.sqrt(one_minus_a2) * (i * x)).reshape(chunk, bsz, lanes)
            return 0
        lax.fori_loop(0, n // chunk, body, 0)

    def sweep(n, store):
        def body(t, carry):
            hf, hb = carry
            tr = n - 1 - t
            hf = a_ref[0, pl.ds(t, 1)][0] * hf + b_ref[0, pl.ds(t, 1)][0]
            hb = a_ref[1, pl.ds(tr, 1)][0] * hb + b_ref[1, pl.ds(tr, 1)][0]
            if store:
                b_ref[0, pl.ds(t, 1)] = hf[None]
                b_ref[1, pl.ds(tr, 1)] = hb[None]
            return hf, hb
        hf, hb = lax.fori_loop(0, n, body, (state_ref[0], state_ref[1]), unroll=8)
        state_ref[0] = hf
        state_ref[1] = hb

    @pl.when(k == 0)
    def _():
        state_ref[...] = jnp.zeros_like(state_ref)
        coefficients(xc_ref, ctx_len, 0)
        coefficients(xc_ref, ctx_len, 1)
        sweep(ctx_len, False)

    coefficients(xf_ref, tb, 0)
    coefficients(xb_ref, tb, 1)
    sweep(tb, True)
    m = nb - 1 - k

    @pl.when(k < half)
    def _():
        h_ref[k] = b_ref[0, 0:tb]
        h_ref[m] = b_ref[1, 0:tb]

    @pl.when(k >= half)
    def _():
        hi_ref[...] = (b_ref[0, 0:tb] + h_ref[k]) * ggf_ref[...]
        lo_ref[...] = (h_ref[m] + b_ref[1, 0:tb]) * ggb_ref[...]


def _scan_t(xa3, xc3, gg3, wg, bg, sp, riders=()):
    length, bsz, width = xa3.shape
    ctx_len = xc3.shape[0]
    tb = min(SCAN_TIME_TILE, length // 2)
    nb = length // tb
    half = nb // 2
    lanes = SCAN_LANES
    blk = (tb, bsz, lanes)
    r_in, r_out, r_shapes = _rider_specs(riders, (width // lanes) * nb, lambda g, k: g * nb + k)
    return pl.pallas_call(
        functools.partial(_scan_t_kernel, tb=tb, nb=nb, ctx_len=ctx_len, n_cast=len(riders)),
        grid=(width // lanes, nb),
        in_specs=[
            pl.BlockSpec(blk, lambda g, k: (k, 0, g)),
            pl.BlockSpec(blk, lambda g, k: (nb - 1 - k, 0, g)),
            pl.BlockSpec(blk, lambda g, k: (jnp.maximum(k, half), 0, g)),
            pl.BlockSpec(blk, lambda g, k: (jnp.minimum(nb - 1 - k, half - 1), 0, g)),
            pl.BlockSpec((ctx_len, bsz, lanes), lambda g, k: (0, 0, g)),
            pl.BlockSpec((1, 2, lanes, 2 * lanes), lambda g, k: (g, 0, 0, 0)),
            pl.BlockSpec((1, 2, 1, 2 * lanes), lambda g, k: (g, 0, 0, 0)),
            pl.BlockSpec((1, 2, lanes), lambda g, k: (g, 0, 0)),
        ] + r_in,
        out_specs=[
            pl.BlockSpec(blk, lambda g, k: (jnp.maximum(k - half, 0), 0, g)),
            pl.BlockSpec(blk, lambda g, k: (jnp.minimum(nb - 1 - k, half - 1), 0, g)),
        ] + r_out,
        out_shape=[jax.ShapeDtypeStruct((length // 2, bsz, width), F32)] * 2 + r_shapes,
        scratch_shapes=[
            pltpu.VMEM((2, max(tb, ctx_len), bsz, lanes), F32),
            pltpu.VMEM((2, max(tb, ctx_len), bsz, lanes), F32),
            pltpu.VMEM((nb, tb, bsz, lanes), F32),
            pltpu.VMEM((2, bsz, lanes), F32),
        ],
        compiler_params=_params(("parallel", "arbitrary")),
        name="rglru_scan",
    )(xa3, xa3, gg3, gg3, xc3, wg, bg, sp, *riders)


def _gate_weights(w_r, b_r, w_i, b_i, lam):
    hd = A_WIDTH // A_HEADS
    heads_per_group = SCAN_LANES // hd
    ngroups = A_WIDTH // SCAN_LANES

    def blockdiag(w):
        w = w.reshape(ngroups, heads_per_group, hd, hd)
        eye = jnp.eye(heads_per_group, dtype=w.dtype)
        return jnp.einsum("ghde,hk->ghdke", w, eye).reshape(ngroups, SCAN_LANES, SCAN_LANES)

    wg = jnp.stack([jnp.concatenate([blockdiag(w_r[d]), blockdiag(w_i[d])], axis=-1) for d in range(2)], axis=1)
    bg = jnp.stack([jnp.concatenate([b_r[d].reshape(ngroups, 1, SCAN_LANES), b_i[d].reshape(ngroups, 1, SCAN_LANES)],
                                    axis=-1) for d in range(2)], axis=1)
    sp = jax.nn.softplus(-lam.astype(F32)).reshape(2, ngroups, SCAN_LANES).transpose(1, 0, 2)
    return (0.5 * wg).astype(BF16), 0.5 * bg, sp


def _conv_t_kernel(*refs, tb, n_cast):
    x_ref, xp_ref, xn_ref, w_ref, b_ref, lng_ref, lnb_ref = refs[:7]
    cast_in = refs[7:7 + n_cast]
    o_ref = refs[7 + n_cast]
    cast_out = refs[8 + n_cast:8 + 2 * n_cast]
    stage_ref = refs[-1]
    _cast_riders(cast_in, cast_out)
    i = pl.program_id(0)
    last = pl.num_programs(0) - 1
    halo = CONV_HALO
    stage_ref[0:halo] = jnp.where(i > 0, xp_ref[...], 0.0)
    stage_ref[halo:halo + tb] = x_ref[...]
    stage_ref[halo + tb:2 * halo + tb] = jnp.where(i < last, xn_ref[...], 0.0)
    sub = 8

    def body(r, _):
        t0 = pl.multiple_of(r * sub, sub)
        def tap(k, acc):
            return acc + w_ref[k] * stage_ref[pl.ds(t0 + 1 + k, sub)]
        acc = lax.fori_loop(0, B_CONV_W + 1, tap, jnp.broadcast_to(b_ref[...], (sub,) + b_ref.shape), unroll=8)
        o_ref[pl.ds(t0, sub)] = acc
        return 0
    lax.fori_loop(0, tb // sub, body, 0)

    norm_rows = 64

    def norm_body(r, _):
        t0 = pl.multiple_of(r * norm_rows, norm_rows)
        acc = o_ref[pl.ds(t0, norm_rows)]
        mu = jnp.mean(acc, axis=-1, keepdims=True)
        cen = acc - mu
        var = jnp.mean(cen * cen, axis=-1, keepdims=True)
        y = cen * lax.rsqrt(var + LN_EPS) * lng_ref[...] + lnb_ref[...]
        o_ref[pl.ds(t0, norm_rows)] = jax.nn.silu(y)
        return 0
    lax.fori_loop(0, tb // norm_rows, norm_body, 0)


def _conv_t(glu3, dw_w, dw_b, ln_g, ln_b, riders=()):
    length, bsz, width = glu3.shape
    tb = min(CONV_TIME_TILE, length)
    halo = CONV_HALO
    taps = dw_w.shape[0]
    w8 = jnp.broadcast_to(jnp.concatenate([dw_w, jnp.zeros((32 - taps, width), F32)], axis=0)[:, None, :],
                          (32, bsz, width))
    b8 = jnp.broadcast_to(dw_b[None, :], (bsz, width))
    r_in, r_out, r_shapes = _rider_specs(riders, length // tb, lambda i: i)
    return pl.pallas_call(
        functools.partial(_conv_t_kernel, tb=tb, n_cast=len(riders)),
        grid=(length // tb,),
        in_specs=[
            pl.BlockSpec((tb, bsz, width), lambda i: (i, 0, 0)),
            pl.BlockSpec((halo, bsz, width), lambda i: (jnp.maximum(i * (tb // halo) - 1, 0), 0, 0)),
            pl.BlockSpec((halo, bsz, width), lambda i: (jnp.minimum((i + 1) * (tb // halo), length // halo - 1), 0, 0)),
            pl.BlockSpec((32, bsz, width), lambda i: (0, 0, 0)),
            pl.BlockSpec((bsz, width), lambda i: (0, 0)),
            pl.BlockSpec((1, width), lambda i: (0, 0)),
            pl.BlockSpec((1, width), lambda i: (0, 0)),
        ] + r_in,
        out_specs=[pl.BlockSpec((tb, bsz, width), lambda i: (i, 0, 0))] + r_out,
        out_shape=[jax.ShapeDtypeStruct((length, bsz, width), F32)] + r_shapes,
        scratch_shapes=[pltpu.VMEM((tb + 2 * halo, bsz, width), F32)],
        compiler_params=_params(("parallel",)),
        name="conformer_conv",
    )(glu3, glu3, glu3, w8, b8, ln_g.reshape(1, -1), ln_b.reshape(1, -1), *riders)


def _outproj_t_kernel(lo_ref, hi_ref, u_ref, x_ref, wout_ref, g_ref, mod_ref, o_ref, wb_ref, *, half_steps):
    i = pl.program_id(0)
    tt, bsz, d = x_ref.shape
    _cast_once(wout_ref, wb_ref, i == 0)
    rec = jnp.where(i < half_steps, lo_ref[...], hi_ref[...])
    rows = tt * bsz
    y = (_bdot(rec.reshape(rows, A_WIDTH).astype(BF16), wb_ref[0:A_WIDTH, :])
         + _bdot(u_ref[...].reshape(rows, B_WIDTH).astype(BF16), wb_ref[A_WIDTH:, :]))
    yn = _rms(y, g_ref[1:2, :]).reshape(tt, bsz, d)
    o_ref[...] = x_ref[...] + mod_ref[2] * yn


def _outproj_t(rec_lo, rec_hi, u3, x0_t, w_out, g4, mods_t):
    length, bsz, d = x0_t.shape
    tt = min(TIME_TILE, length // 2)
    half_steps = (length // 2) // tt
    return pl.pallas_call(
        functools.partial(_outproj_t_kernel, half_steps=half_steps),
        grid=(length // tt,),
        in_specs=[
            pl.BlockSpec((tt, bsz, A_WIDTH), lambda i: (jnp.minimum(i, half_steps - 1), 0, 0)),
            pl.BlockSpec((tt, bsz, A_WIDTH), lambda i: (jnp.maximum(i - half_steps, 0), 0, 0)),
            pl.BlockSpec((tt, bsz, B_WIDTH), lambda i: (i, 0, 0)),
            pl.BlockSpec((tt, bsz, d), lambda i: (i, 0, 0)),
            pl.BlockSpec((A_WIDTH + B_WIDTH, d), lambda i: (0, 0)),
            pl.BlockSpec((4, d), lambda i: (0, 0)),
            pl.BlockSpec((6, bsz, d), lambda i: (0, 0, 0)),
        ],
        out_specs=pl.BlockSpec((tt, bsz, d), lambda i: (i, 0, 0)),
        out_shape=jax.ShapeDtypeStruct((length, bsz, d), F32),
        scratch_shapes=[pltpu.VMEM((A_WIDTH + B_WIDTH, d), BF16)],
        compiler_params=_params(("arbitrary",)),
        name="mixer_ab_outproj",
    )(rec_lo, rec_hi, u3, x0_t, w_out, g4, mods_t)


def _ffn_t_kernel(x_ref, mod_ref, g_ref, w1_ref, w3_ref, w2_ref, *rest, n_cast):
    _cast_riders(rest[:n_cast], rest[n_cast + 1:])
    o_ref = rest[n_cast]
    x = x_ref[...]
    tt, bsz, d = x.shape
    h = (_rms(x, g_ref[2:3, :]) * (1.0 + mod_ref[4]) + mod_ref[3]).reshape(tt * bsz, d).astype(BF16)
    y = _swiglu_halves(h, w1_ref, w3_ref, w2_ref)
    o_ref[...] = jnp.swapaxes(x + mod_ref[5] * _rms(y, g_ref[3:4, :]).reshape(tt, bsz, d), 0, 1)


def _ffn_t(x3, mods_t, g4, w1, w3, w2, riders=()):
    length, bsz, d = x3.shape
    dff = w1.shape[1]
    tt = min(TIME_TILE, length)
    r_in, r_out, r_shapes = _rider_specs(riders, length // tt, lambda i: i)
    return pl.pallas_call(
        functools.partial(_ffn_t_kernel, n_cast=len(riders)),
        grid=(length // tt,),
        in_specs=[
            pl.BlockSpec((tt, bsz, d), lambda i: (i, 0, 0)),
            pl.BlockSpec((6, bsz, d), lambda i: (0, 0, 0)),
            pl.BlockSpec((4, d), lambda i: (0, 0)),
            pl.BlockSpec((d, dff), lambda i: (0, 0), pipeline_mode=pl.Buffered(1)),
            pl.BlockSpec((d, dff), lambda i: (0, 0), pipeline_mode=pl.Buffered(1)),
            pl.BlockSpec((dff, d), lambda i: (0, 0), pipeline_mode=pl.Buffered(1)),
        ] + r_in,
        out_specs=[pl.BlockSpec((bsz, tt, d), lambda i: (0, i, 0))] + r_out,
        out_shape=[jax.ShapeDtypeStruct((bsz, length, d), F32)] + r_shapes,
        compiler_params=_params(("parallel",)),
        name="dense_swiglu",
    )(x3, mods_t, g4, w1, w3, w2, *riders)


def _inproj_cd_kernel(x_ref, mod_ref, g_ref, w_ref, u_ref, v_ref, f_ref, wb_ref):
    _cast_once(w_ref, wb_ref, jnp.logical_and(pl.program_id(0) == 0, pl.program_id(1) == 0))
    x = x_ref[0]
    h = _rms(x, g_ref[0:1, :]) * (1.0 + mod_ref[0, 1:2, :]) + mod_ref[0, 0:1, :]
    z = _bdot(h.astype(BF16), wb_ref[...])
    u_ref[0] = jax.nn.gelu(z[:, 0:C_WIDTH])
    v_ref[0] = jax.nn.gelu(z[:, C_WIDTH:2 * C_WIDTH])
    f_ref[0] = z[:, 2 * C_WIDTH:]


def _inproj_cd(x, mods, g4, w):
    bsz, length, d = x.shape
    n = w.shape[1]
    tm = min(ROW_TILE, length)
    return pl.pallas_call(
        _inproj_cd_kernel,
        grid=(bsz, length // tm),
        in_specs=[
            pl.BlockSpec((1, tm, d), lambda b, i: (b, i, 0)),
            pl.BlockSpec((1, 6, d), lambda b, i: (b, 0, 0)),
            pl.BlockSpec((4, d), lambda b, i: (0, 0)),
            pl.BlockSpec((d, n), lambda b, i: (0, 0)),
        ],
        out_specs=[pl.BlockSpec((1, tm, C_WIDTH), lambda b, i: (b, i, 0))] * 3,
        out_shape=[jax.ShapeDtypeStruct((bsz, length, C_WIDTH), F32)] * 3,
        scratch_shapes=[pltpu.VMEM((d, n), BF16)],
        compiler_params=_params(("arbitrary", "arbitrary")),
        name="inproj_cd",
    )(x, mods, g4, w)


def _fourier_kernel(f_ref, dmat_ref, lmat_ref, o_ref, z_ref, *, length, scale):
    j = pl.program_id(1)
    rows_per_step = 512 if length % 512 == 0 else length

    @pl.when(j == 0)
    def _():
        def body(c, _):
            rows = pl.ds(pl.multiple_of(c * rows_per_step, rows_per_step), rows_per_step)
            fb = f_ref[0, rows, :].astype(BF16)
            for g in range(D_GROUPS):
                cs = _bdot(fb[:, g * D_GROUP_DIM:(g + 1) * D_GROUP_DIM], dmat_ref[...])
                z_ref[0, rows, g * D_GROUP_DIM:(g + 1) * D_GROUP_DIM] = cs[:, 0:D_GROUP_DIM].astype(BF16)
                z_ref[1, rows, g * D_GROUP_DIM:(g + 1) * D_GROUP_DIM] = cs[:, D_GROUP_DIM:].astype(BF16)
            return 0
        lax.fori_loop(0, length // rows_per_step, body, 0)

    out = _bdot(lmat_ref[0], z_ref[0]) + _bdot(lmat_ref[1], z_ref[1])
    o_ref[0] = (out * scale).astype(BF16)


def _dft_matrices(length):
    k = np.arange(length, dtype=np.int64)
    ang_l = 2.0 * np.pi * ((k[:, None] * k[None, :]) % length).astype(np.float64) / length
    lmat = np.stack([np.cos(ang_l), -np.sin(ang_l)], axis=0)
    d = np.arange(D_GROUP_DIM, dtype=np.int64)
    ang_d = 2.0 * np.pi * ((d[:, None] * d[None, :]) % D_GROUP_DIM).astype(np.float64) / D_GROUP_DIM
    dmat = np.concatenate([np.cos(ang_d), np.sin(ang_d)], axis=1)
    return jnp.asarray(lmat, dtype=F32).astype(BF16), jnp.asarray(dmat, dtype=F32).astype(BF16)


def _fourier(f):
    bsz, length, width = f.shape
    tm = min(ROW_TILE, length)
    lmat, dmat = _dft_matrices(length)
    scale = float(1.0 / np.sqrt(length * D_GROUP_DIM))
    return pl.pallas_call(
        functools.partial(_fourier_kernel, length=length, scale=scale),
        grid=(bsz, length // tm),
        in_specs=[
            pl.BlockSpec((1, length, width), lambda b, i: (b, 0, 0)),
            pl.BlockSpec((D_GROUP_DIM, 2 * D_GROUP_DIM), lambda b, i: (0, 0)),
            pl.BlockSpec((2, tm, length), lambda b, i: (0, i, 0)),
        ],
        out_specs=pl.BlockSpec((1, tm, width), lambda b, i: (b, i, 0)),
        out_shape=jax.ShapeDtypeStruct((bsz, length, width), BF16),
        scratch_shapes=[pltpu.VMEM((2, length, width), BF16)],
        compiler_params=_params(("parallel", "arbitrary")),
        name="fourier_mix",
    )(f, dmat, lmat)


def _gate_out_kernel(u_ref, v_ref, fo_ref, x_ref, lng_ref, lnb_ref, wcat_ref, bs_ref, wout_ref, g_ref, mod_ref,
                     o_ref, vs_ref, gt_ref, wb_ref, *, tm):
    _cast_once(wout_ref, wb_ref, jnp.logical_and(pl.program_id(0) == 0, pl.program_id(1) == 0))
    head_dim = C_WIDTH // C_HEADS
    head_of_lane = lax.broadcasted_iota(jnp.int32, (CHUNK, C_WIDTH), 1) // head_dim
    for ci in range(tm // CHUNK):
        rows = slice(ci * CHUNK, (ci + 1) * CHUNK)
        v = v_ref[0, rows, :]
        mu = jnp.mean(v, axis=-1, keepdims=True)
        cen = v - mu
        var = jnp.mean(cen * cen, axis=-1, keepdims=True)
        vn = (cen * lax.rsqrt(var + LN_EPS) * lng_ref[...] + lnb_ref[...]).astype(BF16)
        for h in range(C_HEADS):
            vs_ref[h * CHUNK:(h + 1) * CHUNK, :] = jnp.where(head_of_lane == h, vn, jnp.zeros_like(vn))
        mixed = _bdot(wcat_ref[...], vs_ref[...]) + bs_ref[...]
        gt_ref[rows, :] = (u_ref[0, rows, :] * mixed).astype(BF16)
    y = _bdot(gt_ref[...], wb_ref[0:C_WIDTH, :]) + _bdot(fo_ref[0], wb_ref[C_WIDTH:, :])
    o_ref[0] = x_ref[0] + mod_ref[0, 2:3, :] * _rms(y, g_ref[1:2, :])


def _gate_out(u, v, fo, x, ln_g, ln_b, w_s, b_s, w_out, g4, mods):
    bsz, length, d = x.shape
    tm = min(ROW_TILE, length)
    head_dim = C_WIDTH // C_HEADS
    wcat = w_s.transpose(1, 0, 2).reshape(CHUNK, C_HEADS * CHUNK).astype(BF16)
    bs_full = jnp.repeat(b_s.T, head_dim, axis=1)
    const = lambda b, i: (0, 0)
    return pl.pallas_call(
        functools.partial(_gate_out_kernel, tm=tm),
        grid=(bsz, length // tm),
        in_specs=[
            pl.BlockSpec((1, tm, C_WIDTH), lambda b, i: (b, i, 0)),
            pl.BlockSpec((1, tm, C_WIDTH), lambda b, i: (b, i, 0)),
            pl.BlockSpec((1, tm, D_WIDTH), lambda b, i: (b, i, 0)),
            pl.BlockSpec((1, tm, d), lambda b, i: (b, i, 0)),
            pl.BlockSpec((1, C_WIDTH), const),
            pl.BlockSpec((1, C_WIDTH), const),
            pl.BlockSpec((CHUNK, C_HEADS * CHUNK), const),
            pl.BlockSpec((CHUNK, C_WIDTH), const),
            pl.BlockSpec((C_WIDTH + D_WIDTH, d), const),
            pl.BlockSpec((4, d), const),
            pl.BlockSpec((1, 6, d), lambda b, i: (b, 0, 0)),
        ],
        out_specs=pl.BlockSpec((1, tm, d), lambda b, i: (b, i, 0)),
        out_shape=jax.ShapeDtypeStruct((bsz, length, d), F32),
        scratch_shapes=[
            pltpu.VMEM((C_HEADS * CHUNK, C_WIDTH), BF16),
            pltpu.VMEM((tm, C_WIDTH), BF16),
            pltpu.VMEM((C_WIDTH + D_WIDTH, d), BF16),
        ],
        compiler_params=_params(("arbitrary", "arbitrary")),
        name="gating_outproj",
    )(u, v, fo, x, ln_g.reshape(1, -1), ln_b.reshape(1, -1), wcat, bs_full, w_out, g4, mods)


def _router_kernel(x_ref, mod_ref, g_ref, wr_ref, tri_ref, xn_ref, meta_ref, metat_ref, cnt_ref, carry_ref, *, tm):
    first = jnp.logical_and(pl.program_id(0) == 0, pl.program_id(1) == 0)

    @pl.when(first)
    def _():
        carry_ref[...] = jnp.zeros_like(carry_ref)

    x = x_ref[0]
    h = _rms(x, g_ref[2:3, :]) * (1.0 + mod_ref[0, 4:5, :]) + mod_ref[0, 3:4, :]
    xn_ref[0] = h
    h_hi = h.astype(BF16)
    h_lo = (h - h_hi.astype(F32)).astype(BF16)
    logits = _bdot(jnp.concatenate([h_hi, h_hi, h_lo], axis=1), wr_ref[...])
    lane = lax.broadcasted_iota(jnp.int32, (tm, LANES), 1).astype(F32)
    neg = jnp.float32(-jnp.inf)
    lm = jnp.where(lane < N_EXPERTS, logits, neg)
    m1 = jnp.max(lm, axis=-1, keepdims=True)
    i1 = jnp.min(jnp.where(lm == m1, lane, float(LANES)), axis=-1, keepdims=True)
    lm2 = jnp.where(lane == i1, neg, lm)
    m2 = jnp.max(lm2, axis=-1, keepdims=True)
    i2 = jnp.min(jnp.where(lm2 == m2, lane, float(LANES)), axis=-1, keepdims=True)
    e = jnp.exp(m2 - m1)
    p1 = 1.0 / (1.0 + e)
    p2 = e / (1.0 + e)
    hot1 = (lane == i1).astype(F32)
    hot2 = (lane == i2).astype(F32)
    both = hot1 + hot2
    before = _bdot(tri_ref[...], both.astype(BF16)) + carry_ref[0:1, :]
    r1 = jnp.sum(before * hot1, axis=-1, keepdims=True)
    r2 = jnp.sum(before * hot2, axis=-1, keepdims=True)
    carry_ref[0:1, :] = carry_ref[0:1, :] + jnp.sum(both, axis=0, keepdims=True)
    cnt_ref[...] = carry_ref[...]
    vals = (i1, i2, p1, p2, r1, r2)
    meta = jnp.zeros((tm, LANES), F32)
    for k, val in enumerate(vals):
        meta = jnp.where(lane == k, val, meta)
    meta_ref[0] = meta
    metat_ref[...] = meta.T[0:8, :]


def _router(x, mods, g4, w_router):
    bsz, length, d = x.shape
    tm = min(ROW_TILE, length)
    wr = jnp.concatenate([w_router, jnp.zeros((d, LANES - N_EXPERTS), F32)], axis=1)
    wr_hi = wr.astype(BF16)
    wr_lo = (wr - wr_hi.astype(F32)).astype(BF16)
    wr = jnp.concatenate([wr_hi, wr_lo, wr_hi], axis=0)
    tri =jnp.asarray(np.tril(np.ones((tm, tm), np.float32), -1), dtype=BF16)
    const = lambda b, i: (0, 0)
    return pl.pallas_call(
        functools.partial(_router_kernel, tm=tm),
        grid=(bsz, length // tm),
        in_specs=[
            pl.BlockSpec((1, tm, d), lambda b, i: (b, i, 0)),
            pl.BlockSpec((1, 6, d), lambda b, i: (b, 0, 0)),
            pl.BlockSpec((4, d), const),
            pl.BlockSpec((3 * d, LANES), const),
            pl.BlockSpec((tm, tm), const),
        ],
        out_specs=[
            pl.BlockSpec((1, tm, d), lambda b, i: (b, i, 0)),
            pl.BlockSpec((1, tm, LANES), lambda b, i: (b, i, 0)),
            pl.BlockSpec((8, tm), lambda b, i: (0, b * (length // tm) + i)),
            pl.BlockSpec((8, LANES), const),
        ],
        out_shape=[
            jax.ShapeDtypeStruct((bsz, length, d), F32),
            jax.ShapeDtypeStruct((bsz, length, LANES), F32),
            jax.ShapeDtypeStruct((8, bsz * length), F32),
            jax.ShapeDtypeStruct((8, LANES), F32),
        ],
        scratch_shapes=[pltpu.VMEM((8, LANES), F32)],
        compiler_params=_params(("arbitrary", "arbitrary")),
        name="moe_router",
    )(x, mods, g4, wr, tri)


def _expert_kernel(te_ref, nv_ref, x_ref, w1_ref, w3_ref, w2_ref, o_ref, acc_ref):
    i = pl.program_id(0)
    j = pl.program_id(1)

    @pl.when(i < nv_ref[0])
    def _():
        y = _swiglu_halves(x_ref[...].astype(BF16), w1_ref, w3_ref, w2_ref, lead=(0,))

        @pl.when(j == 0)
        def _():
            acc_ref[...] = y

        @pl.when(j > 0)
        def _():
            acc_ref[...] += y

        @pl.when(j == pl.num_programs(1) - 1)
        def _():
            o_ref[...] = acc_ref[...]


def _experts(x_sorted, tile_expert, n_valid, w1, w3, w2):
    rows = x_sorted.shape[0]
    d = w1.shape[1]
    dff = w1.shape[2]
    tm = EXPERT_ROW_TILE
    n_ff = 2
    tf = dff // n_ff

    def row_map(i, j, te, nv):
        return (jnp.maximum(jnp.minimum(i, nv[0] - 1), 0), 0)

    def w13_map(i, j, te, nv):
        return (te[jnp.maximum(jnp.minimum(i, nv[0] - 1), 0)], 0, jnp.where(i < nv[0], j, n_ff - 1))

    def w2_map(i, j, te, nv):
        return (te[jnp.maximum(jnp.minimum(i, nv[0] - 1), 0)], jnp.where(i < nv[0], j, n_ff - 1), 0)

    return pl.pallas_call(
        _expert_kernel,
        grid_spec=pltpu.PrefetchScalarGridSpec(
            num_scalar_prefetch=2,
            grid=(rows // tm, n_ff),
            in_specs=[
                pl.BlockSpec((tm, d), row_map),
                pl.BlockSpec((1, d, tf), w13_map),
                pl.BlockSpec((1, d, tf), w13_map),
                pl.BlockSpec((1, tf, d), w2_map),
            ],
            out_specs=pl.BlockSpec((tm, d), row_map),
            scratch_shapes=[pltpu.VMEM((tm, d), F32)],
        ),
        out_shape=jax.ShapeDtypeStruct((rows, d), F32),
        compiler_params=_params(("arbitrary", "arbitrary")),
        name="expert_swiglu",
    )(tile_expert, n_valid, x_sorted, w1, w3, w2)


def _combine_kernel(x_ref, y_ref, meta_ref, g_ref, mod_ref, o_ref):
    meta = meta_ref[0]
    y = meta[:, 2:3] * y_ref[0, 0] + meta[:, 3:4] * y_ref[1, 0]
    o_ref[0] = x_ref[0] + mod_ref[0, 5:6, :] * _rms(y, g_ref[3:4, :])


def _combine(x, y2, meta, g4, mods):
    bsz, length, d = x.shape
    tm = min(ROW_TILE, length)
    return pl.pallas_call(
        _combine_kernel,
        grid=(bsz, length // tm),
        in_specs=[
            pl.BlockSpec((1, tm, d), lambda b, i: (b, i, 0)),
            pl.BlockSpec((2, 1, tm, d), lambda b, i: (0, b, i, 0)),
            pl.BlockSpec((1, tm, LANES), lambda b, i: (b, i, 0)),
            pl.BlockSpec((4, d), lambda b, i: (0, 0)),
            pl.BlockSpec((1, 6, d), lambda b, i: (b, 0, 0)),
        ],
        out_specs=pl.BlockSpec((1, tm, d), lambda b, i: (b, i, 0)),
        out_shape=jax.ShapeDtypeStruct((bsz, length, d), F32),
        compiler_params=_params(("parallel", "parallel")),
        name="moe_combine",
    )(x, y2, meta, g4, mods)


SC_CORES = 2
SC_SUBCORES = 16
SC_WORKERS = SC_CORES * SC_SUBCORES
SC_ROW_CHUNK = 32


def _sc_mesh():
    return plsc.VectorSubcoreMesh(core_axis_name="c", subcore_axis_name="s")


def _sc_dispatch(rows_in, pos2, n_out):
    tokens, width = rows_in.shape
    per_worker = tokens // SC_WORKERS
    chunk = SC_ROW_CHUNK
    n_pairs = per_worker // (2 * chunk)
    idx = pos2.reshape(2, SC_WORKERS, 2 * n_pairs, chunk)

    @functools.partial(
        pl.kernel, mesh=_sc_mesh(),
        out_type=jax.ShapeDtypeStruct((n_out, width), rows_in.dtype),
        scratch_types=[
            pltpu.VMEM((2 * n_pairs, chunk), jnp.int32),
            pltpu.VMEM((2 * n_pairs, chunk), jnp.int32),
            pltpu.VMEM((2, chunk, width), rows_in.dtype),
        ] + [pltpu.SemaphoreType.DMA] * 6,
        name="sc_dispatch",
    )
    def body(rows_hbm, idx_hbm, out_hbm, idx0_v, idx1_v, rows_v, rsem0, rsem1, wsem0, wsem1, wsem2, wsem3):
        wid = lax.axis_index("s") * SC_CORES + lax.axis_index("c")
        base = wid * per_worker
        rsem = (rsem0, rsem1)
        wsem = ((wsem0, wsem1), (wsem2, wsem3))
        pltpu.sync_copy(idx_hbm.at[0, wid], idx0_v)
        pltpu.sync_copy(idx_hbm.at[1, wid], idx1_v)

        @pl.loop(0, n_pairs)
        def _(p):
            c0 = 2 * p
            reads = [pltpu.async_copy(rows_hbm.at[pl.ds(base + (c0 + s) * chunk, chunk)], rows_v.at[s], rsem[s])
                     for s in range(2)]
            writes = []
            for s in range(2):
                reads[s].wait()
                writes.append(pltpu.async_copy(rows_v.at[s], out_hbm.at[idx0_v.at[c0 + s]], wsem[s][0]))
                writes.append(pltpu.async_copy(rows_v.at[s], out_hbm.at[idx1_v.at[c0 + s]], wsem[s][1]))
            for w in writes:
                w.wait()

    return body(rows_in, idx)


def _sc_gather(table, idx):
    n = idx.shape[0]
    width = table.shape[1]
    chunk = SC_ROW_CHUNK
    per_worker = n // SC_WORKERS
    n_pairs = per_worker // (2 * chunk)
    idx3 = idx.reshape(SC_WORKERS, 2 * n_pairs, chunk)

    @functools.partial(
        pl.kernel, mesh=_sc_mesh(),
        out_type=jax.ShapeDtypeStruct((n, width), table.dtype),
        scratch_types=[
            pltpu.VMEM((2 * n_pairs, chunk), jnp.int32),
            pltpu.VMEM((2, chunk, width), table.dtype),
            pltpu.SemaphoreType.DMA, pltpu.SemaphoreType.DMA,
            pltpu.SemaphoreType.DMA, pltpu.SemaphoreType.DMA,
        ],
        name="sc_gather",
    )
    def body(table_hbm, idx_hbm, out_hbm, idx_v, rows_v, gsem0, gsem1, wsem0, wsem1):
        wid = lax.axis_index("s") * SC_CORES + lax.axis_index("c")
        base = wid * per_worker
        gsem = (gsem0, gsem1)
        wsem = (wsem0, wsem1)
        pltpu.sync_copy(idx_hbm.at[wid], idx_v)

        @pl.loop(0, n_pairs)
        def _(p):
            c0 = 2 * p
            gathers = [pltpu.async_copy(table_hbm.at[idx_v.at[c0 + s]], rows_v.at[s], gsem[s]) for s in range(2)]
            writes = []
            for s in range(2):
                gathers[s].wait()
                rows_out = out_hbm.at[pl.ds(base + (c0 + s) * chunk, chunk)]
                writes.append(pltpu.async_copy(rows_v.at[s], rows_out, wsem[s]))
            for s in range(2):
                writes[s].wait()

    return body(table, idx3)


def _moe(x, mods, g4, w_router, w1, w3, w2):
    bsz, length, d = x.shape
    tokens = bsz * length
    tm = EXPERT_ROW_TILE
    xn, meta, meta_t, counts = _router(x, mods, g4, w_router)
    experts = meta_t[0:2].astype(jnp.int32)
    ranks = meta_t[4:6].astype(jnp.int32)
    cnt = counts[0, 0:N_EXPERTS].astype(jnp.int32)
    padded = ((cnt + tm - 1) // tm) * tm
    ends = jnp.cumsum(padded)
    starts = ends - padded
    group_start = jnp.zeros_like(experts)
    for e in range(N_EXPERTS):
        group_start = jnp.where(experts == e, starts[e], group_start)
    pos = (group_start + ranks).reshape(2 * tokens)
    rows = 2 * tokens + N_EXPERTS * tm
    n_tiles = rows // tm
    n_valid = (ends[-1] // tm).astype(jnp.int32).reshape(1)
    tile_start = jnp.arange(n_tiles, dtype=jnp.int32) * tm
    tile_expert = jnp.minimum(
        jnp.sum((tile_start[:, None] >= ends[None, :]).astype(jnp.int32), axis=1), N_EXPERTS - 1)
    x_sorted = _sc_dispatch(xn.reshape(tokens, d), pos.reshape(2, tokens), rows)
    y_sorted = _experts(x_sorted, tile_expert, n_valid, w1, w3, w2)
    y2 = _sc_gather(y_sorted, pos).reshape(2, bsz, length, d)
    return _combine(x, y2, meta, g4, mods)


def _grid_pos_embed(rows, d):
    row = np.repeat(np.arange(rows, dtype=np.float64), GRID_W)
    col = np.tile(np.arange(GRID_W, dtype=np.float64), rows)
    n_freq = d // 4
    omega = POS_THETA ** (-np.arange(n_freq, dtype=np.float64) / n_freq)
    ang_r = row[:, None] * omega
    ang_c = col[:, None] * omega
    return jnp.asarray(np.concatenate([np.sin(ang_r), np.cos(ang_r), np.sin(ang_c), np.cos(ang_c)], axis=-1),
                       dtype=F32)


def kernel(x, c, ctx, c_ctx, ada_w, ada_b, norm_g, ab_w_in, rg_conv_w, rg_conv_b, rg_w_r, rg_b_r, rg_w_i, rg_b_i, rg_lambda, cv_dw_w, cv_dw_b, cv_ln_g, cv_ln_b, ab_w_out, ffn_w1, ffn_w3, ffn_w2, cd_w_in, sg_ln_g, sg_ln_b, sg_w_s, sg_b_s, cd_w_out, moe_router, moe_w1, moe_w3, moe_w2):
    bsz, length, d = x.shape
    assert ada_w.shape[0] == 2, "two layers: one even (RG-LRU | Conformer), one odd (gMLP | Fourier)"
    pos = _grid_pos_embed(length // GRID_W, d)

    cpad = jnp.concatenate([c, c_ctx[None, :], jnp.zeros((16 - bsz - 1, d), F32)], axis=0)
    m = _ada(cpad, ada_w, ada_b)
    mods = m[:, :bsz].reshape(2, bsz, 6, d)
    mods_ctx = m[:, bsz:bsz + 1].reshape(2, 1, 6, d)

    mods_t = mods[0].transpose(1, 0, 2)
    mods_ctx_t = jnp.broadcast_to(mods_ctx[0].reshape(6, 1, d), (6, bsz, d))
    pos3 = pos[:, None, :]
    w_in = ab_w_in[0]
    xa3, gg3, glu3, x0_t, ffn1b, ffn3b, ffn2b = _inproj_t(
        x, pos3, mods_t, norm_g[0], w_in, rg_conv_w[0], rg_conv_b[0], True, riders=(ffn_w1[0], ffn_w3[0], ffn_w2[0]))
    (xc3,) = _inproj_t(ctx, None, mods_ctx_t, norm_g[0], w_in[:, 0:A_WIDTH], rg_conv_w[0], rg_conv_b[0], False)
    wg, bg, sp = _gate_weights(rg_w_r[0], rg_b_r[0], rg_w_i[0], rg_b_i[0], rg_lambda[0])
    n_exp, _, dff_e = moe_w1[0].shape
    rec_hi, rec_lo, w2b = _scan_t(xa3, xc3, gg3, wg, bg, sp, riders=(moe_w2[0].reshape(n_exp * dff_e, d),))
    u3, w1b = _conv_t(glu3, cv_dw_w[0], cv_dw_b[0], cv_ln_g[0], cv_ln_b[0],
                      riders=(moe_w1[0].reshape(n_exp * d, dff_e),))
    x1_t = _outproj_t(rec_lo, rec_hi, u3, x0_t, ab_w_out[0], norm_g[0], mods_t)
    x2, w3b = _ffn_t(x1_t, mods_t, norm_g[0], ffn1b, ffn3b, ffn2b, riders=(moe_w3[0].reshape(n_exp * d, dff_e),))

    u, v, f = _inproj_cd(x2, mods[1], norm_g[1], cd_w_in[0])
    fo = _fourier(f)
    x3 = _gate_out(u, v, fo, x2, sg_ln_g[0], sg_ln_b[0], sg_w_s[0], sg_b_s[0], cd_w_out[0],
                   norm_g[1], mods[1])
    return _moe(x3, mods[1], norm_g[1], moe_router[0], w1b.reshape(n_exp, d, dff_e), w3b.reshape(n_exp, d, dff_e),
                w2b.reshape(n_exp, dff_e, d))
```

```python
import functools

import numpy as np
import jax
import jax.numpy as jnp
from jax import lax
from jax.experimental import pallas as pl
from jax.experimental.pallas import tpu as pltpu
from jax.experimental.pallas import tpu_sc as plsc

F32 = jnp.float32
BF16 = jnp.bfloat16

GRID_W = 64
POS_THETA = 10000.0
NORM_EPS = 1e-6
LN_EPS = 1e-5
LRU_C = 8.0

A_WIDTH = 512
A_HEADS = 8
A_CONV_W = 4
B_WIDTH = 512
B_CONV_W = 31
C_WIDTH = 512
C_HEADS = 8
CHUNK = 128
D_WIDTH = 512
D_GROUPS = 4
D_GROUP_DIM = D_WIDTH // D_GROUPS
N_EXPERTS = 8

LANES = 128
ROW_TILE = 512
EXPERT_ROW_TILE = 512
VMEM_LIMIT = 56 * 2 ** 20


def _params(sem):
    return pltpu.CompilerParams(dimension_semantics=sem, vmem_limit_bytes=VMEM_LIMIT)


def _rms(x, g):
    return x * lax.rsqrt(jnp.mean(x * x, axis=-1, keepdims=True) + NORM_EPS) * g


def _bdot(a, b):
    return jnp.dot(a, b, preferred_element_type=F32)


MXU_WIDTH = 256


def _swiglu_halves(x, w1_ref, w3_ref, w2_ref, lead=()):
    n = w1_ref.shape[-1]
    half = -(-(n // 2) // MXU_WIDTH) * MXU_WIDTH
    y = None
    for lo, hi in ((0, half), (half, n)):
        a = _bdot(x, w1_ref[lead + (slice(None), slice(lo, hi))])
        b = _bdot(x, w3_ref[lead + (slice(None), slice(lo, hi))])
        t = (jax.nn.silu(a) * b).astype(BF16)
        part = _bdot(t, w2_ref[lead + (slice(lo, hi), slice(None))])
        y = part if y is None else y + part
    return y


def _ada_kernel(c_ref, w_ref, b_ref, o_ref):
    s = jax.nn.silu(c_ref[...])
    o_ref[0] = _bdot(s.astype(BF16), w_ref[0].astype(BF16)) + b_ref[0]


def _ada(cpad, ada_w, ada_b):
    nl, d, n6 = ada_w.shape
    rows = cpad.shape[0]
    tn = n6 // 4
    return pl.pallas_call(
        _ada_kernel,
        grid=(nl, n6 // tn),
        in_specs=[
            pl.BlockSpec((rows, d), lambda l, j: (0, 0)),
            pl.BlockSpec((1, d, tn), lambda l, j: (l, 0, j)),
            pl.BlockSpec((1, 1, tn), lambda l, j: (l, 0, j)),
        ],
        out_specs=pl.BlockSpec((1, rows, tn), lambda l, j: (l, 0, j)),
        out_shape=jax.ShapeDtypeStruct((nl, rows, n6), F32),
        compiler_params=_params(("arbitrary", "arbitrary")),
        name="ada_terms",
    )(cpad, ada_w, ada_b.reshape(nl, 1, n6))


TIME_TILE = 128
SCAN_TIME_TILE = 128
SCAN_LANES = 256
CONV_TIME_TILE = 128
CONV_HALO = 16


def _time_major(ref):
    return jnp.swapaxes(ref[...], 0, 1)


def _inproj_t_kernel(*refs, tt, add_pos, branches, n_cast):
    if n_cast:
        n_in = len(refs) - n_cast - (4 if branches else 1) - n_cast
        _cast_riders(refs[n_in:n_in + n_cast], refs[len(refs) - n_cast:])
        refs = refs[:n_in] + refs[n_in + n_cast:len(refs) - n_cast]
    if add_pos:
        x_ref, xp_ref, xn_ref, pos_ref, posp_ref, posn_ref, mod_ref, g_ref, w_ref, cw_ref, cb_ref, *outs = refs
        xall = jnp.concatenate(
            [_time_major(xp_ref)[7:8] + posp_ref[...], _time_major(x_ref) + pos_ref[...],
             _time_major(xn_ref)[0:2] + posn_ref[...]], axis=0)
    else:
        x_ref, xp_ref, xn_ref, mod_ref, g_ref, w_ref, cw_ref, cb_ref, *outs = refs
        xall = jnp.concatenate([_time_major(xp_ref)[7:8], _time_major(x_ref), _time_major(xn_ref)[0:2]], axis=0)
    i = pl.program_id(0)
    last = pl.num_programs(0) - 1
    steps, bsz, d = xall.shape
    h = _rms(xall, g_ref[0:1, :]) * (1.0 + mod_ref[1]) + mod_ref[0]
    z = _bdot(h.reshape(steps * bsz, d).astype(BF16), w_ref[...].astype(BF16))
    z3 = z.reshape(steps, bsz, z.shape[-1])
    t_idx = lax.broadcasted_iota(jnp.int32, (steps, bsz, A_WIDTH), 0)
    inside = jnp.logical_and(jnp.logical_or(t_idx >= 1, i > 0), jnp.logical_or(t_idx <= tt, i < last))
    xa = jnp.where(inside, z3[:, :, 0:A_WIDTH], 0.0)
    xconv = cb_ref[...] + cw_ref[0:1, :] * xa[0:tt]
    for k in range(1, A_CONV_W):
        xconv = xconv + cw_ref[k:k + 1, :] * xa[k:k + tt]
    outs[0][...] = xconv
    if branches:
        outs[1][...] = jax.nn.gelu(z3[1:tt + 1, :, A_WIDTH:2 * A_WIDTH])
        vb = z3[1:tt + 1, :, 2 * A_WIDTH:2 * A_WIDTH + B_WIDTH]
        gb = z3[1:tt + 1, :, 2 * A_WIDTH + B_WIDTH:]
        outs[2][...] = vb * jax.nn.sigmoid(gb)
        outs[3][...] = xall[1:tt + 1]


def _inproj_t(x, pos3, mods_t, g4, w, conv_w, conv_b, branches, riders=()):
    bsz, length, d = x.shape
    n = w.shape[1]
    tt = min(TIME_TILE, length)
    add_pos = pos3 is not None
    n_out = 3 if branches else 1
    prev_map = lambda i: (jnp.maximum(i * tt - 1, 0), 0, 0)
    next_map = lambda i: (jnp.minimum((i + 1) * (tt // 2), length // 2 - 1), 0, 0)
    in_specs = [
        pl.BlockSpec((bsz, tt, d), lambda i: (0, i, 0)),
        pl.BlockSpec((bsz, 8, d), lambda i: (0, jnp.maximum(i * (tt // 8) - 1, 0), 0)),
        pl.BlockSpec((bsz, 8, d), lambda i: (0, jnp.minimum((i + 1) * (tt // 8), length // 8 - 1), 0)),
    ]
    args = [x, x, x]
    if add_pos:
        in_specs += [
            pl.BlockSpec((tt, 1, d), lambda i: (i, 0, 0)),
            pl.BlockSpec((1, 1, d), prev_map),
            pl.BlockSpec((2, 1, d), next_map),
        ]
        args += [pos3, pos3, pos3]
    in_specs += [
        pl.BlockSpec((6, bsz, d), lambda i: (0, 0, 0)),
        pl.BlockSpec((4, d), lambda i: (0, 0)),
        pl.BlockSpec((d, n), lambda i: (0, 0)),
        pl.BlockSpec((A_CONV_W, A_WIDTH), lambda i: (0, 0)),
        pl.BlockSpec((1, A_WIDTH), lambda i: (0, 0)),
    ]
    args += [mods_t, g4, w, conv_w, conv_b.reshape(1, A_WIDTH)]
    r_in, r_out, r_shapes = _rider_specs(riders, length // tt, lambda i: i)
    out_specs = [pl.BlockSpec((tt, bsz, A_WIDTH), lambda i: (i, 0, 0))] * n_out
    out_shapes = [jax.ShapeDtypeStruct((length, bsz, A_WIDTH), F32)] * n_out
    if branches:
        out_specs.append(pl.BlockSpec((tt, bsz, d), lambda i: (i, 0, 0)))
        out_shapes.append(jax.ShapeDtypeStruct((length, bsz, d), F32))
    return pl.pallas_call(
        functools.partial(_inproj_t_kernel, tt=tt, add_pos=add_pos, branches=branches, n_cast=len(riders)),
        grid=(length // tt,),
        in_specs=in_specs + r_in,
        out_specs=out_specs + r_out,
        out_shape=out_shapes + r_shapes,
        compiler_params=_params(("parallel",)),
        name="inproj_ab" if branches else "inproj_ctx",
    )(*args, *riders)


def _cast_riders(cast_in, cast_out):
    for src, dst in zip(cast_in, cast_out):
        dst[...] = src[...].astype(BF16)


def _rider_specs(arrays, n_steps, step_of):
    in_specs, out_specs, out_shapes = [], [], []
    for arr in arrays:
        rows, cols = arr.shape
        block = (rows // n_steps, cols)
        in_specs.append(pl.BlockSpec(block, lambda *idx: (step_of(*idx), 0)))
        out_specs.append(pl.BlockSpec(block, lambda *idx: (step_of(*idx), 0)))
        out_shapes.append(jax.ShapeDtypeStruct(arr.shape, BF16))
    return in_specs, out_specs, out_shapes


def _scan_t_kernel(*refs, tb, nb, ctx_len, n_cast):
    xf_ref, xb_ref, ggf_ref, ggb_ref, xc_ref, wg_ref, bg_ref, sp_ref = refs[:8]
    cast_in = refs[8:8 + n_cast]
    hi_ref, lo_ref = refs[8 + n_cast:10 + n_cast]
    cast_out = refs[10 + n_cast:10 + 2 * n_cast]
    a_ref, b_ref, h_ref, state_ref = refs[10 + 2 * n_cast:]
    _cast_riders(cast_in, cast_out)
    k = pl.program_id(1)
    half = nb // 2
    bsz, lanes = xf_ref.shape[1], xf_ref.shape[2]
    chunk = 128

    def coefficients(x_ref, n, direction):
        def body(c, _):
            t0 = pl.multiple_of(c * chunk, chunk)
            x = x_ref[pl.ds(t0, chunk)].reshape(chunk * bsz, lanes)
            th = jnp.tanh(_bdot(x.astype(BF16), wg_ref[0, direction]) + bg_ref[0, direction])
            i = 0.5 * th[:, lanes:] + 0.5
            c = (-0.5 * LRU_C) * sp_ref[0, direction:direction + 1, :]
            log_a = c * th[:, 0:lanes] + c
            a = jnp.exp(log_a)
            one_minus_a2 = -jnp.tanh(log_a) * (a * a + 1.0)
            a_ref[direction, pl.ds(t0, chunk)] = a.reshape(chunk, bsz, lanes)
            b_ref[direction, pl.ds(t0, chunk)] = (jnp.sqrt(one_minus_a2) * (i * x)).reshape(chunk, bsz, lanes)
            return 0
        lax.fori_loop(0, n // chunk, body, 0)

    def sweep(n, store):
        def body(t, carry):
            hf, hb = carry
            tr = n - 1 - t
            hf = a_ref[0, pl.ds(t, 1)][0] * hf + b_ref[0, pl.ds(t, 1)][0]
            hb = a_ref[1, pl.ds(tr, 1)][0] * hb + b_ref[1, pl.ds(tr, 1)][0]
            if store:
                b_ref[0, pl.ds(t, 1)] = hf[None]
                b_ref[1, pl.ds(tr, 1)] = hb[None]
            return hf, hb
        hf, hb = lax.fori_loop(0, n, body, (state_ref[0], state_ref[1]), unroll=8)
        state_ref[0] = hf
        state_ref[1] = hb

    @pl.when(k == 0)
    def _():
        state_ref[...] = jnp.zeros_like(state_ref)
        coefficients(xc_ref, ctx_len, 0)
        coefficients(xc_ref, ctx_len, 1)
        sweep(ctx_len, False)

    coefficients(xf_ref, tb, 0)
    coefficients(xb_ref, tb, 1)
    sweep(tb, True)
    m = nb - 1 - k

    @pl.when(k < half)
    def _():
        h_ref[k] = b_ref[0, 0:tb]
        h_ref[m] = b_ref[1, 0:tb]

    @pl.when(k >= half)
    def _():
        hi_ref[...] = (b_ref[0, 0:tb] + h_ref[k]) * ggf_ref[...]
        lo_ref[...] = (h_ref[m] + b_ref[1, 0:tb]) * ggb_ref[...]


def _scan_t(xa3, xc3, gg3, wg, bg, sp, riders=()):
    length, bsz, width = xa3.shape
    ctx_len = xc3.shape[0]
    tb = min(SCAN_TIME_TILE, length // 2)
    nb = length // tb
    half = nb // 2
    lanes = SCAN_LANES
    blk = (tb, bsz, lanes)
    r_in, r_out, r_shapes = _rider_specs(riders, (width // lanes) * nb, lambda g, k: g * nb + k)
    return pl.pallas_call(
        functools.partial(_scan_t_kernel, tb=tb, nb=nb, ctx_len=ctx_len, n_cast=len(riders)),
        grid=(width // lanes, nb),
        in_specs=[
            pl.BlockSpec(blk, lambda g, k: (k, 0, g)),
            pl.BlockSpec(blk, lambda g, k: (nb - 1 - k, 0, g)),
            pl.BlockSpec(blk, lambda g, k: (jnp.maximum(k, half), 0, g)),
            pl.BlockSpec(blk, lambda g, k: (jnp.minimum(nb - 1 - k, half - 1), 0, g)),
            pl.BlockSpec((ctx_len, bsz, lanes), lambda g, k: (0, 0, g)),
            pl.BlockSpec((1, 2, lanes, 2 * lanes), lambda g, k: (g, 0, 0, 0)),
            pl.BlockSpec((1, 2, 1, 2 * lanes), lambda g, k: (g, 0, 0, 0)),
            pl.BlockSpec((1, 2, lanes), lambda g, k: (g, 0, 0)),
        ] + r_in,
        out_specs=[
            pl.BlockSpec(blk, lambda g, k: (jnp.maximum(k - half, 0), 0, g)),
            pl.BlockSpec(blk, lambda g, k: (jnp.minimum(nb - 1 - k, half - 1), 0, g)),
        ] + r_out,
        out_shape=[jax.ShapeDtypeStruct((length // 2, bsz, width), F32)] * 2 + r_shapes,
        scratch_shapes=[
            pltpu.VMEM((2, max(tb, ctx_len), bsz, lanes), F32),
            pltpu.VMEM((2, max(tb, ctx_len), bsz, lanes), F32),
            pltpu.VMEM((nb, tb, bsz, lanes), F32),
            pltpu.VMEM((2, bsz, lanes), F32),
        ],
        compiler_params=_params(("parallel", "arbitrary")),
        name="rglru_scan",
    )(xa3, xa3, gg3, gg3, xc3, wg, bg, sp, *riders)


def _gate_weights(w_r, b_r, w_i, b_i, lam):
    hd = A_WIDTH // A_HEADS
    heads_per_group = SCAN_LANES // hd
    ngroups = A_WIDTH // SCAN_LANES

    def blockdiag(w):
        w = w.reshape(ngroups, heads_per_group, hd, hd)
        eye = jnp.eye(heads_per_group, dtype=w.dtype)
        return jnp.einsum("ghde,hk->ghdke", w, eye).reshape(ngroups, SCAN_LANES, SCAN_LANES)

    wg = jnp.stack([jnp.concatenate([blockdiag(w_r[d]), blockdiag(w_i[d])], axis=-1) for d in range(2)], axis=1)
    bg = jnp.stack([jnp.concatenate([b_r[d].reshape(ngroups, 1, SCAN_LANES), b_i[d].reshape(ngroups, 1, SCAN_LANES)],
                                    axis=-1) for d in range(2)], axis=1)
    sp = jax.nn.softplus(-lam.astype(F32)).reshape(2, ngroups, SCAN_LANES).transpose(1, 0, 2)
    return (0.5 * wg).astype(BF16), 0.5 * bg, sp


def _conv_t_kernel(*refs, tb, n_cast):
    x_ref, xp_ref, xn_ref, w_ref, b_ref, lng_ref, lnb_ref = refs[:7]
    cast_in = refs[7:7 + n_cast]
    o_ref = refs[7 + n_cast]
    cast_out = refs[8 + n_cast:8 + 2 * n_cast]
    stage_ref = refs[-1]
    _cast_riders(cast_in, cast_out)
    i = pl.program_id(0)
    last = pl.num_programs(0) - 1
    halo = CONV_HALO
    stage_ref[0:halo] = jnp.where(i > 0, xp_ref[...], 0.0)
    stage_ref[halo:halo + tb] = x_ref[...]
    stage_ref[halo + tb:2 * halo + tb] = jnp.where(i < last, xn_ref[...], 0.0)
    sub = 8

    def body(r, _):
        t0 = pl.multiple_of(r * sub, sub)
        def tap(k, acc):
            return acc + w_ref[k] * stage_ref[pl.ds(t0 + 1 + k, sub)]
        acc = lax.fori_loop(0, B_CONV_W + 1, tap, jnp.broadcast_to(b_ref[...], (sub,) + b_ref.shape), unroll=8)
        o_ref[pl.ds(t0, sub)] = acc
        return 0
    lax.fori_loop(0, tb // sub, body, 0)

    norm_rows = 64

    def norm_body(r, _):
        t0 = pl.multiple_of(r * norm_rows, norm_rows)
        acc = o_ref[pl.ds(t0, norm_rows)]
        mu = jnp.mean(acc, axis=-1, keepdims=True)
        cen = acc - mu
        var = jnp.mean(cen * cen, axis=-1, keepdims=True)
        y = cen * lax.rsqrt(var + LN_EPS) * lng_ref[...] + lnb_ref[...]
        o_ref[pl.ds(t0, norm_rows)] = jax.nn.silu(y)
        return 0
    lax.fori_loop(0, tb // norm_rows, norm_body, 0)


def _conv_t(glu3, dw_w, dw_b, ln_g, ln_b, riders=()):
    length, bsz, width = glu3.shape
    tb = min(CONV_TIME_TILE, length)
    halo = CONV_HALO
    taps = dw_w.shape[0]
    w8 = jnp.broadcast_to(jnp.concatenate([dw_w, jnp.zeros((32 - taps, width), F32)], axis=0)[:, None, :],
                          (32, bsz, width))
    b8 = jnp.broadcast_to(dw_b[None, :], (bsz, width))
    r_in, r_out, r_shapes = _rider_specs(riders, length // tb, lambda i: i)
    return pl.pallas_call(
        functools.partial(_conv_t_kernel, tb=tb, n_cast=len(riders)),
        grid=(length // tb,),
        in_specs=[
            pl.BlockSpec((tb, bsz, width), lambda i: (i, 0, 0)),
            pl.BlockSpec((halo, bsz, width), lambda i: (jnp.maximum(i * (tb // halo) - 1, 0), 0, 0)),
            pl.BlockSpec((halo, bsz, width), lambda i: (jnp.minimum((i + 1) * (tb // halo), length // halo - 1), 0, 0)),
            pl.BlockSpec((32, bsz, width), lambda i: (0, 0, 0)),
            pl.BlockSpec((bsz, width), lambda i: (0, 0)),
            pl.BlockSpec((1, width), lambda i: (0, 0)),
            pl.BlockSpec((1, width), lambda i: (0, 0)),
        ] + r_in,
        out_specs=[pl.BlockSpec((tb, bsz, width), lambda i: (i, 0, 0))] + r_out,
        out_shape=[jax.ShapeDtypeStruct((length, bsz, width), F32)] + r_shapes,
        scratch_shapes=[pltpu.VMEM((tb + 2 * halo, bsz, width), F32)],
        compiler_params=_params(("parallel",)),
        name="conformer_conv",
    )(glu3, glu3, glu3, w8, b8, ln_g.reshape(1, -1), ln_b.reshape(1, -1), *riders)


def _outproj_t_kernel(lo_ref, hi_ref, u_ref, x_ref, wout_ref, g_ref, mod_ref, o_ref, *, half_steps):
    i = pl.program_id(0)
    tt, bsz, d = x_ref.shape
    rec = jnp.where(i < half_steps, lo_ref[...], hi_ref[...])
    rows = tt * bsz
    y = (_bdot(rec.reshape(rows, A_WIDTH).astype(BF16), wout_ref[0:A_WIDTH, :].astype(BF16))
         + _bdot(u_ref[...].reshape(rows, B_WIDTH).astype(BF16), wout_ref[A_WIDTH:, :].astype(BF16)))
    yn = _rms(y, g_ref[1:2, :]).reshape(tt, bsz, d)
    o_ref[...] = x_ref[...] + mod_ref[2] * yn


def _outproj_t(rec_lo, rec_hi, u3, x0_t, w_out, g4, mods_t):
    length, bsz, d = x0_t.shape
    tt = min(TIME_TILE // 2, length // 2)
    half_steps = (length // 2) // tt
    return pl.pallas_call(
        functools.partial(_outproj_t_kernel, half_steps=half_steps),
        grid=(length // tt,),
        in_specs=[
            pl.BlockSpec((tt, bsz, A_WIDTH), lambda i: (jnp.minimum(i, half_steps - 1), 0, 0)),
            pl.BlockSpec((tt, bsz, A_WIDTH), lambda i: (jnp.maximum(i - half_steps, 0), 0, 0)),
            pl.BlockSpec((tt, bsz, B_WIDTH), lambda i: (i, 0, 0)),
            pl.BlockSpec((tt, bsz, d), lambda i: (i, 0, 0)),
            pl.BlockSpec((A_WIDTH + B_WIDTH, d), lambda i: (0, 0)),
            pl.BlockSpec((4, d), lambda i: (0, 0)),
            pl.BlockSpec((6, bsz, d), lambda i: (0, 0, 0)),
        ],
        out_specs=pl.BlockSpec((tt, bsz, d), lambda i: (i, 0, 0)),
        out_shape=jax.ShapeDtypeStruct((length, bsz, d), F32),
        compiler_params=_params(("parallel",)),
        name="mixer_ab_outproj",
    )(rec_lo, rec_hi, u3, x0_t, w_out, g4, mods_t)


def _ffn_t_kernel(x_ref, mod_ref, g_ref, w1_ref, w3_ref, w2_ref, *rest, n_cast):
    _cast_riders(rest[:n_cast], rest[n_cast + 1:])
    o_ref = rest[n_cast]
    x = x_ref[...]
    tt, bsz, d = x.shape
    h = (_rms(x, g_ref[2:3, :]) * (1.0 + mod_ref[4]) + mod_ref[3]).reshape(tt * bsz, d).astype(BF16)
    y = _swiglu_halves(h, w1_ref, w3_ref, w2_ref)
    o_ref[...] = jnp.swapaxes(x + mod_ref[5] * _rms(y, g_ref[3:4, :]).reshape(tt, bsz, d), 0, 1)


def _ffn_t(x3, mods_t, g4, w1, w3, w2, riders=()):
    length, bsz, d = x3.shape
    dff = w1.shape[1]
    tt = min(TIME_TILE // 2, length)
    r_in, r_out, r_shapes = _rider_specs(riders, length // tt, lambda i: i)
    return pl.pallas_call(
        functools.partial(_ffn_t_kernel, n_cast=len(riders)),
        grid=(length // tt,),
        in_specs=[
            pl.BlockSpec((tt, bsz, d), lambda i: (i, 0, 0)),
            pl.BlockSpec((6, bsz, d), lambda i: (0, 0, 0)),
            pl.BlockSpec((4, d), lambda i: (0, 0)),
            pl.BlockSpec((d, dff), lambda i: (0, 0), pipeline_mode=pl.Buffered(1)),
            pl.BlockSpec((d, dff), lambda i: (0, 0), pipeline_mode=pl.Buffered(1)),
            pl.BlockSpec((dff, d), lambda i: (0, 0), pipeline_mode=pl.Buffered(1)),
        ] + r_in,
        out_specs=[pl.BlockSpec((bsz, tt, d), lambda i: (0, i, 0))] + r_out,
        out_shape=[jax.ShapeDtypeStruct((bsz, length, d), F32)] + r_shapes,
        compiler_params=_params(("parallel",)),
        name="dense_swiglu",
    )(x3, mods_t, g4, w1, w3, w2, *riders)


def _inproj_cd_kernel(x_ref, mod_ref, g_ref, w_ref, u_ref, v_ref, f_ref):
    x = x_ref[0]
    h = _rms(x, g_ref[0:1, :]) * (1.0 + mod_ref[0, 1:2, :]) + mod_ref[0, 0:1, :]
    z = _bdot(h.astype(BF16), w_ref[...].astype(BF16))
    u_ref[0] = jax.nn.gelu(z[:, 0:C_WIDTH])
    v_ref[0] = jax.nn.gelu(z[:, C_WIDTH:2 * C_WIDTH])
    f_ref[0] = z[:, 2 * C_WIDTH:]


def _inproj_cd(x, mods, g4, w):
    bsz, length, d = x.shape
    n = w.shape[1]
    tm = min(ROW_TILE, length)
    return pl.pallas_call(
        _inproj_cd_kernel,
        grid=(bsz, length // tm),
        in_specs=[
            pl.BlockSpec((1, tm, d), lambda b, i: (b, i, 0)),
            pl.BlockSpec((1, 6, d), lambda b, i: (b, 0, 0)),
            pl.BlockSpec((4, d), lambda b, i: (0, 0)),
            pl.BlockSpec((d, n), lambda b, i: (0, 0)),
        ],
        out_specs=[pl.BlockSpec((1, tm, C_WIDTH), lambda b, i: (b, i, 0))] * 3,
        out_shape=[jax.ShapeDtypeStruct((bsz, length, C_WIDTH), F32)] * 3,
        compiler_params=_params(("parallel", "parallel")),
        name="inproj_cd",
    )(x, mods, g4, w)


def _fourier_kernel(f_ref, dmat_ref, lmat_ref, o_ref, z_ref, *, length, scale):
    j = pl.program_id(1)
    rows_per_step = 512 if length % 512 == 0 else length

    @pl.when(j == 0)
    def _():
        def body(c, _):
            rows = pl.ds(pl.multiple_of(c * rows_per_step, rows_per_step), rows_per_step)
            fb = f_ref[0, rows, :].astype(BF16)
            for g in range(D_GROUPS):
                cs = _bdot(fb[:, g * D_GROUP_DIM:(g + 1) * D_GROUP_DIM], dmat_ref[...])
                z_ref[0, rows, g * D_GROUP_DIM:(g + 1) * D_GROUP_DIM] = cs[:, 0:D_GROUP_DIM].astype(BF16)
                z_ref[1, rows, g * D_GROUP_DIM:(g + 1) * D_GROUP_DIM] = cs[:, D_GROUP_DIM:].astype(BF16)
            return 0
        lax.fori_loop(0, length // rows_per_step, body, 0)

    out = _bdot(lmat_ref[0], z_ref[0]) + _bdot(lmat_ref[1], z_ref[1])
    o_ref[0] = (out * scale).astype(BF16)


def _dft_matrices(length):
    k = np.arange(length, dtype=np.int64)
    ang_l = 2.0 * np.pi * ((k[:, None] * k[None, :]) % length).astype(np.float64) / length
    lmat = np.stack([np.cos(ang_l), -np.sin(ang_l)], axis=0)
    d = np.arange(D_GROUP_DIM, dtype=np.int64)
    ang_d = 2.0 * np.pi * ((d[:, None] * d[None, :]) % D_GROUP_DIM).astype(np.float64) / D_GROUP_DIM
    dmat = np.concatenate([np.cos(ang_d), np.sin(ang_d)], axis=1)
    return jnp.asarray(lmat, dtype=F32).astype(BF16), jnp.asarray(dmat, dtype=F32).astype(BF16)


def _fourier(f):
    bsz, length, width = f.shape
    tm = min(ROW_TILE, length)
    lmat, dmat = _dft_matrices(length)
    scale = float(1.0 / np.sqrt(length * D_GROUP_DIM))
    return pl.pallas_call(
        functools.partial(_fourier_kernel, length=length, scale=scale),
        grid=(bsz, length // tm),
        in_specs=[
            pl.BlockSpec((1, length, width), lambda b, i: (b, 0, 0)),
            pl.BlockSpec((D_GROUP_DIM, 2 * D_GROUP_DIM), lambda b, i: (0, 0)),
            pl.BlockSpec((2, tm, length), lambda b, i: (0, i, 0)),
        ],
        out_specs=pl.BlockSpec((1, tm, width), lambda b, i: (b, i, 0)),
        out_shape=jax.ShapeDtypeStruct((bsz, length, width), BF16),
        scratch_shapes=[pltpu.VMEM((2, length, width), BF16)],
        compiler_params=_params(("parallel", "arbitrary")),
        name="fourier_mix",
    )(f, dmat, lmat)


def _gate_out_kernel(u_ref, v_ref, fo_ref, x_ref, lng_ref, lnb_ref, wcat_ref, bs_ref, wout_ref, g_ref, mod_ref,
                     o_ref, vs_ref, gt_ref, *, tm):
    head_dim = C_WIDTH // C_HEADS
    head_of_lane = lax.broadcasted_iota(jnp.int32, (CHUNK, C_WIDTH), 1) // head_dim
    for ci in range(tm // CHUNK):
        rows = slice(ci * CHUNK, (ci + 1) * CHUNK)
        v = v_ref[0, rows, :]
        mu = jnp.mean(v, axis=-1, keepdims=True)
        cen = v - mu
        var = jnp.mean(cen * cen, axis=-1, keepdims=True)
        vn = (cen * lax.rsqrt(var + LN_EPS) * lng_ref[...] + lnb_ref[...]).astype(BF16)
        for h in range(C_HEADS):
            vs_ref[h * CHUNK:(h + 1) * CHUNK, :] = jnp.where(head_of_lane == h, vn, jnp.zeros_like(vn))
        mixed = _bdot(wcat_ref[...], vs_ref[...]) + bs_ref[...]
        gt_ref[rows, :] = (u_ref[0, rows, :] * mixed).astype(BF16)
    y = (_bdot(gt_ref[...], wout_ref[0:C_WIDTH, :].astype(BF16))
         + _bdot(fo_ref[0], wout_ref[C_WIDTH:, :].astype(BF16)))
    o_ref[0] = x_ref[0] + mod_ref[0, 2:3, :] * _rms(y, g_ref[1:2, :])


def _gate_out(u, v, fo, x, ln_g, ln_b, w_s, b_s, w_out, g4, mods):
    bsz, length, d = x.shape
    tm = min(ROW_TILE, length)
    head_dim = C_WIDTH // C_HEADS
    wcat = w_s.transpose(1, 0, 2).reshape(CHUNK, C_HEADS * CHUNK).astype(BF16)
    bs_full = jnp.repeat(b_s.T, head_dim, axis=1)
    const = lambda b, i: (0, 0)
    return pl.pallas_call(
        functools.partial(_gate_out_kernel, tm=tm),
        grid=(bsz, length // tm),
        in_specs=[
            pl.BlockSpec((1, tm, C_WIDTH), lambda b, i: (b, i, 0)),
            pl.BlockSpec((1, tm, C_WIDTH), lambda b, i: (b, i, 0)),
            pl.BlockSpec((1, tm, D_WIDTH), lambda b, i: (b, i, 0)),
            pl.BlockSpec((1, tm, d), lambda b, i: (b, i, 0)),
            pl.BlockSpec((1, C_WIDTH), const),
            pl.BlockSpec((1, C_WIDTH), const),
            pl.BlockSpec((CHUNK, C_HEADS * CHUNK), const),
            pl.BlockSpec((CHUNK, C_WIDTH), const),
            pl.BlockSpec((C_WIDTH + D_WIDTH, d), const),
            pl.BlockSpec((4, d), const),
            pl.BlockSpec((1, 6, d), lambda b, i: (b, 0, 0)),
        ],
        out_specs=pl.BlockSpec((1, tm, d), lambda b, i: (b, i, 0)),
        out_shape=jax.ShapeDtypeStruct((bsz, length, d), F32),
        scratch_shapes=[
            pltpu.VMEM((C_HEADS * CHUNK, C_WIDTH), BF16),
            pltpu.VMEM((tm, C_WIDTH), BF16),
        ],
        compiler_params=_params(("parallel", "parallel")),
        name="gating_outproj",
    )(u, v, fo, x, ln_g.reshape(1, -1), ln_b.reshape(1, -1), wcat, bs_full, w_out, g4, mods)


def _router_kernel(x_ref, mod_ref, g_ref, wr_ref, tri_ref, xn_ref, meta_ref, metat_ref, cnt_ref, carry_ref, *, tm):
    first = jnp.logical_and(pl.program_id(0) == 0, pl.program_id(1) == 0)

    @pl.when(first)
    def _():
        carry_ref[...] = jnp.zeros_like(carry_ref)

    x = x_ref[0]
    h = _rms(x, g_ref[2:3, :]) * (1.0 + mod_ref[0, 4:5, :]) + mod_ref[0, 3:4, :]
    xn_ref[0] = h
    h_hi = h.astype(BF16)
    h_lo = (h - h_hi.astype(F32)).astype(BF16)
    logits = _bdot(jnp.concatenate([h_hi, h_hi, h_lo], axis=1), wr_ref[...])
    lane = lax.broadcasted_iota(jnp.int32, (tm, LANES), 1).astype(F32)
    neg = jnp.float32(-jnp.inf)
    lm = jnp.where(lane < N_EXPERTS, logits, neg)
    m1 = jnp.max(lm, axis=-1, keepdims=True)
    i1 = jnp.min(jnp.where(lm == m1, lane, float(LANES)), axis=-1, keepdims=True)
    lm2 = jnp.where(lane == i1, neg, lm)
    m2 = jnp.max(lm2, axis=-1, keepdims=True)
    i2 = jnp.min(jnp.where(lm2 == m2, lane, float(LANES)), axis=-1, keepdims=True)
    e = jnp.exp(m2 - m1)
    p1 = 1.0 / (1.0 + e)
    p2 = e / (1.0 + e)
    hot1 = (lane == i1).astype(F32)
    hot2 = (lane == i2).astype(F32)
    both = hot1 + hot2
    before = _bdot(tri_ref[...], both.astype(BF16)) + carry_ref[0:1, :]
    r1 = jnp.sum(before * hot1, axis=-1, keepdims=True)
    r2 = jnp.sum(before * hot2, axis=-1, keepdims=True)
    carry_ref[0:1, :] = carry_ref[0:1, :] + jnp.sum(both, axis=0, keepdims=True)
    cnt_ref[...] = carry_ref[...]
    vals = (i1, i2, p1, p2, r1, r2)
    meta = jnp.zeros((tm, LANES), F32)
    for k, val in enumerate(vals):
        meta = jnp.where(lane == k, val, meta)
    meta_ref[0] = meta
    metat_ref[...] = meta.T[0:8, :]


def _router(x, mods, g4, w_router):
    bsz, length, d = x.shape
    tm = min(ROW_TILE, length)
    wr = jnp.concatenate([w_router, jnp.zeros((d, LANES - N_EXPERTS), F32)], axis=1)
    wr_hi = wr.astype(BF16)
    wr_lo = (wr - wr_hi.astype(F32)).astype(BF16)
    wr = jnp.concatenate([wr_hi, wr_lo, wr_hi], axis=0)
    tri =jnp.asarray(np.tril(np.ones((tm, tm), np.float32), -1), dtype=BF16)
    const = lambda b, i: (0, 0)
    return pl.pallas_call(
        functools.partial(_router_kernel, tm=tm),
        grid=(bsz, length // tm),
        in_specs=[
            pl.BlockSpec((1, tm, d), lambda b, i: (b, i, 0)),
            pl.BlockSpec((1, 6, d), lambda b, i: (b, 0, 0)),
            pl.BlockSpec((4, d), const),
            pl.BlockSpec((3 * d, LANES), const),
            pl.BlockSpec((tm, tm), const),
        ],
        out_specs=[
            pl.BlockSpec((1, tm, d), lambda b, i: (b, i, 0)),
            pl.BlockSpec((1, tm, LANES), lambda b, i: (b, i, 0)),
            pl.BlockSpec((8, tm), lambda b, i: (0, b * (length // tm) + i)),
            pl.BlockSpec((8, LANES), const),
        ],
        out_shape=[
            jax.ShapeDtypeStruct((bsz, length, d), F32),
            jax.ShapeDtypeStruct((bsz, length, LANES), F32),
            jax.ShapeDtypeStruct((8, bsz * length), F32),
            jax.ShapeDtypeStruct((8, LANES), F32),
        ],
        scratch_shapes=[pltpu.VMEM((8, LANES), F32)],
        compiler_params=_params(("arbitrary", "arbitrary")),
        name="moe_router",
    )(x, mods, g4, wr, tri)


def _expert_kernel(te_ref, nv_ref, x_ref, w1_ref, w3_ref, w2_ref, o_ref, acc_ref):
    i = pl.program_id(0)
    j = pl.program_id(1)

    @pl.when(i < nv_ref[0])
    def _():
        y = _swiglu_halves(x_ref[...].astype(BF16), w1_ref, w3_ref, w2_ref, lead=(0,))

        @pl.when(j == 0)
        def _():
            acc_ref[...] = y

        @pl.when(j > 0)
        def _():
            acc_ref[...] += y

        @pl.when(j == pl.num_programs(1) - 1)
        def _():
            o_ref[...] = acc_ref[...]


def _experts(x_sorted, tile_expert, n_valid, w1, w3, w2):
    rows = x_sorted.shape[0]
    d = w1.shape[1]
    dff = w1.shape[2]
    tm = EXPERT_ROW_TILE
    n_ff = 2
    tf = dff // n_ff

    def row_map(i, j, te, nv):
        return (jnp.maximum(jnp.minimum(i, nv[0] - 1), 0), 0)

    def w13_map(i, j, te, nv):
        return (te[jnp.maximum(jnp.minimum(i, nv[0] - 1), 0)], 0, jnp.where(i < nv[0], j, n_ff - 1))

    def w2_map(i, j, te, nv):
        return (te[jnp.maximum(jnp.minimum(i, nv[0] - 1), 0)], jnp.where(i < nv[0], j, n_ff - 1), 0)

    return pl.pallas_call(
        _expert_kernel,
        grid_spec=pltpu.PrefetchScalarGridSpec(
            num_scalar_prefetch=2,
            grid=(rows // tm, n_ff),
            in_specs=[
                pl.BlockSpec((tm, d), row_map),
                pl.BlockSpec((1, d, tf), w13_map),
                pl.BlockSpec((1, d, tf), w13_map),
                pl.BlockSpec((1, tf, d), w2_map),
            ],
            out_specs=pl.BlockSpec((tm, d), row_map),
            scratch_shapes=[pltpu.VMEM((tm, d), F32)],
        ),
        out_shape=jax.ShapeDtypeStruct((rows, d), F32),
        compiler_params=_params(("arbitrary", "arbitrary")),
        name="expert_swiglu",
    )(tile_expert, n_valid, x_sorted, w1, w3, w2)


def _combine_kernel(x_ref, y_ref, meta_ref, g_ref, mod_ref, o_ref):
    meta = meta_ref[0]
    y = meta[:, 2:3] * y_ref[0, 0] + meta[:, 3:4] * y_ref[1, 0]
    o_ref[0] = x_ref[0] + mod_ref[0, 5:6, :] * _rms(y, g_ref[3:4, :])


def _combine(x, y2, meta, g4, mods):
    bsz, length, d = x.shape
    tm = min(ROW_TILE, length)
    return pl.pallas_call(
        _combine_kernel,
        grid=(bsz, length // tm),
        in_specs=[
            pl.BlockSpec((1, tm, d), lambda b, i: (b, i, 0)),
            pl.BlockSpec((2, 1, tm, d), lambda b, i: (0, b, i, 0)),
            pl.BlockSpec((1, tm, LANES), lambda b, i: (b, i, 0)),
            pl.BlockSpec((4, d), lambda b, i: (0, 0)),
            pl.BlockSpec((1, 6, d), lambda b, i: (b, 0, 0)),
        ],
        out_specs=pl.BlockSpec((1, tm, d), lambda b, i: (b, i, 0)),
        out_shape=jax.ShapeDtypeStruct((bsz, length, d), F32),
        compiler_params=_params(("parallel", "parallel")),
        name="moe_combine",
    )(x, y2, meta, g4, mods)


SC_CORES = 2
SC_SUBCORES = 16
SC_WORKERS = SC_CORES * SC_SUBCORES
SC_ROW_CHUNK = 32


def _sc_mesh():
    return plsc.VectorSubcoreMesh(core_axis_name="c", subcore_axis_name="s")


def _sc_dispatch(rows_in, pos2, n_out):
    tokens, width = rows_in.shape
    per_worker = tokens // SC_WORKERS
    chunk = SC_ROW_CHUNK
    n_pairs = per_worker // (2 * chunk)
    idx = pos2.reshape(2, SC_WORKERS, 2 * n_pairs, chunk)

    @functools.partial(
        pl.kernel, mesh=_sc_mesh(),
        out_type=jax.ShapeDtypeStruct((n_out, width), rows_in.dtype),
        scratch_types=[
            pltpu.VMEM((2 * n_pairs, chunk), jnp.int32),
            pltpu.VMEM((2 * n_pairs, chunk), jnp.int32),
            pltpu.VMEM((2, chunk, width), rows_in.dtype),
        ] + [pltpu.SemaphoreType.DMA] * 6,
        name="sc_dispatch",
    )
    def body(rows_hbm, idx_hbm, out_hbm, idx0_v, idx1_v, rows_v, rsem0, rsem1, wsem0, wsem1, wsem2, wsem3):
        wid = lax.axis_index("s") * SC_CORES + lax.axis_index("c")
        base = wid * per_worker
        rsem = (rsem0, rsem1)
        wsem = ((wsem0, wsem1), (wsem2, wsem3))
        pltpu.sync_copy(idx_hbm.at[0, wid], idx0_v)
        pltpu.sync_copy(idx_hbm.at[1, wid], idx1_v)

        @pl.loop(0, n_pairs)
        def _(p):
            c0 = 2 * p
            reads = [pltpu.async_copy(rows_hbm.at[pl.ds(base + (c0 + s) * chunk, chunk)], rows_v.at[s], rsem[s])
                     for s in range(2)]
            writes = []
            for s in range(2):
                reads[s].wait()
                writes.append(pltpu.async_copy(rows_v.at[s], out_hbm.at[idx0_v.at[c0 + s]], wsem[s][0]))
                writes.append(pltpu.async_copy(rows_v.at[s], out_hbm.at[idx1_v.at[c0 + s]], wsem[s][1]))
            for w in writes:
                w.wait()

    return body(rows_in, idx)


def _sc_gather(table, idx):
    n = idx.shape[0]
    width = table.shape[1]
    chunk = SC_ROW_CHUNK
    per_worker = n // SC_WORKERS
    n_pairs = per_worker // (2 * chunk)
    idx3 = idx.reshape(SC_WORKERS, 2 * n_pairs, chunk)

    @functools.partial(
        pl.kernel, mesh=_sc_mesh(),
        out_type=jax.ShapeDtypeStruct((n, width), table.dtype),
        scratch_types=[
            pltpu.VMEM((2 * n_pairs, chunk), jnp.int32),
            pltpu.VMEM((2, chunk, width), table.dtype),
            pltpu.SemaphoreType.DMA, pltpu.SemaphoreType.DMA,
            pltpu.SemaphoreType.DMA, pltpu.SemaphoreType.DMA,
        ],
        name="sc_gather",
    )
    def body(table_hbm, idx_hbm, out_hbm, idx_v, rows_v, gsem0, gsem1, wsem0, wsem1):
        wid = lax.axis_index("s") * SC_CORES + lax.axis_index("c")
        base = wid * per_worker
        gsem = (gsem0, gsem1)
        wsem = (wsem0, wsem1)
        pltpu.sync_copy(idx_hbm.at[wid], idx_v)

        @pl.loop(0, n_pairs)
        def _(p):
            c0 = 2 * p
            gathers = [pltpu.async_copy(table_hbm.at[idx_v.at[c0 + s]], rows_v.at[s], gsem[s]) for s in range(2)]
            writes = []
            for s in range(2):
                gathers[s].wait()
                rows_out = out_hbm.at[pl.ds(base + (c0 + s) * chunk, chunk)]
                writes.append(pltpu.async_copy(rows_v.at[s], rows_out, wsem[s]))
            for s in range(2):
                writes[s].wait()

    return body(table, idx3)


def _moe(x, mods, g4, w_router, w1, w3, w2):
    bsz, length, d = x.shape
    tokens = bsz * length
    tm = EXPERT_ROW_TILE
    xn, meta, meta_t, counts = _router(x, mods, g4, w_router)
    experts = meta_t[0:2].astype(jnp.int32)
    ranks = meta_t[4:6].astype(jnp.int32)
    cnt = counts[0, 0:N_EXPERTS].astype(jnp.int32)
    padded = ((cnt + tm - 1) // tm) * tm
    ends = jnp.cumsum(padded)
    starts = ends - padded
    group_start = jnp.zeros_like(experts)
    for e in range(N_EXPERTS):
        group_start = jnp.where(experts == e, starts[e], group_start)
    pos = (group_start + ranks).reshape(2 * tokens)
    rows = 2 * tokens + N_EXPERTS * tm
    n_tiles = rows // tm
    n_valid = (ends[-1] // tm).astype(jnp.int32).reshape(1)
    tile_start = jnp.arange(n_tiles, dtype=jnp.int32) * tm
    tile_expert = jnp.minimum(
        jnp.sum((tile_start[:, None] >= ends[None, :]).astype(jnp.int32), axis=1), N_EXPERTS - 1)
    x_sorted = _sc_dispatch(xn.reshape(tokens, d), pos.reshape(2, tokens), rows)
    y_sorted = _experts(x_sorted, tile_expert, n_valid, w1, w3, w2)
    y2 = _sc_gather(y_sorted, pos).reshape(2, bsz, length, d)
    return _combine(x, y2, meta, g4, mods)


def _grid_pos_embed(rows, d):
    row = np.repeat(np.arange(rows, dtype=np.float64), GRID_W)
    col = np.tile(np.arange(GRID_W, dtype=np.float64), rows)
    n_freq = d // 4
    omega = POS_THETA ** (-np.arange(n_freq, dtype=np.float64) / n_freq)
    ang_r = row[:, None] * omega
    ang_c = col[:, None] * omega
    return jnp.asarray(np.concatenate([np.sin(ang_r), np.cos(ang_r), np.sin(ang_c), np.cos(ang_c)], axis=-1),
                       dtype=F32)


def kernel(x, c, ctx, c_ctx, ada_w, ada_b, norm_g, ab_w_in, rg_conv_w, rg_conv_b, rg_w_r, rg_b_r, rg_w_i, rg_b_i, rg_lambda, cv_dw_w, cv_dw_b, cv_ln_g, cv_ln_b, ab_w_out, ffn_w1, ffn_w3, ffn_w2, cd_w_in, sg_ln_g, sg_ln_b, sg_w_s, sg_b_s, cd_w_out, moe_router, moe_w1, moe_w3, moe_w2):
    bsz, length, d = x.shape
    assert ada_w.shape[0] == 2, "two layers: one even (RG-LRU | Conformer), one odd (gMLP | Fourier)"
    pos = _grid_pos_embed(length // GRID_W, d)

    cpad = jnp.concatenate([c, c_ctx[None, :], jnp.zeros((16 - bsz - 1, d), F32)], axis=0)
    m = _ada(cpad, ada_w, ada_b)
    mods = m[:, :bsz].reshape(2, bsz, 6, d)
    mods_ctx = m[:, bsz:bsz + 1].reshape(2, 1, 6, d)

    mods_t = mods[0].transpose(1, 0, 2)
    mods_ctx_t = jnp.broadcast_to(mods_ctx[0].reshape(6, 1, d), (6, bsz, d))
    pos3 = pos[:, None, :]
    w_in = ab_w_in[0]
    xa3, gg3, glu3, x0_t, ffn1b, ffn3b, ffn2b = _inproj_t(
        x, pos3, mods_t, norm_g[0], w_in, rg_conv_w[0], rg_conv_b[0], True, riders=(ffn_w1[0], ffn_w3[0], ffn_w2[0]))
    (xc3,) = _inproj_t(ctx, None, mods_ctx_t, norm_g[0], w_in[:, 0:A_WIDTH], rg_conv_w[0], rg_conv_b[0], False)
    wg, bg, sp = _gate_weights(rg_w_r[0], rg_b_r[0], rg_w_i[0], rg_b_i[0], rg_lambda[0])
    n_exp, _, dff_e = moe_w1[0].shape
    rec_hi, rec_lo, w2b = _scan_t(xa3, xc3, gg3, wg, bg, sp, riders=(moe_w2[0].reshape(n_exp * dff_e, d),))
    (u3,) = _conv_t(glu3, cv_dw_w[0], cv_dw_b[0], cv_ln_g[0], cv_ln_b[0])
    x1_t = _outproj_t(rec_lo, rec_hi, u3, x0_t, ab_w_out[0], norm_g[0], mods_t)
    x2, w1b, w3b = _ffn_t(x1_t, mods_t, norm_g[0], ffn1b, ffn3b, ffn2b,
                          riders=(moe_w1[0].reshape(n_exp * d, dff_e), moe_w3[0].reshape(n_exp * d, dff_e)))

    u, v, f = _inproj_cd(x2, mods[1], norm_g[1], cd_w_in[0])
    fo = _fourier(f)
    x3 = _gate_out(u, v, fo, x2, sg_ln_g[0], sg_ln_b[0], sg_w_s[0], sg_b_s[0], cd_w_out[0],
                   norm_g[1], mods[1])
    return _moe(x3, mods[1], norm_g[1], moe_router[0], w1b.reshape(n_exp, d, dff_e), w3b.reshape(n_exp, d, dff_e),
                w2b.reshape(n_exp, dff_e, d))
```

```python
import functools

import numpy as np
import jax
import jax.numpy as jnp
from jax import lax
from jax.experimental import pallas as pl
from jax.experimental.pallas import tpu as pltpu
from jax.experimental.pallas import tpu_sc as plsc

F32 = jnp.float32
BF16 = jnp.bfloat16

GRID_W = 64
POS_THETA = 10000.0
NORM_EPS = 1e-6
LN_EPS = 1e-5
LRU_C = 8.0

A_WIDTH = 512
A_HEADS = 8
A_CONV_W = 4
B_WIDTH = 512
B_CONV_W = 31
C_WIDTH = 512
C_HEADS = 8
CHUNK = 128
D_WIDTH = 512
D_GROUPS = 4
D_GROUP_DIM = D_WIDTH // D_GROUPS
N_EXPERTS = 8

LANES = 128
MXU_WIDTH = 256
ROW_TILE = 512
EXPERT_ROW_TILE = 512
VMEM_LIMIT = 56 * 2 ** 20


def _params(sem):
    return pltpu.CompilerParams(dimension_semantics=sem, vmem_limit_bytes=VMEM_LIMIT)


def _rms(x, g):
    return x * lax.rsqrt(jnp.mean(x * x, axis=-1, keepdims=True) + NORM_EPS) * g


def _bdot(a, b):
    return jnp.dot(a, b, preferred_element_type=F32)


def _swiglu_halves(x, w1_ref, w3_ref, w2_ref, lead=()):
    n = w1_ref.shape[-1]
    half = -(-(n // 2) // MXU_WIDTH) * MXU_WIDTH
    y = None
    for lo, hi in ((0, half), (half, n)):
        a = _bdot(x, w1_ref[lead + (slice(None), slice(lo, hi))])
        b = _bdot(x, w3_ref[lead + (slice(None), slice(lo, hi))])
        t = (jax.nn.silu(a) * b).astype(BF16)
        part = _bdot(t, w2_ref[lead + (slice(lo, hi), slice(None))])
        y = part if y is None else y + part
    return y


def _ada_kernel(c_ref, w_ref, b_ref, o_ref):
    s = jax.nn.silu(c_ref[...])
    o_ref[0] = _bdot(s.astype(BF16), w_ref[0].astype(BF16)) + b_ref[0]


def _ada(cpad, ada_w, ada_b):
    nl, d, n6 = ada_w.shape
    rows = cpad.shape[0]
    tn = n6 // 4
    return pl.pallas_call(
        _ada_kernel,
        grid=(nl, n6 // tn),
        in_specs=[
            pl.BlockSpec((rows, d), lambda l, j: (0, 0)),
            pl.BlockSpec((1, d, tn), lambda l, j: (l, 0, j)),
            pl.BlockSpec((1, 1, tn), lambda l, j: (l, 0, j)),
        ],
        out_specs=pl.BlockSpec((1, rows, tn), lambda l, j: (l, 0, j)),
        out_shape=jax.ShapeDtypeStruct((nl, rows, n6), F32),
        compiler_params=_params(("arbitrary", "arbitrary")),
        name="ada_terms",
    )(cpad, ada_w, ada_b.reshape(nl, 1, n6))


TIME_TILE = 128
SCAN_TIME_TILE = 128
SCAN_LANES = 256
CONV_TIME_TILE = 128
CONV_HALO = 16


def _time_major(ref):
    return jnp.swapaxes(ref[...], 0, 1)


def _cast_riders(cast_in, cast_out):
    for src, dst in zip(cast_in, cast_out):
        dst[...] = src[...].astype(BF16)


def _rider_specs(arrays, n_steps, step_of):
    in_specs, out_specs, out_shapes = [], [], []
    for arr in arrays:
        rows, cols = arr.shape
        block = (rows // n_steps, cols)
        in_specs.append(pl.BlockSpec(block, lambda *idx: (step_of(*idx), 0)))
        out_specs.append(pl.BlockSpec(block, lambda *idx: (step_of(*idx), 0)))
        out_shapes.append(jax.ShapeDtypeStruct(arr.shape, BF16))
    return in_specs, out_specs, out_shapes


def _inproj_t_kernel(*refs, tt, add_pos, branches, n_cast):
    if n_cast:
        n_in = len(refs) - n_cast - (4 if branches else 1) - n_cast
        _cast_riders(refs[n_in:n_in + n_cast], refs[len(refs) - n_cast:])
        refs = refs[:n_in] + refs[n_in + n_cast:len(refs) - n_cast]
    if add_pos:
        x_ref, xp_ref, xn_ref, pos_ref, posp_ref, posn_ref, mod_ref, g_ref, w_ref, cw_ref, cb_ref, *outs = refs
        xall = jnp.concatenate(
            [_time_major(xp_ref)[7:8] + posp_ref[...], _time_major(x_ref) + pos_ref[...],
             _time_major(xn_ref)[0:2] + posn_ref[...]], axis=0)
    else:
        x_ref, xp_ref, xn_ref, mod_ref, g_ref, w_ref, cw_ref, cb_ref, *outs = refs
        xall = jnp.concatenate([_time_major(xp_ref)[7:8], _time_major(x_ref), _time_major(xn_ref)[0:2]], axis=0)
    i = pl.program_id(0)
    last = pl.num_programs(0) - 1
    steps, bsz, d = xall.shape
    h = _rms(xall, g_ref[0:1, :]) * (1.0 + mod_ref[1]) + mod_ref[0]
    z = _bdot(h.reshape(steps * bsz, d).astype(BF16), w_ref[...].astype(BF16))
    z3 = z.reshape(steps, bsz, z.shape[-1])
    t_idx = lax.broadcasted_iota(jnp.int32, (steps, bsz, A_WIDTH), 0)
    inside = jnp.logical_and(jnp.logical_or(t_idx >= 1, i > 0), jnp.logical_or(t_idx <= tt, i < last))
    xa = jnp.where(inside, z3[:, :, 0:A_WIDTH], 0.0)
    xconv = cb_ref[...] + cw_ref[0:1, :] * xa[0:tt]
    for k in range(1, A_CONV_W):
        xconv = xconv + cw_ref[k:k + 1, :] * xa[k:k + tt]
    outs[0][...] = xconv
    if branches:
        outs[1][...] = jax.nn.gelu(z3[1:tt + 1, :, A_WIDTH:2 * A_WIDTH])
        vb = z3[1:tt + 1, :, 2 * A_WIDTH:2 * A_WIDTH + B_WIDTH]
        gb = z3[1:tt + 1, :, 2 * A_WIDTH + B_WIDTH:]
        outs[2][...] = vb * jax.nn.sigmoid(gb)
        outs[3][...] = xall[1:tt + 1]


def _inproj_t(x, pos3, mods_t, g4, w, conv_w, conv_b, branches, riders=()):
    bsz, length, d = x.shape
    n = w.shape[1]
    tt = min(TIME_TILE, length)
    add_pos = pos3 is not None
    n_out = 3 if branches else 1
    prev_map = lambda i: (jnp.maximum(i * tt - 1, 0), 0, 0)
    next_map = lambda i: (jnp.minimum((i + 1) * (tt // 2), length // 2 - 1), 0, 0)
    in_specs = [
        pl.BlockSpec((bsz, tt, d), lambda i: (0, i, 0)),
        pl.BlockSpec((bsz, 8, d), lambda i: (0, jnp.maximum(i * (tt // 8) - 1, 0), 0)),
        pl.BlockSpec((bsz, 8, d), lambda i: (0, jnp.minimum((i + 1) * (tt // 8), length // 8 - 1), 0)),
    ]
    args = [x, x, x]
    if add_pos:
        in_specs += [
            pl.BlockSpec((tt, 1, d), lambda i: (i, 0, 0)),
            pl.BlockSpec((1, 1, d), prev_map),
            pl.BlockSpec((2, 1, d), next_map),
        ]
        args += [pos3, pos3, pos3]
    in_specs += [
        pl.BlockSpec((6, bsz, d), lambda i: (0, 0, 0)),
        pl.BlockSpec((4, d), lambda i: (0, 0)),
        pl.BlockSpec((d, n), lambda i: (0, 0)),
        pl.BlockSpec((A_CONV_W, A_WIDTH), lambda i: (0, 0)),
        pl.BlockSpec((1, A_WIDTH), lambda i: (0, 0)),
    ]
    args += [mods_t, g4, w, conv_w, conv_b.reshape(1, A_WIDTH)]
    r_in, r_out, r_shapes = _rider_specs(riders, length // tt, lambda i: i)
    out_specs = [pl.BlockSpec((tt, bsz, A_WIDTH), lambda i: (i, 0, 0))] * n_out
    out_shapes = [jax.ShapeDtypeStruct((length, bsz, A_WIDTH), F32)] * n_out
    if branches:
        out_specs.append(pl.BlockSpec((tt, bsz, d), lambda i: (i, 0, 0)))
        out_shapes.append(jax.ShapeDtypeStruct((length, bsz, d), F32))
    return pl.pallas_call(
        functools.partial(_inproj_t_kernel, tt=tt, add_pos=add_pos, branches=branches, n_cast=len(riders)),
        grid=(length // tt,),
        in_specs=in_specs + r_in,
        out_specs=out_specs + r_out,
        out_shape=out_shapes + r_shapes,
        compiler_params=_params(("parallel",)),
        name="inproj_ab" if branches else "inproj_ctx",
    )(*args, *riders)


def _scan_t_kernel(*refs, tb, nb, ctx_len, n_cast):
    xf_ref, xb_ref, ggf_ref, ggb_ref, xc_ref, wg_ref, bg_ref, sp_ref = refs[:8]
    cast_in = refs[8:8 + n_cast]
    hi_ref, lo_ref = refs[8 + n_cast:10 + n_cast]
    cast_out = refs[10 + n_cast:10 + 2 * n_cast]
    a_ref, b_ref, h_ref, state_ref = refs[10 + 2 * n_cast:]
    _cast_riders(cast_in, cast_out)
    k = pl.program_id(1)
    half = nb // 2
    bsz, lanes = xf_ref.shape[1], xf_ref.shape[2]
    chunk = 128

    def coefficients(x_ref, n, direction):
        def body(c, _):
            t0 = pl.multiple_of(c * chunk, chunk)
            x = x_ref[pl.ds(t0, chunk)].reshape(chunk * bsz, lanes)
            th = jnp.tanh(_bdot(x.astype(BF16), wg_ref[0, direction]) + bg_ref[0, direction])
            i = 0.5 * th[:, lanes:] + 0.5
            c = (-0.5 * LRU_C) * sp_ref[0, direction:direction + 1, :]
            log_a = c * th[:, 0:lanes] + c
            a = jnp.exp(log_a)
            one_minus_a2 = -jnp.tanh(log_a) * (a * a + 1.0)
            a_ref[direction, pl.ds(t0, chunk)] = a.reshape(chunk, bsz, lanes)
            b_ref[direction, pl.ds(t0, chunk)] = (jnp.sqrt(one_minus_a2) * (i * x)).reshape(chunk, bsz, lanes)
            return 0
        lax.fori_loop(0, n // chunk, body, 0)

    def sweep(n, store):
        def body(t, carry):
            hf, hb = carry
            tr = n - 1 - t
            hf = a_ref[0, pl.ds(t, 1)][0] * hf + b_ref[0, pl.ds(t, 1)][0]
            hb = a_ref[1, pl.ds(tr, 1)][0] * hb + b_ref[1, pl.ds(tr, 1)][0]
            if store:
                b_ref[0, pl.ds(t, 1)] = hf[None]
                b_ref[1, pl.ds(tr, 1)] = hb[None]
            return hf, hb
        hf, hb = lax.fori_loop(0, n, body, (state_ref[0], state_ref[1]), unroll=8)
        state_ref[0] = hf
        state_ref[1] = hb

    @pl.when(k == 0)
    def _():
        state_ref[...] = jnp.zeros_like(state_ref)
        coefficients(xc_ref, ctx_len, 0)
        coefficients(xc_ref, ctx_len, 1)
        sweep(ctx_len, False)

    coefficients(xf_ref, tb, 0)
    coefficients(xb_ref, tb, 1)
    sweep(tb, True)
    m = nb - 1 - k

    @pl.when(k < half)
    def _():
        h_ref[k] = b_ref[0, 0:tb]
        h_ref[m] = b_ref[1, 0:tb]

    @pl.when(k >= half)
    def _():
        hi_ref[...] = (b_ref[0, 0:tb] + h_ref[k]) * ggf_ref[...]
        lo_ref[...] = (h_ref[m] + b_ref[1, 0:tb]) * ggb_ref[...]


def _scan_t(xa3, xc3, gg3, wg, bg, sp, riders=()):
    length, bsz, width = xa3.shape
    ctx_len = xc3.shape[0]
    tb = min(SCAN_TIME_TILE, length // 2)
    nb = length // tb
    half = nb // 2
    lanes = SCAN_LANES
    blk = (tb, bsz, lanes)
    r_in, r_out, r_shapes = _rider_specs(riders, (width // lanes) * nb, lambda g, k: g * nb + k)
    return pl.pallas_call(
        functools.partial(_scan_t_kernel, tb=tb, nb=nb, ctx_len=ctx_len, n_cast=len(riders)),
        grid=(width // lanes, nb),
        in_specs=[
            pl.BlockSpec(blk, lambda g, k: (k, 0, g)),
            pl.BlockSpec(blk, lambda g, k: (nb - 1 - k, 0, g)),
            pl.BlockSpec(blk, lambda g, k: (jnp.maximum(k, half), 0, g)),
            pl.BlockSpec(blk, lambda g, k: (jnp.minimum(nb - 1 - k, half - 1), 0, g)),
            pl.BlockSpec((ctx_len, bsz, lanes), lambda g, k: (0, 0, g)),
            pl.BlockSpec((1, 2, lanes, 2 * lanes), lambda g, k: (g, 0, 0, 0)),
            pl.BlockSpec((1, 2, 1, 2 * lanes), lambda g, k: (g, 0, 0, 0)),
            pl.BlockSpec((1, 2, lanes), lambda g, k: (g, 0, 0)),
        ] + r_in,
        out_specs=[
            pl.BlockSpec(blk, lambda g, k: (jnp.maximum(k - half, 0), 0, g)),
            pl.BlockSpec(blk, lambda g, k: (jnp.minimum(nb - 1 - k, half - 1), 0, g)),
        ] + r_out,
        out_shape=[jax.ShapeDtypeStruct((length // 2, bsz, width), F32)] * 2 + r_shapes,
        scratch_shapes=[
            pltpu.VMEM((2, max(tb, ctx_len), bsz, lanes), F32),
            pltpu.VMEM((2, max(tb, ctx_len), bsz, lanes), F32),
            pltpu.VMEM((nb, tb, bsz, lanes), F32),
            pltpu.VMEM((2, bsz, lanes), F32),
        ],
        compiler_params=_params(("parallel", "arbitrary")),
        name="rglru_scan",
    )(xa3, xa3, gg3, gg3, xc3, wg, bg, sp, *riders)


def _gate_weights(w_r, b_r, w_i, b_i, lam):
    hd = A_WIDTH // A_HEADS
    heads_per_group = SCAN_LANES // hd
    ngroups = A_WIDTH // SCAN_LANES

    def blockdiag(w):
        w = w.reshape(ngroups, heads_per_group, hd, hd)
        eye = jnp.eye(heads_per_group, dtype=w.dtype)
        return jnp.einsum("ghde,hk->ghdke", w, eye).reshape(ngroups, SCAN_LANES, SCAN_LANES)

    wg = jnp.stack([jnp.concatenate([blockdiag(w_r[d]), blockdiag(w_i[d])], axis=-1) for d in range(2)], axis=1)
    bg = jnp.stack([jnp.concatenate([b_r[d].reshape(ngroups, 1, SCAN_LANES), b_i[d].reshape(ngroups, 1, SCAN_LANES)],
                                    axis=-1) for d in range(2)], axis=1)
    sp = jax.nn.softplus(-lam.astype(F32)).reshape(2, ngroups, SCAN_LANES).transpose(1, 0, 2)
    return (0.5 * wg).astype(BF16), 0.5 * bg, sp


def _conv_t_kernel(*refs, tb, n_cast):
    x_ref, xp_ref, xn_ref, w_ref, b_ref, lng_ref, lnb_ref = refs[:7]
    cast_in = refs[7:7 + n_cast]
    o_ref = refs[7 + n_cast]
    cast_out = refs[8 + n_cast:8 + 2 * n_cast]
    stage_ref = refs[-1]
    _cast_riders(cast_in, cast_out)
    i = pl.program_id(0)
    last = pl.num_programs(0) - 1
    halo = CONV_HALO
    stage_ref[0:halo] = jnp.where(i > 0, xp_ref[...], 0.0)
    stage_ref[halo:halo + tb] = x_ref[...]
    stage_ref[halo + tb:2 * halo + tb] = jnp.where(i < last, xn_ref[...], 0.0)
    sub = 8

    def body(r, _):
        t0 = pl.multiple_of(r * sub, sub)
        def tap(k, acc):
            return acc + w_ref[k] * stage_ref[pl.ds(t0 + 1 + k, sub)]
        acc = lax.fori_loop(0, B_CONV_W + 1, tap, jnp.broadcast_to(b_ref[...], (sub,) + b_ref.shape), unroll=8)
        o_ref[pl.ds(t0, sub)] = acc
        return 0
    lax.fori_loop(0, tb // sub, body, 0)

    norm_rows = 64

    def norm_body(r, _):
        t0 = pl.multiple_of(r * norm_rows, norm_rows)
        acc = o_ref[pl.ds(t0, norm_rows)]
        mu = jnp.mean(acc, axis=-1, keepdims=True)
        cen = acc - mu
        var = jnp.mean(cen * cen, axis=-1, keepdims=True)
        y = cen * lax.rsqrt(var + LN_EPS) * lng_ref[...] + lnb_ref[...]
        o_ref[pl.ds(t0, norm_rows)] = jax.nn.silu(y)
        return 0
    lax.fori_loop(0, tb // norm_rows, norm_body, 0)


def _conv_t(glu3, dw_w, dw_b, ln_g, ln_b, riders=()):
    length, bsz, width = glu3.shape
    tb = min(CONV_TIME_TILE, length)
    halo = CONV_HALO
    taps = dw_w.shape[0]
    w8 = jnp.broadcast_to(jnp.concatenate([dw_w, jnp.zeros((32 - taps, width), F32)], axis=0)[:, None, :],
                          (32, bsz, width))
    b8 = jnp.broadcast_to(dw_b[None, :], (bsz, width))
    r_in, r_out, r_shapes = _rider_specs(riders, length // tb, lambda i: i)
    return pl.pallas_call(
        functools.partial(_conv_t_kernel, tb=tb, n_cast=len(riders)),
        grid=(length // tb,),
        in_specs=[
            pl.BlockSpec((tb, bsz, width), lambda i: (i, 0, 0)),
            pl.BlockSpec((halo, bsz, width), lambda i: (jnp.maximum(i * (tb // halo) - 1, 0), 0, 0)),
            pl.BlockSpec((halo, bsz, width), lambda i: (jnp.minimum((i + 1) * (tb // halo), length // halo - 1), 0, 0)),
            pl.BlockSpec((32, bsz, width), lambda i: (0, 0, 0)),
            pl.BlockSpec((bsz, width), lambda i: (0, 0)),
            pl.BlockSpec((1, width), lambda i: (0, 0)),
            pl.BlockSpec((1, width), lambda i: (0, 0)),
        ] + r_in,
        out_specs=[pl.BlockSpec((tb, bsz, width), lambda i: (i, 0, 0))] + r_out,
        out_shape=[jax.ShapeDtypeStruct((length, bsz, width), F32)] + r_shapes,
        scratch_shapes=[pltpu.VMEM((tb + 2 * halo, bsz, width), F32)],
        compiler_params=_params(("parallel",)),
        name="conformer_conv",
    )(glu3, glu3, glu3, w8, b8, ln_g.reshape(1, -1), ln_b.reshape(1, -1), *riders)


def _outproj_t_kernel(lo_ref, hi_ref, u_ref, x_ref, wout_ref, g_ref, mod_ref, o_ref, *, half_steps):
    i = pl.program_id(0)
    tt, bsz, d = x_ref.shape
    rec = jnp.where(i < half_steps, lo_ref[...], hi_ref[...])
    rows = tt * bsz
    y = (_bdot(rec.reshape(rows, A_WIDTH).astype(BF16), wout_ref[0:A_WIDTH, :].astype(BF16))
         + _bdot(u_ref[...].reshape(rows, B_WIDTH).astype(BF16), wout_ref[A_WIDTH:, :].astype(BF16)))
    yn = _rms(y, g_ref[1:2, :]).reshape(tt, bsz, d)
    o_ref[...] = x_ref[...] + mod_ref[2] * yn


def _outproj_t(rec_lo, rec_hi, u3, x0_t, w_out, g4, mods_t):
    length, bsz, d = x0_t.shape
    tt = min(TIME_TILE // 2, length // 2)
    half_steps = (length // 2) // tt
    return pl.pallas_call(
        functools.partial(_outproj_t_kernel, half_steps=half_steps),
        grid=(length // tt,),
        in_specs=[
            pl.BlockSpec((tt, bsz, A_WIDTH), lambda i: (jnp.minimum(i, half_steps - 1), 0, 0)),
            pl.BlockSpec((tt, bsz, A_WIDTH), lambda i: (jnp.maximum(i - half_steps, 0), 0, 0)),
            pl.BlockSpec((tt, bsz, B_WIDTH), lambda i: (i, 0, 0)),
            pl.BlockSpec((tt, bsz, d), lambda i: (i, 0, 0)),
            pl.BlockSpec((A_WIDTH + B_WIDTH, d), lambda i: (0, 0)),
            pl.BlockSpec((4, d), lambda i: (0, 0)),
            pl.BlockSpec((6, bsz, d), lambda i: (0, 0, 0)),
        ],
        out_specs=pl.BlockSpec((tt, bsz, d), lambda i: (i, 0, 0)),
        out_shape=jax.ShapeDtypeStruct((length, bsz, d), F32),
        compiler_params=_params(("parallel",)),
        name="mixer_ab_outproj",
    )(rec_lo, rec_hi, u3, x0_t, w_out, g4, mods_t)


def _ffn_t_kernel(x_ref, mod_ref, g_ref, w1_ref, w3_ref, w2_ref, *rest, n_cast):
    _cast_riders(rest[:n_cast], rest[n_cast + 1:])
    o_ref = rest[n_cast]
    x = x_ref[...]
    tt, bsz, d = x.shape
    h = (_rms(x, g_ref[2:3, :]) * (1.0 + mod_ref[4]) + mod_ref[3]).reshape(tt * bsz, d).astype(BF16)
    y = _swiglu_halves(h, w1_ref, w3_ref, w2_ref)
    o_ref[...] = jnp.swapaxes(x + mod_ref[5] * _rms(y, g_ref[3:4, :]).reshape(tt, bsz, d), 0, 1)


def _ffn_t(x3, mods_t, g4, w1, w3, w2, riders=()):
    length, bsz, d = x3.shape
    dff = w1.shape[1]
    tt = min(TIME_TILE // 2, length)
    r_in, r_out, r_shapes = _rider_specs(riders, length // tt, lambda i: i)
    return pl.pallas_call(
        functools.partial(_ffn_t_kernel, n_cast=len(riders)),
        grid=(length // tt,),
        in_specs=[
            pl.BlockSpec((tt, bsz, d), lambda i: (i, 0, 0)),
            pl.BlockSpec((6, bsz, d), lambda i: (0, 0, 0)),
            pl.BlockSpec((4, d), lambda i: (0, 0)),
            pl.BlockSpec((d, dff), lambda i: (0, 0), pipeline_mode=pl.Buffered(1)),
            pl.BlockSpec((d, dff), lambda i: (0, 0), pipeline_mode=pl.Buffered(1)),
            pl.BlockSpec((dff, d), lambda i: (0, 0), pipeline_mode=pl.Buffered(1)),
        ] + r_in,
        out_specs=[pl.BlockSpec((bsz, tt, d), lambda i: (0, i, 0))] + r_out,
        out_shape=[jax.ShapeDtypeStruct((bsz, length, d), F32)] + r_shapes,
        compiler_params=_params(("parallel",)),
        name="dense_swiglu",
    )(x3, mods_t, g4, w1, w3, w2, *riders)


def _inproj_cd_kernel(x_ref, mod_ref, g_ref, w_ref, u_ref, v_ref, f_ref):
    x = x_ref[0]
    h = _rms(x, g_ref[0:1, :]) * (1.0 + mod_ref[0, 1:2, :]) + mod_ref[0, 0:1, :]
    z = _bdot(h.astype(BF16), w_ref[...].astype(BF16))
    u_ref[0] = jax.nn.gelu(z[:, 0:C_WIDTH])
    v_ref[0] = jax.nn.gelu(z[:, C_WIDTH:2 * C_WIDTH])
    f_ref[0] = z[:, 2 * C_WIDTH:]


def _inproj_cd(x, mods, g4, w):
    bsz, length, d = x.shape
    n = w.shape[1]
    tm = min(2 * ROW_TILE, length)
    return pl.pallas_call(
        _inproj_cd_kernel,
        grid=(bsz, length // tm),
        in_specs=[
            pl.BlockSpec((1, tm, d), lambda b, i: (b, i, 0)),
            pl.BlockSpec((1, 6, d), lambda b, i: (b, 0, 0)),
            pl.BlockSpec((4, d), lambda b, i: (0, 0)),
            pl.BlockSpec((d, n), lambda b, i: (0, 0)),
        ],
        out_specs=[pl.BlockSpec((1, tm, C_WIDTH), lambda b, i: (b, i, 0))] * 3,
        out_shape=[jax.ShapeDtypeStruct((bsz, length, C_WIDTH), F32)] * 3,
        compiler_params=_params(("parallel", "parallel")),
        name="inproj_cd",
    )(x, mods, g4, w)


def _fourier_kernel(f_ref, dmat_ref, lmat_ref, o_ref, z_ref, *, length, scale):
    j = pl.program_id(1)
    rows_per_step = 512 if length % 512 == 0 else length

    @pl.when(j == 0)
    def _():
        def body(c, _):
            rows = pl.ds(pl.multiple_of(c * rows_per_step, rows_per_step), rows_per_step)
            fb = f_ref[0, rows, :].astype(BF16)
            for g in range(D_GROUPS):
                cs = _bdot(fb[:, g * D_GROUP_DIM:(g + 1) * D_GROUP_DIM], dmat_ref[...])
                z_ref[0, rows, g * D_GROUP_DIM:(g + 1) * D_GROUP_DIM] = cs[:, 0:D_GROUP_DIM].astype(BF16)
                z_ref[1, rows, g * D_GROUP_DIM:(g + 1) * D_GROUP_DIM] = cs[:, D_GROUP_DIM:].astype(BF16)
            return 0
        lax.fori_loop(0, length // rows_per_step, body, 0)

    out = _bdot(lmat_ref[0], z_ref[0]) + _bdot(lmat_ref[1], z_ref[1])
    o_ref[0] = (out * scale).astype(BF16)


def _dft_matrices(length):
    k = np.arange(length, dtype=np.int64)
    ang_l = 2.0 * np.pi * ((k[:, None] * k[None, :]) % length).astype(np.float64) / length
    lmat = np.stack([np.cos(ang_l), -np.sin(ang_l)], axis=0)
    d = np.arange(D_GROUP_DIM, dtype=np.int64)
    ang_d = 2.0 * np.pi * ((d[:, None] * d[None, :]) % D_GROUP_DIM).astype(np.float64) / D_GROUP_DIM
    dmat = np.concatenate([np.cos(ang_d), np.sin(ang_d)], axis=1)
    return jnp.asarray(lmat, dtype=F32).astype(BF16), jnp.asarray(dmat, dtype=F32).astype(BF16)


def _fourier(f):
    bsz, length, width = f.shape
    tm = min(ROW_TILE, length)
    lmat, dmat = _dft_matrices(length)
    scale = float(1.0 / np.sqrt(length * D_GROUP_DIM))
    return pl.pallas_call(
        functools.partial(_fourier_kernel, length=length, scale=scale),
        grid=(bsz, length // tm),
        in_specs=[
            pl.BlockSpec((1, length, width), lambda b, i: (b, 0, 0)),
            pl.BlockSpec((D_GROUP_DIM, 2 * D_GROUP_DIM), lambda b, i: (0, 0)),
            pl.BlockSpec((2, tm, length), lambda b, i: (0, i, 0)),
        ],
        out_specs=pl.BlockSpec((1, tm, width), lambda b, i: (b, i, 0)),
        out_shape=jax.ShapeDtypeStruct((bsz, length, width), BF16),
        scratch_shapes=[pltpu.VMEM((2, length, width), BF16)],
        compiler_params=_params(("parallel", "arbitrary")),
        name="fourier_mix",
    )(f, dmat, lmat)


def _gate_out_kernel(u_ref, v_ref, fo_ref, x_ref, lng_ref, lnb_ref, wcat_ref, bs_ref, wout_ref, g_ref, mod_ref,
                     o_ref, vs_ref, gt_ref, *, tm):
    head_dim = C_WIDTH // C_HEADS
    head_of_lane = lax.broadcasted_iota(jnp.int32, (CHUNK, C_WIDTH), 1) // head_dim
    for ci in range(tm // CHUNK):
        rows = slice(ci * CHUNK, (ci + 1) * CHUNK)
        v = v_ref[0, rows, :]
        mu = jnp.mean(v, axis=-1, keepdims=True)
        cen = v - mu
        var = jnp.mean(cen * cen, axis=-1, keepdims=True)
        vn = (cen * lax.rsqrt(var + LN_EPS) * lng_ref[...] + lnb_ref[...]).astype(BF16)
        for h in range(C_HEADS):
            vs_ref[h * CHUNK:(h + 1) * CHUNK, :] = jnp.where(head_of_lane == h, vn, jnp.zeros_like(vn))
        mixed = _bdot(wcat_ref[...], vs_ref[...]) + bs_ref[...]
        gt_ref[rows, :] = (u_ref[0, rows, :] * mixed).astype(BF16)
    y = (_bdot(gt_ref[...], wout_ref[0:C_WIDTH, :].astype(BF16))
         + _bdot(fo_ref[0], wout_ref[C_WIDTH:, :].astype(BF16)))
    o_ref[0] = x_ref[0] + mod_ref[0, 2:3, :] * _rms(y, g_ref[1:2, :])


def _gate_out(u, v, fo, x, ln_g, ln_b, w_s, b_s, w_out, g4, mods):
    bsz, length, d = x.shape
    tm = min(2 * ROW_TILE, length)
    head_dim = C_WIDTH // C_HEADS
    wcat = w_s.transpose(1, 0, 2).reshape(CHUNK, C_HEADS * CHUNK).astype(BF16)
    bs_full = jnp.repeat(b_s.T, head_dim, axis=1)
    const = lambda b, i: (0, 0)
    return pl.pallas_call(
        functools.partial(_gate_out_kernel, tm=tm),
        grid=(bsz, length // tm),
        in_specs=[
            pl.BlockSpec((1, tm, C_WIDTH), lambda b, i: (b, i, 0)),
            pl.BlockSpec((1, tm, C_WIDTH), lambda b, i: (b, i, 0)),
            pl.BlockSpec((1, tm, D_WIDTH), lambda b, i: (b, i, 0)),
            pl.BlockSpec((1, tm, d), lambda b, i: (b, i, 0)),
            pl.BlockSpec((1, C_WIDTH), const),
            pl.BlockSpec((1, C_WIDTH), const),
            pl.BlockSpec((CHUNK, C_HEADS * CHUNK), const),
            pl.BlockSpec((CHUNK, C_WIDTH), const),
            pl.BlockSpec((C_WIDTH + D_WIDTH, d), const),
            pl.BlockSpec((4, d), const),
            pl.BlockSpec((1, 6, d), lambda b, i: (b, 0, 0)),
        ],
        out_specs=pl.BlockSpec((1, tm, d), lambda b, i: (b, i, 0)),
        out_shape=jax.ShapeDtypeStruct((bsz, length, d), F32),
        scratch_shapes=[
            pltpu.VMEM((C_HEADS * CHUNK, C_WIDTH), BF16),
            pltpu.VMEM((tm, C_WIDTH), BF16),
        ],
        compiler_params=_params(("parallel", "parallel")),
        name="gating_outproj",
    )(u, v, fo, x, ln_g.reshape(1, -1), ln_b.reshape(1, -1), wcat, bs_full, w_out, g4, mods)


def _router_kernel(x_ref, mod_ref, g_ref, wr_ref, tri_ref, xn_ref, meta_ref, metat_ref, cnt_ref, carry_ref, *, tm):
    first = jnp.logical_and(pl.program_id(0) == 0, pl.program_id(1) == 0)

    @pl.when(first)
    def _():
        carry_ref[...] = jnp.zeros_like(carry_ref)

    x = x_ref[0]
    h = _rms(x, g_ref[2:3, :]) * (1.0 + mod_ref[0, 4:5, :]) + mod_ref[0, 3:4, :]
    xn_ref[0] = h
    h_hi = h.astype(BF16)
    h_lo = (h - h_hi.astype(F32)).astype(BF16)
    logits = _bdot(jnp.concatenate([h_hi, h_hi, h_lo], axis=1), wr_ref[...])
    lane = lax.broadcasted_iota(jnp.int32, (tm, LANES), 1).astype(F32)
    neg = jnp.float32(-jnp.inf)
    lm = jnp.where(lane < N_EXPERTS, logits, neg)
    m1 = jnp.max(lm, axis=-1, keepdims=True)
    i1 = jnp.min(jnp.where(lm == m1, lane, float(LANES)), axis=-1, keepdims=True)
    lm2 = jnp.where(lane == i1, neg, lm)
    m2 = jnp.max(lm2, axis=-1, keepdims=True)
    i2 = jnp.min(jnp.where(lm2 == m2, lane, float(LANES)), axis=-1, keepdims=True)
    e = jnp.exp(m2 - m1)
    p1 = 1.0 / (1.0 + e)
    p2 = e / (1.0 + e)
    hot1 = (lane == i1).astype(F32)
    hot2 = (lane == i2).astype(F32)
    both = hot1 + hot2
    before = _bdot(tri_ref[...], both.astype(BF16)) + carry_ref[0:1, :]
    r1 = jnp.sum(before * hot1, axis=-1, keepdims=True)
    r2 = jnp.sum(before * hot2, axis=-1, keepdims=True)
    carry_ref[0:1, :] = carry_ref[0:1, :] + jnp.sum(both, axis=0, keepdims=True)
    cnt_ref[...] = carry_ref[...]
    vals = (i1, i2, p1, p2, r1, r2)
    meta = jnp.zeros((tm, LANES), F32)
    for k, val in enumerate(vals):
        meta = jnp.where(lane == k, val, meta)
    meta_ref[0] = meta
    metat_ref[...] = meta.T[0:8, :]


def _router(x, mods, g4, w_router):
    bsz, length, d = x.shape
    tm = min(ROW_TILE, length)
    wr = jnp.concatenate([w_router, jnp.zeros((d, LANES - N_EXPERTS), F32)], axis=1)
    wr_hi = wr.astype(BF16)
    wr_lo = (wr - wr_hi.astype(F32)).astype(BF16)
    wr = jnp.concatenate([wr_hi, wr_lo, wr_hi], axis=0)
    tri = jnp.asarray(np.tril(np.ones((tm, tm), np.float32), -1), dtype=BF16)
    const = lambda b, i: (0, 0)
    return pl.pallas_call(
        functools.partial(_router_kernel, tm=tm),
        grid=(bsz, length // tm),
        in_specs=[
            pl.BlockSpec((1, tm, d), lambda b, i: (b, i, 0)),
            pl.BlockSpec((1, 6, d), lambda b, i: (b, 0, 0)),
            pl.BlockSpec((4, d), const),
            pl.BlockSpec((3 * d, LANES), const),
            pl.BlockSpec((tm, tm), const),
        ],
        out_specs=[
            pl.BlockSpec((1, tm, d), lambda b, i: (b, i, 0)),
            pl.BlockSpec((1, tm, LANES), lambda b, i: (b, i, 0)),
            pl.BlockSpec((8, tm), lambda b, i: (0, b * (length // tm) + i)),
            pl.BlockSpec((8, LANES), const),
        ],
        out_shape=[
            jax.ShapeDtypeStruct((bsz, length, d), F32),
            jax.ShapeDtypeStruct((bsz, length, LANES), F32),
            jax.ShapeDtypeStruct((8, bsz * length), F32),
            jax.ShapeDtypeStruct((8, LANES), F32),
        ],
        scratch_shapes=[pltpu.VMEM((8, LANES), F32)],
        compiler_params=_params(("arbitrary", "arbitrary")),
        name="moe_router",
    )(x, mods, g4, wr, tri)


def _expert_kernel(te_ref, nv_ref, x_ref, w1_ref, w3_ref, w2_ref, o_ref, acc_ref):
    i = pl.program_id(0)
    j = pl.program_id(1)

    @pl.when(i < nv_ref[0])
    def _():
        y = _swiglu_halves(x_ref[...].astype(BF16), w1_ref, w3_ref, w2_ref, lead=(0,))

        @pl.when(j == 0)
        def _():
            acc_ref[...] = y

        @pl.when(j > 0)
        def _():
            acc_ref[...] += y

        @pl.when(j == pl.num_programs(1) - 1)
        def _():
            o_ref[...] = acc_ref[...]


def _experts(x_sorted, tile_expert, n_valid, w1, w3, w2):
    rows = x_sorted.shape[0]
    d = w1.shape[1]
    dff = w1.shape[2]
    tm = EXPERT_ROW_TILE
    n_ff = 2
    tf = dff // n_ff

    def row_map(i, j, te, nv):
        return (jnp.maximum(jnp.minimum(i, nv[0] - 1), 0), 0)

    def w13_map(i, j, te, nv):
        return (te[jnp.maximum(jnp.minimum(i, nv[0] - 1), 0)], 0, jnp.where(i < nv[0], j, n_ff - 1))

    def w2_map(i, j, te, nv):
        return (te[jnp.maximum(jnp.minimum(i, nv[0] - 1), 0)], jnp.where(i < nv[0], j, n_ff - 1), 0)

    return pl.pallas_call(
        _expert_kernel,
        grid_spec=pltpu.PrefetchScalarGridSpec(
            num_scalar_prefetch=2,
            grid=(rows // tm, n_ff),
            in_specs=[
                pl.BlockSpec((tm, d), row_map),
                pl.BlockSpec((1, d, tf), w13_map),
                pl.BlockSpec((1, d, tf), w13_map),
                pl.BlockSpec((1, tf, d), w2_map),
            ],
            out_specs=pl.BlockSpec((tm, d), row_map),
            scratch_shapes=[pltpu.VMEM((tm, d), F32)],
        ),
        out_shape=jax.ShapeDtypeStruct((rows, d), F32),
        compiler_params=_params(("arbitrary", "arbitrary")),
        name="expert_swiglu",
    )(tile_expert, n_valid, x_sorted, w1, w3, w2)


def _combine_kernel(x_ref, y_ref, meta_ref, g_ref, mod_ref, o_ref):
    meta = meta_ref[0]
    y = meta[:, 2:3] * y_ref[0, 0] + meta[:, 3:4] * y_ref[1, 0]
    o_ref[0] = x_ref[0] + mod_ref[0, 5:6, :] * _rms(y, g_ref[3:4, :])


def _combine(x, y2, meta, g4, mods):
    bsz, length, d = x.shape
    tm = min(ROW_TILE, length)
    return pl.pallas_call(
        _combine_kernel,
        grid=(bsz, length // tm),
        in_specs=[
            pl.BlockSpec((1, tm, d), lambda b, i: (b, i, 0)),
            pl.BlockSpec((2, 1, tm, d), lambda b, i: (0, b, i, 0)),
            pl.BlockSpec((1, tm, LANES), lambda b, i: (b, i, 0)),
            pl.BlockSpec((4, d), lambda b, i: (0, 0)),
            pl.BlockSpec((1, 6, d), lambda b, i: (b, 0, 0)),
        ],
        out_specs=pl.BlockSpec((1, tm, d), lambda b, i: (b, i, 0)),
        out_shape=jax.ShapeDtypeStruct((bsz, length, d), F32),
        compiler_params=_params(("parallel", "parallel")),
        name="moe_combine",
    )(x, y2, meta, g4, mods)


SC_CORES = 2
SC_SUBCORES = 16
SC_WORKERS = SC_CORES * SC_SUBCORES
SC_ROW_CHUNK = 32


def _sc_mesh():
    return plsc.VectorSubcoreMesh(core_axis_name="c", subcore_axis_name="s")


def _sc_dispatch(rows_in, pos2, n_out):
    tokens, width = rows_in.shape
    per_worker = tokens // SC_WORKERS
    chunk = SC_ROW_CHUNK
    n_pairs = per_worker // (2 * chunk)
    idx = pos2.reshape(2, SC_WORKERS, 2 * n_pairs, chunk)

    @functools.partial(
        pl.kernel, mesh=_sc_mesh(),
        out_type=jax.ShapeDtypeStruct((n_out, width), rows_in.dtype),
        scratch_types=[
            pltpu.VMEM((2 * n_pairs, chunk), jnp.int32),
            pltpu.VMEM((2 * n_pairs, chunk), jnp.int32),
            pltpu.VMEM((2, chunk, width), rows_in.dtype),
        ] + [pltpu.SemaphoreType.DMA] * 6,
        name="sc_dispatch",
    )
    def body(rows_hbm, idx_hbm, out_hbm, idx0_v, idx1_v, rows_v, rsem0, rsem1, wsem0, wsem1, wsem2, wsem3):
        wid = lax.axis_index("s") * SC_CORES + lax.axis_index("c")
        base = wid * per_worker
        rsem = (rsem0, rsem1)
        wsem = ((wsem0, wsem1), (wsem2, wsem3))
        pltpu.sync_copy(idx_hbm.at[0, wid], idx0_v)
        pltpu.sync_copy(idx_hbm.at[1, wid], idx1_v)

        @pl.loop(0, n_pairs)
        def _(p):
            c0 = 2 * p
            reads = [pltpu.async_copy(rows_hbm.at[pl.ds(base + (c0 + s) * chunk, chunk)], rows_v.at[s], rsem[s])
                     for s in range(2)]
            writes = []
            for s in range(2):
                reads[s].wait()
                writes.append(pltpu.async_copy(rows_v.at[s], out_hbm.at[idx0_v.at[c0 + s]], wsem[s][0]))
                writes.append(pltpu.async_copy(rows_v.at[s], out_hbm.at[idx1_v.at[c0 + s]], wsem[s][1]))
            for w in writes:
                w.wait()

    return body(rows_in, idx)


def _sc_gather(table, idx):
    n = idx.shape[0]
    width = table.shape[1]
    chunk = SC_ROW_CHUNK
    per_worker = n // SC_WORKERS
    n_pairs = per_worker // (2 * chunk)
    idx3 = idx.reshape(SC_WORKERS, 2 * n_pairs, chunk)

    @functools.partial(
        pl.kernel, mesh=_sc_mesh(),
        out_type=jax.ShapeDtypeStruct((n, width), table.dtype),
        scratch_types=[
            pltpu.VMEM((2 * n_pairs, chunk), jnp.int32),
            pltpu.VMEM((2, chunk, width), table.dtype),
            pltpu.SemaphoreType.DMA, pltpu.SemaphoreType.DMA,
            pltpu.SemaphoreType.DMA, pltpu.SemaphoreType.DMA,
        ],
        name="sc_gather",
    )
    def body(table_hbm, idx_hbm, out_hbm, idx_v, rows_v, gsem0, gsem1, wsem0, wsem1):
        wid = lax.axis_index("s") * SC_CORES + lax.axis_index("c")
        base = wid * per_worker
        gsem = (gsem0, gsem1)
        wsem = (wsem0, wsem1)
        pltpu.sync_copy(idx_hbm.at[wid], idx_v)

        @pl.loop(0, n_pairs)
        def _(p):
            c0 = 2 * p
            gathers = [pltpu.async_copy(table_hbm.at[idx_v.at[c0 + s]], rows_v.at[s], gsem[s]) for s in range(2)]
            writes = []
            for s in range(2):
                gathers[s].wait()
                rows_out = out_hbm.at[pl.ds(base + (c0 + s) * chunk, chunk)]
                writes.append(pltpu.async_copy(rows_v.at[s], rows_out, wsem[s]))
            for s in range(2):
                writes[s].wait()

    return body(table, idx3)


def _moe(x, mods, g4, w_router, w1, w3, w2):
    bsz, length, d = x.shape
    tokens = bsz * length
    tm = EXPERT_ROW_TILE
    xn, meta, meta_t, counts = _router(x, mods, g4, w_router)
    experts = meta_t[0:2].astype(jnp.int32)
    ranks = meta_t[4:6].astype(jnp.int32)
    cnt = counts[0, 0:N_EXPERTS].astype(jnp.int32)
    padded = ((cnt + tm - 1) // tm) * tm
    ends = jnp.cumsum(padded)
    starts = ends - padded
    group_start = jnp.zeros_like(experts)
    for e in range(N_EXPERTS):
        group_start = jnp.where(experts == e, starts[e], group_start)
    pos = (group_start + ranks).reshape(2 * tokens)
    rows = 2 * tokens + N_EXPERTS * tm
    n_tiles = rows // tm
    n_valid = (ends[-1] // tm).astype(jnp.int32).reshape(1)
    tile_start = jnp.arange(n_tiles, dtype=jnp.int32) * tm
    tile_expert = jnp.minimum(
        jnp.sum((tile_start[:, None] >= ends[None, :]).astype(jnp.int32), axis=1), N_EXPERTS - 1)
    x_sorted = _sc_dispatch(xn.reshape(tokens, d), pos.reshape(2, tokens), rows)
    y_sorted = _experts(x_sorted, tile_expert, n_valid, w1, w3, w2)
    y2 = _sc_gather(y_sorted, pos).reshape(2, bsz, length, d)
    return _combine(x, y2, meta, g4, mods)


def _grid_pos_embed(rows, d):
    row = np.repeat(np.arange(rows, dtype=np.float64), GRID_W)
    col = np.tile(np.arange(GRID_W, dtype=np.float64), rows)
    n_freq = d // 4
    omega = POS_THETA ** (-np.arange(n_freq, dtype=np.float64) / n_freq)
    ang_r = row[:, None] * omega
    ang_c = col[:, None] * omega
    return jnp.asarray(np.concatenate([np.sin(ang_r), np.cos(ang_r), np.sin(ang_c), np.cos(ang_c)], axis=-1),
                       dtype=F32)


def kernel(x, c, ctx, c_ctx, ada_w, ada_b, norm_g, ab_w_in, rg_conv_w, rg_conv_b, rg_w_r, rg_b_r, rg_w_i, rg_b_i, rg_lambda, cv_dw_w, cv_dw_b, cv_ln_g, cv_ln_b, ab_w_out, ffn_w1, ffn_w3, ffn_w2, cd_w_in, sg_ln_g, sg_ln_b, sg_w_s, sg_b_s, cd_w_out, moe_router, moe_w1, moe_w3, moe_w2):
    bsz, length, d = x.shape
    assert ada_w.shape[0] == 2, "two layers: one even (RG-LRU | Conformer), one odd (gMLP | Fourier)"
    pos = _grid_pos_embed(length // GRID_W, d)

    cpad = jnp.concatenate([c, c_ctx[None, :], jnp.zeros((16 - bsz - 1, d), F32)], axis=0)
    m = _ada(cpad, ada_w, ada_b)
    mods = m[:, :bsz].reshape(2, bsz, 6, d)
    mods_ctx = m[:, bsz:bsz + 1].reshape(2, 1, 6, d)

    mods_t = mods[0].transpose(1, 0, 2)
    mods_ctx_t = jnp.broadcast_to(mods_ctx[0].reshape(6, 1, d), (6, bsz, d))
    pos3 = pos[:, None, :]
    w_in = ab_w_in[0]
    xa3, gg3, glu3, x0_t, ffn1b, ffn3b, ffn2b = _inproj_t(
        x, pos3, mods_t, norm_g[0], w_in, rg_conv_w[0], rg_conv_b[0], True, riders=(ffn_w1[0], ffn_w3[0], ffn_w2[0]))
    (xc3,) = _inproj_t(ctx, None, mods_ctx_t, norm_g[0], w_in[:, 0:A_WIDTH], rg_conv_w[0], rg_conv_b[0], False)
    wg, bg, sp = _gate_weights(rg_w_r[0], rg_b_r[0], rg_w_i[0], rg_b_i[0], rg_lambda[0])
    n_exp, _, dff_e = moe_w1[0].shape
    rec_hi, rec_lo, w2b = _scan_t(xa3, xc3, gg3, wg, bg, sp, riders=(moe_w2[0].reshape(n_exp * dff_e, d),))
    (u3,) = _conv_t(glu3, cv_dw_w[0], cv_dw_b[0], cv_ln_g[0], cv_ln_b[0])
    x1_t = _outproj_t(rec_lo, rec_hi, u3, x0_t, ab_w_out[0], norm_g[0], mods_t)
    x2, w1b, w3b = _ffn_t(x1_t, mods_t, norm_g[0], ffn1b, ffn3b, ffn2b,
                          riders=(moe_w1[0].reshape(n_exp * d, dff_e), moe_w3[0].reshape(n_exp * d, dff_e)))

    u, v, f = _inproj_cd(x2, mods[1], norm_g[1], cd_w_in[0])
    fo = _fourier(f)
    x3 = _gate_out(u, v, fo, x2, sg_ln_g[0], sg_ln_b[0], sg_w_s[0], sg_b_s[0], cd_w_out[0],
                   norm_g[1], mods[1])
    return _moe(x3, mods[1], norm_g[1], moe_router[0], w1b.reshape(n_exp, d, dff_e), w3b.reshape(n_exp, d, dff_e),
                w2b.reshape(n_exp, dff_e, d))
```

```python
import functools

import numpy as np
import jax
import jax.numpy as jnp
from jax import lax
from jax.experimental import pallas as pl
from jax.experimental.pallas import tpu as pltpu
from jax.experimental.pallas import tpu_sc as plsc

F32 = jnp.float32
BF16 = jnp.bfloat16

GRID_W = 64
POS_THETA = 10000.0
NORM_EPS = 1e-6
LN_EPS = 1e-5
LRU_C = 8.0

A_WIDTH = 512
A_HEADS = 8
A_CONV_W = 4
B_WIDTH = 512
B_CONV_W = 31
C_WIDTH = 512
C_HEADS = 8
CHUNK = 128
D_WIDTH = 512
D_GROUPS = 4
D_GROUP_DIM = D_WIDTH // D_GROUPS
N_EXPERTS = 8

LANES = 128
MXU_WIDTH = 256
ROW_TILE = 512
EXPERT_ROW_TILE = 512
VMEM_LIMIT = 56 * 2 ** 20


def _params(sem):
    return pltpu.CompilerParams(dimension_semantics=sem, vmem_limit_bytes=VMEM_LIMIT)


def _rms(x, g):
    return x * lax.rsqrt(jnp.mean(x * x, axis=-1, keepdims=True) + NORM_EPS) * g


def _bdot(a, b):
    return jnp.dot(a, b, preferred_element_type=F32)


def _swiglu_halves(x, w1_ref, w3_ref, w2_ref, lead=()):
    n = w1_ref.shape[-1]
    half = -(-(n // 2) // MXU_WIDTH) * MXU_WIDTH
    y = None
    for lo, hi in ((0, half), (half, n)):
        a = _bdot(x, w1_ref[lead + (slice(None), slice(lo, hi))])
        b = _bdot(x, w3_ref[lead + (slice(None), slice(lo, hi))])
        t = (jax.nn.silu(a) * b).astype(BF16)
        part = _bdot(t, w2_ref[lead + (slice(lo, hi), slice(None))])
        y = part if y is None else y + part
    return y


def _ada_kernel(c_ref, w_ref, b_ref, o_ref):
    s = jax.nn.silu(c_ref[...])
    o_ref[0] = _bdot(s.astype(BF16), w_ref[0].astype(BF16)) + b_ref[0]


def _ada(cpad, ada_w, ada_b):
    nl, d, n6 = ada_w.shape
    rows = cpad.shape[0]
    tn = n6 // 4
    return pl.pallas_call(
        _ada_kernel,
        grid=(nl, n6 // tn),
        in_specs=[
            pl.BlockSpec((rows, d), lambda l, j: (0, 0)),
            pl.BlockSpec((1, d, tn), lambda l, j: (l, 0, j)),
            pl.BlockSpec((1, 1, tn), lambda l, j: (l, 0, j)),
        ],
        out_specs=pl.BlockSpec((1, rows, tn), lambda l, j: (l, 0, j)),
        out_shape=jax.ShapeDtypeStruct((nl, rows, n6), F32),
        compiler_params=_params(("arbitrary", "arbitrary")),
        name="ada_terms",
    )(cpad, ada_w, ada_b.reshape(nl, 1, n6))


TIME_TILE = 128
SCAN_TIME_TILE = 128
SCAN_LANES = 256
CONV_TIME_TILE = 128
CONV_HALO = 16


def _time_major(ref):
    return jnp.swapaxes(ref[...], 0, 1)


def _cast_riders(cast_in, cast_out):
    for src, dst in zip(cast_in, cast_out):
        dst[...] = src[...].astype(BF16)


def _rider_specs(arrays, n_steps, step_of):
    in_specs, out_specs, out_shapes = [], [], []
    for arr in arrays:
        rows, cols = arr.shape
        block = (rows // n_steps, cols)
        in_specs.append(pl.BlockSpec(block, lambda *idx: (step_of(*idx), 0)))
        out_specs.append(pl.BlockSpec(block, lambda *idx: (step_of(*idx), 0)))
        out_shapes.append(jax.ShapeDtypeStruct(arr.shape, BF16))
    return in_specs, out_specs, out_shapes


def _inproj_t_kernel(*refs, tt, add_pos, branches, n_cast):
    if n_cast:
        n_in = len(refs) - n_cast - (4 if branches else 1) - n_cast
        _cast_riders(refs[n_in:n_in + n_cast], refs[len(refs) - n_cast:])
        refs = refs[:n_in] + refs[n_in + n_cast:len(refs) - n_cast]
    if add_pos:
        x_ref, xp_ref, xn_ref, pos_ref, posp_ref, posn_ref, mod_ref, g_ref, w_ref, cw_ref, cb_ref, *outs = refs
        xall = jnp.concatenate(
            [_time_major(xp_ref)[7:8] + posp_ref[...], _time_major(x_ref) + pos_ref[...],
             _time_major(xn_ref)[0:2] + posn_ref[...]], axis=0)
    else:
        x_ref, xp_ref, xn_ref, mod_ref, g_ref, w_ref, cw_ref, cb_ref, *outs = refs
        xall = jnp.concatenate([_time_major(xp_ref)[7:8], _time_major(x_ref), _time_major(xn_ref)[0:2]], axis=0)
    i = pl.program_id(0)
    last = pl.num_programs(0) - 1
    steps, bsz, d = xall.shape
    h = _rms(xall, g_ref[0:1, :]) * (1.0 + mod_ref[1]) + mod_ref[0]
    z = _bdot(h.reshape(steps * bsz, d).astype(BF16), w_ref[...].astype(BF16))
    z3 = z.reshape(steps, bsz, z.shape[-1])
    t_idx = lax.broadcasted_iota(jnp.int32, (steps, bsz, A_WIDTH), 0)
    inside = jnp.logical_and(jnp.logical_or(t_idx >= 1, i > 0), jnp.logical_or(t_idx <= tt, i < last))
    xa = jnp.where(inside, z3[:, :, 0:A_WIDTH], 0.0)
    xconv = cb_ref[...] + cw_ref[0:1, :] * xa[0:tt]
    for k in range(1, A_CONV_W):
        xconv = xconv + cw_ref[k:k + 1, :] * xa[k:k + tt]
    outs[0][...] = xconv
    if branches:
        outs[1][...] = jax.nn.gelu(z3[1:tt + 1, :, A_WIDTH:2 * A_WIDTH])
        vb = z3[1:tt + 1, :, 2 * A_WIDTH:2 * A_WIDTH + B_WIDTH]
        gb = z3[1:tt + 1, :, 2 * A_WIDTH + B_WIDTH:]
        outs[2][...] = vb * jax.nn.sigmoid(gb)
        outs[3][...] = xall[1:tt + 1]


def _inproj_t(x, pos3, mods_t, g4, w, conv_w, conv_b, branches, riders=()):
    bsz, length, d = x.shape
    n = w.shape[1]
    tt = min(TIME_TILE, length)
    add_pos = pos3 is not None
    n_out = 3 if branches else 1
    prev_map = lambda i: (jnp.maximum(i * tt - 1, 0), 0, 0)
    next_map = lambda i: (jnp.minimum((i + 1) * (tt // 2), length // 2 - 1), 0, 0)
    in_specs = [
        pl.BlockSpec((bsz, tt, d), lambda i: (0, i, 0)),
        pl.BlockSpec((bsz, 8, d), lambda i: (0, jnp.maximum(i * (tt // 8) - 1, 0), 0)),
        pl.BlockSpec((bsz, 8, d), lambda i: (0, jnp.minimum((i + 1) * (tt // 8), length // 8 - 1), 0)),
    ]
    args = [x, x, x]
    if add_pos:
        in_specs += [
            pl.BlockSpec((tt, 1, d), lambda i: (i, 0, 0)),
            pl.BlockSpec((1, 1, d), prev_map),
            pl.BlockSpec((2, 1, d), next_map),
        ]
        args += [pos3, pos3, pos3]
    in_specs += [
        pl.BlockSpec((6, bsz, d), lambda i: (0, 0, 0)),
        pl.BlockSpec((4, d), lambda i: (0, 0)),
        pl.BlockSpec((d, n), lambda i: (0, 0)),
        pl.BlockSpec((A_CONV_W, A_WIDTH), lambda i: (0, 0)),
        pl.BlockSpec((1, A_WIDTH), lambda i: (0, 0)),
    ]
    args += [mods_t, g4, w, conv_w, conv_b.reshape(1, A_WIDTH)]
    r_in, r_out, r_shapes = _rider_specs(riders, length // tt, lambda i: i)
    out_specs = [pl.BlockSpec((tt, bsz, A_WIDTH), lambda i: (i, 0, 0))] * n_out
    out_shapes = [jax.ShapeDtypeStruct((length, bsz, A_WIDTH), F32)] * n_out
    if branches:
        out_specs.append(pl.BlockSpec((tt, bsz, d), lambda i: (i, 0, 0)))
        out_shapes.append(jax.ShapeDtypeStruct((length, bsz, d), F32))
    return pl.pallas_call(
        functools.partial(_inproj_t_kernel, tt=tt, add_pos=add_pos, branches=branches, n_cast=len(riders)),
        grid=(length // tt,),
        in_specs=in_specs + r_in,
        out_specs=out_specs + r_out,
        out_shape=out_shapes + r_shapes,
        compiler_params=_params(("parallel",)),
        name="inproj_ab" if branches else "inproj_ctx",
    )(*args, *riders)


def _scan_t_kernel(*refs, tb, nb, ctx_len, n_cast):
    xf_ref, xb_ref, ggf_ref, ggb_ref, xc_ref, wg_ref, bg_ref, sp_ref = refs[:8]
    cast_in = refs[8:8 + n_cast]
    hi_ref, lo_ref = refs[8 + n_cast:10 + n_cast]
    cast_out = refs[10 + n_cast:10 + 2 * n_cast]
    a_ref, b_ref, h_ref, state_ref = refs[10 + 2 * n_cast:]
    _cast_riders(cast_in, cast_out)
    k = pl.program_id(1)
    half = nb // 2
    bsz, lanes = xf_ref.shape[1], xf_ref.shape[2]
    chunk = 128

    def coefficients(x_ref, n, direction):
        def body(c, _):
            t0 = pl.multiple_of(c * chunk, chunk)
            x = x_ref[pl.ds(t0, chunk)].reshape(chunk * bsz, lanes)
            th = jnp.tanh(_bdot(x.astype(BF16), wg_ref[0, direction]) + bg_ref[0, direction])
            i = 0.5 * th[:, lanes:] + 0.5
            c = (-0.5 * LRU_C) * sp_ref[0, direction:direction + 1, :]
            log_a = c * th[:, 0:lanes] + c
            a = jnp.exp(log_a)
            one_minus_a2 = -jnp.tanh(log_a) * (a * a + 1.0)
            a_ref[direction, pl.ds(t0, chunk)] = a.reshape(chunk, bsz, lanes)
            b_ref[direction, pl.ds(t0, chunk)] = (jnp.sqrt(one_minus_a2) * (i * x)).reshape(chunk, bsz, lanes)
            return 0
        lax.fori_loop(0, n // chunk, body, 0)

    def sweep(n, store):
        def body(t, carry):
            hf, hb = carry
            tr = n - 1 - t
            hf = a_ref[0, pl.ds(t, 1)][0] * hf + b_ref[0, pl.ds(t, 1)][0]
            hb = a_ref[1, pl.ds(tr, 1)][0] * hb + b_ref[1, pl.ds(tr, 1)][0]
            if store:
                b_ref[0, pl.ds(t, 1)] = hf[None]
                b_ref[1, pl.ds(tr, 1)] = hb[None]
            return hf, hb
        hf, hb = lax.fori_loop(0, n, body, (state_ref[0], state_ref[1]), unroll=8)
        state_ref[0] = hf
        state_ref[1] = hb

    @pl.when(k == 0)
    def _():
        state_ref[...] = jnp.zeros_like(state_ref)
        coefficients(xc_ref, ctx_len, 0)
        coefficients(xc_ref, ctx_len, 1)
        sweep(ctx_len, False)

    coefficients(xf_ref, tb, 0)
    coefficients(xb_ref, tb, 1)
    sweep(tb, True)
    m = nb - 1 - k

    @pl.when(k < half)
    def _():
        h_ref[k] = b_ref[0, 0:tb]
        h_ref[m] = b_ref[1, 0:tb]

    @pl.when(k >= half)
    def _():
        hi_ref[...] = (b_ref[0, 0:tb] + h_ref[k]) * ggf_ref[...]
        lo_ref[...] = (h_ref[m] + b_ref[1, 0:tb]) * ggb_ref[...]


def _scan_t(xa3, xc3, gg3, wg, bg, sp, riders=()):
    length, bsz, width = xa3.shape
    ctx_len = xc3.shape[0]
    tb = min(SCAN_TIME_TILE, length // 2)
    nb = length // tb
    half = nb // 2
    lanes = SCAN_LANES
    blk = (tb, bsz, lanes)
    r_in, r_out, r_shapes = _rider_specs(riders, (width // lanes) * nb, lambda g, k: g * nb + k)
    return pl.pallas_call(
        functools.partial(_scan_t_kernel, tb=tb, nb=nb, ctx_len=ctx_len, n_cast=len(riders)),
        grid=(width // lanes, nb),
        in_specs=[
            pl.BlockSpec(blk, lambda g, k: (k, 0, g)),
            pl.BlockSpec(blk, lambda g, k: (nb - 1 - k, 0, g)),
            pl.BlockSpec(blk, lambda g, k: (jnp.maximum(k, half), 0, g)),
            pl.BlockSpec(blk, lambda g, k: (jnp.minimum(nb - 1 - k, half - 1), 0, g)),
            pl.BlockSpec((ctx_len, bsz, lanes), lambda g, k: (0, 0, g)),
            pl.BlockSpec((1, 2, lanes, 2 * lanes), lambda g, k: (g, 0, 0, 0)),
            pl.BlockSpec((1, 2, 1, 2 * lanes), lambda g, k: (g, 0, 0, 0)),
            pl.BlockSpec((1, 2, lanes), lambda g, k: (g, 0, 0)),
        ] + r_in,
        out_specs=[
            pl.BlockSpec(blk, lambda g, k: (jnp.maximum(k - half, 0), 0, g)),
            pl.BlockSpec(blk, lambda g, k: (jnp.minimum(nb - 1 - k, half - 1), 0, g)),
        ] + r_out,
        out_shape=[jax.ShapeDtypeStruct((length // 2, bsz, width), F32)] * 2 + r_shapes,
        scratch_shapes=[
            pltpu.VMEM((2, max(tb, ctx_len), bsz, lanes), F32),
            pltpu.VMEM((2, max(tb, ctx_len), bsz, lanes), F32),
            pltpu.VMEM((nb, tb, bsz, lanes), F32),
            pltpu.VMEM((2, bsz, lanes), F32),
        ],
        compiler_params=_params(("parallel", "arbitrary")),
        name="rglru_scan",
    )(xa3, xa3, gg3, gg3, xc3, wg, bg, sp, *riders)


def _gate_weights(w_r, b_r, w_i, b_i, lam):
    hd = A_WIDTH // A_HEADS
    heads_per_group = SCAN_LANES // hd
    ngroups = A_WIDTH // SCAN_LANES

    def blockdiag(w):
        w = w.reshape(ngroups, heads_per_group, hd, hd)
        eye = jnp.eye(heads_per_group, dtype=w.dtype)
        return jnp.einsum("ghde,hk->ghdke", w, eye).reshape(ngroups, SCAN_LANES, SCAN_LANES)

    wg = jnp.stack([jnp.concatenate([blockdiag(w_r[d]), blockdiag(w_i[d])], axis=-1) for d in range(2)], axis=1)
    bg = jnp.stack([jnp.concatenate([b_r[d].reshape(ngroups, 1, SCAN_LANES), b_i[d].reshape(ngroups, 1, SCAN_LANES)],
                                    axis=-1) for d in range(2)], axis=1)
    sp = jax.nn.softplus(-lam.astype(F32)).reshape(2, ngroups, SCAN_LANES).transpose(1, 0, 2)
    return (0.5 * wg).astype(BF16), 0.5 * bg, sp


def _conv_t_kernel(*refs, tb, n_cast):
    x_ref, xp_ref, xn_ref, w_ref, b_ref, lng_ref, lnb_ref = refs[:7]
    cast_in = refs[7:7 + n_cast]
    o_ref = refs[7 + n_cast]
    cast_out = refs[8 + n_cast:8 + 2 * n_cast]
    stage_ref = refs[-1]
    _cast_riders(cast_in, cast_out)
    i = pl.program_id(0)
    last = pl.num_programs(0) - 1
    halo = CONV_HALO
    stage_ref[0:halo] = jnp.where(i > 0, xp_ref[...], 0.0)
    stage_ref[halo:halo + tb] = x_ref[...]
    stage_ref[halo + tb:2 * halo + tb] = jnp.where(i < last, xn_ref[...], 0.0)
    sub = 8

    def body(r, _):
        t0 = pl.multiple_of(r * sub, sub)
        def tap(k, acc):
            return acc + w_ref[k] * stage_ref[pl.ds(t0 + 1 + k, sub)]
        acc = lax.fori_loop(0, B_CONV_W + 1, tap, jnp.broadcast_to(b_ref[...], (sub,) + b_ref.shape), unroll=8)
        o_ref[pl.ds(t0, sub)] = acc
        return 0
    lax.fori_loop(0, tb // sub, body, 0)

    norm_rows = 64

    def norm_body(r, _):
        t0 = pl.multiple_of(r * norm_rows, norm_rows)
        acc = o_ref[pl.ds(t0, norm_rows)]
        mu = jnp.mean(acc, axis=-1, keepdims=True)
        cen = acc - mu
        var = jnp.mean(cen * cen, axis=-1, keepdims=True)
        y = cen * lax.rsqrt(var + LN_EPS) * lng_ref[...] + lnb_ref[...]
        o_ref[pl.ds(t0, norm_rows)] = jax.nn.silu(y)
        return 0
    lax.fori_loop(0, tb // norm_rows, norm_body, 0)


def _conv_t(glu3, dw_w, dw_b, ln_g, ln_b, riders=()):
    length, bsz, width = glu3.shape
    tb = min(CONV_TIME_TILE, length)
    halo = CONV_HALO
    taps = dw_w.shape[0]
    w8 = jnp.broadcast_to(jnp.concatenate([dw_w, jnp.zeros((32 - taps, width), F32)], axis=0)[:, None, :],
                          (32, bsz, width))
    b8 = jnp.broadcast_to(dw_b[None, :], (bsz, width))
    r_in, r_out, r_shapes = _rider_specs(riders, length // tb, lambda i: i)
    return pl.pallas_call(
        functools.partial(_conv_t_kernel, tb=tb, n_cast=len(riders)),
        grid=(length // tb,),
        in_specs=[
            pl.BlockSpec((tb, bsz, width), lambda i: (i, 0, 0)),
            pl.BlockSpec((halo, bsz, width), lambda i: (jnp.maximum(i * (tb // halo) - 1, 0), 0, 0)),
            pl.BlockSpec((halo, bsz, width), lambda i: (jnp.minimum((i + 1) * (tb // halo), length // halo - 1), 0, 0)),
            pl.BlockSpec((32, bsz, width), lambda i: (0, 0, 0)),
            pl.BlockSpec((bsz, width), lambda i: (0, 0)),
            pl.BlockSpec((1, width), lambda i: (0, 0)),
            pl.BlockSpec((1, width), lambda i: (0, 0)),
        ] + r_in,
        out_specs=[pl.BlockSpec((tb, bsz, width), lambda i: (i, 0, 0))] + r_out,
        out_shape=[jax.ShapeDtypeStruct((length, bsz, width), F32)] + r_shapes,
        scratch_shapes=[pltpu.VMEM((tb + 2 * halo, bsz, width), F32)],
        compiler_params=_params(("parallel",)),
        name="conformer_conv",
    )(glu3, glu3, glu3, w8, b8, ln_g.reshape(1, -1), ln_b.reshape(1, -1), *riders)


def _outproj_t_kernel(lo_ref, hi_ref, u_ref, x_ref, wout_ref, g_ref, mod_ref, o_ref, *, half_steps):
    i = pl.program_id(0)
    tt, bsz, d = x_ref.shape
    rec = jnp.where(i < half_steps, lo_ref[...], hi_ref[...])
    rows = tt * bsz
    y = (_bdot(rec.reshape(rows, A_WIDTH).astype(BF16), wout_ref[0:A_WIDTH, :].astype(BF16))
         + _bdot(u_ref[...].reshape(rows, B_WIDTH).astype(BF16), wout_ref[A_WIDTH:, :].astype(BF16)))
    yn = _rms(y, g_ref[1:2, :]).reshape(tt, bsz, d)
    o_ref[...] = x_ref[...] + mod_ref[2] * yn


def _outproj_t(rec_lo, rec_hi, u3, x0_t, w_out, g4, mods_t):
    length, bsz, d = x0_t.shape
    tt = min(TIME_TILE, length // 2)
    half_steps = (length // 2) // tt
    return pl.pallas_call(
        functools.partial(_outproj_t_kernel, half_steps=half_steps),
        grid=(length // tt,),
        in_specs=[
            pl.BlockSpec((tt, bsz, A_WIDTH), lambda i: (jnp.minimum(i, half_steps - 1), 0, 0)),
            pl.BlockSpec((tt, bsz, A_WIDTH), lambda i: (jnp.maximum(i - half_steps, 0), 0, 0)),
            pl.BlockSpec((tt, bsz, B_WIDTH), lambda i: (i, 0, 0)),
            pl.BlockSpec((tt, bsz, d), lambda i: (i, 0, 0)),
            pl.BlockSpec((A_WIDTH + B_WIDTH, d), lambda i: (0, 0)),
            pl.BlockSpec((4, d), lambda i: (0, 0)),
            pl.BlockSpec((6, bsz, d), lambda i: (0, 0, 0)),
        ],
        out_specs=pl.BlockSpec((tt, bsz, d), lambda i: (i, 0, 0)),
        out_shape=jax.ShapeDtypeStruct((length, bsz, d), F32),
        compiler_params=_params(("parallel",)),
        name="mixer_ab_outproj",
    )(rec_lo, rec_hi, u3, x0_t, w_out, g4, mods_t)


def _ffn_t_kernel(x_ref, mod_ref, g_ref, w1_ref, w3_ref, w2_ref, *rest, n_cast):
    _cast_riders(rest[:n_cast], rest[n_cast + 1:])
    o_ref = rest[n_cast]
    x = x_ref[...]
    tt, bsz, d = x.shape
    h = (_rms(x, g_ref[2:3, :]) * (1.0 + mod_ref[4]) + mod_ref[3]).reshape(tt * bsz, d).astype(BF16)
    y = _swiglu_halves(h, w1_ref, w3_ref, w2_ref)
    o_ref[...] = jnp.swapaxes(x + mod_ref[5] * _rms(y, g_ref[3:4, :]).reshape(tt, bsz, d), 0, 1)


def _ffn_t(x3, mods_t, g4, w1, w3, w2, riders=()):
    length, bsz, d = x3.shape
    dff = w1.shape[1]
    tt = min(TIME_TILE // 2, length)
    r_in, r_out, r_shapes = _rider_specs(riders, length // tt, lambda i: i)
    return pl.pallas_call(
        functools.partial(_ffn_t_kernel, n_cast=len(riders)),
        grid=(length // tt,),
        in_specs=[
            pl.BlockSpec((tt, bsz, d), lambda i: (i, 0, 0)),
            pl.BlockSpec((6, bsz, d), lambda i: (0, 0, 0)),
            pl.BlockSpec((4, d), lambda i: (0, 0)),
            pl.BlockSpec((d, dff), lambda i: (0, 0), pipeline_mode=pl.Buffered(1)),
            pl.BlockSpec((d, dff), lambda i: (0, 0), pipeline_mode=pl.Buffered(1)),
            pl.BlockSpec((dff, d), lambda i: (0, 0), pipeline_mode=pl.Buffered(1)),
        ] + r_in,
        out_specs=[pl.BlockSpec((bsz, tt, d), lambda i: (0, i, 0))] + r_out,
        out_shape=[jax.ShapeDtypeStruct((bsz, length, d), F32)] + r_shapes,
        compiler_params=_params(("parallel",)),
        name="dense_swiglu",
    )(x3, mods_t, g4, w1, w3, w2, *riders)


def _inproj_cd_kernel(x_ref, mod_ref, g_ref, w_ref, u_ref, v_ref, f_ref):
    x = x_ref[0]
    h = _rms(x, g_ref[0:1, :]) * (1.0 + mod_ref[0, 1:2, :]) + mod_ref[0, 0:1, :]
    z = _bdot(h.astype(BF16), w_ref[...].astype(BF16))
    u_ref[0] = jax.nn.gelu(z[:, 0:C_WIDTH])
    v_ref[0] = jax.nn.gelu(z[:, C_WIDTH:2 * C_WIDTH])
    f_ref[0] = z[:, 2 * C_WIDTH:]


def _inproj_cd(x, mods, g4, w):
    bsz, length, d = x.shape
    n = w.shape[1]
    tm = min(2 * ROW_TILE, length)
    return pl.pallas_call(
        _inproj_cd_kernel,
        grid=(bsz, length // tm),
        in_specs=[
            pl.BlockSpec((1, tm, d), lambda b, i: (b, i, 0)),
            pl.BlockSpec((1, 6, d), lambda b, i: (b, 0, 0)),
            pl.BlockSpec((4, d), lambda b, i: (0, 0)),
            pl.BlockSpec((d, n), lambda b, i: (0, 0)),
        ],
        out_specs=[pl.BlockSpec((1, tm, C_WIDTH), lambda b, i: (b, i, 0))] * 3,
        out_shape=[jax.ShapeDtypeStruct((bsz, length, C_WIDTH), F32)] * 3,
        compiler_params=_params(("parallel", "parallel")),
        name="inproj_cd",
    )(x, mods, g4, w)


def _fourier_kernel(f_ref, dmat_ref, lmat_ref, o_ref, z_ref, *, length, scale):
    j = pl.program_id(1)
    rows_per_step = 512 if length % 512 == 0 else length

    @pl.when(j == 0)
    def _():
        def body(c, _):
            rows = pl.ds(pl.multiple_of(c * rows_per_step, rows_per_step), rows_per_step)
            fb = f_ref[0, rows, :].astype(BF16)
            for g in range(D_GROUPS):
                cs = _bdot(fb[:, g * D_GROUP_DIM:(g + 1) * D_GROUP_DIM], dmat_ref[...])
                z_ref[0, rows, g * D_GROUP_DIM:(g + 1) * D_GROUP_DIM] = cs[:, 0:D_GROUP_DIM].astype(BF16)
                z_ref[1, rows, g * D_GROUP_DIM:(g + 1) * D_GROUP_DIM] = cs[:, D_GROUP_DIM:].astype(BF16)
            return 0
        lax.fori_loop(0, length // rows_per_step, body, 0)

    out = _bdot(lmat_ref[0], z_ref[0]) + _bdot(lmat_ref[1], z_ref[1])
    o_ref[0] = (out * scale).astype(BF16)


def _dft_matrices(length):
    k = np.arange(length, dtype=np.int64)
    ang_l = 2.0 * np.pi * ((k[:, None] * k[None, :]) % length).astype(np.float64) / length
    lmat = np.stack([np.cos(ang_l), -np.sin(ang_l)], axis=0)
    d = np.arange(D_GROUP_DIM, dtype=np.int64)
    ang_d = 2.0 * np.pi * ((d[:, None] * d[None, :]) % D_GROUP_DIM).astype(np.float64) / D_GROUP_DIM
    dmat = np.concatenate([np.cos(ang_d), np.sin(ang_d)], axis=1)
    return jnp.asarray(lmat, dtype=F32).astype(BF16), jnp.asarray(dmat, dtype=F32).astype(BF16)


def _fourier(f):
    bsz, length, width = f.shape
    tm = min(2 * ROW_TILE, length)
    lmat, dmat = _dft_matrices(length)
    scale = float(1.0 / np.sqrt(length * D_GROUP_DIM))
    return pl.pallas_call(
        functools.partial(_fourier_kernel, length=length, scale=scale),
        grid=(bsz, length // tm),
        in_specs=[
            pl.BlockSpec((1, length, width), lambda b, i: (b, 0, 0)),
            pl.BlockSpec((D_GROUP_DIM, 2 * D_GROUP_DIM), lambda b, i: (0, 0)),
            pl.BlockSpec((2, tm, length), lambda b, i: (0, i, 0)),
        ],
        out_specs=pl.BlockSpec((1, tm, width), lambda b, i: (b, i, 0)),
        out_shape=jax.ShapeDtypeStruct((bsz, length, width), BF16),
        scratch_shapes=[pltpu.VMEM((2, length, width), BF16)],
        compiler_params=_params(("parallel", "arbitrary")),
        name="fourier_mix",
    )(f, dmat, lmat)


def _gate_out_kernel(u_ref, v_ref, fo_ref, x_ref, lng_ref, lnb_ref, wcat_ref, bs_ref, wout_ref, g_ref, mod_ref,
                     o_ref, vs_ref, gt_ref, *, tm):
    head_dim = C_WIDTH // C_HEADS
    head_of_lane = lax.broadcasted_iota(jnp.int32, (CHUNK, C_WIDTH), 1) // head_dim
    for ci in range(tm // CHUNK):
        rows = slice(ci * CHUNK, (ci + 1) * CHUNK)
        v = v_ref[0, rows, :]
        mu = jnp.mean(v, axis=-1, keepdims=True)
        cen = v - mu
        var = jnp.mean(cen * cen, axis=-1, keepdims=True)
        vn = (cen * lax.rsqrt(var + LN_EPS) * lng_ref[...] + lnb_ref[...]).astype(BF16)
        for h in range(C_HEADS):
            vs_ref[h * CHUNK:(h + 1) * CHUNK, :] = jnp.where(head_of_lane == h, vn, jnp.zeros_like(vn))
        mixed = _bdot(wcat_ref[...], vs_ref[...]) + bs_ref[...]
        gt_ref[rows, :] = (u_ref[0, rows, :] * mixed).astype(BF16)
    y = (_bdot(gt_ref[...], wout_ref[0:C_WIDTH, :].astype(BF16))
         + _bdot(fo_ref[0], wout_ref[C_WIDTH:, :].astype(BF16)))
    o_ref[0] = x_ref[0] + mod_ref[0, 2:3, :] * _rms(y, g_ref[1:2, :])


def _gate_out(u, v, fo, x, ln_g, ln_b, w_s, b_s, w_out, g4, mods):
    bsz, length, d = x.shape
    tm = min(2 * ROW_TILE, length)
    head_dim = C_WIDTH // C_HEADS
    wcat = w_s.transpose(1, 0, 2).reshape(CHUNK, C_HEADS * CHUNK).astype(BF16)
    bs_full = jnp.repeat(b_s.T, head_dim, axis=1)
    const = lambda b, i: (0, 0)
    return pl.pallas_call(
        functools.partial(_gate_out_kernel, tm=tm),
        grid=(bsz, length // tm),
        in_specs=[
            pl.BlockSpec((1, tm, C_WIDTH), lambda b, i: (b, i, 0)),
            pl.BlockSpec((1, tm, C_WIDTH), lambda b, i: (b, i, 0)),
            pl.BlockSpec((1, tm, D_WIDTH), lambda b, i: (b, i, 0)),
            pl.BlockSpec((1, tm, d), lambda b, i: (b, i, 0)),
            pl.BlockSpec((1, C_WIDTH), const),
            pl.BlockSpec((1, C_WIDTH), const),
            pl.BlockSpec((CHUNK, C_HEADS * CHUNK), const),
            pl.BlockSpec((CHUNK, C_WIDTH), const),
            pl.BlockSpec((C_WIDTH + D_WIDTH, d), const),
            pl.BlockSpec((4, d), const),
            pl.BlockSpec((1, 6, d), lambda b, i: (b, 0, 0)),
        ],
        out_specs=pl.BlockSpec((1, tm, d), lambda b, i: (b, i, 0)),
        out_shape=jax.ShapeDtypeStruct((bsz, length, d), F32),
        scratch_shapes=[
            pltpu.VMEM((C_HEADS * CHUNK, C_WIDTH), BF16),
            pltpu.VMEM((tm, C_WIDTH), BF16),
        ],
        compiler_params=_params(("parallel", "parallel")),
        name="gating_outproj",
    )(u, v, fo, x, ln_g.reshape(1, -1), ln_b.reshape(1, -1), wcat, bs_full, w_out, g4, mods)


def _router_kernel(x_ref, mod_ref, g_ref, wr_ref, tri_ref, xn_ref, meta_ref, metat_ref, cnt_ref, carry_ref, *, tm):
    first = jnp.logical_and(pl.program_id(0) == 0, pl.program_id(1) == 0)

    @pl.when(first)
    def _():
        carry_ref[...] = jnp.zeros_like(carry_ref)

    x = x_ref[0]
    h = _rms(x, g_ref[2:3, :]) * (1.0 + mod_ref[0, 4:5, :]) + mod_ref[0, 3:4, :]
    xn_ref[0] = h
    h_hi = h.astype(BF16)
    h_lo = (h - h_hi.astype(F32)).astype(BF16)
    logits = _bdot(jnp.concatenate([h_hi, h_hi, h_lo], axis=1), wr_ref[...])
    lane = lax.broadcasted_iota(jnp.int32, (tm, LANES), 1).astype(F32)
    neg = jnp.float32(-jnp.inf)
    lm = jnp.where(lane < N_EXPERTS, logits, neg)
    m1 = jnp.max(lm, axis=-1, keepdims=True)
    i1 = jnp.min(jnp.where(lm == m1, lane, float(LANES)), axis=-1, keepdims=True)
    lm2 = jnp.where(lane == i1, neg, lm)
    m2 = jnp.max(lm2, axis=-1, keepdims=True)
    i2 = jnp.min(jnp.where(lm2 == m2, lane, float(LANES)), axis=-1, keepdims=True)
    e = jnp.exp(m2 - m1)
    p1 = 1.0 / (1.0 + e)
    p2 = e / (1.0 + e)
    hot1 = (lane == i1).astype(F32)
    hot2 = (lane == i2).astype(F32)
    both = hot1 + hot2
    before = _bdot(tri_ref[...], both.astype(BF16)) + carry_ref[0:1, :]
    r1 = jnp.sum(before * hot1, axis=-1, keepdims=True)
    r2 = jnp.sum(before * hot2, axis=-1, keepdims=True)
    carry_ref[0:1, :] = carry_ref[0:1, :] + jnp.sum(both, axis=0, keepdims=True)
    cnt_ref[...] = carry_ref[...]
    vals = (i1, i2, p1, p2, r1, r2)
    meta = jnp.zeros((tm, LANES), F32)
    for k, val in enumerate(vals):
        meta = jnp.where(lane == k, val, meta)
    meta_ref[0] = meta
    metat_ref[...] = meta.T[0:8, :]


def _router(x, mods, g4, w_router):
    bsz, length, d = x.shape
    tm = min(ROW_TILE, length)
    wr = jnp.concatenate([w_router, jnp.zeros((d, LANES - N_EXPERTS), F32)], axis=1)
    wr_hi = wr.astype(BF16)
    wr_lo = (wr - wr_hi.astype(F32)).astype(BF16)
    wr = jnp.concatenate([wr_hi, wr_lo, wr_hi], axis=0)
    tri = jnp.asarray(np.tril(np.ones((tm, tm), np.float32), -1), dtype=BF16)
    const = lambda b, i: (0, 0)
    return pl.pallas_call(
        functools.partial(_router_kernel, tm=tm),
        grid=(bsz, length // tm),
        in_specs=[
            pl.BlockSpec((1, tm, d), lambda b, i: (b, i, 0)),
            pl.BlockSpec((1, 6, d), lambda b, i: (b, 0, 0)),
            pl.BlockSpec((4, d), const),
            pl.BlockSpec((3 * d, LANES), const),
            pl.BlockSpec((tm, tm), const),
        ],
        out_specs=[
            pl.BlockSpec((1, tm, d), lambda b, i: (b, i, 0)),
            pl.BlockSpec((1, tm, LANES), lambda b, i: (b, i, 0)),
            pl.BlockSpec((8, tm), lambda b, i: (0, b * (length // tm) + i)),
            pl.BlockSpec((8, LANES), const),
        ],
        out_shape=[
            jax.ShapeDtypeStruct((bsz, length, d), F32),
            jax.ShapeDtypeStruct((bsz, length, LANES), F32),
            jax.ShapeDtypeStruct((8, bsz * length), F32),
            jax.ShapeDtypeStruct((8, LANES), F32),
        ],
        scratch_shapes=[pltpu.VMEM((8, LANES), F32)],
        compiler_params=_params(("arbitrary", "arbitrary")),
        name="moe_router",
    )(x, mods, g4, wr, tri)


def _expert_kernel(te_ref, nv_ref, x_ref, w1_ref, w3_ref, w2_ref, o_ref, acc_ref):
    i = pl.program_id(0)
    j = pl.program_id(1)

    @pl.when(i < nv_ref[0])
    def _():
        y = _swiglu_halves(x_ref[...].astype(BF16), w1_ref, w3_ref, w2_ref, lead=(0,))

        @pl.when(j == 0)
        def _():
            acc_ref[...] = y

        @pl.when(j > 0)
        def _():
            acc_ref[...] += y

        @pl.when(j == pl.num_programs(1) - 1)
        def _():
            o_ref[...] = acc_ref[...]


def _experts(x_sorted, tile_expert, n_valid, w1, w3, w2):
    rows = x_sorted.shape[0]
    d = w1.shape[1]
    dff = w1.shape[2]
    tm = EXPERT_ROW_TILE
    n_ff = 2
    tf = dff // n_ff

    def row_map(i, j, te, nv):
        return (jnp.maximum(jnp.minimum(i, nv[0] - 1), 0), 0)

    def w13_map(i, j, te, nv):
        return (te[jnp.maximum(jnp.minimum(i, nv[0] - 1), 0)], 0, jnp.where(i < nv[0], j, n_ff - 1))

    def w2_map(i, j, te, nv):
        return (te[jnp.maximum(jnp.minimum(i, nv[0] - 1), 0)], jnp.where(i < nv[0], j, n_ff - 1), 0)

    return pl.pallas_call(
        _expert_kernel,
        grid_spec=pltpu.PrefetchScalarGridSpec(
            num_scalar_prefetch=2,
            grid=(rows // tm, n_ff),
            in_specs=[
                pl.BlockSpec((tm, d), row_map),
                pl.BlockSpec((1, d, tf), w13_map),
                pl.BlockSpec((1, d, tf), w13_map),
                pl.BlockSpec((1, tf, d), w2_map),
            ],
            out_specs=pl.BlockSpec((tm, d), row_map),
            scratch_shapes=[pltpu.VMEM((tm, d), F32)],
        ),
        out_shape=jax.ShapeDtypeStruct((rows, d), F32),
        compiler_params=_params(("arbitrary", "arbitrary")),
        name="expert_swiglu",
    )(tile_expert, n_valid, x_sorted, w1, w3, w2)


def _combine_kernel(x_ref, y_ref, meta_ref, g_ref, mod_ref, o_ref):
    meta = meta_ref[0]
    y = meta[:, 2:3] * y_ref[0, 0] + meta[:, 3:4] * y_ref[1, 0]
    o_ref[0] = x_ref[0] + mod_ref[0, 5:6, :] * _rms(y, g_ref[3:4, :])


def _combine(x, y2, meta, g4, mods):
    bsz, length, d = x.shape
    tm = min(ROW_TILE, length)
    return pl.pallas_call(
        _combine_kernel,
        grid=(bsz, length // tm),
        in_specs=[
            pl.BlockSpec((1, tm, d), lambda b, i: (b, i, 0)),
            pl.BlockSpec((2, 1, tm, d), lambda b, i: (0, b, i, 0)),
            pl.BlockSpec((1, tm, LANES), lambda b, i: (b, i, 0)),
            pl.BlockSpec((4, d), lambda b, i: (0, 0)),
            pl.BlockSpec((1, 6, d), lambda b, i: (b, 0, 0)),
        ],
        out_specs=pl.BlockSpec((1, tm, d), lambda b, i: (b, i, 0)),
        out_shape=jax.ShapeDtypeStruct((bsz, length, d), F32),
        compiler_params=_params(("parallel", "parallel")),
        name="moe_combine",
    )(x, y2, meta, g4, mods)


SC_CORES = 2
SC_SUBCORES = 16
SC_WORKERS = SC_CORES * SC_SUBCORES
SC_ROW_CHUNK = 32


def _sc_mesh():
    return plsc.VectorSubcoreMesh(core_axis_name="c", subcore_axis_name="s")


def _sc_dispatch(rows_in, pos2, n_out):
    tokens, width = rows_in.shape
    per_worker = tokens // SC_WORKERS
    chunk = SC_ROW_CHUNK
    n_pairs = per_worker // (2 * chunk)
    idx = pos2.reshape(2, SC_WORKERS, 2 * n_pairs, chunk)

    @functools.partial(
        pl.kernel, mesh=_sc_mesh(),
        out_type=jax.ShapeDtypeStruct((n_out, width), rows_in.dtype),
        scratch_types=[
            pltpu.VMEM((2 * n_pairs, chunk), jnp.int32),
            pltpu.VMEM((2 * n_pairs, chunk), jnp.int32),
            pltpu.VMEM((2, chunk, width), rows_in.dtype),
        ] + [pltpu.SemaphoreType.DMA] * 6,
        name="sc_dispatch",
    )
    def body(rows_hbm, idx_hbm, out_hbm, idx0_v, idx1_v, rows_v, rsem0, rsem1, wsem0, wsem1, wsem2, wsem3):
        wid = lax.axis_index("s") * SC_CORES + lax.axis_index("c")
        base = wid * per_worker
        rsem = (rsem0, rsem1)
        wsem = ((wsem0, wsem1), (wsem2, wsem3))
        pltpu.sync_copy(idx_hbm.at[0, wid], idx0_v)
        pltpu.sync_copy(idx_hbm.at[1, wid], idx1_v)

        @pl.loop(0, n_pairs)
        def _(p):
            c0 = 2 * p
            reads = [pltpu.async_copy(rows_hbm.at[pl.ds(base + (c0 + s) * chunk, chunk)], rows_v.at[s], rsem[s])
                     for s in range(2)]
            writes = []
            for s in range(2):
                reads[s].wait()
                writes.append(pltpu.async_copy(rows_v.at[s], out_hbm.at[idx0_v.at[c0 + s]], wsem[s][0]))
                writes.append(pltpu.async_copy(rows_v.at[s], out_hbm.at[idx1_v.at[c0 + s]], wsem[s][1]))
            for w in writes:
                w.wait()

    return body(rows_in, idx)


def _sc_gather(table, idx):
    n = idx.shape[0]
    width = table.shape[1]
    chunk = SC_ROW_CHUNK
    per_worker = n // SC_WORKERS
    n_pairs = per_worker // (2 * chunk)
    idx3 = idx.reshape(SC_WORKERS, 2 * n_pairs, chunk)

    @functools.partial(
        pl.kernel, mesh=_sc_mesh(),
        out_type=jax.ShapeDtypeStruct((n, width), table.dtype),
        scratch_types=[
            pltpu.VMEM((2 * n_pairs, chunk), jnp.int32),
            pltpu.VMEM((2, chunk, width), table.dtype),
            pltpu.SemaphoreType.DMA, pltpu.SemaphoreType.DMA,
            pltpu.SemaphoreType.DMA, pltpu.SemaphoreType.DMA,
        ],
        name="sc_gather",
    )
    def body(table_hbm, idx_hbm, out_hbm, idx_v, rows_v, gsem0, gsem1, wsem0, wsem1):
        wid = lax.axis_index("s") * SC_CORES + lax.axis_index("c")
        base = wid * per_worker
        gsem = (gsem0, gsem1)
        wsem = (wsem0, wsem1)
        pltpu.sync_copy(idx_hbm.at[wid], idx_v)

        @pl.loop(0, n_pairs)
        def _(p):
            c0 = 2 * p
            gathers = [pltpu.async_copy(table_hbm.at[idx_v.at[c0 + s]], rows_v.at[s], gsem[s]) for s in range(2)]
            writes = []
            for s in range(2):
                gathers[s].wait()
                rows_out = out_hbm.at[pl.ds(base + (c0 + s) * chunk, chunk)]
                writes.append(pltpu.async_copy(rows_v.at[s], rows_out, wsem[s]))
            for s in range(2):
                writes[s].wait()

    return body(table, idx3)


def _moe(x, mods, g4, w_router, w1, w3, w2):
    bsz, length, d = x.shape
    tokens = bsz * length
    tm = EXPERT_ROW_TILE
    xn, meta, meta_t, counts = _router(x, mods, g4, w_router)
    experts = meta_t[0:2].astype(jnp.int32)
    ranks = meta_t[4:6].astype(jnp.int32)
    cnt = counts[0, 0:N_EXPERTS].astype(jnp.int32)
    padded = ((cnt + tm - 1) // tm) * tm
    ends = jnp.cumsum(padded)
    starts = ends - padded
    group_start = jnp.zeros_like(experts)
    for e in range(N_EXPERTS):
        group_start = jnp.where(experts == e, starts[e], group_start)
    pos = (group_start + ranks).reshape(2 * tokens)
    rows = 2 * tokens + N_EXPERTS * tm
    n_tiles = rows // tm
    n_valid = (ends[-1] // tm).astype(jnp.int32).reshape(1)
    tile_start = jnp.arange(n_tiles, dtype=jnp.int32) * tm
    tile_expert = jnp.minimum(
        jnp.sum((tile_start[:, None] >= ends[None, :]).astype(jnp.int32), axis=1), N_EXPERTS - 1)
    x_sorted = _sc_dispatch(xn.reshape(tokens, d), pos.reshape(2, tokens), rows)
    y_sorted = _experts(x_sorted, tile_expert, n_valid, w1, w3, w2)
    y2 = _sc_gather(y_sorted, pos).reshape(2, bsz, length, d)
    return _combine(x, y2, meta, g4, mods)


def _grid_pos_embed(rows, d):
    row = np.repeat(np.arange(rows, dtype=np.float64), GRID_W)
    col = np.tile(np.arange(GRID_W, dtype=np.float64), rows)
    n_freq = d // 4
    omega = POS_THETA ** (-np.arange(n_freq, dtype=np.float64) / n_freq)
    ang_r = row[:, None] * omega
    ang_c = col[:, None] * omega
    return jnp.asarray(np.concatenate([np.sin(ang_r), np.cos(ang_r), np.sin(ang_c), np.cos(ang_c)], axis=-1),
                       dtype=F32)


def kernel(x, c, ctx, c_ctx, ada_w, ada_b, norm_g, ab_w_in, rg_conv_w, rg_conv_b, rg_w_r, rg_b_r, rg_w_i, rg_b_i, rg_lambda, cv_dw_w, cv_dw_b, cv_ln_g, cv_ln_b, ab_w_out, ffn_w1, ffn_w3, ffn_w2, cd_w_in, sg_ln_g, sg_ln_b, sg_w_s, sg_b_s, cd_w_out, moe_router, moe_w1, moe_w3, moe_w2):
    bsz, length, d = x.shape
    assert ada_w.shape[0] == 2, "two layers: one even (RG-LRU | Conformer), one odd (gMLP | Fourier)"
    pos = _grid_pos_embed(length // GRID_W, d)

    cpad = jnp.concatenate([c, c_ctx[None, :], jnp.zeros((16 - bsz - 1, d), F32)], axis=0)
    m = _ada(cpad, ada_w, ada_b)
    mods = m[:, :bsz].reshape(2, bsz, 6, d)
    mods_ctx = m[:, bsz:bsz + 1].reshape(2, 1, 6, d)

    mods_t = mods[0].transpose(1, 0, 2)
    mods_ctx_t = jnp.broadcast_to(mods_ctx[0].reshape(6, 1, d), (6, bsz, d))
    pos3 = pos[:, None, :]
    w_in = ab_w_in[0]
    xa3, gg3, glu3, x0_t, ffn1b, ffn3b, ffn2b = _inproj_t(
        x, pos3, mods_t, norm_g[0], w_in, rg_conv_w[0], rg_conv_b[0], True, riders=(ffn_w1[0], ffn_w3[0], ffn_w2[0]))
    (xc3,) = _inproj_t(ctx, None, mods_ctx_t, norm_g[0], w_in[:, 0:A_WIDTH], rg_conv_w[0], rg_conv_b[0], False)
    wg, bg, sp = _gate_weights(rg_w_r[0], rg_b_r[0], rg_w_i[0], rg_b_i[0], rg_lambda[0])
    n_exp, _, dff_e = moe_w1[0].shape
    rec_hi, rec_lo, w2b = _scan_t(xa3, xc3, gg3, wg, bg, sp, riders=(moe_w2[0].reshape(n_exp * dff_e, d),))
    (u3,) = _conv_t(glu3, cv_dw_w[0], cv_dw_b[0], cv_ln_g[0], cv_ln_b[0])
    x1_t = _outproj_t(rec_lo, rec_hi, u3, x0_t, ab_w_out[0], norm_g[0], mods_t)
    x2, w1b, w3b = _ffn_t(x1_t, mods_t, norm_g[0], ffn1b, ffn3b, ffn2b,
                          riders=(moe_w1[0].reshape(n_exp * d, dff_e), moe_w3[0].reshape(n_exp * d, dff_e)))

    u, v, f = _inproj_cd(x2, mods[1], norm_g[1], cd_w_in[0])
    fo = _fourier(f)
    x3 = _gate_out(u, v, fo, x2, sg_ln_g[0], sg_ln_b[0], sg_w_s[0], sg_b_s[0], cd_w_out[0],
                   norm_g[1], mods[1])
    return _moe(x3, mods[1], norm_g[1], moe_router[0], w1b.reshape(n_exp, d, dff_e), w3b.reshape(n_exp, d, dff_e),
                w2b.reshape(n_exp, dff_e, d))
```

```python
import functools

import numpy as np
import jax
import jax.numpy as jnp
from jax import lax
from jax.experimental import pallas as pl
from jax.experimental.pallas import tpu as pltpu
from jax.experimental.pallas import tpu_sc as plsc

F32 = jnp.float32
BF16 = jnp.bfloat16

GRID_W = 64
POS_THETA = 10000.0
NORM_EPS = 1e-6
LN_EPS = 1e-5
LRU_C = 8.0

A_WIDTH = 512
A_HEADS = 8
A_CONV_W = 4
B_WIDTH = 512
B_CONV_W = 31
C_WIDTH = 512
C_HEADS = 8
CHUNK = 128
D_WIDTH = 512
D_GROUPS = 4
D_GROUP_DIM = D_WIDTH // D_GROUPS
N_EXPERTS = 8

LANES = 128
MXU_WIDTH = 256
ROW_TILE = 512
EXPERT_ROW_TILE = 512
VMEM_LIMIT = 56 * 2 ** 20


def _params(sem):
    return pltpu.CompilerParams(dimension_semantics=sem, vmem_limit_bytes=VMEM_LIMIT)


def _rms(x, g):
    return x * lax.rsqrt(jnp.mean(x * x, axis=-1, keepdims=True) + NORM_EPS) * g


def _bdot(a, b):
    return jnp.dot(a, b, preferred_element_type=F32)


def _swiglu_halves(x, w1_ref, w3_ref, w2_ref, lead=()):
    n = w1_ref.shape[-1]
    half = -(-(n // 2) // MXU_WIDTH) * MXU_WIDTH
    y = None
    for lo, hi in ((0, half), (half, n)):
        a = _bdot(x, w1_ref[lead + (slice(None), slice(lo, hi))])
        b = _bdot(x, w3_ref[lead + (slice(None), slice(lo, hi))])
        t = (jax.nn.silu(a) * b).astype(BF16)
        part = _bdot(t, w2_ref[lead + (slice(lo, hi), slice(None))])
        y = part if y is None else y + part
    return y


def _ada_kernel(c_ref, w_ref, b_ref, o_ref):
    s = jax.nn.silu(c_ref[...])
    o_ref[0] = _bdot(s.astype(BF16), w_ref[0].astype(BF16)) + b_ref[0]


def _ada(cpad, ada_w, ada_b):
    nl, d, n6 = ada_w.shape
    rows = cpad.shape[0]
    tn = n6 // 4
    return pl.pallas_call(
        _ada_kernel,
        grid=(nl, n6 // tn),
        in_specs=[
            pl.BlockSpec((rows, d), lambda l, j: (0, 0)),
            pl.BlockSpec((1, d, tn), lambda l, j: (l, 0, j)),
            pl.BlockSpec((1, 1, tn), lambda l, j: (l, 0, j)),
        ],
        out_specs=pl.BlockSpec((1, rows, tn), lambda l, j: (l, 0, j)),
        out_shape=jax.ShapeDtypeStruct((nl, rows, n6), F32),
        compiler_params=_params(("arbitrary", "arbitrary")),
        name="ada_terms",
    )(cpad, ada_w, ada_b.reshape(nl, 1, n6))


TIME_TILE = 128
SCAN_TIME_TILE = 128
SCAN_LANES = 256
CONV_TIME_TILE = 128
CONV_HALO = 16


def _time_major(ref):
    return jnp.swapaxes(ref[...], 0, 1)


def _cast_riders(cast_in, cast_out):
    for src, dst in zip(cast_in, cast_out):
        dst[...] = src[...].astype(BF16)


def _rider_specs(arrays, n_steps, step_of):
    in_specs, out_specs, out_shapes = [], [], []
    for arr in arrays:
        rows, cols = arr.shape
        block = (rows // n_steps, cols)
        in_specs.append(pl.BlockSpec(block, lambda *idx: (step_of(*idx), 0)))
        out_specs.append(pl.BlockSpec(block, lambda *idx: (step_of(*idx), 0)))
        out_shapes.append(jax.ShapeDtypeStruct(arr.shape, BF16))
    return in_specs, out_specs, out_shapes


def _inproj_t_kernel(*refs, tt, add_pos, branches, n_cast):
    if n_cast:
        n_in = len(refs) - n_cast - (4 if branches else 1) - n_cast
        _cast_riders(refs[n_in:n_in + n_cast], refs[len(refs) - n_cast:])
        refs = refs[:n_in] + refs[n_in + n_cast:len(refs) - n_cast]
    if add_pos:
        x_ref, xp_ref, xn_ref, pos_ref, posp_ref, posn_ref, mod_ref, g_ref, w_ref, cw_ref, cb_ref, *outs = refs
        xall = jnp.concatenate(
            [_time_major(xp_ref)[7:8] + posp_ref[...], _time_major(x_ref) + pos_ref[...],
             _time_major(xn_ref)[0:2] + posn_ref[...]], axis=0)
    else:
        x_ref, xp_ref, xn_ref, mod_ref, g_ref, w_ref, cw_ref, cb_ref, *outs = refs
        xall = jnp.concatenate([_time_major(xp_ref)[7:8], _time_major(x_ref), _time_major(xn_ref)[0:2]], axis=0)
    i = pl.program_id(0)
    last = pl.num_programs(0) - 1
    steps, bsz, d = xall.shape
    h = _rms(xall, g_ref[0:1, :]) * (1.0 + mod_ref[1]) + mod_ref[0]
    z = _bdot(h.reshape(steps * bsz, d).astype(BF16), w_ref[...].astype(BF16))
    z3 = z.reshape(steps, bsz, z.shape[-1])
    t_idx = lax.broadcasted_iota(jnp.int32, (steps, bsz, A_WIDTH), 0)
    inside = jnp.logical_and(jnp.logical_or(t_idx >= 1, i > 0), jnp.logical_or(t_idx <= tt, i < last))
    xa = jnp.where(inside, z3[:, :, 0:A_WIDTH], 0.0)
    xconv = cb_ref[...] + cw_ref[0:1, :] * xa[0:tt]
    for k in range(1, A_CONV_W):
        xconv = xconv + cw_ref[k:k + 1, :] * xa[k:k + tt]
    outs[0][...] = xconv
    if branches:
        outs[1][...] = jax.nn.gelu(z3[1:tt + 1, :, A_WIDTH:2 * A_WIDTH])
        vb = z3[1:tt + 1, :, 2 * A_WIDTH:2 * A_WIDTH + B_WIDTH]
        gb = z3[1:tt + 1, :, 2 * A_WIDTH + B_WIDTH:]
        outs[2][...] = vb * jax.nn.sigmoid(gb)
        outs[3][...] = xall[1:tt + 1]


def _inproj_t(x, pos3, mods_t, g4, w, conv_w, conv_b, branches, riders=()):
    bsz, length, d = x.shape
    n = w.shape[1]
    tt = min(TIME_TILE, length)
    add_pos = pos3 is not None
    n_out = 3 if branches else 1
    prev_map = lambda i: (jnp.maximum(i * tt - 1, 0), 0, 0)
    next_map = lambda i: (jnp.minimum((i + 1) * (tt // 2), length // 2 - 1), 0, 0)
    in_specs = [
        pl.BlockSpec((bsz, tt, d), lambda i: (0, i, 0)),
        pl.BlockSpec((bsz, 8, d), lambda i: (0, jnp.maximum(i * (tt // 8) - 1, 0), 0)),
        pl.BlockSpec((bsz, 8, d), lambda i: (0, jnp.minimum((i + 1) * (tt // 8), length // 8 - 1), 0)),
    ]
    args = [x, x, x]
    if add_pos:
        in_specs += [
            pl.BlockSpec((tt, 1, d), lambda i: (i, 0, 0)),
            pl.BlockSpec((1, 1, d), prev_map),
            pl.BlockSpec((2, 1, d), next_map),
        ]
        args += [pos3, pos3, pos3]
    in_specs += [
        pl.BlockSpec((6, bsz, d), lambda i: (0, 0, 0)),
        pl.BlockSpec((4, d), lambda i: (0, 0)),
        pl.BlockSpec((d, n), lambda i: (0, 0)),
        pl.BlockSpec((A_CONV_W, A_WIDTH), lambda i: (0, 0)),
        pl.BlockSpec((1, A_WIDTH), lambda i: (0, 0)),
    ]
    args += [mods_t, g4, w, conv_w, conv_b.reshape(1, A_WIDTH)]
    r_in, r_out, r_shapes = _rider_specs(riders, length // tt, lambda i: i)
    out_specs = [pl.BlockSpec((tt, bsz, A_WIDTH), lambda i: (i, 0, 0))] * n_out
    out_shapes = [jax.ShapeDtypeStruct((length, bsz, A_WIDTH), F32)] * n_out
    if branches:
        out_specs.append(pl.BlockSpec((tt, bsz, d), lambda i: (i, 0, 0)))
        out_shapes.append(jax.ShapeDtypeStruct((length, bsz, d), F32))
    return pl.pallas_call(
        functools.partial(_inproj_t_kernel, tt=tt, add_pos=add_pos, branches=branches, n_cast=len(riders)),
        grid=(length // tt,),
        in_specs=in_specs + r_in,
        out_specs=out_specs + r_out,
        out_shape=out_shapes + r_shapes,
        compiler_params=_params(("parallel",)),
        name="inproj_ab" if branches else "inproj_ctx",
    )(*args, *riders)


def _scan_t_kernel(*refs, tb, nb, ctx_len, n_cast):
    xf_ref, xb_ref, ggf_ref, ggb_ref, xc_ref, wg_ref, bg_ref, sp_ref = refs[:8]
    cast_in = refs[8:8 + n_cast]
    hi_ref, lo_ref = refs[8 + n_cast:10 + n_cast]
    cast_out = refs[10 + n_cast:10 + 2 * n_cast]
    a_ref, b_ref, h_ref, state_ref = refs[10 + 2 * n_cast:]
    _cast_riders(cast_in, cast_out)
    k = pl.program_id(1)
    half = nb // 2
    bsz, lanes = xf_ref.shape[1], xf_ref.shape[2]
    chunk = 128

    def coefficients(x_ref, n, direction):
        def body(c, _):
            t0 = pl.multiple_of(c * chunk, chunk)
            x = x_ref[pl.ds(t0, chunk)].reshape(chunk * bsz, lanes)
            th = jnp.tanh(_bdot(x.astype(BF16), wg_ref[0, direction]) + bg_ref[0, direction])
            i = 0.5 * th[:, lanes:] + 0.5
            c = (-0.5 * LRU_C) * sp_ref[0, direction:direction + 1, :]
            log_a = c * th[:, 0:lanes] + c
            a = jnp.exp(log_a)
            one_minus_a2 = -jnp.tanh(log_a) * (a * a + 1.0)
            a_ref[direction, pl.ds(t0, chunk)] = a.reshape(chunk, bsz, lanes)
            b_ref[direction, pl.ds(t0, chunk)] = (jnp.sqrt(one_minus_a2) * (i * x)).reshape(chunk, bsz, lanes)
            return 0
        lax.fori_loop(0, n // chunk, body, 0)

    def sweep(n, store):
        def body(t, carry):
            hf, hb = carry
            tr = n - 1 - t
            hf = a_ref[0, pl.ds(t, 1)][0] * hf + b_ref[0, pl.ds(t, 1)][0]
            hb = a_ref[1, pl.ds(tr, 1)][0] * hb + b_ref[1, pl.ds(tr, 1)][0]
            if store:
                b_ref[0, pl.ds(t, 1)] = hf[None]
                b_ref[1, pl.ds(tr, 1)] = hb[None]
            return hf, hb
        hf, hb = lax.fori_loop(0, n, body, (state_ref[0], state_ref[1]), unroll=8)
        state_ref[0] = hf
        state_ref[1] = hb

    @pl.when(k == 0)
    def _():
        state_ref[...] = jnp.zeros_like(state_ref)
        coefficients(xc_ref, ctx_len, 0)
        coefficients(xc_ref, ctx_len, 1)
        sweep(ctx_len, False)

    coefficients(xf_ref, tb, 0)
    coefficients(xb_ref, tb, 1)
    sweep(tb, True)
    m = nb - 1 - k

    @pl.when(k < half)
    def _():
        h_ref[k] = b_ref[0, 0:tb]
        h_ref[m] = b_ref[1, 0:tb]

    @pl.when(k >= half)
    def _():
        hi_ref[...] = (b_ref[0, 0:tb] + h_ref[k]) * ggf_ref[...]
        lo_ref[...] = (h_ref[m] + b_ref[1, 0:tb]) * ggb_ref[...]


def _scan_t(xa3, xc3, gg3, wg, bg, sp, riders=()):
    length, bsz, width = xa3.shape
    ctx_len = xc3.shape[0]
    tb = min(SCAN_TIME_TILE, length // 2)
    nb = length // tb
    half = nb // 2
    lanes = SCAN_LANES
    blk = (tb, bsz, lanes)
    r_in, r_out, r_shapes = _rider_specs(riders, (width // lanes) * nb, lambda g, k: g * nb + k)
    return pl.pallas_call(
        functools.partial(_scan_t_kernel, tb=tb, nb=nb, ctx_len=ctx_len, n_cast=len(riders)),
        grid=(width // lanes, nb),
        in_specs=[
            pl.BlockSpec(blk, lambda g, k: (k, 0, g)),
            pl.BlockSpec(blk, lambda g, k: (nb - 1 - k, 0, g)),
            pl.BlockSpec(blk, lambda g, k: (jnp.maximum(k, half), 0, g)),
            pl.BlockSpec(blk, lambda g, k: (jnp.minimum(nb - 1 - k, half - 1), 0, g)),
            pl.BlockSpec((ctx_len, bsz, lanes), lambda g, k: (0, 0, g)),
            pl.BlockSpec((1, 2, lanes, 2 * lanes), lambda g, k: (g, 0, 0, 0)),
            pl.BlockSpec((1, 2, 1, 2 * lanes), lambda g, k: (g, 0, 0, 0)),
            pl.BlockSpec((1, 2, lanes), lambda g, k: (g, 0, 0)),
        ] + r_in,
        out_specs=[
            pl.BlockSpec(blk, lambda g, k: (jnp.maximum(k - half, 0), 0, g)),
            pl.BlockSpec(blk, lambda g, k: (jnp.minimum(nb - 1 - k, half - 1), 0, g)),
        ] + r_out,
        out_shape=[jax.ShapeDtypeStruct((length // 2, bsz, width), F32)] * 2 + r_shapes,
        scratch_shapes=[
            pltpu.VMEM((2, max(tb, ctx_len), bsz, lanes), F32),
            pltpu.VMEM((2, max(tb, ctx_len), bsz, lanes), F32),
            pltpu.VMEM((nb, tb, bsz, lanes), F32),
            pltpu.VMEM((2, bsz, lanes), F32),
        ],
        compiler_params=_params(("parallel", "arbitrary")),
        name="rglru_scan",
    )(xa3, xa3, gg3, gg3, xc3, wg, bg, sp, *riders)


def _gate_weights(w_r, b_r, w_i, b_i, lam):
    hd = A_WIDTH // A_HEADS
    heads_per_group = SCAN_LANES // hd
    ngroups = A_WIDTH // SCAN_LANES

    def blockdiag(w):
        w = w.reshape(ngroups, heads_per_group, hd, hd)
        eye = jnp.eye(heads_per_group, dtype=w.dtype)
        return jnp.einsum("ghde,hk->ghdke", w, eye).reshape(ngroups, SCAN_LANES, SCAN_LANES)

    wg = jnp.stack([jnp.concatenate([blockdiag(w_r[d]), blockdiag(w_i[d])], axis=-1) for d in range(2)], axis=1)
    bg = jnp.stack([jnp.concatenate([b_r[d].reshape(ngroups, 1, SCAN_LANES), b_i[d].reshape(ngroups, 1, SCAN_LANES)],
                                    axis=-1) for d in range(2)], axis=1)
    sp = jax.nn.softplus(-lam.astype(F32)).reshape(2, ngroups, SCAN_LANES).transpose(1, 0, 2)
    return (0.5 * wg).astype(BF16), 0.5 * bg, sp


def _conv_t_kernel(*refs, tb, n_cast):
    x_ref, xp_ref, xn_ref, w_ref, b_ref, lng_ref, lnb_ref = refs[:7]
    cast_in = refs[7:7 + n_cast]
    o_ref = refs[7 + n_cast]
    cast_out = refs[8 + n_cast:8 + 2 * n_cast]
    stage_ref = refs[-1]
    _cast_riders(cast_in, cast_out)
    i = pl.program_id(0)
    last = pl.num_programs(0) - 1
    halo = CONV_HALO
    stage_ref[0:halo] = jnp.where(i > 0, xp_ref[...], 0.0)
    stage_ref[halo:halo + tb] = x_ref[...]
    stage_ref[halo + tb:2 * halo + tb] = jnp.where(i < last, xn_ref[...], 0.0)
    sub = 8

    def body(r, _):
        t0 = pl.multiple_of(r * sub, sub)
        def tap(k, acc):
            return acc + w_ref[k] * stage_ref[pl.ds(t0 + 1 + k, sub)]
        acc = lax.fori_loop(0, B_CONV_W + 1, tap, jnp.broadcast_to(b_ref[...], (sub,) + b_ref.shape), unroll=8)
        o_ref[pl.ds(t0, sub)] = acc
        return 0
    lax.fori_loop(0, tb // sub, body, 0)

    norm_rows = 64

    def norm_body(r, _):
        t0 = pl.multiple_of(r * norm_rows, norm_rows)
        acc = o_ref[pl.ds(t0, norm_rows)]
        mu = jnp.mean(acc, axis=-1, keepdims=True)
        cen = acc - mu
        var = jnp.mean(cen * cen, axis=-1, keepdims=True)
        y = cen * lax.rsqrt(var + LN_EPS) * lng_ref[...] + lnb_ref[...]
        o_ref[pl.ds(t0, norm_rows)] = jax.nn.silu(y)
        return 0
    lax.fori_loop(0, tb // norm_rows, norm_body, 0)


def _conv_t(glu3, dw_w, dw_b, ln_g, ln_b, riders=()):
    length, bsz, width = glu3.shape
    tb = min(CONV_TIME_TILE, length)
    halo = CONV_HALO
    taps = dw_w.shape[0]
    w8 = jnp.broadcast_to(jnp.concatenate([dw_w, jnp.zeros((32 - taps, width), F32)], axis=0)[:, None, :],
                          (32, bsz, width))
    b8 = jnp.broadcast_to(dw_b[None, :], (bsz, width))
    r_in, r_out, r_shapes = _rider_specs(riders, length // tb, lambda i: i)
    return pl.pallas_call(
        functools.partial(_conv_t_kernel, tb=tb, n_cast=len(riders)),
        grid=(length // tb,),
        in_specs=[
            pl.BlockSpec((tb, bsz, width), lambda i: (i, 0, 0)),
            pl.BlockSpec((halo, bsz, width), lambda i: (jnp.maximum(i * (tb // halo) - 1, 0), 0, 0)),
            pl.BlockSpec((halo, bsz, width), lambda i: (jnp.minimum((i + 1) * (tb // halo), length // halo - 1), 0, 0)),
            pl.BlockSpec((32, bsz, width), lambda i: (0, 0, 0)),
            pl.BlockSpec((bsz, width), lambda i: (0, 0)),
            pl.BlockSpec((1, width), lambda i: (0, 0)),
            pl.BlockSpec((1, width), lambda i: (0, 0)),
        ] + r_in,
        out_specs=[pl.BlockSpec((tb, bsz, width), lambda i: (i, 0, 0))] + r_out,
        out_shape=[jax.ShapeDtypeStruct((length, bsz, width), F32)] + r_shapes,
        scratch_shapes=[pltpu.VMEM((tb + 2 * halo, bsz, width), F32)],
        compiler_params=_params(("parallel",)),
        name="conformer_conv",
    )(glu3, glu3, glu3, w8, b8, ln_g.reshape(1, -1), ln_b.reshape(1, -1), *riders)


def _outproj_t_kernel(lo_ref, hi_ref, u_ref, x_ref, wout_ref, g_ref, mod_ref, o_ref, *, half_steps):
    i = pl.program_id(0)
    tt, bsz, d = x_ref.shape
    rec = jnp.where(i < half_steps, lo_ref[...], hi_ref[...])
    rows = tt * bsz
    y = (_bdot(rec.reshape(rows, A_WIDTH).astype(BF16), wout_ref[0:A_WIDTH, :].astype(BF16))
         + _bdot(u_ref[...].reshape(rows, B_WIDTH).astype(BF16), wout_ref[A_WIDTH:, :].astype(BF16)))
    yn = _rms(y, g_ref[1:2, :]).reshape(tt, bsz, d)
    o_ref[...] = x_ref[...] + mod_ref[2] * yn


def _outproj_t(rec_lo, rec_hi, u3, x0_t, w_out, g4, mods_t):
    length, bsz, d = x0_t.shape
    tt = min(TIME_TILE, length // 2)
    half_steps = (length // 2) // tt
    return pl.pallas_call(
        functools.partial(_outproj_t_kernel, half_steps=half_steps),
        grid=(length // tt,),
        in_specs=[
            pl.BlockSpec((tt, bsz, A_WIDTH), lambda i: (jnp.minimum(i, half_steps - 1), 0, 0)),
            pl.BlockSpec((tt, bsz, A_WIDTH), lambda i: (jnp.maximum(i - half_steps, 0), 0, 0)),
            pl.BlockSpec((tt, bsz, B_WIDTH), lambda i: (i, 0, 0)),
            pl.BlockSpec((tt, bsz, d), lambda i: (i, 0, 0)),
            pl.BlockSpec((A_WIDTH + B_WIDTH, d), lambda i: (0, 0)),
            pl.BlockSpec((4, d), lambda i: (0, 0)),
            pl.BlockSpec((6, bsz, d), lambda i: (0, 0, 0)),
        ],
        out_specs=pl.BlockSpec((tt, bsz, d), lambda i: (i, 0, 0)),
        out_shape=jax.ShapeDtypeStruct((length, bsz, d), F32),
        compiler_params=_params(("parallel",)),
        name="mixer_ab_outproj",
    )(rec_lo, rec_hi, u3, x0_t, w_out, g4, mods_t)


def _ffn_t_kernel(x_ref, mod_ref, g_ref, w1_ref, w3_ref, w2_ref, *rest, n_cast):
    _cast_riders(rest[:n_cast], rest[n_cast + 1:])
    o_ref = rest[n_cast]
    x = x_ref[...]
    tt, bsz, d = x.shape
    h = (_rms(x, g_ref[2:3, :]) * (1.0 + mod_ref[4]) + mod_ref[3]).reshape(tt * bsz, d).astype(BF16)
    y = _swiglu_halves(h, w1_ref, w3_ref, w2_ref)
    o_ref[...] = jnp.swapaxes(x + mod_ref[5] * _rms(y, g_ref[3:4, :]).reshape(tt, bsz, d), 0, 1)


def _ffn_t(x3, mods_t, g4, w1, w3, w2, riders=()):
    length, bsz, d = x3.shape
    dff = w1.shape[1]
    tt = min(TIME_TILE // 2, length)
    r_in, r_out, r_shapes = _rider_specs(riders, length // tt, lambda i: i)
    return pl.pallas_call(
        functools.partial(_ffn_t_kernel, n_cast=len(riders)),
        grid=(length // tt,),
        in_specs=[
            pl.BlockSpec((tt, bsz, d), lambda i: (i, 0, 0)),
            pl.BlockSpec((6, bsz, d), lambda i: (0, 0, 0)),
            pl.BlockSpec((4, d), lambda i: (0, 0)),
            pl.BlockSpec((d, dff), lambda i: (0, 0), pipeline_mode=pl.Buffered(1)),
            pl.BlockSpec((d, dff), lambda i: (0, 0), pipeline_mode=pl.Buffered(1)),
            pl.BlockSpec((dff, d), lambda i: (0, 0), pipeline_mode=pl.Buffered(1)),
        ] + r_in,
        out_specs=[pl.BlockSpec((bsz, tt, d), lambda i: (0, i, 0))] + r_out,
        out_shape=[jax.ShapeDtypeStruct((bsz, length, d), F32)] + r_shapes,
        compiler_params=_params(("parallel",)),
        name="dense_swiglu",
    )(x3, mods_t, g4, w1, w3, w2, *riders)


def _inproj_cd_kernel(x_ref, mod_ref, g_ref, w_ref, u_ref, v_ref, f_ref):
    x = x_ref[0]
    h = _rms(x, g_ref[0:1, :]) * (1.0 + mod_ref[0, 1:2, :]) + mod_ref[0, 0:1, :]
    z = _bdot(h.astype(BF16), w_ref[...].astype(BF16))
    u_ref[0] = jax.nn.gelu(z[:, 0:C_WIDTH])
    v_ref[0] = jax.nn.gelu(z[:, C_WIDTH:2 * C_WIDTH])
    f_ref[0] = z[:, 2 * C_WIDTH:]


def _inproj_cd(x, mods, g4, w):
    bsz, length, d = x.shape
    n = w.shape[1]
    tm = min(2 * ROW_TILE, length)
    return pl.pallas_call(
        _inproj_cd_kernel,
        grid=(bsz, length // tm),
        in_specs=[
            pl.BlockSpec((1, tm, d), lambda b, i: (b, i, 0)),
            pl.BlockSpec((1, 6, d), lambda b, i: (b, 0, 0)),
            pl.BlockSpec((4, d), lambda b, i: (0, 0)),
            pl.BlockSpec((d, n), lambda b, i: (0, 0)),
        ],
        out_specs=[pl.BlockSpec((1, tm, C_WIDTH), lambda b, i: (b, i, 0))] * 3,
        out_shape=[jax.ShapeDtypeStruct((bsz, length, C_WIDTH), F32)] * 3,
        compiler_params=_params(("parallel", "parallel")),
        name="inproj_cd",
    )(x, mods, g4, w)


def _fourier_kernel(f_ref, dmat_ref, lmat_ref, o_ref, z_ref, *, length, scale):
    j = pl.program_id(1)
    rows_per_step = 512 if length % 512 == 0 else length

    @pl.when(j == 0)
    def _():
        def body(c, _):
            rows = pl.ds(pl.multiple_of(c * rows_per_step, rows_per_step), rows_per_step)
            fb = f_ref[0, rows, :].astype(BF16)
            for g in range(D_GROUPS):
                cs = _bdot(fb[:, g * D_GROUP_DIM:(g + 1) * D_GROUP_DIM], dmat_ref[...])
                z_ref[0, rows, g * D_GROUP_DIM:(g + 1) * D_GROUP_DIM] = cs[:, 0:D_GROUP_DIM].astype(BF16)
                z_ref[1, rows, g * D_GROUP_DIM:(g + 1) * D_GROUP_DIM] = cs[:, D_GROUP_DIM:].astype(BF16)
            return 0
        lax.fori_loop(0, length // rows_per_step, body, 0)

    out = _bdot(lmat_ref[0], z_ref[0]) + _bdot(lmat_ref[1], z_ref[1])
    o_ref[0] = (out * scale).astype(BF16)


def _dft_matrices(length):
    k = np.arange(length, dtype=np.int64)
    ang_l = 2.0 * np.pi * ((k[:, None] * k[None, :]) % length).astype(np.float64) / length
    lmat = np.stack([np.cos(ang_l), -np.sin(ang_l)], axis=0)
    d = np.arange(D_GROUP_DIM, dtype=np.int64)
    ang_d = 2.0 * np.pi * ((d[:, None] * d[None, :]) % D_GROUP_DIM).astype(np.float64) / D_GROUP_DIM
    dmat = np.concatenate([np.cos(ang_d), np.sin(ang_d)], axis=1)
    return jnp.asarray(lmat, dtype=F32).astype(BF16), jnp.asarray(dmat, dtype=F32).astype(BF16)


def _fourier(f):
    bsz, length, width = f.shape
    tm = min(4 * ROW_TILE, length)
    lmat, dmat = _dft_matrices(length)
    scale = float(1.0 / np.sqrt(length * D_GROUP_DIM))
    return pl.pallas_call(
        functools.partial(_fourier_kernel, length=length, scale=scale),
        grid=(bsz, length // tm),
        in_specs=[
            pl.BlockSpec((1, length, width), lambda b, i: (b, 0, 0)),
            pl.BlockSpec((D_GROUP_DIM, 2 * D_GROUP_DIM), lambda b, i: (0, 0)),
            pl.BlockSpec((2, tm, length), lambda b, i: (0, i, 0)),
        ],
        out_specs=pl.BlockSpec((1, tm, width), lambda b, i: (b, i, 0)),
        out_shape=jax.ShapeDtypeStruct((bsz, length, width), BF16),
        scratch_shapes=[pltpu.VMEM((2, length, width), BF16)],
        compiler_params=_params(("parallel", "arbitrary")),
        name="fourier_mix",
    )(f, dmat, lmat)


def _gate_out_kernel(u_ref, v_ref, fo_ref, x_ref, lng_ref, lnb_ref, wcat_ref, bs_ref, wout_ref, g_ref, mod_ref,
                     o_ref, vs_ref, gt_ref, *, tm):
    head_dim = C_WIDTH // C_HEADS
    head_of_lane = lax.broadcasted_iota(jnp.int32, (CHUNK, C_WIDTH), 1) // head_dim
    for ci in range(tm // CHUNK):
        rows = slice(ci * CHUNK, (ci + 1) * CHUNK)
        v = v_ref[0, rows, :]
        mu = jnp.mean(v, axis=-1, keepdims=True)
        cen = v - mu
        var = jnp.mean(cen * cen, axis=-1, keepdims=True)
        vn = (cen * lax.rsqrt(var + LN_EPS) * lng_ref[...] + lnb_ref[...]).astype(BF16)
        for h in range(C_HEADS):
            vs_ref[h * CHUNK:(h + 1) * CHUNK, :] = jnp.where(head_of_lane == h, vn, jnp.zeros_like(vn))
        mixed = _bdot(wcat_ref[...], vs_ref[...]) + bs_ref[...]
        gt_ref[rows, :] = (u_ref[0, rows, :] * mixed).astype(BF16)
    y = (_bdot(gt_ref[...], wout_ref[0:C_WIDTH, :].astype(BF16))
         + _bdot(fo_ref[0], wout_ref[C_WIDTH:, :].astype(BF16)))
    o_ref[0] = x_ref[0] + mod_ref[0, 2:3, :] * _rms(y, g_ref[1:2, :])


def _gate_out(u, v, fo, x, ln_g, ln_b, w_s, b_s, w_out, g4, mods):
    bsz, length, d = x.shape
    tm = min(2 * ROW_TILE, length)
    head_dim = C_WIDTH // C_HEADS
    wcat = w_s.transpose(1, 0, 2).reshape(CHUNK, C_HEADS * CHUNK).astype(BF16)
    bs_full = jnp.repeat(b_s.T, head_dim, axis=1)
    const = lambda b, i: (0, 0)
    return pl.pallas_call(
        functools.partial(_gate_out_kernel, tm=tm),
        grid=(bsz, length // tm),
        in_specs=[
            pl.BlockSpec((1, tm, C_WIDTH), lambda b, i: (b, i, 0)),
            pl.BlockSpec((1, tm, C_WIDTH), lambda b, i: (b, i, 0)),
            pl.BlockSpec((1, tm, D_WIDTH), lambda b, i: (b, i, 0)),
            pl.BlockSpec((1, tm, d), lambda b, i: (b, i, 0)),
            pl.BlockSpec((1, C_WIDTH), const),
            pl.BlockSpec((1, C_WIDTH), const),
            pl.BlockSpec((CHUNK, C_HEADS * CHUNK), const),
            pl.BlockSpec((CHUNK, C_WIDTH), const),
            pl.BlockSpec((C_WIDTH + D_WIDTH, d), const),
            pl.BlockSpec((4, d), const),
            pl.BlockSpec((1, 6, d), lambda b, i: (b, 0, 0)),
        ],
        out_specs=pl.BlockSpec((1, tm, d), lambda b, i: (b, i, 0)),
        out_shape=jax.ShapeDtypeStruct((bsz, length, d), F32),
        scratch_shapes=[
            pltpu.VMEM((C_HEADS * CHUNK, C_WIDTH), BF16),
            pltpu.VMEM((tm, C_WIDTH), BF16),
        ],
        compiler_params=_params(("parallel", "parallel")),
        name="gating_outproj",
    )(u, v, fo, x, ln_g.reshape(1, -1), ln_b.reshape(1, -1), wcat, bs_full, w_out, g4, mods)


def _router_kernel(x_ref, mod_ref, g_ref, wr_ref, tri_ref, xn_ref, meta_ref, metat_ref, cnt_ref, carry_ref, *, tm):
    first = jnp.logical_and(pl.program_id(0) == 0, pl.program_id(1) == 0)

    @pl.when(first)
    def _():
        carry_ref[...] = jnp.zeros_like(carry_ref)

    x = x_ref[0]
    h = _rms(x, g_ref[2:3, :]) * (1.0 + mod_ref[0, 4:5, :]) + mod_ref[0, 3:4, :]
    xn_ref[0] = h
    h_hi = h.astype(BF16)
    h_lo = (h - h_hi.astype(F32)).astype(BF16)
    logits = _bdot(jnp.concatenate([h_hi, h_hi, h_lo], axis=1), wr_ref[...])
    lane = lax.broadcasted_iota(jnp.int32, (tm, LANES), 1).astype(F32)
    neg = jnp.float32(-jnp.inf)
    lm = jnp.where(lane < N_EXPERTS, logits, neg)
    m1 = jnp.max(lm, axis=-1, keepdims=True)
    i1 = jnp.min(jnp.where(lm == m1, lane, float(LANES)), axis=-1, keepdims=True)
    lm2 = jnp.where(lane == i1, neg, lm)
    m2 = jnp.max(lm2, axis=-1, keepdims=True)
    i2 = jnp.min(jnp.where(lm2 == m2, lane, float(LANES)), axis=-1, keepdims=True)
    e = jnp.exp(m2 - m1)
    p1 = 1.0 / (1.0 + e)
    p2 = e / (1.0 + e)
    hot1 = (lane == i1).astype(F32)
    hot2 = (lane == i2).astype(F32)
    both = hot1 + hot2
    before = _bdot(tri_ref[...], both.astype(BF16)) + carry_ref[0:1, :]
    r1 = jnp.sum(before * hot1, axis=-1, keepdims=True)
    r2 = jnp.sum(before * hot2, axis=-1, keepdims=True)
    carry_ref[0:1, :] = carry_ref[0:1, :] + jnp.sum(both, axis=0, keepdims=True)
    cnt_ref[...] = carry_ref[...]
    vals = (i1, i2, p1, p2, r1, r2)
    meta = jnp.zeros((tm, LANES), F32)
    for k, val in enumerate(vals):
        meta = jnp.where(lane == k, val, meta)
    meta_ref[0] = meta
    metat_ref[...] = meta.T[0:8, :]


def _router(x, mods, g4, w_router):
    bsz, length, d = x.shape
    tm = min(2 * ROW_TILE, length)
    wr = jnp.concatenate([w_router, jnp.zeros((d, LANES - N_EXPERTS), F32)], axis=1)
    wr_hi = wr.astype(BF16)
    wr_lo = (wr - wr_hi.astype(F32)).astype(BF16)
    wr = jnp.concatenate([wr_hi, wr_lo, wr_hi], axis=0)
    tri = jnp.asarray(np.tril(np.ones((tm, tm), np.float32), -1), dtype=BF16)
    const = lambda b, i: (0, 0)
    return pl.pallas_call(
        functools.partial(_router_kernel, tm=tm),
        grid=(bsz, length // tm),
        in_specs=[
            pl.BlockSpec((1, tm, d), lambda b, i: (b, i, 0)),
            pl.BlockSpec((1, 6, d), lambda b, i: (b, 0, 0)),
            pl.BlockSpec((4, d), const),
            pl.BlockSpec((3 * d, LANES), const),
            pl.BlockSpec((tm, tm), const),
        ],
        out_specs=[
            pl.BlockSpec((1, tm, d), lambda b, i: (b, i, 0)),
            pl.BlockSpec((1, tm, LANES), lambda b, i: (b, i, 0)),
            pl.BlockSpec((8, tm), lambda b, i: (0, b * (length // tm) + i)),
            pl.BlockSpec((8, LANES), const),
        ],
        out_shape=[
            jax.ShapeDtypeStruct((bsz, length, d), F32),
            jax.ShapeDtypeStruct((bsz, length, LANES), F32),
            jax.ShapeDtypeStruct((8, bsz * length), F32),
            jax.ShapeDtypeStruct((8, LANES), F32),
        ],
        scratch_shapes=[pltpu.VMEM((8, LANES), F32)],
        compiler_params=_params(("arbitrary", "arbitrary")),
        name="moe_router",
    )(x, mods, g4, wr, tri)


def _expert_kernel(te_ref, nv_ref, x_ref, w1_ref, w3_ref, w2_ref, o_ref, acc_ref):
    i = pl.program_id(0)
    j = pl.program_id(1)

    @pl.when(i < nv_ref[0])
    def _():
        y = _swiglu_halves(x_ref[...].astype(BF16), w1_ref, w3_ref, w2_ref, lead=(0,))

        @pl.when(j == 0)
        def _():
            acc_ref[...] = y

        @pl.when(j > 0)
        def _():
            acc_ref[...] += y

        @pl.when(j == pl.num_programs(1) - 1)
        def _():
            o_ref[...] = acc_ref[...]


def _experts(x_sorted, tile_expert, n_valid, w1, w3, w2):
    rows = x_sorted.shape[0]
    d = w1.shape[1]
    dff = w1.shape[2]
    tm = EXPERT_ROW_TILE
    n_ff = 2
    tf = dff // n_ff

    def row_map(i, j, te, nv):
        return (jnp.maximum(jnp.minimum(i, nv[0] - 1), 0), 0)

    def w13_map(i, j, te, nv):
        return (te[jnp.maximum(jnp.minimum(i, nv[0] - 1), 0)], 0, jnp.where(i < nv[0], j, n_ff - 1))

    def w2_map(i, j, te, nv):
        return (te[jnp.maximum(jnp.minimum(i, nv[0] - 1), 0)], jnp.where(i < nv[0], j, n_ff - 1), 0)

    return pl.pallas_call(
        _expert_kernel,
        grid_spec=pltpu.PrefetchScalarGridSpec(
            num_scalar_prefetch=2,
            grid=(rows // tm, n_ff),
            in_specs=[
                pl.BlockSpec((tm, d), row_map),
                pl.BlockSpec((1, d, tf), w13_map),
                pl.BlockSpec((1, d, tf), w13_map),
                pl.BlockSpec((1, tf, d), w2_map),
            ],
            out_specs=pl.BlockSpec((tm, d), row_map),
            scratch_shapes=[pltpu.VMEM((tm, d), F32)],
        ),
        out_shape=jax.ShapeDtypeStruct((rows, d), F32),
        compiler_params=_params(("arbitrary", "arbitrary")),
        name="expert_swiglu",
    )(tile_expert, n_valid, x_sorted, w1, w3, w2)


def _combine_kernel(x_ref, y_ref, meta_ref, g_ref, mod_ref, o_ref):
    meta = meta_ref[0]
    y = meta[:, 2:3] * y_ref[0, 0] + meta[:, 3:4] * y_ref[1, 0]
    o_ref[0] = x_ref[0] + mod_ref[0, 5:6, :] * _rms(y, g_ref[3:4, :])


def _combine(x, y2, meta, g4, mods):
    bsz, length, d = x.shape
    tm = min(2 * ROW_TILE, length)
    return pl.pallas_call(
        _combine_kernel,
        grid=(bsz, length // tm),
        in_specs=[
            pl.BlockSpec((1, tm, d), lambda b, i: (b, i, 0)),
            pl.BlockSpec((2, 1, tm, d), lambda b, i: (0, b, i, 0)),
            pl.BlockSpec((1, tm, LANES), lambda b, i: (b, i, 0)),
            pl.BlockSpec((4, d), lambda b, i: (0, 0)),
            pl.BlockSpec((1, 6, d), lambda b, i: (b, 0, 0)),
        ],
        out_specs=pl.BlockSpec((1, tm, d), lambda b, i: (b, i, 0)),
        out_shape=jax.ShapeDtypeStruct((bsz, length, d), F32),
        compiler_params=_params(("parallel", "parallel")),
        name="moe_combine",
    )(x, y2, meta, g4, mods)


SC_CORES = 2
SC_SUBCORES = 16
SC_WORKERS = SC_CORES * SC_SUBCORES
SC_ROW_CHUNK = 32


def _sc_mesh():
    return plsc.VectorSubcoreMesh(core_axis_name="c", subcore_axis_name="s")


def _sc_dispatch(rows_in, pos2, n_out):
    tokens, width = rows_in.shape
    per_worker = tokens // SC_WORKERS
    chunk = SC_ROW_CHUNK
    n_pairs = per_worker // (2 * chunk)
    idx = pos2.reshape(2, SC_WORKERS, 2 * n_pairs, chunk)

    @functools.partial(
        pl.kernel, mesh=_sc_mesh(),
        out_type=jax.ShapeDtypeStruct((n_out, width), rows_in.dtype),
        scratch_types=[
            pltpu.VMEM((2 * n_pairs, chunk), jnp.int32),
            pltpu.VMEM((2 * n_pairs, chunk), jnp.int32),
            pltpu.VMEM((2, chunk, width), rows_in.dtype),
        ] + [pltpu.SemaphoreType.DMA] * 6,
        name="sc_dispatch",
    )
    def body(rows_hbm, idx_hbm, out_hbm, idx0_v, idx1_v, rows_v, rsem0, rsem1, wsem0, wsem1, wsem2, wsem3):
        wid = lax.axis_index("s") * SC_CORES + lax.axis_index("c")
        base = wid * per_worker
        rsem = (rsem0, rsem1)
        wsem = ((wsem0, wsem1), (wsem2, wsem3))
        pltpu.sync_copy(idx_hbm.at[0, wid], idx0_v)
        pltpu.sync_copy(idx_hbm.at[1, wid], idx1_v)

        @pl.loop(0, n_pairs)
        def _(p):
            c0 = 2 * p
            reads = [pltpu.async_copy(rows_hbm.at[pl.ds(base + (c0 + s) * chunk, chunk)], rows_v.at[s], rsem[s])
                     for s in range(2)]
            writes = []
            for s in range(2):
                reads[s].wait()
                writes.append(pltpu.async_copy(rows_v.at[s], out_hbm.at[idx0_v.at[c0 + s]], wsem[s][0]))
                writes.append(pltpu.async_copy(rows_v.at[s], out_hbm.at[idx1_v.at[c0 + s]], wsem[s][1]))
            for w in writes:
                w.wait()

    return body(rows_in, idx)


def _sc_gather(table, idx):
    n = idx.shape[0]
    width = table.shape[1]
    chunk = SC_ROW_CHUNK
    per_worker = n // SC_WORKERS
    n_pairs = per_worker // (2 * chunk)
    idx3 = idx.reshape(SC_WORKERS, 2 * n_pairs, chunk)

    @functools.partial(
        pl.kernel, mesh=_sc_mesh(),
        out_type=jax.ShapeDtypeStruct((n, width), table.dtype),
        scratch_types=[
            pltpu.VMEM((2 * n_pairs, chunk), jnp.int32),
            pltpu.VMEM((2, chunk, width), table.dtype),
            pltpu.SemaphoreType.DMA, pltpu.SemaphoreType.DMA,
            pltpu.SemaphoreType.DMA, pltpu.SemaphoreType.DMA,
        ],
        name="sc_gather",
    )
    def body(table_hbm, idx_hbm, out_hbm, idx_v, rows_v, gsem0, gsem1, wsem0, wsem1):
        wid = lax.axis_index("s") * SC_CORES + lax.axis_index("c")
        base = wid * per_worker
        gsem = (gsem0, gsem1)
        wsem = (wsem0, wsem1)
        pltpu.sync_copy(idx_hbm.at[wid], idx_v)

        @pl.loop(0, n_pairs)
        def _(p):
            c0 = 2 * p
            gathers = [pltpu.async_copy(table_hbm.at[idx_v.at[c0 + s]], rows_v.at[s], gsem[s]) for s in range(2)]
            writes = []
            for s in range(2):
                gathers[s].wait()
                rows_out = out_hbm.at[pl.ds(base + (c0 + s) * chunk, chunk)]
                writes.append(pltpu.async_copy(rows_v.at[s], rows_out, wsem[s]))
            for s in range(2):
                writes[s].wait()

    return body(table, idx3)


def _moe(x, mods, g4, w_router, w1, w3, w2):
    bsz, length, d = x.shape
    tokens = bsz * length
    tm = EXPERT_ROW_TILE
    xn, meta, meta_t, counts = _router(x, mods, g4, w_router)
    experts = meta_t[0:2].astype(jnp.int32)
    ranks = meta_t[4:6].astype(jnp.int32)
    cnt = counts[0, 0:N_EXPERTS].astype(jnp.int32)
    padded = ((cnt + tm - 1) // tm) * tm
    ends = jnp.cumsum(padded)
    starts = ends - padded
    group_start = jnp.zeros_like(experts)
    for e in range(N_EXPERTS):
        group_start = jnp.where(experts == e, starts[e], group_start)
    pos = (group_start + ranks).reshape(2 * tokens)
    rows = 2 * tokens + N_EXPERTS * tm
    n_tiles = rows // tm
    n_valid = (ends[-1] // tm).astype(jnp.int32).reshape(1)
    tile_start = jnp.arange(n_tiles, dtype=jnp.int32) * tm
    tile_expert = jnp.minimum(
        jnp.sum((tile_start[:, None] >= ends[None, :]).astype(jnp.int32), axis=1), N_EXPERTS - 1)
    x_sorted = _sc_dispatch(xn.reshape(tokens, d), pos.reshape(2, tokens), rows)
    y_sorted = _experts(x_sorted, tile_expert, n_valid, w1, w3, w2)
    y2 = _sc_gather(y_sorted, pos).reshape(2, bsz, length, d)
    return _combine(x, y2, meta, g4, mods)


def _grid_pos_embed(rows, d):
    row = np.repeat(np.arange(rows, dtype=np.float64), GRID_W)
    col = np.tile(np.arange(GRID_W, dtype=np.float64), rows)
    n_freq = d // 4
    omega = POS_THETA ** (-np.arange(n_freq, dtype=np.float64) / n_freq)
    ang_r = row[:, None] * omega
    ang_c = col[:, None] * omega
    return jnp.asarray(np.concatenate([np.sin(ang_r), np.cos(ang_r), np.sin(ang_c), np.cos(ang_c)], axis=-1),
                       dtype=F32)


def kernel(x, c, ctx, c_ctx, ada_w, ada_b, norm_g, ab_w_in, rg_conv_w, rg_conv_b, rg_w_r, rg_b_r, rg_w_i, rg_b_i, rg_lambda, cv_dw_w, cv_dw_b, cv_ln_g, cv_ln_b, ab_w_out, ffn_w1, ffn_w3, ffn_w2, cd_w_in, sg_ln_g, sg_ln_b, sg_w_s, sg_b_s, cd_w_out, moe_router, moe_w1, moe_w3, moe_w2):
    bsz, length, d = x.shape
    assert ada_w.shape[0] == 2, "two layers: one even (RG-LRU | Conformer), one odd (gMLP | Fourier)"
    pos = _grid_pos_embed(length // GRID_W, d)

    cpad = jnp.concatenate([c, c_ctx[None, :], jnp.zeros((16 - bsz - 1, d), F32)], axis=0)
    m = _ada(cpad, ada_w, ada_b)
    mods = m[:, :bsz].reshape(2, bsz, 6, d)
    mods_ctx = m[:, bsz:bsz + 1].reshape(2, 1, 6, d)

    mods_t = mods[0].transpose(1, 0, 2)
    mods_ctx_t = jnp.broadcast_to(mods_ctx[0].reshape(6, 1, d), (6, bsz, d))
    pos3 = pos[:, None, :]
    w_in = ab_w_in[0]
    xa3, gg3, glu3, x0_t, ffn1b, ffn3b, ffn2b = _inproj_t(
        x, pos3, mods_t, norm_g[0], w_in, rg_conv_w[0], rg_conv_b[0], True, riders=(ffn_w1[0], ffn_w3[0], ffn_w2[0]))
    (xc3,) = _inproj_t(ctx, None, mods_ctx_t, norm_g[0], w_in[:, 0:A_WIDTH], rg_conv_w[0], rg_conv_b[0], False)
    wg, bg, sp = _gate_weights(rg_w_r[0], rg_b_r[0], rg_w_i[0], rg_b_i[0], rg_lambda[0])
    n_exp, _, dff_e = moe_w1[0].shape
    rec_hi, rec_lo, w2b = _scan_t(xa3, xc3, gg3, wg, bg, sp, riders=(moe_w2[0].reshape(n_exp * dff_e, d),))
    (u3,) = _conv_t(glu3, cv_dw_w[0], cv_dw_b[0], cv_ln_g[0], cv_ln_b[0])
    x1_t = _outproj_t(rec_lo, rec_hi, u3, x0_t, ab_w_out[0], norm_g[0], mods_t)
    x2, w1b, w3b = _ffn_t(x1_t, mods_t, norm_g[0], ffn1b, ffn3b, ffn2b,
                          riders=(moe_w1[0].reshape(n_exp * d, dff_e), moe_w3[0].reshape(n_exp * d, dff_e)))

    u, v, f = _inproj_cd(x2, mods[1], norm_g[1], cd_w_in[0])
    fo = _fourier(f)
    x3 = _gate_out(u, v, fo, x2, sg_ln_g[0], sg_ln_b[0], sg_w_s[0], sg_b_s[0], cd_w_out[0],
                   norm_g[1], mods[1])
    return _moe(x3, mods[1], norm_g[1], moe_router[0], w1b.reshape(n_exp, d, dff_e), w3b.reshape(n_exp, d, dff_e),
                w2b.reshape(n_exp, dff_e, d))
```

```python
import functools

import numpy as np
import jax
import jax.numpy as jnp
from jax import lax
from jax.experimental import pallas as pl
from jax.experimental.pallas import tpu as pltpu
from jax.experimental.pallas import tpu_sc as plsc

F32 = jnp.float32
BF16 = jnp.bfloat16

GRID_W = 64
POS_THETA = 10000.0
NORM_EPS = 1e-6
LN_EPS = 1e-5
LRU_C = 8.0

A_WIDTH = 512
A_HEADS = 8
A_CONV_W = 4
B_WIDTH = 512
B_CONV_W = 31
C_WIDTH = 512
C_HEADS = 8
CHUNK = 128
D_WIDTH = 512
D_GROUPS = 4
D_GROUP_DIM = D_WIDTH // D_GROUPS
N_EXPERTS = 8

LANES = 128
MXU_WIDTH = 256
ROW_TILE = 512
EXPERT_ROW_TILE = 512
VMEM_LIMIT = 56 * 2 ** 20


def _params(sem):
    return pltpu.CompilerParams(dimension_semantics=sem, vmem_limit_bytes=VMEM_LIMIT)


def _rms(x, g):
    return x * lax.rsqrt(jnp.mean(x * x, axis=-1, keepdims=True) + NORM_EPS) * g


def _bdot(a, b):
    return jnp.dot(a, b, preferred_element_type=F32)


def _swiglu_halves(x, w1_ref, w3_ref, w2_ref, lead=()):
    n = w1_ref.shape[-1]
    half = -(-(n // 2) // MXU_WIDTH) * MXU_WIDTH
    y = None
    for lo, hi in ((0, half), (half, n)):
        a = _bdot(x, w1_ref[lead + (slice(None), slice(lo, hi))])
        b = _bdot(x, w3_ref[lead + (slice(None), slice(lo, hi))])
        t = (jax.nn.silu(a) * b).astype(BF16)
        part = _bdot(t, w2_ref[lead + (slice(lo, hi), slice(None))])
        y = part if y is None else y + part
    return y


def _ada_kernel(c_ref, w_ref, b_ref, o_ref):
    s = jax.nn.silu(c_ref[...])
    o_ref[0] = _bdot(s.astype(BF16), w_ref[0].astype(BF16)) + b_ref[0]


def _ada(cpad, ada_w, ada_b):
    nl, d, n6 = ada_w.shape
    rows = cpad.shape[0]
    tn = n6 // 4
    return pl.pallas_call(
        _ada_kernel,
        grid=(nl, n6 // tn),
        in_specs=[
            pl.BlockSpec((rows, d), lambda l, j: (0, 0)),
            pl.BlockSpec((1, d, tn), lambda l, j: (l, 0, j)),
            pl.BlockSpec((1, 1, tn), lambda l, j: (l, 0, j)),
        ],
        out_specs=pl.BlockSpec((1, rows, tn), lambda l, j: (l, 0, j)),
        out_shape=jax.ShapeDtypeStruct((nl, rows, n6), F32),
        compiler_params=_params(("arbitrary", "arbitrary")),
        name="ada_terms",
    )(cpad, ada_w, ada_b.reshape(nl, 1, n6))


TIME_TILE = 128
SCAN_TIME_TILE = 128
SCAN_LANES = 256
CONV_TIME_TILE = 128
CONV_HALO = 16


def _time_major(ref):
    return jnp.swapaxes(ref[...], 0, 1)


def _cast_riders(cast_in, cast_out):
    for src, dst in zip(cast_in, cast_out):
        dst[...] = src[...].astype(BF16)


def _rider_specs(arrays, n_steps, step_of):
    in_specs, out_specs, out_shapes = [], [], []
    for arr in arrays:
        rows, cols = arr.shape
        block = (rows // n_steps, cols)
        in_specs.append(pl.BlockSpec(block, lambda *idx: (step_of(*idx), 0)))
        out_specs.append(pl.BlockSpec(block, lambda *idx: (step_of(*idx), 0)))
        out_shapes.append(jax.ShapeDtypeStruct(arr.shape, BF16))
    return in_specs, out_specs, out_shapes


def _inproj_t_kernel(*refs, tt, add_pos, branches, n_cast):
    if n_cast:
        n_in = len(refs) - n_cast - (4 if branches else 1) - n_cast
        _cast_riders(refs[n_in:n_in + n_cast], refs[len(refs) - n_cast:])
        refs = refs[:n_in] + refs[n_in + n_cast:len(refs) - n_cast]
    if add_pos:
        x_ref, xp_ref, xn_ref, pos_ref, posp_ref, posn_ref, mod_ref, g_ref, w_ref, cw_ref, cb_ref, *outs = refs
        xall = jnp.concatenate(
            [_time_major(xp_ref)[7:8] + posp_ref[...], _time_major(x_ref) + pos_ref[...],
             _time_major(xn_ref)[0:2] + posn_ref[...]], axis=0)
    else:
        x_ref, xp_ref, xn_ref, mod_ref, g_ref, w_ref, cw_ref, cb_ref, *outs = refs
        xall = jnp.concatenate([_time_major(xp_ref)[7:8], _time_major(x_ref), _time_major(xn_ref)[0:2]], axis=0)
    i = pl.program_id(0)
    last = pl.num_programs(0) - 1
    steps, bsz, d = xall.shape
    h = _rms(xall, g_ref[0:1, :]) * (1.0 + mod_ref[1]) + mod_ref[0]
    z = _bdot(h.reshape(steps * bsz, d).astype(BF16), w_ref[...].astype(BF16))
    z3 = z.reshape(steps, bsz, z.shape[-1])
    t_idx = lax.broadcasted_iota(jnp.int32, (steps, bsz, A_WIDTH), 0)
    inside = jnp.logical_and(jnp.logical_or(t_idx >= 1, i > 0), jnp.logical_or(t_idx <= tt, i < last))
    xa = jnp.where(inside, z3[:, :, 0:A_WIDTH], 0.0)
    xconv = cb_ref[...] + cw_ref[0:1, :] * xa[0:tt]
    for k in range(1, A_CONV_W):
        xconv = xconv + cw_ref[k:k + 1, :] * xa[k:k + tt]
    outs[0][...] = xconv
    if branches:
        outs[1][...] = jax.nn.gelu(z3[1:tt + 1, :, A_WIDTH:2 * A_WIDTH])
        vb = z3[1:tt + 1, :, 2 * A_WIDTH:2 * A_WIDTH + B_WIDTH]
        gb = z3[1:tt + 1, :, 2 * A_WIDTH + B_WIDTH:]
        outs[2][...] = vb * jax.nn.sigmoid(gb)
        outs[3][...] = xall[1:tt + 1]


def _inproj_t(x, pos3, mods_t, g4, w, conv_w, conv_b, branches, riders=()):
    bsz, length, d = x.shape
    n = w.shape[1]
    tt = min(TIME_TILE, length)
    add_pos = pos3 is not None
    n_out = 3 if branches else 1
    prev_map = lambda i: (jnp.maximum(i * tt - 1, 0), 0, 0)
    next_map = lambda i: (jnp.minimum((i + 1) * (tt // 2), length // 2 - 1), 0, 0)
    in_specs = [
        pl.BlockSpec((bsz, tt, d), lambda i: (0, i, 0)),
        pl.BlockSpec((bsz, 8, d), lambda i: (0, jnp.maximum(i * (tt // 8) - 1, 0), 0)),
        pl.BlockSpec((bsz, 8, d), lambda i: (0, jnp.minimum((i + 1) * (tt // 8), length // 8 - 1), 0)),
    ]
    args = [x, x, x]
    if add_pos:
        in_specs += [
            pl.BlockSpec((tt, 1, d), lambda i: (i, 0, 0)),
            pl.BlockSpec((1, 1, d), prev_map),
            pl.BlockSpec((2, 1, d), next_map),
        ]
        args += [pos3, pos3, pos3]
    in_specs += [
        pl.BlockSpec((6, bsz, d), lambda i: (0, 0, 0)),
        pl.BlockSpec((4, d), lambda i: (0, 0)),
        pl.BlockSpec((d, n), lambda i: (0, 0)),
        pl.BlockSpec((A_CONV_W, A_WIDTH), lambda i: (0, 0)),
        pl.BlockSpec((1, A_WIDTH), lambda i: (0, 0)),
    ]
    args += [mods_t, g4, w, conv_w, conv_b.reshape(1, A_WIDTH)]
    r_in, r_out, r_shapes = _rider_specs(riders, length // tt, lambda i: i)
    out_specs = [pl.BlockSpec((tt, bsz, A_WIDTH), lambda i: (i, 0, 0))] * n_out
    out_shapes = [jax.ShapeDtypeStruct((length, bsz, A_WIDTH), F32)] * n_out
    if branches:
        out_specs.append(pl.BlockSpec((tt, bsz, d), lambda i: (i, 0, 0)))
        out_shapes.append(jax.ShapeDtypeStruct((length, bsz, d), F32))
    return pl.pallas_call(
        functools.partial(_inproj_t_kernel, tt=tt, add_pos=add_pos, branches=branches, n_cast=len(riders)),
        grid=(length // tt,),
        in_specs=in_specs + r_in,
        out_specs=out_specs + r_out,
        out_shape=out_shapes + r_shapes,
        compiler_params=_params(("parallel",)),
        name="inproj_ab" if branches else "inproj_ctx",
    )(*args, *riders)


def _scan_t_kernel(*refs, tb, nb, ctx_len, n_cast):
    xf_ref, xb_ref, ggf_ref, ggb_ref, xc_ref, wg_ref, bg_ref, sp_ref = refs[:8]
    cast_in = refs[8:8 + n_cast]
    hi_ref, lo_ref = refs[8 + n_cast:10 + n_cast]
    cast_out = refs[10 + n_cast:10 + 2 * n_cast]
    a_ref, b_ref, hs_ref, h_ref, state_ref = refs[10 + 2 * n_cast:]
    _cast_riders(cast_in, cast_out)
    k = pl.program_id(1)
    half = nb // 2
    bsz, lanes = xf_ref.shape[1], xf_ref.shape[2]
    chunk = 128

    def coefficients(x_ref, n, direction):
        def body(c, _):
            t0 = pl.multiple_of(c * chunk, chunk)
            x = x_ref[pl.ds(t0, chunk)].reshape(chunk * bsz, lanes)
            th = jnp.tanh(_bdot(x.astype(BF16), wg_ref[0, direction]) + bg_ref[0, direction])
            i = 0.5 * th[:, lanes:] + 0.5
            c = (-0.5 * LRU_C) * sp_ref[0, direction:direction + 1, :]
            log_a = c * th[:, 0:lanes] + c
            a = jnp.exp(log_a)
            one_minus_a2 = -jnp.tanh(log_a) * (a * a + 1.0)
            a_ref[direction, pl.ds(t0, chunk)] = a.reshape(chunk, bsz, lanes)
            b_ref[direction, pl.ds(t0, chunk)] = (jnp.sqrt(one_minus_a2) * (i * x)).reshape(chunk, bsz, lanes)
            return 0
        lax.fori_loop(0, n // chunk, body, 0)

    def sweep(n, store):
        def body(t, carry):
            hf, hb = carry
            tr = n - 1 - t
            hf = a_ref[0, pl.ds(t, 1)][0] * hf + b_ref[0, pl.ds(t, 1)][0]
            hb = a_ref[1, pl.ds(tr, 1)][0] * hb + b_ref[1, pl.ds(tr, 1)][0]
            if store:
                hs_ref[0, pl.ds(t, 1)] = hf[None]
                hs_ref[1, pl.ds(tr, 1)] = hb[None]
            return hf, hb
        hf, hb = lax.fori_loop(0, n, body, (state_ref[0], state_ref[1]), unroll=8)
        state_ref[0] = hf
        state_ref[1] = hb

    @pl.when(k == 0)
    def _():
        state_ref[...] = jnp.zeros_like(state_ref)
        coefficients(xc_ref, ctx_len, 0)
        coefficients(xc_ref, ctx_len, 1)
        sweep(ctx_len, False)

    coefficients(xf_ref, tb, 0)
    coefficients(xb_ref, tb, 1)
    sweep(tb, True)
    m = nb - 1 - k

    @pl.when(k < half)
    def _():
        h_ref[k] = hs_ref[0]
        h_ref[m] = hs_ref[1]

    @pl.when(k >= half)
    def _():
        hi_ref[...] = (hs_ref[0] + h_ref[k]) * ggf_ref[...]
        lo_ref[...] = (h_ref[m] + hs_ref[1]) * ggb_ref[...]


def _scan_t(xa3, xc3, gg3, wg, bg, sp, riders=()):
    length, bsz, width = xa3.shape
    ctx_len = xc3.shape[0]
    tb = min(SCAN_TIME_TILE, length // 2)
    nb = length // tb
    half = nb // 2
    lanes = SCAN_LANES
    blk = (tb, bsz, lanes)
    r_in, r_out, r_shapes = _rider_specs(riders, (width // lanes) * nb, lambda g, k: g * nb + k)
    return pl.pallas_call(
        functools.partial(_scan_t_kernel, tb=tb, nb=nb, ctx_len=ctx_len, n_cast=len(riders)),
        grid=(width // lanes, nb),
        in_specs=[
            pl.BlockSpec(blk, lambda g, k: (k, 0, g)),
            pl.BlockSpec(blk, lambda g, k: (nb - 1 - k, 0, g)),
            pl.BlockSpec(blk, lambda g, k: (jnp.maximum(k, half), 0, g)),
            pl.BlockSpec(blk, lambda g, k: (jnp.minimum(nb - 1 - k, half - 1), 0, g)),
            pl.BlockSpec((ctx_len, bsz, lanes), lambda g, k: (0, 0, g)),
            pl.BlockSpec((1, 2, lanes, 2 * lanes), lambda g, k: (g, 0, 0, 0)),
            pl.BlockSpec((1, 2, 1, 2 * lanes), lambda g, k: (g, 0, 0, 0)),
            pl.BlockSpec((1, 2, lanes), lambda g, k: (g, 0, 0)),
        ] + r_in,
        out_specs=[
            pl.BlockSpec(blk, lambda g, k: (jnp.maximum(k - half, 0), 0, g)),
            pl.BlockSpec(blk, lambda g, k: (jnp.minimum(nb - 1 - k, half - 1), 0, g)),
        ] + r_out,
        out_shape=[jax.ShapeDtypeStruct((length // 2, bsz, width), F32)] * 2 + r_shapes,
        scratch_shapes=[
            pltpu.VMEM((2, max(tb, ctx_len), bsz, lanes), F32),
            pltpu.VMEM((2, max(tb, ctx_len), bsz, lanes), F32),
            pltpu.VMEM((2, tb, bsz, lanes), F32),
            pltpu.VMEM((nb, tb, bsz, lanes), F32),
            pltpu.VMEM((2, bsz, lanes), F32),
        ],
        compiler_params=_params(("parallel", "arbitrary")),
        name="rglru_scan",
    )(xa3, xa3, gg3, gg3, xc3, wg, bg, sp, *riders)


def _gate_weights(w_r, b_r, w_i, b_i, lam):
    hd = A_WIDTH // A_HEADS
    heads_per_group = SCAN_LANES // hd
    ngroups = A_WIDTH // SCAN_LANES

    def blockdiag(w):
        w = w.reshape(ngroups, heads_per_group, hd, hd)
        eye = jnp.eye(heads_per_group, dtype=w.dtype)
        return jnp.einsum("ghde,hk->ghdke", w, eye).reshape(ngroups, SCAN_LANES, SCAN_LANES)

    wg = jnp.stack([jnp.concatenate([blockdiag(w_r[d]), blockdiag(w_i[d])], axis=-1) for d in range(2)], axis=1)
    bg = jnp.stack([jnp.concatenate([b_r[d].reshape(ngroups, 1, SCAN_LANES), b_i[d].reshape(ngroups, 1, SCAN_LANES)],
                                    axis=-1) for d in range(2)], axis=1)
    sp = jax.nn.softplus(-lam.astype(F32)).reshape(2, ngroups, SCAN_LANES).transpose(1, 0, 2)
    return (0.5 * wg).astype(BF16), 0.5 * bg, sp


def _conv_t_kernel(*refs, tb, n_cast):
    x_ref, xp_ref, xn_ref, w_ref, b_ref, lng_ref, lnb_ref = refs[:7]
    cast_in = refs[7:7 + n_cast]
    o_ref = refs[7 + n_cast]
    cast_out = refs[8 + n_cast:8 + 2 * n_cast]
    stage_ref = refs[-1]
    _cast_riders(cast_in, cast_out)
    i = pl.program_id(0)
    last = pl.num_programs(0) - 1
    halo = CONV_HALO
    stage_ref[0:halo] = jnp.where(i > 0, xp_ref[...], 0.0)
    stage_ref[halo:halo + tb] = x_ref[...]
    stage_ref[halo + tb:2 * halo + tb] = jnp.where(i < last, xn_ref[...], 0.0)
    sub = 8

    def body(r, _):
        t0 = pl.multiple_of(r * sub, sub)
        def tap(k, acc):
            return acc + w_ref[k] * stage_ref[pl.ds(t0 + 1 + k, sub)]
        acc = lax.fori_loop(0, B_CONV_W + 1, tap, jnp.broadcast_to(b_ref[...], (sub,) + b_ref.shape), unroll=8)
        o_ref[pl.ds(t0, sub)] = acc
        return 0
    lax.fori_loop(0, tb // sub, body, 0)

    norm_rows = 64

    def norm_body(r, _):
        t0 = pl.multiple_of(r * norm_rows, norm_rows)
        acc = o_ref[pl.ds(t0, norm_rows)]
        mu = jnp.mean(acc, axis=-1, keepdims=True)
        cen = acc - mu
        var = jnp.mean(cen * cen, axis=-1, keepdims=True)
        y = cen * lax.rsqrt(var + LN_EPS) * lng_ref[...] + lnb_ref[...]
        o_ref[pl.ds(t0, norm_rows)] = jax.nn.silu(y)
        return 0
    lax.fori_loop(0, tb // norm_rows, norm_body, 0)


def _conv_t(glu3, dw_w, dw_b, ln_g, ln_b, riders=()):
    length, bsz, width = glu3.shape
    tb = min(CONV_TIME_TILE, length)
    halo = CONV_HALO
    taps = dw_w.shape[0]
    w8 = jnp.broadcast_to(jnp.concatenate([dw_w, jnp.zeros((32 - taps, width), F32)], axis=0)[:, None, :],
                          (32, bsz, width))
    b8 = jnp.broadcast_to(dw_b[None, :], (bsz, width))
    r_in, r_out, r_shapes = _rider_specs(riders, length // tb, lambda i: i)
    return pl.pallas_call(
        functools.partial(_conv_t_kernel, tb=tb, n_cast=len(riders)),
        grid=(length // tb,),
        in_specs=[
            pl.BlockSpec((tb, bsz, width), lambda i: (i, 0, 0)),
            pl.BlockSpec((halo, bsz, width), lambda i: (jnp.maximum(i * (tb // halo) - 1, 0), 0, 0)),
            pl.BlockSpec((halo, bsz, width), lambda i: (jnp.minimum((i + 1) * (tb // halo), length // halo - 1), 0, 0)),
            pl.BlockSpec((32, bsz, width), lambda i: (0, 0, 0)),
            pl.BlockSpec((bsz, width), lambda i: (0, 0)),
            pl.BlockSpec((1, width), lambda i: (0, 0)),
            pl.BlockSpec((1, width), lambda i: (0, 0)),
        ] + r_in,
        out_specs=[pl.BlockSpec((tb, bsz, width), lambda i: (i, 0, 0))] + r_out,
        out_shape=[jax.ShapeDtypeStruct((length, bsz, width), F32)] + r_shapes,
        scratch_shapes=[pltpu.VMEM((tb + 2 * halo, bsz, width), F32)],
        compiler_params=_params(("parallel",)),
        name="conformer_conv",
    )(glu3, glu3, glu3, w8, b8, ln_g.reshape(1, -1), ln_b.reshape(1, -1), *riders)


def _outproj_t_kernel(lo_ref, hi_ref, u_ref, x_ref, wout_ref, g_ref, mod_ref, o_ref, *, half_steps):
    i = pl.program_id(0)
    tt, bsz, d = x_ref.shape
    rec = jnp.where(i < half_steps, lo_ref[...], hi_ref[...])
    rows = tt * bsz
    y = (_bdot(rec.reshape(rows, A_WIDTH).astype(BF16), wout_ref[0:A_WIDTH, :].astype(BF16))
         + _bdot(u_ref[...].reshape(rows, B_WIDTH).astype(BF16), wout_ref[A_WIDTH:, :].astype(BF16)))
    yn = _rms(y, g_ref[1:2, :]).reshape(tt, bsz, d)
    o_ref[...] = x_ref[...] + mod_ref[2] * yn


def _outproj_t(rec_lo, rec_hi, u3, x0_t, w_out, g4, mods_t):
    length, bsz, d = x0_t.shape
    tt = min(TIME_TILE, length // 2)
    half_steps = (length // 2) // tt
    return pl.pallas_call(
        functools.partial(_outproj_t_kernel, half_steps=half_steps),
        grid=(length // tt,),
        in_specs=[
            pl.BlockSpec((tt, bsz, A_WIDTH), lambda i: (jnp.minimum(i, half_steps - 1), 0, 0)),
            pl.BlockSpec((tt, bsz, A_WIDTH), lambda i: (jnp.maximum(i - half_steps, 0), 0, 0)),
            pl.BlockSpec((tt, bsz, B_WIDTH), lambda i: (i, 0, 0)),
            pl.BlockSpec((tt, bsz, d), lambda i: (i, 0, 0)),
            pl.BlockSpec((A_WIDTH + B_WIDTH, d), lambda i: (0, 0)),
            pl.BlockSpec((4, d), lambda i: (0, 0)),
            pl.BlockSpec((6, bsz, d), lambda i: (0, 0, 0)),
        ],
        out_specs=pl.BlockSpec((tt, bsz, d), lambda i: (i, 0, 0)),
        out_shape=jax.ShapeDtypeStruct((length, bsz, d), F32),
        compiler_params=_params(("parallel",)),
        name="mixer_ab_outproj",
    )(rec_lo, rec_hi, u3, x0_t, w_out, g4, mods_t)


def _ffn_t_kernel(x_ref, mod_ref, g_ref, w1_ref, w3_ref, w2_ref, *rest, n_cast):
    _cast_riders(rest[:n_cast], rest[n_cast + 1:])
    o_ref = rest[n_cast]
    x = x_ref[...]
    tt, bsz, d = x.shape
    h = (_rms(x, g_ref[2:3, :]) * (1.0 + mod_ref[4]) + mod_ref[3]).reshape(tt * bsz, d).astype(BF16)
    y = _swiglu_halves(h, w1_ref, w3_ref, w2_ref)
    o_ref[...] = jnp.swapaxes(x + mod_ref[5] * _rms(y, g_ref[3:4, :]).reshape(tt, bsz, d), 0, 1)


def _ffn_t(x3, mods_t, g4, w1, w3, w2, riders=()):
    length, bsz, d = x3.shape
    dff = w1.shape[1]
    tt = min(TIME_TILE // 2, length)
    r_in, r_out, r_shapes = _rider_specs(riders, length // tt, lambda i: i)
    return pl.pallas_call(
        functools.partial(_ffn_t_kernel, n_cast=len(riders)),
        grid=(length // tt,),
        in_specs=[
            pl.BlockSpec((tt, bsz, d), lambda i: (i, 0, 0)),
            pl.BlockSpec((6, bsz, d), lambda i: (0, 0, 0)),
            pl.BlockSpec((4, d), lambda i: (0, 0)),
            pl.BlockSpec((d, dff), lambda i: (0, 0), pipeline_mode=pl.Buffered(1)),
            pl.BlockSpec((d, dff), lambda i: (0, 0), pipeline_mode=pl.Buffered(1)),
            pl.BlockSpec((dff, d), lambda i: (0, 0), pipeline_mode=pl.Buffered(1)),
        ] + r_in,
        out_specs=[pl.BlockSpec((bsz, tt, d), lambda i: (0, i, 0))] + r_out,
        out_shape=[jax.ShapeDtypeStruct((bsz, length, d), F32)] + r_shapes,
        compiler_params=_params(("parallel",)),
        name="dense_swiglu",
    )(x3, mods_t, g4, w1, w3, w2, *riders)


def _inproj_cd_kernel(x_ref, mod_ref, g_ref, w_ref, u_ref, v_ref, f_ref):
    x = x_ref[0]
    h = _rms(x, g_ref[0:1, :]) * (1.0 + mod_ref[0, 1:2, :]) + mod_ref[0, 0:1, :]
    z = _bdot(h.astype(BF16), w_ref[...].astype(BF16))
    u_ref[0] = jax.nn.gelu(z[:, 0:C_WIDTH])
    v_ref[0] = jax.nn.gelu(z[:, C_WIDTH:2 * C_WIDTH])
    f_ref[0] = z[:, 2 * C_WIDTH:]


def _inproj_cd(x, mods, g4, w):
    bsz, length, d = x.shape
    n = w.shape[1]
    tm = min(2 * ROW_TILE, length)
    return pl.pallas_call(
        _inproj_cd_kernel,
        grid=(bsz, length // tm),
        in_specs=[
            pl.BlockSpec((1, tm, d), lambda b, i: (b, i, 0)),
            pl.BlockSpec((1, 6, d), lambda b, i: (b, 0, 0)),
            pl.BlockSpec((4, d), lambda b, i: (0, 0)),
            pl.BlockSpec((d, n), lambda b, i: (0, 0)),
        ],
        out_specs=[pl.BlockSpec((1, tm, C_WIDTH), lambda b, i: (b, i, 0))] * 3,
        out_shape=[jax.ShapeDtypeStruct((bsz, length, C_WIDTH), F32)] * 3,
        compiler_params=_params(("parallel", "parallel")),
        name="inproj_cd",
    )(x, mods, g4, w)


def _fourier_kernel(f_ref, dmat_ref, lmat_ref, o_ref, z_ref, *, length, scale):
    j = pl.program_id(1)
    rows_per_step = 512 if length % 512 == 0 else length

    @pl.when(j == 0)
    def _():
        def body(c, _):
            rows = pl.ds(pl.multiple_of(c * rows_per_step, rows_per_step), rows_per_step)
            fb = f_ref[0, rows, :].astype(BF16)
            for g in range(D_GROUPS):
                cs = _bdot(fb[:, g * D_GROUP_DIM:(g + 1) * D_GROUP_DIM], dmat_ref[...])
                z_ref[0, rows, g * D_GROUP_DIM:(g + 1) * D_GROUP_DIM] = cs[:, 0:D_GROUP_DIM].astype(BF16)
                z_ref[1, rows, g * D_GROUP_DIM:(g + 1) * D_GROUP_DIM] = cs[:, D_GROUP_DIM:].astype(BF16)
            return 0
        lax.fori_loop(0, length // rows_per_step, body, 0)

    out = _bdot(lmat_ref[0], z_ref[0]) + _bdot(lmat_ref[1], z_ref[1])
    o_ref[0] = (out * scale).astype(BF16)


def _dft_matrices(length):
    k = np.arange(length, dtype=np.int64)
    ang_l = 2.0 * np.pi * ((k[:, None] * k[None, :]) % length).astype(np.float64) / length
    lmat = np.stack([np.cos(ang_l), -np.sin(ang_l)], axis=0)
    d = np.arange(D_GROUP_DIM, dtype=np.int64)
    ang_d = 2.0 * np.pi * ((d[:, None] * d[None, :]) % D_GROUP_DIM).astype(np.float64) / D_GROUP_DIM
    dmat = np.concatenate([np.cos(ang_d), np.sin(ang_d)], axis=1)
    return jnp.asarray(lmat, dtype=F32).astype(BF16), jnp.asarray(dmat, dtype=F32).astype(BF16)


def _fourier(f):
    bsz, length, width = f.shape
    tm = min(4 * ROW_TILE, length)
    lmat, dmat = _dft_matrices(length)
    scale = float(1.0 / np.sqrt(length * D_GROUP_DIM))
    return pl.pallas_call(
        functools.partial(_fourier_kernel, length=length, scale=scale),
        grid=(bsz, length // tm),
        in_specs=[
            pl.BlockSpec((1, length, width), lambda b, i: (b, 0, 0)),
            pl.BlockSpec((D_GROUP_DIM, 2 * D_GROUP_DIM), lambda b, i: (0, 0)),
            pl.BlockSpec((2, tm, length), lambda b, i: (0, i, 0)),
        ],
        out_specs=pl.BlockSpec((1, tm, width), lambda b, i: (b, i, 0)),
        out_shape=jax.ShapeDtypeStruct((bsz, length, width), BF16),
        scratch_shapes=[pltpu.VMEM((2, length, width), BF16)],
        compiler_params=_params(("parallel", "arbitrary")),
        name="fourier_mix",
    )(f, dmat, lmat)


def _gate_out_kernel(u_ref, v_ref, fo_ref, x_ref, lng_ref, lnb_ref, wcat_ref, bs_ref, wout_ref, g_ref, mod_ref,
                     o_ref, vs_ref, gt_ref, *, tm):
    head_dim = C_WIDTH // C_HEADS
    head_of_lane = lax.broadcasted_iota(jnp.int32, (CHUNK, C_WIDTH), 1) // head_dim
    for ci in range(tm // CHUNK):
        rows = slice(ci * CHUNK, (ci + 1) * CHUNK)
        v = v_ref[0, rows, :]
        mu = jnp.mean(v, axis=-1, keepdims=True)
        cen = v - mu
        var = jnp.mean(cen * cen, axis=-1, keepdims=True)
        vn = (cen * lax.rsqrt(var + LN_EPS) * lng_ref[...] + lnb_ref[...]).astype(BF16)
        for h in range(C_HEADS):
            vs_ref[h * CHUNK:(h + 1) * CHUNK, :] = jnp.where(head_of_lane == h, vn, jnp.zeros_like(vn))
        mixed = _bdot(wcat_ref[...], vs_ref[...]) + bs_ref[...]
        gt_ref[rows, :] = (u_ref[0, rows, :] * mixed).astype(BF16)
    y = (_bdot(gt_ref[...], wout_ref[0:C_WIDTH, :].astype(BF16))
         + _bdot(fo_ref[0], wout_ref[C_WIDTH:, :].astype(BF16)))
    o_ref[0] = x_ref[0] + mod_ref[0, 2:3, :] * _rms(y, g_ref[1:2, :])


def _gate_out(u, v, fo, x, ln_g, ln_b, w_s, b_s, w_out, g4, mods):
    bsz, length, d = x.shape
    tm = min(2 * ROW_TILE, length)
    head_dim = C_WIDTH // C_HEADS
    wcat = w_s.transpose(1, 0, 2).reshape(CHUNK, C_HEADS * CHUNK).astype(BF16)
    bs_full = jnp.repeat(b_s.T, head_dim, axis=1)
    const = lambda b, i: (0, 0)
    return pl.pallas_call(
        functools.partial(_gate_out_kernel, tm=tm),
        grid=(bsz, length // tm),
        in_specs=[
            pl.BlockSpec((1, tm, C_WIDTH), lambda b, i: (b, i, 0)),
            pl.BlockSpec((1, tm, C_WIDTH), lambda b, i: (b, i, 0)),
            pl.BlockSpec((1, tm, D_WIDTH), lambda b, i: (b, i, 0)),
            pl.BlockSpec((1, tm, d), lambda b, i: (b, i, 0)),
            pl.BlockSpec((1, C_WIDTH), const),
            pl.BlockSpec((1, C_WIDTH), const),
            pl.BlockSpec((CHUNK, C_HEADS * CHUNK), const),
            pl.BlockSpec((CHUNK, C_WIDTH), const),
            pl.BlockSpec((C_WIDTH + D_WIDTH, d), const),
            pl.BlockSpec((4, d), const),
            pl.BlockSpec((1, 6, d), lambda b, i: (b, 0, 0)),
        ],
        out_specs=pl.BlockSpec((1, tm, d), lambda b, i: (b, i, 0)),
        out_shape=jax.ShapeDtypeStruct((bsz, length, d), F32),
        scratch_shapes=[
            pltpu.VMEM((C_HEADS * CHUNK, C_WIDTH), BF16),
            pltpu.VMEM((tm, C_WIDTH), BF16),
        ],
        compiler_params=_params(("parallel", "parallel")),
        name="gating_outproj",
    )(u, v, fo, x, ln_g.reshape(1, -1), ln_b.reshape(1, -1), wcat, bs_full, w_out, g4, mods)


def _router_kernel(x_ref, mod_ref, g_ref, wr_ref, tri_ref, xn_ref, meta_ref, metat_ref, cnt_ref, carry_ref, *, tm):
    first = jnp.logical_and(pl.program_id(0) == 0, pl.program_id(1) == 0)

    @pl.when(first)
    def _():
        carry_ref[...] = jnp.zeros_like(carry_ref)

    x = x_ref[0]
    h = _rms(x, g_ref[2:3, :]) * (1.0 + mod_ref[0, 4:5, :]) + mod_ref[0, 3:4, :]
    xn_ref[0] = h
    h_hi = h.astype(BF16)
    h_lo = (h - h_hi.astype(F32)).astype(BF16)
    logits = _bdot(jnp.concatenate([h_hi, h_hi, h_lo], axis=1), wr_ref[...])
    lane = lax.broadcasted_iota(jnp.int32, (tm, LANES), 1).astype(F32)
    neg = jnp.float32(-jnp.inf)
    lm = jnp.where(lane < N_EXPERTS, logits, neg)
    m1 = jnp.max(lm, axis=-1, keepdims=True)
    i1 = jnp.min(jnp.where(lm == m1, lane, float(LANES)), axis=-1, keepdims=True)
    lm2 = jnp.where(lane == i1, neg, lm)
    m2 = jnp.max(lm2, axis=-1, keepdims=True)
    i2 = jnp.min(jnp.where(lm2 == m2, lane, float(LANES)), axis=-1, keepdims=True)
    e = jnp.exp(m2 - m1)
    p1 = 1.0 / (1.0 + e)
    p2 = e / (1.0 + e)
    hot1 = (lane == i1).astype(F32)
    hot2 = (lane == i2).astype(F32)
    both = hot1 + hot2
    before = _bdot(tri_ref[...], both.astype(BF16)) + carry_ref[0:1, :]
    r1 = jnp.sum(before * hot1, axis=-1, keepdims=True)
    r2 = jnp.sum(before * hot2, axis=-1, keepdims=True)
    carry_ref[0:1, :] = carry_ref[0:1, :] + jnp.sum(both, axis=0, keepdims=True)
    cnt_ref[...] = carry_ref[...]
    vals = (i1, i2, p1, p2, r1, r2)
    meta = jnp.zeros((tm, LANES), F32)
    for k, val in enumerate(vals):
        meta = jnp.where(lane == k, val, meta)
    meta_ref[0] = meta
    metat_ref[...] = meta.T[0:8, :]


def _router(x, mods, g4, w_router):
    bsz, length, d = x.shape
    tm = min(2 * ROW_TILE, length)
    wr = jnp.concatenate([w_router, jnp.zeros((d, LANES - N_EXPERTS), F32)], axis=1)
    wr_hi = wr.astype(BF16)
    wr_lo = (wr - wr_hi.astype(F32)).astype(BF16)
    wr = jnp.concatenate([wr_hi, wr_lo, wr_hi], axis=0)
    tri = jnp.asarray(np.tril(np.ones((tm, tm), np.float32), -1), dtype=BF16)
    const = lambda b, i: (0, 0)
    return pl.pallas_call(
        functools.partial(_router_kernel, tm=tm),
        grid=(bsz, length // tm),
        in_specs=[
            pl.BlockSpec((1, tm, d), lambda b, i: (b, i, 0)),
            pl.BlockSpec((1, 6, d), lambda b, i: (b, 0, 0)),
            pl.BlockSpec((4, d), const),
            pl.BlockSpec((3 * d, LANES), const),
            pl.BlockSpec((tm, tm), const),
        ],
        out_specs=[
            pl.BlockSpec((1, tm, d), lambda b, i: (b, i, 0)),
            pl.BlockSpec((1, tm, LANES), lambda b, i: (b, i, 0)),
            pl.BlockSpec((8, tm), lambda b, i: (0, b * (length // tm) + i)),
            pl.BlockSpec((8, LANES), const),
        ],
        out_shape=[
            jax.ShapeDtypeStruct((bsz, length, d), F32),
            jax.ShapeDtypeStruct((bsz, length, LANES), F32),
            jax.ShapeDtypeStruct((8, bsz * length), F32),
            jax.ShapeDtypeStruct((8, LANES), F32),
        ],
        scratch_shapes=[pltpu.VMEM((8, LANES), F32)],
        compiler_params=_params(("arbitrary", "arbitrary")),
        name="moe_router",
    )(x, mods, g4, wr, tri)


def _expert_kernel(te_ref, nv_ref, th_ref, x_ref, w1_ref, w3_ref, w2_ref, o_ref, acc_ref):
    i = pl.program_id(0)
    j = pl.program_id(1)
    tm = x_ref.shape[0]

    def run(rows):
        y = _swiglu_halves(x_ref[0:rows, :].astype(BF16), w1_ref, w3_ref, w2_ref, lead=(0,))

        @pl.when(j == 0)
        def _():
            acc_ref[0:rows, :] = y

        @pl.when(j > 0)
        def _():
            acc_ref[0:rows, :] += y

        @pl.when(j == pl.num_programs(1) - 1)
        def _():
            o_ref[0:rows, :] = acc_ref[0:rows, :]

    valid = i < nv_ref[0]
    half_full = th_ref[i] == 1

    @pl.when(jnp.logical_and(valid, jnp.logical_not(half_full)))
    def _():
        run(tm)

    @pl.when(jnp.logical_and(valid, half_full))
    def _():
        run(tm // 2)


def _experts(x_sorted, tile_expert, n_valid, tile_half, w1, w3, w2):
    rows = x_sorted.shape[0]
    d = w1.shape[1]
    dff = w1.shape[2]
    tm = EXPERT_ROW_TILE
    n_ff = 2
    tf = dff // n_ff

    def row_map(i, j, te, nv, th):
        return (jnp.maximum(jnp.minimum(i, nv[0] - 1), 0), 0)

    def w13_map(i, j, te, nv, th):
        return (te[jnp.maximum(jnp.minimum(i, nv[0] - 1), 0)], 0, jnp.where(i < nv[0], j, n_ff - 1))

    def w2_map(i, j, te, nv, th):
        return (te[jnp.maximum(jnp.minimum(i, nv[0] - 1), 0)], jnp.where(i < nv[0], j, n_ff - 1), 0)

    return pl.pallas_call(
        _expert_kernel,
        grid_spec=pltpu.PrefetchScalarGridSpec(
            num_scalar_prefetch=3,
            grid=(rows // tm, n_ff),
            in_specs=[
                pl.BlockSpec((tm, d), row_map),
                pl.BlockSpec((1, d, tf), w13_map),
                pl.BlockSpec((1, d, tf), w13_map),
                pl.BlockSpec((1, tf, d), w2_map),
            ],
            out_specs=pl.BlockSpec((tm, d), row_map),
            scratch_shapes=[pltpu.VMEM((tm, d), F32)],
        ),
        out_shape=jax.ShapeDtypeStruct((rows, d), F32),
        compiler_params=_params(("arbitrary", "arbitrary")),
        name="expert_swiglu",
    )(tile_expert, n_valid, tile_half, x_sorted, w1, w3, w2)


def _combine_kernel(x_ref, y_ref, meta_ref, g_ref, mod_ref, o_ref):
    meta = meta_ref[0]
    y = meta[:, 2:3] * y_ref[0, 0] + meta[:, 3:4] * y_ref[1, 0]
    o_ref[0] = x_ref[0] + mod_ref[0, 5:6, :] * _rms(y, g_ref[3:4, :])


def _combine(x, y2, meta, g4, mods):
    bsz, length, d = x.shape
    tm = min(2 * ROW_TILE, length)
    return pl.pallas_call(
        _combine_kernel,
        grid=(bsz, length // tm),
        in_specs=[
            pl.BlockSpec((1, tm, d), lambda b, i: (b, i, 0)),
            pl.BlockSpec((2, 1, tm, d), lambda b, i: (0, b, i, 0)),
            pl.BlockSpec((1, tm, LANES), lambda b, i: (b, i, 0)),
            pl.BlockSpec((4, d), lambda b, i: (0, 0)),
            pl.BlockSpec((1, 6, d), lambda b, i: (b, 0, 0)),
        ],
        out_specs=pl.BlockSpec((1, tm, d), lambda b, i: (b, i, 0)),
        out_shape=jax.ShapeDtypeStruct((bsz, length, d), F32),
        compiler_params=_params(("parallel", "parallel")),
        name="moe_combine",
    )(x, y2, meta, g4, mods)


SC_CORES = 2
SC_SUBCORES = 16
SC_WORKERS = SC_CORES * SC_SUBCORES
SC_ROW_CHUNK = 32


def _sc_mesh():
    return plsc.VectorSubcoreMesh(core_axis_name="c", subcore_axis_name="s")


def _sc_dispatch(rows_in, pos2, n_out):
    tokens, width = rows_in.shape
    per_worker = tokens // SC_WORKERS
    chunk = SC_ROW_CHUNK
    n_pairs = per_worker // (2 * chunk)
    idx = pos2.reshape(2, SC_WORKERS, 2 * n_pairs, chunk)

    @functools.partial(
        pl.kernel, mesh=_sc_mesh(),
        out_type=jax.ShapeDtypeStruct((n_out, width), rows_in.dtype),
        scratch_types=[
            pltpu.VMEM((2 * n_pairs, chunk), jnp.int32),
            pltpu.VMEM((2 * n_pairs, chunk), jnp.int32),
            pltpu.VMEM((2, chunk, width), rows_in.dtype),
        ] + [pltpu.SemaphoreType.DMA] * 6,
        name="sc_dispatch",
    )
    def body(rows_hbm, idx_hbm, out_hbm, idx0_v, idx1_v, rows_v, rsem0, rsem1, wsem0, wsem1, wsem2, wsem3):
        wid = lax.axis_index("s") * SC_CORES + lax.axis_index("c")
        base = wid * per_worker
        rsem = (rsem0, rsem1)
        wsem = ((wsem0, wsem1), (wsem2, wsem3))
        pltpu.sync_copy(idx_hbm.at[0, wid], idx0_v)
        pltpu.sync_copy(idx_hbm.at[1, wid], idx1_v)

        @pl.loop(0, n_pairs)
        def _(p):
            c0 = 2 * p
            reads = [pltpu.async_copy(rows_hbm.at[pl.ds(base + (c0 + s) * chunk, chunk)], rows_v.at[s], rsem[s])
                     for s in range(2)]
            writes = []
            for s in range(2):
                reads[s].wait()
                writes.append(pltpu.async_copy(rows_v.at[s], out_hbm.at[idx0_v.at[c0 + s]], wsem[s][0]))
                writes.append(pltpu.async_copy(rows_v.at[s], out_hbm.at[idx1_v.at[c0 + s]], wsem[s][1]))
            for w in writes:
                w.wait()

    return body(rows_in, idx)


def _sc_gather(table, idx):
    n = idx.shape[0]
    width = table.shape[1]
    chunk = SC_ROW_CHUNK
    per_worker = n // SC_WORKERS
    n_pairs = per_worker // (2 * chunk)
    idx3 = idx.reshape(SC_WORKERS, 2 * n_pairs, chunk)

    @functools.partial(
        pl.kernel, mesh=_sc_mesh(),
        out_type=jax.ShapeDtypeStruct((n, width), table.dtype),
        scratch_types=[
            pltpu.VMEM((2 * n_pairs, chunk), jnp.int32),
            pltpu.VMEM((2, chunk, width), table.dtype),
            pltpu.SemaphoreType.DMA, pltpu.SemaphoreType.DMA,
            pltpu.SemaphoreType.DMA, pltpu.SemaphoreType.DMA,
        ],
        name="sc_gather",
    )
    def body(table_hbm, idx_hbm, out_hbm, idx_v, rows_v, gsem0, gsem1, wsem0, wsem1):
        wid = lax.axis_index("s") * SC_CORES + lax.axis_index("c")
        base = wid * per_worker
        gsem = (gsem0, gsem1)
        wsem = (wsem0, wsem1)
        pltpu.sync_copy(idx_hbm.at[wid], idx_v)

        @pl.loop(0, n_pairs)
        def _(p):
            c0 = 2 * p
            gathers = [pltpu.async_copy(table_hbm.at[idx_v.at[c0 + s]], rows_v.at[s], gsem[s]) for s in range(2)]
            writes = []
            for s in range(2):
                gathers[s].wait()
                rows_out = out_hbm.at[pl.ds(base + (c0 + s) * chunk, chunk)]
                writes.append(pltpu.async_copy(rows_v.at[s], rows_out, wsem[s]))
            for s in range(2):
                writes[s].wait()

    return body(table, idx3)


def _moe(x, mods, g4, w_router, w1, w3, w2):
    bsz, length, d = x.shape
    tokens = bsz * length
    tm = EXPERT_ROW_TILE
    xn, meta, meta_t, counts = _router(x, mods, g4, w_router)
    experts = meta_t[0:2].astype(jnp.int32)
    ranks = meta_t[4:6].astype(jnp.int32)
    cnt = counts[0, 0:N_EXPERTS].astype(jnp.int32)
    padded = ((cnt + tm - 1) // tm) * tm
    ends = jnp.cumsum(padded)
    starts = ends - padded
    group_start = jnp.zeros_like(experts)
    for e in range(N_EXPERTS):
        group_start = jnp.where(experts == e, starts[e], group_start)
    pos = (group_start + ranks).reshape(2 * tokens)
    rows = 2 * tokens + N_EXPERTS * tm
    n_tiles = rows // tm
    n_valid = (ends[-1] // tm).astype(jnp.int32).reshape(1)
    tile_start = jnp.arange(n_tiles, dtype=jnp.int32) * tm
    tile_expert = jnp.minimum(
        jnp.sum((tile_start[:, None] >= ends[None, :]).astype(jnp.int32), axis=1), N_EXPERTS - 1)
    own = (tile_expert[:, None] == jnp.arange(N_EXPERTS)[None, :]).astype(jnp.int32)
    tile_tokens = jnp.sum(own * (starts + cnt)[None, :], axis=1) - tile_start
    tile_half = (tile_tokens <= tm // 2).astype(jnp.int32)
    x_sorted = _sc_dispatch(xn.reshape(tokens, d), pos.reshape(2, tokens), rows)
    y_sorted = _experts(x_sorted, tile_expert, n_valid, tile_half, w1, w3, w2)
    y2 = _sc_gather(y_sorted, pos).reshape(2, bsz, length, d)
    return _combine(x, y2, meta, g4, mods)


def _grid_pos_embed(rows, d):
    row = np.repeat(np.arange(rows, dtype=np.float64), GRID_W)
    col = np.tile(np.arange(GRID_W, dtype=np.float64), rows)
    n_freq = d // 4
    omega = POS_THETA ** (-np.arange(n_freq, dtype=np.float64) / n_freq)
    ang_r = row[:, None] * omega
    ang_c = col[:, None] * omega
    return jnp.asarray(np.concatenate([np.sin(ang_r), np.cos(ang_r), np.sin(ang_c), np.cos(ang_c)], axis=-1),
                       dtype=F32)


def kernel(x, c, ctx, c_ctx, ada_w, ada_b, norm_g, ab_w_in, rg_conv_w, rg_conv_b, rg_w_r, rg_b_r, rg_w_i, rg_b_i, rg_lambda, cv_dw_w, cv_dw_b, cv_ln_g, cv_ln_b, ab_w_out, ffn_w1, ffn_w3, ffn_w2, cd_w_in, sg_ln_g, sg_ln_b, sg_w_s, sg_b_s, cd_w_out, moe_router, moe_w1, moe_w3, moe_w2):
    bsz, length, d = x.shape
    assert ada_w.shape[0] == 2, "two layers: one even (RG-LRU | Conformer), one odd (gMLP | Fourier)"
    pos = _grid_pos_embed(length // GRID_W, d)

    cpad = jnp.concatenate([c, c_ctx[None, :], jnp.zeros((16 - bsz - 1, d), F32)], axis=0)
    m = _ada(cpad, ada_w, ada_b)
    mods = m[:, :bsz].reshape(2, bsz, 6, d)
    mods_ctx = m[:, bsz:bsz + 1].reshape(2, 1, 6, d)

    mods_t = mods[0].transpose(1, 0, 2)
    mods_ctx_t = jnp.broadcast_to(mods_ctx[0].reshape(6, 1, d), (6, bsz, d))
    pos3 = pos[:, None, :]
    w_in = ab_w_in[0]
    xa3, gg3, glu3, x0_t, ffn1b, ffn3b, ffn2b = _inproj_t(
        x, pos3, mods_t, norm_g[0], w_in, rg_conv_w[0], rg_conv_b[0], True, riders=(ffn_w1[0], ffn_w3[0], ffn_w2[0]))
    (xc3,) = _inproj_t(ctx, None, mods_ctx_t, norm_g[0], w_in[:, 0:A_WIDTH], rg_conv_w[0], rg_conv_b[0], False)
    wg, bg, sp = _gate_weights(rg_w_r[0], rg_b_r[0], rg_w_i[0], rg_b_i[0], rg_lambda[0])
    n_exp, _, dff_e = moe_w1[0].shape
    rec_hi, rec_lo, w2b = _scan_t(xa3, xc3, gg3, wg, bg, sp, riders=(moe_w2[0].reshape(n_exp * dff_e, d),))
    (u3,) = _conv_t(glu3, cv_dw_w[0], cv_dw_b[0], cv_ln_g[0], cv_ln_b[0])
    x1_t = _outproj_t(rec_lo, rec_hi, u3, x0_t, ab_w_out[0], norm_g[0], mods_t)
    x2, w1b, w3b = _ffn_t(x1_t, mods_t, norm_g[0], ffn1b, ffn3b, ffn2b,
                          riders=(moe_w1[0].reshape(n_exp * d, dff_e), moe_w3[0].reshape(n_exp * d, dff_e)))

    u, v, f = _inproj_cd(x2, mods[1], norm_g[1], cd_w_in[0])
    fo = _fourier(f)
    x3 = _gate_out(u, v, fo, x2, sg_ln_g[0], sg_ln_b[0], sg_w_s[0], sg_b_s[0], cd_w_out[0],
                   norm_g[1], mods[1])
    return _moe(x3, mods[1], norm_g[1], moe_router[0], w1b.reshape(n_exp, d, dff_e), w3b.reshape(n_exp, d, dff_e),
                w2b.reshape(n_exp, dff_e, d))
```

```python
import functools

import numpy as np
import jax
import jax.numpy as jnp
from jax import lax
from jax.experimental import pallas as pl
from jax.experimental.pallas import tpu as pltpu
from jax.experimental.pallas import tpu_sc as plsc

F32 = jnp.float32
BF16 = jnp.bfloat16

GRID_W = 64
POS_THETA = 10000.0
NORM_EPS = 1e-6
LN_EPS = 1e-5
LRU_C = 8.0

A_WIDTH = 512
A_HEADS = 8
A_CONV_W = 4
B_WIDTH = 512
B_CONV_W = 31
C_WIDTH = 512
C_HEADS = 8
CHUNK = 128
D_WIDTH = 512
D_GROUPS = 4
D_GROUP_DIM = D_WIDTH // D_GROUPS
N_EXPERTS = 8

LANES = 128
MXU_WIDTH = 256
ROW_TILE = 512
EXPERT_ROW_TILE = 512
VMEM_LIMIT = 56 * 2 ** 20


def _params(sem):
    return pltpu.CompilerParams(dimension_semantics=sem, vmem_limit_bytes=VMEM_LIMIT)


def _rms(x, g):
    return x * lax.rsqrt(jnp.mean(x * x, axis=-1, keepdims=True) + NORM_EPS) * g


def _bdot(a, b):
    return jnp.dot(a, b, preferred_element_type=F32)


def _swiglu_halves(x, w1_ref, w3_ref, w2_ref, lead=()):
    n = w1_ref.shape[-1]
    half = -(-(n // 2) // MXU_WIDTH) * MXU_WIDTH
    y = None
    for lo, hi in ((0, half), (half, n)):
        a = _bdot(x, w1_ref[lead + (slice(None), slice(lo, hi))])
        b = _bdot(x, w3_ref[lead + (slice(None), slice(lo, hi))])
        t = (jax.nn.silu(a) * b).astype(BF16)
        part = _bdot(t, w2_ref[lead + (slice(lo, hi), slice(None))])
        y = part if y is None else y + part
    return y


def _ada_kernel(c_ref, w_ref, b_ref, o_ref):
    s = jax.nn.silu(c_ref[...])
    o_ref[0] = _bdot(s.astype(BF16), w_ref[0].astype(BF16)) + b_ref[0]


def _ada(cpad, ada_w, ada_b):
    nl, d, n6 = ada_w.shape
    rows = cpad.shape[0]
    tn = n6 // 4
    return pl.pallas_call(
        _ada_kernel,
        grid=(nl, n6 // tn),
        in_specs=[
            pl.BlockSpec((rows, d), lambda l, j: (0, 0)),
            pl.BlockSpec((1, d, tn), lambda l, j: (l, 0, j)),
            pl.BlockSpec((1, 1, tn), lambda l, j: (l, 0, j)),
        ],
        out_specs=pl.BlockSpec((1, rows, tn), lambda l, j: (l, 0, j)),
        out_shape=jax.ShapeDtypeStruct((nl, rows, n6), F32),
        compiler_params=_params(("arbitrary", "arbitrary")),
        name="ada_terms",
    )(cpad, ada_w, ada_b.reshape(nl, 1, n6))


TIME_TILE = 128
SCAN_TIME_TILE = 128
SCAN_LANES = 256
CONV_TIME_TILE = 128
CONV_HALO = 16


def _time_major(ref):
    return jnp.swapaxes(ref[...], 0, 1)


def _cast_riders(cast_in, cast_out):
    for src, dst in zip(cast_in, cast_out):
        dst[...] = src[...].astype(BF16)


def _rider_specs(arrays, n_steps, step_of):
    in_specs, out_specs, out_shapes = [], [], []
    for arr in arrays:
        rows, cols = arr.shape
        block = (rows // n_steps, cols)
        in_specs.append(pl.BlockSpec(block, lambda *idx: (step_of(*idx), 0)))
        out_specs.append(pl.BlockSpec(block, lambda *idx: (step_of(*idx), 0)))
        out_shapes.append(jax.ShapeDtypeStruct(arr.shape, BF16))
    return in_specs, out_specs, out_shapes


def _inproj_t_kernel(*refs, tt, add_pos, branches, n_cast):
    if n_cast:
        n_in = len(refs) - n_cast - (4 if branches else 1) - n_cast
        _cast_riders(refs[n_in:n_in + n_cast], refs[len(refs) - n_cast:])
        refs = refs[:n_in] + refs[n_in + n_cast:len(refs) - n_cast]
    if add_pos:
        x_ref, xp_ref, xn_ref, pos_ref, posp_ref, posn_ref, mod_ref, g_ref, w_ref, cw_ref, cb_ref, *outs = refs
        xall = jnp.concatenate(
            [_time_major(xp_ref)[7:8] + posp_ref[...], _time_major(x_ref) + pos_ref[...],
             _time_major(xn_ref)[0:2] + posn_ref[...]], axis=0)
    else:
        x_ref, xp_ref, xn_ref, mod_ref, g_ref, w_ref, cw_ref, cb_ref, *outs = refs
        xall = jnp.concatenate([_time_major(xp_ref)[7:8], _time_major(x_ref), _time_major(xn_ref)[0:2]], axis=0)
    i = pl.program_id(0)
    last = pl.num_programs(0) - 1
    steps, bsz, d = xall.shape
    h = _rms(xall, g_ref[0:1, :]) * (1.0 + mod_ref[1]) + mod_ref[0]
    z = _bdot(h.reshape(steps * bsz, d).astype(BF16), w_ref[...].astype(BF16))
    z3 = z.reshape(steps, bsz, z.shape[-1])
    t_idx = lax.broadcasted_iota(jnp.int32, (steps, bsz, A_WIDTH), 0)
    inside = jnp.logical_and(jnp.logical_or(t_idx >= 1, i > 0), jnp.logical_or(t_idx <= tt, i < last))
    xa = jnp.where(inside, z3[:, :, 0:A_WIDTH], 0.0)
    xconv = cb_ref[...] + cw_ref[0:1, :] * xa[0:tt]
    for k in range(1, A_CONV_W):
        xconv = xconv + cw_ref[k:k + 1, :] * xa[k:k + tt]
    outs[0][...] = xconv
    if branches:
        outs[1][...] = jax.nn.gelu(z3[1:tt + 1, :, A_WIDTH:2 * A_WIDTH])
        vb = z3[1:tt + 1, :, 2 * A_WIDTH:2 * A_WIDTH + B_WIDTH]
        gb = z3[1:tt + 1, :, 2 * A_WIDTH + B_WIDTH:]
        outs[2][...] = vb * jax.nn.sigmoid(gb)
        outs[3][...] = xall[1:tt + 1]


def _inproj_t(x, pos3, mods_t, g4, w, conv_w, conv_b, branches, riders=()):
    bsz, length, d = x.shape
    n = w.shape[1]
    tt = min(TIME_TILE, length)
    add_pos = pos3 is not None
    n_out = 3 if branches else 1
    prev_map = lambda i: (jnp.maximum(i * tt - 1, 0), 0, 0)
    next_map = lambda i: (jnp.minimum((i + 1) * (tt // 2), length // 2 - 1), 0, 0)
    in_specs = [
        pl.BlockSpec((bsz, tt, d), lambda i: (0, i, 0)),
        pl.BlockSpec((bsz, 8, d), lambda i: (0, jnp.maximum(i * (tt // 8) - 1, 0), 0)),
        pl.BlockSpec((bsz, 8, d), lambda i: (0, jnp.minimum((i + 1) * (tt // 8), length // 8 - 1), 0)),
    ]
    args = [x, x, x]
    if add_pos:
        in_specs += [
            pl.BlockSpec((tt, 1, d), lambda i: (i, 0, 0)),
            pl.BlockSpec((1, 1, d), prev_map),
            pl.BlockSpec((2, 1, d), next_map),
        ]
        args += [pos3, pos3, pos3]
    in_specs += [
        pl.BlockSpec((6, bsz, d), lambda i: (0, 0, 0)),
        pl.BlockSpec((4, d), lambda i: (0, 0)),
        pl.BlockSpec((d, n), lambda i: (0, 0)),
        pl.BlockSpec((A_CONV_W, A_WIDTH), lambda i: (0, 0)),
        pl.BlockSpec((1, A_WIDTH), lambda i: (0, 0)),
    ]
    args += [mods_t, g4, w, conv_w, conv_b.reshape(1, A_WIDTH)]
    r_in, r_out, r_shapes = _rider_specs(riders, length // tt, lambda i: i)
    out_specs = [pl.BlockSpec((tt, bsz, A_WIDTH), lambda i: (i, 0, 0))] * n_out
    out_shapes = [jax.ShapeDtypeStruct((length, bsz, A_WIDTH), F32)] * n_out
    if branches:
        out_specs.append(pl.BlockSpec((tt, bsz, d), lambda i: (i, 0, 0)))
        out_shapes.append(jax.ShapeDtypeStruct((length, bsz, d), F32))
    return pl.pallas_call(
        functools.partial(_inproj_t_kernel, tt=tt, add_pos=add_pos, branches=branches, n_cast=len(riders)),
        grid=(length // tt,),
        in_specs=in_specs + r_in,
        out_specs=out_specs + r_out,
        out_shape=out_shapes + r_shapes,
        compiler_params=_params(("parallel",)),
        name="inproj_ab" if branches else "inproj_ctx",
    )(*args, *riders)


def _scan_t_kernel(*refs, tb, nb, ctx_len, n_cast):
    xf_ref, xb_ref, ggf_ref, ggb_ref, xc_ref, wg_ref, bg_ref, sp_ref = refs[:8]
    cast_in = refs[8:8 + n_cast]
    hi_ref, lo_ref = refs[8 + n_cast:10 + n_cast]
    cast_out = refs[10 + n_cast:10 + 2 * n_cast]
    a_ref, b_ref, hs_ref, h_ref, state_ref = refs[10 + 2 * n_cast:]
    _cast_riders(cast_in, cast_out)
    k = pl.program_id(1)
    half = nb // 2
    bsz, lanes = xf_ref.shape[1], xf_ref.shape[2]
    chunk = 128

    def coefficients(x_ref, n, direction):
        def body(c, _):
            t0 = pl.multiple_of(c * chunk, chunk)
            x = x_ref[pl.ds(t0, chunk)].reshape(chunk * bsz, lanes)
            th = jnp.tanh(_bdot(x.astype(BF16), wg_ref[0, direction]) + bg_ref[0, direction])
            i = 0.5 * th[:, lanes:] + 0.5
            c = (-0.5 * LRU_C) * sp_ref[0, direction:direction + 1, :]
            log_a = c * th[:, 0:lanes] + c
            a = jnp.exp(log_a)
            one_minus_a2 = -jnp.tanh(log_a) * (a * a + 1.0)
            a_ref[direction, pl.ds(t0, chunk)] = a.reshape(chunk, bsz, lanes)
            b_ref[direction, pl.ds(t0, chunk)] = (jnp.sqrt(one_minus_a2) * (i * x)).reshape(chunk, bsz, lanes)
            return 0
        lax.fori_loop(0, n // chunk, body, 0)

    def sweep(n, store):
        def body(t, carry):
            hf, hb = carry
            tr = n - 1 - t
            hf = a_ref[0, pl.ds(t, 1)][0] * hf + b_ref[0, pl.ds(t, 1)][0]
            hb = a_ref[1, pl.ds(tr, 1)][0] * hb + b_ref[1, pl.ds(tr, 1)][0]
            if store:
                hs_ref[0, pl.ds(t, 1)] = hf[None]
                hs_ref[1, pl.ds(tr, 1)] = hb[None]
            return hf, hb
        hf, hb = lax.fori_loop(0, n, body, (state_ref[0], state_ref[1]), unroll=8)
        state_ref[0] = hf
        state_ref[1] = hb

    @pl.when(k == 0)
    def _():
        state_ref[...] = jnp.zeros_like(state_ref)
        coefficients(xc_ref, ctx_len, 0)
        coefficients(xc_ref, ctx_len, 1)
        sweep(ctx_len, False)

    coefficients(xf_ref, tb, 0)
    coefficients(xb_ref, tb, 1)
    sweep(tb, True)
    m = nb - 1 - k

    @pl.when(k < half)
    def _():
        h_ref[k] = hs_ref[0]
        h_ref[m] = hs_ref[1]

    @pl.when(k >= half)
    def _():
        hi_ref[...] = (hs_ref[0] + h_ref[k]) * ggf_ref[...]
        lo_ref[...] = (h_ref[m] + hs_ref[1]) * ggb_ref[...]


def _scan_t(xa3, xc3, gg3, wg, bg, sp, riders=()):
    length, bsz, width = xa3.shape
    ctx_len = xc3.shape[0]
    tb = min(SCAN_TIME_TILE, length // 2)
    nb = length // tb
    half = nb // 2
    lanes = SCAN_LANES
    blk = (tb, bsz, lanes)
    r_in, r_out, r_shapes = _rider_specs(riders, (width // lanes) * nb, lambda g, k: g * nb + k)
    return pl.pallas_call(
        functools.partial(_scan_t_kernel, tb=tb, nb=nb, ctx_len=ctx_len, n_cast=len(riders)),
        grid=(width // lanes, nb),
        in_specs=[
            pl.BlockSpec(blk, lambda g, k: (k, 0, g)),
            pl.BlockSpec(blk, lambda g, k: (nb - 1 - k, 0, g)),
            pl.BlockSpec(blk, lambda g, k: (jnp.maximum(k, half), 0, g)),
            pl.BlockSpec(blk, lambda g, k: (jnp.minimum(nb - 1 - k, half - 1), 0, g)),
            pl.BlockSpec((ctx_len, bsz, lanes), lambda g, k: (0, 0, g)),
            pl.BlockSpec((1, 2, lanes, 2 * lanes), lambda g, k: (g, 0, 0, 0)),
            pl.BlockSpec((1, 2, 1, 2 * lanes), lambda g, k: (g, 0, 0, 0)),
            pl.BlockSpec((1, 2, lanes), lambda g, k: (g, 0, 0)),
        ] + r_in,
        out_specs=[
            pl.BlockSpec(blk, lambda g, k: (jnp.maximum(k - half, 0), 0, g)),
            pl.BlockSpec(blk, lambda g, k: (jnp.minimum(nb - 1 - k, half - 1), 0, g)),
        ] + r_out,
        out_shape=[jax.ShapeDtypeStruct((length // 2, bsz, width), F32)] * 2 + r_shapes,
        scratch_shapes=[
            pltpu.VMEM((2, max(tb, ctx_len), bsz, lanes), F32),
            pltpu.VMEM((2, max(tb, ctx_len), bsz, lanes), F32),
            pltpu.VMEM((2, tb, bsz, lanes), F32),
            pltpu.VMEM((nb, tb, bsz, lanes), F32),
            pltpu.VMEM((2, bsz, lanes), F32),
        ],
        compiler_params=_params(("parallel", "arbitrary")),
        name="rglru_scan",
    )(xa3, xa3, gg3, gg3, xc3, wg, bg, sp, *riders)


def _gate_weights(w_r, b_r, w_i, b_i, lam):
    hd = A_WIDTH // A_HEADS
    heads_per_group = SCAN_LANES // hd
    ngroups = A_WIDTH // SCAN_LANES

    def blockdiag(w):
        w = w.reshape(ngroups, heads_per_group, hd, hd)
        eye = jnp.eye(heads_per_group, dtype=w.dtype)
        return jnp.einsum("ghde,hk->ghdke", w, eye).reshape(ngroups, SCAN_LANES, SCAN_LANES)

    wg = jnp.stack([jnp.concatenate([blockdiag(w_r[d]), blockdiag(w_i[d])], axis=-1) for d in range(2)], axis=1)
    bg = jnp.stack([jnp.concatenate([b_r[d].reshape(ngroups, 1, SCAN_LANES), b_i[d].reshape(ngroups, 1, SCAN_LANES)],
                                    axis=-1) for d in range(2)], axis=1)
    sp = jax.nn.softplus(-lam.astype(F32)).reshape(2, ngroups, SCAN_LANES).transpose(1, 0, 2)
    return (0.5 * wg).astype(BF16), 0.5 * bg, sp


def _conv_t_kernel(*refs, tb, n_cast):
    x_ref, xp_ref, xn_ref, w_ref, b_ref, lng_ref, lnb_ref = refs[:7]
    cast_in = refs[7:7 + n_cast]
    o_ref = refs[7 + n_cast]
    cast_out = refs[8 + n_cast:8 + 2 * n_cast]
    stage_ref = refs[-1]
    _cast_riders(cast_in, cast_out)
    i = pl.program_id(0)
    last = pl.num_programs(0) - 1
    halo = CONV_HALO
    stage_ref[0:halo] = jnp.where(i > 0, xp_ref[...], 0.0)
    stage_ref[halo:halo + tb] = x_ref[...]
    stage_ref[halo + tb:2 * halo + tb] = jnp.where(i < last, xn_ref[...], 0.0)
    sub = 8

    def body(r, _):
        t0 = pl.multiple_of(r * sub, sub)
        def tap(k, acc):
            return acc + w_ref[k] * stage_ref[pl.ds(t0 + 1 + k, sub)]
        acc = lax.fori_loop(0, B_CONV_W + 1, tap, jnp.broadcast_to(b_ref[...], (sub,) + b_ref.shape), unroll=8)
        o_ref[pl.ds(t0, sub)] = acc
        return 0
    lax.fori_loop(0, tb // sub, body, 0)

    norm_rows = 64

    def norm_body(r, _):
        t0 = pl.multiple_of(r * norm_rows, norm_rows)
        acc = o_ref[pl.ds(t0, norm_rows)]
        mu = jnp.mean(acc, axis=-1, keepdims=True)
        cen = acc - mu
        var = jnp.mean(cen * cen, axis=-1, keepdims=True)
        y = cen * lax.rsqrt(var + LN_EPS) * lng_ref[...] + lnb_ref[...]
        o_ref[pl.ds(t0, norm_rows)] = jax.nn.silu(y)
        return 0
    lax.fori_loop(0, tb // norm_rows, norm_body, 0)


def _conv_t(glu3, dw_w, dw_b, ln_g, ln_b, riders=()):
    length, bsz, width = glu3.shape
    tb = min(CONV_TIME_TILE, length)
    halo = CONV_HALO
    taps = dw_w.shape[0]
    w8 = jnp.broadcast_to(jnp.concatenate([dw_w, jnp.zeros((32 - taps, width), F32)], axis=0)[:, None, :],
                          (32, bsz, width))
    b8 = jnp.broadcast_to(dw_b[None, :], (bsz, width))
    r_in, r_out, r_shapes = _rider_specs(riders, length // tb, lambda i: i)
    return pl.pallas_call(
        functools.partial(_conv_t_kernel, tb=tb, n_cast=len(riders)),
        grid=(length // tb,),
        in_specs=[
            pl.BlockSpec((tb, bsz, width), lambda i: (i, 0, 0)),
            pl.BlockSpec((halo, bsz, width), lambda i: (jnp.maximum(i * (tb // halo) - 1, 0), 0, 0)),
            pl.BlockSpec((halo, bsz, width), lambda i: (jnp.minimum((i + 1) * (tb // halo), length // halo - 1), 0, 0)),
            pl.BlockSpec((32, bsz, width), lambda i: (0, 0, 0)),
            pl.BlockSpec((bsz, width), lambda i: (0, 0)),
            pl.BlockSpec((1, width), lambda i: (0, 0)),
            pl.BlockSpec((1, width), lambda i: (0, 0)),
        ] + r_in,
        out_specs=[pl.BlockSpec((tb, bsz, width), lambda i: (i, 0, 0))] + r_out,
        out_shape=[jax.ShapeDtypeStruct((length, bsz, width), F32)] + r_shapes,
        scratch_shapes=[pltpu.VMEM((tb + 2 * halo, bsz, width), F32)],
        compiler_params=_params(("parallel",)),
        name="conformer_conv",
    )(glu3, glu3, glu3, w8, b8, ln_g.reshape(1, -1), ln_b.reshape(1, -1), *riders)


def _outproj_t_kernel(lo_ref, hi_ref, u_ref, x_ref, wout_ref, g_ref, mod_ref, o_ref, *, half_steps):
    i = pl.program_id(0)
    tt, bsz, d = x_ref.shape
    rec = jnp.where(i < half_steps, lo_ref[...], hi_ref[...])
    rows = tt * bsz
    y = (_bdot(rec.reshape(rows, A_WIDTH).astype(BF16), wout_ref[0:A_WIDTH, :].astype(BF16))
         + _bdot(u_ref[...].reshape(rows, B_WIDTH).astype(BF16), wout_ref[A_WIDTH:, :].astype(BF16)))
    yn = _rms(y, g_ref[1:2, :]).reshape(tt, bsz, d)
    o_ref[...] = x_ref[...] + mod_ref[2] * yn


def _outproj_t(rec_lo, rec_hi, u3, x0_t, w_out, g4, mods_t):
    length, bsz, d = x0_t.shape
    tt = min(TIME_TILE, length // 2)
    half_steps = (length // 2) // tt
    return pl.pallas_call(
        functools.partial(_outproj_t_kernel, half_steps=half_steps),
        grid=(length // tt,),
        in_specs=[
            pl.BlockSpec((tt, bsz, A_WIDTH), lambda i: (jnp.minimum(i, half_steps - 1), 0, 0)),
            pl.BlockSpec((tt, bsz, A_WIDTH), lambda i: (jnp.maximum(i - half_steps, 0), 0, 0)),
            pl.BlockSpec((tt, bsz, B_WIDTH), lambda i: (i, 0, 0)),
            pl.BlockSpec((tt, bsz, d), lambda i: (i, 0, 0)),
            pl.BlockSpec((A_WIDTH + B_WIDTH, d), lambda i: (0, 0)),
            pl.BlockSpec((4, d), lambda i: (0, 0)),
            pl.BlockSpec((6, bsz, d), lambda i: (0, 0, 0)),
        ],
        out_specs=pl.BlockSpec((tt, bsz, d), lambda i: (i, 0, 0)),
        out_shape=jax.ShapeDtypeStruct((length, bsz, d), F32),
        compiler_params=_params(("parallel",)),
        name="mixer_ab_outproj",
    )(rec_lo, rec_hi, u3, x0_t, w_out, g4, mods_t)


def _ffn_t_kernel(x_ref, mod_ref, g_ref, w1_ref, w3_ref, w2_ref, *rest, n_cast):
    _cast_riders(rest[:n_cast], rest[n_cast + 1:])
    o_ref = rest[n_cast]
    x = x_ref[...]
    tt, bsz, d = x.shape
    h = (_rms(x, g_ref[2:3, :]) * (1.0 + mod_ref[4]) + mod_ref[3]).reshape(tt * bsz, d).astype(BF16)
    y = _swiglu_halves(h, w1_ref, w3_ref, w2_ref)
    o_ref[...] = jnp.swapaxes(x + mod_ref[5] * _rms(y, g_ref[3:4, :]).reshape(tt, bsz, d), 0, 1)


def _ffn_t(x3, mods_t, g4, w1, w3, w2, riders=()):
    length, bsz, d = x3.shape
    dff = w1.shape[1]
    tt = min(TIME_TILE // 2, length)
    r_in, r_out, r_shapes = _rider_specs(riders, length // tt, lambda i: i)
    return pl.pallas_call(
        functools.partial(_ffn_t_kernel, n_cast=len(riders)),
        grid=(length // tt,),
        in_specs=[
            pl.BlockSpec((tt, bsz, d), lambda i: (i, 0, 0)),
            pl.BlockSpec((6, bsz, d), lambda i: (0, 0, 0)),
            pl.BlockSpec((4, d), lambda i: (0, 0)),
            pl.BlockSpec((d, dff), lambda i: (0, 0), pipeline_mode=pl.Buffered(1)),
            pl.BlockSpec((d, dff), lambda i: (0, 0), pipeline_mode=pl.Buffered(1)),
            pl.BlockSpec((dff, d), lambda i: (0, 0), pipeline_mode=pl.Buffered(1)),
        ] + r_in,
        out_specs=[pl.BlockSpec((bsz, tt, d), lambda i: (0, i, 0))] + r_out,
        out_shape=[jax.ShapeDtypeStruct((bsz, length, d), F32)] + r_shapes,
        compiler_params=_params(("parallel",)),
        name="dense_swiglu",
    )(x3, mods_t, g4, w1, w3, w2, *riders)


def _inproj_cd_kernel(x_ref, mod_ref, g_ref, w_ref, u_ref, v_ref, f_ref):
    x = x_ref[0]
    h = _rms(x, g_ref[0:1, :]) * (1.0 + mod_ref[0, 1:2, :]) + mod_ref[0, 0:1, :]
    z = _bdot(h.astype(BF16), w_ref[...].astype(BF16))
    u_ref[0] = jax.nn.gelu(z[:, 0:C_WIDTH])
    v_ref[0] = jax.nn.gelu(z[:, C_WIDTH:2 * C_WIDTH])
    f_ref[0] = z[:, 2 * C_WIDTH:]


def _inproj_cd(x, mods, g4, w):
    bsz, length, d = x.shape
    n = w.shape[1]
    tm = min(2 * ROW_TILE, length)
    return pl.pallas_call(
        _inproj_cd_kernel,
        grid=(bsz, length // tm),
        in_specs=[
            pl.BlockSpec((1, tm, d), lambda b, i: (b, i, 0)),
            pl.BlockSpec((1, 6, d), lambda b, i: (b, 0, 0)),
            pl.BlockSpec((4, d), lambda b, i: (0, 0)),
            pl.BlockSpec((d, n), lambda b, i: (0, 0)),
        ],
        out_specs=[pl.BlockSpec((1, tm, C_WIDTH), lambda b, i: (b, i, 0))] * 3,
        out_shape=[jax.ShapeDtypeStruct((bsz, length, C_WIDTH), F32)] * 3,
        compiler_params=_params(("parallel", "parallel")),
        name="inproj_cd",
    )(x, mods, g4, w)


def _fourier_kernel(f_ref, dmat_ref, lmat_ref, o_ref, z_ref, *, length, scale):
    j = pl.program_id(1)
    rows_per_step = 512 if length % 512 == 0 else length

    @pl.when(j == 0)
    def _():
        def body(c, _):
            rows = pl.ds(pl.multiple_of(c * rows_per_step, rows_per_step), rows_per_step)
            fb = f_ref[0, rows, :].astype(BF16)
            for g in range(D_GROUPS):
                cs = _bdot(fb[:, g * D_GROUP_DIM:(g + 1) * D_GROUP_DIM], dmat_ref[...])
                z_ref[0, rows, g * D_GROUP_DIM:(g + 1) * D_GROUP_DIM] = cs[:, 0:D_GROUP_DIM].astype(BF16)
                z_ref[1, rows, g * D_GROUP_DIM:(g + 1) * D_GROUP_DIM] = cs[:, D_GROUP_DIM:].astype(BF16)
            return 0
        lax.fori_loop(0, length // rows_per_step, body, 0)

    out = _bdot(lmat_ref[0], z_ref[0]) + _bdot(lmat_ref[1], z_ref[1])
    o_ref[0] = (out * scale).astype(BF16)


def _dft_matrices(length):
    k = np.arange(length, dtype=np.int64)
    ang_l = 2.0 * np.pi * ((k[:, None] * k[None, :]) % length).astype(np.float64) / length
    lmat = np.stack([np.cos(ang_l), -np.sin(ang_l)], axis=0)
    d = np.arange(D_GROUP_DIM, dtype=np.int64)
    ang_d = 2.0 * np.pi * ((d[:, None] * d[None, :]) % D_GROUP_DIM).astype(np.float64) / D_GROUP_DIM
    dmat = np.concatenate([np.cos(ang_d), np.sin(ang_d)], axis=1)
    return jnp.asarray(lmat, dtype=F32).astype(BF16), jnp.asarray(dmat, dtype=F32).astype(BF16)


def _fourier(f):
    bsz, length, width = f.shape
    tm = min(4 * ROW_TILE, length)
    lmat, dmat = _dft_matrices(length)
    scale = float(1.0 / np.sqrt(length * D_GROUP_DIM))
    return pl.pallas_call(
        functools.partial(_fourier_kernel, length=length, scale=scale),
        grid=(bsz, length // tm),
        in_specs=[
            pl.BlockSpec((1, length, width), lambda b, i: (b, 0, 0)),
            pl.BlockSpec((D_GROUP_DIM, 2 * D_GROUP_DIM), lambda b, i: (0, 0)),
            pl.BlockSpec((2, tm, length), lambda b, i: (0, i, 0)),
        ],
        out_specs=pl.BlockSpec((1, tm, width), lambda b, i: (b, i, 0)),
        out_shape=jax.ShapeDtypeStruct((bsz, length, width), BF16),
        scratch_shapes=[pltpu.VMEM((2, length, width), BF16)],
        compiler_params=_params(("parallel", "arbitrary")),
        name="fourier_mix",
    )(f, dmat, lmat)


def _gate_out_kernel(u_ref, v_ref, fo_ref, x_ref, lng_ref, lnb_ref, wcat_ref, bs_ref, wout_ref, g_ref, mod_ref,
                     o_ref, vs_ref, gt_ref, *, tm):
    head_dim = C_WIDTH // C_HEADS
    head_of_lane = lax.broadcasted_iota(jnp.int32, (CHUNK, C_WIDTH), 1) // head_dim
    for ci in range(tm // CHUNK):
        rows = slice(ci * CHUNK, (ci + 1) * CHUNK)
        v = v_ref[0, rows, :]
        mu = jnp.mean(v, axis=-1, keepdims=True)
        cen = v - mu
        var = jnp.mean(cen * cen, axis=-1, keepdims=True)
        vn = (cen * lax.rsqrt(var + LN_EPS) * lng_ref[...] + lnb_ref[...]).astype(BF16)
        for h in range(C_HEADS):
            vs_ref[h * CHUNK:(h + 1) * CHUNK, :] = jnp.where(head_of_lane == h, vn, jnp.zeros_like(vn))
        mixed = _bdot(wcat_ref[...], vs_ref[...]) + bs_ref[...]
        gt_ref[rows, :] = (u_ref[0, rows, :] * mixed).astype(BF16)
    y = (_bdot(gt_ref[...], wout_ref[0:C_WIDTH, :].astype(BF16))
         + _bdot(fo_ref[0], wout_ref[C_WIDTH:, :].astype(BF16)))
    o_ref[0] = x_ref[0] + mod_ref[0, 2:3, :] * _rms(y, g_ref[1:2, :])


def _gate_out(u, v, fo, x, ln_g, ln_b, w_s, b_s, w_out, g4, mods):
    bsz, length, d = x.shape
    tm = min(2 * ROW_TILE, length)
    head_dim = C_WIDTH // C_HEADS
    wcat = w_s.transpose(1, 0, 2).reshape(CHUNK, C_HEADS * CHUNK).astype(BF16)
    bs_full = jnp.repeat(b_s.T, head_dim, axis=1)
    const = lambda b, i: (0, 0)
    return pl.pallas_call(
        functools.partial(_gate_out_kernel, tm=tm),
        grid=(bsz, length // tm),
        in_specs=[
            pl.BlockSpec((1, tm, C_WIDTH), lambda b, i: (b, i, 0)),
            pl.BlockSpec((1, tm, C_WIDTH), lambda b, i: (b, i, 0)),
            pl.BlockSpec((1, tm, D_WIDTH), lambda b, i: (b, i, 0)),
            pl.BlockSpec((1, tm, d), lambda b, i: (b, i, 0)),
            pl.BlockSpec((1, C_WIDTH), const),
            pl.BlockSpec((1, C_WIDTH), const),
            pl.BlockSpec((CHUNK, C_HEADS * CHUNK), const),
            pl.BlockSpec((CHUNK, C_WIDTH), const),
            pl.BlockSpec((C_WIDTH + D_WIDTH, d), const),
            pl.BlockSpec((4, d), const),
            pl.BlockSpec((1, 6, d), lambda b, i: (b, 0, 0)),
        ],
        out_specs=pl.BlockSpec((1, tm, d), lambda b, i: (b, i, 0)),
        out_shape=jax.ShapeDtypeStruct((bsz, length, d), F32),
        scratch_shapes=[
            pltpu.VMEM((C_HEADS * CHUNK, C_WIDTH), BF16),
            pltpu.VMEM((tm, C_WIDTH), BF16),
        ],
        compiler_params=_params(("parallel", "parallel")),
        name="gating_outproj",
    )(u, v, fo, x, ln_g.reshape(1, -1), ln_b.reshape(1, -1), wcat, bs_full, w_out, g4, mods)


def _router_kernel(x_ref, mod_ref, g_ref, wr_ref, tri_ref, xn_ref, meta_ref, metat_ref, cnt_ref, carry_ref, *, tm):
    first = jnp.logical_and(pl.program_id(0) == 0, pl.program_id(1) == 0)

    @pl.when(first)
    def _():
        carry_ref[...] = jnp.zeros_like(carry_ref)

    x = x_ref[0]
    h = _rms(x, g_ref[2:3, :]) * (1.0 + mod_ref[0, 4:5, :]) + mod_ref[0, 3:4, :]
    xn_ref[0] = h
    h_hi = h.astype(BF16)
    h_lo = (h - h_hi.astype(F32)).astype(BF16)
    logits = _bdot(jnp.concatenate([h_hi, h_hi, h_lo], axis=1), wr_ref[...])
    lane = lax.broadcasted_iota(jnp.int32, (tm, LANES), 1).astype(F32)
    neg = jnp.float32(-jnp.inf)
    lm = jnp.where(lane < N_EXPERTS, logits, neg)
    m1 = jnp.max(lm, axis=-1, keepdims=True)
    i1 = jnp.min(jnp.where(lm == m1, lane, float(LANES)), axis=-1, keepdims=True)
    lm2 = jnp.where(lane == i1, neg, lm)
    m2 = jnp.max(lm2, axis=-1, keepdims=True)
    i2 = jnp.min(jnp.where(lm2 == m2, lane, float(LANES)), axis=-1, keepdims=True)
    e = jnp.exp(m2 - m1)
    p1 = 1.0 / (1.0 + e)
    p2 = e / (1.0 + e)
    hot1 = (lane == i1).astype(F32)
    hot2 = (lane == i2).astype(F32)
    both = hot1 + hot2
    before = _bdot(tri_ref[...], both.astype(BF16)) + carry_ref[0:1, :]
    r1 = jnp.sum(before * hot1, axis=-1, keepdims=True)
    r2 = jnp.sum(before * hot2, axis=-1, keepdims=True)
    carry_ref[0:1, :] = carry_ref[0:1, :] + jnp.sum(both, axis=0, keepdims=True)
    cnt_ref[...] = carry_ref[...]
    vals = (i1, i2, p1, p2, r1, r2)
    meta = jnp.zeros((tm, LANES), F32)
    for k, val in enumerate(vals):
        meta = jnp.where(lane == k, val, meta)
    meta_ref[0] = meta
    metat_ref[...] = meta.T[0:8, :]


def _router(x, mods, g4, w_router):
    bsz, length, d = x.shape
    tm = min(2 * ROW_TILE, length)
    wr = jnp.concatenate([w_router, jnp.zeros((d, LANES - N_EXPERTS), F32)], axis=1)
    wr_hi = wr.astype(BF16)
    wr_lo = (wr - wr_hi.astype(F32)).astype(BF16)
    wr = jnp.concatenate([wr_hi, wr_lo, wr_hi], axis=0)
    tri = jnp.asarray(np.tril(np.ones((tm, tm), np.float32), -1), dtype=BF16)
    const = lambda b, i: (0, 0)
    return pl.pallas_call(
        functools.partial(_router_kernel, tm=tm),
        grid=(bsz, length // tm),
        in_specs=[
            pl.BlockSpec((1, tm, d), lambda b, i: (b, i, 0)),
            pl.BlockSpec((1, 6, d), lambda b, i: (b, 0, 0)),
            pl.BlockSpec((4, d), const),
            pl.BlockSpec((3 * d, LANES), const),
            pl.BlockSpec((tm, tm), const),
        ],
        out_specs=[
            pl.BlockSpec((1, tm, d), lambda b, i: (b, i, 0)),
            pl.BlockSpec((1, tm, LANES), lambda b, i: (b, i, 0)),
            pl.BlockSpec((8, tm), lambda b, i: (0, b * (length // tm) + i)),
            pl.BlockSpec((8, LANES), const),
        ],
        out_shape=[
            jax.ShapeDtypeStruct((bsz, length, d), F32),
            jax.ShapeDtypeStruct((bsz, length, LANES), F32),
            jax.ShapeDtypeStruct((8, bsz * length), F32),
            jax.ShapeDtypeStruct((8, LANES), F32),
        ],
        scratch_shapes=[pltpu.VMEM((8, LANES), F32)],
        compiler_params=_params(("arbitrary", "arbitrary")),
        name="moe_router",
    )(x, mods, g4, wr, tri)


def _expert_kernel(te_ref, nv_ref, th_ref, x_ref, w1_ref, w3_ref, w2_ref, o_ref, acc_ref):
    i = pl.program_id(0)
    j = pl.program_id(1)
    tm = x_ref.shape[0]

    def run(rows):
        y = _swiglu_halves(x_ref[0:rows, :].astype(BF16), w1_ref, w3_ref, w2_ref, lead=(0,))

        @pl.when(j == 0)
        def _():
            acc_ref[0:rows, :] = y

        @pl.when(j > 0)
        def _():
            acc_ref[0:rows, :] += y

        @pl.when(j == pl.num_programs(1) - 1)
        def _():
            o_ref[0:rows, :] = acc_ref[0:rows, :]

    valid = i < nv_ref[0]
    half_full = th_ref[i] == 1

    @pl.when(jnp.logical_and(valid, jnp.logical_not(half_full)))
    def _():
        run(tm)

    @pl.when(jnp.logical_and(valid, half_full))
    def _():
        run(tm // 2)


def _experts(x_sorted, tile_expert, n_valid, tile_half, w1, w3, w2):
    rows = x_sorted.shape[0]
    d = w1.shape[1]
    dff = w1.shape[2]
    tm = EXPERT_ROW_TILE
    n_ff = 2
    tf = dff // n_ff

    def row_map(i, j, te, nv, th):
        return (jnp.maximum(jnp.minimum(i, nv[0] - 1), 0), 0)

    def w13_map(i, j, te, nv, th):
        return (te[jnp.maximum(jnp.minimum(i, nv[0] - 1), 0)], 0, jnp.where(i < nv[0], j, n_ff - 1))

    def w2_map(i, j, te, nv, th):
        return (te[jnp.maximum(jnp.minimum(i, nv[0] - 1), 0)], jnp.where(i < nv[0], j, n_ff - 1), 0)

    return pl.pallas_call(
        _expert_kernel,
        grid_spec=pltpu.PrefetchScalarGridSpec(
            num_scalar_prefetch=3,
            grid=(rows // tm, n_ff),
            in_specs=[
                pl.BlockSpec((tm, d), row_map),
                pl.BlockSpec((1, d, tf), w13_map),
                pl.BlockSpec((1, d, tf), w13_map),
                pl.BlockSpec((1, tf, d), w2_map),
            ],
            out_specs=pl.BlockSpec((tm, d), row_map),
            scratch_shapes=[pltpu.VMEM((tm, d), F32)],
        ),
        out_shape=jax.ShapeDtypeStruct((rows, d), F32),
        compiler_params=_params(("arbitrary", "arbitrary")),
        name="expert_swiglu",
    )(tile_expert, n_valid, tile_half, x_sorted, w1, w3, w2)


def _combine_kernel(x_ref, y_ref, meta_ref, g_ref, mod_ref, o_ref):
    meta = meta_ref[0]
    y = meta[:, 2:3] * y_ref[0, 0] + meta[:, 3:4] * y_ref[1, 0]
    o_ref[0] = x_ref[0] + mod_ref[0, 5:6, :] * _rms(y, g_ref[3:4, :])


def _combine(x, y2, meta, g4, mods):
    bsz, length, d = x.shape
    tm = min(2 * ROW_TILE, length)
    return pl.pallas_call(
        _combine_kernel,
        grid=(bsz, length // tm),
        in_specs=[
            pl.BlockSpec((1, tm, d), lambda b, i: (b, i, 0)),
            pl.BlockSpec((2, 1, tm, d), lambda b, i: (0, b, i, 0)),
            pl.BlockSpec((1, tm, LANES), lambda b, i: (b, i, 0)),
            pl.BlockSpec((4, d), lambda b, i: (0, 0)),
            pl.BlockSpec((1, 6, d), lambda b, i: (b, 0, 0)),
        ],
        out_specs=pl.BlockSpec((1, tm, d), lambda b, i: (b, i, 0)),
        out_shape=jax.ShapeDtypeStruct((bsz, length, d), F32),
        compiler_params=_params(("parallel", "parallel")),
        name="moe_combine",
    )(x, y2, meta, g4, mods)


SC_CORES = 2
SC_SUBCORES = 16
SC_WORKERS = SC_CORES * SC_SUBCORES
SC_ROW_CHUNK = 32


def _sc_mesh():
    return plsc.VectorSubcoreMesh(core_axis_name="c", subcore_axis_name="s")


def _sc_dispatch(rows_in, pos2, n_out):
    tokens, width = rows_in.shape
    per_worker = tokens // SC_WORKERS
    chunk = SC_ROW_CHUNK
    n_pairs = per_worker // (2 * chunk)
    idx = pos2.reshape(2, SC_WORKERS, 2 * n_pairs, chunk)

    @functools.partial(
        pl.kernel, mesh=_sc_mesh(),
        out_type=jax.ShapeDtypeStruct((n_out, width), rows_in.dtype),
        scratch_types=[
            pltpu.VMEM((2 * n_pairs, chunk), jnp.int32),
            pltpu.VMEM((2 * n_pairs, chunk), jnp.int32),
            pltpu.VMEM((2, chunk, width), rows_in.dtype),
        ] + [pltpu.SemaphoreType.DMA] * 6,
        name="sc_dispatch",
    )
    def body(rows_hbm, idx_hbm, out_hbm, idx0_v, idx1_v, rows_v, rsem0, rsem1, wsem0, wsem1, wsem2, wsem3):
        wid = lax.axis_index("s") * SC_CORES + lax.axis_index("c")
        base = wid * per_worker
        rsem = (rsem0, rsem1)
        wsem = ((wsem0, wsem1), (wsem2, wsem3))
        pltpu.sync_copy(idx_hbm.at[0, wid], idx0_v)
        pltpu.sync_copy(idx_hbm.at[1, wid], idx1_v)

        @pl.loop(0, n_pairs)
        def _(p):
            c0 = 2 * p
            reads = [pltpu.async_copy(rows_hbm.at[pl.ds(base + (c0 + s) * chunk, chunk)], rows_v.at[s], rsem[s])
                     for s in range(2)]
            writes = []
            for s in range(2):
                reads[s].wait()
                writes.append(pltpu.async_copy(rows_v.at[s], out_hbm.at[idx0_v.at[c0 + s]], wsem[s][0]))
                writes.append(pltpu.async_copy(rows_v.at[s], out_hbm.at[idx1_v.at[c0 + s]], wsem[s][1]))
            for w in writes:
                w.wait()

    return body(rows_in, idx)


def _sc_gather(table, idx):
    n = idx.shape[0]
    width = table.shape[1]
    chunk = SC_ROW_CHUNK
    per_worker = n // SC_WORKERS
    n_pairs = per_worker // (2 * chunk)
    idx3 = idx.reshape(SC_WORKERS, 2 * n_pairs, chunk)

    @functools.partial(
        pl.kernel, mesh=_sc_mesh(),
        out_type=jax.ShapeDtypeStruct((n, width), table.dtype),
        scratch_types=[
            pltpu.VMEM((2 * n_pairs, chunk), jnp.int32),
            pltpu.VMEM((2, chunk, width), table.dtype),
            pltpu.SemaphoreType.DMA, pltpu.SemaphoreType.DMA,
            pltpu.SemaphoreType.DMA, pltpu.SemaphoreType.DMA,
        ],
        name="sc_gather",
    )
    def body(table_hbm, idx_hbm, out_hbm, idx_v, rows_v, gsem0, gsem1, wsem0, wsem1):
        wid = lax.axis_index("s") * SC_CORES + lax.axis_index("c")
        base = wid * per_worker
        gsem = (gsem0, gsem1)
        wsem = (wsem0, wsem1)
        pltpu.sync_copy(idx_hbm.at[wid], idx_v)

        @pl.loop(0, n_pairs)
        def _(p):
            c0 = 2 * p
            gathers = [pltpu.async_copy(table_hbm.at[idx_v.at[c0 + s]], rows_v.at[s], gsem[s]) for s in range(2)]
            writes = []
            for s in range(2):
                gathers[s].wait()
                rows_out = out_hbm.at[pl.ds(base + (c0 + s) * chunk, chunk)]
                writes.append(pltpu.async_copy(rows_v.at[s], rows_out, wsem[s]))
            for s in range(2):
                writes[s].wait()

    return body(table, idx3)


def _moe(x, mods, g4, w_router, w1, w3, w2):
    bsz, length, d = x.shape
    tokens = bsz * length
    tm = EXPERT_ROW_TILE
    xn, meta, meta_t, counts = _router(x, mods, g4, w_router)
    experts = meta_t[0:2].astype(jnp.int32)
    ranks = meta_t[4:6].astype(jnp.int32)
    cnt = counts[0, 0:N_EXPERTS].astype(jnp.int32)
    padded = ((cnt + tm - 1) // tm) * tm
    ends = jnp.cumsum(padded)
    starts = ends - padded
    group_start = jnp.zeros_like(experts)
    for e in range(N_EXPERTS):
        group_start = jnp.where(experts == e, starts[e], group_start)
    pos = (group_start + ranks).reshape(2 * tokens)
    rows = 2 * tokens + N_EXPERTS * tm
    n_tiles = rows // tm
    n_valid = (ends[-1] // tm).astype(jnp.int32).reshape(1)
    tile_start = jnp.arange(n_tiles, dtype=jnp.int32) * tm
    tile_expert = jnp.minimum(
        jnp.sum((tile_start[:, None] >= ends[None, :]).astype(jnp.int32), axis=1), N_EXPERTS - 1)
    own = (tile_expert[:, None] == jnp.arange(N_EXPERTS)[None, :]).astype(jnp.int32)
    tile_tokens = jnp.sum(own * (starts + cnt)[None, :], axis=1) - tile_start
    tile_half = (tile_tokens <= tm // 2).astype(jnp.int32)
    x_sorted = _sc_dispatch(xn.reshape(tokens, d), pos.reshape(2, tokens), rows)
    y_sorted = _experts(x_sorted, tile_expert, n_valid, tile_half, w1, w3, w2)
    y2 = _sc_gather(y_sorted, pos).reshape(2, bsz, length, d)
    return _combine(x, y2, meta, g4, mods)


def _grid_pos_embed(rows, d):
    row = np.repeat(np.arange(rows, dtype=np.float64), GRID_W)
    col = np.tile(np.arange(GRID_W, dtype=np.float64), rows)
    n_freq = d // 4
    omega = POS_THETA ** (-np.arange(n_freq, dtype=np.float64) / n_freq)
    ang_r = row[:, None] * omega
    ang_c = col[:, None] * omega
    return jnp.asarray(np.concatenate([np.sin(ang_r), np.cos(ang_r), np.sin(ang_c), np.cos(ang_c)], axis=-1),
                       dtype=F32)


def kernel(x, c, ctx, c_ctx, ada_w, ada_b, norm_g, ab_w_in, rg_conv_w, rg_conv_b, rg_w_r, rg_b_r, rg_w_i, rg_b_i, rg_lambda, cv_dw_w, cv_dw_b, cv_ln_g, cv_ln_b, ab_w_out, ffn_w1, ffn_w3, ffn_w2, cd_w_in, sg_ln_g, sg_ln_b, sg_w_s, sg_b_s, cd_w_out, moe_router, moe_w1, moe_w3, moe_w2):
    bsz, length, d = x.shape
    assert ada_w.shape[0] == 2, "two layers: one even (RG-LRU | Conformer), one odd (gMLP | Fourier)"
    pos = _grid_pos_embed(length // GRID_W, d)

    cpad = jnp.concatenate([c, c_ctx[None, :], jnp.zeros((16 - bsz - 1, d), F32)], axis=0)
    m = _ada(cpad, ada_w, ada_b)
    mods = m[:, :bsz].reshape(2, bsz, 6, d)
    mods_ctx = m[:, bsz:bsz + 1].reshape(2, 1, 6, d)

    mods_t = mods[0].transpose(1, 0, 2)
    mods_ctx_t = jnp.broadcast_to(mods_ctx[0].reshape(6, 1, d), (6, bsz, d))
    pos3 = pos[:, None, :]
    w_in = ab_w_in[0]
    xa3, gg3, glu3, x0_t, ffn1b, ffn3b, ffn2b = _inproj_t(
        x, pos3, mods_t, norm_g[0], w_in, rg_conv_w[0], rg_conv_b[0], True, riders=(ffn_w1[0], ffn_w3[0], ffn_w2[0]))
    (xc3,) = _inproj_t(ctx, None, mods_ctx_t, norm_g[0], w_in[:, 0:A_WIDTH], rg_conv_w[0], rg_conv_b[0], False)
    wg, bg, sp = _gate_weights(rg_w_r[0], rg_b_r[0], rg_w_i[0], rg_b_i[0], rg_lambda[0])
    n_exp, _, dff_e = moe_w1[0].shape
    rec_hi, rec_lo = _scan_t(xa3, xc3, gg3, wg, bg, sp)
    u3, w2b = _conv_t(glu3, cv_dw_w[0], cv_dw_b[0], cv_ln_g[0], cv_ln_b[0],
                      riders=(moe_w2[0].reshape(n_exp * dff_e, d),))
    x1_t = _outproj_t(rec_lo, rec_hi, u3, x0_t, ab_w_out[0], norm_g[0], mods_t)
    x2, w1b, w3b = _ffn_t(x1_t, mods_t, norm_g[0], ffn1b, ffn3b, ffn2b,
                          riders=(moe_w1[0].reshape(n_exp * d, dff_e), moe_w3[0].reshape(n_exp * d, dff_e)))

    u, v, f = _inproj_cd(x2, mods[1], norm_g[1], cd_w_in[0])
    fo = _fourier(f)
    x3 = _gate_out(u, v, fo, x2, sg_ln_g[0], sg_ln_b[0], sg_w_s[0], sg_b_s[0], cd_w_out[0],
                   norm_g[1], mods[1])
    return _moe(x3, mods[1], norm_g[1], moe_router[0], w1b.reshape(n_exp, d, dff_e), w3b.reshape(n_exp, d, dff_e),
                w2b.reshape(n_exp, dff_e, d))
```

```python
import functools

import numpy as np
import jax
import jax.numpy as jnp
from jax import lax
from jax.experimental import pallas as pl
from jax.experimental.pallas import tpu as pltpu
from jax.experimental.pallas import tpu_sc as plsc

F32 = jnp.float32
BF16 = jnp.bfloat16

GRID_W = 64
POS_THETA = 10000.0
NORM_EPS = 1e-6
LN_EPS = 1e-5
LRU_C = 8.0

A_WIDTH = 512
A_HEADS = 8
A_CONV_W = 4
B_WIDTH = 512
B_CONV_W = 31
C_WIDTH = 512
C_HEADS = 8
CHUNK = 128
D_WIDTH = 512
D_GROUPS = 4
D_GROUP_DIM = D_WIDTH // D_GROUPS
N_EXPERTS = 8

LANES = 128
MXU_WIDTH = 256
ROW_TILE = 512
EXPERT_ROW_TILE = 512
VMEM_LIMIT = 56 * 2 ** 20


def _params(sem):
    return pltpu.CompilerParams(dimension_semantics=sem, vmem_limit_bytes=VMEM_LIMIT)


def _rms(x, g):
    return x * lax.rsqrt(jnp.mean(x * x, axis=-1, keepdims=True) + NORM_EPS) * g


def _bdot(a, b):
    return jnp.dot(a, b, preferred_element_type=F32)


def _swiglu_halves(x, w1_ref, w3_ref, w2_ref, lead=()):
    n = w1_ref.shape[-1]
    half = -(-(n // 2) // MXU_WIDTH) * MXU_WIDTH
    y = None
    for lo, hi in ((0, half), (half, n)):
        a = _bdot(x, w1_ref[lead + (slice(None), slice(lo, hi))])
        b = _bdot(x, w3_ref[lead + (slice(None), slice(lo, hi))])
        t = (jax.nn.silu(a) * b).astype(BF16)
        part = _bdot(t, w2_ref[lead + (slice(lo, hi), slice(None))])
        y = part if y is None else y + part
    return y


def _ada_kernel(c_ref, w_ref, b_ref, o_ref):
    s = jax.nn.silu(c_ref[...])
    o_ref[0] = _bdot(s.astype(BF16), w_ref[0].astype(BF16)) + b_ref[0]


def _ada(cpad, ada_w, ada_b):
    nl, d, n6 = ada_w.shape
    rows = cpad.shape[0]
    tn = n6 // 4
    return pl.pallas_call(
        _ada_kernel,
        grid=(nl, n6 // tn),
        in_specs=[
            pl.BlockSpec((rows, d), lambda l, j: (0, 0)),
            pl.BlockSpec((1, d, tn), lambda l, j: (l, 0, j)),
            pl.BlockSpec((1, 1, tn), lambda l, j: (l, 0, j)),
        ],
        out_specs=pl.BlockSpec((1, rows, tn), lambda l, j: (l, 0, j)),
        out_shape=jax.ShapeDtypeStruct((nl, rows, n6), F32),
        compiler_params=_params(("arbitrary", "arbitrary")),
        name="ada_terms",
    )(cpad, ada_w, ada_b.reshape(nl, 1, n6))


TIME_TILE = 128
SCAN_TIME_TILE = 128
SCAN_LANES = 256
CONV_TIME_TILE = 128
CONV_HALO = 16


def _time_major(ref):
    return jnp.swapaxes(ref[...], 0, 1)


def _cast_riders(cast_in, cast_out):
    for src, dst in zip(cast_in, cast_out):
        dst[...] = src[...].astype(BF16)


def _rider_specs(arrays, n_steps, step_of):
    in_specs, out_specs, out_shapes = [], [], []
    for arr in arrays:
        rows, cols = arr.shape
        block = (rows // n_steps, cols)
        in_specs.append(pl.BlockSpec(block, lambda *idx: (step_of(*idx), 0)))
        out_specs.append(pl.BlockSpec(block, lambda *idx: (step_of(*idx), 0)))
        out_shapes.append(jax.ShapeDtypeStruct(arr.shape, BF16))
    return in_specs, out_specs, out_shapes


def _inproj_t_kernel(*refs, tt, add_pos, branches, n_cast):
    if n_cast:
        n_in = len(refs) - n_cast - (4 if branches else 1) - n_cast
        _cast_riders(refs[n_in:n_in + n_cast], refs[len(refs) - n_cast:])
        refs = refs[:n_in] + refs[n_in + n_cast:len(refs) - n_cast]
    if add_pos:
        x_ref, xp_ref, xn_ref, pos_ref, posp_ref, posn_ref, mod_ref, g_ref, w_ref, cw_ref, cb_ref, *outs = refs
        xall = jnp.concatenate(
            [_time_major(xp_ref)[7:8] + posp_ref[...], _time_major(x_ref) + pos_ref[...],
             _time_major(xn_ref)[0:2] + posn_ref[...]], axis=0)
    else:
        x_ref, xp_ref, xn_ref, mod_ref, g_ref, w_ref, cw_ref, cb_ref, *outs = refs
        xall = jnp.concatenate([_time_major(xp_ref)[7:8], _time_major(x_ref), _time_major(xn_ref)[0:2]], axis=0)
    i = pl.program_id(0)
    last = pl.num_programs(0) - 1
    steps, bsz, d = xall.shape
    h = _rms(xall, g_ref[0:1, :]) * (1.0 + mod_ref[1]) + mod_ref[0]
    z = _bdot(h.reshape(steps * bsz, d).astype(BF16), w_ref[...].astype(BF16))
    z3 = z.reshape(steps, bsz, z.shape[-1])
    t_idx = lax.broadcasted_iota(jnp.int32, (steps, bsz, A_WIDTH), 0)
    inside = jnp.logical_and(jnp.logical_or(t_idx >= 1, i > 0), jnp.logical_or(t_idx <= tt, i < last))
    xa = jnp.where(inside, z3[:, :, 0:A_WIDTH], 0.0)
    xconv = cb_ref[...] + cw_ref[0:1, :] * xa[0:tt]
    for k in range(1, A_CONV_W):
        xconv = xconv + cw_ref[k:k + 1, :] * xa[k:k + tt]
    outs[0][...] = xconv
    if branches:
        outs[1][...] = jax.nn.gelu(z3[1:tt + 1, :, A_WIDTH:2 * A_WIDTH])
        vb = z3[1:tt + 1, :, 2 * A_WIDTH:2 * A_WIDTH + B_WIDTH]
        gb = z3[1:tt + 1, :, 2 * A_WIDTH + B_WIDTH:]
        outs[2][...] = vb * jax.nn.sigmoid(gb)
        outs[3][...] = xall[1:tt + 1]


def _inproj_t(x, pos3, mods_t, g4, w, conv_w, conv_b, branches, riders=()):
    bsz, length, d = x.shape
    n = w.shape[1]
    tt = min(TIME_TILE, length)
    add_pos = pos3 is not None
    n_out = 3 if branches else 1
    prev_map = lambda i: (jnp.maximum(i * tt - 1, 0), 0, 0)
    next_map = lambda i: (jnp.minimum((i + 1) * (tt // 2), length // 2 - 1), 0, 0)
    in_specs = [
        pl.BlockSpec((bsz, tt, d), lambda i: (0, i, 0)),
        pl.BlockSpec((bsz, 8, d), lambda i: (0, jnp.maximum(i * (tt // 8) - 1, 0), 0)),
        pl.BlockSpec((bsz, 8, d), lambda i: (0, jnp.minimum((i + 1) * (tt // 8), length // 8 - 1), 0)),
    ]
    args = [x, x, x]
    if add_pos:
        in_specs += [
            pl.BlockSpec((tt, 1, d), lambda i: (i, 0, 0)),
            pl.BlockSpec((1, 1, d), prev_map),
            pl.BlockSpec((2, 1, d), next_map),
        ]
        args += [pos3, pos3, pos3]
    in_specs += [
        pl.BlockSpec((6, bsz, d), lambda i: (0, 0, 0)),
        pl.BlockSpec((4, d), lambda i: (0, 0)),
        pl.BlockSpec((d, n), lambda i: (0, 0)),
        pl.BlockSpec((A_CONV_W, A_WIDTH), lambda i: (0, 0)),
        pl.BlockSpec((1, A_WIDTH), lambda i: (0, 0)),
    ]
    args += [mods_t, g4, w, conv_w, conv_b.reshape(1, A_WIDTH)]
    r_in, r_out, r_shapes = _rider_specs(riders, length // tt, lambda i: i)
    out_specs = [pl.BlockSpec((tt, bsz, A_WIDTH), lambda i: (i, 0, 0))] * n_out
    out_shapes = [jax.ShapeDtypeStruct((length, bsz, A_WIDTH), F32)] * n_out
    if branches:
        out_specs.append(pl.BlockSpec((tt, bsz, d), lambda i: (i, 0, 0)))
        out_shapes.append(jax.ShapeDtypeStruct((length, bsz, d), F32))
    return pl.pallas_call(
        functools.partial(_inproj_t_kernel, tt=tt, add_pos=add_pos, branches=branches, n_cast=len(riders)),
        grid=(length // tt,),
        in_specs=in_specs + r_in,
        out_specs=out_specs + r_out,
        out_shape=out_shapes + r_shapes,
        compiler_params=_params(("parallel",)),
        name="inproj_ab" if branches else "inproj_ctx",
    )(*args, *riders)


def _scan_t_kernel(*refs, tb, nb, ctx_len, n_cast):
    xf_ref, xb_ref, ggf_ref, ggb_ref, xc_ref, wg_ref, bg_ref, sp_ref = refs[:8]
    cast_in = refs[8:8 + n_cast]
    hi_ref, lo_ref = refs[8 + n_cast:10 + n_cast]
    cast_out = refs[10 + n_cast:10 + 2 * n_cast]
    a_ref, b_ref, hs_ref, h_ref, state_ref = refs[10 + 2 * n_cast:]
    _cast_riders(cast_in, cast_out)
    k = pl.program_id(1)
    half = nb // 2
    bsz, lanes = xf_ref.shape[1], xf_ref.shape[2]
    chunk = 128

    def coefficients(x_ref, n, direction):
        def body(c, _):
            t0 = pl.multiple_of(c * chunk, chunk)
            x = x_ref[pl.ds(t0, chunk)].reshape(chunk * bsz, lanes)
            th = jnp.tanh(_bdot(x.astype(BF16), wg_ref[0, direction]) + bg_ref[0, direction])
            i = 0.5 * th[:, lanes:] + 0.5
            c = (-0.5 * LRU_C) * sp_ref[0, direction:direction + 1, :]
            log_a = c * th[:, 0:lanes] + c
            a = jnp.exp(log_a)
            one_minus_a2 = -jnp.tanh(log_a) * (a * a + 1.0)
            a_ref[direction, pl.ds(t0, chunk)] = a.reshape(chunk, bsz, lanes)
            b_ref[direction, pl.ds(t0, chunk)] = (jnp.sqrt(one_minus_a2) * (i * x)).reshape(chunk, bsz, lanes)
            return 0
        lax.fori_loop(0, n // chunk, body, 0)

    def sweep(n, store):
        def body(t, carry):
            hf, hb = carry
            tr = n - 1 - t
            hf = a_ref[0, pl.ds(t, 1)][0] * hf + b_ref[0, pl.ds(t, 1)][0]
            hb = a_ref[1, pl.ds(tr, 1)][0] * hb + b_ref[1, pl.ds(tr, 1)][0]
            if store:
                hs_ref[0, pl.ds(t, 1)] = hf[None]
                hs_ref[1, pl.ds(tr, 1)] = hb[None]
            return hf, hb
        hf, hb = lax.fori_loop(0, n, body, (state_ref[0], state_ref[1]), unroll=8)
        state_ref[0] = hf
        state_ref[1] = hb

    @pl.when(k == 0)
    def _():
        state_ref[...] = jnp.zeros_like(state_ref)
        coefficients(xc_ref, ctx_len, 0)
        coefficients(xc_ref, ctx_len, 1)
        sweep(ctx_len, False)

    coefficients(xf_ref, tb, 0)
    coefficients(xb_ref, tb, 1)
    sweep(tb, True)
    m = nb - 1 - k

    @pl.when(k < half)
    def _():
        h_ref[k] = hs_ref[0]
        h_ref[m] = hs_ref[1]

    @pl.when(k >= half)
    def _():
        hi_ref[...] = (hs_ref[0] + h_ref[k]) * ggf_ref[...]
        lo_ref[...] = (h_ref[m] + hs_ref[1]) * ggb_ref[...]


def _scan_t(xa3, xc3, gg3, wg, bg, sp, riders=()):
    length, bsz, width = xa3.shape
    ctx_len = xc3.shape[0]
    tb = min(SCAN_TIME_TILE, length // 2)
    nb = length // tb
    half = nb // 2
    lanes = SCAN_LANES
    blk = (tb, bsz, lanes)
    r_in, r_out, r_shapes = _rider_specs(riders, (width // lanes) * nb, lambda g, k: g * nb + k)
    return pl.pallas_call(
        functools.partial(_scan_t_kernel, tb=tb, nb=nb, ctx_len=ctx_len, n_cast=len(riders)),
        grid=(width // lanes, nb),
        in_specs=[
            pl.BlockSpec(blk, lambda g, k: (k, 0, g)),
            pl.BlockSpec(blk, lambda g, k: (nb - 1 - k, 0, g)),
            pl.BlockSpec(blk, lambda g, k: (jnp.maximum(k, half), 0, g)),
            pl.BlockSpec(blk, lambda g, k: (jnp.minimum(nb - 1 - k, half - 1), 0, g)),
            pl.BlockSpec((ctx_len, bsz, lanes), lambda g, k: (0, 0, g)),
            pl.BlockSpec((1, 2, lanes, 2 * lanes), lambda g, k: (g, 0, 0, 0)),
            pl.BlockSpec((1, 2, 1, 2 * lanes), lambda g, k: (g, 0, 0, 0)),
            pl.BlockSpec((1, 2, lanes), lambda g, k: (g, 0, 0)),
        ] + r_in,
        out_specs=[
            pl.BlockSpec(blk, lambda g, k: (jnp.maximum(k - half, 0), 0, g)),
            pl.BlockSpec(blk, lambda g, k: (jnp.minimum(nb - 1 - k, half - 1), 0, g)),
        ] + r_out,
        out_shape=[jax.ShapeDtypeStruct((length // 2, bsz, width), F32)] * 2 + r_shapes,
        scratch_shapes=[
            pltpu.VMEM((2, max(tb, ctx_len), bsz, lanes), F32),
            pltpu.VMEM((2, max(tb, ctx_len), bsz, lanes), F32),
            pltpu.VMEM((2, tb, bsz, lanes), F32),
            pltpu.VMEM((nb, tb, bsz, lanes), F32),
            pltpu.VMEM((2, bsz, lanes), F32),
        ],
        compiler_params=_params(("parallel", "arbitrary")),
        name="rglru_scan",
    )(xa3, xa3, gg3, gg3, xc3, wg, bg, sp, *riders)


def _gate_weights(w_r, b_r, w_i, b_i, lam):
    hd = A_WIDTH // A_HEADS
    heads_per_group = SCAN_LANES // hd
    ngroups = A_WIDTH // SCAN_LANES

    def blockdiag(w):
        w = w.reshape(ngroups, heads_per_group, hd, hd)
        eye = jnp.eye(heads_per_group, dtype=w.dtype)
        return jnp.einsum("ghde,hk->ghdke", w, eye).reshape(ngroups, SCAN_LANES, SCAN_LANES)

    wg = jnp.stack([jnp.concatenate([blockdiag(w_r[d]), blockdiag(w_i[d])], axis=-1) for d in range(2)], axis=1)
    bg = jnp.stack([jnp.concatenate([b_r[d].reshape(ngroups, 1, SCAN_LANES), b_i[d].reshape(ngroups, 1, SCAN_LANES)],
                                    axis=-1) for d in range(2)], axis=1)
    sp = jax.nn.softplus(-lam.astype(F32)).reshape(2, ngroups, SCAN_LANES).transpose(1, 0, 2)
    return (0.5 * wg).astype(BF16), 0.5 * bg, sp


def _conv_t_kernel(*refs, tb, n_cast):
    x_ref, xp_ref, xn_ref, w_ref, b_ref, lng_ref, lnb_ref = refs[:7]
    cast_in = refs[7:7 + n_cast]
    o_ref = refs[7 + n_cast]
    cast_out = refs[8 + n_cast:8 + 2 * n_cast]
    stage_ref = refs[-1]
    _cast_riders(cast_in, cast_out)
    i = pl.program_id(0)
    last = pl.num_programs(0) - 1
    halo = CONV_HALO
    stage_ref[0:halo] = jnp.where(i > 0, xp_ref[...], 0.0)
    stage_ref[halo:halo + tb] = x_ref[...]
    stage_ref[halo + tb:2 * halo + tb] = jnp.where(i < last, xn_ref[...], 0.0)
    sub = 8

    def body(r, _):
        t0 = pl.multiple_of(r * sub, sub)
        def tap(k, acc):
            return acc + w_ref[k] * stage_ref[pl.ds(t0 + 1 + k, sub)]
        acc = lax.fori_loop(0, B_CONV_W + 1, tap, jnp.broadcast_to(b_ref[...], (sub,) + b_ref.shape), unroll=8)
        o_ref[pl.ds(t0, sub)] = acc
        return 0
    lax.fori_loop(0, tb // sub, body, 0)

    norm_rows = 64

    def norm_body(r, _):
        t0 = pl.multiple_of(r * norm_rows, norm_rows)
        acc = o_ref[pl.ds(t0, norm_rows)]
        mu = jnp.mean(acc, axis=-1, keepdims=True)
        cen = acc - mu
        var = jnp.mean(cen * cen, axis=-1, keepdims=True)
        y = cen * lax.rsqrt(var + LN_EPS) * lng_ref[...] + lnb_ref[...]
        o_ref[pl.ds(t0, norm_rows)] = jax.nn.silu(y)
        return 0
    lax.fori_loop(0, tb // norm_rows, norm_body, 0)


def _conv_t(glu3, dw_w, dw_b, ln_g, ln_b, riders=()):
    length, bsz, width = glu3.shape
    tb = min(CONV_TIME_TILE, length)
    halo = CONV_HALO
    taps = dw_w.shape[0]
    w8 = jnp.broadcast_to(jnp.concatenate([dw_w, jnp.zeros((32 - taps, width), F32)], axis=0)[:, None, :],
                          (32, bsz, width))
    b8 = jnp.broadcast_to(dw_b[None, :], (bsz, width))
    r_in, r_out, r_shapes = _rider_specs(riders, length // tb, lambda i: i)
    return pl.pallas_call(
        functools.partial(_conv_t_kernel, tb=tb, n_cast=len(riders)),
        grid=(length // tb,),
        in_specs=[
            pl.BlockSpec((tb, bsz, width), lambda i: (i, 0, 0)),
            pl.BlockSpec((halo, bsz, width), lambda i: (jnp.maximum(i * (tb // halo) - 1, 0), 0, 0)),
            pl.BlockSpec((halo, bsz, width), lambda i: (jnp.minimum((i + 1) * (tb // halo), length // halo - 1), 0, 0)),
            pl.BlockSpec((32, bsz, width), lambda i: (0, 0, 0)),
            pl.BlockSpec((bsz, width), lambda i: (0, 0)),
            pl.BlockSpec((1, width), lambda i: (0, 0)),
            pl.BlockSpec((1, width), lambda i: (0, 0)),
        ] + r_in,
        out_specs=[pl.BlockSpec((tb, bsz, width), lambda i: (i, 0, 0))] + r_out,
        out_shape=[jax.ShapeDtypeStruct((length, bsz, width), F32)] + r_shapes,
        scratch_shapes=[pltpu.VMEM((tb + 2 * halo, bsz, width), F32)],
        compiler_params=_params(("parallel",)),
        name="conformer_conv",
    )(glu3, glu3, glu3, w8, b8, ln_g.reshape(1, -1), ln_b.reshape(1, -1), *riders)


def _conv_out_t_kernel(*refs, tb, n_cast, half_steps):
    conv_in = refs[:7]
    lo_ref, hi_ref, x0_ref, wout_ref, g_ref, mod_ref = refs[7:13]
    cast_in = refs[13:13 + n_cast]
    o_ref = refs[13 + n_cast]
    cast_out = refs[14 + n_cast:14 + 2 * n_cast]
    stage_ref, u_ref = refs[-2:]
    _conv_t_kernel(*conv_in, *cast_in, u_ref, *cast_out, stage_ref, tb=tb, n_cast=n_cast)
    i = pl.program_id(0)
    bsz, d = x0_ref.shape[1], x0_ref.shape[2]
    rows = tb * bsz
    rec = jnp.where(i < half_steps, lo_ref[...], hi_ref[...])
    y = (_bdot(rec.reshape(rows, A_WIDTH).astype(BF16), wout_ref[0:A_WIDTH, :].astype(BF16))
         + _bdot(u_ref[...].reshape(rows, B_WIDTH).astype(BF16), wout_ref[A_WIDTH:, :].astype(BF16)))
    o_ref[...] = x0_ref[...] + mod_ref[2] * _rms(y, g_ref[1:2, :]).reshape(tb, bsz, d)


def _conv_out_t(glu3, dw_w, dw_b, ln_g, ln_b, rec_lo, rec_hi, x0_t, w_out, g4, mods_t, riders=()):
    length, bsz, width = glu3.shape
    d = x0_t.shape[2]
    tb = min(CONV_TIME_TILE // 2, length // 2)
    half_steps = (length // 2) // tb
    halo = CONV_HALO
    taps = dw_w.shape[0]
    w8 = jnp.broadcast_to(jnp.concatenate([dw_w, jnp.zeros((32 - taps, width), F32)], axis=0)[:, None, :],
                          (32, bsz, width))
    b8 = jnp.broadcast_to(dw_b[None, :], (bsz, width))
    r_in, r_out, r_shapes = _rider_specs(riders, length // tb, lambda i: i)
    return pl.pallas_call(
        functools.partial(_conv_out_t_kernel, tb=tb, n_cast=len(riders), half_steps=half_steps),
        grid=(length // tb,),
        in_specs=[
            pl.BlockSpec((tb, bsz, width), lambda i: (i, 0, 0)),
            pl.BlockSpec((halo, bsz, width), lambda i: (jnp.maximum(i * (tb // halo) - 1, 0), 0, 0)),
            pl.BlockSpec((halo, bsz, width), lambda i: (jnp.minimum((i + 1) * (tb // halo), length // halo - 1), 0, 0)),
            pl.BlockSpec((32, bsz, width), lambda i: (0, 0, 0)),
            pl.BlockSpec((bsz, width), lambda i: (0, 0)),
            pl.BlockSpec((1, width), lambda i: (0, 0)),
            pl.BlockSpec((1, width), lambda i: (0, 0)),
            pl.BlockSpec((tb, bsz, A_WIDTH), lambda i: (jnp.minimum(i, half_steps - 1), 0, 0)),
            pl.BlockSpec((tb, bsz, A_WIDTH), lambda i: (jnp.maximum(i - half_steps, 0), 0, 0)),
            pl.BlockSpec((tb, bsz, d), lambda i: (i, 0, 0)),
            pl.BlockSpec((A_WIDTH + B_WIDTH, d), lambda i: (0, 0)),
            pl.BlockSpec((4, d), lambda i: (0, 0)),
            pl.BlockSpec((6, bsz, d), lambda i: (0, 0, 0)),
        ] + r_in,
        out_specs=[pl.BlockSpec((tb, bsz, d), lambda i: (i, 0, 0))] + r_out,
        out_shape=[jax.ShapeDtypeStruct((length, bsz, d), F32)] + r_shapes,
        scratch_shapes=[pltpu.VMEM((tb + 2 * halo, bsz, width), F32), pltpu.VMEM((tb, bsz, width), F32)],
        compiler_params=_params(("parallel",)),
        name="conformer_conv_outproj",
    )(glu3, glu3, glu3, w8, b8, ln_g.reshape(1, -1), ln_b.reshape(1, -1), rec_lo, rec_hi, x0_t, w_out, g4, mods_t,
      *riders)


def _outproj_t_kernel(lo_ref, hi_ref, u_ref, x_ref, wout_ref, g_ref, mod_ref, o_ref, *, half_steps):
    i = pl.program_id(0)
    tt, bsz, d = x_ref.shape
    rec = jnp.where(i < half_steps, lo_ref[...], hi_ref[...])
    rows = tt * bsz
    y = (_bdot(rec.reshape(rows, A_WIDTH).astype(BF16), wout_ref[0:A_WIDTH, :].astype(BF16))
         + _bdot(u_ref[...].reshape(rows, B_WIDTH).astype(BF16), wout_ref[A_WIDTH:, :].astype(BF16)))
    yn = _rms(y, g_ref[1:2, :]).reshape(tt, bsz, d)
    o_ref[...] = x_ref[...] + mod_ref[2] * yn


def _outproj_t(rec_lo, rec_hi, u3, x0_t, w_out, g4, mods_t):
    length, bsz, d = x0_t.shape
    tt = min(TIME_TILE, length // 2)
    half_steps = (length // 2) // tt
    return pl.pallas_call(
        functools.partial(_outproj_t_kernel, half_steps=half_steps),
        grid=(length // tt,),
        in_specs=[
            pl.BlockSpec((tt, bsz, A_WIDTH), lambda i: (jnp.minimum(i, half_steps - 1), 0, 0)),
            pl.BlockSpec((tt, bsz, A_WIDTH), lambda i: (jnp.maximum(i - half_steps, 0), 0, 0)),
            pl.BlockSpec((tt, bsz, B_WIDTH), lambda i: (i, 0, 0)),
            pl.BlockSpec((tt, bsz, d), lambda i: (i, 0, 0)),
            pl.BlockSpec((A_WIDTH + B_WIDTH, d), lambda i: (0, 0)),
            pl.BlockSpec((4, d), lambda i: (0, 0)),
            pl.BlockSpec((6, bsz, d), lambda i: (0, 0, 0)),
        ],
        out_specs=pl.BlockSpec((tt, bsz, d), lambda i: (i, 0, 0)),
        out_shape=jax.ShapeDtypeStruct((length, bsz, d), F32),
        compiler_params=_params(("parallel",)),
        name="mixer_ab_outproj",
    )(rec_lo, rec_hi, u3, x0_t, w_out, g4, mods_t)


def _ffn_t_kernel(x_ref, mod_ref, g_ref, w1_ref, w3_ref, w2_ref, *rest, n_cast):
    _cast_riders(rest[:n_cast], rest[n_cast + 1:])
    o_ref = rest[n_cast]
    x = x_ref[...]
    tt, bsz, d = x.shape
    h = (_rms(x, g_ref[2:3, :]) * (1.0 + mod_ref[4]) + mod_ref[3]).reshape(tt * bsz, d).astype(BF16)
    y = _swiglu_halves(h, w1_ref, w3_ref, w2_ref)
    o_ref[...] = jnp.swapaxes(x + mod_ref[5] * _rms(y, g_ref[3:4, :]).reshape(tt, bsz, d), 0, 1)


def _ffn_t(x3, mods_t, g4, w1, w3, w2, riders=()):
    length, bsz, d = x3.shape
    dff = w1.shape[1]
    tt = min(TIME_TILE // 2, length)
    r_in, r_out, r_shapes = _rider_specs(riders, length // tt, lambda i: i)
    return pl.pallas_call(
        functools.partial(_ffn_t_kernel, n_cast=len(riders)),
        grid=(length // tt,),
        in_specs=[
            pl.BlockSpec((tt, bsz, d), lambda i: (i, 0, 0)),
            pl.BlockSpec((6, bsz, d), lambda i: (0, 0, 0)),
            pl.BlockSpec((4, d), lambda i: (0, 0)),
            pl.BlockSpec((d, dff), lambda i: (0, 0), pipeline_mode=pl.Buffered(1)),
            pl.BlockSpec((d, dff), lambda i: (0, 0), pipeline_mode=pl.Buffered(1)),
            pl.BlockSpec((dff, d), lambda i: (0, 0), pipeline_mode=pl.Buffered(1)),
        ] + r_in,
        out_specs=[pl.BlockSpec((bsz, tt, d), lambda i: (0, i, 0))] + r_out,
        out_shape=[jax.ShapeDtypeStruct((bsz, length, d), F32)] + r_shapes,
        compiler_params=_params(("parallel",)),
        name="dense_swiglu",
    )(x3, mods_t, g4, w1, w3, w2, *riders)


def _inproj_cd_kernel(x_ref, mod_ref, g_ref, w_ref, u_ref, v_ref, f_ref):
    x = x_ref[0]
    h = _rms(x, g_ref[0:1, :]) * (1.0 + mod_ref[0, 1:2, :]) + mod_ref[0, 0:1, :]
    z = _bdot(h.astype(BF16), w_ref[...].astype(BF16))
    u_ref[0] = jax.nn.gelu(z[:, 0:C_WIDTH])
    v_ref[0] = jax.nn.gelu(z[:, C_WIDTH:2 * C_WIDTH])
    f_ref[0] = z[:, 2 * C_WIDTH:]


def _inproj_cd(x, mods, g4, w):
    bsz, length, d = x.shape
    n = w.shape[1]
    tm = min(2 * ROW_TILE, length)
    return pl.pallas_call(
        _inproj_cd_kernel,
        grid=(bsz, length // tm),
        in_specs=[
            pl.BlockSpec((1, tm, d), lambda b, i: (b, i, 0)),
            pl.BlockSpec((1, 6, d), lambda b, i: (b, 0, 0)),
            pl.BlockSpec((4, d), lambda b, i: (0, 0)),
            pl.BlockSpec((d, n), lambda b, i: (0, 0)),
        ],
        out_specs=[pl.BlockSpec((1, tm, C_WIDTH), lambda b, i: (b, i, 0))] * 3,
        out_shape=[jax.ShapeDtypeStruct((bsz, length, C_WIDTH), F32)] * 3,
        compiler_params=_params(("parallel", "parallel")),
        name="inproj_cd",
    )(x, mods, g4, w)


def _fourier_kernel(f_ref, dmat_ref, lmat_ref, o_ref, z_ref, *, length, scale):
    j = pl.program_id(1)
    rows_per_step = 512 if length % 512 == 0 else length

    @pl.when(j == 0)
    def _():
        def body(c, _):
            rows = pl.ds(pl.multiple_of(c * rows_per_step, rows_per_step), rows_per_step)
            fb = f_ref[0, rows, :].astype(BF16)
            for g in range(D_GROUPS):
                cs = _bdot(fb[:, g * D_GROUP_DIM:(g + 1) * D_GROUP_DIM], dmat_ref[...])
                z_ref[0, rows, g * D_GROUP_DIM:(g + 1) * D_GROUP_DIM] = cs[:, 0:D_GROUP_DIM].astype(BF16)
                z_ref[1, rows, g * D_GROUP_DIM:(g + 1) * D_GROUP_DIM] = cs[:, D_GROUP_DIM:].astype(BF16)
            return 0
        lax.fori_loop(0, length // rows_per_step, body, 0)

    out = _bdot(lmat_ref[0], z_ref[0]) + _bdot(lmat_ref[1], z_ref[1])
    o_ref[0] = (out * scale).astype(BF16)


def _dft_matrices(length):
    k = np.arange(length, dtype=np.int64)
    ang_l = 2.0 * np.pi * ((k[:, None] * k[None, :]) % length).astype(np.float64) / length
    lmat = np.stack([np.cos(ang_l), -np.sin(ang_l)], axis=0)
    d = np.arange(D_GROUP_DIM, dtype=np.int64)
    ang_d = 2.0 * np.pi * ((d[:, None] * d[None, :]) % D_GROUP_DIM).astype(np.float64) / D_GROUP_DIM
    dmat = np.concatenate([np.cos(ang_d), np.sin(ang_d)], axis=1)
    return jnp.asarray(lmat, dtype=F32).astype(BF16), jnp.asarray(dmat, dtype=F32).astype(BF16)


def _fourier(f):
    bsz, length, width = f.shape
    tm = min(4 * ROW_TILE, length)
    lmat, dmat = _dft_matrices(length)
    scale = float(1.0 / np.sqrt(length * D_GROUP_DIM))
    return pl.pallas_call(
        functools.partial(_fourier_kernel, length=length, scale=scale),
        grid=(bsz, length // tm),
        in_specs=[
            pl.BlockSpec((1, length, width), lambda b, i: (b, 0, 0)),
            pl.BlockSpec((D_GROUP_DIM, 2 * D_GROUP_DIM), lambda b, i: (0, 0)),
            pl.BlockSpec((2, tm, length), lambda b, i: (0, i, 0)),
        ],
        out_specs=pl.BlockSpec((1, tm, width), lambda b, i: (b, i, 0)),
        out_shape=jax.ShapeDtypeStruct((bsz, length, width), BF16),
        scratch_shapes=[pltpu.VMEM((2, length, width), BF16)],
        compiler_params=_params(("parallel", "arbitrary")),
        name="fourier_mix",
    )(f, dmat, lmat)


def _gate_out_kernel(u_ref, v_ref, fo_ref, x_ref, lng_ref, lnb_ref, wcat_ref, bs_ref, wout_ref, g_ref, mod_ref,
                     o_ref, vs_ref, gt_ref, *, tm):
    head_dim = C_WIDTH // C_HEADS
    head_of_lane = lax.broadcasted_iota(jnp.int32, (CHUNK, C_WIDTH), 1) // head_dim
    for ci in range(tm // CHUNK):
        rows = slice(ci * CHUNK, (ci + 1) * CHUNK)
        v = v_ref[0, rows, :]
        mu = jnp.mean(v, axis=-1, keepdims=True)
        cen = v - mu
        var = jnp.mean(cen * cen, axis=-1, keepdims=True)
        vn = (cen * lax.rsqrt(var + LN_EPS) * lng_ref[...] + lnb_ref[...]).astype(BF16)
        for h in range(C_HEADS):
            vs_ref[h * CHUNK:(h + 1) * CHUNK, :] = jnp.where(head_of_lane == h, vn, jnp.zeros_like(vn))
        mixed = _bdot(wcat_ref[...], vs_ref[...]) + bs_ref[...]
        gt_ref[rows, :] = (u_ref[0, rows, :] * mixed).astype(BF16)
    y = (_bdot(gt_ref[...], wout_ref[0:C_WIDTH, :].astype(BF16))
         + _bdot(fo_ref[0], wout_ref[C_WIDTH:, :].astype(BF16)))
    o_ref[0] = x_ref[0] + mod_ref[0, 2:3, :] * _rms(y, g_ref[1:2, :])


def _gate_out(u, v, fo, x, ln_g, ln_b, w_s, b_s, w_out, g4, mods):
    bsz, length, d = x.shape
    tm = min(2 * ROW_TILE, length)
    head_dim = C_WIDTH // C_HEADS
    wcat = w_s.transpose(1, 0, 2).reshape(CHUNK, C_HEADS * CHUNK).astype(BF16)
    bs_full = jnp.repeat(b_s.T, head_dim, axis=1)
    const = lambda b, i: (0, 0)
    return pl.pallas_call(
        functools.partial(_gate_out_kernel, tm=tm),
        grid=(bsz, length // tm),
        in_specs=[
            pl.BlockSpec((1, tm, C_WIDTH), lambda b, i: (b, i, 0)),
            pl.BlockSpec((1, tm, C_WIDTH), lambda b, i: (b, i, 0)),
            pl.BlockSpec((1, tm, D_WIDTH), lambda b, i: (b, i, 0)),
            pl.BlockSpec((1, tm, d), lambda b, i: (b, i, 0)),
            pl.BlockSpec((1, C_WIDTH), const),
            pl.BlockSpec((1, C_WIDTH), const),
            pl.BlockSpec((CHUNK, C_HEADS * CHUNK), const),
            pl.BlockSpec((CHUNK, C_WIDTH), const),
            pl.BlockSpec((C_WIDTH + D_WIDTH, d), const),
            pl.BlockSpec((4, d), const),
            pl.BlockSpec((1, 6, d), lambda b, i: (b, 0, 0)),
        ],
        out_specs=pl.BlockSpec((1, tm, d), lambda b, i: (b, i, 0)),
        out_shape=jax.ShapeDtypeStruct((bsz, length, d), F32),
        scratch_shapes=[
            pltpu.VMEM((C_HEADS * CHUNK, C_WIDTH), BF16),
            pltpu.VMEM((tm, C_WIDTH), BF16),
        ],
        compiler_params=_params(("parallel", "parallel")),
        name="gating_outproj",
    )(u, v, fo, x, ln_g.reshape(1, -1), ln_b.reshape(1, -1), wcat, bs_full, w_out, g4, mods)


def _router_kernel(x_ref, mod_ref, g_ref, wr_ref, tri_ref, xn_ref, meta_ref, metat_ref, cnt_ref, carry_ref, *, tm):
    first = jnp.logical_and(pl.program_id(0) == 0, pl.program_id(1) == 0)

    @pl.when(first)
    def _():
        carry_ref[...] = jnp.zeros_like(carry_ref)

    x = x_ref[0]
    h = _rms(x, g_ref[2:3, :]) * (1.0 + mod_ref[0, 4:5, :]) + mod_ref[0, 3:4, :]
    xn_ref[0] = h
    h_hi = h.astype(BF16)
    h_lo = (h - h_hi.astype(F32)).astype(BF16)
    logits = _bdot(jnp.concatenate([h_hi, h_hi, h_lo], axis=1), wr_ref[...])
    lane = lax.broadcasted_iota(jnp.int32, (tm, LANES), 1).astype(F32)
    neg = jnp.float32(-jnp.inf)
    lm = jnp.where(lane < N_EXPERTS, logits, neg)
    m1 = jnp.max(lm, axis=-1, keepdims=True)
    i1 = jnp.min(jnp.where(lm == m1, lane, float(LANES)), axis=-1, keepdims=True)
    lm2 = jnp.where(lane == i1, neg, lm)
    m2 = jnp.max(lm2, axis=-1, keepdims=True)
    i2 = jnp.min(jnp.where(lm2 == m2, lane, float(LANES)), axis=-1, keepdims=True)
    e = jnp.exp(m2 - m1)
    p1 = 1.0 / (1.0 + e)
    p2 = e / (1.0 + e)
    hot1 = (lane == i1).astype(F32)
    hot2 = (lane == i2).astype(F32)
    both = hot1 + hot2
    before = _bdot(tri_ref[...], both.astype(BF16)) + carry_ref[0:1, :]
    r1 = jnp.sum(before * hot1, axis=-1, keepdims=True)
    r2 = jnp.sum(before * hot2, axis=-1, keepdims=True)
    carry_ref[0:1, :] = carry_ref[0:1, :] + jnp.sum(both, axis=0, keepdims=True)
    cnt_ref[...] = carry_ref[...]
    vals = (i1, i2, p1, p2, r1, r2)
    meta = jnp.zeros((tm, LANES), F32)
    for k, val in enumerate(vals):
        meta = jnp.where(lane == k, val, meta)
    meta_ref[0] = meta
    metat_ref[...] = meta.T[0:8, :]


def _router(x, mods, g4, w_router):
    bsz, length, d = x.shape
    tm = min(2 * ROW_TILE, length)
    wr = jnp.concatenate([w_router, jnp.zeros((d, LANES - N_EXPERTS), F32)], axis=1)
    wr_hi = wr.astype(BF16)
    wr_lo = (wr - wr_hi.astype(F32)).astype(BF16)
    wr = jnp.concatenate([wr_hi, wr_lo, wr_hi], axis=0)
    tri = jnp.asarray(np.tril(np.ones((tm, tm), np.float32), -1), dtype=BF16)
    const = lambda b, i: (0, 0)
    return pl.pallas_call(
        functools.partial(_router_kernel, tm=tm),
        grid=(bsz, length // tm),
        in_specs=[
            pl.BlockSpec((1, tm, d), lambda b, i: (b, i, 0)),
            pl.BlockSpec((1, 6, d), lambda b, i: (b, 0, 0)),
            pl.BlockSpec((4, d), const),
            pl.BlockSpec((3 * d, LANES), const),
            pl.BlockSpec((tm, tm), const),
        ],
        out_specs=[
            pl.BlockSpec((1, tm, d), lambda b, i: (b, i, 0)),
            pl.BlockSpec((1, tm, LANES), lambda b, i: (b, i, 0)),
            pl.BlockSpec((8, tm), lambda b, i: (0, b * (length // tm) + i)),
            pl.BlockSpec((8, LANES), const),
        ],
        out_shape=[
            jax.ShapeDtypeStruct((bsz, length, d), F32),
            jax.ShapeDtypeStruct((bsz, length, LANES), F32),
            jax.ShapeDtypeStruct((8, bsz * length), F32),
            jax.ShapeDtypeStruct((8, LANES), F32),
        ],
        scratch_shapes=[pltpu.VMEM((8, LANES), F32)],
        compiler_params=_params(("arbitrary", "arbitrary")),
        name="moe_router",
    )(x, mods, g4, wr, tri)


def _expert_kernel(te_ref, nv_ref, th_ref, x_ref, w1_ref, w3_ref, w2_ref, o_ref, acc_ref):
    i = pl.program_id(0)
    j = pl.program_id(1)
    tm = x_ref.shape[0]

    def run(rows):
        y = _swiglu_halves(x_ref[0:rows, :].astype(BF16), w1_ref, w3_ref, w2_ref, lead=(0,))

        @pl.when(j == 0)
        def _():
            acc_ref[0:rows, :] = y

        @pl.when(j > 0)
        def _():
            acc_ref[0:rows, :] += y

        @pl.when(j == pl.num_programs(1) - 1)
        def _():
            o_ref[0:rows, :] = acc_ref[0:rows, :]

    valid = i < nv_ref[0]
    half_full = th_ref[i] == 1

    @pl.when(jnp.logical_and(valid, jnp.logical_not(half_full)))
    def _():
        run(tm)

    @pl.when(jnp.logical_and(valid, half_full))
    def _():
        run(tm // 2)


def _experts(x_sorted, tile_expert, n_valid, tile_half, w1, w3, w2):
    rows = x_sorted.shape[0]
    d = w1.shape[1]
    dff = w1.shape[2]
    tm = EXPERT_ROW_TILE
    n_ff = 2
    tf = dff // n_ff

    def row_map(i, j, te, nv, th):
        return (jnp.maximum(jnp.minimum(i, nv[0] - 1), 0), 0)

    def w13_map(i, j, te, nv, th):
        return (te[jnp.maximum(jnp.minimum(i, nv[0] - 1), 0)], 0, jnp.where(i < nv[0], j, n_ff - 1))

    def w2_map(i, j, te, nv, th):
        return (te[jnp.maximum(jnp.minimum(i, nv[0] - 1), 0)], jnp.where(i < nv[0], j, n_ff - 1), 0)

    return pl.pallas_call(
        _expert_kernel,
        grid_spec=pltpu.PrefetchScalarGridSpec(
            num_scalar_prefetch=3,
            grid=(rows // tm, n_ff),
            in_specs=[
                pl.BlockSpec((tm, d), row_map),
                pl.BlockSpec((1, d, tf), w13_map),
                pl.BlockSpec((1, d, tf), w13_map),
                pl.BlockSpec((1, tf, d), w2_map),
            ],
            out_specs=pl.BlockSpec((tm, d), row_map),
            scratch_shapes=[pltpu.VMEM((tm, d), F32)],
        ),
        out_shape=jax.ShapeDtypeStruct((rows, d), F32),
        compiler_params=_params(("arbitrary", "arbitrary")),
        name="expert_swiglu",
    )(tile_expert, n_valid, tile_half, x_sorted, w1, w3, w2)


def _combine_kernel(x_ref, y_ref, meta_ref, g_ref, mod_ref, o_ref):
    meta = meta_ref[0]
    y = meta[:, 2:3] * y_ref[0, 0] + meta[:, 3:4] * y_ref[1, 0]
    o_ref[0] = x_ref[0] + mod_ref[0, 5:6, :] * _rms(y, g_ref[3:4, :])


def _combine(x, y2, meta, g4, mods):
    bsz, length, d = x.shape
    tm = min(2 * ROW_TILE, length)
    return pl.pallas_call(
        _combine_kernel,
        grid=(bsz, length // tm),
        in_specs=[
            pl.BlockSpec((1, tm, d), lambda b, i: (b, i, 0)),
            pl.BlockSpec((2, 1, tm, d), lambda b, i: (0, b, i, 0)),
            pl.BlockSpec((1, tm, LANES), lambda b, i: (b, i, 0)),
            pl.BlockSpec((4, d), lambda b, i: (0, 0)),
            pl.BlockSpec((1, 6, d), lambda b, i: (b, 0, 0)),
        ],
        out_specs=pl.BlockSpec((1, tm, d), lambda b, i: (b, i, 0)),
        out_shape=jax.ShapeDtypeStruct((bsz, length, d), F32),
        compiler_params=_params(("parallel", "parallel")),
        name="moe_combine",
    )(x, y2, meta, g4, mods)


SC_CORES = 2
SC_SUBCORES = 16
SC_WORKERS = SC_CORES * SC_SUBCORES
SC_ROW_CHUNK = 32


def _sc_mesh():
    return plsc.VectorSubcoreMesh(core_axis_name="c", subcore_axis_name="s")


def _sc_dispatch(rows_in, pos2, n_out):
    tokens, width = rows_in.shape
    per_worker = tokens // SC_WORKERS
    chunk = SC_ROW_CHUNK
    n_pairs = per_worker // (2 * chunk)
    idx = pos2.reshape(2, SC_WORKERS, 2 * n_pairs, chunk)

    @functools.partial(
        pl.kernel, mesh=_sc_mesh(),
        out_type=jax.ShapeDtypeStruct((n_out, width), rows_in.dtype),
        scratch_types=[
            pltpu.VMEM((2 * n_pairs, chunk), jnp.int32),
            pltpu.VMEM((2 * n_pairs, chunk), jnp.int32),
            pltpu.VMEM((2, chunk, width), rows_in.dtype),
        ] + [pltpu.SemaphoreType.DMA] * 6,
        name="sc_dispatch",
    )
    def body(rows_hbm, idx_hbm, out_hbm, idx0_v, idx1_v, rows_v, rsem0, rsem1, wsem0, wsem1, wsem2, wsem3):
        wid = lax.axis_index("s") * SC_CORES + lax.axis_index("c")
        base = wid * per_worker
        rsem = (rsem0, rsem1)
        wsem = ((wsem0, wsem1), (wsem2, wsem3))
        pltpu.sync_copy(idx_hbm.at[0, wid], idx0_v)
        pltpu.sync_copy(idx_hbm.at[1, wid], idx1_v)

        @pl.loop(0, n_pairs)
        def _(p):
            c0 = 2 * p
            reads = [pltpu.async_copy(rows_hbm.at[pl.ds(base + (c0 + s) * chunk, chunk)], rows_v.at[s], rsem[s])
                     for s in range(2)]
            writes = []
            for s in range(2):
                reads[s].wait()
                writes.append(pltpu.async_copy(rows_v.at[s], out_hbm.at[idx0_v.at[c0 + s]], wsem[s][0]))
                writes.append(pltpu.async_copy(rows_v.at[s], out_hbm.at[idx1_v.at[c0 + s]], wsem[s][1]))
            for w in writes:
                w.wait()

    return body(rows_in, idx)


def _sc_gather(table, idx):
    n = idx.shape[0]
    width = table.shape[1]
    chunk = SC_ROW_CHUNK
    per_worker = n // SC_WORKERS
    n_pairs = per_worker // (2 * chunk)
    idx3 = idx.reshape(SC_WORKERS, 2 * n_pairs, chunk)

    @functools.partial(
        pl.kernel, mesh=_sc_mesh(),
        out_type=jax.ShapeDtypeStruct((n, width), table.dtype),
        scratch_types=[
            pltpu.VMEM((2 * n_pairs, chunk), jnp.int32),
            pltpu.VMEM((2, chunk, width), table.dtype),
            pltpu.SemaphoreType.DMA, pltpu.SemaphoreType.DMA,
            pltpu.SemaphoreType.DMA, pltpu.SemaphoreType.DMA,
        ],
        name="sc_gather",
    )
    def body(table_hbm, idx_hbm, out_hbm, idx_v, rows_v, gsem0, gsem1, wsem0, wsem1):
        wid = lax.axis_index("s") * SC_CORES + lax.axis_index("c")
        base = wid * per_worker
        gsem = (gsem0, gsem1)
        wsem = (wsem0, wsem1)
        pltpu.sync_copy(idx_hbm.at[wid], idx_v)

        @pl.loop(0, n_pairs)
        def _(p):
            c0 = 2 * p
            gathers = [pltpu.async_copy(table_hbm.at[idx_v.at[c0 + s]], rows_v.at[s], gsem[s]) for s in range(2)]
            writes = []
            for s in range(2):
                gathers[s].wait()
                rows_out = out_hbm.at[pl.ds(base + (c0 + s) * chunk, chunk)]
                writes.append(pltpu.async_copy(rows_v.at[s], rows_out, wsem[s]))
            for s in range(2):
                writes[s].wait()

    return body(table, idx3)


def _moe(x, mods, g4, w_router, w1, w3, w2):
    bsz, length, d = x.shape
    tokens = bsz * length
    tm = EXPERT_ROW_TILE
    xn, meta, meta_t, counts = _router(x, mods, g4, w_router)
    experts = meta_t[0:2].astype(jnp.int32)
    ranks = meta_t[4:6].astype(jnp.int32)
    cnt = counts[0, 0:N_EXPERTS].astype(jnp.int32)
    padded = ((cnt + tm - 1) // tm) * tm
    ends = jnp.cumsum(padded)
    starts = ends - padded
    group_start = jnp.zeros_like(experts)
    for e in range(N_EXPERTS):
        group_start = jnp.where(experts == e, starts[e], group_start)
    pos = (group_start + ranks).reshape(2 * tokens)
    rows = 2 * tokens + N_EXPERTS * tm
    n_tiles = rows // tm
    n_valid = (ends[-1] // tm).astype(jnp.int32).reshape(1)
    tile_start = jnp.arange(n_tiles, dtype=jnp.int32) * tm
    tile_expert = jnp.minimum(
        jnp.sum((tile_start[:, None] >= ends[None, :]).astype(jnp.int32), axis=1), N_EXPERTS - 1)
    own = (tile_expert[:, None] == jnp.arange(N_EXPERTS)[None, :]).astype(jnp.int32)
    tile_tokens = jnp.sum(own * (starts + cnt)[None, :], axis=1) - tile_start
    tile_half = (tile_tokens <= tm // 2).astype(jnp.int32)
    x_sorted = _sc_dispatch(xn.reshape(tokens, d), pos.reshape(2, tokens), rows)
    y_sorted = _experts(x_sorted, tile_expert, n_valid, tile_half, w1, w3, w2)
    y2 = _sc_gather(y_sorted, pos).reshape(2, bsz, length, d)
    return _combine(x, y2, meta, g4, mods)


def _grid_pos_embed(rows, d):
    row = np.repeat(np.arange(rows, dtype=np.float64), GRID_W)
    col = np.tile(np.arange(GRID_W, dtype=np.float64), rows)
    n_freq = d // 4
    omega = POS_THETA ** (-np.arange(n_freq, dtype=np.float64) / n_freq)
    ang_r = row[:, None] * omega
    ang_c = col[:, None] * omega
    return jnp.asarray(np.concatenate([np.sin(ang_r), np.cos(ang_r), np.sin(ang_c), np.cos(ang_c)], axis=-1),
                       dtype=F32)


def kernel(x, c, ctx, c_ctx, ada_w, ada_b, norm_g, ab_w_in, rg_conv_w, rg_conv_b, rg_w_r, rg_b_r, rg_w_i, rg_b_i, rg_lambda, cv_dw_w, cv_dw_b, cv_ln_g, cv_ln_b, ab_w_out, ffn_w1, ffn_w3, ffn_w2, cd_w_in, sg_ln_g, sg_ln_b, sg_w_s, sg_b_s, cd_w_out, moe_router, moe_w1, moe_w3, moe_w2):
    bsz, length, d = x.shape
    assert ada_w.shape[0] == 2, "two layers: one even (RG-LRU | Conformer), one odd (gMLP | Fourier)"
    pos = _grid_pos_embed(length // GRID_W, d)

    cpad = jnp.concatenate([c, c_ctx[None, :], jnp.zeros((16 - bsz - 1, d), F32)], axis=0)
    m = _ada(cpad, ada_w, ada_b)
    mods = m[:, :bsz].reshape(2, bsz, 6, d)
    mods_ctx = m[:, bsz:bsz + 1].reshape(2, 1, 6, d)

    mods_t = mods[0].transpose(1, 0, 2)
    mods_ctx_t = jnp.broadcast_to(mods_ctx[0].reshape(6, 1, d), (6, bsz, d))
    pos3 = pos[:, None, :]
    w_in = ab_w_in[0]
    xa3, gg3, glu3, x0_t, ffn1b, ffn3b, ffn2b = _inproj_t(
        x, pos3, mods_t, norm_g[0], w_in, rg_conv_w[0], rg_conv_b[0], True, riders=(ffn_w1[0], ffn_w3[0], ffn_w2[0]))
    (xc3,) = _inproj_t(ctx, None, mods_ctx_t, norm_g[0], w_in[:, 0:A_WIDTH], rg_conv_w[0], rg_conv_b[0], False)
    wg, bg, sp = _gate_weights(rg_w_r[0], rg_b_r[0], rg_w_i[0], rg_b_i[0], rg_lambda[0])
    n_exp, _, dff_e = moe_w1[0].shape
    rec_hi, rec_lo = _scan_t(xa3, xc3, gg3, wg, bg, sp)
    x1_t, w2b = _conv_out_t(glu3, cv_dw_w[0], cv_dw_b[0], cv_ln_g[0], cv_ln_b[0], rec_lo, rec_hi, x0_t, ab_w_out[0],
                            norm_g[0], mods_t, riders=(moe_w2[0].reshape(n_exp * dff_e, d),))
    x2, w1b, w3b = _ffn_t(x1_t, mods_t, norm_g[0], ffn1b, ffn3b, ffn2b,
                          riders=(moe_w1[0].reshape(n_exp * d, dff_e), moe_w3[0].reshape(n_exp * d, dff_e)))

    u, v, f = _inproj_cd(x2, mods[1], norm_g[1], cd_w_in[0])
    fo = _fourier(f)
    x3 = _gate_out(u, v, fo, x2, sg_ln_g[0], sg_ln_b[0], sg_w_s[0], sg_b_s[0], cd_w_out[0],
                   norm_g[1], mods[1])
    return _moe(x3, mods[1], norm_g[1], moe_router[0], w1b.reshape(n_exp, d, dff_e), w3b.reshape(n_exp, d, dff_e),
                w2b.reshape(n_exp, dff_e, d))
```
